```python
import jax, jax.numpy as jnp
from jax import lax
import numpy as np

D_MODEL = 1024
BATCH = 8
SEQ = 16384
DEPTH = 2

CHUNK = 64
Q_BLOCK = 128
N_MEM = 256
EPS = 1e-6
FOX_HEADS = 8
FOX_HEAD_DIM = 64
FOX_WIDTH = FOX_HEADS * FOX_HEAD_DIM
GDN_HEADS = 4
GDN_HEAD_DIM = 128
GDN_WIDTH = GDN_HEADS * GDN_HEAD_DIM
CONV_K = 4
MEM_HEADS = 4
MEM_HEAD_DIM = 128
MEM_WIDTH = MEM_HEADS * MEM_HEAD_DIM
N_BRANCH = 3
BRANCH_WIDTH = 512

IN_SIZES = (FOX_WIDTH, FOX_WIDTH, FOX_WIDTH, FOX_HEADS, FOX_WIDTH,
            GDN_WIDTH, GDN_WIDTH, GDN_WIDTH, GDN_HEADS, GDN_HEADS, GDN_WIDTH,
            MEM_WIDTH, MEM_WIDTH,
            N_BRANCH * D_MODEL)
N_IN = sum(IN_SIZES)

kernel_name = 'hybrid_fox_gdn_memory_gated_merge'


def _split_cols(z, sizes):
    idx = []
    acc = 0
    for s in sizes[:-1]:
        acc += s
        idx.append(acc)
    return jnp.split(z, idx, axis=-1)


def rms_norm(x, g):
    xf = x.astype(jnp.float32)
    y = xf * lax.rsqrt(jnp.mean(xf * xf, axis=-1, keepdims=True) + EPS)
    return (y * g.astype(jnp.float32)).astype(x.dtype)


def l2_normalize(x):
    return x * lax.rsqrt(jnp.sum(x * x, axis=-1, keepdims=True) + EPS)


def forgetting_attention(q, k, v, f_logit):
    B, S, H, d = q.shape
    log_f = jax.nn.log_sigmoid(f_logit.astype(jnp.float32))
    F = jnp.cumsum(log_f, axis=1).transpose(0, 2, 1)
    nb = S // Q_BLOCK
    pos = jnp.arange(S)
    q_blocks = q.reshape(B, nb, Q_BLOCK, H, d).swapaxes(0, 1)
    F_blocks = F.reshape(B, H, nb, Q_BLOCK).transpose(2, 0, 1, 3)
    p_blocks = pos.reshape(nb, Q_BLOCK)
    scale = d ** -0.5

    def block(args):
        q_blk, F_blk, p_blk = args
        s = jnp.einsum('bqhd,bkhd->bhqk', q_blk, k,
                       preferred_element_type=jnp.float32) * scale
        s = s + F_blk[..., :, None] - F[:, :, None, :]
        s = jnp.where(pos[None, None, None, :] <= p_blk[None, None, :, None], s, -jnp.inf)
        p = jax.nn.softmax(s, axis=-1)
        return jnp.einsum('bhqk,bkhd->bqhd', p.astype(v.dtype), v)

    o = lax.map(block, (q_blocks, F_blocks, p_blocks))
    return o.swapaxes(0, 1).reshape(B, S, H * d)


def causal_dwconv(x, w):
    K = w.shape[0]
    return lax.conv_general_dilated(
        x, w[:, None, :].astype(x.dtype), window_strides=(1,),
        padding=[(K - 1, 0)], dimension_numbers=('NWC', 'WIO', 'NWC'),
        feature_group_count=x.shape[-1])


def gated_delta_rule(q, k, v, g, beta):
    B, S, H, dk = q.shape
    dv = v.shape[-1]
    N = S // CHUNK
    C = CHUNK

    def chunks(t):
        t = t.reshape((B, N, C, H) + t.shape[3:])
        return jnp.moveaxis(t, 3, 1)

    qc, kc, vc = chunks(q), chunks(k), chunks(v)
    gc, bc = chunks(g), chunks(beta)
    G = jnp.cumsum(gc, axis=-1)
    tril_incl = jnp.tril(jnp.ones((C, C), dtype=bool))
    tril_strict = jnp.tril(jnp.ones((C, C), dtype=bool), -1)
    gamma = jnp.exp(jnp.where(tril_incl, G[..., :, None] - G[..., None, :], -jnp.inf))
    kb = kc * bc[..., None]
    A = jnp.where(tril_strict, jnp.einsum('bhncd,bhnsd->bhncs', kb, kc) * gamma, 0.0)
    eye = jnp.eye(C, dtype=A.dtype)
    rhs = jnp.concatenate([vc * bc[..., None], kb * jnp.exp(G)[..., None]], axis=-1)
    sol = lax.linalg.triangular_solve(A + eye, rhs, left_side=True, lower=True,
                                      unit_diagonal=True)
    u, w = sol[..., :dv], sol[..., dv:]
    a_qk = jnp.einsum('bhncd,bhnsd->bhncs', qc, kc) * gamma
    q_dec = qc * jnp.exp(G)[..., None]
    G_last = G[..., -1]
    k_dec = kc * jnp.exp(G_last[..., None] - G)[..., None]

    def step(state, inp):
        dq, kd, uu, ww, aqk, gl = inp
        v_new = uu - jnp.einsum('bhcd,bhde->bhce', ww, state)
        o = (jnp.einsum('bhcd,bhde->bhce', dq, state)
             + jnp.einsum('bhcs,bhse->bhce', aqk, v_new))
        state = state * jnp.exp(gl)[..., None, None] + jnp.einsum('bhcd,bhce->bhde', kd, v_new)
        return state, o

    xs = tuple(jnp.moveaxis(t, 2, 0) for t in (q_dec, k_dec, u, w, a_qk, G_last))
    state0 = jnp.zeros((B, H, dk, dv), jnp.float32)
    _, o = lax.scan(step, state0, xs)
    o = jnp.moveaxis(o, 0, 2)
    return jnp.moveaxis(o, 1, 3).reshape(B, S, H, dv)


def hybrid_layer(x, mem, norm_g, w_in, b_fg, b_merge, conv_w, a_log, dt_bias,
                 gdn_norm_g, mem_norm_g, w_mem_kv, w_branch, w_out):
    B, S, D = x.shape
    dt = x.dtype
    h = rms_norm(x, norm_g)
    z = h @ w_in
    (aq, ak, av, af, az, bq, bk, bv, ba, bb, bz, mq, mz, gates) = _split_cols(z, IN_SIZES)

    o_a = forgetting_attention(aq.reshape(B, S, FOX_HEADS, FOX_HEAD_DIM),
                               ak.reshape(B, S, FOX_HEADS, FOX_HEAD_DIM),
                               av.reshape(B, S, FOX_HEADS, FOX_HEAD_DIM),
                               af + b_fg)
    y_a = (o_a * jax.nn.silu(az)).astype(dt)

    qkv = jax.nn.silu(causal_dwconv(jnp.concatenate([bq, bk, bv], axis=-1), conv_w))
    gq, gk, gv = jnp.split(qkv.astype(jnp.float32), 3, axis=-1)
    gq = l2_normalize(gq.reshape(B, S, GDN_HEADS, GDN_HEAD_DIM)) * (GDN_HEAD_DIM ** -0.5)
    gk = l2_normalize(gk.reshape(B, S, GDN_HEADS, GDN_HEAD_DIM))
    gv = gv.reshape(B, S, GDN_HEADS, GDN_HEAD_DIM)
    g_log = -jnp.exp(a_log.astype(jnp.float32)) * jax.nn.softplus(
        ba.astype(jnp.float32) + dt_bias.astype(jnp.float32))
    beta = jax.nn.sigmoid(bb.astype(jnp.float32))
    o_b = gated_delta_rule(gq, gk, gv, g_log, beta)
    y_b = (rms_norm(o_b, gdn_norm_g).reshape(B, S, GDN_WIDTH) * jax.nn.silu(bz)).astype(dt)

    mem_n = rms_norm(mem, mem_norm_g)
    mk, mv = jnp.split(mem_n @ w_mem_kv, 2, axis=-1)
    M = mem.shape[1]
    mk = mk.reshape(B, M, MEM_HEADS, MEM_HEAD_DIM)
    mv = mv.reshape(B, M, MEM_HEADS, MEM_HEAD_DIM)
    s_m = jnp.einsum('bshd,bmhd->bhsm', mq.reshape(B, S, MEM_HEADS, MEM_HEAD_DIM), mk,
                     preferred_element_type=jnp.float32) * (MEM_HEAD_DIM ** -0.5)
    p_m = jax.nn.softmax(s_m, axis=-1)
    o_m = jnp.einsum('bhsm,bmhd->bshd', p_m.astype(mv.dtype), mv).reshape(B, S, MEM_WIDTH)
    y_m = (o_m * jax.nn.silu(mz)).astype(dt)

    ys = jnp.stack([y_a, y_b, y_m], axis=2)
    proj = jnp.einsum('bsnc,ncd->bsnd', ys, w_branch)
    gate = jax.nn.sigmoid(gates + b_merge).reshape(B, S, N_BRANCH, D)
    merged = jnp.sum(gate * proj, axis=2)
    return x + merged @ w_out


def _fwd_setup_inputs(seed: int = 0) -> dict:
    key = jax.random.key(seed)
    ks = jax.random.split(key, 16)
    f32 = jnp.float32
    x = jax.random.normal(ks[0], (BATCH, SEQ, D_MODEL), f32)
    mem = jax.random.normal(ks[1], (BATCH, N_MEM, D_MODEL), f32)
    norm_g = 1.0 + 0.02 * jax.random.normal(ks[2], (DEPTH, D_MODEL), f32)
    w_in = jax.random.normal(ks[3], (DEPTH, D_MODEL, N_IN), f32) * (D_MODEL ** -0.5)
    b_fg = 1.0 + 3.0 * jax.random.uniform(ks[4], (DEPTH, FOX_HEADS), f32)
    b_merge = 0.02 * jax.random.normal(ks[5], (DEPTH, N_BRANCH * D_MODEL), f32)
    conv_w = jax.random.normal(ks[6], (DEPTH, CONV_K, 3 * GDN_WIDTH), f32) * (CONV_K ** -0.5)
    a_log = jnp.log(jax.random.uniform(ks[7], (DEPTH, GDN_HEADS), f32, 1.0, 16.0))
    dt0 = jnp.exp(jax.random.uniform(ks[8], (DEPTH, GDN_HEADS), f32,
                                     float(np.log(1e-3)), float(np.log(1e-1))))
    dt_bias = dt0 + jnp.log(-jnp.expm1(-dt0))
    gdn_norm_g = 1.0 + 0.02 * jax.random.normal(ks[9], (DEPTH, GDN_HEAD_DIM), f32)
    mem_norm_g = 1.0 + 0.02 * jax.random.normal(ks[10], (DEPTH, D_MODEL), f32)
    w_mem_kv = jax.random.normal(ks[11], (DEPTH, D_MODEL, 2 * MEM_WIDTH), f32) * (D_MODEL ** -0.5)
    w_branch = jax.random.normal(ks[12], (DEPTH, N_BRANCH, BRANCH_WIDTH, D_MODEL), f32) * (BRANCH_WIDTH ** -0.5)
    w_out = jax.random.normal(ks[13], (DEPTH, D_MODEL, D_MODEL), f32) * (0.5 * D_MODEL ** -0.5)
    final_norm_g = 1.0 + 0.02 * jax.random.normal(ks[14], (D_MODEL,), f32)
    return {'x': x, 'mem': mem, 'norm_g': norm_g, 'w_in': w_in, 'b_fg': b_fg,
            'b_merge': b_merge, 'conv_w': conv_w, 'a_log': a_log, 'dt_bias': dt_bias,
            'gdn_norm_g': gdn_norm_g, 'mem_norm_g': mem_norm_g, 'w_mem_kv': w_mem_kv,
            'w_branch': w_branch, 'w_out': w_out, 'final_norm_g': final_norm_g}


def _fwd_reference(x, mem, norm_g, w_in, b_fg, b_merge, conv_w, a_log, dt_bias,
              gdn_norm_g, mem_norm_g, w_mem_kv, w_branch, w_out, final_norm_g):
    for l in range(DEPTH):
        x = hybrid_layer(x, mem, norm_g[l], w_in[l], b_fg[l], b_merge[l], conv_w[l],
                         a_log[l], dt_bias[l], gdn_norm_g[l], mem_norm_g[l],
                         w_mem_kv[l], w_branch[l], w_out[l])
    return rms_norm(x, final_norm_g)


import jax as _jax
import jax.numpy as _jnp

TWIN_FORMAT = 'train_step'
FWD_PARAMS = ['x', 'mem', 'norm_g', 'w_in', 'b_fg', 'b_merge', 'conv_w', 'a_log', 'dt_bias', 'gdn_norm_g', 'mem_norm_g', 'w_mem_kv', 'w_branch', 'w_out', 'final_norm_g']
TWIN_WEIGHTS = ['norm_g', 'w_in', 'b_fg', 'b_merge', 'conv_w', 'a_log', 'dt_bias', 'gdn_norm_g', 'mem_norm_g', 'w_mem_kv', 'w_branch', 'w_out', 'final_norm_g']
TWIN_DIFF_INPUT = 'x'
TWIN_INPUTS = ['x', 'mem', 'norm_g', 'w_in', 'b_fg', 'b_merge', 'conv_w', 'a_log', 'dt_bias', 'gdn_norm_g', 'mem_norm_g', 'w_mem_kv', 'w_branch', 'w_out', 'final_norm_g', 'loss_target', 'm_norm_g', 'm_w_in', 'm_b_fg', 'm_b_merge', 'm_conv_w', 'm_a_log', 'm_dt_bias', 'm_gdn_norm_g', 'm_mem_norm_g', 'm_w_mem_kv', 'm_w_branch', 'm_w_out', 'm_final_norm_g', 'v_norm_g', 'v_w_in', 'v_b_fg', 'v_b_merge', 'v_conv_w', 'v_a_log', 'v_dt_bias', 'v_gdn_norm_g', 'v_mem_norm_g', 'v_w_mem_kv', 'v_w_branch', 'v_w_out', 'v_final_norm_g']
TWIN_OUTPUTS = ['loss', 'grad_x', 'grad_norm_g', 'grad_w_in', 'grad_b_fg', 'grad_b_merge', 'grad_conv_w', 'grad_a_log', 'grad_dt_bias', 'grad_gdn_norm_g', 'grad_mem_norm_g', 'grad_w_mem_kv', 'grad_w_branch', 'grad_w_out', 'grad_final_norm_g', 'delta_norm_g', 'delta_w_in', 'delta_b_fg', 'delta_b_merge', 'delta_conv_w', 'delta_a_log', 'delta_dt_bias', 'delta_gdn_norm_g', 'delta_mem_norm_g', 'delta_w_mem_kv', 'delta_w_branch', 'delta_w_out', 'delta_final_norm_g', 'new_m_norm_g', 'new_m_w_in', 'new_m_b_fg', 'new_m_b_merge', 'new_m_conv_w', 'new_m_a_log', 'new_m_dt_bias', 'new_m_gdn_norm_g', 'new_m_mem_norm_g', 'new_m_w_mem_kv', 'new_m_w_branch', 'new_m_w_out', 'new_m_final_norm_g', 'new_v_norm_g', 'new_v_w_in', 'new_v_b_fg', 'new_v_b_merge', 'new_v_conv_w', 'new_v_a_log', 'new_v_dt_bias', 'new_v_gdn_norm_g', 'new_v_mem_norm_g', 'new_v_w_mem_kv', 'new_v_w_branch', 'new_v_w_out', 'new_v_final_norm_g']
TWIN_LEAF_KINDS = {'loss': 'loss', 'grad_x': 'grad_x', 'grad_norm_g': 'grad_w', 'grad_w_in': 'grad_w', 'grad_b_fg': 'grad_w', 'grad_b_merge': 'grad_w', 'grad_conv_w': 'grad_w', 'grad_a_log': 'grad_w', 'grad_dt_bias': 'grad_w', 'grad_gdn_norm_g': 'grad_w', 'grad_mem_norm_g': 'grad_w', 'grad_w_mem_kv': 'grad_w', 'grad_w_branch': 'grad_w', 'grad_w_out': 'grad_w', 'grad_final_norm_g': 'grad_w', 'delta_norm_g': 'delta_w', 'delta_w_in': 'delta_w', 'delta_b_fg': 'delta_w', 'delta_b_merge': 'delta_w', 'delta_conv_w': 'delta_w', 'delta_a_log': 'delta_w', 'delta_dt_bias': 'delta_w', 'delta_gdn_norm_g': 'delta_w', 'delta_mem_norm_g': 'delta_w', 'delta_w_mem_kv': 'delta_w', 'delta_w_branch': 'delta_w', 'delta_w_out': 'delta_w', 'delta_final_norm_g': 'delta_w', 'new_m_norm_g': 'new_m', 'new_m_w_in': 'new_m', 'new_m_b_fg': 'new_m', 'new_m_b_merge': 'new_m', 'new_m_conv_w': 'new_m', 'new_m_a_log': 'new_m', 'new_m_dt_bias': 'new_m', 'new_m_gdn_norm_g': 'new_m', 'new_m_mem_norm_g': 'new_m', 'new_m_w_mem_kv': 'new_m', 'new_m_w_branch': 'new_m', 'new_m_w_out': 'new_m', 'new_m_final_norm_g': 'new_m', 'new_v_norm_g': 'new_v', 'new_v_w_in': 'new_v', 'new_v_b_fg': 'new_v', 'new_v_b_merge': 'new_v', 'new_v_conv_w': 'new_v', 'new_v_a_log': 'new_v', 'new_v_dt_bias': 'new_v', 'new_v_gdn_norm_g': 'new_v', 'new_v_mem_norm_g': 'new_v', 'new_v_w_mem_kv': 'new_v', 'new_v_w_branch': 'new_v', 'new_v_w_out': 'new_v', 'new_v_final_norm_g': 'new_v'}


def _forward(args):
    return _fwd_reference(*[args[k] for k in FWD_PARAMS])


def _output_shape():
    def fwd():
        inp = _fwd_setup_inputs(0)
        return _fwd_reference(*[inp[k] for k in FWD_PARAMS])
    out = _jax.eval_shape(fwd)
    return out.shape, out.dtype

N_MICROBATCH = 1
ADAM_LR = 0.001
ADAM_B1 = 0.9
ADAM_B2 = 0.999
ADAM_EPS = 1e-08
ADAM_WD = 0.01
ADAM_STEP = 10
PER_EXAMPLE_BATCH_AXIS = {'x': 0, 'mem': 0, 'loss_target': 0}
SHARED_INPUTS = []
_WEIGHT_DTYPES = {'norm_g': _jnp.float32, 'w_in': _jnp.float32, 'b_fg': _jnp.float32, 'b_merge': _jnp.float32, 'conv_w': _jnp.float32, 'a_log': _jnp.float32, 'dt_bias': _jnp.float32, 'gdn_norm_g': _jnp.float32, 'mem_norm_g': _jnp.float32, 'w_mem_kv': _jnp.float32, 'w_branch': _jnp.float32, 'w_out': _jnp.float32, 'final_norm_g': _jnp.float32}
MOMENT_SCALE = {'norm_g': 1.046490e-01, 'w_in': 3.729752e-02, 'b_fg': 1.305490e-01, 'b_merge': 1.419289e-02, 'conv_w': 6.169137e-02, 'a_log': 2.744181e-01, 'dt_bias': 2.697672e-01, 'gdn_norm_g': 1.618305e-01, 'mem_norm_g': 9.969250e-03, 'w_mem_kv': 8.509514e-03, 'w_branch': 3.530436e-02, 'w_out': 1.231382e-01, 'final_norm_g': 1.279684e+02}


def _to_microbatches(a, axis):
    t = _jnp.moveaxis(a, axis, 0)
    t = t.reshape((N_MICROBATCH, t.shape[0] // N_MICROBATCH) + t.shape[1:])
    return _jnp.moveaxis(t, 1, axis + 1)


def setup_inputs(seed: int = 0) -> dict:
    inp = _fwd_setup_inputs(seed)
    key = _jax.random.fold_in(_jax.random.key(seed), 7919)
    shape, _ = _output_shape()
    out = dict(inp)
    out["loss_target"] = _jax.random.normal(_jax.random.fold_in(key, 0), shape, _jnp.float32)
    for i, name in enumerate(TWIN_WEIGHTS):
        w = inp[name].astype(_jnp.float32)
        if MOMENT_SCALE is None:
            s = _jnp.sqrt(_jnp.mean(_jnp.square(w)) + 1e-30)
        else:
            s = MOMENT_SCALE[name]
        km, kv = _jax.random.split(_jax.random.fold_in(key, i + 1))
        out[name] = w
        out["m_" + name] = s * _jax.random.normal(km, w.shape, _jnp.float32)
        out["v_" + name] = (s * s) * _jax.random.uniform(kv, w.shape, _jnp.float32, 0.5, 1.5)
    if N_MICROBATCH > 1:
        for name, axis in PER_EXAMPLE_BATCH_AXIS.items():
            out[name] = _to_microbatches(out[name], axis)
    return {'x': out['x'], 'mem': out['mem'], 'norm_g': out['norm_g'], 'w_in': out['w_in'], 'b_fg': out['b_fg'], 'b_merge': out['b_merge'], 'conv_w': out['conv_w'], 'a_log': out['a_log'], 'dt_bias': out['dt_bias'], 'gdn_norm_g': out['gdn_norm_g'], 'mem_norm_g': out['mem_norm_g'], 'w_mem_kv': out['w_mem_kv'], 'w_branch': out['w_branch'], 'w_out': out['w_out'], 'final_norm_g': out['final_norm_g'], 'loss_target': out['loss_target'], 'm_norm_g': out['m_norm_g'], 'm_w_in': out['m_w_in'], 'm_b_fg': out['m_b_fg'], 'm_b_merge': out['m_b_merge'], 'm_conv_w': out['m_conv_w'], 'm_a_log': out['m_a_log'], 'm_dt_bias': out['m_dt_bias'], 'm_gdn_norm_g': out['m_gdn_norm_g'], 'm_mem_norm_g': out['m_mem_norm_g'], 'm_w_mem_kv': out['m_w_mem_kv'], 'm_w_branch': out['m_w_branch'], 'm_w_out': out['m_w_out'], 'm_final_norm_g': out['m_final_norm_g'], 'v_norm_g': out['v_norm_g'], 'v_w_in': out['v_w_in'], 'v_b_fg': out['v_b_fg'], 'v_b_merge': out['v_b_merge'], 'v_conv_w': out['v_conv_w'], 'v_a_log': out['v_a_log'], 'v_dt_bias': out['v_dt_bias'], 'v_gdn_norm_g': out['v_gdn_norm_g'], 'v_mem_norm_g': out['v_mem_norm_g'], 'v_w_mem_kv': out['v_w_mem_kv'], 'v_w_branch': out['v_w_branch'], 'v_w_out': out['v_w_out'], 'v_final_norm_g': out['v_final_norm_g']}


def _loss(weights, diff, rest, loss_target):
    with _jax.named_scope("forward"):
        args = {**rest, TWIN_DIFF_INPUT: diff, **{k: w.astype(_WEIGHT_DTYPES[k]) for k, w in weights.items()}}
        y = _forward(args)
    with _jax.named_scope("loss_head"):
        err = _jnp.square(y.astype(_jnp.float32) - loss_target)
        return 0.5 * _jnp.sum(_jnp.mean(err, axis=-1)) if err.ndim else 0.5 * err


def _adamw(w, g, m, v):
    m = ADAM_B1 * m + (1.0 - ADAM_B1) * g
    v = ADAM_B2 * v + (1.0 - ADAM_B2) * _jnp.square(g)
    m_hat = m / (1.0 - ADAM_B1 ** ADAM_STEP)
    v_hat = v / (1.0 - ADAM_B2 ** ADAM_STEP)
    delta = -ADAM_LR * (m_hat / (_jnp.sqrt(v_hat) + ADAM_EPS) + ADAM_WD * w)
    return delta, m, v


def reference(x, mem, norm_g, w_in, b_fg, b_merge, conv_w, a_log, dt_bias, gdn_norm_g, mem_norm_g, w_mem_kv, w_branch, w_out, final_norm_g, loss_target, m_norm_g, m_w_in, m_b_fg, m_b_merge, m_conv_w, m_a_log, m_dt_bias, m_gdn_norm_g, m_mem_norm_g, m_w_mem_kv, m_w_branch, m_w_out, m_final_norm_g, v_norm_g, v_w_in, v_b_fg, v_b_merge, v_conv_w, v_a_log, v_dt_bias, v_gdn_norm_g, v_mem_norm_g, v_w_mem_kv, v_w_branch, v_w_out, v_final_norm_g):
    given = dict(x=x, mem=mem, norm_g=norm_g, w_in=w_in, b_fg=b_fg, b_merge=b_merge, conv_w=conv_w, a_log=a_log, dt_bias=dt_bias, gdn_norm_g=gdn_norm_g, mem_norm_g=mem_norm_g, w_mem_kv=w_mem_kv, w_branch=w_branch, w_out=w_out, final_norm_g=final_norm_g, loss_target=loss_target, m_norm_g=m_norm_g, m_w_in=m_w_in, m_b_fg=m_b_fg, m_b_merge=m_b_merge, m_conv_w=m_conv_w, m_a_log=m_a_log, m_dt_bias=m_dt_bias, m_gdn_norm_g=m_gdn_norm_g, m_mem_norm_g=m_mem_norm_g, m_w_mem_kv=m_w_mem_kv, m_w_branch=m_w_branch, m_w_out=m_w_out, m_final_norm_g=m_final_norm_g, v_norm_g=v_norm_g, v_w_in=v_w_in, v_b_fg=v_b_fg, v_b_merge=v_b_merge, v_conv_w=v_conv_w, v_a_log=v_a_log, v_dt_bias=v_dt_bias, v_gdn_norm_g=v_gdn_norm_g, v_mem_norm_g=v_mem_norm_g, v_w_mem_kv=v_w_mem_kv, v_w_branch=v_w_branch, v_w_out=v_w_out, v_final_norm_g=v_final_norm_g)
    weights = {n: given[n] for n in TWIN_WEIGHTS}
    shared = {n: given[n] for n in SHARED_INPUTS}
    per_example = {n: given[n] for n in ['x', 'mem']}
    grad_fn = _jax.value_and_grad(_loss, argnums=(0, 1))

    def one_microbatch(ex, loss_target):
        ex = dict(ex)
        diff = ex.pop(TWIN_DIFF_INPUT)
        return grad_fn(weights, diff, {**shared, **ex}, loss_target)

    if N_MICROBATCH == 1:
        loss, (grad_w, grad_x) = one_microbatch(per_example, given["loss_target"])
    else:
        def body(carry, xs):
            loss_sum, grad_sum = carry
            l_k, (gw_k, gx_k) = one_microbatch(xs[0], xs[1])
            with _jax.named_scope("update"):
                return (loss_sum + l_k, _jax.tree.map(_jnp.add, grad_sum, gw_k)), gx_k

        init = (_jnp.zeros((), _jnp.float32), _jax.tree.map(_jnp.zeros_like, weights))
        (loss, grad_w), grad_x = _jax.lax.scan(body, init, (per_example, given["loss_target"]))
    with _jax.named_scope("update"):
        delta_w, new_m, new_v = {}, {}, {}
        for n in TWIN_WEIGHTS:
            delta_w[n], new_m[n], new_v[n] = _adamw(weights[n], grad_w[n], given["m_" + n], given["v_" + n])
    return (loss, grad_x, *[grad_w[n] for n in TWIN_WEIGHTS], *[delta_w[n] for n in TWIN_WEIGHTS],
            *[new_m[n] for n in TWIN_WEIGHTS], *[new_v[n] for n in TWIN_WEIGHTS])
```

```python
import functools

import jax
import jax.numpy as jnp
from jax import lax
from jax.experimental import pallas as pl
from jax.experimental.pallas import tpu as pltpu

f32, bf16 = jnp.float32, jnp.bfloat16
HI = lax.Precision.HIGHEST

D = 1024
EPS = 1e-6
CH = 64
N_DEV = 8
DEPTH = 2
FOX_SCALE = 64 ** -0.5
GDN_SCALE = 128 ** -0.5
MEM_SCALE = 128 ** -0.5
NEG = -1e30
VMEM_LIMIT = 56 * 1024 * 1024

ADAM_LR, ADAM_B1, ADAM_B2, ADAM_EPS, ADAM_WD, ADAM_STEP = 0.001, 0.9, 0.999, 1e-08, 0.01, 10

_COLS = dict(aq=(0, 512), ak=(512, 1024), av=(1024, 1536), af=(1536, 1544), az=(1544, 2056),
             bq=(2056, 2568), bk=(2568, 3080), bv=(3080, 3592), ba=(3592, 3596), bb=(3596, 3600),
             bz=(3600, 4112), mq=(4112, 4624), mz=(4624, 5136), gates=(5136, 8208))
_ORDER = ("aq", "ak", "av", "mq", "bq", "bk", "bv", "az", "bz", "mz", "gates", "af", "ba", "bb")
N_IN = 8208
NB, NF, NS = 2048, 6144, 128
N_ALL = NB + NF + NS

_SHARD_SHAPES = (("w_in", (DEPTH, D, 1026)), ("w_mem_kv", (DEPTH, 128, D)),
                 ("w_branch", (DEPTH, 3, 512, 128)), ("w_out", (DEPTH, 128, D)), ("conv_w", (DEPTH, 4, 192)))
FLAT_C = 1024
FLAT_R = 2960


def _cp(sem=None, vmem=None):
    kw = {}
    if sem is not None:
        kw["dimension_semantics"] = sem
    if vmem is not None:
        kw["vmem_limit_bytes"] = vmem
    return pltpu.CompilerParams(**kw)


def _dot(a, b):
    return jnp.dot(a, b, preferred_element_type=f32)


def _dot_nt(a, b):
    return lax.dot_general(a, b, (((1,), (1,)), ((), ())), preferred_element_type=f32)


def _dot_tn(a, b):
    return lax.dot_general(a, b, (((0,), (0,)), ((), ())), preferred_element_type=f32)


def _hi(a, b):
    return jnp.dot(a, b, preferred_element_type=f32, precision=HI)


def _hi_nt(a, b):
    return lax.dot_general(a, b, (((1,), (1,)), ((), ())), preferred_element_type=f32, precision=HI)


def _hi_tn(a, b):
    return lax.dot_general(a, b, (((0,), (0,)), ((), ())), preferred_element_type=f32, precision=HI)


def _b(x):
    return x.astype(bf16)


def _sig(x):
    return jax.nn.sigmoid(x)


def _silu(x):
    return x * _sig(x)


def _dsilu(x):
    s = _sig(x)
    return s * (1.0 + x * (1.0 - s))


def _softplus(x):
    return jnp.maximum(x, 0.0) + jnp.log1p(jnp.exp(-jnp.abs(x)))


def _rowsum(x):
    return jnp.sum(x, axis=1, keepdims=True)


def _colsum(x):
    return jnp.sum(x, axis=0, keepdims=True)


def _norm_fwd(x, g, name):
    M = x.shape[0]
    ts = min(M, 512)

    def body(x_ref, g_ref, h_ref):
        xv = x_ref[...]
        r = lax.rsqrt(jnp.mean(xv * xv, axis=-1, keepdims=True) + EPS)
        h_ref[...] = _b(xv * r * g_ref[...])

    return pl.pallas_call(
        body, grid=(M // ts,), out_shape=jax.ShapeDtypeStruct((M, D), bf16),
        in_specs=[pl.BlockSpec((ts, D), lambda i: (i, 0)), pl.BlockSpec((1, D), lambda i: (0, 0))],
        out_specs=pl.BlockSpec((ts, D), lambda i: (i, 0)), compiler_params=_cp(("parallel",)), name=name)(x, g)


def _norm_bwd(x, g, dh, dres, name):
    M = x.shape[0]
    ts = min(M, 512)
    with_dx = dres is not None

    def body(*refs):
        if with_dx:
            x_ref, g_ref, dh_ref, dres_ref, dx_ref, dg_ref = refs
        else:
            x_ref, g_ref, dh_ref, dg_ref = refs
        i = pl.program_id(0)
        xv = x_ref[...]
        r = lax.rsqrt(jnp.mean(xv * xv, axis=-1, keepdims=True) + EPS)
        xh = xv * r
        dh = dh_ref[...].astype(f32)
        part = jnp.broadcast_to(_colsum(dh * xh), (8, D))

        @pl.when(i == 0)
        def _():
            dg_ref[...] = part

        @pl.when(i > 0)
        def _():
            dg_ref[...] += part

        if with_dx:
            dxh = dh * g_ref[...]
            dx_ref[...] = dres_ref[...] + r * (dxh - xh * jnp.mean(dxh * xh, axis=-1, keepdims=True))

    tile = pl.BlockSpec((ts, D), lambda i: (i, 0))
    gspec = pl.BlockSpec((1, D), lambda i: (0, 0))
    acc = pl.BlockSpec((8, D), lambda i: (0, 0))
    if with_dx:
        return pl.pallas_call(
            body, grid=(M // ts,), out_shape=(jax.ShapeDtypeStruct((M, D), f32), jax.ShapeDtypeStruct((8, D), f32)),
            in_specs=[tile, gspec, tile, tile], out_specs=(tile, acc), compiler_params=_cp(("arbitrary",)), name=name)(x, g, dh, dres)
    return pl.pallas_call(
        body, grid=(M // ts,), out_shape=jax.ShapeDtypeStruct((8, D), f32),
        in_specs=[tile, gspec, tile], out_specs=acc, compiler_params=_cp(("arbitrary",)), name=name)(x, g, dh)


def _loss_head(x, g, tgt, name):
    M = x.shape[0]
    ts = min(M, 512)

    def body(x_ref, g_ref, t_ref, dx_ref, dg_ref, ls_ref):
        i = pl.program_id(0)
        xv = x_ref[...]
        gv = g_ref[...]
        r = lax.rsqrt(jnp.mean(xv * xv, axis=-1, keepdims=True) + EPS)
        xh = xv * r
        e = xh * gv - t_ref[...]
        lpart = 0.5 * jnp.sum(jnp.mean(e * e, axis=-1, keepdims=True), axis=0, keepdims=True)
        dy = e * (1.0 / D)
        dgp = jnp.broadcast_to(_colsum(dy * xh), (8, D))
        lp = jnp.broadcast_to(lpart, (8, 128))

        @pl.when(i == 0)
        def _():
            dg_ref[...] = dgp
            ls_ref[...] = lp

        @pl.when(i > 0)
        def _():
            dg_ref[...] += dgp
            ls_ref[...] += lp

        dxh = dy * gv
        dx_ref[...] = r * (dxh - xh * jnp.mean(dxh * xh, axis=-1, keepdims=True))

    tile = pl.BlockSpec((ts, D), lambda i: (i, 0))
    return pl.pallas_call(
        body, grid=(M // ts,),
        out_shape=(jax.ShapeDtypeStruct((M, D), f32), jax.ShapeDtypeStruct((8, D), f32), jax.ShapeDtypeStruct((8, 128), f32)),
        in_specs=[tile, pl.BlockSpec((1, D), lambda i: (0, 0)), tile],
        out_specs=(tile, pl.BlockSpec((8, D), lambda i: (0, 0)), pl.BlockSpec((8, 128), lambda i: (0, 0))),
        compiler_params=_cp(("arbitrary",)), name=name)(x, g, tgt)


def _mm(a, b, out_dtype, tm, tn, tk, name, trans_a=False):
    if trans_a:
        K, M = a.shape
    else:
        M, K = a.shape
    N = b.shape[1]
    tm, tn, tk = min(tm, M), min(tn, N), min(tk, K)
    nk = K // tk

    def body(a_ref, b_ref, o_ref, acc_ref):
        k = pl.program_id(2)
        av, bv = _b(a_ref[...]), _b(b_ref[...])
        part = _dot_tn(av, bv) if trans_a else _dot(av, bv)
        if nk == 1:
            o_ref[...] = part.astype(out_dtype)
        else:
            @pl.when(k == 0)
            def _():
                acc_ref[...] = part

            @pl.when(k > 0)
            def _():
                acc_ref[...] += part

            @pl.when(k == nk - 1)
            def _():
                o_ref[...] = acc_ref[...].astype(out_dtype)

    a_spec = pl.BlockSpec((tk, tm), lambda i, j, k: (k, i)) if trans_a else pl.BlockSpec((tm, tk), lambda i, j, k: (i, k))
    return pl.pallas_call(
        body, grid=(M // tm, N // tn, nk), out_shape=jax.ShapeDtypeStruct((M, N), out_dtype),
        in_specs=[a_spec, pl.BlockSpec((tk, tn), lambda i, j, k: (k, j))],
        out_specs=pl.BlockSpec((tm, tn), lambda i, j, k: (i, j)),
        scratch_shapes=[pltpu.VMEM((tm, tn), f32)],
        compiler_params=_cp(("parallel", "parallel", "arbitrary"), VMEM_LIMIT), name=name)(a, b)


def _small_pars(b_fg, a_log, dt_bias):
    par = jnp.zeros((8, 128), f32)
    par = par.at[0, 0:8].set(b_fg).at[1, 8:12].set(a_log).at[2, 8:12].set(dt_bias)
    return par


def _small_prep(zs, par, name):
    S = zs.shape[0]
    ts = min(S, 512)

    def body(z_ref, par_ref, o_ref, carry_ref):
        i = pl.program_id(0)

        @pl.when(i == 0)
        def _():
            carry_ref[...] = jnp.zeros_like(carry_ref)

        z = z_ref[...]
        lane = lax.broadcasted_iota(jnp.int32, (ts, 128), 1)
        row = lax.broadcasted_iota(jnp.int32, (ts, 128), 0)
        za = z + par_ref[0:1, :]
        logf = jnp.minimum(za, 0.0) - jnp.log1p(jnp.exp(-jnp.abs(za)))
        glog = -jnp.exp(par_ref[1:2, :]) * _softplus(z + par_ref[2:3, :])
        x = jnp.where(lane < 8, logf, jnp.where(lane < 12, glog, 0.0))
        pos = jnp.where(lane < 8, row, row & (CH - 1))
        s = 1
        while s < ts:
            x = x + jnp.where(pos >= s, pltpu.roll(x, s, 0), 0.0)
            s *= 2
        tot = x + carry_ref[0:1, :]
        carry_ref[...] = jnp.broadcast_to(jnp.where(lane[0:1] < 8, tot[ts - 1:ts, :], 0.0), (8, 128))
        o_ref[...] = jnp.where(lane < 8, tot, jnp.where(lane < 12, x, jnp.where(lane < 16, _sig(z), 0.0)))

    return pl.pallas_call(
        body, grid=(S // ts,), out_shape=jax.ShapeDtypeStruct((S, 128), f32),
        in_specs=[pl.BlockSpec((ts, 128), lambda i: (i, 0)), pl.BlockSpec((8, 128), lambda i: (0, 0))],
        out_specs=pl.BlockSpec((ts, 128), lambda i: (i, 0)), scratch_shapes=[pltpu.VMEM((8, 128), f32)],
        compiler_params=_cp(("arbitrary",)), name=name)(zs, par)


def _small_bwd(zs, par, dfr, dfc, dsm, name):
    S = zs.shape[0]
    ts = min(S, 512)
    nt = S // ts

    def body(z_ref, par_ref, dfr_ref, dfc_ref, dsm_ref, dz_ref, acc_ref, carry_ref):
        i = pl.program_id(0)

        @pl.when(i == 0)
        def _():
            carry_ref[...] = jnp.zeros_like(carry_ref)

        z = z_ref[...]
        dsm_v = dsm_ref[...]
        lane = lax.broadcasted_iota(jnp.int32, (ts, 128), 1)
        row = lax.broadcasted_iota(jnp.int32, (ts, 128), 0)
        x = jnp.where(lane < 8, dfr_ref[...] - dfc_ref[...], jnp.where(lane < 12, dsm_v, 0.0))
        pos = jnp.where(lane < 8, row, row & (CH - 1))
        seg = jnp.where(lane < 8, ts, CH)
        s = 1
        while s < ts:
            x = x + jnp.where(pos + s < seg, pltpu.roll(x, ts - s, 0), 0.0)
            s *= 2
        tot = x + carry_ref[0:1, :]
        carry_ref[...] = jnp.broadcast_to(jnp.where(lane[0:1] < 8, tot[0:1, :], 0.0), (8, 128))
        za = z + par_ref[0:1, :]
        daf = tot * _sig(-za)
        zb = z + par_ref[2:3, :]
        nea = -jnp.exp(par_ref[1:2, :])
        glog = nea * _softplus(zb)
        dba = x * nea * _sig(zb)
        beta = _sig(z)
        dbb = dsm_v * beta * (1.0 - beta)
        dz_ref[...] = _b(jnp.where(lane < 8, daf, jnp.where(lane < 12, dba, jnp.where(lane < 16, dbb, 0.0))))
        r0 = _colsum(jnp.where(lane < 8, daf, 0.0))
        r1 = _colsum(jnp.where((lane >= 8) & (lane < 12), x * glog, 0.0))
        r2 = _colsum(jnp.where((lane >= 8) & (lane < 12), dba, 0.0))
        r8 = lax.broadcasted_iota(jnp.int32, (8, 128), 0)
        part = jnp.where(r8 == 0, r0, jnp.where(r8 == 1, r1, jnp.where(r8 == 2, r2, 0.0)))

        @pl.when(i == 0)
        def _():
            acc_ref[...] = part

        @pl.when(i > 0)
        def _():
            acc_ref[...] += part

    rev = pl.BlockSpec((ts, 128), lambda i: (nt - 1 - i, 0))
    c8 = pl.BlockSpec((8, 128), lambda i: (0, 0))
    return pl.pallas_call(
        body, grid=(nt,), out_shape=(jax.ShapeDtypeStruct((S, 128), bf16), jax.ShapeDtypeStruct((8, 128), f32)),
        in_specs=[rev, c8, rev, rev, rev], out_specs=(rev, c8), scratch_shapes=[pltpu.VMEM((8, 128), f32)],
        compiler_params=_cp(("arbitrary",)), name=name)(zs, par, dfr, dfc, dsm)


def _head_col(fc, p, h):
    lane = lax.broadcasted_iota(jnp.int32, fc.shape, 1)
    return _rowsum(jnp.where(lane == 2 * p + h, fc, 0.0))


def _fox_fwd(zb, sm, ft, name):
    S = zb.shape[0]
    T = min(S, 512)

    def body(q_ref, k_ref, v_ref, fc_ref, ft_ref, o_ref, lse_ref, m_ref, l_ref, acc_ref):
        p, i = pl.program_id(0), pl.program_id(1)
        lane1 = lax.broadcasted_iota(jnp.int32, (1, 128), 1)
        hm = (lane1 < 64, lane1 >= 64)
        q = q_ref[...]
        qs = [jnp.where(hm[h], q, jnp.zeros_like(q)) * FOX_SCALE for h in (0, 1)]
        fc = fc_ref[...]
        fcol = [_head_col(fc, p, h) for h in (0, 1)]
        m_ref[...] = jnp.full_like(m_ref, NEG)
        l_ref[...] = jnp.zeros_like(l_ref)
        acc_ref[...] = jnp.zeros_like(acc_ref)
        row = lax.broadcasted_iota(jnp.int32, (T, T), 0)
        col = lax.broadcasted_iota(jnp.int32, (T, T), 1)

        def tile(j, masked):
            off = pl.multiple_of(j * T, T)
            ks, vs = k_ref[pl.ds(off, T), :], v_ref[pl.ds(off, T), :]
            for h in (0, 1):
                fr = ft_ref[pl.ds(2 * p + h, 1), pl.ds(off, T)]
                s = _dot_nt(qs[h], ks) + (fcol[h] - fr)
                if masked:
                    s = jnp.where(row >= col, s, NEG)
                m_old = m_ref[h]
                m_new = jnp.maximum(m_old, jnp.max(s, axis=1, keepdims=True))
                alpha = jnp.exp(m_old - m_new)
                pr = jnp.exp(s - m_new)
                l_ref[h] = alpha * l_ref[h] + _rowsum(pr)
                acc_ref[h] = alpha * acc_ref[h] + _dot(_b(pr), vs)
                m_ref[h] = m_new

        def step(j, c):
            tile(j, False)
            return c

        lax.fori_loop(0, i, step, 0)
        tile(i, True)
        lane2 = lax.broadcasted_iota(jnp.int32, (T, 128), 1)
        o_ref[...] = jnp.where(lane2 < 64, acc_ref[0] / l_ref[0], acc_ref[1] / l_ref[1])
        lse_ref[0] = jnp.where(lane2 < 64, m_ref[0] + jnp.log(l_ref[0]), m_ref[1] + jnp.log(l_ref[1]))

    return pl.pallas_call(
        body, grid=(4, S // T),
        out_shape=(jax.ShapeDtypeStruct((S, 512), f32), jax.ShapeDtypeStruct((4, S, 128), f32)),
        in_specs=[pl.BlockSpec((T, 128), lambda p, i: (i, p)), pl.BlockSpec((S, 128), lambda p, i: (0, 4 + p)),
                  pl.BlockSpec((S, 128), lambda p, i: (0, 8 + p)), pl.BlockSpec((T, 128), lambda p, i: (i, 0)),
                  pl.BlockSpec((8, S), lambda p, i: (0, 0))],
        out_specs=(pl.BlockSpec((T, 128), lambda p, i: (i, p)), pl.BlockSpec((1, T, 128), lambda p, i: (p, i, 0))),
        scratch_shapes=[pltpu.VMEM((2, T, 1), f32), pltpu.VMEM((2, T, 1), f32), pltpu.VMEM((2, T, 128), f32)],
        compiler_params=_cp(("arbitrary", "arbitrary"), VMEM_LIMIT), name=name)(zb, zb, zb, sm, ft)


def _fox_bwd_dq(zb, sm, ft, lse, o, do, name):
    S = zb.shape[0]
    T = min(S, 512)

    def body(q_ref, k_ref, v_ref, fc_ref, ft_ref, lse_ref, o_ref, do_ref, dq_ref, dl_ref, dfr_ref, acc_ref, fr_ref):
        p, i = pl.program_id(0), pl.program_id(1)
        lane1 = lax.broadcasted_iota(jnp.int32, (1, 128), 1)
        lane2 = lax.broadcasted_iota(jnp.int32, (T, 128), 1)
        hm = (lane1 < 64, lane1 >= 64)
        q = q_ref[...]
        qs = [jnp.where(hm[h], q, jnp.zeros_like(q)) * FOX_SCALE for h in (0, 1)]
        fc = fc_ref[...]
        fcol = [_head_col(fc, p, h) for h in (0, 1)]
        lse_v = lse_ref[0]
        lcol = [lse_v[:, 0:1], lse_v[:, 64:65]]
        do = do_ref[...]
        prod = do * o_ref[...]
        dcol = [_rowsum(jnp.where(lane2 < 64, prod, 0.0)), _rowsum(jnp.where(lane2 >= 64, prod, 0.0))]
        dob = _b(do)
        dos = [jnp.where(hm[h], dob, jnp.zeros_like(dob)) for h in (0, 1)]
        acc_ref[...] = jnp.zeros_like(acc_ref)
        fr_ref[...] = jnp.zeros_like(fr_ref)
        row = lax.broadcasted_iota(jnp.int32, (T, T), 0)
        col = lax.broadcasted_iota(jnp.int32, (T, T), 1)

        def tile(j, masked):
            off = pl.multiple_of(j * T, T)
            ks, vs = k_ref[pl.ds(off, T), :], v_ref[pl.ds(off, T), :]
            for h in (0, 1):
                fr = ft_ref[pl.ds(2 * p + h, 1), pl.ds(off, T)]
                s = _dot_nt(qs[h], ks) + (fcol[h] - fr)
                if masked:
                    s = jnp.where(row >= col, s, NEG)
                pr = jnp.exp(s - lcol[h])
                dp = _dot_nt(dos[h], vs)
                ds = pr * (dp - dcol[h])
                acc_ref[h] += _dot(_b(ds), ks)
                fr_ref[h] += _rowsum(ds)

        def step(j, c):
            tile(j, False)
            return c

        lax.fori_loop(0, i, step, 0)
        tile(i, True)
        dq_ref[...] = _b(jnp.where(lane2 < 64, acc_ref[0], acc_ref[1]) * FOX_SCALE)
        dl_ref[0] = jnp.where(lane2 < 64, dcol[0], dcol[1])
        dfr_ref[0] = jnp.where(lane2 < 64, fr_ref[0], fr_ref[1])

    tq = pl.BlockSpec((T, 128), lambda p, i: (i, p))
    pair = pl.BlockSpec((1, T, 128), lambda p, i: (p, i, 0))
    return pl.pallas_call(
        body, grid=(4, S // T),
        out_shape=(jax.ShapeDtypeStruct((S, 512), bf16), jax.ShapeDtypeStruct((4, S, 128), f32), jax.ShapeDtypeStruct((4, S, 128), f32)),
        in_specs=[tq, pl.BlockSpec((S, 128), lambda p, i: (0, 4 + p)), pl.BlockSpec((S, 128), lambda p, i: (0, 8 + p)),
                  pl.BlockSpec((T, 128), lambda p, i: (i, 0)), pl.BlockSpec((8, S), lambda p, i: (0, 0)), pair, tq, tq],
        out_specs=(tq, pair, pair),
        scratch_shapes=[pltpu.VMEM((2, T, 128), f32), pltpu.VMEM((2, T, 1), f32)],
        compiler_params=_cp(("arbitrary", "arbitrary"), VMEM_LIMIT), name=name)(zb, zb, zb, sm, ft, lse, o, do)


def _fox_bwd_dkv(zb, dob, sm, ft, lse_t, dl_t, name):
    S = zb.shape[0]
    T = min(S, 512)
    nq = S // T

    def body(k_ref, v_ref, q_ref, do_ref, fc_ref, ft_ref, lt_ref, dt_ref, dk_ref, dv_ref, dfc_ref, dka_ref, dva_ref, fs_ref):
        p, j = pl.program_id(0), pl.program_id(1)
        lane1 = lax.broadcasted_iota(jnp.int32, (1, 128), 1)
        lane2 = lax.broadcasted_iota(jnp.int32, (T, 128), 1)
        hm = (lane1 < 64, lane1 >= 64)
        k, v = k_ref[...], v_ref[...]
        ksm = [jnp.where(hm[h], k, jnp.zeros_like(k)) * FOX_SCALE for h in (0, 1)]
        vsm = [jnp.where(hm[h], v, jnp.zeros_like(v)) for h in (0, 1)]
        fc = fc_ref[...]
        fck = [_head_col(fc, p, h) for h in (0, 1)]
        dka_ref[...] = jnp.zeros_like(dka_ref)
        dva_ref[...] = jnp.zeros_like(dva_ref)
        fs_ref[...] = jnp.zeros_like(fs_ref)
        row = lax.broadcasted_iota(jnp.int32, (T, T), 0)
        col = lax.broadcasted_iota(jnp.int32, (T, T), 1)

        def tile(i, masked):
            off = pl.multiple_of(i * T, T)
            qt, dot_ = q_ref[pl.ds(off, T), :], do_ref[pl.ds(off, T), :]
            for h in (0, 1):
                hr = pl.ds(2 * p + h, 1)
                fi = ft_ref[hr, pl.ds(off, T)]
                s_t = _dot_nt(ksm[h], qt) + (fi - fck[h])
                if masked:
                    s_t = jnp.where(col >= row, s_t, NEG)
                p_t = jnp.exp(s_t - lt_ref[hr, pl.ds(off, T)])
                dva_ref[h] += _dot(_b(p_t), dot_)
                dp_t = _dot_nt(vsm[h], dot_)
                ds_t = p_t * (dp_t - dt_ref[hr, pl.ds(off, T)])
                dka_ref[h] += _dot(_b(ds_t), qt)
                fs_ref[h] += _rowsum(ds_t)

        def step(i, c):
            tile(i, False)
            return c

        tile(j, True)
        lax.fori_loop(j + 1, nq, step, 0)
        dk_ref[...] = _b(jnp.where(lane2 < 64, dka_ref[0], dka_ref[1]) * FOX_SCALE)
        dv_ref[...] = _b(jnp.where(lane2 < 64, dva_ref[0], dva_ref[1]))
        dfc_ref[0] = jnp.where(lane2 < 64, fs_ref[0], fs_ref[1])

    res = pl.BlockSpec((8, S), lambda p, j: (0, 0))
    tk = pl.BlockSpec((T, 128), lambda p, j: (j, p))
    return pl.pallas_call(
        body, grid=(4, nq),
        out_shape=(jax.ShapeDtypeStruct((S, 512), bf16), jax.ShapeDtypeStruct((S, 512), bf16), jax.ShapeDtypeStruct((4, S, 128), f32)),
        in_specs=[pl.BlockSpec((T, 128), lambda p, j: (j, 4 + p)), pl.BlockSpec((T, 128), lambda p, j: (j, 8 + p)),
                  pl.BlockSpec((S, 128), lambda p, j: (0, p)), pl.BlockSpec((S, 128), lambda p, j: (0, p)),
                  pl.BlockSpec((T, 128), lambda p, j: (j, 0)), res, res, res],
        out_specs=(tk, tk, pl.BlockSpec((1, T, 128), lambda p, j: (p, j, 0))),
        scratch_shapes=[pltpu.VMEM((2, T, 128), f32), pltpu.VMEM((2, T, 128), f32), pltpu.VMEM((2, T, 1), f32)],
        compiler_params=_cp(("arbitrary", "arbitrary"), VMEM_LIMIT), name=name)(zb, zb, zb, dob, sm, ft, lse_t, dl_t)


def _pair_to_rows(a):
    S = a.shape[1]
    return jnp.transpose(a[:, :, 0::64], (0, 2, 1)).reshape(8, S)


def _pair_to_cols(a):
    S = a.shape[1]
    c = jnp.transpose(a[:, :, 0::64], (1, 0, 2)).reshape(S, 8)
    return jnp.pad(c, ((0, 0), (0, 120)))


def _conv_taps(ext, x, w_ref, ts):
    y = x * w_ref[3:4, :]
    shifted = []
    for k in (1, 2, 3):
        xs = pltpu.roll(ext, k, 0)[8:]
        shifted.append(xs)
        y = y + xs * w_ref[3 - k:4 - k, :]
    return y, shifted


def _gdn_prep(zf, cw, name):
    S = zf.shape[0]
    ts = min(S, 512)

    def body(x_ref, w_ref, o_ref, tail_ref):
        i = pl.program_id(0)

        @pl.when(i == 0)
        def _():
            tail_ref[...] = jnp.zeros_like(tail_ref)

        x = x_ref[...]
        ext = jnp.concatenate([tail_ref[...], x], axis=0)
        y, _ = _conv_taps(ext, x, w_ref, ts)
        tail_ref[...] = x[ts - 8:, :]
        a = _silu(y)
        for hb in range(12):
            blk = a[:, hb * 128:(hb + 1) * 128]
            if hb < 8:
                blk = blk * lax.rsqrt(_rowsum(blk * blk) + EPS)
            if hb < 4:
                blk = blk * GDN_SCALE
            o_ref[:, hb * 128:(hb + 1) * 128] = blk

    return pl.pallas_call(
        body, grid=(S // ts,), out_shape=jax.ShapeDtypeStruct((S, 1536), f32),
        in_specs=[pl.BlockSpec((ts, 1536), lambda i: (i, 0)), pl.BlockSpec((8, 1536), lambda i: (0, 0))],
        out_specs=pl.BlockSpec((ts, 1536), lambda i: (i, 0)), scratch_shapes=[pltpu.VMEM((8, 1536), f32)],
        compiler_params=_cp(("arbitrary",), VMEM_LIMIT), name=name)(zf, cw)


def _gdn_prep_bwd(zf, cw, dg, name):
    S = zf.shape[0]
    ts = min(S, 512)
    nt = S // ts

    def body(x_ref, xp_ref, w_ref, dg_ref, dx_ref, dw_ref, head_ref):
        i = pl.program_id(0)

        @pl.when(i == 0)
        def _():
            head_ref[...] = jnp.zeros_like(head_ref)

        x = x_ref[...]
        prev = jnp.where(i == nt - 1, 0.0, xp_ref[...])
        ext = jnp.concatenate([prev, x], axis=0)
        y, shifted = _conv_taps(ext, x, w_ref, ts)
        a = _silu(y)
        das = []
        for hb in range(12):
            blk = a[:, hb * 128:(hb + 1) * 128]
            d = dg_ref[:, hb * 128:(hb + 1) * 128]
            if hb < 4:
                d = d * GDN_SCALE
            if hb < 8:
                r = lax.rsqrt(_rowsum(blk * blk) + EPS)
                n = blk * r
                d = r * (d - n * _rowsum(d * n))
            das.append(d)
        dy = jnp.concatenate(das, axis=1) * _dsilu(y)
        extd = jnp.concatenate([dy, head_ref[...]], axis=0)
        dx = dy * w_ref[3:4, :]
        for k in (1, 2, 3):
            dx = dx + pltpu.roll(extd, ts + 8 - k, 0)[:ts] * w_ref[3 - k:4 - k, :]
        head_ref[...] = dy[0:8, :]
        dx_ref[...] = _b(dx)
        r8 = lax.broadcasted_iota(jnp.int32, (8, 1536), 0)
        part = jnp.where(r8 == 3, _colsum(dy * x), 0.0)
        for k in (1, 2, 3):
            part = jnp.where(r8 == 3 - k, _colsum(dy * shifted[k - 1]), part)

        @pl.when(i == 0)
        def _():
            dw_ref[...] = part

        @pl.when(i > 0)
        def _():
            dw_ref[...] += part

    rev = pl.BlockSpec((ts, 1536), lambda i: (nt - 1 - i, 0))
    prev8 = pl.BlockSpec((8, 1536), lambda i: (jnp.maximum((nt - 1 - i) * (ts // 8) - 1, 0), 0))
    w8 = pl.BlockSpec((8, 1536), lambda i: (0, 0))
    return pl.pallas_call(
        body, grid=(nt,), out_shape=(jax.ShapeDtypeStruct((S, 1536), bf16), jax.ShapeDtypeStruct((8, 1536), f32)),
        in_specs=[rev, prev8, w8, rev], out_specs=(rev, w8), scratch_shapes=[pltpu.VMEM((8, 1536), f32)],
        compiler_params=_cp(("arbitrary",), VMEM_LIMIT), name=name)(zf, zf, cw, dg)


def _tri_inv(a, row, col):
    same = (row >> 4) == (col >> 4)
    dm = jnp.where(same, a, 0.0)
    lo = a - dm
    eye = jnp.where(row == col, 1.0, 0.0)
    d2 = _hi(dm, dm)
    d4 = _hi(d2, d2)
    d8 = _hi(d4, d4)
    x0 = _hi(_hi(eye - dm, eye + d2), _hi(eye + d4, eye + d8))
    n = _hi(x0, lo)
    n2 = _hi(n, n)
    return _hi(_hi(eye - n, eye + n2), x0)


def _gdn_local(x_ref, sm_ref, gt_ref, h, row, col):
    q = x_ref[:, h * 128:(h + 1) * 128]
    k = x_ref[:, 512 + h * 128:512 + (h + 1) * 128]
    v = x_ref[:, 1024 + h * 128:1024 + (h + 1) * 128]
    gc = sm_ref[:, 8 + h:9 + h]
    beta = sm_ref[:, 12 + h:13 + h]
    gr = gt_ref[h, 0]
    eg = jnp.exp(gc)
    gl = gc[CH - 1:CH, :]
    dec = jnp.exp(gl - gc)
    gm = gc - gr
    gam_i = jnp.exp(jnp.where(row >= col, gm, -jnp.inf))
    gam_s = jnp.where(row > col, gam_i, 0.0)
    kb = k * beta
    return dict(q=q, k=k, v=v, beta=beta, eg=eg, egl=jnp.exp(gl), dec=dec, gam_i=gam_i, gam_s=gam_s,
                kb=kb, vb=v * beta, kbg=kb * eg, qdec=q * eg, kdec=k * dec,
                a=_dot_nt(_b(kb), _b(k)) * gam_s, aqk=_dot_nt(_b(q), _b(k)) * gam_i)


def _gdn_fwd(gqkv, sm, gt4, name):
    S = gqkv.shape[0]
    N = S // CH

    def body(x_ref, sm_ref, gt_ref, o_ref, t_ref, st_ref, s_ref):
        n = pl.program_id(0)

        @pl.when(n == 0)
        def _():
            s_ref[...] = jnp.zeros_like(s_ref)

        row = lax.broadcasted_iota(jnp.int32, (CH, CH), 0)
        col = lax.broadcasted_iota(jnp.int32, (CH, CH), 1)
        for h in range(4):
            c = _gdn_local(x_ref, sm_ref, gt_ref, h, row, col)
            t = _tri_inv(c["a"], row, col)
            t_ref[h] = t
            uw = _hi(t, jnp.concatenate([c["vb"], c["kbg"]], axis=1))
            u, w = uw[:, :128], uw[:, 128:]
            st = s_ref[h]
            st_ref[0, h] = st
            sb = _b(st)
            vnew = u - _dot(_b(w), sb)
            o_ref[:, h * 128:(h + 1) * 128] = _dot(_b(c["qdec"]), sb) + _dot(_b(c["aqk"]), _b(vnew))
            s_ref[h] = st * c["egl"] + _dot_tn(_b(c["kdec"]), _b(vnew))

    return pl.pallas_call(
        body, grid=(N,),
        out_shape=(jax.ShapeDtypeStruct((S, 512), f32), jax.ShapeDtypeStruct((4, S, CH), f32), jax.ShapeDtypeStruct((N, 4, 128, 128), f32)),
        in_specs=[pl.BlockSpec((CH, 1536), lambda n: (n, 0)), pl.BlockSpec((CH, 128), lambda n: (n, 0)),
                  pl.BlockSpec((4, 1, 1, CH), lambda n: (0, n, 0, 0))],
        out_specs=(pl.BlockSpec((CH, 512), lambda n: (n, 0)), pl.BlockSpec((4, CH, CH), lambda n: (0, n, 0)),
                   pl.BlockSpec((1, 4, 128, 128), lambda n: (n, 0, 0, 0))),
        scratch_shapes=[pltpu.VMEM((4, 128, 128), f32)], compiler_params=_cp(("arbitrary",)), name=name)(gqkv, sm, gt4)


def _gdn_bwd(gqkv, sm, gt4, tinv, states, do, name):
    S = gqkv.shape[0]
    N = S // CH

    def body(x_ref, sm_ref, gt_ref, t_ref, st_ref, do_ref, dx_ref, dsm_ref, ds_ref):
        n = pl.program_id(0)

        @pl.when(n == 0)
        def _():
            ds_ref[...] = jnp.zeros_like(ds_ref)

        row = lax.broadcasted_iota(jnp.int32, (CH, CH), 0)
        col = lax.broadcasted_iota(jnp.int32, (CH, CH), 1)
        row1 = lax.broadcasted_iota(jnp.int32, (CH, 1), 0)
        lane = lax.broadcasted_iota(jnp.int32, (CH, 128), 1)
        ones = jnp.ones((CH, 128), f32)
        dsm = jnp.zeros((CH, 128), f32)
        for h in range(4):
            c = _gdn_local(x_ref, sm_ref, gt_ref, h, row, col)
            q, k, v, beta, eg = c["q"], c["k"], c["v"], c["beta"], c["eg"]
            t = t_ref[h]
            uw = _hi(t, jnp.concatenate([c["vb"], c["kbg"]], axis=1))
            u, w = uw[:, :128], uw[:, 128:]
            st = st_ref[0, h]
            sb = _b(st)
            vnew = u - _dot(_b(w), sb)
            dob = _b(do_ref[:, h * 128:(h + 1) * 128])
            dsp = ds_ref[h]
            dspb = _b(dsp)
            vnb = _b(vnew)
            dvnew = _dot(_b(c["kdec"]), dspb) + _dot_tn(_b(c["aqk"]), dob)
            dkdec = _dot_nt(vnb, dspb)
            dgl = c["egl"] * jnp.sum(dsp * st, keepdims=True)
            dqdec = _dot_nt(dob, sb)
            daqk = jnp.where(row >= col, _dot_nt(dob, vnb), 0.0)
            dvnb = _b(dvnew)
            dw = -_dot_nt(dvnb, sb)
            ds_ref[h] = dsp * c["egl"] + _dot_tn(_b(c["qdec"]), dob) - _dot_tn(_b(w), dvnb)
            duw = _hi_tn(t, jnp.concatenate([dvnew, dw], axis=1))
            dvb, dkbg = duw[:, :128], duw[:, 128:]
            da = -jnp.where(row > col, _hi_nt(duw, uw), 0.0)
            dp = da * c["gam_s"]
            dqk = daqk * c["gam_i"]
            m = da * c["a"] + daqk * c["aqk"]
            csum = _hi_tn(m, ones)[:, 0:1]
            kk = dkdec * c["kdec"]
            dgv = _rowsum(m) - csum + _rowsum(dqdec * c["qdec"]) - _rowsum(kk) + _rowsum(dkbg * c["kbg"])
            dgv = dgv + jnp.where(row1 == CH - 1, dgl + jnp.sum(kk, keepdims=True), 0.0)
            dpb, dqkb = _b(dp), _b(dqk)
            dkb = _dot(dpb, _b(k)) + dkbg * eg
            dk = _dot_tn(dpb, _b(c["kb"])) + _dot_tn(dqkb, _b(q)) + dkdec * c["dec"] + dkb * beta
            dq = _dot(dqkb, _b(k)) + dqdec * eg
            dbeta = _rowsum(dkb * k) + _rowsum(dvb * v)
            dx_ref[:, h * 128:(h + 1) * 128] = dq
            dx_ref[:, 512 + h * 128:512 + (h + 1) * 128] = dk
            dx_ref[:, 1024 + h * 128:1024 + (h + 1) * 128] = dvb * beta
            dsm = jnp.where(lane == 8 + h, dgv, jnp.where(lane == 12 + h, dbeta, dsm))
        dsm_ref[...] = dsm

    return pl.pallas_call(
        body, grid=(N,), out_shape=(jax.ShapeDtypeStruct((S, 1536), f32), jax.ShapeDtypeStruct((S, 128), f32)),
        in_specs=[pl.BlockSpec((CH, 1536), lambda n: (N - 1 - n, 0)), pl.BlockSpec((CH, 128), lambda n: (N - 1 - n, 0)),
                  pl.BlockSpec((4, 1, 1, CH), lambda n: (0, N - 1 - n, 0, 0)), pl.BlockSpec((4, CH, CH), lambda n: (0, N - 1 - n, 0)),
                  pl.BlockSpec((1, 4, 128, 128), lambda n: (N - 1 - n, 0, 0, 0)), pl.BlockSpec((CH, 512), lambda n: (N - 1 - n, 0))],
        out_specs=(pl.BlockSpec((CH, 1536), lambda n: (N - 1 - n, 0)), pl.BlockSpec((CH, 128), lambda n: (N - 1 - n, 0))),
        scratch_shapes=[pltpu.VMEM((4, 128, 128), f32)], compiler_params=_cp(("arbitrary",)), name=name)(gqkv, sm, gt4, tinv, states, do)


def _mem_attn(q, kv_ref, h):
    s = _dot_nt(q, kv_ref[:, h * 128:(h + 1) * 128]) * MEM_SCALE
    e = jnp.exp(s - jnp.max(s, axis=1, keepdims=True))
    return e / _rowsum(e)


def _gdn_out_norm(ob):
    r = lax.rsqrt(jnp.mean(ob * ob, axis=-1, keepdims=True) + EPS)
    return ob * r, r


def _merge_fwd(x, oa, ob, zb, zf, kv, b_merge, gdn_g, w_branch, w_out, name):
    S = x.shape[0]
    ts = min(S, 256)

    def body(x_ref, oa_ref, ob_ref, mq_ref, az_ref, bz_ref, mz_ref, gt_ref, kv_ref, bm_ref, gg_ref, wb_ref, wo_ref,
             xo_ref, y_ref, mg_ref):
        y_ref[:, 0:512] = _b(oa_ref[...] * _silu(az_ref[...]))
        for h in range(4):
            sl = slice(h * 128, (h + 1) * 128)
            nb, _ = _gdn_out_norm(ob_ref[:, sl])
            y_ref[:, 512 + h * 128:512 + (h + 1) * 128] = _b(nb * gg_ref[...] * _silu(bz_ref[:, sl]))
            pm = _mem_attn(mq_ref[:, sl], kv_ref, h)
            om = _dot(_b(pm), kv_ref[:, 512 + h * 128:512 + (h + 1) * 128])
            y_ref[:, 1024 + h * 128:1024 + (h + 1) * 128] = _b(om * _silu(mz_ref[:, sl]))
        merged = jnp.zeros((ts, D), f32)
        for n in range(3):
            gate = _sig(gt_ref[:, n * D:(n + 1) * D] + bm_ref[:, n * D:(n + 1) * D])
            merged = merged + gate * _dot(y_ref[:, n * 512:(n + 1) * 512], wb_ref[n])
        mb = _b(merged)
        mg_ref[...] = mb
        xo_ref[...] = x_ref[...] + _dot(mb, wo_ref[...])

    def col(w, c):
        return pl.BlockSpec((ts, w), lambda i: (i, c))

    def full(shape):
        return pl.BlockSpec(shape, lambda i: tuple(0 for _ in shape))

    return pl.pallas_call(
        body, grid=(S // ts,),
        out_shape=(jax.ShapeDtypeStruct((S, D), f32), jax.ShapeDtypeStruct((S, 1536), bf16), jax.ShapeDtypeStruct((S, D), bf16)),
        in_specs=[col(D, 0), col(512, 0), col(512, 0), col(512, 3), col(512, 3), col(512, 4), col(512, 5), col(3072, 1),
                  full((256, D)), full((1, 3072)), full((1, 128)), full((3, 512, D)), full((D, D))],
        out_specs=(col(D, 0), col(1536, 0), col(D, 0)),
        compiler_params=_cp(("parallel",), VMEM_LIMIT), name=name)(x, oa, ob, zb, zf, zf, zf, zf, kv, b_merge, gdn_g, w_branch, w_out)


def _merge_bwd(dout, ycat, oa, ob, zb, zf, kv, b_merge, gdn_g, w_branch, w_branch_t, w_out_t, name):
    S = dout.shape[0]
    ts = min(S, 256)

    def body(do_ref, y_ref, oa_ref, ob_ref, mq_ref, az_ref, bz_ref, mz_ref, gt_ref, kv_ref, bm_ref, gg_ref, wb_ref, wbt_ref, wot_ref,
             dpj_ref, dz_ref, dmq_ref, doa_ref, doab_ref, dob_ref, dkv_ref, dbm_ref, dgg_ref):
        i = pl.program_id(0)
        dmerged = _dot(_b(do_ref[...]), wot_ref[...])
        dys = []
        dbm_parts = []
        for n in range(3):
            cs = slice(n * D, (n + 1) * D)
            gate = _sig(gt_ref[:, cs] + bm_ref[:, cs])
            proj = _dot(y_ref[:, n * 512:(n + 1) * 512], wb_ref[n])
            dlogit = dmerged * proj * gate * (1.0 - gate)
            dz_ref[:, 1536 + n * D:1536 + (n + 1) * D] = _b(dlogit)
            dbm_parts.append(_colsum(dlogit))
            dproj = _b(dmerged * gate)
            dpj_ref[:, cs] = dproj
            dys.append(_dot(dproj, wbt_ref[n]))
        dbm = jnp.broadcast_to(jnp.concatenate(dbm_parts, axis=1), (8, 3072))
        az = az_ref[...]
        oa = oa_ref[...]
        doa = dys[0] * _silu(az)
        doa_ref[...] = doa
        doab_ref[...] = _b(doa)
        dz_ref[:, 0:512] = _b(dys[0] * oa * _dsilu(az))
        gg = gg_ref[...]
        dgg = jnp.zeros((1, 128), f32)
        dkv_parts_k, dkv_parts_v = [], []
        for h in range(4):
            sl = slice(h * 128, (h + 1) * 128)
            bz = bz_ref[:, sl]
            dyb = dys[1][:, sl]
            nb, r = _gdn_out_norm(ob_ref[:, sl])
            dz_ref[:, 512 + h * 128:512 + (h + 1) * 128] = _b(dyb * nb * gg * _dsilu(bz))
            dng = dyb * _silu(bz)
            dgg = dgg + _colsum(dng * nb)
            dnb = dng * gg
            dob_ref[:, sl] = r * (dnb - nb * jnp.mean(dnb * nb, axis=-1, keepdims=True))
            mz = mz_ref[:, sl]
            dym = dys[2][:, sl]
            q = mq_ref[:, sl]
            kh = kv_ref[:, sl]
            vh = kv_ref[:, 512 + h * 128:512 + (h + 1) * 128]
            pm = _mem_attn(q, kv_ref, h)
            pmb = _b(pm)
            om = _dot(pmb, vh)
            dz_ref[:, 1024 + h * 128:1024 + (h + 1) * 128] = _b(dym * om * _dsilu(mz))
            dom = _b(dym * _silu(mz))
            dkv_parts_v.append(_dot_tn(pmb, dom))
            dpm = _dot_nt(dom, vh)
            dsm = _b(pm * (dpm - _rowsum(dpm * pm)) * MEM_SCALE)
            dmq_ref[:, sl] = _b(_dot(dsm, kh))
            dkv_parts_k.append(_dot_tn(dsm, q))
        dkv = jnp.concatenate(dkv_parts_k + dkv_parts_v, axis=1)
        dggb = jnp.broadcast_to(dgg, (8, 128))

        @pl.when(i == 0)
        def _():
            dkv_ref[...] = dkv
            dbm_ref[...] = dbm
            dgg_ref[...] = dggb

        @pl.when(i > 0)
        def _():
            dkv_ref[...] += dkv
            dbm_ref[...] += dbm
            dgg_ref[...] += dggb

    def col(w, c):
        return pl.BlockSpec((ts, w), lambda i: (i, c))

    def full(shape):
        return pl.BlockSpec(shape, lambda i: tuple(0 for _ in shape))

    return pl.pallas_call(
        body, grid=(S // ts,),
        out_shape=(jax.ShapeDtypeStruct((S, 3072), bf16), jax.ShapeDtypeStruct((S, 4608), bf16), jax.ShapeDtypeStruct((S, 512), bf16),
                   jax.ShapeDtypeStruct((S, 512), f32), jax.ShapeDtypeStruct((S, 512), bf16), jax.ShapeDtypeStruct((S, 512), f32),
                   jax.ShapeDtypeStruct((256, D), f32), jax.ShapeDtypeStruct((8, 3072), f32), jax.ShapeDtypeStruct((8, 128), f32)),
        in_specs=[col(D, 0), col(1536, 0), col(512, 0), col(512, 0), col(512, 3), col(512, 3), col(512, 4), col(512, 5), col(3072, 1),
                  full((256, D)), full((1, 3072)), full((1, 128)), full((3, 512, D)), full((3, D, 512)), full((D, D))],
        out_specs=(col(3072, 0), col(4608, 0), col(512, 0), col(512, 0), col(512, 0), col(512, 0),
                   full((256, D)), full((8, 3072)), full((8, 128))),
        compiler_params=_cp(("arbitrary",), VMEM_LIMIT), name=name)(
            dout, ycat, oa, ob, zb, zf, zf, zf, zf, kv, b_merge, gdn_g, w_branch, w_branch_t, w_out_t)


def _mesh_pos():
    return lax.axis_index("x"), lax.axis_index("y"), lax.axis_index("c")


def _all_gather(x, name):
    R, C = x.shape

    def body(x_ref, out_ref, send_sems, recv_sems, local_sem):
        mx, my, mc = _mesh_pos()
        me, sibling = (mx, my, mc), (mx, my, 1 - mc)
        chips = [(1 - mx, my), (mx, 1 - my), (1 - mx, 1 - my)]

        def slot(px, py, pc):
            return out_ref.at[4 * px + 2 * py + pc]

        def copy(k, block, to, src=None):
            return pltpu.make_async_remote_copy(
                src_ref=slot(*block) if src is None else src, dst_ref=slot(*block),
                send_sem=send_sems.at[k], recv_sem=recv_sems.at[k], device_id=to, device_id_type=pl.DeviceIdType.MESH)

        mine = pltpu.make_async_copy(x_ref, slot(*me), local_sem)
        mine.start()
        first = [copy(0, me, sibling, src=x_ref)]
        first += [copy(1 + j, me, (*chip, mc), src=x_ref) for j, chip in enumerate(chips)]
        for cp in first:
            cp.start()
        passed = [copy(4 + j, (*chip, mc), sibling) for j, chip in enumerate(chips)]
        for j, chip in enumerate(chips):
            copy(1 + j, (*chip, mc), me).wait_recv()
            passed[j].start()
        copy(0, sibling, me).wait_recv()
        for j, chip in enumerate(chips):
            copy(4 + j, (*chip, 1 - mc), me).wait_recv()
        for cp in first + passed:
            cp.wait_send()
        mine.wait()

    return pl.pallas_call(
        body, out_shape=jax.ShapeDtypeStruct((N_DEV, R, C), x.dtype),
        in_specs=[pl.BlockSpec(memory_space=pl.ANY)], out_specs=pl.BlockSpec(memory_space=pl.ANY),
        scratch_shapes=[pltpu.SemaphoreType.DMA((7,)), pltpu.SemaphoreType.DMA((7,)), pltpu.SemaphoreType.DMA],
        name=name)(x)


def _exchange(send, name):
    def body(s_ref, r_ref, send_sems, recv_sems, local_sem):
        mx, my, mc = _mesh_pos()
        me_id = 4 * mx + 2 * my + mc
        mine = pltpu.make_async_copy(s_ref.at[me_id], r_ref.at[me_id], local_sem)
        mine.start()
        copies = []
        for k in range(1, N_DEV):
            px = 1 - mx if k & 4 else mx
            py = 1 - my if k & 2 else my
            pc = 1 - mc if k & 1 else mc
            peer_id = 4 * px + 2 * py + pc
            copies.append(pltpu.make_async_remote_copy(
                src_ref=s_ref.at[peer_id], dst_ref=r_ref.at[me_id], send_sem=send_sems.at[k - 1], recv_sem=recv_sems.at[k - 1],
                device_id=(px, py, pc), device_id_type=pl.DeviceIdType.MESH))
        for cp in copies:
            cp.start()
        for cp in copies:
            cp.wait()
        mine.wait()

    return pl.pallas_call(
        body, out_shape=jax.ShapeDtypeStruct(send.shape, send.dtype),
        in_specs=[pl.BlockSpec(memory_space=pl.ANY)], out_specs=pl.BlockSpec(memory_space=pl.ANY),
        scratch_shapes=[pltpu.SemaphoreType.DMA((7,)), pltpu.SemaphoreType.DMA((7,)), pltpu.SemaphoreType.DMA],
        name=name)(send)


def _adamw(parts, w, m, v, name):
    _, R, C = parts.shape
    tr = 80 if R % 80 == 0 else R

    def body(p_ref, w_ref, m_ref, v_ref, g_ref, d_ref, nm_ref, nv_ref):
        g = p_ref[0]
        for j in range(1, N_DEV):
            g = g + p_ref[j]
        mn = ADAM_B1 * m_ref[...] + (1.0 - ADAM_B1) * g
        vn = ADAM_B2 * v_ref[...] + (1.0 - ADAM_B2) * jnp.square(g)
        m_hat = mn / (1.0 - ADAM_B1 ** ADAM_STEP)
        v_hat = vn / (1.0 - ADAM_B2 ** ADAM_STEP)
        g_ref[...] = g
        d_ref[...] = -ADAM_LR * (m_hat / (jnp.sqrt(v_hat) + ADAM_EPS) + ADAM_WD * w_ref[...])
        nm_ref[...] = mn
        nv_ref[...] = vn

    t2 = pl.BlockSpec((tr, C), lambda i: (i, 0))
    out = jax.ShapeDtypeStruct((R, C), f32)
    return pl.pallas_call(
        body, grid=(R // tr,), out_shape=(out, out, out, out),
        in_specs=[pl.BlockSpec((N_DEV, tr, C), lambda i: (0, i, 0)), t2, t2, t2], out_specs=(t2, t2, t2, t2),
        compiler_params=_cp(("parallel",), VMEM_LIMIT), name=name)(parts, w, m, v)


def _flat_shard(tensors):
    flat = jnp.concatenate([tensors[n].reshape(-1) for n, _ in _SHARD_SHAPES])
    return jnp.pad(flat, (0, FLAT_R * FLAT_C - flat.shape[0])).reshape(FLAT_R, FLAT_C)


def _unflat_shard(flat):
    lead = flat.shape[:-2]
    flat = flat.reshape(lead + (FLAT_R * FLAT_C,))
    out, off = {}, 0
    for n, shp in _SHARD_SHAPES:
        size = 1
        for s in shp:
            size *= s
        out[n] = flat[..., off:off + size].reshape(lead + shp)
        off += size
    return out


def _perm_cols(w):
    parts = [w[..., _COLS[n][0]:_COLS[n][1]] for n in _ORDER]
    pad = jnp.zeros(w.shape[:-1] + (N_ALL - N_IN,), w.dtype)
    return jnp.concatenate(parts + [pad], axis=-1)


def _unperm_cols(w):
    pieces, off = {}, 0
    for n in _ORDER:
        width = _COLS[n][1] - _COLS[n][0]
        pieces[n] = w[..., off:off + width]
        off += width
    return jnp.concatenate([pieces[n] for n in sorted(_COLS, key=lambda n: _COLS[n][0])], axis=-1)


_SMALL_ROWS = 16


def _pack_small(t):
    z = jnp.zeros((D,), f32)
    misc = z.at[0:16].set(t["b_fg"].reshape(-1)).at[16:24].set(t["a_log"].reshape(-1)).at[24:32].set(t["dt_bias"].reshape(-1))
    misc = misc.at[128:384].set(t["gdn_norm_g"].reshape(-1))
    if "extra" in t:
        misc = misc.at[512].set(t["extra"])
    rows = [t["norm_g"], t["b_merge"].reshape(6, D), t["mem_norm_g"], t["final_norm_g"][None], misc[None],
            jnp.zeros((_SMALL_ROWS - 12, D), f32)]
    return jnp.concatenate(rows, axis=0)


def _unpack_small(a):
    misc = a[11]
    return dict(norm_g=a[0:2], b_merge=a[2:8].reshape(2, 3072), mem_norm_g=a[8:10], final_norm_g=a[10],
                b_fg=misc[0:16].reshape(2, 8), a_log=misc[16:24].reshape(2, 4), dt_bias=misc[24:32].reshape(2, 4),
                gdn_norm_g=misc[128:384].reshape(2, 128), extra=misc[512])


def _layer_fwd(l, x, mem, p):
    sfx = f"_l{l}"
    h = _norm_fwd(x, p["norm_g"], "norm_fwd" + sfx)
    zb = _mm(h, p["w_b"], bf16, 512, 1024, 1024, "inproj_b" + sfx)
    zf = _mm(h, p["w_f"], f32, 512, 1024, 1024, "inproj_f" + sfx)
    zs = _mm(h, p["w_s"], f32, 512, 128, 1024, "inproj_s" + sfx)
    sm = _small_prep(zs, p["par"], "small_prep" + sfx)
    S = x.shape[0]
    ft = jnp.transpose(sm[:, 0:8])
    gt4 = jnp.transpose(sm[:, 8:12]).reshape(4, S // CH, 1, CH)
    oa, lse = _fox_fwd(zb, sm, ft, "fox_fwd" + sfx)
    gqkv = _gdn_prep(zf, p["conv_w"], "gdn_prep" + sfx)
    ob, tinv, states = _gdn_fwd(gqkv, sm, gt4, "gdn_fwd" + sfx)
    memn = _norm_fwd(mem, p["mem_norm_g"], "mem_norm" + sfx)
    kv = _mm(memn, p["w_mem_kv"], bf16, 256, 1024, 1024, "mem_kv" + sfx)
    xo, ycat, merged = _merge_fwd(x, oa, ob, zb, zf, kv, p["b_merge"], p["gdn_norm_g"], p["w_branch"], p["w_out"], "merge_fwd" + sfx)
    saved = dict(x=x, h=h, zb=zb, zf=zf, zs=zs, sm=sm, ft=ft, gt4=gt4, oa=oa, lse=lse, gqkv=gqkv, ob=ob, tinv=tinv,
                 states=states, memn=memn, kv=kv, ycat=ycat, merged=merged)
    return xo, saved


def _layer_bwd(l, dout, mem, p, s):
    sfx = f"_l{l}"
    dproj, dzf2, dmq, doa, doab, dob, dkv, dbm, dgg = _merge_bwd(
        dout, s["ycat"], s["oa"], s["ob"], s["zb"], s["zf"], s["kv"], p["b_merge"], p["gdn_norm_g"],
        p["w_branch"], p["w_branch_t"], p["w_out_t"], "merge_bwd" + sfx)
    g = {}
    g["w_out"] = _mm(s["merged"], dout, f32, 512, 1024, 512, "dw_out" + sfx, trans_a=True)
    g["w_branch"] = jnp.stack([
        _mm(s["ycat"][:, n * 512:(n + 1) * 512], dproj[:, n * D:(n + 1) * D], f32, 512, 1024, 512, f"dw_branch{n}" + sfx, trans_a=True)
        for n in range(3)])
    g["b_merge"] = dbm[0]
    g["gdn_norm_g"] = dgg[0]
    g["w_mem_kv"] = _mm(s["memn"], dkv, f32, 512, 1024, 256, "dw_mem_kv" + sfx, trans_a=True)
    dmemn = _mm(dkv, p["w_mem_kv_t"], f32, 256, 1024, 1024, "dmem_n" + sfx)
    g["mem_norm_g"] = _norm_bwd(mem, p["mem_norm_g"], dmemn, None, "mem_norm_bwd" + sfx)[0]
    dgqkv, dsm = _gdn_bwd(s["gqkv"], s["sm"], s["gt4"], s["tinv"], s["states"], dob, "gdn_bwd" + sfx)
    dbqkv, dcw = _gdn_prep_bwd(s["zf"], p["conv_w"], dgqkv, "gdn_prep_bwd" + sfx)
    g["conv_w"] = dcw[0:4]
    dq, delta, dfr = _fox_bwd_dq(s["zb"], s["sm"], s["ft"], s["lse"], s["oa"], doa, "fox_bwd_dq" + sfx)
    dk, dv, dfc = _fox_bwd_dkv(s["zb"], doab, s["sm"], s["ft"], _pair_to_rows(s["lse"]), _pair_to_rows(delta), "fox_bwd_dkv" + sfx)
    dzs, sacc = _small_bwd(s["zs"], p["par"], _pair_to_cols(dfr), _pair_to_cols(dfc), dsm, "small_bwd" + sfx)
    g["b_fg"], g["a_log"], g["dt_bias"] = sacc[0, 0:8], sacc[1, 8:12], sacc[2, 8:12]
    dz = jnp.concatenate([dq, dk, dv, dmq, dbqkv, dzf2, dzs], axis=1)
    dh = _mm(dz, p["w_all_t"], f32, 512, 1024, 1664, "dh" + sfx)
    g["w_in"] = _mm(s["h"], dz, f32, 512, 1664, 512, "dw_in" + sfx, trans_a=True)
    dx, dng = _norm_bwd(s["x"], p["norm_g"], dh, dout, "norm_bwd" + sfx)
    g["norm_g"] = dng[0]
    return dx, g


def kernel(x, mem, norm_g, w_in, b_fg, b_merge, conv_w, a_log, dt_bias, gdn_norm_g, mem_norm_g, w_mem_kv, w_branch, w_out, final_norm_g, loss_target, m_norm_g, m_w_in, m_b_fg, m_b_merge, m_conv_w, m_a_log, m_dt_bias, m_gdn_norm_g, m_mem_norm_g, m_w_mem_kv, m_w_branch, m_w_out, m_final_norm_g, v_norm_g, v_w_in, v_b_fg, v_b_merge, v_conv_w, v_a_log, v_dt_bias, v_gdn_norm_g, v_mem_norm_g, v_w_mem_kv, v_w_branch, v_w_out, v_final_norm_g):
    x0, mem0, tgt = x[0], mem[0], loss_target[0]
    shard_w = dict(w_in=w_in, w_mem_kv=w_mem_kv, w_branch=w_branch, w_out=w_out, conv_w=conv_w)
    shard_m = dict(w_in=m_w_in, w_mem_kv=m_w_mem_kv, w_branch=m_w_branch, w_out=m_w_out, conv_w=m_conv_w)
    shard_v = dict(w_in=v_w_in, w_mem_kv=v_w_mem_kv, w_branch=v_w_branch, w_out=v_w_out, conv_w=v_conv_w)
    small_w = dict(norm_g=norm_g, b_fg=b_fg, b_merge=b_merge, a_log=a_log, dt_bias=dt_bias, gdn_norm_g=gdn_norm_g,
                   mem_norm_g=mem_norm_g, final_norm_g=final_norm_g)
    small_m = dict(norm_g=m_norm_g, b_fg=m_b_fg, b_merge=m_b_merge, a_log=m_a_log, dt_bias=m_dt_bias, gdn_norm_g=m_gdn_norm_g,
                   mem_norm_g=m_mem_norm_g, final_norm_g=m_final_norm_g)
    small_v = dict(norm_g=v_norm_g, b_fg=v_b_fg, b_merge=v_b_merge, a_log=v_a_log, dt_bias=v_dt_bias, gdn_norm_g=v_gdn_norm_g,
                   mem_norm_g=v_mem_norm_g, final_norm_g=v_final_norm_g)

    flat_w = _flat_shard(shard_w)
    gathered = _unflat_shard(_all_gather(_b(flat_w), "gather_weights"))
    conv_all = _all_gather(conv_w.reshape(DEPTH * 4, 192), "gather_conv")
    conv_full = jnp.transpose(conv_all.reshape(N_DEV, DEPTH, 4, 192), (1, 2, 0, 3)).reshape(DEPTH, 4, 1536)
    w_in_full = jnp.transpose(gathered["w_in"], (1, 2, 0, 3)).reshape(DEPTH, D, N_IN)
    w_all = _perm_cols(w_in_full)
    w_kv_full = jnp.transpose(gathered["w_mem_kv"], (1, 0, 2, 3)).reshape(DEPTH, D, D)
    w_br_full = jnp.transpose(gathered["w_branch"], (1, 2, 3, 0, 4)).reshape(DEPTH, 3, 512, D)
    w_out_full = jnp.transpose(gathered["w_out"], (1, 0, 2, 3)).reshape(DEPTH, D, D)

    layers = []
    for l in range(DEPTH):
        layers.append(dict(
            norm_g=norm_g[l][None], mem_norm_g=mem_norm_g[l][None], gdn_norm_g=gdn_norm_g[l][None], b_merge=b_merge[l][None],
            par=_small_pars(b_fg[l], a_log[l], dt_bias[l]),
            conv_w=jnp.pad(conv_full[l], ((0, 4), (0, 0))),
            w_b=w_all[l][:, 0:NB], w_f=w_all[l][:, NB:NB + NF], w_s=w_all[l][:, NB + NF:], w_all_t=jnp.transpose(w_all[l]),
            w_mem_kv=w_kv_full[l], w_mem_kv_t=jnp.transpose(w_kv_full[l]),
            w_branch=w_br_full[l], w_branch_t=jnp.transpose(w_br_full[l], (0, 2, 1)),
            w_out=w_out_full[l], w_out_t=jnp.transpose(w_out_full[l])))

    acts, saved = x0, []
    for l in range(DEPTH):
        acts, s = _layer_fwd(l, acts, mem0, layers[l])
        saved.append(s)
    dx, dfg, lsum = _loss_head(acts, final_norm_g[None], tgt, "loss_head")

    grads = [None] * DEPTH
    for l in reversed(range(DEPTH)):
        dx, grads[l] = _layer_bwd(l, dx, mem0, layers[l], saved[l])
    grad_x = dx[None]

    def per_dev(name):
        return jnp.stack([grads[l][name] for l in range(DEPTH)])

    dw_in = _unperm_cols(per_dev("w_in"))
    send = dict(
        w_in=jnp.transpose(dw_in.reshape(DEPTH, D, N_DEV, 1026), (2, 0, 1, 3)),
        w_mem_kv=jnp.transpose(per_dev("w_mem_kv").reshape(DEPTH, N_DEV, 128, D), (1, 0, 2, 3)),
        w_branch=jnp.transpose(per_dev("w_branch").reshape(DEPTH, 3, 512, N_DEV, 128), (3, 0, 1, 2, 4)),
        w_out=jnp.transpose(per_dev("w_out").reshape(DEPTH, N_DEV, 128, D), (1, 0, 2, 3)),
        conv_w=jnp.transpose(per_dev("conv_w").reshape(DEPTH, 4, N_DEV, 192), (2, 0, 1, 3)))
    send_flat = jnp.concatenate([send[n].reshape(N_DEV, -1) for n, _ in _SHARD_SHAPES], axis=1)
    send_flat = jnp.pad(send_flat, ((0, 0), (0, FLAT_R * FLAT_C - send_flat.shape[1]))).reshape(N_DEV, FLAT_R, FLAT_C)
    parts = _exchange(send_flat, "scatter_grads")
    g_big, d_big, m_big, v_big = _adamw(parts, flat_w, _flat_shard(shard_m), _flat_shard(shard_v), "adamw_sharded")

    small_g = {k: jnp.stack([grads[l][k] for l in range(DEPTH)]) for k in ("norm_g", "b_fg", "b_merge", "a_log", "dt_bias", "gdn_norm_g", "mem_norm_g")}
    small_g["final_norm_g"] = dfg[0]
    small_g["extra"] = lsum[0, 0]
    parts_s = _all_gather(_pack_small(small_g), "gather_small")
    g_sm, d_sm, m_sm, v_sm = _adamw(parts_s, _pack_small(small_w), _pack_small(small_m), _pack_small(small_v), "adamw_replicated")

    big = [_unflat_shard(a) for a in (g_big, d_big, m_big, v_big)]
    sml = [_unpack_small(a) for a in (g_sm, d_sm, m_sm, v_sm)]
    loss = sml[0]["extra"]
    names = ("norm_g", "w_in", "b_fg", "b_merge", "conv_w", "a_log", "dt_bias", "gdn_norm_g", "mem_norm_g", "w_mem_kv", "w_branch", "w_out", "final_norm_g")
    outs = [loss, grad_x]
    for kind in range(4):
        for n in names:
            outs.append(big[kind][n] if n in big[kind] else sml[kind][n])
    return tuple(outs)
```

```python
import functools

import jax
import jax.numpy as jnp
from jax import lax
from jax.experimental import pallas as pl
from jax.experimental.pallas import tpu as pltpu

f32, bf16 = jnp.float32, jnp.bfloat16

D = 1024
EPS = 1e-6
CH = 64
N_DEV = 8
DEPTH = 2
FOX_SCALE = 64 ** -0.5
GDN_SCALE = 128 ** -0.5
MEM_SCALE = 128 ** -0.5
NEG = -1e30
VMEM_LIMIT = 56 * 1024 * 1024

ADAM_LR, ADAM_B1, ADAM_B2, ADAM_EPS, ADAM_WD, ADAM_STEP = 0.001, 0.9, 0.999, 1e-08, 0.01, 10

_COLS = dict(aq=(0, 512), ak=(512, 1024), av=(1024, 1536), af=(1536, 1544), az=(1544, 2056),
             bq=(2056, 2568), bk=(2568, 3080), bv=(3080, 3592), ba=(3592, 3596), bb=(3596, 3600),
             bz=(3600, 4112), mq=(4112, 4624), mz=(4624, 5136), gates=(5136, 8208))
_ORDER = ("aq", "ak", "av", "mq", "bq", "bk", "bv", "az", "bz", "mz", "gates", "af", "ba", "bb")
N_IN = 8208
NB, NF, NS = 2048, 6144, 128
N_ALL = NB + NF + NS

_SHARDED = ("w_in", "w_mem_kv", "w_branch", "w_out", "conv_w")


def _cp(sem=None, vmem=None):
    kw = {}
    if sem is not None:
        kw["dimension_semantics"] = sem
    if vmem is not None:
        kw["vmem_limit_bytes"] = vmem
    return pltpu.CompilerParams(**kw)


def _dot(a, b):
    return jnp.dot(a, b, preferred_element_type=f32)


def _dot_nt(a, b):
    return lax.dot_general(a, b, (((1,), (1,)), ((), ())), preferred_element_type=f32)


def _dot_tn(a, b):
    return lax.dot_general(a, b, (((0,), (0,)), ((), ())), preferred_element_type=f32)


def _split2(x):
    hi = x.astype(bf16)
    return hi, (x - hi.astype(f32)).astype(bf16)


def _mm3(a, b, dims):
    ah, al = _split2(a)
    bh, bl = _split2(b)
    dg = functools.partial(lax.dot_general, dimension_numbers=dims, preferred_element_type=f32)
    return dg(ah, bh) + (dg(ah, bl) + dg(al, bh))


def _hi(a, b):
    return _mm3(a, b, (((1,), (0,)), ((), ())))


def _hi_nt(a, b):
    return _mm3(a, b, (((1,), (1,)), ((), ())))


def _hi_tn(a, b):
    return _mm3(a, b, (((0,), (0,)), ((), ())))


def _hi_b(a, b):
    return _mm3(a, b, (((2,), (1,)), ((0,), (0,))))


def _b(x):
    return x.astype(bf16)


def _sig(x):
    return jax.nn.sigmoid(x)


def _silu(x):
    return x * _sig(x)


def _dsilu(x):
    s = _sig(x)
    return s * (1.0 + x * (1.0 - s))


def _softplus(x):
    return jnp.maximum(x, 0.0) + jnp.log1p(jnp.exp(-jnp.abs(x)))


def _rowsum(x):
    return jnp.sum(x, axis=1, keepdims=True)


def _colsum(x):
    return jnp.sum(x, axis=0, keepdims=True)


def _norm_fwd(x, g, name):
    M = x.shape[0]
    ts = min(M, 512)

    def body(x_ref, g_ref, h_ref):
        xv = x_ref[...]
        r = lax.rsqrt(jnp.mean(xv * xv, axis=-1, keepdims=True) + EPS)
        h_ref[...] = _b(xv * r * g_ref[...])

    return pl.pallas_call(
        body, grid=(M // ts,), out_shape=jax.ShapeDtypeStruct((M, D), bf16),
        in_specs=[pl.BlockSpec((ts, D), lambda i: (i, 0)), pl.BlockSpec((1, D), lambda i: (0, 0))],
        out_specs=pl.BlockSpec((ts, D), lambda i: (i, 0)), compiler_params=_cp(("parallel",)), name=name)(x, g)


def _norm_bwd(x, g, dh, dres, name):
    M = x.shape[0]
    ts = min(M, 512)
    with_dx = dres is not None

    def body(*refs):
        if with_dx:
            x_ref, g_ref, dh_ref, dres_ref, dx_ref, dg_ref = refs
        else:
            x_ref, g_ref, dh_ref, dg_ref = refs
        i = pl.program_id(0)
        xv = x_ref[...]
        r = lax.rsqrt(jnp.mean(xv * xv, axis=-1, keepdims=True) + EPS)
        xh = xv * r
        dh = dh_ref[...].astype(f32)
        part = jnp.broadcast_to(_colsum(dh * xh), (8, D))

        @pl.when(i == 0)
        def _():
            dg_ref[...] = part

        @pl.when(i > 0)
        def _():
            dg_ref[...] += part

        if with_dx:
            dxh = dh * g_ref[...]
            dx_ref[...] = dres_ref[...] + r * (dxh - xh * jnp.mean(dxh * xh, axis=-1, keepdims=True))

    tile = pl.BlockSpec((ts, D), lambda i: (i, 0))
    gspec = pl.BlockSpec((1, D), lambda i: (0, 0))
    acc = pl.BlockSpec((8, D), lambda i: (0, 0))
    if with_dx:
        return pl.pallas_call(
            body, grid=(M // ts,), out_shape=(jax.ShapeDtypeStruct((M, D), f32), jax.ShapeDtypeStruct((8, D), f32)),
            in_specs=[tile, gspec, tile, tile], out_specs=(tile, acc), compiler_params=_cp(("arbitrary",)), name=name)(x, g, dh, dres)
    return pl.pallas_call(
        body, grid=(M // ts,), out_shape=jax.ShapeDtypeStruct((8, D), f32),
        in_specs=[tile, gspec, tile], out_specs=acc, compiler_params=_cp(("arbitrary",)), name=name)(x, g, dh)


def _loss_head(x, g, tgt, name):
    M = x.shape[0]
    ts = min(M, 512)

    def body(x_ref, g_ref, t_ref, dx_ref, dg_ref, ls_ref):
        i = pl.program_id(0)
        xv = x_ref[...]
        gv = g_ref[...]
        r = lax.rsqrt(jnp.mean(xv * xv, axis=-1, keepdims=True) + EPS)
        xh = xv * r
        e = xh * gv - t_ref[...]
        lpart = 0.5 * jnp.sum(jnp.mean(e * e, axis=-1, keepdims=True), axis=0, keepdims=True)
        dy = e * (1.0 / D)
        dgp = jnp.broadcast_to(_colsum(dy * xh), (8, D))
        lp = jnp.broadcast_to(lpart, (8, 128))

        @pl.when(i == 0)
        def _():
            dg_ref[...] = dgp
            ls_ref[...] = lp

        @pl.when(i > 0)
        def _():
            dg_ref[...] += dgp
            ls_ref[...] += lp

        dxh = dy * gv
        dx_ref[...] = r * (dxh - xh * jnp.mean(dxh * xh, axis=-1, keepdims=True))

    tile = pl.BlockSpec((ts, D), lambda i: (i, 0))
    return pl.pallas_call(
        body, grid=(M // ts,),
        out_shape=(jax.ShapeDtypeStruct((M, D), f32), jax.ShapeDtypeStruct((8, D), f32), jax.ShapeDtypeStruct((8, 128), f32)),
        in_specs=[tile, pl.BlockSpec((1, D), lambda i: (0, 0)), tile],
        out_specs=(tile, pl.BlockSpec((8, D), lambda i: (0, 0)), pl.BlockSpec((8, 128), lambda i: (0, 0))),
        compiler_params=_cp(("arbitrary",)), name=name)(x, g, tgt)


def _mm(a, b, out_dtype, tm, tn, tk, name, trans_a=False):
    if trans_a:
        K, M = a.shape
    else:
        M, K = a.shape
    N = b.shape[1]
    tm, tn, tk = min(tm, M), min(tn, N), min(tk, K)
    nk = K // tk

    def body(a_ref, b_ref, o_ref, acc_ref):
        k = pl.program_id(2)
        av, bv = _b(a_ref[...]), _b(b_ref[...])
        part = _dot_tn(av, bv) if trans_a else _dot(av, bv)
        if nk == 1:
            o_ref[...] = part.astype(out_dtype)
        else:
            @pl.when(k == 0)
            def _():
                acc_ref[...] = part

            @pl.when(k > 0)
            def _():
                acc_ref[...] += part

            @pl.when(k == nk - 1)
            def _():
                o_ref[...] = acc_ref[...].astype(out_dtype)

    a_spec = pl.BlockSpec((tk, tm), lambda i, j, k: (k, i)) if trans_a else pl.BlockSpec((tm, tk), lambda i, j, k: (i, k))
    return pl.pallas_call(
        body, grid=(M // tm, N // tn, nk), out_shape=jax.ShapeDtypeStruct((M, N), out_dtype),
        in_specs=[a_spec, pl.BlockSpec((tk, tn), lambda i, j, k: (k, j))],
        out_specs=pl.BlockSpec((tm, tn), lambda i, j, k: (i, j)),
        scratch_shapes=[pltpu.VMEM((tm, tn), f32)],
        compiler_params=_cp(("parallel", "parallel", "arbitrary"), VMEM_LIMIT), name=name)(a, b)


def _small_pars(b_fg, a_log, dt_bias):
    par = jnp.zeros((8, 128), f32)
    par = par.at[0, 0:8].set(b_fg).at[1, 8:12].set(a_log).at[2, 8:12].set(dt_bias)
    return par


def _small_prep(zs, par, name):
    S = zs.shape[0]
    ts = min(S, 512)

    def body(z_ref, par_ref, o_ref, carry_ref):
        i = pl.program_id(0)

        @pl.when(i == 0)
        def _():
            carry_ref[...] = jnp.zeros_like(carry_ref)

        z = z_ref[...]
        lane = lax.broadcasted_iota(jnp.int32, (ts, 128), 1)
        row = lax.broadcasted_iota(jnp.int32, (ts, 128), 0)
        za = z + par_ref[0:1, :]
        logf = jnp.minimum(za, 0.0) - jnp.log1p(jnp.exp(-jnp.abs(za)))
        glog = -jnp.exp(par_ref[1:2, :]) * _softplus(z + par_ref[2:3, :])
        x = jnp.where(lane < 8, logf, jnp.where(lane < 12, glog, 0.0))
        pos = jnp.where(lane < 8, row, row & (CH - 1))
        s = 1
        while s < ts:
            x = x + jnp.where(pos >= s, pltpu.roll(x, s, 0), 0.0)
            s *= 2
        tot = x + carry_ref[0:1, :]
        carry_ref[...] = jnp.broadcast_to(jnp.where(lane[0:1] < 8, tot[ts - 1:ts, :], 0.0), (8, 128))
        o_ref[...] = jnp.where(lane < 8, tot, jnp.where(lane < 12, x, jnp.where(lane < 16, _sig(z), 0.0)))

    return pl.pallas_call(
        body, grid=(S // ts,), out_shape=jax.ShapeDtypeStruct((S, 128), f32),
        in_specs=[pl.BlockSpec((ts, 128), lambda i: (i, 0)), pl.BlockSpec((8, 128), lambda i: (0, 0))],
        out_specs=pl.BlockSpec((ts, 128), lambda i: (i, 0)), scratch_shapes=[pltpu.VMEM((8, 128), f32)],
        compiler_params=_cp(("arbitrary",)), name=name)(zs, par)


def _small_bwd(zs, par, dfr, dfc, dsm, name):
    S = zs.shape[0]
    ts = min(S, 512)
    nt = S // ts

    def body(z_ref, par_ref, dfr_ref, dfc_ref, dsm_ref, dz_ref, acc_ref, carry_ref):
        i = pl.program_id(0)

        @pl.when(i == 0)
        def _():
            carry_ref[...] = jnp.zeros_like(carry_ref)

        z = z_ref[...]
        dsm_v = dsm_ref[...]
        lane = lax.broadcasted_iota(jnp.int32, (ts, 128), 1)
        row = lax.broadcasted_iota(jnp.int32, (ts, 128), 0)
        x = jnp.where(lane < 8, dfr_ref[...] - dfc_ref[...], jnp.where(lane < 12, dsm_v, 0.0))
        pos = jnp.where(lane < 8, row, row & (CH - 1))
        seg = jnp.where(lane < 8, ts, CH)
        s = 1
        while s < ts:
            x = x + jnp.where(pos + s < seg, pltpu.roll(x, ts - s, 0), 0.0)
            s *= 2
        tot = x + carry_ref[0:1, :]
        carry_ref[...] = jnp.broadcast_to(jnp.where(lane[0:1] < 8, tot[0:1, :], 0.0), (8, 128))
        za = z + par_ref[0:1, :]
        daf = tot * _sig(-za)
        zb = z + par_ref[2:3, :]
        nea = -jnp.exp(par_ref[1:2, :])
        glog = nea * _softplus(zb)
        dba = x * nea * _sig(zb)
        beta = _sig(z)
        dbb = dsm_v * beta * (1.0 - beta)
        dz_ref[...] = _b(jnp.where(lane < 8, daf, jnp.where(lane < 12, dba, jnp.where(lane < 16, dbb, 0.0))))
        r0 = _colsum(jnp.where(lane < 8, daf, 0.0))
        r1 = _colsum(jnp.where((lane >= 8) & (lane < 12), x * glog, 0.0))
        r2 = _colsum(jnp.where((lane >= 8) & (lane < 12), dba, 0.0))
        r8 = lax.broadcasted_iota(jnp.int32, (8, 128), 0)
        part = jnp.where(r8 == 0, r0, jnp.where(r8 == 1, r1, jnp.where(r8 == 2, r2, 0.0)))

        @pl.when(i == 0)
        def _():
            acc_ref[...] = part

        @pl.when(i > 0)
        def _():
            acc_ref[...] += part

    rev = pl.BlockSpec((ts, 128), lambda i: (nt - 1 - i, 0))
    c8 = pl.BlockSpec((8, 128), lambda i: (0, 0))
    return pl.pallas_call(
        body, grid=(nt,), out_shape=(jax.ShapeDtypeStruct((S, 128), bf16), jax.ShapeDtypeStruct((8, 128), f32)),
        in_specs=[rev, c8, rev, rev, rev], out_specs=(rev, c8), scratch_shapes=[pltpu.VMEM((8, 128), f32)],
        compiler_params=_cp(("arbitrary",)), name=name)(zs, par, dfr, dfc, dsm)


def _split3(x):
    hi = _b(x).astype(f32)
    r = x - hi
    mid = _b(r).astype(f32)
    return hi, mid, _b(r - mid).astype(f32)


def _fox_prep(zb, sm, name):
    S = zb.shape[0]
    ts = min(S, 512)

    def body(q_ref, k_ref, f_ref, qa_ref, ka_ref):
        lane = lax.broadcasted_iota(jnp.int32, (ts, 128), 1)
        f = f_ref[...]
        for p in range(4):
            q = q_ref[:, p * 128:(p + 1) * 128].astype(f32) * FOX_SCALE
            k = k_ref[:, p * 128:(p + 1) * 128].astype(f32)
            for h in (0, 1):
                hi, mid, lo = _split3(f[:, 2 * p + h:2 * p + h + 1])
                own = (lane < 64) if h == 0 else (lane >= 64)
                o = 64 if h == 0 else 0
                ones_lo = (lane >= o) & (lane < o + 3)
                ones_hi = (lane >= o + 3) & (lane < o + 6)
                qaug = jnp.where(lane == o, hi, jnp.where(lane == o + 1, mid, jnp.where(lane == o + 2, lo, jnp.where(ones_hi, 1.0, 0.0))))
                kaug = jnp.where(lane == o + 3, -hi, jnp.where(lane == o + 4, -mid, jnp.where(lane == o + 5, -lo, jnp.where(ones_lo, 1.0, 0.0))))
                qa_ref[2 * p + h] = _b(jnp.where(own, q, qaug))
                ka_ref[2 * p + h] = _b(jnp.where(own, k, kaug))

    out = jax.ShapeDtypeStruct((8, S, 128), bf16)
    return pl.pallas_call(
        body, grid=(S // ts,), out_shape=(out, out),
        in_specs=[pl.BlockSpec((ts, 512), lambda i: (i, 0)), pl.BlockSpec((ts, 512), lambda i: (i, 1)), pl.BlockSpec((ts, 128), lambda i: (i, 0))],
        out_specs=(pl.BlockSpec((8, ts, 128), lambda i: (0, i, 0)), pl.BlockSpec((8, ts, 128), lambda i: (0, i, 0))),
        compiler_params=_cp(("parallel",)), name=name)(zb, zb, sm)


def _fox_fwd(qa, ka, zb, name):
    S = zb.shape[0]
    T = min(S, 1024)

    def body(qa_ref, ka_ref, v_ref, o_ref, lse_ref, m_ref, l_ref, acc_ref):
        i = pl.program_id(1)
        m_ref[...] = jnp.full_like(m_ref, NEG)
        l_ref[...] = jnp.zeros_like(l_ref)
        acc_ref[...] = jnp.zeros_like(acc_ref)
        row = lax.broadcasted_iota(jnp.int32, (T, T), 0)
        col = lax.broadcasted_iota(jnp.int32, (T, T), 1)

        def tile(j, masked):
            off = pl.multiple_of(j * T, T)
            vs = v_ref[pl.ds(off, T), :]
            for h in (0, 1):
                s = _dot_nt(qa_ref[h], ka_ref[h, pl.ds(off, T), :])
                if masked:
                    s = jnp.where(row >= col, s, NEG)
                m_old = m_ref[h]
                m_new = jnp.maximum(m_old, jnp.max(s, axis=1, keepdims=True))
                alpha = jnp.exp(m_old - m_new)
                pr = jnp.exp(s - jnp.tile(m_new, (1, T // 128)))
                l_ref[h] = alpha * l_ref[h] + _rowsum(pr)
                acc_ref[h] = alpha * acc_ref[h] + _dot(_b(pr), vs)
                m_ref[h] = m_new

        def step(j, c):
            tile(j, False)
            return c

        lax.fori_loop(0, i, step, 0)
        tile(i, True)
        lane2 = lax.broadcasted_iota(jnp.int32, (T, 128), 1)
        o_ref[...] = jnp.where(lane2 < 64, acc_ref[0] / l_ref[0], acc_ref[1] / l_ref[1])
        lse_ref[0] = jnp.where(lane2 < 64, m_ref[0] + jnp.log(l_ref[0]), m_ref[1] + jnp.log(l_ref[1]))

    return pl.pallas_call(
        body, grid=(4, S // T),
        out_shape=(jax.ShapeDtypeStruct((S, 512), f32), jax.ShapeDtypeStruct((4, S, 128), f32)),
        in_specs=[pl.BlockSpec((2, T, 128), lambda p, i: (p, i, 0)), pl.BlockSpec((2, S, 128), lambda p, i: (p, 0, 0)),
                  pl.BlockSpec((S, 128), lambda p, i: (0, 8 + p))],
        out_specs=(pl.BlockSpec((T, 128), lambda p, i: (i, p)), pl.BlockSpec((1, T, 128), lambda p, i: (p, i, 0))),
        scratch_shapes=[pltpu.VMEM((2, T, 128), f32), pltpu.VMEM((2, T, 128), f32), pltpu.VMEM((2, T, 128), f32)],
        compiler_params=_cp(("arbitrary", "arbitrary"), VMEM_LIMIT), name=name)(qa, ka, zb)


def _fox_bwd_dq(qa, ka, zb, lse, o, do, name, tile_rows=1024):
    S = zb.shape[0]
    T = min(S, tile_rows)

    def body(qa_ref, ka_ref, v_ref, lse_ref, o_ref, do_ref, dq_ref, dl_ref, dfr_ref, acc_ref, fr_ref):
        i = pl.program_id(1)
        lane1 = lax.broadcasted_iota(jnp.int32, (1, 128), 1)
        lane2 = lax.broadcasted_iota(jnp.int32, (T, 128), 1)
        hm = (lane1 < 64, lane1 >= 64)
        lse_v = lse_ref[0]
        lcol = [lse_v[:, 0:1], lse_v[:, 64:65]]
        do = do_ref[...]
        prod = do * o_ref[...]
        dcol = [_rowsum(jnp.where(lane2 < 64, prod, 0.0)), _rowsum(jnp.where(lane2 >= 64, prod, 0.0))]
        dob = _b(do)
        dos = [jnp.where(hm[h], dob, jnp.zeros_like(dob)) for h in (0, 1)]
        acc_ref[...] = jnp.zeros_like(acc_ref)
        fr_ref[...] = jnp.zeros_like(fr_ref)
        row = lax.broadcasted_iota(jnp.int32, (T, T), 0)
        col = lax.broadcasted_iota(jnp.int32, (T, T), 1)

        def tile(j, masked):
            off = pl.multiple_of(j * T, T)
            vs = v_ref[pl.ds(off, T), :]
            for h in (0, 1):
                ks = ka_ref[h, pl.ds(off, T), :]
                s = _dot_nt(qa_ref[h], ks)
                if masked:
                    s = jnp.where(row >= col, s, NEG)
                pr = jnp.exp(s - lcol[h])
                dp = _dot_nt(dos[h], vs)
                ds = pr * (dp - dcol[h])
                acc_ref[h] += _dot(_b(ds), ks)
                fr_ref[h] += _rowsum(ds)

        def step(j, c):
            tile(j, False)
            return c

        lax.fori_loop(0, i, step, 0)
        tile(i, True)
        dq_ref[...] = _b(jnp.where(lane2 < 64, acc_ref[0], acc_ref[1]) * FOX_SCALE)
        dl_ref[0] = jnp.where(lane2 < 64, dcol[0], dcol[1])
        dfr_ref[0] = jnp.where(lane2 < 64, fr_ref[0], fr_ref[1])

    tq = pl.BlockSpec((T, 128), lambda p, i: (i, p))
    pair = pl.BlockSpec((1, T, 128), lambda p, i: (p, i, 0))
    return pl.pallas_call(
        body, grid=(4, S // T),
        out_shape=(jax.ShapeDtypeStruct((S, 512), bf16), jax.ShapeDtypeStruct((4, S, 128), f32), jax.ShapeDtypeStruct((4, S, 128), f32)),
        in_specs=[pl.BlockSpec((2, T, 128), lambda p, i: (p, i, 0)), pl.BlockSpec((2, S, 128), lambda p, i: (p, 0, 0)),
                  pl.BlockSpec((S, 128), lambda p, i: (0, 8 + p)), pair, tq, tq],
        out_specs=(tq, pair, pair),
        scratch_shapes=[pltpu.VMEM((2, T, 128), f32), pltpu.VMEM((2, T, 1), f32)],
        compiler_params=_cp(("arbitrary", "arbitrary"), VMEM_LIMIT), name=name)(qa, ka, zb, lse, o, do)


def _fox_bwd_dkv(qa, ka, zb, dob, lse_t, dl_t, name, tile_rows=1024):
    S = zb.shape[0]
    T = min(S, tile_rows)
    nq = S // T

    def body(ka_ref, v_ref, qa_ref, do_ref, lt_ref, dt_ref, dk_ref, dv_ref, dfc_ref, dka_ref, dva_ref, fs_ref):
        p, j = pl.program_id(0), pl.program_id(1)
        lane1 = lax.broadcasted_iota(jnp.int32, (1, 128), 1)
        lane2 = lax.broadcasted_iota(jnp.int32, (T, 128), 1)
        hm = (lane1 < 64, lane1 >= 64)
        v = v_ref[...]
        vsm = [jnp.where(hm[h], v, jnp.zeros_like(v)) for h in (0, 1)]
        dka_ref[...] = jnp.zeros_like(dka_ref)
        dva_ref[...] = jnp.zeros_like(dva_ref)
        fs_ref[...] = jnp.zeros_like(fs_ref)
        row = lax.broadcasted_iota(jnp.int32, (T, T), 0)
        col = lax.broadcasted_iota(jnp.int32, (T, T), 1)

        def tile(i, masked):
            off = pl.multiple_of(i * T, T)
            dot_ = do_ref[pl.ds(off, T), :]
            for h in (0, 1):
                hr = pl.ds(2 * p + h, 1)
                qt = qa_ref[h, pl.ds(off, T), :]
                s_t = _dot_nt(ka_ref[h], qt)
                if masked:
                    s_t = jnp.where(col >= row, s_t, NEG)
                p_t = jnp.exp(s_t - lt_ref[hr, pl.ds(off, T)])
                dva_ref[h] += _dot(_b(p_t), dot_)
                dp_t = _dot_nt(vsm[h], dot_)
                ds_t = p_t * (dp_t - dt_ref[hr, pl.ds(off, T)])
                dka_ref[h] += _dot(_b(ds_t), qt)
                fs_ref[h] += _rowsum(ds_t)

        def step(i, c):
            tile(i, False)
            return c

        tile(j, True)
        lax.fori_loop(j + 1, nq, step, 0)
        dk_ref[...] = _b(jnp.where(lane2 < 64, dka_ref[0], dka_ref[1]))
        dv_ref[...] = _b(jnp.where(lane2 < 64, dva_ref[0], dva_ref[1]))
        dfc_ref[0] = jnp.where(lane2 < 64, fs_ref[0], fs_ref[1])

    res = pl.BlockSpec((8, S), lambda p, j: (0, 0))
    tk = pl.BlockSpec((T, 128), lambda p, j: (j, p))
    return pl.pallas_call(
        body, grid=(4, nq),
        out_shape=(jax.ShapeDtypeStruct((S, 512), bf16), jax.ShapeDtypeStruct((S, 512), bf16), jax.ShapeDtypeStruct((4, S, 128), f32)),
        in_specs=[pl.BlockSpec((2, T, 128), lambda p, j: (p, j, 0)), pl.BlockSpec((T, 128), lambda p, j: (j, 8 + p)),
                  pl.BlockSpec((2, S, 128), lambda p, j: (p, 0, 0)), pl.BlockSpec((S, 128), lambda p, j: (0, p)), res, res],
        out_specs=(tk, tk, pl.BlockSpec((1, T, 128), lambda p, j: (p, j, 0))),
        scratch_shapes=[pltpu.VMEM((2, T, 128), f32), pltpu.VMEM((2, T, 128), f32), pltpu.VMEM((2, T, 1), f32)],
        compiler_params=_cp(("arbitrary", "arbitrary"), VMEM_LIMIT), name=name)(ka, zb, qa, dob, lse_t, dl_t)


def _pair_to_rows(a):
    S = a.shape[1]
    return jnp.transpose(a[:, :, 0::64], (0, 2, 1)).reshape(8, S)


def _pair_to_cols(a):
    S = a.shape[1]
    c = jnp.transpose(a[:, :, 0::64], (1, 0, 2)).reshape(S, 8)
    return jnp.pad(c, ((0, 0), (0, 120)))


def _conv_taps(ext, x, w_ref, ts):
    y = x * w_ref[3:4, :]
    shifted = []
    for k in (1, 2, 3):
        xs = pltpu.roll(ext, k, 0)[8:]
        shifted.append(xs)
        y = y + xs * w_ref[3 - k:4 - k, :]
    return y, shifted


def _gdn_prep(zf, cw, name):
    S = zf.shape[0]
    ts = min(S, 512)

    def body(x_ref, w_ref, o_ref, tail_ref):
        i = pl.program_id(0)

        @pl.when(i == 0)
        def _():
            tail_ref[...] = jnp.zeros_like(tail_ref)

        x = x_ref[...]
        ext = jnp.concatenate([tail_ref[...], x], axis=0)
        y, _ = _conv_taps(ext, x, w_ref, ts)
        tail_ref[...] = x[ts - 8:, :]
        a = _silu(y)
        for hb in range(12):
            blk = a[:, hb * 128:(hb + 1) * 128]
            if hb < 8:
                blk = blk * lax.rsqrt(_rowsum(blk * blk) + EPS)
            if hb < 4:
                blk = blk * GDN_SCALE
            o_ref[:, hb * 128:(hb + 1) * 128] = blk

    return pl.pallas_call(
        body, grid=(S // ts,), out_shape=jax.ShapeDtypeStruct((S, 1536), f32),
        in_specs=[pl.BlockSpec((ts, 1536), lambda i: (i, 0)), pl.BlockSpec((8, 1536), lambda i: (0, 0))],
        out_specs=pl.BlockSpec((ts, 1536), lambda i: (i, 0)), scratch_shapes=[pltpu.VMEM((8, 1536), f32)],
        compiler_params=_cp(("arbitrary",), VMEM_LIMIT), name=name)(zf, cw)


def _gdn_prep_bwd(zf, cw, dg, name):
    S = zf.shape[0]
    ts = min(S, 512)
    nt = S // ts

    def body(x_ref, xp_ref, w_ref, dg_ref, dx_ref, dw_ref, head_ref):
        i = pl.program_id(0)

        @pl.when(i == 0)
        def _():
            head_ref[...] = jnp.zeros_like(head_ref)

        x = x_ref[...]
        prev = jnp.where(i == nt - 1, 0.0, xp_ref[...])
        ext = jnp.concatenate([prev, x], axis=0)
        y, shifted = _conv_taps(ext, x, w_ref, ts)
        a = _silu(y)
        das = []
        for hb in range(12):
            blk = a[:, hb * 128:(hb + 1) * 128]
            d = dg_ref[:, hb * 128:(hb + 1) * 128]
            if hb < 4:
                d = d * GDN_SCALE
            if hb < 8:
                r = lax.rsqrt(_rowsum(blk * blk) + EPS)
                n = blk * r
                d = r * (d - n * _rowsum(d * n))
            das.append(d)
        dy = jnp.concatenate(das, axis=1) * _dsilu(y)
        extd = jnp.concatenate([dy, head_ref[...]], axis=0)
        dx = dy * w_ref[3:4, :]
        for k in (1, 2, 3):
            dx = dx + pltpu.roll(extd, ts + 8 - k, 0)[:ts] * w_ref[3 - k:4 - k, :]
        head_ref[...] = dy[0:8, :]
        dx_ref[...] = _b(dx)
        r8 = lax.broadcasted_iota(jnp.int32, (8, 1536), 0)
        part = jnp.where(r8 == 3, _colsum(dy * x), 0.0)
        for k in (1, 2, 3):
            part = jnp.where(r8 == 3 - k, _colsum(dy * shifted[k - 1]), part)

        @pl.when(i == 0)
        def _():
            dw_ref[...] = part

        @pl.when(i > 0)
        def _():
            dw_ref[...] += part

    rev = pl.BlockSpec((ts, 1536), lambda i: (nt - 1 - i, 0))
    prev8 = pl.BlockSpec((8, 1536), lambda i: (jnp.maximum((nt - 1 - i) * (ts // 8) - 1, 0), 0))
    w8 = pl.BlockSpec((8, 1536), lambda i: (0, 0))
    return pl.pallas_call(
        body, grid=(nt,), out_shape=(jax.ShapeDtypeStruct((S, 1536), bf16), jax.ShapeDtypeStruct((8, 1536), f32)),
        in_specs=[rev, prev8, w8, rev], out_specs=(rev, w8), scratch_shapes=[pltpu.VMEM((8, 1536), f32)],
        compiler_params=_cp(("arbitrary",), VMEM_LIMIT), name=name)(zf, zf, cw, dg)


def _tri_inv(a, row, col):
    same = (row >> 4) == (col >> 4)
    dm = jnp.where(same, a, 0.0)
    lo = a - dm
    eye = jnp.where(row == col, 1.0, 0.0)
    d2 = _hi_b(dm, dm)
    d4 = _hi_b(d2, d2)
    d8 = _hi_b(d4, d4)
    x0 = _hi_b(_hi_b(eye - dm, eye + d2), _hi_b(eye + d4, eye + d8))
    n = _hi_b(x0, lo)
    n2 = _hi_b(n, n)
    return _hi_b(_hi_b(eye - n, eye + n2), x0)


def _gdn_local(x_ref, sm_ref, gt_ref, h, row, col):
    q = x_ref[:, h * 128:(h + 1) * 128]
    k = x_ref[:, 512 + h * 128:512 + (h + 1) * 128]
    v = x_ref[:, 1024 + h * 128:1024 + (h + 1) * 128]
    gc = sm_ref[:, 8 + h:9 + h]
    beta = sm_ref[:, 12 + h:13 + h]
    gr = gt_ref[h, 0]
    eg = jnp.exp(gc)
    gl = gc[CH - 1:CH, :]
    dec = jnp.exp(gl - gc)
    gm = gc - gr
    gam_i = jnp.exp(jnp.where(row >= col, gm, -jnp.inf))
    gam_s = jnp.where(row > col, gam_i, 0.0)
    kb = k * beta
    return dict(q=q, k=k, v=v, beta=beta, eg=eg, egl=jnp.exp(gl), dec=dec, gam_i=gam_i, gam_s=gam_s,
                kb=kb, vb=v * beta, kbg=kb * eg, qdec=q * eg, kdec=k * dec,
                a=_dot_nt(_b(kb), _b(k)) * gam_s, aqk=_dot_nt(_b(q), _b(k)) * gam_i)


def _gdn_fwd(gqkv, sm, gt4, name):
    S = gqkv.shape[0]
    N = S // CH

    def body(x_ref, sm_ref, gt_ref, o_ref, t_ref, st_ref, s_ref):
        n = pl.program_id(0)

        @pl.when(n == 0)
        def _():
            s_ref[...] = jnp.zeros_like(s_ref)

        row = lax.broadcasted_iota(jnp.int32, (CH, CH), 0)
        col = lax.broadcasted_iota(jnp.int32, (CH, CH), 1)
        cs = [_gdn_local(x_ref, sm_ref, gt_ref, h, row, col) for h in range(4)]
        t_all = _tri_inv(jnp.stack([c["a"] for c in cs]), row, col)
        t_ref[...] = t_all
        for h in range(4):
            c = cs[h]
            uw = _hi(t_all[h], jnp.concatenate([c["vb"], c["kbg"]], axis=1))
            u, w = uw[:, :128], uw[:, 128:]
            st = s_ref[h]
            st_ref[0, h] = st
            sb = _b(st)
            vnew = u - _dot(_b(w), sb)
            o_ref[:, h * 128:(h + 1) * 128] = _dot(_b(c["qdec"]), sb) + _dot(_b(c["aqk"]), _b(vnew))
            s_ref[h] = st * c["egl"] + _dot_tn(_b(c["kdec"]), _b(vnew))

    return pl.pallas_call(
        body, grid=(N,),
        out_shape=(jax.ShapeDtypeStruct((S, 512), f32), jax.ShapeDtypeStruct((4, S, CH), f32), jax.ShapeDtypeStruct((N, 4, 128, 128), f32)),
        in_specs=[pl.BlockSpec((CH, 1536), lambda n: (n, 0)), pl.BlockSpec((CH, 128), lambda n: (n, 0)),
                  pl.BlockSpec((4, 1, 1, CH), lambda n: (0, n, 0, 0))],
        out_specs=(pl.BlockSpec((CH, 512), lambda n: (n, 0)), pl.BlockSpec((4, CH, CH), lambda n: (0, n, 0)),
                   pl.BlockSpec((1, 4, 128, 128), lambda n: (n, 0, 0, 0))),
        scratch_shapes=[pltpu.VMEM((4, 128, 128), f32)], compiler_params=_cp(("arbitrary",)), name=name)(gqkv, sm, gt4)


def _gdn_bwd(gqkv, sm, gt4, tinv, states, do, name):
    S = gqkv.shape[0]
    N = S // CH

    def body(x_ref, sm_ref, gt_ref, t_ref, st_ref, do_ref, dx_ref, dsm_ref, ds_ref):
        n = pl.program_id(0)

        @pl.when(n == 0)
        def _():
            ds_ref[...] = jnp.zeros_like(ds_ref)

        row = lax.broadcasted_iota(jnp.int32, (CH, CH), 0)
        col = lax.broadcasted_iota(jnp.int32, (CH, CH), 1)
        row1 = lax.broadcasted_iota(jnp.int32, (CH, 1), 0)
        lane = lax.broadcasted_iota(jnp.int32, (CH, 128), 1)
        ones = jnp.ones((CH, 128), f32)
        dsm = jnp.zeros((CH, 128), f32)
        for h in range(4):
            c = _gdn_local(x_ref, sm_ref, gt_ref, h, row, col)
            q, k, v, beta, eg = c["q"], c["k"], c["v"], c["beta"], c["eg"]
            t = t_ref[h]
            uw = _hi(t, jnp.concatenate([c["vb"], c["kbg"]], axis=1))
            u, w = uw[:, :128], uw[:, 128:]
            st = st_ref[0, h]
            sb = _b(st)
            vnew = u - _dot(_b(w), sb)
            dob = _b(do_ref[:, h * 128:(h + 1) * 128])
            dsp = ds_ref[h]
            dspb = _b(dsp)
            vnb = _b(vnew)
            dvnew = _dot(_b(c["kdec"]), dspb) + _dot_tn(_b(c["aqk"]), dob)
            dkdec = _dot_nt(vnb, dspb)
            dgl = c["egl"] * jnp.sum(dsp * st, keepdims=True)
            dqdec = _dot_nt(dob, sb)
            daqk = jnp.where(row >= col, _dot_nt(dob, vnb), 0.0)
            dvnb = _b(dvnew)
            dw = -_dot_nt(dvnb, sb)
            ds_ref[h] = dsp * c["egl"] + _dot_tn(_b(c["qdec"]), dob) - _dot_tn(_b(w), dvnb)
            duw = _hi_tn(t, jnp.concatenate([dvnew, dw], axis=1))
            dvb, dkbg = duw[:, :128], duw[:, 128:]
            da = -jnp.where(row > col, _hi_nt(duw, uw), 0.0)
            dp = da * c["gam_s"]
            dqk = daqk * c["gam_i"]
            m = da * c["a"] + daqk * c["aqk"]
            csum = _hi_tn(m, ones)[:, 0:1]
            kk = dkdec * c["kdec"]
            dgv = _rowsum(m) - csum + _rowsum(dqdec * c["qdec"]) - _rowsum(kk) + _rowsum(dkbg * c["kbg"])
            dgv = dgv + jnp.where(row1 == CH - 1, dgl + jnp.sum(kk, keepdims=True), 0.0)
            dpb, dqkb = _b(dp), _b(dqk)
            dkb = _dot(dpb, _b(k)) + dkbg * eg
            dk = _dot_tn(dpb, _b(c["kb"])) + _dot_tn(dqkb, _b(q)) + dkdec * c["dec"] + dkb * beta
            dq = _dot(dqkb, _b(k)) + dqdec * eg
            dbeta = _rowsum(dkb * k) + _rowsum(dvb * v)
            dx_ref[:, h * 128:(h + 1) * 128] = dq
            dx_ref[:, 512 + h * 128:512 + (h + 1) * 128] = dk
            dx_ref[:, 1024 + h * 128:1024 + (h + 1) * 128] = dvb * beta
            dsm = jnp.where(lane == 8 + h, dgv, jnp.where(lane == 12 + h, dbeta, dsm))
        dsm_ref[...] = dsm

    return pl.pallas_call(
        body, grid=(N,), out_shape=(jax.ShapeDtypeStruct((S, 1536), f32), jax.ShapeDtypeStruct((S, 128), f32)),
        in_specs=[pl.BlockSpec((CH, 1536), lambda n: (N - 1 - n, 0)), pl.BlockSpec((CH, 128), lambda n: (N - 1 - n, 0)),
                  pl.BlockSpec((4, 1, 1, CH), lambda n: (0, N - 1 - n, 0, 0)), pl.BlockSpec((4, CH, CH), lambda n: (0, N - 1 - n, 0)),
                  pl.BlockSpec((1, 4, 128, 128), lambda n: (N - 1 - n, 0, 0, 0)), pl.BlockSpec((CH, 512), lambda n: (N - 1 - n, 0))],
        out_specs=(pl.BlockSpec((CH, 1536), lambda n: (N - 1 - n, 0)), pl.BlockSpec((CH, 128), lambda n: (N - 1 - n, 0))),
        scratch_shapes=[pltpu.VMEM((4, 128, 128), f32)], compiler_params=_cp(("arbitrary",)), name=name)(gqkv, sm, gt4, tinv, states, do)


def _mem_attn(q, kv_ref, h):
    s = _dot_nt(q, kv_ref[:, h * 128:(h + 1) * 128]) * MEM_SCALE
    e = jnp.exp(s - jnp.max(s, axis=1, keepdims=True))
    return e / _rowsum(e)


def _gdn_out_norm(ob):
    r = lax.rsqrt(jnp.mean(ob * ob, axis=-1, keepdims=True) + EPS)
    return ob * r, r


def _merge_fwd(x, oa, ob, zb, zf, kv, b_merge, gdn_g, w_branch, w_out, name):
    S = x.shape[0]
    ts = min(S, 256)

    def body(x_ref, oa_ref, ob_ref, mq_ref, az_ref, bz_ref, mz_ref, gt_ref, kv_ref, bm_ref, gg_ref, wb_ref, wo_ref,
             xo_ref, y_ref, mg_ref):
        y_ref[:, 0:512] = _b(oa_ref[...] * _silu(az_ref[...]))
        for h in range(4):
            sl = slice(h * 128, (h + 1) * 128)
            nb, _ = _gdn_out_norm(ob_ref[:, sl])
            y_ref[:, 512 + h * 128:512 + (h + 1) * 128] = _b(nb * gg_ref[...] * _silu(bz_ref[:, sl]))
            pm = _mem_attn(mq_ref[:, sl], kv_ref, h)
            om = _dot(_b(pm), kv_ref[:, 512 + h * 128:512 + (h + 1) * 128])
            y_ref[:, 1024 + h * 128:1024 + (h + 1) * 128] = _b(om * _silu(mz_ref[:, sl]))
        merged = jnp.zeros((ts, D), f32)
        for n in range(3):
            gate = _sig(gt_ref[:, n * D:(n + 1) * D] + bm_ref[:, n * D:(n + 1) * D])
            merged = merged + gate * _dot(y_ref[:, n * 512:(n + 1) * 512], wb_ref[n])
        mb = _b(merged)
        mg_ref[...] = mb
        xo_ref[...] = x_ref[...] + _dot(mb, wo_ref[...])

    def col(w, c):
        return pl.BlockSpec((ts, w), lambda i: (i, c))

    def full(shape):
        return pl.BlockSpec(shape, lambda i: tuple(0 for _ in shape))

    return pl.pallas_call(
        body, grid=(S // ts,),
        out_shape=(jax.ShapeDtypeStruct((S, D), f32), jax.ShapeDtypeStruct((S, 1536), bf16), jax.ShapeDtypeStruct((S, D), bf16)),
        in_specs=[col(D, 0), col(512, 0), col(512, 0), col(512, 3), col(512, 3), col(512, 4), col(512, 5), col(3072, 1),
                  full((256, D)), full((1, 3072)), full((1, 128)), full((3, 512, D)), full((D, D))],
        out_specs=(col(D, 0), col(1536, 0), col(D, 0)),
        compiler_params=_cp(("parallel",), VMEM_LIMIT), name=name)(x, oa, ob, zb, zf, zf, zf, zf, kv, b_merge, gdn_g, w_branch, w_out)


def _merge_bwd(dout, ycat, oa, ob, zb, zf, kv, b_merge, gdn_g, w_branch, w_branch_t, w_out_t, name):
    S = dout.shape[0]
    ts = min(S, 256)

    def body(do_ref, y_ref, oa_ref, ob_ref, mq_ref, az_ref, bz_ref, mz_ref, gt_ref, kv_ref, bm_ref, gg_ref, wb_ref, wbt_ref, wot_ref,
             dpj_ref, dz_ref, dmq_ref, doa_ref, doab_ref, dob_ref, dkv_ref, dbm_ref, dgg_ref):
        i = pl.program_id(0)
        dmerged = _dot(_b(do_ref[...]), wot_ref[...])
        dys = []
        dbm_parts = []
        for n in range(3):
            cs = slice(n * D, (n + 1) * D)
            gate = _sig(gt_ref[:, cs] + bm_ref[:, cs])
            proj = _dot(y_ref[:, n * 512:(n + 1) * 512], wb_ref[n])
            dlogit = dmerged * proj * gate * (1.0 - gate)
            dz_ref[:, 1536 + n * D:1536 + (n + 1) * D] = _b(dlogit)
            dbm_parts.append(_colsum(dlogit))
            dproj = _b(dmerged * gate)
            dpj_ref[:, cs] = dproj
            dys.append(_dot(dproj, wbt_ref[n]))
        dbm = jnp.broadcast_to(jnp.concatenate(dbm_parts, axis=1), (8, 3072))
        az = az_ref[...]
        oa = oa_ref[...]
        doa = dys[0] * _silu(az)
        doa_ref[...] = doa
        doab_ref[...] = _b(doa)
        dz_ref[:, 0:512] = _b(dys[0] * oa * _dsilu(az))
        gg = gg_ref[...]
        dgg = jnp.zeros((1, 128), f32)
        dkv_parts_k, dkv_parts_v = [], []
        for h in range(4):
            sl = slice(h * 128, (h + 1) * 128)
            bz = bz_ref[:, sl]
            dyb = dys[1][:, sl]
            nb, r = _gdn_out_norm(ob_ref[:, sl])
            dz_ref[:, 512 + h * 128:512 + (h + 1) * 128] = _b(dyb * nb * gg * _dsilu(bz))
            dng = dyb * _silu(bz)
            dgg = dgg + _colsum(dng * nb)
            dnb = dng * gg
            dob_ref[:, sl] = r * (dnb - nb * jnp.mean(dnb * nb, axis=-1, keepdims=True))
            mz = mz_ref[:, sl]
            dym = dys[2][:, sl]
            q = mq_ref[:, sl]
            kh = kv_ref[:, sl]
            vh = kv_ref[:, 512 + h * 128:512 + (h + 1) * 128]
            pm = _mem_attn(q, kv_ref, h)
            pmb = _b(pm)
            om = _dot(pmb, vh)
            dz_ref[:, 1024 + h * 128:1024 + (h + 1) * 128] = _b(dym * om * _dsilu(mz))
            dom = _b(dym * _silu(mz))
            dkv_parts_v.append(_dot_tn(pmb, dom))
            dpm = _dot_nt(dom, vh)
            dsm = _b(pm * (dpm - _rowsum(dpm * pm)) * MEM_SCALE)
            dmq_ref[:, sl] = _b(_dot(dsm, kh))
            dkv_parts_k.append(_dot_tn(dsm, q))
        dkv = jnp.concatenate(dkv_parts_k + dkv_parts_v, axis=1)
        dggb = jnp.broadcast_to(dgg, (8, 128))

        @pl.when(i == 0)
        def _():
            dkv_ref[...] = dkv
            dbm_ref[...] = dbm
            dgg_ref[...] = dggb

        @pl.when(i > 0)
        def _():
            dkv_ref[...] += dkv
            dbm_ref[...] += dbm
            dgg_ref[...] += dggb

    def col(w, c):
        return pl.BlockSpec((ts, w), lambda i: (i, c))

    def full(shape):
        return pl.BlockSpec(shape, lambda i: tuple(0 for _ in shape))

    return pl.pallas_call(
        body, grid=(S // ts,),
        out_shape=(jax.ShapeDtypeStruct((S, 3072), bf16), jax.ShapeDtypeStruct((S, 4608), bf16), jax.ShapeDtypeStruct((S, 512), bf16),
                   jax.ShapeDtypeStruct((S, 512), f32), jax.ShapeDtypeStruct((S, 512), bf16), jax.ShapeDtypeStruct((S, 512), f32),
                   jax.ShapeDtypeStruct((256, D), f32), jax.ShapeDtypeStruct((8, 3072), f32), jax.ShapeDtypeStruct((8, 128), f32)),
        in_specs=[col(D, 0), col(1536, 0), col(512, 0), col(512, 0), col(512, 3), col(512, 3), col(512, 4), col(512, 5), col(3072, 1),
                  full((256, D)), full((1, 3072)), full((1, 128)), full((3, 512, D)), full((3, D, 512)), full((D, D))],
        out_specs=(col(3072, 0), col(4608, 0), col(512, 0), col(512, 0), col(512, 0), col(512, 0),
                   full((256, D)), full((8, 3072)), full((8, 128))),
        compiler_params=_cp(("arbitrary",), VMEM_LIMIT), name=name)(
            dout, ycat, oa, ob, zb, zf, zf, zf, zf, kv, b_merge, gdn_g, w_branch, w_branch_t, w_out_t)


def _mesh_pos():
    return lax.axis_index("x"), lax.axis_index("y"), lax.axis_index("c")


def _all_gather(xs, name):
    n = len(xs)

    def body(*refs):
        x_refs, out_refs = refs[:n], refs[n:2 * n]
        send_sems, recv_sems, local_sems = refs[2 * n:]
        mx, my, mc = _mesh_pos()
        me, sibling = (mx, my, mc), (mx, my, 1 - mc)
        chips = [(1 - mx, my), (mx, 1 - my), (1 - mx, 1 - my)]

        def copy(a, k, block, to, src=None):
            px, py, pc = block
            slot = out_refs[a].at[4 * px + 2 * py + pc]
            return pltpu.make_async_remote_copy(
                src_ref=slot if src is None else src, dst_ref=slot,
                send_sem=send_sems.at[7 * a + k], recv_sem=recv_sems.at[7 * a + k], device_id=to, device_id_type=pl.DeviceIdType.MESH)

        mine = [pltpu.make_async_copy(x_refs[a], out_refs[a].at[4 * mx + 2 * my + mc], local_sems.at[a]) for a in range(n)]
        for cp in mine:
            cp.start()
        first = []
        for a in range(n):
            first.append(copy(a, 0, me, sibling, src=x_refs[a]))
            first += [copy(a, 1 + j, me, (*chip, mc), src=x_refs[a]) for j, chip in enumerate(chips)]
        for cp in first:
            cp.start()
        passed = []
        for j, chip in enumerate(chips):
            for a in range(n):
                copy(a, 1 + j, (*chip, mc), me).wait_recv()
                fwd = copy(a, 4 + j, (*chip, mc), sibling)
                fwd.start()
                passed.append(fwd)
        for a in range(n):
            copy(a, 0, sibling, me).wait_recv()
            for j, chip in enumerate(chips):
                copy(a, 4 + j, (*chip, 1 - mc), me).wait_recv()
        for cp in first + passed:
            cp.wait_send()
        for cp in mine:
            cp.wait()

    anyspec = pl.BlockSpec(memory_space=pl.ANY)
    return pl.pallas_call(
        body, out_shape=tuple(jax.ShapeDtypeStruct((N_DEV,) + x.shape, x.dtype) for x in xs),
        in_specs=[anyspec] * n, out_specs=tuple([anyspec] * n),
        scratch_shapes=[pltpu.SemaphoreType.DMA((7 * n,)), pltpu.SemaphoreType.DMA((7 * n,)), pltpu.SemaphoreType.DMA((n,))],
        name=name)(*xs)


def _exchange(sends, name):
    n = len(sends)

    def body(*refs):
        s_refs, r_refs = refs[:n], refs[n:2 * n]
        send_sems, recv_sems, local_sems = refs[2 * n:]
        mx, my, mc = _mesh_pos()
        me_id = 4 * mx + 2 * my + mc
        mine = [pltpu.make_async_copy(s_refs[a].at[me_id], r_refs[a].at[me_id], local_sems.at[a]) for a in range(n)]
        for cp in mine:
            cp.start()
        copies = []
        for k in range(1, N_DEV):
            px = 1 - mx if k & 4 else mx
            py = 1 - my if k & 2 else my
            pc = 1 - mc if k & 1 else mc
            for a in range(n):
                copies.append(pltpu.make_async_remote_copy(
                    src_ref=s_refs[a].at[4 * px + 2 * py + pc], dst_ref=r_refs[a].at[me_id],
                    send_sem=send_sems.at[7 * a + k - 1], recv_sem=recv_sems.at[7 * a + k - 1],
                    device_id=(px, py, pc), device_id_type=pl.DeviceIdType.MESH))
        for cp in copies:
            cp.start()
        for cp in copies:
            cp.wait()
        for cp in mine:
            cp.wait()

    anyspec = pl.BlockSpec(memory_space=pl.ANY)
    return pl.pallas_call(
        body, out_shape=tuple(jax.ShapeDtypeStruct(s.shape, s.dtype) for s in sends),
        in_specs=[anyspec] * n, out_specs=tuple([anyspec] * n),
        scratch_shapes=[pltpu.SemaphoreType.DMA((7 * n,)), pltpu.SemaphoreType.DMA((7 * n,)), pltpu.SemaphoreType.DMA((n,))],
        name=name)(*sends)


ADAMW_BLOCK_BYTES = 4 * 1024 * 1024


def _adamw(parts, w, m, v, name):
    _, R, C = parts.shape
    tr = R
    for t in (1024, 512, 256, 128, 64, 32, 16, 8):
        if R % t == 0 and N_DEV * t * C * 4 <= ADAMW_BLOCK_BYTES:
            tr = t
            break

    def body(p_ref, w_ref, m_ref, v_ref, g_ref, d_ref, nm_ref, nv_ref):
        g = p_ref[0]
        for j in range(1, N_DEV):
            g = g + p_ref[j]
        mn = ADAM_B1 * m_ref[...] + (1.0 - ADAM_B1) * g
        vn = ADAM_B2 * v_ref[...] + (1.0 - ADAM_B2) * jnp.square(g)
        m_hat = mn / (1.0 - ADAM_B1 ** ADAM_STEP)
        v_hat = vn / (1.0 - ADAM_B2 ** ADAM_STEP)
        g_ref[...] = g
        d_ref[...] = -ADAM_LR * (m_hat / (jnp.sqrt(v_hat) + ADAM_EPS) + ADAM_WD * w_ref[...])
        nm_ref[...] = mn
        nv_ref[...] = vn

    t2 = pl.BlockSpec((tr, C), lambda i: (i, 0))
    out = jax.ShapeDtypeStruct((R, C), f32)
    return pl.pallas_call(
        body, grid=(R // tr,), out_shape=(out, out, out, out),
        in_specs=[pl.BlockSpec((N_DEV, tr, C), lambda i: (0, i, 0)), t2, t2, t2], out_specs=(t2, t2, t2, t2),
        compiler_params=_cp(("parallel",), VMEM_LIMIT), name=name)(parts, w, m, v)


def _as2d(a):
    return a.reshape(-1, a.shape[-1])


def _perm_cols(w):
    parts = [w[..., _COLS[n][0]:_COLS[n][1]] for n in _ORDER]
    pad = jnp.zeros(w.shape[:-1] + (N_ALL - N_IN,), w.dtype)
    return jnp.concatenate(parts + [pad], axis=-1)


def _unperm_cols(w):
    pieces, off = {}, 0
    for n in _ORDER:
        width = _COLS[n][1] - _COLS[n][0]
        pieces[n] = w[..., off:off + width]
        off += width
    return jnp.concatenate([pieces[n] for n in sorted(_COLS, key=lambda n: _COLS[n][0])], axis=-1)


_SMALL_ROWS = 16


def _pack_small(t):
    z = jnp.zeros((D,), f32)
    misc = z.at[0:16].set(t["b_fg"].reshape(-1)).at[16:24].set(t["a_log"].reshape(-1)).at[24:32].set(t["dt_bias"].reshape(-1))
    misc = misc.at[128:384].set(t["gdn_norm_g"].reshape(-1))
    if "extra" in t:
        misc = misc.at[512].set(t["extra"])
    rows = [t["norm_g"], t["b_merge"].reshape(6, D), t["mem_norm_g"], t["final_norm_g"][None], misc[None],
            jnp.zeros((_SMALL_ROWS - 12, D), f32)]
    return jnp.concatenate(rows, axis=0)


def _unpack_small(a):
    misc = a[11]
    return dict(norm_g=a[0:2], b_merge=a[2:8].reshape(2, 3072), mem_norm_g=a[8:10], final_norm_g=a[10],
                b_fg=misc[0:16].reshape(2, 8), a_log=misc[16:24].reshape(2, 4), dt_bias=misc[24:32].reshape(2, 4),
                gdn_norm_g=misc[128:384].reshape(2, 128), extra=misc[512])


def _layer_fwd(l, x, mem, p):
    sfx = f"_l{l}"
    h = _norm_fwd(x, p["norm_g"], "norm_fwd" + sfx)
    zb = _mm(h, p["w_b"], bf16, 512, 1024, 1024, "inproj_b" + sfx)
    zf = _mm(h, p["w_f"], f32, 512, 1024, 1024, "inproj_f" + sfx)
    zs = _mm(h, p["w_s"], f32, 512, 128, 1024, "inproj_s" + sfx)
    sm = _small_prep(zs, p["par"], "small_prep" + sfx)
    S = x.shape[0]
    gt4 = jnp.transpose(sm[:, 8:12]).reshape(4, S // CH, 1, CH)
    qa, ka = _fox_prep(zb, sm, "fox_prep" + sfx)
    oa, lse = _fox_fwd(qa, ka, zb, "fox_fwd" + sfx)
    gqkv = _gdn_prep(zf, p["conv_w"], "gdn_prep" + sfx)
    ob, tinv, states = _gdn_fwd(gqkv, sm, gt4, "gdn_fwd" + sfx)
    memn = _norm_fwd(mem, p["mem_norm_g"], "mem_norm" + sfx)
    kv = _mm(memn, p["w_mem_kv"], bf16, 256, 1024, 1024, "mem_kv" + sfx)
    xo, ycat, merged = _merge_fwd(x, oa, ob, zb, zf, kv, p["b_merge"], p["gdn_norm_g"], p["w_branch"], p["w_out"], "merge_fwd" + sfx)
    saved = dict(x=x, h=h, zb=zb, zf=zf, zs=zs, sm=sm, qa=qa, ka=ka, gt4=gt4, oa=oa, lse=lse, gqkv=gqkv, ob=ob, tinv=tinv,
                 states=states, memn=memn, kv=kv, ycat=ycat, merged=merged)
    return xo, saved


def _layer_bwd(l, dout, mem, p, s):
    sfx = f"_l{l}"
    dproj, dzf2, dmq, doa, doab, dob, dkv, dbm, dgg = _merge_bwd(
        dout, s["ycat"], s["oa"], s["ob"], s["zb"], s["zf"], s["kv"], p["b_merge"], p["gdn_norm_g"],
        p["w_branch"], p["w_branch_t"], p["w_out_t"], "merge_bwd" + sfx)
    g = {}
    g["w_out"] = _mm(s["merged"], dout, f32, 512, 1024, 512, "dw_out" + sfx, trans_a=True)
    g["w_branch"] = jnp.stack([
        _mm(s["ycat"][:, n * 512:(n + 1) * 512], dproj[:, n * D:(n + 1) * D], f32, 512, 1024, 512, f"dw_branch{n}" + sfx, trans_a=True)
        for n in range(3)])
    g["b_merge"] = dbm[0]
    g["gdn_norm_g"] = dgg[0]
    g["w_mem_kv"] = _mm(s["memn"], dkv, f32, 512, 1024, 256, "dw_mem_kv" + sfx, trans_a=True)
    dmemn = _mm(dkv, p["w_mem_kv_t"], f32, 256, 1024, 1024, "dmem_n" + sfx)
    g["mem_norm_g"] = _norm_bwd(mem, p["mem_norm_g"], dmemn, None, "mem_norm_bwd" + sfx)[0]
    dgqkv, dsm = _gdn_bwd(s["gqkv"], s["sm"], s["gt4"], s["tinv"], s["states"], dob, "gdn_bwd" + sfx)
    dbqkv, dcw = _gdn_prep_bwd(s["zf"], p["conv_w"], dgqkv, "gdn_prep_bwd" + sfx)
    g["conv_w"] = dcw[0:4]
    dq, delta, dfr = _fox_bwd_dq(s["qa"], s["ka"], s["zb"], s["lse"], s["oa"], doa, "fox_bwd_dq" + sfx)
    dk, dv, dfc = _fox_bwd_dkv(s["qa"], s["ka"], s["zb"], doab, _pair_to_rows(s["lse"]), _pair_to_rows(delta), "fox_bwd_dkv" + sfx)
    dzs, sacc = _small_bwd(s["zs"], p["par"], _pair_to_cols(dfr), _pair_to_cols(dfc), dsm, "small_bwd" + sfx)
    g["b_fg"], g["a_log"], g["dt_bias"] = sacc[0, 0:8], sacc[1, 8:12], sacc[2, 8:12]
    dz = jnp.concatenate([dq, dk, dv, dmq, dbqkv, dzf2, dzs], axis=1)
    dh = _mm(dz, p["w_all_t"], f32, 512, 1024, 1664, "dh" + sfx)
    g["w_in"] = _mm(s["h"], dz, f32, 512, 1664, 512, "dw_in" + sfx, trans_a=True)
    dx, dng = _norm_bwd(s["x"], p["norm_g"], dh, dout, "norm_bwd" + sfx)
    g["norm_g"] = dng[0]
    return dx, g


def kernel(x, mem, norm_g, w_in, b_fg, b_merge, conv_w, a_log, dt_bias, gdn_norm_g, mem_norm_g, w_mem_kv, w_branch, w_out, final_norm_g, loss_target, m_norm_g, m_w_in, m_b_fg, m_b_merge, m_conv_w, m_a_log, m_dt_bias, m_gdn_norm_g, m_mem_norm_g, m_w_mem_kv, m_w_branch, m_w_out, m_final_norm_g, v_norm_g, v_w_in, v_b_fg, v_b_merge, v_conv_w, v_a_log, v_dt_bias, v_gdn_norm_g, v_mem_norm_g, v_w_mem_kv, v_w_branch, v_w_out, v_final_norm_g):
    x0, mem0, tgt = x[0], mem[0], loss_target[0]
    shard_w = dict(w_in=w_in, w_mem_kv=w_mem_kv, w_branch=w_branch, w_out=w_out, conv_w=conv_w)
    shard_m = dict(w_in=m_w_in, w_mem_kv=m_w_mem_kv, w_branch=m_w_branch, w_out=m_w_out, conv_w=m_conv_w)
    shard_v = dict(w_in=v_w_in, w_mem_kv=v_w_mem_kv, w_branch=v_w_branch, w_out=v_w_out, conv_w=v_conv_w)
    small_w = dict(norm_g=norm_g, b_fg=b_fg, b_merge=b_merge, a_log=a_log, dt_bias=dt_bias, gdn_norm_g=gdn_norm_g,
                   mem_norm_g=mem_norm_g, final_norm_g=final_norm_g)
    small_m = dict(norm_g=m_norm_g, b_fg=m_b_fg, b_merge=m_b_merge, a_log=m_a_log, dt_bias=m_dt_bias, gdn_norm_g=m_gdn_norm_g,
                   mem_norm_g=m_mem_norm_g, final_norm_g=m_final_norm_g)
    small_v = dict(norm_g=v_norm_g, b_fg=v_b_fg, b_merge=v_b_merge, a_log=v_a_log, dt_bias=v_dt_bias, gdn_norm_g=v_gdn_norm_g,
                   mem_norm_g=v_mem_norm_g, final_norm_g=v_final_norm_g)

    g_in, g_kv, g_br, g_out, conv_all = _all_gather(
        [_b(_as2d(w_in)), _b(_as2d(w_mem_kv)), _b(_as2d(w_branch)), _b(_as2d(w_out)), _as2d(conv_w)], "gather_weights")
    conv_full = jnp.transpose(conv_all.reshape(N_DEV, DEPTH, 4, 192), (1, 2, 0, 3)).reshape(DEPTH, 4, 1536)
    w_in_full = jnp.transpose(g_in.reshape(N_DEV, DEPTH, D, 1026), (1, 2, 0, 3)).reshape(DEPTH, D, N_IN)
    w_all = _perm_cols(w_in_full)
    w_kv_full = jnp.transpose(g_kv.reshape(N_DEV, DEPTH, 128, D), (1, 0, 2, 3)).reshape(DEPTH, D, D)
    w_br_full = jnp.transpose(g_br.reshape(N_DEV, DEPTH, 3, 512, 128), (1, 2, 3, 0, 4)).reshape(DEPTH, 3, 512, D)
    w_out_full = jnp.transpose(g_out.reshape(N_DEV, DEPTH, 128, D), (1, 0, 2, 3)).reshape(DEPTH, D, D)

    layers = []
    for l in range(DEPTH):
        layers.append(dict(
            norm_g=norm_g[l][None], mem_norm_g=mem_norm_g[l][None], gdn_norm_g=gdn_norm_g[l][None], b_merge=b_merge[l][None],
            par=_small_pars(b_fg[l], a_log[l], dt_bias[l]),
            conv_w=jnp.pad(conv_full[l], ((0, 4), (0, 0))),
            w_b=w_all[l][:, 0:NB], w_f=w_all[l][:, NB:NB + NF], w_s=w_all[l][:, NB + NF:], w_all_t=jnp.transpose(w_all[l]),
            w_mem_kv=w_kv_full[l], w_mem_kv_t=jnp.transpose(w_kv_full[l]),
            w_branch=w_br_full[l], w_branch_t=jnp.transpose(w_br_full[l], (0, 2, 1)),
            w_out=w_out_full[l], w_out_t=jnp.transpose(w_out_full[l])))

    acts, saved = x0, []
    for l in range(DEPTH):
        acts, s = _layer_fwd(l, acts, mem0, layers[l])
        saved.append(s)
    dx, dfg, lsum = _loss_head(acts, final_norm_g[None], tgt, "loss_head")

    grads = [None] * DEPTH
    for l in reversed(range(DEPTH)):
        dx, grads[l] = _layer_bwd(l, dx, mem0, layers[l], saved[l])
    grad_x = dx[None]

    def per_dev(name):
        return jnp.stack([grads[l][name] for l in range(DEPTH)])

    dw_in = _unperm_cols(per_dev("w_in"))
    send = dict(
        w_in=jnp.transpose(dw_in.reshape(DEPTH, D, N_DEV, 1026), (2, 0, 1, 3)).reshape(N_DEV, DEPTH * D, 1026),
        w_mem_kv=jnp.transpose(per_dev("w_mem_kv").reshape(DEPTH, N_DEV, 128, D), (1, 0, 2, 3)).reshape(N_DEV, DEPTH * 128, D),
        w_branch=jnp.transpose(per_dev("w_branch").reshape(DEPTH, 3, 512, N_DEV, 128), (3, 0, 1, 2, 4)).reshape(N_DEV, DEPTH * 3 * 512, 128),
        w_out=jnp.transpose(per_dev("w_out").reshape(DEPTH, N_DEV, 128, D), (1, 0, 2, 3)).reshape(N_DEV, DEPTH * 128, D),
        conv_w=jnp.transpose(per_dev("conv_w").reshape(DEPTH, 4, N_DEV, 192), (2, 0, 1, 3)).reshape(N_DEV, DEPTH * 4, 192))
    parts = dict(zip(_SHARDED, _exchange([send[n] for n in _SHARDED], "scatter_grads")))
    big = [{}, {}, {}, {}]
    for n in _SHARDED:
        res = _adamw(parts[n], _as2d(shard_w[n]), _as2d(shard_m[n]), _as2d(shard_v[n]), "adamw_" + n)
        for kind in range(4):
            big[kind][n] = res[kind].reshape(shard_w[n].shape)

    small_g = {k: jnp.stack([grads[l][k] for l in range(DEPTH)]) for k in ("norm_g", "b_fg", "b_merge", "a_log", "dt_bias", "gdn_norm_g", "mem_norm_g")}
    small_g["final_norm_g"] = dfg[0]
    small_g["extra"] = lsum[0, 0]
    parts_s, = _all_gather([_pack_small(small_g)], "gather_small")
    g_sm, d_sm, m_sm, v_sm = _adamw(parts_s, _pack_small(small_w), _pack_small(small_m), _pack_small(small_v), "adamw_replicated")

    sml = [_unpack_small(a) for a in (g_sm, d_sm, m_sm, v_sm)]
    loss = sml[0]["extra"]
    names = ("norm_g", "w_in", "b_fg", "b_merge", "conv_w", "a_log", "dt_bias", "gdn_norm_g", "mem_norm_g", "w_mem_kv", "w_branch", "w_out", "final_norm_g")
    outs = [loss, grad_x]
    for kind in range(4):
        for n in names:
            outs.append(big[kind][n] if n in big[kind] else sml[kind][n])
    return tuple(outs)
```

```python
import functools

import jax
import jax.numpy as jnp
from jax import lax
from jax.experimental import pallas as pl
from jax.experimental.pallas import tpu as pltpu

f32, bf16 = jnp.float32, jnp.bfloat16

D = 1024
EPS = 1e-6
CH = 64
N_DEV = 8
DEPTH = 2
FOX_SCALE = 64 ** -0.5
GDN_SCALE = 128 ** -0.5
MEM_SCALE = 128 ** -0.5
NEG = -1e30
VMEM_LIMIT = 56 * 1024 * 1024

ADAM_LR, ADAM_B1, ADAM_B2, ADAM_EPS, ADAM_WD, ADAM_STEP = 0.001, 0.9, 0.999, 1e-08, 0.01, 10

_COLS = dict(aq=(0, 512), ak=(512, 1024), av=(1024, 1536), af=(1536, 1544), az=(1544, 2056),
             bq=(2056, 2568), bk=(2568, 3080), bv=(3080, 3592), ba=(3592, 3596), bb=(3596, 3600),
             bz=(3600, 4112), mq=(4112, 4624), mz=(4624, 5136), gates=(5136, 8208))
_ORDER = ("aq", "ak", "av", "mq", "bq", "bk", "bv", "az", "bz", "mz", "gates", "af", "ba", "bb")
N_IN = 8208
NB, NF, NS = 2048, 6144, 128
N_ALL = NB + NF + NS

_SHARDED = ("w_in", "w_mem_kv", "w_branch", "w_out", "conv_w")


def _cp(sem=None, vmem=None):
    kw = {}
    if sem is not None:
        kw["dimension_semantics"] = sem
    if vmem is not None:
        kw["vmem_limit_bytes"] = vmem
    return pltpu.CompilerParams(**kw)


def _dot(a, b):
    return jnp.dot(a, b, preferred_element_type=f32)


def _dot_nt(a, b):
    return lax.dot_general(a, b, (((1,), (1,)), ((), ())), preferred_element_type=f32)


def _dot_tn(a, b):
    return lax.dot_general(a, b, (((0,), (0,)), ((), ())), preferred_element_type=f32)


def _split2(x):
    hi = x.astype(bf16)
    return hi, (x - hi.astype(f32)).astype(bf16)


def _mm3(a, b, dims):
    ah, al = _split2(a)
    bh, bl = _split2(b)
    dg = functools.partial(lax.dot_general, dimension_numbers=dims, preferred_element_type=f32)
    return dg(ah, bh) + (dg(ah, bl) + dg(al, bh))


def _hi(a, b):
    return _mm3(a, b, (((1,), (0,)), ((), ())))


def _hi_nt(a, b):
    return _mm3(a, b, (((1,), (1,)), ((), ())))


def _hi_tn(a, b):
    return _mm3(a, b, (((0,), (0,)), ((), ())))


def _hi_b(a, b):
    return _mm3(a, b, (((2,), (1,)), ((0,), (0,))))


def _b(x):
    return x.astype(bf16)


def _sig(x):
    return jax.nn.sigmoid(x)


def _silu(x):
    return x * _sig(x)


def _dsilu(x):
    s = _sig(x)
    return s * (1.0 + x * (1.0 - s))


def _softplus(x):
    return jnp.maximum(x, 0.0) + jnp.log1p(jnp.exp(-jnp.abs(x)))


def _rowsum(x):
    return jnp.sum(x, axis=1, keepdims=True)


def _colsum(x):
    return jnp.sum(x, axis=0, keepdims=True)


def _norm_fwd(x, g, name):
    M = x.shape[0]
    ts = min(M, 512)

    def body(x_ref, g_ref, h_ref):
        xv = x_ref[...]
        r = lax.rsqrt(jnp.mean(xv * xv, axis=-1, keepdims=True) + EPS)
        h_ref[...] = _b(xv * r * g_ref[...])

    return pl.pallas_call(
        body, grid=(M // ts,), out_shape=jax.ShapeDtypeStruct((M, D), bf16),
        in_specs=[pl.BlockSpec((ts, D), lambda i: (i, 0)), pl.BlockSpec((1, D), lambda i: (0, 0))],
        out_specs=pl.BlockSpec((ts, D), lambda i: (i, 0)), compiler_params=_cp(("parallel",)), name=name)(x, g)


def _norm_bwd(x, g, dh, dres, name):
    M = x.shape[0]
    ts = min(M, 512)
    with_dx = dres is not None

    def body(*refs):
        if with_dx:
            x_ref, g_ref, dh_ref, dres_ref, dx_ref, dg_ref = refs
        else:
            x_ref, g_ref, dh_ref, dg_ref = refs
        i = pl.program_id(0)
        xv = x_ref[...]
        r = lax.rsqrt(jnp.mean(xv * xv, axis=-1, keepdims=True) + EPS)
        xh = xv * r
        dh = dh_ref[...].astype(f32)
        part = jnp.broadcast_to(_colsum(dh * xh), (8, D))

        @pl.when(i == 0)
        def _():
            dg_ref[...] = part

        @pl.when(i > 0)
        def _():
            dg_ref[...] += part

        if with_dx:
            dxh = dh * g_ref[...]
            dx_ref[...] = dres_ref[...] + r * (dxh - xh * jnp.mean(dxh * xh, axis=-1, keepdims=True))

    tile = pl.BlockSpec((ts, D), lambda i: (i, 0))
    gspec = pl.BlockSpec((1, D), lambda i: (0, 0))
    acc = pl.BlockSpec((8, D), lambda i: (0, 0))
    if with_dx:
        return pl.pallas_call(
            body, grid=(M // ts,), out_shape=(jax.ShapeDtypeStruct((M, D), f32), jax.ShapeDtypeStruct((8, D), f32)),
            in_specs=[tile, gspec, tile, tile], out_specs=(tile, acc), compiler_params=_cp(("arbitrary",)), name=name)(x, g, dh, dres)
    return pl.pallas_call(
        body, grid=(M // ts,), out_shape=jax.ShapeDtypeStruct((8, D), f32),
        in_specs=[tile, gspec, tile], out_specs=acc, compiler_params=_cp(("arbitrary",)), name=name)(x, g, dh)


def _loss_head(x, g, tgt, name):
    M = x.shape[0]
    ts = min(M, 512)

    def body(x_ref, g_ref, t_ref, dx_ref, dg_ref, ls_ref):
        i = pl.program_id(0)
        xv = x_ref[...]
        gv = g_ref[...]
        r = lax.rsqrt(jnp.mean(xv * xv, axis=-1, keepdims=True) + EPS)
        xh = xv * r
        e = xh * gv - t_ref[...]
        lpart = 0.5 * jnp.sum(jnp.mean(e * e, axis=-1, keepdims=True), axis=0, keepdims=True)
        dy = e * (1.0 / D)
        dgp = jnp.broadcast_to(_colsum(dy * xh), (8, D))
        lp = jnp.broadcast_to(lpart, (8, 128))

        @pl.when(i == 0)
        def _():
            dg_ref[...] = dgp
            ls_ref[...] = lp

        @pl.when(i > 0)
        def _():
            dg_ref[...] += dgp
            ls_ref[...] += lp

        dxh = dy * gv
        dx_ref[...] = r * (dxh - xh * jnp.mean(dxh * xh, axis=-1, keepdims=True))

    tile = pl.BlockSpec((ts, D), lambda i: (i, 0))
    return pl.pallas_call(
        body, grid=(M // ts,),
        out_shape=(jax.ShapeDtypeStruct((M, D), f32), jax.ShapeDtypeStruct((8, D), f32), jax.ShapeDtypeStruct((8, 128), f32)),
        in_specs=[tile, pl.BlockSpec((1, D), lambda i: (0, 0)), tile],
        out_specs=(tile, pl.BlockSpec((8, D), lambda i: (0, 0)), pl.BlockSpec((8, 128), lambda i: (0, 0))),
        compiler_params=_cp(("arbitrary",)), name=name)(x, g, tgt)


def _mm(a, b, out_dtype, tm, tn, tk, name, trans_a=False, a_cols=None, b_cols=None):
    if trans_a:
        K, M = a.shape
    else:
        M, K = a.shape
    N = b.shape[1]
    a0, b0 = 0, 0
    if a_cols is not None:
        a0, M = a_cols
    if b_cols is not None:
        b0, N = b_cols
    tm, tn, tk = min(tm, M), min(tn, N), min(tk, K)
    nk = K // tk
    a0, b0 = a0 // tm, b0 // tn

    def body(a_ref, b_ref, o_ref, acc_ref):
        k = pl.program_id(2)
        av, bv = _b(a_ref[...]), _b(b_ref[...])
        part = _dot_tn(av, bv) if trans_a else _dot(av, bv)
        if nk == 1:
            o_ref[...] = part.astype(out_dtype)
        else:
            @pl.when(k == 0)
            def _():
                acc_ref[...] = part

            @pl.when(k > 0)
            def _():
                acc_ref[...] += part

            @pl.when(k == nk - 1)
            def _():
                o_ref[...] = acc_ref[...].astype(out_dtype)

    a_spec = pl.BlockSpec((tk, tm), lambda i, j, k: (k, i + a0)) if trans_a else pl.BlockSpec((tm, tk), lambda i, j, k: (i, k))
    return pl.pallas_call(
        body, grid=(M // tm, N // tn, nk), out_shape=jax.ShapeDtypeStruct((M, N), out_dtype),
        in_specs=[a_spec, pl.BlockSpec((tk, tn), lambda i, j, k: (k, j + b0))],
        out_specs=pl.BlockSpec((tm, tn), lambda i, j, k: (i, j)),
        scratch_shapes=[pltpu.VMEM((tm, tn), f32)],
        compiler_params=_cp(("parallel", "parallel", "arbitrary"), VMEM_LIMIT), name=name)(a, b)


def _small_pars(b_fg, a_log, dt_bias):
    par = jnp.zeros((8, 128), f32)
    par = par.at[0, 0:8].set(b_fg).at[1, 8:12].set(a_log).at[2, 8:12].set(dt_bias)
    return par


def _small_prep(zs, par, name):
    S = zs.shape[0]
    ts = min(S, 512)

    def body(z_ref, par_ref, o_ref, carry_ref):
        i = pl.program_id(0)

        @pl.when(i == 0)
        def _():
            carry_ref[...] = jnp.zeros_like(carry_ref)

        z = z_ref[...]
        lane = lax.broadcasted_iota(jnp.int32, (ts, 128), 1)
        row = lax.broadcasted_iota(jnp.int32, (ts, 128), 0)
        za = z + par_ref[0:1, :]
        logf = jnp.minimum(za, 0.0) - jnp.log1p(jnp.exp(-jnp.abs(za)))
        glog = -jnp.exp(par_ref[1:2, :]) * _softplus(z + par_ref[2:3, :])
        x = jnp.where(lane < 8, logf, jnp.where(lane < 12, glog, 0.0))
        pos = jnp.where(lane < 8, row, row & (CH - 1))
        s = 1
        while s < ts:
            x = x + jnp.where(pos >= s, pltpu.roll(x, s, 0), 0.0)
            s *= 2
        tot = x + carry_ref[0:1, :]
        carry_ref[...] = jnp.broadcast_to(jnp.where(lane[0:1] < 8, tot[ts - 1:ts, :], 0.0), (8, 128))
        o_ref[...] = jnp.where(lane < 8, tot, jnp.where(lane < 12, x, jnp.where(lane < 16, _sig(z), 0.0)))

    return pl.pallas_call(
        body, grid=(S // ts,), out_shape=jax.ShapeDtypeStruct((S, 128), f32),
        in_specs=[pl.BlockSpec((ts, 128), lambda i: (i, 0)), pl.BlockSpec((8, 128), lambda i: (0, 0))],
        out_specs=pl.BlockSpec((ts, 128), lambda i: (i, 0)), scratch_shapes=[pltpu.VMEM((8, 128), f32)],
        compiler_params=_cp(("arbitrary",)), name=name)(zs, par)


def _small_bwd(zs, par, dfr, dfc, dsm, name):
    S = zs.shape[0]
    ts = min(S, 512)
    nt = S // ts

    def body(z_ref, par_ref, dfr_ref, dfc_ref, dsm_ref, dz_ref, acc_ref, carry_ref):
        i = pl.program_id(0)

        @pl.when(i == 0)
        def _():
            carry_ref[...] = jnp.zeros_like(carry_ref)

        z = z_ref[...]
        dsm_v = dsm_ref[...]
        lane = lax.broadcasted_iota(jnp.int32, (ts, 128), 1)
        row = lax.broadcasted_iota(jnp.int32, (ts, 128), 0)
        df = jnp.zeros((ts, 128), f32)
        for p in range(4):
            dpair = dfr_ref[p] - dfc_ref[p]
            df = jnp.where(lane == 2 * p, dpair[:, 0:1], jnp.where(lane == 2 * p + 1, dpair[:, 64:65], df))
        x = jnp.where(lane < 8, df, jnp.where(lane < 12, dsm_v, 0.0))
        pos = jnp.where(lane < 8, row, row & (CH - 1))
        seg = jnp.where(lane < 8, ts, CH)
        s = 1
        while s < ts:
            x = x + jnp.where(pos + s < seg, pltpu.roll(x, ts - s, 0), 0.0)
            s *= 2
        tot = x + carry_ref[0:1, :]
        carry_ref[...] = jnp.broadcast_to(jnp.where(lane[0:1] < 8, tot[0:1, :], 0.0), (8, 128))
        za = z + par_ref[0:1, :]
        daf = tot * _sig(-za)
        zb = z + par_ref[2:3, :]
        nea = -jnp.exp(par_ref[1:2, :])
        glog = nea * _softplus(zb)
        dba = x * nea * _sig(zb)
        beta = _sig(z)
        dbb = dsm_v * beta * (1.0 - beta)
        dz_ref[...] = _b(jnp.where(lane < 8, daf, jnp.where(lane < 12, dba, jnp.where(lane < 16, dbb, 0.0))))
        r0 = _colsum(jnp.where(lane < 8, daf, 0.0))
        r1 = _colsum(jnp.where((lane >= 8) & (lane < 12), x * glog, 0.0))
        r2 = _colsum(jnp.where((lane >= 8) & (lane < 12), dba, 0.0))
        r8 = lax.broadcasted_iota(jnp.int32, (8, 128), 0)
        part = jnp.where(r8 == 0, r0, jnp.where(r8 == 1, r1, jnp.where(r8 == 2, r2, 0.0)))

        @pl.when(i == 0)
        def _():
            acc_ref[...] = part

        @pl.when(i > 0)
        def _():
            acc_ref[...] += part

    rev = pl.BlockSpec((ts, 128), lambda i: (nt - 1 - i, 0))
    rev4 = pl.BlockSpec((4, ts, 128), lambda i: (0, nt - 1 - i, 0))
    c8 = pl.BlockSpec((8, 128), lambda i: (0, 0))
    return pl.pallas_call(
        body, grid=(nt,), out_shape=(jax.ShapeDtypeStruct((S, 128), bf16), jax.ShapeDtypeStruct((8, 128), f32)),
        in_specs=[rev, c8, rev4, rev4, rev], out_specs=(rev, c8), scratch_shapes=[pltpu.VMEM((8, 128), f32)],
        compiler_params=_cp(("arbitrary",)), name=name)(zs, par, dfr, dfc, dsm)


def _split3(x):
    hi = _b(x).astype(f32)
    r = x - hi
    mid = _b(r).astype(f32)
    return hi, mid, _b(r - mid).astype(f32)


def _fox_prep(zb, sm, name):
    S = zb.shape[0]
    ts = min(S, 512)

    def body(q_ref, k_ref, f_ref, qa_ref, ka_ref):
        lane = lax.broadcasted_iota(jnp.int32, (ts, 128), 1)
        f = f_ref[...]
        for p in range(4):
            q = q_ref[:, p * 128:(p + 1) * 128].astype(f32) * FOX_SCALE
            k = k_ref[:, p * 128:(p + 1) * 128].astype(f32)
            for h in (0, 1):
                hi, mid, lo = _split3(f[:, 2 * p + h:2 * p + h + 1])
                own = (lane < 64) if h == 0 else (lane >= 64)
                o = 64 if h == 0 else 0
                ones_lo = (lane >= o) & (lane < o + 3)
                ones_hi = (lane >= o + 3) & (lane < o + 6)
                qaug = jnp.where(lane == o, hi, jnp.where(lane == o + 1, mid, jnp.where(lane == o + 2, lo, jnp.where(ones_hi, 1.0, 0.0))))
                kaug = jnp.where(lane == o + 3, -hi, jnp.where(lane == o + 4, -mid, jnp.where(lane == o + 5, -lo, jnp.where(ones_lo, 1.0, 0.0))))
                qa_ref[2 * p + h] = _b(jnp.where(own, q, qaug))
                ka_ref[2 * p + h] = _b(jnp.where(own, k, kaug))

    out = jax.ShapeDtypeStruct((8, S, 128), bf16)
    return pl.pallas_call(
        body, grid=(S // ts,), out_shape=(out, out),
        in_specs=[pl.BlockSpec((ts, 512), lambda i: (i, 0)), pl.BlockSpec((ts, 512), lambda i: (i, 1)), pl.BlockSpec((ts, 128), lambda i: (i, 0))],
        out_specs=(pl.BlockSpec((8, ts, 128), lambda i: (0, i, 0)), pl.BlockSpec((8, ts, 128), lambda i: (0, i, 0))),
        compiler_params=_cp(("parallel",)), name=name)(zb, zb, sm)


def _pair_rows(a, T):
    at = jnp.transpose(a)
    r8 = lax.broadcasted_iota(jnp.int32, (8, T), 0)
    return jnp.where(r8 == 0, at[0:1, :], at[64:65, :])


def _fox_fwd(qa, ka, zb, name):
    S = zb.shape[0]
    T = min(S, 1024)

    def body(qa_ref, ka_ref, v_ref, o_ref, lse_ref, lset_ref, m_ref, l_ref, acc_ref):
        i = pl.program_id(1)
        m_ref[...] = jnp.full_like(m_ref, NEG)
        l_ref[...] = jnp.zeros_like(l_ref)
        acc_ref[...] = jnp.zeros_like(acc_ref)
        row = lax.broadcasted_iota(jnp.int32, (T, T), 0)
        col = lax.broadcasted_iota(jnp.int32, (T, T), 1)

        def tile(j, masked):
            off = pl.multiple_of(j * T, T)
            vs = v_ref[pl.ds(off, T), :]
            for h in (0, 1):
                s = _dot_nt(qa_ref[h], ka_ref[h, pl.ds(off, T), :])
                if masked:
                    s = jnp.where(row >= col, s, NEG)
                m_old = m_ref[h]
                m_new = jnp.maximum(m_old, jnp.max(s, axis=1, keepdims=True))
                alpha = jnp.exp(m_old - m_new)
                pr = jnp.exp(s - jnp.tile(m_new, (1, T // 128)))
                l_ref[h] = alpha * l_ref[h] + _rowsum(pr)
                acc_ref[h] = alpha * acc_ref[h] + _dot(_b(pr), vs)
                m_ref[h] = m_new

        def step(j, c):
            tile(j, False)
            return c

        lax.fori_loop(0, i, step, 0)
        tile(i, True)
        lane2 = lax.broadcasted_iota(jnp.int32, (T, 128), 1)
        o_ref[...] = jnp.where(lane2 < 64, acc_ref[0] / l_ref[0], acc_ref[1] / l_ref[1])
        lse = jnp.where(lane2 < 64, m_ref[0] + jnp.log(l_ref[0]), m_ref[1] + jnp.log(l_ref[1]))
        lse_ref[0] = lse
        lset_ref[0] = _pair_rows(lse, T)

    return pl.pallas_call(
        body, grid=(4, S // T),
        out_shape=(jax.ShapeDtypeStruct((S, 512), f32), jax.ShapeDtypeStruct((4, S, 128), f32), jax.ShapeDtypeStruct((4, 8, S), f32)),
        in_specs=[pl.BlockSpec((2, T, 128), lambda p, i: (p, i, 0)), pl.BlockSpec((2, S, 128), lambda p, i: (p, 0, 0)),
                  pl.BlockSpec((S, 128), lambda p, i: (0, 8 + p))],
        out_specs=(pl.BlockSpec((T, 128), lambda p, i: (i, p)), pl.BlockSpec((1, T, 128), lambda p, i: (p, i, 0)),
                   pl.BlockSpec((1, 8, T), lambda p, i: (p, 0, i))),
        scratch_shapes=[pltpu.VMEM((2, T, 128), f32), pltpu.VMEM((2, T, 128), f32), pltpu.VMEM((2, T, 128), f32)],
        compiler_params=_cp(("arbitrary", "arbitrary"), VMEM_LIMIT), name=name)(qa, ka, zb)


def _fox_bwd_dq(qa, ka, zb, lse, o, do, name, tile_rows=1024):
    S = zb.shape[0]
    T = min(S, tile_rows)

    def body(qa_ref, ka_ref, v_ref, lse_ref, o_ref, do_ref, dq_ref, dl_ref, dfr_ref, acc_ref, fr_ref):
        i = pl.program_id(1)
        lane1 = lax.broadcasted_iota(jnp.int32, (1, 128), 1)
        lane2 = lax.broadcasted_iota(jnp.int32, (T, 128), 1)
        hm = (lane1 < 64, lane1 >= 64)
        lse_v = lse_ref[0]
        lcol = [lse_v[:, 0:1], lse_v[:, 64:65]]
        do = do_ref[...]
        prod = do * o_ref[...]
        dcol = [_rowsum(jnp.where(lane2 < 64, prod, 0.0)), _rowsum(jnp.where(lane2 >= 64, prod, 0.0))]
        dob = _b(do)
        dos = [jnp.where(hm[h], dob, jnp.zeros_like(dob)) for h in (0, 1)]
        acc_ref[...] = jnp.zeros_like(acc_ref)
        fr_ref[...] = jnp.zeros_like(fr_ref)
        row = lax.broadcasted_iota(jnp.int32, (T, T), 0)
        col = lax.broadcasted_iota(jnp.int32, (T, T), 1)

        def tile(j, masked):
            off = pl.multiple_of(j * T, T)
            vs = v_ref[pl.ds(off, T), :]
            for h in (0, 1):
                ks = ka_ref[h, pl.ds(off, T), :]
                s = _dot_nt(qa_ref[h], ks)
                if masked:
                    s = jnp.where(row >= col, s, NEG)
                pr = jnp.exp(s - lcol[h])
                dp = _dot_nt(dos[h], vs)
                ds = pr * (dp - dcol[h])
                acc_ref[h] += _dot(_b(ds), ks)
                fr_ref[h] += _rowsum(ds)

        def step(j, c):
            tile(j, False)
            return c

        lax.fori_loop(0, i, step, 0)
        tile(i, True)
        dq_ref[...] = _b(jnp.where(lane2 < 64, acc_ref[0], acc_ref[1]) * FOX_SCALE)
        dl_ref[0] = _pair_rows(jnp.where(lane2 < 64, dcol[0], dcol[1]), T)
        dfr_ref[0] = jnp.where(lane2 < 64, fr_ref[0], fr_ref[1])

    tq = pl.BlockSpec((T, 128), lambda p, i: (i, p))
    pair = pl.BlockSpec((1, T, 128), lambda p, i: (p, i, 0))
    return pl.pallas_call(
        body, grid=(4, S // T),
        out_shape=(jax.ShapeDtypeStruct((S, 512), bf16), jax.ShapeDtypeStruct((4, 8, S), f32), jax.ShapeDtypeStruct((4, S, 128), f32)),
        in_specs=[pl.BlockSpec((2, T, 128), lambda p, i: (p, i, 0)), pl.BlockSpec((2, S, 128), lambda p, i: (p, 0, 0)),
                  pl.BlockSpec((S, 128), lambda p, i: (0, 8 + p)), pair, tq, tq],
        out_specs=(tq, pl.BlockSpec((1, 8, T), lambda p, i: (p, 0, i)), pair),
        scratch_shapes=[pltpu.VMEM((2, T, 128), f32), pltpu.VMEM((2, T, 1), f32)],
        compiler_params=_cp(("arbitrary", "arbitrary"), VMEM_LIMIT), name=name)(qa, ka, zb, lse, o, do)


def _fox_bwd_dkv(qa, ka, zb, dob, lse_t, dl_t, name, tile_rows=1024):
    S = zb.shape[0]
    T = min(S, tile_rows)
    nq = S // T

    def body(ka_ref, v_ref, qa_ref, do_ref, lt_ref, dt_ref, dk_ref, dv_ref, dfc_ref, dka_ref, dva_ref, fs_ref):
        p, j = pl.program_id(0), pl.program_id(1)
        lane1 = lax.broadcasted_iota(jnp.int32, (1, 128), 1)
        lane2 = lax.broadcasted_iota(jnp.int32, (T, 128), 1)
        hm = (lane1 < 64, lane1 >= 64)
        v = v_ref[...]
        vsm = [jnp.where(hm[h], v, jnp.zeros_like(v)) for h in (0, 1)]
        dka_ref[...] = jnp.zeros_like(dka_ref)
        dva_ref[...] = jnp.zeros_like(dva_ref)
        fs_ref[...] = jnp.zeros_like(fs_ref)
        row = lax.broadcasted_iota(jnp.int32, (T, T), 0)
        col = lax.broadcasted_iota(jnp.int32, (T, T), 1)

        def tile(i, masked):
            off = pl.multiple_of(i * T, T)
            dot_ = do_ref[pl.ds(off, T), :]
            for h in (0, 1):
                hr = pl.ds(2 * p + h, 1)
                qt = qa_ref[h, pl.ds(off, T), :]
                s_t = _dot_nt(ka_ref[h], qt)
                if masked:
                    s_t = jnp.where(col >= row, s_t, NEG)
                p_t = jnp.exp(s_t - lt_ref[hr, pl.ds(off, T)])
                dva_ref[h] += _dot(_b(p_t), dot_)
                dp_t = _dot_nt(vsm[h], dot_)
                ds_t = p_t * (dp_t - dt_ref[hr, pl.ds(off, T)])
                dka_ref[h] += _dot(_b(ds_t), qt)
                fs_ref[h] += _rowsum(ds_t)

        def step(i, c):
            tile(i, False)
            return c

        tile(j, True)
        lax.fori_loop(j + 1, nq, step, 0)
        dk_ref[...] = _b(jnp.where(lane2 < 64, dka_ref[0], dka_ref[1]))
        dv_ref[...] = _b(jnp.where(lane2 < 64, dva_ref[0], dva_ref[1]))
        dfc_ref[0] = jnp.where(lane2 < 64, fs_ref[0], fs_ref[1])

    res = pl.BlockSpec((8, S), lambda p, j: (0, 0))
    tk = pl.BlockSpec((T, 128), lambda p, j: (j, p))
    return pl.pallas_call(
        body, grid=(4, nq),
        out_shape=(jax.ShapeDtypeStruct((S, 512), bf16), jax.ShapeDtypeStruct((S, 512), bf16), jax.ShapeDtypeStruct((4, S, 128), f32)),
        in_specs=[pl.BlockSpec((2, T, 128), lambda p, j: (p, j, 0)), pl.BlockSpec((T, 128), lambda p, j: (j, 8 + p)),
                  pl.BlockSpec((2, S, 128), lambda p, j: (p, 0, 0)), pl.BlockSpec((S, 128), lambda p, j: (0, p)), res, res],
        out_specs=(tk, tk, pl.BlockSpec((1, T, 128), lambda p, j: (p, j, 0))),
        scratch_shapes=[pltpu.VMEM((2, T, 128), f32), pltpu.VMEM((2, T, 128), f32), pltpu.VMEM((2, T, 1), f32)],
        compiler_params=_cp(("arbitrary", "arbitrary"), VMEM_LIMIT), name=name)(ka, zb, qa, dob, lse_t, dl_t)


def _head_rows(a):
    return a[:, 0:2, :].reshape(8, a.shape[2])


def _conv_taps(ext, x, w_ref, ts):
    y = x * w_ref[3:4, :]
    shifted = []
    for k in (1, 2, 3):
        xs = pltpu.roll(ext, k, 0)[8:]
        shifted.append(xs)
        y = y + xs * w_ref[3 - k:4 - k, :]
    return y, shifted


def _gdn_prep(zf, cw, name):
    S = zf.shape[0]
    ts = min(S, 512)

    def body(x_ref, w_ref, o_ref, tail_ref):
        i = pl.program_id(0)

        @pl.when(i == 0)
        def _():
            tail_ref[...] = jnp.zeros_like(tail_ref)

        x = x_ref[...]
        ext = jnp.concatenate([tail_ref[...], x], axis=0)
        y, _ = _conv_taps(ext, x, w_ref, ts)
        tail_ref[...] = x[ts - 8:, :]
        a = _silu(y)
        for hb in range(12):
            blk = a[:, hb * 128:(hb + 1) * 128]
            if hb < 8:
                blk = blk * lax.rsqrt(_rowsum(blk * blk) + EPS)
            if hb < 4:
                blk = blk * GDN_SCALE
            o_ref[:, hb * 128:(hb + 1) * 128] = blk

    return pl.pallas_call(
        body, grid=(S // ts,), out_shape=jax.ShapeDtypeStruct((S, 1536), f32),
        in_specs=[pl.BlockSpec((ts, 1536), lambda i: (i, 0)), pl.BlockSpec((8, 1536), lambda i: (0, 0))],
        out_specs=pl.BlockSpec((ts, 1536), lambda i: (i, 0)), scratch_shapes=[pltpu.VMEM((8, 1536), f32)],
        compiler_params=_cp(("arbitrary",), VMEM_LIMIT), name=name)(zf, cw)


def _gdn_prep_bwd(zf, cw, dg, name):
    S = zf.shape[0]
    ts = min(S, 512)
    nt = S // ts

    def body(x_ref, xp_ref, w_ref, dg_ref, dx_ref, dw_ref, head_ref):
        i = pl.program_id(0)

        @pl.when(i == 0)
        def _():
            head_ref[...] = jnp.zeros_like(head_ref)

        x = x_ref[...]
        prev = jnp.where(i == nt - 1, 0.0, xp_ref[...])
        ext = jnp.concatenate([prev, x], axis=0)
        y, shifted = _conv_taps(ext, x, w_ref, ts)
        a = _silu(y)
        das = []
        for hb in range(12):
            blk = a[:, hb * 128:(hb + 1) * 128]
            d = dg_ref[:, hb * 128:(hb + 1) * 128]
            if hb < 4:
                d = d * GDN_SCALE
            if hb < 8:
                r = lax.rsqrt(_rowsum(blk * blk) + EPS)
                n = blk * r
                d = r * (d - n * _rowsum(d * n))
            das.append(d)
        dy = jnp.concatenate(das, axis=1) * _dsilu(y)
        extd = jnp.concatenate([dy, head_ref[...]], axis=0)
        dx = dy * w_ref[3:4, :]
        for k in (1, 2, 3):
            dx = dx + pltpu.roll(extd, ts + 8 - k, 0)[:ts] * w_ref[3 - k:4 - k, :]
        head_ref[...] = dy[0:8, :]
        dx_ref[...] = _b(dx)
        r8 = lax.broadcasted_iota(jnp.int32, (8, 1536), 0)
        part = jnp.where(r8 == 3, _colsum(dy * x), 0.0)
        for k in (1, 2, 3):
            part = jnp.where(r8 == 3 - k, _colsum(dy * shifted[k - 1]), part)

        @pl.when(i == 0)
        def _():
            dw_ref[...] = part

        @pl.when(i > 0)
        def _():
            dw_ref[...] += part

    rev = pl.BlockSpec((ts, 1536), lambda i: (nt - 1 - i, 0))
    prev8 = pl.BlockSpec((8, 1536), lambda i: (jnp.maximum((nt - 1 - i) * (ts // 8) - 1, 0), 0))
    w8 = pl.BlockSpec((8, 1536), lambda i: (0, 0))
    return pl.pallas_call(
        body, grid=(nt,), out_shape=(jax.ShapeDtypeStruct((S, 1536), bf16), jax.ShapeDtypeStruct((8, 1536), f32)),
        in_specs=[rev, prev8, w8, rev], out_specs=(rev, w8), scratch_shapes=[pltpu.VMEM((8, 1536), f32)],
        compiler_params=_cp(("arbitrary",), VMEM_LIMIT), name=name)(zf, zf, cw, dg)


def _tri_inv(a, row, col):
    same = (row >> 4) == (col >> 4)
    dm = jnp.where(same, a, 0.0)
    lo = a - dm
    eye = jnp.where(row == col, 1.0, 0.0)
    d2 = _hi_b(dm, dm)
    d4 = _hi_b(d2, d2)
    d8 = _hi_b(d4, d4)
    x0 = _hi_b(_hi_b(eye - dm, eye + d2), _hi_b(eye + d4, eye + d8))
    n = _hi_b(x0, lo)
    n2 = _hi_b(n, n)
    return _hi_b(_hi_b(eye - n, eye + n2), x0)


def _bd(a, b):
    return lax.dot_general(a, b, (((2,), (1,)), ((0,), (0,))), preferred_element_type=f32)


def _bd_nt(a, b):
    return lax.dot_general(a, b, (((2,), (2,)), ((0,), (0,))), preferred_element_type=f32)


def _bd_tn(a, b):
    return lax.dot_general(a, b, (((1,), (1,)), ((0,), (0,))), preferred_element_type=f32)


def _hi_b_nt(a, b):
    return _mm3(a, b, (((2,), (2,)), ((0,), (0,))))


def _hi_b_tn(a, b):
    return _mm3(a, b, (((1,), (1,)), ((0,), (0,))))


def _gdn_local(x_ref, sm_ref, gt_ref, row, col, cps=1):
    idx = [(c, h) for c in range(cps) for h in range(4)]

    def rows(c):
        return slice(c * CH, (c + 1) * CH)

    q = jnp.stack([x_ref[rows(c), h * 128:(h + 1) * 128] for c, h in idx])
    k = jnp.stack([x_ref[rows(c), 512 + h * 128:512 + (h + 1) * 128] for c, h in idx])
    v = jnp.stack([x_ref[rows(c), 1024 + h * 128:1024 + (h + 1) * 128] for c, h in idx])
    gc = jnp.stack([sm_ref[rows(c), 8 + h:9 + h] for c, h in idx])
    beta = jnp.stack([sm_ref[rows(c), 12 + h:13 + h] for c, h in idx])
    gr = jnp.stack([gt_ref[h, c] for c, h in idx])
    eg = jnp.exp(gc)
    gl = gc[:, CH - 1:CH, :]
    dec = jnp.exp(gl - gc)
    gm = gc - gr
    gam_i = jnp.exp(jnp.where(row >= col, gm, -jnp.inf))
    gam_s = jnp.where(row > col, gam_i, 0.0)
    kb = k * beta
    return dict(q=q, k=k, v=v, beta=beta, eg=eg, egl=jnp.exp(gl), dec=dec, gam_i=gam_i, gam_s=gam_s,
                kb=kb, vb=v * beta, kbg=kb * eg, qdec=q * eg, kdec=k * dec,
                a=_bd_nt(_b(kb), _b(k)) * gam_s, aqk=_bd_nt(_b(q), _b(k)) * gam_i)


GDN_FWD_CHUNKS = 4


def _gdn_fwd(gqkv, sm, gt4, name, cps=GDN_FWD_CHUNKS):
    S = gqkv.shape[0]
    N = S // CH
    cps = min(cps, N)
    R = cps * CH

    def body(x_ref, sm_ref, gt_ref, o_ref, t_ref, st_ref, s_ref):
        n = pl.program_id(0)

        @pl.when(n == 0)
        def _():
            s_ref[...] = jnp.zeros_like(s_ref)

        row = lax.broadcasted_iota(jnp.int32, (CH, CH), 0)
        col = lax.broadcasted_iota(jnp.int32, (CH, CH), 1)
        c = _gdn_local(x_ref, sm_ref, gt_ref, row, col, cps)
        t = _tri_inv(c["a"], row, col)
        uw = _hi_b(t, jnp.concatenate([c["vb"], c["kbg"]], axis=2))
        u, w = uw[:, :, :128], uw[:, :, 128:]
        for ci in range(cps):
            sl = slice(4 * ci, 4 * ci + 4)
            rs = slice(ci * CH, (ci + 1) * CH)
            st = s_ref[...]
            st_ref[ci] = st
            sb = _b(st)
            vnew = u[sl] - _bd(_b(w[sl]), sb)
            o = _bd(_b(c["qdec"][sl]), sb) + _bd(_b(c["aqk"][sl]), _b(vnew))
            for h in range(4):
                o_ref[rs, h * 128:(h + 1) * 128] = o[h]
                t_ref[h, rs, :] = t[4 * ci + h]
            s_ref[...] = st * c["egl"][sl] + _bd_tn(_b(c["kdec"][sl]), _b(vnew))

    return pl.pallas_call(
        body, grid=(N // cps,),
        out_shape=(jax.ShapeDtypeStruct((S, 512), f32), jax.ShapeDtypeStruct((4, S, CH), f32), jax.ShapeDtypeStruct((N, 4, 128, 128), f32)),
        in_specs=[pl.BlockSpec((R, 1536), lambda n: (n, 0)), pl.BlockSpec((R, 128), lambda n: (n, 0)),
                  pl.BlockSpec((4, cps, 1, CH), lambda n: (0, n, 0, 0))],
        out_specs=(pl.BlockSpec((R, 512), lambda n: (n, 0)), pl.BlockSpec((4, R, CH), lambda n: (0, n, 0)),
                   pl.BlockSpec((cps, 4, 128, 128), lambda n: (n, 0, 0, 0))),
        scratch_shapes=[pltpu.VMEM((4, 128, 128), f32)], compiler_params=_cp(("arbitrary",)), name=name)(gqkv, sm, gt4)


GDN_BWD_CHUNKS = 4


def _gdn_bwd(gqkv, sm, gt4, tinv, states, do, name, cps=GDN_BWD_CHUNKS):
    S = gqkv.shape[0]
    N = S // CH
    cps = min(cps, N)
    R = cps * CH

    def body(x_ref, sm_ref, gt_ref, t_ref, st_ref, do_ref, dx_ref, dsm_ref, ds_ref):
        n = pl.program_id(0)

        @pl.when(n == 0)
        def _():
            ds_ref[...] = jnp.zeros_like(ds_ref)

        row = lax.broadcasted_iota(jnp.int32, (CH, CH), 0)
        col = lax.broadcasted_iota(jnp.int32, (CH, CH), 1)
        row1 = lax.broadcasted_iota(jnp.int32, (CH, 1), 0)
        lane = lax.broadcasted_iota(jnp.int32, (CH, 128), 1)
        ones = jnp.ones((4 * cps, CH, 128), f32)
        idx = [(ci, h) for ci in range(cps) for h in range(4)]
        c = _gdn_local(x_ref, sm_ref, gt_ref, row, col, cps)
        q, k, v, beta, eg = c["q"], c["k"], c["v"], c["beta"], c["eg"]
        t = jnp.stack([t_ref[h, ci * CH:(ci + 1) * CH, :] for ci, h in idx])
        uw = _hi_b(t, jnp.concatenate([c["vb"], c["kbg"]], axis=2))
        u, w = uw[:, :, :128], uw[:, :, 128:]
        st = st_ref[...].reshape(4 * cps, 128, 128)
        sb = _b(st)
        vnew = u - _bd(_b(w), sb)
        dob = _b(jnp.stack([do_ref[ci * CH:(ci + 1) * CH, h * 128:(h + 1) * 128] for ci, h in idx]))
        vnb = _b(vnew)
        dqdec = _bd_nt(dob, sb)
        daqk = jnp.where(row >= col, _bd_nt(dob, vnb), 0.0)
        qd_do = _bd_tn(_b(c["qdec"]), dob)
        aqk_do = _bd_tn(_b(c["aqk"]), dob)
        kdecb, wb = _b(c["kdec"]), _b(w)
        dvnew_l, dkdec_l, dgl_l = [None] * cps, [None] * cps, [None] * cps
        for ci in reversed(range(cps)):
            sl = slice(4 * ci, 4 * ci + 4)
            dsp = ds_ref[...]
            dspb = _b(dsp)
            dvn = _bd(kdecb[sl], dspb) + aqk_do[sl]
            dvnew_l[ci] = dvn
            dkdec_l[ci] = _bd_nt(vnb[sl], dspb)
            dgl_l[ci] = c["egl"][sl] * jnp.sum(dsp * st[sl], axis=(1, 2), keepdims=True)
            ds_ref[...] = dsp * c["egl"][sl] + qd_do[sl] - _bd_tn(wb[sl], _b(dvn))
        dvnew = jnp.concatenate(dvnew_l, axis=0)
        dkdec = jnp.concatenate(dkdec_l, axis=0)
        dgl = jnp.concatenate(dgl_l, axis=0)
        dw = -_bd_nt(_b(dvnew), sb)
        duw = _hi_b_tn(t, jnp.concatenate([dvnew, dw], axis=2))
        dvb, dkbg = duw[:, :, :128], duw[:, :, 128:]
        da = -jnp.where(row > col, _hi_b_nt(duw, uw), 0.0)
        dp = da * c["gam_s"]
        dqk = daqk * c["gam_i"]
        m = da * c["a"] + daqk * c["aqk"]
        csum = _hi_b_tn(m, ones)[:, :, 0:1]
        kk = dkdec * c["kdec"]

        def lsum(a):
            return jnp.sum(a, axis=2, keepdims=True)

        dgv = lsum(m) - csum + lsum(dqdec * c["qdec"]) - lsum(kk) + lsum(dkbg * c["kbg"])
        dgv = dgv + jnp.where(row1 == CH - 1, dgl + jnp.sum(kk, axis=(1, 2), keepdims=True), 0.0)
        dpb, dqkb = _b(dp), _b(dqk)
        dkb = _bd(dpb, _b(k)) + dkbg * eg
        dk = _bd_tn(dpb, _b(c["kb"])) + _bd_tn(dqkb, _b(q)) + dkdec * c["dec"] + dkb * beta
        dq = _bd(dqkb, _b(k)) + dqdec * eg
        dbeta = lsum(dkb * k) + lsum(dvb * v)
        dv = dvb * beta
        for ci in range(cps):
            rs = slice(ci * CH, (ci + 1) * CH)
            dsm = jnp.zeros((CH, 128), f32)
            for h in range(4):
                b = 4 * ci + h
                dx_ref[rs, h * 128:(h + 1) * 128] = dq[b]
                dx_ref[rs, 512 + h * 128:512 + (h + 1) * 128] = dk[b]
                dx_ref[rs, 1024 + h * 128:1024 + (h + 1) * 128] = dv[b]
                dsm = jnp.where(lane == 8 + h, dgv[b], jnp.where(lane == 12 + h, dbeta[b], dsm))
            dsm_ref[rs, :] = dsm

    G = N // cps
    return pl.pallas_call(
        body, grid=(G,), out_shape=(jax.ShapeDtypeStruct((S, 1536), f32), jax.ShapeDtypeStruct((S, 128), f32)),
        in_specs=[pl.BlockSpec((R, 1536), lambda n: (G - 1 - n, 0)), pl.BlockSpec((R, 128), lambda n: (G - 1 - n, 0)),
                  pl.BlockSpec((4, cps, 1, CH), lambda n: (0, G - 1 - n, 0, 0)), pl.BlockSpec((4, R, CH), lambda n: (0, G - 1 - n, 0)),
                  pl.BlockSpec((cps, 4, 128, 128), lambda n: (G - 1 - n, 0, 0, 0)), pl.BlockSpec((R, 512), lambda n: (G - 1 - n, 0))],
        out_specs=(pl.BlockSpec((R, 1536), lambda n: (G - 1 - n, 0)), pl.BlockSpec((R, 128), lambda n: (G - 1 - n, 0))),
        scratch_shapes=[pltpu.VMEM((4, 128, 128), f32)], compiler_params=_cp(("arbitrary",), VMEM_LIMIT), name=name)(gqkv, sm, gt4, tinv, states, do)


def _mem_attn(q, kv_ref, h):
    s = _dot_nt(q, kv_ref[:, h * 128:(h + 1) * 128]) * MEM_SCALE
    e = jnp.exp(s - jnp.max(s, axis=1, keepdims=True))
    return e / _rowsum(e)


def _gdn_out_norm(ob):
    r = lax.rsqrt(jnp.mean(ob * ob, axis=-1, keepdims=True) + EPS)
    return ob * r, r


def _merge_fwd(x, oa, ob, zb, zf, kv, b_merge, gdn_g, w_branch, w_out, name):
    S = x.shape[0]
    ts = min(S, 256)

    def body(x_ref, oa_ref, ob_ref, mq_ref, az_ref, bz_ref, mz_ref, gt_ref, kv_ref, bm_ref, gg_ref, wb_ref, wo_ref,
             xo_ref, y_ref, mg_ref):
        y_ref[:, 0:512] = _b(oa_ref[...] * _silu(az_ref[...]))
        for h in range(4):
            sl = slice(h * 128, (h + 1) * 128)
            nb, _ = _gdn_out_norm(ob_ref[:, sl])
            y_ref[:, 512 + h * 128:512 + (h + 1) * 128] = _b(nb * gg_ref[...] * _silu(bz_ref[:, sl]))
            pm = _mem_attn(mq_ref[:, sl], kv_ref, h)
            om = _dot(_b(pm), kv_ref[:, 512 + h * 128:512 + (h + 1) * 128])
            y_ref[:, 1024 + h * 128:1024 + (h + 1) * 128] = _b(om * _silu(mz_ref[:, sl]))
        merged = jnp.zeros((ts, D), f32)
        for n in range(3):
            gate = _sig(gt_ref[:, n * D:(n + 1) * D] + bm_ref[:, n * D:(n + 1) * D])
            merged = merged + gate * _dot(y_ref[:, n * 512:(n + 1) * 512], wb_ref[n])
        mb = _b(merged)
        mg_ref[...] = mb
        xo_ref[...] = x_ref[...] + _dot(mb, wo_ref[...])

    def col(w, c):
        return pl.BlockSpec((ts, w), lambda i: (i, c))

    def full(shape):
        return pl.BlockSpec(shape, lambda i: tuple(0 for _ in shape))

    return pl.pallas_call(
        body, grid=(S // ts,),
        out_shape=(jax.ShapeDtypeStruct((S, D), f32), jax.ShapeDtypeStruct((S, 1536), bf16), jax.ShapeDtypeStruct((S, D), bf16)),
        in_specs=[col(D, 0), col(512, 0), col(512, 0), col(512, 3), col(512, 3), col(512, 4), col(512, 5), col(3072, 1),
                  full((256, D)), full((1, 3072)), full((1, 128)), full((3, 512, D)), full((D, D))],
        out_specs=(col(D, 0), col(1536, 0), col(D, 0)),
        compiler_params=_cp(("parallel",), VMEM_LIMIT), name=name)(x, oa, ob, zb, zf, zf, zf, zf, kv, b_merge, gdn_g, w_branch, w_out)


def _merge_bwd(dout, ycat, oa, ob, zb, zf, kv, b_merge, gdn_g, w_branch, w_branch_t, w_out_t, name):
    S = dout.shape[0]
    ts = min(S, 256)

    def body(do_ref, y_ref, oa_ref, ob_ref, mq_ref, az_ref, bz_ref, mz_ref, gt_ref, kv_ref, bm_ref, gg_ref, wb_ref, wbt_ref, wot_ref,
             dpj_ref, dz_ref, dmq_ref, doa_ref, doab_ref, dob_ref, dkv_ref, dbm_ref, dgg_ref):
        i = pl.program_id(0)
        dmerged = _dot(_b(do_ref[...]), wot_ref[...])
        dys = []
        dbm_parts = []
        for n in range(3):
            cs = slice(n * D, (n + 1) * D)
            gate = _sig(gt_ref[:, cs] + bm_ref[:, cs])
            proj = _dot(y_ref[:, n * 512:(n + 1) * 512], wb_ref[n])
            dlogit = dmerged * proj * gate * (1.0 - gate)
            dz_ref[:, 1536 + n * D:1536 + (n + 1) * D] = _b(dlogit)
            dbm_parts.append(_colsum(dlogit))
            dproj = _b(dmerged * gate)
            dpj_ref[:, cs] = dproj
            dys.append(_dot(dproj, wbt_ref[n]))
        dbm = jnp.broadcast_to(jnp.concatenate(dbm_parts, axis=1), (8, 3072))
        az = az_ref[...]
        oa = oa_ref[...]
        doa = dys[0] * _silu(az)
        doa_ref[...] = doa
        doab_ref[...] = _b(doa)
        dz_ref[:, 0:512] = _b(dys[0] * oa * _dsilu(az))
        gg = gg_ref[...]
        dgg = jnp.zeros((1, 128), f32)
        dkv_parts_k, dkv_parts_v = [], []
        for h in range(4):
            sl = slice(h * 128, (h + 1) * 128)
            bz = bz_ref[:, sl]
            dyb = dys[1][:, sl]
            nb, r = _gdn_out_norm(ob_ref[:, sl])
            dz_ref[:, 512 + h * 128:512 + (h + 1) * 128] = _b(dyb * nb * gg * _dsilu(bz))
            dng = dyb * _silu(bz)
            dgg = dgg + _colsum(dng * nb)
            dnb = dng * gg
            dob_ref[:, sl] = r * (dnb - nb * jnp.mean(dnb * nb, axis=-1, keepdims=True))
            mz = mz_ref[:, sl]
            dym = dys[2][:, sl]
            q = mq_ref[:, sl]
            kh = kv_ref[:, sl]
            vh = kv_ref[:, 512 + h * 128:512 + (h + 1) * 128]
            pm = _mem_attn(q, kv_ref, h)
            pmb = _b(pm)
            om = _dot(pmb, vh)
            dz_ref[:, 1024 + h * 128:1024 + (h + 1) * 128] = _b(dym * om * _dsilu(mz))
            dom = _b(dym * _silu(mz))
            dkv_parts_v.append(_dot_tn(pmb, dom))
            dpm = _dot_nt(dom, vh)
            dsm = _b(pm * (dpm - _rowsum(dpm * pm)) * MEM_SCALE)
            dmq_ref[:, sl] = _b(_dot(dsm, kh))
            dkv_parts_k.append(_dot_tn(dsm, q))
        dkv = jnp.concatenate(dkv_parts_k + dkv_parts_v, axis=1)
        dggb = jnp.broadcast_to(dgg, (8, 128))

        @pl.when(i == 0)
        def _():
            dkv_ref[...] = dkv
            dbm_ref[...] = dbm
            dgg_ref[...] = dggb

        @pl.when(i > 0)
        def _():
            dkv_ref[...] += dkv
            dbm_ref[...] += dbm
            dgg_ref[...] += dggb

    def col(w, c):
        return pl.BlockSpec((ts, w), lambda i: (i, c))

    def full(shape):
        return pl.BlockSpec(shape, lambda i: tuple(0 for _ in shape))

    return pl.pallas_call(
        body, grid=(S // ts,),
        out_shape=(jax.ShapeDtypeStruct((S, 3072), bf16), jax.ShapeDtypeStruct((S, 4608), bf16), jax.ShapeDtypeStruct((S, 512), bf16),
                   jax.ShapeDtypeStruct((S, 512), f32), jax.ShapeDtypeStruct((S, 512), bf16), jax.ShapeDtypeStruct((S, 512), f32),
                   jax.ShapeDtypeStruct((256, D), f32), jax.ShapeDtypeStruct((8, 3072), f32), jax.ShapeDtypeStruct((8, 128), f32)),
        in_specs=[col(D, 0), col(1536, 0), col(512, 0), col(512, 0), col(512, 3), col(512, 3), col(512, 4), col(512, 5), col(3072, 1),
                  full((256, D)), full((1, 3072)), full((1, 128)), full((3, 512, D)), full((3, D, 512)), full((D, D))],
        out_specs=(col(3072, 0), col(4608, 0), col(512, 0), col(512, 0), col(512, 0), col(512, 0),
                   full((256, D)), full((8, 3072)), full((8, 128))),
        compiler_params=_cp(("arbitrary",), VMEM_LIMIT), name=name)(
            dout, ycat, oa, ob, zb, zf, zf, zf, zf, kv, b_merge, gdn_g, w_branch, w_branch_t, w_out_t)


def _mesh_pos():
    return lax.axis_index("x"), lax.axis_index("y"), lax.axis_index("c")


def _all_gather(xs, name):
    n = len(xs)

    def body(*refs):
        x_refs, out_refs = refs[:n], refs[n:2 * n]
        send_sems, recv_sems, local_sems = refs[2 * n:]
        mx, my, mc = _mesh_pos()
        me, sibling = (mx, my, mc), (mx, my, 1 - mc)
        chips = [(1 - mx, my), (mx, 1 - my), (1 - mx, 1 - my)]

        def copy(a, k, block, to, src=None):
            px, py, pc = block
            slot = out_refs[a].at[4 * px + 2 * py + pc]
            return pltpu.make_async_remote_copy(
                src_ref=slot if src is None else src, dst_ref=slot,
                send_sem=send_sems.at[7 * a + k], recv_sem=recv_sems.at[7 * a + k], device_id=to, device_id_type=pl.DeviceIdType.MESH)

        mine = [pltpu.make_async_copy(x_refs[a], out_refs[a].at[4 * mx + 2 * my + mc], local_sems.at[a]) for a in range(n)]
        for cp in mine:
            cp.start()
        first = []
        for a in range(n):
            first.append(copy(a, 0, me, sibling, src=x_refs[a]))
            first += [copy(a, 1 + j, me, (*chip, mc), src=x_refs[a]) for j, chip in enumerate(chips)]
        for cp in first:
            cp.start()
        passed = []
        for j, chip in enumerate(chips):
            for a in range(n):
                copy(a, 1 + j, (*chip, mc), me).wait_recv()
                fwd = copy(a, 4 + j, (*chip, mc), sibling)
                fwd.start()
                passed.append(fwd)
        for a in range(n):
            copy(a, 0, sibling, me).wait_recv()
            for j, chip in enumerate(chips):
                copy(a, 4 + j, (*chip, 1 - mc), me).wait_recv()
        for cp in first + passed:
            cp.wait_send()
        for cp in mine:
            cp.wait()

    anyspec = pl.BlockSpec(memory_space=pl.ANY)
    return pl.pallas_call(
        body, out_shape=tuple(jax.ShapeDtypeStruct((N_DEV,) + x.shape, x.dtype) for x in xs),
        in_specs=[anyspec] * n, out_specs=tuple([anyspec] * n),
        scratch_shapes=[pltpu.SemaphoreType.DMA((7 * n,)), pltpu.SemaphoreType.DMA((7 * n,)), pltpu.SemaphoreType.DMA((n,))],
        name=name)(*xs)


def _exchange(sends, name):
    n = len(sends)

    def body(*refs):
        s_refs, r_refs = refs[:n], refs[n:2 * n]
        send_sems, recv_sems, local_sems = refs[2 * n:]
        mx, my, mc = _mesh_pos()
        me_id = 4 * mx + 2 * my + mc
        mine = [pltpu.make_async_copy(s_refs[a].at[me_id], r_refs[a].at[me_id], local_sems.at[a]) for a in range(n)]
        for cp in mine:
            cp.start()
        copies = []
        for k in range(1, N_DEV):
            px = 1 - mx if k & 4 else mx
            py = 1 - my if k & 2 else my
            pc = 1 - mc if k & 1 else mc
            for a in range(n):
                copies.append(pltpu.make_async_remote_copy(
                    src_ref=s_refs[a].at[4 * px + 2 * py + pc], dst_ref=r_refs[a].at[me_id],
                    send_sem=send_sems.at[7 * a + k - 1], recv_sem=recv_sems.at[7 * a + k - 1],
                    device_id=(px, py, pc), device_id_type=pl.DeviceIdType.MESH))
        for cp in copies:
            cp.start()
        for cp in copies:
            cp.wait()
        for cp in mine:
            cp.wait()

    anyspec = pl.BlockSpec(memory_space=pl.ANY)
    return pl.pallas_call(
        body, out_shape=tuple(jax.ShapeDtypeStruct(s.shape, s.dtype) for s in sends),
        in_specs=[anyspec] * n, out_specs=tuple([anyspec] * n),
        scratch_shapes=[pltpu.SemaphoreType.DMA((7 * n,)), pltpu.SemaphoreType.DMA((7 * n,)), pltpu.SemaphoreType.DMA((n,))],
        name=name)(*sends)


ADAMW_BLOCK_BYTES = 4 * 1024 * 1024


def _adamw(parts, w, m, v, name):
    _, R, C = parts.shape
    tr = R
    for t in (1024, 512, 256, 128, 64, 32, 16, 8):
        if R % t == 0 and N_DEV * t * C * 4 <= ADAMW_BLOCK_BYTES:
            tr = t
            break

    def body(p_ref, w_ref, m_ref, v_ref, g_ref, d_ref, nm_ref, nv_ref):
        g = p_ref[0].astype(f32)
        for j in range(1, N_DEV):
            g = g + p_ref[j].astype(f32)
        mn = ADAM_B1 * m_ref[...] + (1.0 - ADAM_B1) * g
        vn = ADAM_B2 * v_ref[...] + (1.0 - ADAM_B2) * jnp.square(g)
        m_hat = mn / (1.0 - ADAM_B1 ** ADAM_STEP)
        v_hat = vn / (1.0 - ADAM_B2 ** ADAM_STEP)
        g_ref[...] = g
        d_ref[...] = -ADAM_LR * (m_hat / (jnp.sqrt(v_hat) + ADAM_EPS) + ADAM_WD * w_ref[...])
        nm_ref[...] = mn
        nv_ref[...] = vn

    t2 = pl.BlockSpec((tr, C), lambda i: (i, 0))
    out = jax.ShapeDtypeStruct((R, C), f32)
    return pl.pallas_call(
        body, grid=(R // tr,), out_shape=(out, out, out, out),
        in_specs=[pl.BlockSpec((N_DEV, tr, C), lambda i: (0, i, 0)), t2, t2, t2], out_specs=(t2, t2, t2, t2),
        compiler_params=_cp(("parallel",), VMEM_LIMIT), name=name)(parts, w, m, v)


def _as2d(a):
    return a.reshape(-1, a.shape[-1])


def _perm_cols(w):
    parts = [w[..., _COLS[n][0]:_COLS[n][1]] for n in _ORDER]
    pad = jnp.zeros(w.shape[:-1] + (N_ALL - N_IN,), w.dtype)
    return jnp.concatenate(parts + [pad], axis=-1)


def _unperm_cols(w):
    pieces, off = {}, 0
    for n in _ORDER:
        width = _COLS[n][1] - _COLS[n][0]
        pieces[n] = w[..., off:off + width]
        off += width
    return jnp.concatenate([pieces[n] for n in sorted(_COLS, key=lambda n: _COLS[n][0])], axis=-1)


_SMALL_ROWS = 16


def _pack_small(t):
    z = jnp.zeros((D,), f32)
    misc = z.at[0:16].set(t["b_fg"].reshape(-1)).at[16:24].set(t["a_log"].reshape(-1)).at[24:32].set(t["dt_bias"].reshape(-1))
    misc = misc.at[128:384].set(t["gdn_norm_g"].reshape(-1))
    if "extra" in t:
        misc = misc.at[512].set(t["extra"])
    rows = [t["norm_g"], t["b_merge"].reshape(6, D), t["mem_norm_g"], t["final_norm_g"][None], misc[None],
            jnp.zeros((_SMALL_ROWS - 12, D), f32)]
    return jnp.concatenate(rows, axis=0)


def _unpack_small(a):
    misc = a[11]
    return dict(norm_g=a[0:2], b_merge=a[2:8].reshape(2, 3072), mem_norm_g=a[8:10], final_norm_g=a[10],
                b_fg=misc[0:16].reshape(2, 8), a_log=misc[16:24].reshape(2, 4), dt_bias=misc[24:32].reshape(2, 4),
                gdn_norm_g=misc[128:384].reshape(2, 128), extra=misc[512])


def _layer_fwd(l, x, mem, p):
    sfx = f"_l{l}"
    h = _norm_fwd(x, p["norm_g"], "norm_fwd" + sfx)
    zb = _mm(h, p["w_b"], bf16, 512, 1024, 1024, "inproj_b" + sfx)
    zf = _mm(h, p["w_f"], f32, 512, 1024, 1024, "inproj_f" + sfx)
    zs = _mm(h, p["w_s"], f32, 512, 128, 1024, "inproj_s" + sfx)
    sm = _small_prep(zs, p["par"], "small_prep" + sfx)
    S = x.shape[0]
    gt4 = jnp.transpose(sm[:, 8:12]).reshape(4, S // CH, 1, CH)
    qa, ka = _fox_prep(zb, sm, "fox_prep" + sfx)
    oa, lse, lse_t = _fox_fwd(qa, ka, zb, "fox_fwd" + sfx)
    gqkv = _gdn_prep(zf, p["conv_w"], "gdn_prep" + sfx)
    ob, tinv, states = _gdn_fwd(gqkv, sm, gt4, "gdn_fwd" + sfx)
    memn = _norm_fwd(mem, p["mem_norm_g"], "mem_norm" + sfx)
    kv = _mm(memn, p["w_mem_kv"], bf16, 256, 1024, 1024, "mem_kv" + sfx)
    xo, ycat, merged = _merge_fwd(x, oa, ob, zb, zf, kv, p["b_merge"], p["gdn_norm_g"], p["w_branch"], p["w_out"], "merge_fwd" + sfx)
    saved = dict(x=x, h=h, zb=zb, zf=zf, zs=zs, sm=sm, qa=qa, ka=ka, gt4=gt4, oa=oa, lse=lse, lse_t=lse_t, gqkv=gqkv, ob=ob, tinv=tinv,
                 states=states, memn=memn, kv=kv, ycat=ycat, merged=merged)
    return xo, saved


def _layer_bwd(l, dout, mem, p, s):
    sfx = f"_l{l}"
    dproj, dzf2, dmq, doa, doab, dob, dkv, dbm, dgg = _merge_bwd(
        dout, s["ycat"], s["oa"], s["ob"], s["zb"], s["zf"], s["kv"], p["b_merge"], p["gdn_norm_g"],
        p["w_branch"], p["w_branch_t"], p["w_out_t"], "merge_bwd" + sfx)
    g = {}
    g["w_out"] = _mm(s["merged"], dout, f32, 512, 1024, 512, "dw_out" + sfx, trans_a=True)
    g["w_branch"] = jnp.stack([
        _mm(s["ycat"], dproj, f32, 512, 1024, 512, f"dw_branch{n}" + sfx, trans_a=True, a_cols=(n * 512, 512), b_cols=(n * D, D))
        for n in range(3)])
    g["b_merge"] = dbm[0]
    g["gdn_norm_g"] = dgg[0]
    g["w_mem_kv"] = _mm(s["memn"], dkv, f32, 512, 1024, 256, "dw_mem_kv" + sfx, trans_a=True)
    dmemn = _mm(dkv, p["w_mem_kv_t"], f32, 256, 1024, 1024, "dmem_n" + sfx)
    g["mem_norm_g"] = _norm_bwd(mem, p["mem_norm_g"], dmemn, None, "mem_norm_bwd" + sfx)[0]
    dgqkv, dsm = _gdn_bwd(s["gqkv"], s["sm"], s["gt4"], s["tinv"], s["states"], dob, "gdn_bwd" + sfx)
    dbqkv, dcw = _gdn_prep_bwd(s["zf"], p["conv_w"], dgqkv, "gdn_prep_bwd" + sfx)
    g["conv_w"] = dcw[0:4]
    dq, delta, dfr = _fox_bwd_dq(s["qa"], s["ka"], s["zb"], s["lse"], s["oa"], doa, "fox_bwd_dq" + sfx)
    dk, dv, dfc = _fox_bwd_dkv(s["qa"], s["ka"], s["zb"], doab, _head_rows(s["lse_t"]), _head_rows(delta), "fox_bwd_dkv" + sfx)
    dzs, sacc = _small_bwd(s["zs"], p["par"], dfr, dfc, dsm, "small_bwd" + sfx)
    g["b_fg"], g["a_log"], g["dt_bias"] = sacc[0, 0:8], sacc[1, 8:12], sacc[2, 8:12]
    dz = jnp.concatenate([dq, dk, dv, dmq, dbqkv, dzf2, dzs], axis=1)
    dh = _mm(dz, p["w_all_t"], f32, 512, 1024, 1664, "dh" + sfx)
    g["w_in"] = _mm(s["h"], dz, f32, 512, 1664, 512, "dw_in" + sfx, trans_a=True)
    dx, dng = _norm_bwd(s["x"], p["norm_g"], dh, dout, "norm_bwd" + sfx)
    g["norm_g"] = dng[0]
    return dx, g


def kernel(x, mem, norm_g, w_in, b_fg, b_merge, conv_w, a_log, dt_bias, gdn_norm_g, mem_norm_g, w_mem_kv, w_branch, w_out, final_norm_g, loss_target, m_norm_g, m_w_in, m_b_fg, m_b_merge, m_conv_w, m_a_log, m_dt_bias, m_gdn_norm_g, m_mem_norm_g, m_w_mem_kv, m_w_branch, m_w_out, m_final_norm_g, v_norm_g, v_w_in, v_b_fg, v_b_merge, v_conv_w, v_a_log, v_dt_bias, v_gdn_norm_g, v_mem_norm_g, v_w_mem_kv, v_w_branch, v_w_out, v_final_norm_g):
    x0, mem0, tgt = x[0], mem[0], loss_target[0]
    shard_w = dict(w_in=w_in, w_mem_kv=w_mem_kv, w_branch=w_branch, w_out=w_out, conv_w=conv_w)
    shard_m = dict(w_in=m_w_in, w_mem_kv=m_w_mem_kv, w_branch=m_w_branch, w_out=m_w_out, conv_w=m_conv_w)
    shard_v = dict(w_in=v_w_in, w_mem_kv=v_w_mem_kv, w_branch=v_w_branch, w_out=v_w_out, conv_w=v_conv_w)
    small_w = dict(norm_g=norm_g, b_fg=b_fg, b_merge=b_merge, a_log=a_log, dt_bias=dt_bias, gdn_norm_g=gdn_norm_g,
                   mem_norm_g=mem_norm_g, final_norm_g=final_norm_g)
    small_m = dict(norm_g=m_norm_g, b_fg=m_b_fg, b_merge=m_b_merge, a_log=m_a_log, dt_bias=m_dt_bias, gdn_norm_g=m_gdn_norm_g,
                   mem_norm_g=m_mem_norm_g, final_norm_g=m_final_norm_g)
    small_v = dict(norm_g=v_norm_g, b_fg=v_b_fg, b_merge=v_b_merge, a_log=v_a_log, dt_bias=v_dt_bias, gdn_norm_g=v_gdn_norm_g,
                   mem_norm_g=v_mem_norm_g, final_norm_g=v_final_norm_g)

    g_in, g_kv, g_br, g_out, conv_all = _all_gather(
        [_b(_as2d(w_in)), _b(_as2d(w_mem_kv)), _b(_as2d(w_branch)), _b(_as2d(w_out)), _as2d(conv_w)], "gather_weights")
    conv_full = jnp.transpose(conv_all.reshape(N_DEV, DEPTH, 4, 192), (1, 2, 0, 3)).reshape(DEPTH, 4, 1536)
    w_in_full = jnp.transpose(g_in.reshape(N_DEV, DEPTH, D, 1026), (1, 2, 0, 3)).reshape(DEPTH, D, N_IN)
    w_all = _perm_cols(w_in_full)
    w_kv_full = jnp.transpose(g_kv.reshape(N_DEV, DEPTH, 128, D), (1, 0, 2, 3)).reshape(DEPTH, D, D)
    w_br_full = jnp.transpose(g_br.reshape(N_DEV, DEPTH, 3, 512, 128), (1, 2, 3, 0, 4)).reshape(DEPTH, 3, 512, D)
    w_out_full = jnp.transpose(g_out.reshape(N_DEV, DEPTH, 128, D), (1, 0, 2, 3)).reshape(DEPTH, D, D)

    layers = []
    for l in range(DEPTH):
        layers.append(dict(
            norm_g=norm_g[l][None], mem_norm_g=mem_norm_g[l][None], gdn_norm_g=gdn_norm_g[l][None], b_merge=b_merge[l][None],
            par=_small_pars(b_fg[l], a_log[l], dt_bias[l]),
            conv_w=jnp.pad(conv_full[l], ((0, 4), (0, 0))),
            w_b=w_all[l][:, 0:NB], w_f=w_all[l][:, NB:NB + NF], w_s=w_all[l][:, NB + NF:], w_all_t=jnp.transpose(w_all[l]),
            w_mem_kv=w_kv_full[l], w_mem_kv_t=jnp.transpose(w_kv_full[l]),
            w_branch=w_br_full[l], w_branch_t=jnp.transpose(w_br_full[l], (0, 2, 1)),
            w_out=w_out_full[l], w_out_t=jnp.transpose(w_out_full[l])))

    acts, saved = x0, []
    for l in range(DEPTH):
        acts, s = _layer_fwd(l, acts, mem0, layers[l])
        saved.append(s)
    dx, dfg, lsum = _loss_head(acts, final_norm_g[None], tgt, "loss_head")

    grads = [None] * DEPTH
    for l in reversed(range(DEPTH)):
        dx, grads[l] = _layer_bwd(l, dx, mem0, layers[l], saved[l])
    grad_x = dx[None]

    def per_dev(name):
        return jnp.stack([grads[l][name] for l in range(DEPTH)])

    dw_in = _unperm_cols(per_dev("w_in"))
    send = dict(
        w_in=jnp.transpose(dw_in.reshape(DEPTH, D, N_DEV, 1026), (2, 0, 1, 3)).reshape(N_DEV, DEPTH * D, 1026),
        w_mem_kv=jnp.transpose(per_dev("w_mem_kv").reshape(DEPTH, N_DEV, 128, D), (1, 0, 2, 3)).reshape(N_DEV, DEPTH * 128, D),
        w_branch=jnp.transpose(per_dev("w_branch").reshape(DEPTH, 3, 512, N_DEV, 128), (3, 0, 1, 2, 4)).reshape(N_DEV, DEPTH * 3 * 512, 128),
        w_out=jnp.transpose(per_dev("w_out").reshape(DEPTH, N_DEV, 128, D), (1, 0, 2, 3)).reshape(N_DEV, DEPTH * 128, D),
        conv_w=jnp.transpose(per_dev("conv_w").reshape(DEPTH, 4, N_DEV, 192), (2, 0, 1, 3)).reshape(N_DEV, DEPTH * 4, 192))
    parts = dict(zip(_SHARDED, _exchange([_b(send[n]) for n in _SHARDED], "scatter_grads")))
    big = [{}, {}, {}, {}]
    for n in _SHARDED:
        res = _adamw(parts[n], _as2d(shard_w[n]), _as2d(shard_m[n]), _as2d(shard_v[n]), "adamw_" + n)
        for kind in range(4):
            big[kind][n] = res[kind].reshape(shard_w[n].shape)

    small_g = {k: jnp.stack([grads[l][k] for l in range(DEPTH)]) for k in ("norm_g", "b_fg", "b_merge", "a_log", "dt_bias", "gdn_norm_g", "mem_norm_g")}
    small_g["final_norm_g"] = dfg[0]
    small_g["extra"] = lsum[0, 0]
    parts_s, = _all_gather([_pack_small(small_g)], "gather_small")
    g_sm, d_sm, m_sm, v_sm = _adamw(parts_s, _pack_small(small_w), _pack_small(small_m), _pack_small(small_v), "adamw_replicated")

    sml = [_unpack_small(a) for a in (g_sm, d_sm, m_sm, v_sm)]
    loss = sml[0]["extra"]
    names = ("norm_g", "w_in", "b_fg", "b_merge", "conv_w", "a_log", "dt_bias", "gdn_norm_g", "mem_norm_g", "w_mem_kv", "w_branch", "w_out", "final_norm_g")
    outs = [loss, grad_x]
    for kind in range(4):
        for n in names:
            outs.append(big[kind][n] if n in big[kind] else sml[kind][n])
    return tuple(outs)
```

```python
import functools

import jax
import jax.numpy as jnp
from jax import lax
from jax.experimental import pallas as pl
from jax.experimental.pallas import tpu as pltpu

f32, bf16 = jnp.float32, jnp.bfloat16

D = 1024
EPS = 1e-6
CH = 64
N_DEV = 8
DEPTH = 2
FOX_SCALE = 64 ** -0.5
GDN_SCALE = 128 ** -0.5
MEM_SCALE = 128 ** -0.5
NEG = -1e30
VMEM_LIMIT = 56 * 1024 * 1024

ADAM_LR, ADAM_B1, ADAM_B2, ADAM_EPS, ADAM_WD, ADAM_STEP = 0.001, 0.9, 0.999, 1e-08, 0.01, 10

_COLS = dict(aq=(0, 512), ak=(512, 1024), av=(1024, 1536), af=(1536, 1544), az=(1544, 2056),
             bq=(2056, 2568), bk=(2568, 3080), bv=(3080, 3592), ba=(3592, 3596), bb=(3596, 3600),
             bz=(3600, 4112), mq=(4112, 4624), mz=(4624, 5136), gates=(5136, 8208))
_ORDER = ("aq", "ak", "av", "mq", "bq", "bk", "bv", "az", "bz", "mz", "gates", "af", "ba", "bb")
N_IN = 8208
NB, NF, NS = 2048, 6144, 128
N_ALL = NB + NF + NS

_SHARDED = ("w_in", "w_mem_kv", "w_branch", "w_out", "conv_w")


def _cp(sem=None, vmem=None):
    kw = {}
    if sem is not None:
        kw["dimension_semantics"] = sem
    if vmem is not None:
        kw["vmem_limit_bytes"] = vmem
    return pltpu.CompilerParams(**kw)


def _dot(a, b):
    return jnp.dot(a, b, preferred_element_type=f32)


def _dot_nt(a, b):
    return lax.dot_general(a, b, (((1,), (1,)), ((), ())), preferred_element_type=f32)


def _dot_tn(a, b):
    return lax.dot_general(a, b, (((0,), (0,)), ((), ())), preferred_element_type=f32)


def _split2(x):
    hi = x.astype(bf16)
    return hi, (x - hi.astype(f32)).astype(bf16)


def _mm3(a, b, dims):
    ah, al = _split2(a)
    bh, bl = _split2(b)
    dg = functools.partial(lax.dot_general, dimension_numbers=dims, preferred_element_type=f32)
    return dg(ah, bh) + (dg(ah, bl) + dg(al, bh))


def _hi(a, b):
    return _mm3(a, b, (((1,), (0,)), ((), ())))


def _hi_nt(a, b):
    return _mm3(a, b, (((1,), (1,)), ((), ())))


def _hi_tn(a, b):
    return _mm3(a, b, (((0,), (0,)), ((), ())))


def _hi_b(a, b):
    return _mm3(a, b, (((2,), (1,)), ((0,), (0,))))


def _b(x):
    return x.astype(bf16)


def _sig(x):
    return jax.nn.sigmoid(x)


def _silu(x):
    return x * _sig(x)


def _dsilu(x):
    s = _sig(x)
    return s * (1.0 + x * (1.0 - s))


def _softplus(x):
    return jnp.maximum(x, 0.0) + jnp.log1p(jnp.exp(-jnp.abs(x)))


def _rowsum(x):
    return jnp.sum(x, axis=1, keepdims=True)


def _colsum(x):
    return jnp.sum(x, axis=0, keepdims=True)


def _norm_fwd(x, g, name):
    M = x.shape[0]
    ts = min(M, 512)

    def body(x_ref, g_ref, h_ref):
        xv = x_ref[...]
        r = lax.rsqrt(jnp.mean(xv * xv, axis=-1, keepdims=True) + EPS)
        h_ref[...] = _b(xv * r * g_ref[...])

    return pl.pallas_call(
        body, grid=(M // ts,), out_shape=jax.ShapeDtypeStruct((M, D), bf16),
        in_specs=[pl.BlockSpec((ts, D), lambda i: (i, 0)), pl.BlockSpec((1, D), lambda i: (0, 0))],
        out_specs=pl.BlockSpec((ts, D), lambda i: (i, 0)), compiler_params=_cp(("parallel",)), name=name)(x, g)


def _norm_bwd(x, g, dh, dres, name):
    M = x.shape[0]
    ts = min(M, 512)
    with_dx = dres is not None

    def body(*refs):
        if with_dx:
            x_ref, g_ref, dh_ref, dres_ref, dx_ref, dg_ref = refs
        else:
            x_ref, g_ref, dh_ref, dg_ref = refs
        i = pl.program_id(0)
        xv = x_ref[...]
        r = lax.rsqrt(jnp.mean(xv * xv, axis=-1, keepdims=True) + EPS)
        xh = xv * r
        dh = dh_ref[...].astype(f32)
        part = jnp.broadcast_to(_colsum(dh * xh), (8, D))

        @pl.when(i == 0)
        def _():
            dg_ref[...] = part

        @pl.when(i > 0)
        def _():
            dg_ref[...] += part

        if with_dx:
            dxh = dh * g_ref[...]
            dx_ref[...] = dres_ref[...] + r * (dxh - xh * jnp.mean(dxh * xh, axis=-1, keepdims=True))

    tile = pl.BlockSpec((ts, D), lambda i: (i, 0))
    gspec = pl.BlockSpec((1, D), lambda i: (0, 0))
    acc = pl.BlockSpec((8, D), lambda i: (0, 0))
    if with_dx:
        return pl.pallas_call(
            body, grid=(M // ts,), out_shape=(jax.ShapeDtypeStruct((M, D), f32), jax.ShapeDtypeStruct((8, D), f32)),
            in_specs=[tile, gspec, tile, tile], out_specs=(tile, acc), compiler_params=_cp(("arbitrary",)), name=name)(x, g, dh, dres)
    return pl.pallas_call(
        body, grid=(M // ts,), out_shape=jax.ShapeDtypeStruct((8, D), f32),
        in_specs=[tile, gspec, tile], out_specs=acc, compiler_params=_cp(("arbitrary",)), name=name)(x, g, dh)


def _loss_head(x, g, tgt, name):
    M = x.shape[0]
    ts = min(M, 512)

    def body(x_ref, g_ref, t_ref, dx_ref, dg_ref, ls_ref):
        i = pl.program_id(0)
        xv = x_ref[...]
        gv = g_ref[...]
        r = lax.rsqrt(jnp.mean(xv * xv, axis=-1, keepdims=True) + EPS)
        xh = xv * r
        e = xh * gv - t_ref[...]
        lpart = 0.5 * jnp.sum(jnp.mean(e * e, axis=-1, keepdims=True), axis=0, keepdims=True)
        dy = e * (1.0 / D)
        dgp = jnp.broadcast_to(_colsum(dy * xh), (8, D))
        lp = jnp.broadcast_to(lpart, (8, 128))

        @pl.when(i == 0)
        def _():
            dg_ref[...] = dgp
            ls_ref[...] = lp

        @pl.when(i > 0)
        def _():
            dg_ref[...] += dgp
            ls_ref[...] += lp

        dxh = dy * gv
        dx_ref[...] = r * (dxh - xh * jnp.mean(dxh * xh, axis=-1, keepdims=True))

    tile = pl.BlockSpec((ts, D), lambda i: (i, 0))
    return pl.pallas_call(
        body, grid=(M // ts,),
        out_shape=(jax.ShapeDtypeStruct((M, D), f32), jax.ShapeDtypeStruct((8, D), f32), jax.ShapeDtypeStruct((8, 128), f32)),
        in_specs=[tile, pl.BlockSpec((1, D), lambda i: (0, 0)), tile],
        out_specs=(tile, pl.BlockSpec((8, D), lambda i: (0, 0)), pl.BlockSpec((8, 128), lambda i: (0, 0))),
        compiler_params=_cp(("arbitrary",)), name=name)(x, g, tgt)


def _mm(a, b, out_dtype, tm, tn, tk, name, trans_a=False, a_cols=None, b_cols=None):
    if trans_a:
        K, M = a.shape
    else:
        M, K = a.shape
    N = b.shape[1]
    a0, b0 = 0, 0
    if a_cols is not None:
        a0, M = a_cols
    if b_cols is not None:
        b0, N = b_cols
    tm, tn, tk = min(tm, M), min(tn, N), min(tk, K)
    nk = K // tk
    a0, b0 = a0 // tm, b0 // tn

    def body(a_ref, b_ref, o_ref, acc_ref):
        k = pl.program_id(2)
        av, bv = _b(a_ref[...]), _b(b_ref[...])
        part = _dot_tn(av, bv) if trans_a else _dot(av, bv)
        if nk == 1:
            o_ref[...] = part.astype(out_dtype)
        else:
            @pl.when(k == 0)
            def _():
                acc_ref[...] = part

            @pl.when(k > 0)
            def _():
                acc_ref[...] += part

            @pl.when(k == nk - 1)
            def _():
                o_ref[...] = acc_ref[...].astype(out_dtype)

    a_spec = pl.BlockSpec((tk, tm), lambda i, j, k: (k, i + a0)) if trans_a else pl.BlockSpec((tm, tk), lambda i, j, k: (i, k))
    return pl.pallas_call(
        body, grid=(M // tm, N // tn, nk), out_shape=jax.ShapeDtypeStruct((M, N), out_dtype),
        in_specs=[a_spec, pl.BlockSpec((tk, tn), lambda i, j, k: (k, j + b0))],
        out_specs=pl.BlockSpec((tm, tn), lambda i, j, k: (i, j)),
        scratch_shapes=[pltpu.VMEM((tm, tn), f32)],
        compiler_params=_cp(("parallel", "parallel", "arbitrary"), VMEM_LIMIT), name=name)(a, b)


def _small_pars(b_fg, a_log, dt_bias):
    par = jnp.zeros((8, 128), f32)
    par = par.at[0, 0:8].set(b_fg).at[1, 8:12].set(a_log).at[2, 8:12].set(dt_bias)
    return par


def _small_prep(zs, par, name):
    S = zs.shape[0]
    ts = min(S, 512)

    def body(z_ref, par_ref, o_ref, carry_ref):
        i = pl.program_id(0)

        @pl.when(i == 0)
        def _():
            carry_ref[...] = jnp.zeros_like(carry_ref)

        z = z_ref[...]
        lane = lax.broadcasted_iota(jnp.int32, (ts, 128), 1)
        row = lax.broadcasted_iota(jnp.int32, (ts, 128), 0)
        za = z + par_ref[0:1, :]
        logf = jnp.minimum(za, 0.0) - jnp.log1p(jnp.exp(-jnp.abs(za)))
        glog = -jnp.exp(par_ref[1:2, :]) * _softplus(z + par_ref[2:3, :])
        x = jnp.where(lane < 8, logf, jnp.where(lane < 12, glog, 0.0))
        pos = jnp.where(lane < 8, row, row & (CH - 1))
        s = 1
        while s < ts:
            x = x + jnp.where(pos >= s, pltpu.roll(x, s, 0), 0.0)
            s *= 2
        tot = x + carry_ref[0:1, :]
        carry_ref[...] = jnp.broadcast_to(jnp.where(lane[0:1] < 8, tot[ts - 1:ts, :], 0.0), (8, 128))
        o_ref[...] = jnp.where(lane < 8, tot, jnp.where(lane < 12, x, jnp.where(lane < 16, _sig(z), 0.0)))

    return pl.pallas_call(
        body, grid=(S // ts,), out_shape=jax.ShapeDtypeStruct((S, 128), f32),
        in_specs=[pl.BlockSpec((ts, 128), lambda i: (i, 0)), pl.BlockSpec((8, 128), lambda i: (0, 0))],
        out_specs=pl.BlockSpec((ts, 128), lambda i: (i, 0)), scratch_shapes=[pltpu.VMEM((8, 128), f32)],
        compiler_params=_cp(("arbitrary",)), name=name)(zs, par)


def _small_bwd(zs, par, dfr, dfc, dsm, name):
    S = zs.shape[0]
    ts = min(S, 512)
    nt = S // ts

    def body(z_ref, par_ref, dfr_ref, dfc_ref, dsm_ref, dz_ref, acc_ref, carry_ref):
        i = pl.program_id(0)

        @pl.when(i == 0)
        def _():
            carry_ref[...] = jnp.zeros_like(carry_ref)

        z = z_ref[...]
        dsm_v = dsm_ref[...]
        lane = lax.broadcasted_iota(jnp.int32, (ts, 128), 1)
        row = lax.broadcasted_iota(jnp.int32, (ts, 128), 0)
        df = jnp.transpose(jnp.concatenate([dfr_ref[...], jnp.zeros((120, ts), f32)], axis=0))
        for p in range(4):
            dpair = dfc_ref[p]
            df = df - jnp.where(lane == 2 * p, dpair[:, 0:1], jnp.where(lane == 2 * p + 1, dpair[:, 64:65], 0.0))
        x = jnp.where(lane < 8, df, jnp.where(lane < 12, dsm_v, 0.0))
        pos = jnp.where(lane < 8, row, row & (CH - 1))
        seg = jnp.where(lane < 8, ts, CH)
        s = 1
        while s < ts:
            x = x + jnp.where(pos + s < seg, pltpu.roll(x, ts - s, 0), 0.0)
            s *= 2
        tot = x + carry_ref[0:1, :]
        carry_ref[...] = jnp.broadcast_to(jnp.where(lane[0:1] < 8, tot[0:1, :], 0.0), (8, 128))
        za = z + par_ref[0:1, :]
        daf = tot * _sig(-za)
        zb = z + par_ref[2:3, :]
        nea = -jnp.exp(par_ref[1:2, :])
        glog = nea * _softplus(zb)
        dba = x * nea * _sig(zb)
        beta = _sig(z)
        dbb = dsm_v * beta * (1.0 - beta)
        dz_ref[...] = _b(jnp.where(lane < 8, daf, jnp.where(lane < 12, dba, jnp.where(lane < 16, dbb, 0.0))))
        r0 = _colsum(jnp.where(lane < 8, daf, 0.0))
        r1 = _colsum(jnp.where((lane >= 8) & (lane < 12), x * glog, 0.0))
        r2 = _colsum(jnp.where((lane >= 8) & (lane < 12), dba, 0.0))
        r8 = lax.broadcasted_iota(jnp.int32, (8, 128), 0)
        part = jnp.where(r8 == 0, r0, jnp.where(r8 == 1, r1, jnp.where(r8 == 2, r2, 0.0)))

        @pl.when(i == 0)
        def _():
            acc_ref[...] = part

        @pl.when(i > 0)
        def _():
            acc_ref[...] += part

    rev = pl.BlockSpec((ts, 128), lambda i: (nt - 1 - i, 0))
    rev4 = pl.BlockSpec((4, ts, 128), lambda i: (0, nt - 1 - i, 0))
    c8 = pl.BlockSpec((8, 128), lambda i: (0, 0))
    return pl.pallas_call(
        body, grid=(nt,), out_shape=(jax.ShapeDtypeStruct((S, 128), bf16), jax.ShapeDtypeStruct((8, 128), f32)),
        in_specs=[rev, c8, pl.BlockSpec((8, ts), lambda i: (0, nt - 1 - i)), rev4, rev], out_specs=(rev, c8),
        scratch_shapes=[pltpu.VMEM((8, 128), f32)],
        compiler_params=_cp(("arbitrary",)), name=name)(zs, par, dfr, dfc, dsm)


def _split3(x):
    hi = _b(x).astype(f32)
    r = x - hi
    mid = _b(r).astype(f32)
    return hi, mid, _b(r - mid).astype(f32)


def _fox_prep(zb, sm, name):
    S = zb.shape[0]
    ts = min(S, 512)

    def body(q_ref, k_ref, f_ref, qa_ref, ka_ref):
        lane = lax.broadcasted_iota(jnp.int32, (ts, 128), 1)
        f = f_ref[...]
        for p in range(4):
            q = q_ref[:, p * 128:(p + 1) * 128].astype(f32) * FOX_SCALE
            k = k_ref[:, p * 128:(p + 1) * 128].astype(f32)
            for h in (0, 1):
                hi, mid, lo = _split3(f[:, 2 * p + h:2 * p + h + 1])
                own = (lane < 64) if h == 0 else (lane >= 64)
                o = 64 if h == 0 else 0
                ones_lo = (lane >= o) & (lane < o + 3)
                ones_hi = (lane >= o + 3) & (lane < o + 6)
                qaug = jnp.where(lane == o, hi, jnp.where(lane == o + 1, mid, jnp.where(lane == o + 2, lo, jnp.where(ones_hi, 1.0, 0.0))))
                kaug = jnp.where(lane == o + 3, -hi, jnp.where(lane == o + 4, -mid, jnp.where(lane == o + 5, -lo, jnp.where(ones_lo, 1.0, 0.0))))
                qa_ref[2 * p + h] = _b(jnp.where(own, q, qaug))
                ka_ref[2 * p + h] = _b(jnp.where(own, k, kaug))

    out = jax.ShapeDtypeStruct((8, S, 128), bf16)
    return pl.pallas_call(
        body, grid=(S // ts,), out_shape=(out, out),
        in_specs=[pl.BlockSpec((ts, 512), lambda i: (i, 0)), pl.BlockSpec((ts, 512), lambda i: (i, 1)), pl.BlockSpec((ts, 128), lambda i: (i, 0))],
        out_specs=(pl.BlockSpec((8, ts, 128), lambda i: (0, i, 0)), pl.BlockSpec((8, ts, 128), lambda i: (0, i, 0))),
        compiler_params=_cp(("parallel",)), name=name)(zb, zb, sm)


def _pair_rows(a, T):
    at = jnp.transpose(a)
    r8 = lax.broadcasted_iota(jnp.int32, (8, T), 0)
    return jnp.where(r8 == 0, at[0:1, :], at[64:65, :])


def _fox_fwd(qa, ka, zb, name):
    S = zb.shape[0]
    T = min(S, 1024)

    def body(qa_ref, ka_ref, v_ref, o_ref, lset_ref, m_ref, l_ref, acc_ref):
        i = pl.program_id(1)
        m_ref[...] = jnp.full_like(m_ref, NEG)
        l_ref[...] = jnp.zeros_like(l_ref)
        acc_ref[...] = jnp.zeros_like(acc_ref)
        row = lax.broadcasted_iota(jnp.int32, (T, T), 0)
        col = lax.broadcasted_iota(jnp.int32, (T, T), 1)

        def tile(j, masked):
            off = pl.multiple_of(j * T, T)
            vs = v_ref[pl.ds(off, T), :]
            for h in (0, 1):
                s = _dot_nt(qa_ref[h], ka_ref[h, pl.ds(off, T), :])
                if masked:
                    s = jnp.where(row >= col, s, NEG)
                m_old = m_ref[h]
                m_new = jnp.maximum(m_old, jnp.max(s, axis=1, keepdims=True))
                alpha = jnp.exp(m_old - m_new)
                pr = jnp.exp(s - jnp.tile(m_new, (1, T // 128)))
                l_ref[h] = alpha * l_ref[h] + _rowsum(pr)
                acc_ref[h] = alpha * acc_ref[h] + _dot(_b(pr), vs)
                m_ref[h] = m_new

        def step(j, c):
            tile(j, False)
            return c

        lax.fori_loop(0, i, step, 0)
        tile(i, True)
        lane2 = lax.broadcasted_iota(jnp.int32, (T, 128), 1)
        o_ref[...] = jnp.where(lane2 < 64, acc_ref[0] / l_ref[0], acc_ref[1] / l_ref[1])
        lse = jnp.where(lane2 < 64, m_ref[0] + jnp.log(l_ref[0]), m_ref[1] + jnp.log(l_ref[1]))
        lset_ref[0] = _pair_rows(lse, T)

    return pl.pallas_call(
        body, grid=(4, S // T),
        out_shape=(jax.ShapeDtypeStruct((S, 512), f32), jax.ShapeDtypeStruct((4, 8, S), f32)),
        in_specs=[pl.BlockSpec((2, T, 128), lambda p, i: (p, i, 0)), pl.BlockSpec((2, S, 128), lambda p, i: (p, 0, 0)),
                  pl.BlockSpec((S, 128), lambda p, i: (0, 8 + p))],
        out_specs=(pl.BlockSpec((T, 128), lambda p, i: (i, p)), pl.BlockSpec((1, 8, T), lambda p, i: (p, 0, i))),
        scratch_shapes=[pltpu.VMEM((2, T, 128), f32), pltpu.VMEM((2, T, 128), f32), pltpu.VMEM((2, T, 128), f32)],
        compiler_params=_cp(("arbitrary", "arbitrary"), VMEM_LIMIT), name=name)(qa, ka, zb)


def _fox_bwd(qa, ka, zb, dob, lse_t, dl_t, name, tile_rows=1024):
    S = zb.shape[0]
    T = min(S, tile_rows)
    nq = S // T

    def body(ka_ref, v_ref, qa_ref, do_ref, lt_ref, dt_ref, dq_ref, dk_ref, dv_ref, dfc_ref, dfr_ref, dqa_ref, dka_ref, dva_ref, fs_ref):
        p, j = pl.program_id(0), pl.program_id(1)
        lane1 = lax.broadcasted_iota(jnp.int32, (1, 128), 1)
        lane2 = lax.broadcasted_iota(jnp.int32, (T, 128), 1)
        hm = (lane1 < 64, lane1 >= 64)
        v = v_ref[...]
        vsm = [jnp.where(hm[h], v, jnp.zeros_like(v)) for h in (0, 1)]
        ksm = [jnp.where(hm[h], ka_ref[h], jnp.zeros_like(v)) for h in (0, 1)]

        @pl.when(j == 0)
        def _():
            dqa_ref[...] = jnp.zeros_like(dqa_ref)
            dfr_ref[...] = jnp.zeros_like(dfr_ref)

        dka_ref[...] = jnp.zeros_like(dka_ref)
        dva_ref[...] = jnp.zeros_like(dva_ref)
        fs_ref[...] = jnp.zeros_like(fs_ref)
        row = lax.broadcasted_iota(jnp.int32, (T, T), 0)
        col = lax.broadcasted_iota(jnp.int32, (T, T), 1)

        def tile(i, masked):
            off = pl.multiple_of(i * T, T)
            dot_ = do_ref[pl.ds(off, T), :]
            dq_part = None
            for h in (0, 1):
                hr = pl.ds(2 * p + h, 1)
                qt = qa_ref[h, pl.ds(off, T), :]
                s_t = _dot_nt(ka_ref[h], qt)
                if masked:
                    s_t = jnp.where(col >= row, s_t, NEG)
                p_t = jnp.exp(s_t - lt_ref[hr, pl.ds(off, T)])
                dva_ref[h] += _dot(_b(p_t), dot_)
                dp_t = _dot_nt(vsm[h], dot_)
                ds_t = p_t * (dp_t - dt_ref[hr, pl.ds(off, T)])
                dsb = _b(ds_t)
                dka_ref[h] += _dot(dsb, qt)
                fs_ref[h] += _rowsum(ds_t)
                dfr_ref[0, pl.ds(h, 1), pl.ds(off, T)] += _colsum(ds_t)
                dqh = _dot_tn(dsb, ksm[h])
                dq_part = dqh if dq_part is None else dq_part + dqh
            dqa_ref[pl.ds(off, T), :] += dq_part

        def step(i, c):
            tile(i, False)
            return c

        tile(j, True)
        lax.fori_loop(j + 1, nq, step, 0)
        dk_ref[...] = _b(jnp.where(lane2 < 64, dka_ref[0], dka_ref[1]))
        dv_ref[...] = _b(jnp.where(lane2 < 64, dva_ref[0], dva_ref[1]))
        dfc_ref[0] = jnp.where(lane2 < 64, fs_ref[0], fs_ref[1])
        dq_ref[...] = _b(dqa_ref[pl.ds(pl.multiple_of(j * T, T), T), :] * FOX_SCALE)

    one = pl.Buffered(1)
    res = pl.BlockSpec((8, S), lambda p, j: (0, 0), pipeline_mode=one)
    tk = pl.BlockSpec((T, 128), lambda p, j: (j, p))
    return pl.pallas_call(
        body, grid=(4, nq),
        out_shape=(jax.ShapeDtypeStruct((S, 512), bf16), jax.ShapeDtypeStruct((S, 512), bf16), jax.ShapeDtypeStruct((S, 512), bf16),
                   jax.ShapeDtypeStruct((4, S, 128), f32), jax.ShapeDtypeStruct((4, 8, S), f32)),
        in_specs=[pl.BlockSpec((2, T, 128), lambda p, j: (p, j, 0)), pl.BlockSpec((T, 128), lambda p, j: (j, 8 + p)),
                  pl.BlockSpec((2, S, 128), lambda p, j: (p, 0, 0), pipeline_mode=one),
                  pl.BlockSpec((S, 128), lambda p, j: (0, p), pipeline_mode=one), res, res],
        out_specs=(tk, tk, tk, pl.BlockSpec((1, T, 128), lambda p, j: (p, j, 0)),
                   pl.BlockSpec((1, 8, S), lambda p, j: (p, 0, 0))),
        scratch_shapes=[pltpu.VMEM((S, 128), f32), pltpu.VMEM((2, T, 128), f32), pltpu.VMEM((2, T, 128), f32), pltpu.VMEM((2, T, 1), f32)],
        compiler_params=_cp(("arbitrary", "arbitrary"), VMEM_LIMIT), name=name)(ka, zb, qa, dob, lse_t, dl_t)


def _head_rows(a):
    return a[:, 0:2, :].reshape(8, a.shape[2])


def _conv_taps(ext, x, w_ref, ts):
    y = x * w_ref[3:4, :]
    shifted = []
    for k in (1, 2, 3):
        xs = pltpu.roll(ext, k, 0)[8:]
        shifted.append(xs)
        y = y + xs * w_ref[3 - k:4 - k, :]
    return y, shifted


def _gdn_prep(zf, cw, name):
    S = zf.shape[0]
    ts = min(S, 512)

    def body(x_ref, w_ref, o_ref, tail_ref):
        i = pl.program_id(0)

        @pl.when(i == 0)
        def _():
            tail_ref[...] = jnp.zeros_like(tail_ref)

        x = x_ref[...]
        ext = jnp.concatenate([tail_ref[...], x], axis=0)
        y, _ = _conv_taps(ext, x, w_ref, ts)
        tail_ref[...] = x[ts - 8:, :]
        a = _silu(y)
        for hb in range(12):
            blk = a[:, hb * 128:(hb + 1) * 128]
            if hb < 8:
                blk = blk * lax.rsqrt(_rowsum(blk * blk) + EPS)
            if hb < 4:
                blk = blk * GDN_SCALE
            o_ref[:, hb * 128:(hb + 1) * 128] = blk

    return pl.pallas_call(
        body, grid=(S // ts,), out_shape=jax.ShapeDtypeStruct((S, 1536), f32),
        in_specs=[pl.BlockSpec((ts, 1536), lambda i: (i, 0)), pl.BlockSpec((8, 1536), lambda i: (0, 0))],
        out_specs=pl.BlockSpec((ts, 1536), lambda i: (i, 0)), scratch_shapes=[pltpu.VMEM((8, 1536), f32)],
        compiler_params=_cp(("arbitrary",), VMEM_LIMIT), name=name)(zf, cw)


def _gdn_prep_bwd(zf, cw, dg, name):
    S = zf.shape[0]
    ts = min(S, 512)
    nt = S // ts

    def body(x_ref, xp_ref, w_ref, dg_ref, dx_ref, dw_ref, head_ref):
        i = pl.program_id(0)

        @pl.when(i == 0)
        def _():
            head_ref[...] = jnp.zeros_like(head_ref)

        x = x_ref[...]
        prev = jnp.where(i == nt - 1, 0.0, xp_ref[...])
        ext = jnp.concatenate([prev, x], axis=0)
        y, shifted = _conv_taps(ext, x, w_ref, ts)
        a = _silu(y)
        das = []
        for hb in range(12):
            blk = a[:, hb * 128:(hb + 1) * 128]
            d = dg_ref[:, hb * 128:(hb + 1) * 128]
            if hb < 4:
                d = d * GDN_SCALE
            if hb < 8:
                r = lax.rsqrt(_rowsum(blk * blk) + EPS)
                n = blk * r
                d = r * (d - n * _rowsum(d * n))
            das.append(d)
        dy = jnp.concatenate(das, axis=1) * _dsilu(y)
        extd = jnp.concatenate([dy, head_ref[...]], axis=0)
        dx = dy * w_ref[3:4, :]
        for k in (1, 2, 3):
            dx = dx + pltpu.roll(extd, ts + 8 - k, 0)[:ts] * w_ref[3 - k:4 - k, :]
        head_ref[...] = dy[0:8, :]
        dx_ref[...] = _b(dx)
        r8 = lax.broadcasted_iota(jnp.int32, (8, 1536), 0)
        part = jnp.where(r8 == 3, _colsum(dy * x), 0.0)
        for k in (1, 2, 3):
            part = jnp.where(r8 == 3 - k, _colsum(dy * shifted[k - 1]), part)

        @pl.when(i == 0)
        def _():
            dw_ref[...] = part

        @pl.when(i > 0)
        def _():
            dw_ref[...] += part

    rev = pl.BlockSpec((ts, 1536), lambda i: (nt - 1 - i, 0))
    prev8 = pl.BlockSpec((8, 1536), lambda i: (jnp.maximum((nt - 1 - i) * (ts // 8) - 1, 0), 0))
    w8 = pl.BlockSpec((8, 1536), lambda i: (0, 0))
    return pl.pallas_call(
        body, grid=(nt,), out_shape=(jax.ShapeDtypeStruct((S, 1536), bf16), jax.ShapeDtypeStruct((8, 1536), f32)),
        in_specs=[rev, prev8, w8, rev], out_specs=(rev, w8), scratch_shapes=[pltpu.VMEM((8, 1536), f32)],
        compiler_params=_cp(("arbitrary",), VMEM_LIMIT), name=name)(zf, zf, cw, dg)


def _tri_inv(a, row, col):
    same = (row >> 4) == (col >> 4)
    dm = jnp.where(same, a, 0.0)
    lo = a - dm
    eye = jnp.where(row == col, 1.0, 0.0)
    d2 = _hi_b(dm, dm)
    d4 = _hi_b(d2, d2)
    d8 = _hi_b(d4, d4)
    x0 = _hi_b(_hi_b(eye - dm, eye + d2), _hi_b(eye + d4, eye + d8))
    n = _hi_b(x0, lo)
    n2 = _hi_b(n, n)
    return _hi_b(_hi_b(eye - n, eye + n2), x0)


def _bd(a, b):
    return lax.dot_general(a, b, (((2,), (1,)), ((0,), (0,))), preferred_element_type=f32)


def _bd_nt(a, b):
    return lax.dot_general(a, b, (((2,), (2,)), ((0,), (0,))), preferred_element_type=f32)


def _bd_tn(a, b):
    return lax.dot_general(a, b, (((1,), (1,)), ((0,), (0,))), preferred_element_type=f32)


def _hi_b_nt(a, b):
    return _mm3(a, b, (((2,), (2,)), ((0,), (0,))))


def _hi_b_tn(a, b):
    return _mm3(a, b, (((1,), (1,)), ((0,), (0,))))


def _gdn_local(x_ref, sm_ref, gt_ref, row, col, cps=1):
    idx = [(c, h) for c in range(cps) for h in range(4)]

    def rows(c):
        return slice(c * CH, (c + 1) * CH)

    q = jnp.stack([x_ref[rows(c), h * 128:(h + 1) * 128] for c, h in idx])
    k = jnp.stack([x_ref[rows(c), 512 + h * 128:512 + (h + 1) * 128] for c, h in idx])
    v = jnp.stack([x_ref[rows(c), 1024 + h * 128:1024 + (h + 1) * 128] for c, h in idx])
    gc = jnp.stack([sm_ref[rows(c), 8 + h:9 + h] for c, h in idx])
    beta = jnp.stack([sm_ref[rows(c), 12 + h:13 + h] for c, h in idx])
    gr = jnp.stack([gt_ref[h, c] for c, h in idx])
    eg = jnp.exp(gc)
    gl = gc[:, CH - 1:CH, :]
    dec = jnp.exp(gl - gc)
    gm = gc - gr
    gam_i = jnp.exp(jnp.where(row >= col, gm, -jnp.inf))
    gam_s = jnp.where(row > col, gam_i, 0.0)
    kb = k * beta
    return dict(q=q, k=k, v=v, beta=beta, eg=eg, egl=jnp.exp(gl), dec=dec, gam_i=gam_i, gam_s=gam_s,
                kb=kb, vb=v * beta, kbg=kb * eg, qdec=q * eg, kdec=k * dec,
                a=_bd_nt(_b(kb), _b(k)) * gam_s, aqk=_bd_nt(_b(q), _b(k)) * gam_i)


GDN_FWD_CHUNKS = 4


def _gdn_fwd(gqkv, sm, gt4, name, cps=GDN_FWD_CHUNKS):
    S = gqkv.shape[0]
    N = S // CH
    cps = min(cps, N)
    R = cps * CH

    def body(x_ref, sm_ref, gt_ref, o_ref, t_ref, st_ref, s_ref):
        n = pl.program_id(0)

        @pl.when(n == 0)
        def _():
            s_ref[...] = jnp.zeros_like(s_ref)

        row = lax.broadcasted_iota(jnp.int32, (CH, CH), 0)
        col = lax.broadcasted_iota(jnp.int32, (CH, CH), 1)
        c = _gdn_local(x_ref, sm_ref, gt_ref, row, col, cps)
        t = _tri_inv(c["a"], row, col)
        uw = _hi_b(t, jnp.concatenate([c["vb"], c["kbg"]], axis=2))
        u, w = uw[:, :, :128], uw[:, :, 128:]
        for ci in range(cps):
            sl = slice(4 * ci, 4 * ci + 4)
            rs = slice(ci * CH, (ci + 1) * CH)
            st = s_ref[...]
            st_ref[ci] = st
            sb = _b(st)
            vnew = u[sl] - _bd(_b(w[sl]), sb)
            o = _bd(_b(c["qdec"][sl]), sb) + _bd(_b(c["aqk"][sl]), _b(vnew))
            for h in range(4):
                o_ref[rs, h * 128:(h + 1) * 128] = o[h]
                t_ref[h, rs, :] = t[4 * ci + h]
            s_ref[...] = st * c["egl"][sl] + _bd_tn(_b(c["kdec"][sl]), _b(vnew))

    return pl.pallas_call(
        body, grid=(N // cps,),
        out_shape=(jax.ShapeDtypeStruct((S, 512), f32), jax.ShapeDtypeStruct((4, S, CH), f32), jax.ShapeDtypeStruct((N, 4, 128, 128), f32)),
        in_specs=[pl.BlockSpec((R, 1536), lambda n: (n, 0)), pl.BlockSpec((R, 128), lambda n: (n, 0)),
                  pl.BlockSpec((4, cps, 1, CH), lambda n: (0, n, 0, 0))],
        out_specs=(pl.BlockSpec((R, 512), lambda n: (n, 0)), pl.BlockSpec((4, R, CH), lambda n: (0, n, 0)),
                   pl.BlockSpec((cps, 4, 128, 128), lambda n: (n, 0, 0, 0))),
        scratch_shapes=[pltpu.VMEM((4, 128, 128), f32)], compiler_params=_cp(("arbitrary",)), name=name)(gqkv, sm, gt4)


GDN_BWD_CHUNKS = 4


def _gdn_bwd(gqkv, sm, gt4, tinv, states, do, name, cps=GDN_BWD_CHUNKS):
    S = gqkv.shape[0]
    N = S // CH
    cps = min(cps, N)
    R = cps * CH

    def body(x_ref, sm_ref, gt_ref, t_ref, st_ref, do_ref, dx_ref, dsm_ref, ds_ref):
        n = pl.program_id(0)

        @pl.when(n == 0)
        def _():
            ds_ref[...] = jnp.zeros_like(ds_ref)

        row = lax.broadcasted_iota(jnp.int32, (CH, CH), 0)
        col = lax.broadcasted_iota(jnp.int32, (CH, CH), 1)
        row1 = lax.broadcasted_iota(jnp.int32, (CH, 1), 0)
        lane = lax.broadcasted_iota(jnp.int32, (CH, 128), 1)
        ones = jnp.ones((4 * cps, CH, 128), f32)
        idx = [(ci, h) for ci in range(cps) for h in range(4)]
        c = _gdn_local(x_ref, sm_ref, gt_ref, row, col, cps)
        q, k, v, beta, eg = c["q"], c["k"], c["v"], c["beta"], c["eg"]
        t = jnp.stack([t_ref[h, ci * CH:(ci + 1) * CH, :] for ci, h in idx])
        uw = _hi_b(t, jnp.concatenate([c["vb"], c["kbg"]], axis=2))
        u, w = uw[:, :, :128], uw[:, :, 128:]
        st = st_ref[...].reshape(4 * cps, 128, 128)
        sb = _b(st)
        vnew = u - _bd(_b(w), sb)
        dob = _b(jnp.stack([do_ref[ci * CH:(ci + 1) * CH, h * 128:(h + 1) * 128] for ci, h in idx]))
        vnb = _b(vnew)
        dqdec = _bd_nt(dob, sb)
        daqk = jnp.where(row >= col, _bd_nt(dob, vnb), 0.0)
        qd_do = _bd_tn(_b(c["qdec"]), dob)
        aqk_do = _bd_tn(_b(c["aqk"]), dob)
        kdecb, wb = _b(c["kdec"]), _b(w)
        dvnew_l, dkdec_l, dgl_l = [None] * cps, [None] * cps, [None] * cps
        for ci in reversed(range(cps)):
            sl = slice(4 * ci, 4 * ci + 4)
            dsp = ds_ref[...]
            dspb = _b(dsp)
            dvn = _bd(kdecb[sl], dspb) + aqk_do[sl]
            dvnew_l[ci] = dvn
            dkdec_l[ci] = _bd_nt(vnb[sl], dspb)
            dgl_l[ci] = c["egl"][sl] * jnp.sum(dsp * st[sl], axis=(1, 2), keepdims=True)
            ds_ref[...] = dsp * c["egl"][sl] + qd_do[sl] - _bd_tn(wb[sl], _b(dvn))
        dvnew = jnp.concatenate(dvnew_l, axis=0)
        dkdec = jnp.concatenate(dkdec_l, axis=0)
        dgl = jnp.concatenate(dgl_l, axis=0)
        dw = -_bd_nt(_b(dvnew), sb)
        duw = _hi_b_tn(t, jnp.concatenate([dvnew, dw], axis=2))
        dvb, dkbg = duw[:, :, :128], duw[:, :, 128:]
        da = -jnp.where(row > col, _hi_b_nt(duw, uw), 0.0)
        dp = da * c["gam_s"]
        dqk = daqk * c["gam_i"]
        m = da * c["a"] + daqk * c["aqk"]
        csum = _hi_b_tn(m, ones)[:, :, 0:1]
        kk = dkdec * c["kdec"]

        def lsum(a):
            return jnp.sum(a, axis=2, keepdims=True)

        dgv = lsum(m) - csum + lsum(dqdec * c["qdec"]) - lsum(kk) + lsum(dkbg * c["kbg"])
        dgv = dgv + jnp.where(row1 == CH - 1, dgl + jnp.sum(kk, axis=(1, 2), keepdims=True), 0.0)
        dpb, dqkb = _b(dp), _b(dqk)
        dkb = _bd(dpb, _b(k)) + dkbg * eg
        dk = _bd_tn(dpb, _b(c["kb"])) + _bd_tn(dqkb, _b(q)) + dkdec * c["dec"] + dkb * beta
        dq = _bd(dqkb, _b(k)) + dqdec * eg
        dbeta = lsum(dkb * k) + lsum(dvb * v)
        dv = dvb * beta
        for ci in range(cps):
            rs = slice(ci * CH, (ci + 1) * CH)
            dsm = jnp.zeros((CH, 128), f32)
            for h in range(4):
                b = 4 * ci + h
                dx_ref[rs, h * 128:(h + 1) * 128] = dq[b]
                dx_ref[rs, 512 + h * 128:512 + (h + 1) * 128] = dk[b]
                dx_ref[rs, 1024 + h * 128:1024 + (h + 1) * 128] = dv[b]
                dsm = jnp.where(lane == 8 + h, dgv[b], jnp.where(lane == 12 + h, dbeta[b], dsm))
            dsm_ref[rs, :] = dsm

    G = N // cps
    return pl.pallas_call(
        body, grid=(G,), out_shape=(jax.ShapeDtypeStruct((S, 1536), f32), jax.ShapeDtypeStruct((S, 128), f32)),
        in_specs=[pl.BlockSpec((R, 1536), lambda n: (G - 1 - n, 0)), pl.BlockSpec((R, 128), lambda n: (G - 1 - n, 0)),
                  pl.BlockSpec((4, cps, 1, CH), lambda n: (0, G - 1 - n, 0, 0)), pl.BlockSpec((4, R, CH), lambda n: (0, G - 1 - n, 0)),
                  pl.BlockSpec((cps, 4, 128, 128), lambda n: (G - 1 - n, 0, 0, 0)), pl.BlockSpec((R, 512), lambda n: (G - 1 - n, 0))],
        out_specs=(pl.BlockSpec((R, 1536), lambda n: (G - 1 - n, 0)), pl.BlockSpec((R, 128), lambda n: (G - 1 - n, 0))),
        scratch_shapes=[pltpu.VMEM((4, 128, 128), f32)], compiler_params=_cp(("arbitrary",), VMEM_LIMIT), name=name)(gqkv, sm, gt4, tinv, states, do)


def _mem_attn(q, kv_ref, h):
    s = _dot_nt(q, kv_ref[:, h * 128:(h + 1) * 128]) * MEM_SCALE
    e = jnp.exp(s - jnp.max(s, axis=1, keepdims=True))
    return e / _rowsum(e)


def _gdn_out_norm(ob):
    r = lax.rsqrt(jnp.mean(ob * ob, axis=-1, keepdims=True) + EPS)
    return ob * r, r


def _merge_fwd(x, oa, ob, zb, zf, kv, b_merge, gdn_g, w_branch, w_out, name):
    S = x.shape[0]
    ts = min(S, 256)

    def body(x_ref, oa_ref, ob_ref, mq_ref, az_ref, bz_ref, mz_ref, gt_ref, kv_ref, bm_ref, gg_ref, wb_ref, wo_ref,
             xo_ref, y_ref, mg_ref):
        y_ref[:, 0:512] = _b(oa_ref[...] * _silu(az_ref[...]))
        for h in range(4):
            sl = slice(h * 128, (h + 1) * 128)
            nb, _ = _gdn_out_norm(ob_ref[:, sl])
            y_ref[:, 512 + h * 128:512 + (h + 1) * 128] = _b(nb * gg_ref[...] * _silu(bz_ref[:, sl]))
            pm = _mem_attn(mq_ref[:, sl], kv_ref, h)
            om = _dot(_b(pm), kv_ref[:, 512 + h * 128:512 + (h + 1) * 128])
            y_ref[:, 1024 + h * 128:1024 + (h + 1) * 128] = _b(om * _silu(mz_ref[:, sl]))
        merged = jnp.zeros((ts, D), f32)
        for n in range(3):
            gate = _sig(gt_ref[:, n * D:(n + 1) * D] + bm_ref[:, n * D:(n + 1) * D])
            merged = merged + gate * _dot(y_ref[:, n * 512:(n + 1) * 512], wb_ref[n])
        mb = _b(merged)
        mg_ref[...] = mb
        xo_ref[...] = x_ref[...] + _dot(mb, wo_ref[...])

    def col(w, c):
        return pl.BlockSpec((ts, w), lambda i: (i, c))

    def full(shape):
        return pl.BlockSpec(shape, lambda i: tuple(0 for _ in shape))

    return pl.pallas_call(
        body, grid=(S // ts,),
        out_shape=(jax.ShapeDtypeStruct((S, D), f32), jax.ShapeDtypeStruct((S, 1536), bf16), jax.ShapeDtypeStruct((S, D), bf16)),
        in_specs=[col(D, 0), col(512, 0), col(512, 0), col(512, 3), col(512, 3), col(512, 4), col(512, 5), col(3072, 1),
                  full((256, D)), full((1, 3072)), full((1, 128)), full((3, 512, D)), full((D, D))],
        out_specs=(col(D, 0), col(1536, 0), col(D, 0)),
        compiler_params=_cp(("parallel",), VMEM_LIMIT), name=name)(x, oa, ob, zb, zf, zf, zf, zf, kv, b_merge, gdn_g, w_branch, w_out)


def _merge_bwd(dout, ycat, oa, ob, zb, zf, kv, b_merge, gdn_g, w_branch, w_branch_t, w_out_t, name):
    S = dout.shape[0]
    ts = min(S, 256)

    def body(do_ref, y_ref, oa_ref, ob_ref, mq_ref, az_ref, bz_ref, mz_ref, gt_ref, kv_ref, bm_ref, gg_ref, wb_ref, wbt_ref, wot_ref,
             dpj_ref, dz_ref, dmq_ref, dlt_ref, doab_ref, dob_ref, dkv_ref, dbm_ref, dgg_ref):
        i = pl.program_id(0)
        dmerged = _dot(_b(do_ref[...]), wot_ref[...])
        dys = []
        dbm_parts = []
        for n in range(3):
            cs = slice(n * D, (n + 1) * D)
            gate = _sig(gt_ref[:, cs] + bm_ref[:, cs])
            proj = _dot(y_ref[:, n * 512:(n + 1) * 512], wb_ref[n])
            dlogit = dmerged * proj * gate * (1.0 - gate)
            dz_ref[:, 1536 + n * D:1536 + (n + 1) * D] = _b(dlogit)
            dbm_parts.append(_colsum(dlogit))
            dproj = _b(dmerged * gate)
            dpj_ref[:, cs] = dproj
            dys.append(_dot(dproj, wbt_ref[n]))
        dbm = jnp.broadcast_to(jnp.concatenate(dbm_parts, axis=1), (8, 3072))
        az = az_ref[...]
        oa = oa_ref[...]
        doa = dys[0] * _silu(az)
        doab_ref[...] = _b(doa)
        prod = doa * oa
        lane = lax.broadcasted_iota(jnp.int32, (ts, 128), 1)
        dl = jnp.zeros((ts, 128), f32)
        for p in range(4):
            blk = prod[:, p * 128:(p + 1) * 128]
            dl = jnp.where(lane == 2 * p, _rowsum(jnp.where(lane < 64, blk, 0.0)),
                           jnp.where(lane == 2 * p + 1, _rowsum(jnp.where(lane >= 64, blk, 0.0)), dl))
        dlt_ref[...] = jnp.transpose(dl)[0:8, :]
        dz_ref[:, 0:512] = _b(dys[0] * oa * _dsilu(az))
        gg = gg_ref[...]
        dgg = jnp.zeros((1, 128), f32)
        dkv_parts_k, dkv_parts_v = [], []
        for h in range(4):
            sl = slice(h * 128, (h + 1) * 128)
            bz = bz_ref[:, sl]
            dyb = dys[1][:, sl]
            nb, r = _gdn_out_norm(ob_ref[:, sl])
            dz_ref[:, 512 + h * 128:512 + (h + 1) * 128] = _b(dyb * nb * gg * _dsilu(bz))
            dng = dyb * _silu(bz)
            dgg = dgg + _colsum(dng * nb)
            dnb = dng * gg
            dob_ref[:, sl] = r * (dnb - nb * jnp.mean(dnb * nb, axis=-1, keepdims=True))
            mz = mz_ref[:, sl]
            dym = dys[2][:, sl]
            q = mq_ref[:, sl]
            kh = kv_ref[:, sl]
            vh = kv_ref[:, 512 + h * 128:512 + (h + 1) * 128]
            pm = _mem_attn(q, kv_ref, h)
            pmb = _b(pm)
            om = _dot(pmb, vh)
            dz_ref[:, 1024 + h * 128:1024 + (h + 1) * 128] = _b(dym * om * _dsilu(mz))
            dom = _b(dym * _silu(mz))
            dkv_parts_v.append(_dot_tn(pmb, dom))
            dpm = _dot_nt(dom, vh)
            dsm = _b(pm * (dpm - _rowsum(dpm * pm)) * MEM_SCALE)
            dmq_ref[:, sl] = _b(_dot(dsm, kh))
            dkv_parts_k.append(_dot_tn(dsm, q))
        dkv = jnp.concatenate(dkv_parts_k + dkv_parts_v, axis=1)
        dggb = jnp.broadcast_to(dgg, (8, 128))

        @pl.when(i == 0)
        def _():
            dkv_ref[...] = dkv
            dbm_ref[...] = dbm
            dgg_ref[...] = dggb

        @pl.when(i > 0)
        def _():
            dkv_ref[...] += dkv
            dbm_ref[...] += dbm
            dgg_ref[...] += dggb

    def col(w, c):
        return pl.BlockSpec((ts, w), lambda i: (i, c))

    def full(shape):
        return pl.BlockSpec(shape, lambda i: tuple(0 for _ in shape))

    return pl.pallas_call(
        body, grid=(S // ts,),
        out_shape=(jax.ShapeDtypeStruct((S, 3072), bf16), jax.ShapeDtypeStruct((S, 4608), bf16), jax.ShapeDtypeStruct((S, 512), bf16),
                   jax.ShapeDtypeStruct((8, S), f32), jax.ShapeDtypeStruct((S, 512), bf16), jax.ShapeDtypeStruct((S, 512), f32),
                   jax.ShapeDtypeStruct((256, D), f32), jax.ShapeDtypeStruct((8, 3072), f32), jax.ShapeDtypeStruct((8, 128), f32)),
        in_specs=[col(D, 0), col(1536, 0), col(512, 0), col(512, 0), col(512, 3), col(512, 3), col(512, 4), col(512, 5), col(3072, 1),
                  full((256, D)), full((1, 3072)), full((1, 128)), full((3, 512, D)), full((3, D, 512)), full((D, D))],
        out_specs=(col(3072, 0), col(4608, 0), col(512, 0), pl.BlockSpec((8, ts), lambda i: (0, i)), col(512, 0), col(512, 0),
                   full((256, D)), full((8, 3072)), full((8, 128))),
        compiler_params=_cp(("arbitrary",), VMEM_LIMIT), name=name)(
            dout, ycat, oa, ob, zb, zf, zf, zf, zf, kv, b_merge, gdn_g, w_branch, w_branch_t, w_out_t)


def _mesh_pos():
    return lax.axis_index("x"), lax.axis_index("y"), lax.axis_index("c")


def _all_gather(xs, name):
    n = len(xs)

    def body(*refs):
        x_refs, out_refs = refs[:n], refs[n:2 * n]
        send_sems, recv_sems, local_sems = refs[2 * n:]
        mx, my, mc = _mesh_pos()
        me, sibling = (mx, my, mc), (mx, my, 1 - mc)
        chips = [(1 - mx, my), (mx, 1 - my), (1 - mx, 1 - my)]

        def copy(a, k, block, to, src=None):
            px, py, pc = block
            slot = out_refs[a].at[4 * px + 2 * py + pc]
            return pltpu.make_async_remote_copy(
                src_ref=slot if src is None else src, dst_ref=slot,
                send_sem=send_sems.at[7 * a + k], recv_sem=recv_sems.at[7 * a + k], device_id=to, device_id_type=pl.DeviceIdType.MESH)

        mine = [pltpu.make_async_copy(x_refs[a], out_refs[a].at[4 * mx + 2 * my + mc], local_sems.at[a]) for a in range(n)]
        for cp in mine:
            cp.start()
        first = []
        for a in range(n):
            first.append(copy(a, 0, me, sibling, src=x_refs[a]))
            first += [copy(a, 1 + j, me, (*chip, mc), src=x_refs[a]) for j, chip in enumerate(chips)]
        for cp in first:
            cp.start()
        passed = []
        for j, chip in enumerate(chips):
            for a in range(n):
                copy(a, 1 + j, (*chip, mc), me).wait_recv()
                fwd = copy(a, 4 + j, (*chip, mc), sibling)
                fwd.start()
                passed.append(fwd)
        for a in range(n):
            copy(a, 0, sibling, me).wait_recv()
            for j, chip in enumerate(chips):
                copy(a, 4 + j, (*chip, 1 - mc), me).wait_recv()
        for cp in first + passed:
            cp.wait_send()
        for cp in mine:
            cp.wait()

    anyspec = pl.BlockSpec(memory_space=pl.ANY)
    return pl.pallas_call(
        body, out_shape=tuple(jax.ShapeDtypeStruct((N_DEV,) + x.shape, x.dtype) for x in xs),
        in_specs=[anyspec] * n, out_specs=tuple([anyspec] * n),
        scratch_shapes=[pltpu.SemaphoreType.DMA((7 * n,)), pltpu.SemaphoreType.DMA((7 * n,)), pltpu.SemaphoreType.DMA((n,))],
        name=name)(*xs)


def _exchange(sends, name):
    n = len(sends)

    def body(*refs):
        s_refs, r_refs = refs[:n], refs[n:2 * n]
        send_sems, recv_sems, local_sems = refs[2 * n:]
        mx, my, mc = _mesh_pos()
        me_id = 4 * mx + 2 * my + mc
        mine = [pltpu.make_async_copy(s_refs[a].at[me_id], r_refs[a].at[me_id], local_sems.at[a]) for a in range(n)]
        for cp in mine:
            cp.start()
        copies = []
        for k in range(1, N_DEV):
            px = 1 - mx if k & 4 else mx
            py = 1 - my if k & 2 else my
            pc = 1 - mc if k & 1 else mc
            for a in range(n):
                copies.append(pltpu.make_async_remote_copy(
                    src_ref=s_refs[a].at[4 * px + 2 * py + pc], dst_ref=r_refs[a].at[me_id],
                    send_sem=send_sems.at[7 * a + k - 1], recv_sem=recv_sems.at[7 * a + k - 1],
                    device_id=(px, py, pc), device_id_type=pl.DeviceIdType.MESH))
        for cp in copies:
            cp.start()
        for cp in copies:
            cp.wait()
        for cp in mine:
            cp.wait()

    anyspec = pl.BlockSpec(memory_space=pl.ANY)
    return pl.pallas_call(
        body, out_shape=tuple(jax.ShapeDtypeStruct(s.shape, s.dtype) for s in sends),
        in_specs=[anyspec] * n, out_specs=tuple([anyspec] * n),
        scratch_shapes=[pltpu.SemaphoreType.DMA((7 * n,)), pltpu.SemaphoreType.DMA((7 * n,)), pltpu.SemaphoreType.DMA((n,))],
        name=name)(*sends)


ADAMW_BLOCK_BYTES = 4 * 1024 * 1024


def _adamw(parts, w, m, v, name):
    _, R, C = parts.shape
    tr = R
    for t in (1024, 512, 256, 128, 64, 32, 16, 8):
        if R % t == 0 and N_DEV * t * C * 4 <= ADAMW_BLOCK_BYTES:
            tr = t
            break

    def body(p_ref, w_ref, m_ref, v_ref, g_ref, d_ref, nm_ref, nv_ref):
        g = p_ref[0].astype(f32)
        for j in range(1, N_DEV):
            g = g + p_ref[j].astype(f32)
        mn = ADAM_B1 * m_ref[...] + (1.0 - ADAM_B1) * g
        vn = ADAM_B2 * v_ref[...] + (1.0 - ADAM_B2) * jnp.square(g)
        m_hat = mn / (1.0 - ADAM_B1 ** ADAM_STEP)
        v_hat = vn / (1.0 - ADAM_B2 ** ADAM_STEP)
        g_ref[...] = g
        d_ref[...] = -ADAM_LR * (m_hat / (jnp.sqrt(v_hat) + ADAM_EPS) + ADAM_WD * w_ref[...])
        nm_ref[...] = mn
        nv_ref[...] = vn

    t2 = pl.BlockSpec((tr, C), lambda i: (i, 0))
    out = jax.ShapeDtypeStruct((R, C), f32)
    return pl.pallas_call(
        body, grid=(R // tr,), out_shape=(out, out, out, out),
        in_specs=[pl.BlockSpec((N_DEV, tr, C), lambda i: (0, i, 0)), t2, t2, t2], out_specs=(t2, t2, t2, t2),
        compiler_params=_cp(("parallel",), VMEM_LIMIT), name=name)(parts, w, m, v)


def _as2d(a):
    return a.reshape(-1, a.shape[-1])


def _perm_cols(w):
    parts = [w[..., _COLS[n][0]:_COLS[n][1]] for n in _ORDER]
    pad = jnp.zeros(w.shape[:-1] + (N_ALL - N_IN,), w.dtype)
    return jnp.concatenate(parts + [pad], axis=-1)


def _unperm_cols(w):
    pieces, off = {}, 0
    for n in _ORDER:
        width = _COLS[n][1] - _COLS[n][0]
        pieces[n] = w[..., off:off + width]
        off += width
    return jnp.concatenate([pieces[n] for n in sorted(_COLS, key=lambda n: _COLS[n][0])], axis=-1)


_SMALL_ROWS = 16


def _pack_small(t):
    z = jnp.zeros((D,), f32)
    misc = z.at[0:16].set(t["b_fg"].reshape(-1)).at[16:24].set(t["a_log"].reshape(-1)).at[24:32].set(t["dt_bias"].reshape(-1))
    misc = misc.at[128:384].set(t["gdn_norm_g"].reshape(-1))
    if "extra" in t:
        misc = misc.at[512].set(t["extra"])
    rows = [t["norm_g"], t["b_merge"].reshape(6, D), t["mem_norm_g"], t["final_norm_g"][None], misc[None],
            jnp.zeros((_SMALL_ROWS - 12, D), f32)]
    return jnp.concatenate(rows, axis=0)


def _unpack_small(a):
    misc = a[11]
    return dict(norm_g=a[0:2], b_merge=a[2:8].reshape(2, 3072), mem_norm_g=a[8:10], final_norm_g=a[10],
                b_fg=misc[0:16].reshape(2, 8), a_log=misc[16:24].reshape(2, 4), dt_bias=misc[24:32].reshape(2, 4),
                gdn_norm_g=misc[128:384].reshape(2, 128), extra=misc[512])


def _layer_fwd(l, x, mem, p):
    sfx = f"_l{l}"
    h = _norm_fwd(x, p["norm_g"], "norm_fwd" + sfx)
    zb = _mm(h, p["w_b"], bf16, 512, 1024, 1024, "inproj_b" + sfx)
    zf = _mm(h, p["w_f"], f32, 512, 1024, 1024, "inproj_f" + sfx)
    zs = _mm(h, p["w_s"], f32, 512, 128, 1024, "inproj_s" + sfx)
    sm = _small_prep(zs, p["par"], "small_prep" + sfx)
    S = x.shape[0]
    gt4 = jnp.transpose(sm[:, 8:12]).reshape(4, S // CH, 1, CH)
    qa, ka = _fox_prep(zb, sm, "fox_prep" + sfx)
    oa, lse_t = _fox_fwd(qa, ka, zb, "fox_fwd" + sfx)
    gqkv = _gdn_prep(zf, p["conv_w"], "gdn_prep" + sfx)
    ob, tinv, states = _gdn_fwd(gqkv, sm, gt4, "gdn_fwd" + sfx)
    memn = _norm_fwd(mem, p["mem_norm_g"], "mem_norm" + sfx)
    kv = _mm(memn, p["w_mem_kv"], bf16, 256, 1024, 1024, "mem_kv" + sfx)
    xo, ycat, merged = _merge_fwd(x, oa, ob, zb, zf, kv, p["b_merge"], p["gdn_norm_g"], p["w_branch"], p["w_out"], "merge_fwd" + sfx)
    saved = dict(x=x, h=h, zb=zb, zf=zf, zs=zs, sm=sm, qa=qa, ka=ka, gt4=gt4, oa=oa, lse_t=lse_t, gqkv=gqkv, ob=ob, tinv=tinv,
                 states=states, memn=memn, kv=kv, ycat=ycat, merged=merged)
    return xo, saved


def _layer_bwd(l, dout, mem, p, s):
    sfx = f"_l{l}"
    dproj, dzf2, dmq, delta, doab, dob, dkv, dbm, dgg = _merge_bwd(
        dout, s["ycat"], s["oa"], s["ob"], s["zb"], s["zf"], s["kv"], p["b_merge"], p["gdn_norm_g"],
        p["w_branch"], p["w_branch_t"], p["w_out_t"], "merge_bwd" + sfx)
    g = {}
    g["w_out"] = _mm(s["merged"], dout, f32, 512, 1024, 512, "dw_out" + sfx, trans_a=True)
    g["w_branch"] = jnp.stack([
        _mm(s["ycat"], dproj, f32, 512, 1024, 512, f"dw_branch{n}" + sfx, trans_a=True, a_cols=(n * 512, 512), b_cols=(n * D, D))
        for n in range(3)])
    g["b_merge"] = dbm[0]
    g["gdn_norm_g"] = dgg[0]
    g["w_mem_kv"] = _mm(s["memn"], dkv, f32, 512, 1024, 256, "dw_mem_kv" + sfx, trans_a=True)
    dmemn = _mm(dkv, p["w_mem_kv_t"], f32, 256, 1024, 1024, "dmem_n" + sfx)
    g["mem_norm_g"] = _norm_bwd(mem, p["mem_norm_g"], dmemn, None, "mem_norm_bwd" + sfx)[0]
    dgqkv, dsm = _gdn_bwd(s["gqkv"], s["sm"], s["gt4"], s["tinv"], s["states"], dob, "gdn_bwd" + sfx)
    dbqkv, dcw = _gdn_prep_bwd(s["zf"], p["conv_w"], dgqkv, "gdn_prep_bwd" + sfx)
    g["conv_w"] = dcw[0:4]
    dq, dk, dv, dfc, dfr = _fox_bwd(s["qa"], s["ka"], s["zb"], doab, _head_rows(s["lse_t"]), delta, "fox_bwd" + sfx)
    dzs, sacc = _small_bwd(s["zs"], p["par"], _head_rows(dfr), dfc, dsm, "small_bwd" + sfx)
    g["b_fg"], g["a_log"], g["dt_bias"] = sacc[0, 0:8], sacc[1, 8:12], sacc[2, 8:12]
    dz = jnp.concatenate([dq, dk, dv, dmq, dbqkv, dzf2, dzs], axis=1)
    dh = _mm(dz, p["w_all_t"], f32, 512, 1024, 1664, "dh" + sfx)
    g["w_in"] = _mm(s["h"], dz, f32, 512, 1664, 512, "dw_in" + sfx, trans_a=True)
    dx, dng = _norm_bwd(s["x"], p["norm_g"], dh, dout, "norm_bwd" + sfx)
    g["norm_g"] = dng[0]
    return dx, g


def kernel(x, mem, norm_g, w_in, b_fg, b_merge, conv_w, a_log, dt_bias, gdn_norm_g, mem_norm_g, w_mem_kv, w_branch, w_out, final_norm_g, loss_target, m_norm_g, m_w_in, m_b_fg, m_b_merge, m_conv_w, m_a_log, m_dt_bias, m_gdn_norm_g, m_mem_norm_g, m_w_mem_kv, m_w_branch, m_w_out, m_final_norm_g, v_norm_g, v_w_in, v_b_fg, v_b_merge, v_conv_w, v_a_log, v_dt_bias, v_gdn_norm_g, v_mem_norm_g, v_w_mem_kv, v_w_branch, v_w_out, v_final_norm_g):
    x0, mem0, tgt = x[0], mem[0], loss_target[0]
    shard_w = dict(w_in=w_in, w_mem_kv=w_mem_kv, w_branch=w_branch, w_out=w_out, conv_w=conv_w)
    shard_m = dict(w_in=m_w_in, w_mem_kv=m_w_mem_kv, w_branch=m_w_branch, w_out=m_w_out, conv_w=m_conv_w)
    shard_v = dict(w_in=v_w_in, w_mem_kv=v_w_mem_kv, w_branch=v_w_branch, w_out=v_w_out, conv_w=v_conv_w)
    small_w = dict(norm_g=norm_g, b_fg=b_fg, b_merge=b_merge, a_log=a_log, dt_bias=dt_bias, gdn_norm_g=gdn_norm_g,
                   mem_norm_g=mem_norm_g, final_norm_g=final_norm_g)
    small_m = dict(norm_g=m_norm_g, b_fg=m_b_fg, b_merge=m_b_merge, a_log=m_a_log, dt_bias=m_dt_bias, gdn_norm_g=m_gdn_norm_g,
                   mem_norm_g=m_mem_norm_g, final_norm_g=m_final_norm_g)
    small_v = dict(norm_g=v_norm_g, b_fg=v_b_fg, b_merge=v_b_merge, a_log=v_a_log, dt_bias=v_dt_bias, gdn_norm_g=v_gdn_norm_g,
                   mem_norm_g=v_mem_norm_g, final_norm_g=v_final_norm_g)

    g_in, g_kv, g_br, g_out, conv_all = _all_gather(
        [_b(_as2d(w_in)), _b(_as2d(w_mem_kv)), _b(_as2d(w_branch)), _b(_as2d(w_out)), _as2d(conv_w)], "gather_weights")
    conv_full = jnp.transpose(conv_all.reshape(N_DEV, DEPTH, 4, 192), (1, 2, 0, 3)).reshape(DEPTH, 4, 1536)
    w_in_full = jnp.transpose(g_in.reshape(N_DEV, DEPTH, D, 1026), (1, 2, 0, 3)).reshape(DEPTH, D, N_IN)
    w_all = _perm_cols(w_in_full)
    w_kv_full = jnp.transpose(g_kv.reshape(N_DEV, DEPTH, 128, D), (1, 0, 2, 3)).reshape(DEPTH, D, D)
    w_br_full = jnp.transpose(g_br.reshape(N_DEV, DEPTH, 3, 512, 128), (1, 2, 3, 0, 4)).reshape(DEPTH, 3, 512, D)
    w_out_full = jnp.transpose(g_out.reshape(N_DEV, DEPTH, 128, D), (1, 0, 2, 3)).reshape(DEPTH, D, D)

    layers = []
    for l in range(DEPTH):
        layers.append(dict(
            norm_g=norm_g[l][None], mem_norm_g=mem_norm_g[l][None], gdn_norm_g=gdn_norm_g[l][None], b_merge=b_merge[l][None],
            par=_small_pars(b_fg[l], a_log[l], dt_bias[l]),
            conv_w=jnp.pad(conv_full[l], ((0, 4), (0, 0))),
            w_b=w_all[l][:, 0:NB], w_f=w_all[l][:, NB:NB + NF], w_s=w_all[l][:, NB + NF:], w_all_t=jnp.transpose(w_all[l]),
            w_mem_kv=w_kv_full[l], w_mem_kv_t=jnp.transpose(w_kv_full[l]),
            w_branch=w_br_full[l], w_branch_t=jnp.transpose(w_br_full[l], (0, 2, 1)),
            w_out=w_out_full[l], w_out_t=jnp.transpose(w_out_full[l])))

    acts, saved = x0, []
    for l in range(DEPTH):
        acts, s = _layer_fwd(l, acts, mem0, layers[l])
        saved.append(s)
    dx, dfg, lsum = _loss_head(acts, final_norm_g[None], tgt, "loss_head")

    grads = [None] * DEPTH
    for l in reversed(range(DEPTH)):
        dx, grads[l] = _layer_bwd(l, dx, mem0, layers[l], saved[l])
    grad_x = dx[None]

    def per_dev(name):
        return jnp.stack([grads[l][name] for l in range(DEPTH)])

    dw_in = _unperm_cols(per_dev("w_in"))
    send = dict(
        w_in=jnp.transpose(dw_in.reshape(DEPTH, D, N_DEV, 1026), (2, 0, 1, 3)).reshape(N_DEV, DEPTH * D, 1026),
        w_mem_kv=jnp.transpose(per_dev("w_mem_kv").reshape(DEPTH, N_DEV, 128, D), (1, 0, 2, 3)).reshape(N_DEV, DEPTH * 128, D),
        w_branch=jnp.transpose(per_dev("w_branch").reshape(DEPTH, 3, 512, N_DEV, 128), (3, 0, 1, 2, 4)).reshape(N_DEV, DEPTH * 3 * 512, 128),
        w_out=jnp.transpose(per_dev("w_out").reshape(DEPTH, N_DEV, 128, D), (1, 0, 2, 3)).reshape(N_DEV, DEPTH * 128, D),
        conv_w=jnp.transpose(per_dev("conv_w").reshape(DEPTH, 4, N_DEV, 192), (2, 0, 1, 3)).reshape(N_DEV, DEPTH * 4, 192))
    parts = dict(zip(_SHARDED, _exchange([_b(send[n]) for n in _SHARDED], "scatter_grads")))
    big = [{}, {}, {}, {}]
    for n in _SHARDED:
        res = _adamw(parts[n], _as2d(shard_w[n]), _as2d(shard_m[n]), _as2d(shard_v[n]), "adamw_" + n)
        for kind in range(4):
            big[kind][n] = res[kind].reshape(shard_w[n].shape)

    small_g = {k: jnp.stack([grads[l][k] for l in range(DEPTH)]) for k in ("norm_g", "b_fg", "b_merge", "a_log", "dt_bias", "gdn_norm_g", "mem_norm_g")}
    small_g["final_norm_g"] = dfg[0]
    small_g["extra"] = lsum[0, 0]
    parts_s, = _all_gather([_pack_small(small_g)], "gather_small")
    g_sm, d_sm, m_sm, v_sm = _adamw(parts_s, _pack_small(small_w), _pack_small(small_m), _pack_small(small_v), "adamw_replicated")

    sml = [_unpack_small(a) for a in (g_sm, d_sm, m_sm, v_sm)]
    loss = sml[0]["extra"]
    names = ("norm_g", "w_in", "b_fg", "b_merge", "conv_w", "a_log", "dt_bias", "gdn_norm_g", "mem_norm_g", "w_mem_kv", "w_branch", "w_out", "final_norm_g")
    outs = [loss, grad_x]
    for kind in range(4):
        for n in names:
            outs.append(big[kind][n] if n in big[kind] else sml[kind][n])
    return tuple(outs)
```

```python
import functools

import jax
import jax.numpy as jnp
from jax import lax
from jax.experimental import pallas as pl
from jax.experimental.pallas import tpu as pltpu

f32, bf16 = jnp.float32, jnp.bfloat16

D = 1024
EPS = 1e-6
CH = 64
N_DEV = 8
DEPTH = 2
FOX_SCALE = 64 ** -0.5
GDN_SCALE = 128 ** -0.5
MEM_SCALE = 128 ** -0.5
NEG = -1e30
VMEM_LIMIT = 56 * 1024 * 1024

ADAM_LR, ADAM_B1, ADAM_B2, ADAM_EPS, ADAM_WD, ADAM_STEP = 0.001, 0.9, 0.999, 1e-08, 0.01, 10

_COLS = dict(aq=(0, 512), ak=(512, 1024), av=(1024, 1536), af=(1536, 1544), az=(1544, 2056),
             bq=(2056, 2568), bk=(2568, 3080), bv=(3080, 3592), ba=(3592, 3596), bb=(3596, 3600),
             bz=(3600, 4112), mq=(4112, 4624), mz=(4624, 5136), gates=(5136, 8208))
_ORDER = ("aq", "ak", "av", "mq", "bq", "bk", "bv", "az", "bz", "mz", "gates", "af", "ba", "bb")
N_IN = 8208
NB, NF, NS = 2048, 6144, 128
N_ALL = NB + NF + NS

_SHARDED = ("w_in", "w_mem_kv", "w_branch", "w_out", "conv_w")


def _cp(sem=None, vmem=None):
    kw = {}
    if sem is not None:
        kw["dimension_semantics"] = sem
    if vmem is not None:
        kw["vmem_limit_bytes"] = vmem
    return pltpu.CompilerParams(**kw)


def _dot(a, b):
    return jnp.dot(a, b, preferred_element_type=f32)


def _dot_nt(a, b):
    return lax.dot_general(a, b, (((1,), (1,)), ((), ())), preferred_element_type=f32)


def _dot_tn(a, b):
    return lax.dot_general(a, b, (((0,), (0,)), ((), ())), preferred_element_type=f32)


def _split2(x):
    hi = x.astype(bf16)
    return hi, (x - hi.astype(f32)).astype(bf16)


def _mm3(a, b, dims):
    ah, al = _split2(a)
    bh, bl = _split2(b)
    dg = functools.partial(lax.dot_general, dimension_numbers=dims, preferred_element_type=f32)
    return dg(ah, bh) + (dg(ah, bl) + dg(al, bh))


def _hi(a, b):
    return _mm3(a, b, (((1,), (0,)), ((), ())))


def _hi_nt(a, b):
    return _mm3(a, b, (((1,), (1,)), ((), ())))


def _hi_tn(a, b):
    return _mm3(a, b, (((0,), (0,)), ((), ())))


def _hi_b(a, b):
    return _mm3(a, b, (((2,), (1,)), ((0,), (0,))))


def _b(x):
    return x.astype(bf16)


def _sig(x):
    return jax.nn.sigmoid(x)


def _silu(x):
    return x * _sig(x)


def _dsilu(x):
    s = _sig(x)
    return s * (1.0 + x * (1.0 - s))


def _softplus(x):
    return jnp.maximum(x, 0.0) + jnp.log1p(jnp.exp(-jnp.abs(x)))


def _rowsum(x):
    return jnp.sum(x, axis=1, keepdims=True)


def _colsum(x):
    return jnp.sum(x, axis=0, keepdims=True)


def _norm_fwd(x, g, name):
    M = x.shape[0]
    ts = min(M, 512)

    def body(x_ref, g_ref, h_ref):
        xv = x_ref[...]
        r = lax.rsqrt(jnp.mean(xv * xv, axis=-1, keepdims=True) + EPS)
        h_ref[...] = _b(xv * r * g_ref[...])

    return pl.pallas_call(
        body, grid=(M // ts,), out_shape=jax.ShapeDtypeStruct((M, D), bf16),
        in_specs=[pl.BlockSpec((ts, D), lambda i: (i, 0)), pl.BlockSpec((1, D), lambda i: (0, 0))],
        out_specs=pl.BlockSpec((ts, D), lambda i: (i, 0)), compiler_params=_cp(("parallel",)), name=name)(x, g)


def _norm_bwd(x, g, dh, dres, name):
    M = x.shape[0]
    ts = min(M, 512)
    with_dx = dres is not None

    def body(*refs):
        if with_dx:
            x_ref, g_ref, dh_ref, dres_ref, dx_ref, dg_ref = refs
        else:
            x_ref, g_ref, dh_ref, dg_ref = refs
        i = pl.program_id(0)
        xv = x_ref[...]
        r = lax.rsqrt(jnp.mean(xv * xv, axis=-1, keepdims=True) + EPS)
        xh = xv * r
        dh = dh_ref[...].astype(f32)
        part = jnp.broadcast_to(_colsum(dh * xh), (8, D))

        @pl.when(i == 0)
        def _():
            dg_ref[...] = part

        @pl.when(i > 0)
        def _():
            dg_ref[...] += part

        if with_dx:
            dxh = dh * g_ref[...]
            dx_ref[...] = dres_ref[...] + r * (dxh - xh * jnp.mean(dxh * xh, axis=-1, keepdims=True))

    tile = pl.BlockSpec((ts, D), lambda i: (i, 0))
    gspec = pl.BlockSpec((1, D), lambda i: (0, 0))
    acc = pl.BlockSpec((8, D), lambda i: (0, 0))
    if with_dx:
        return pl.pallas_call(
            body, grid=(M // ts,), out_shape=(jax.ShapeDtypeStruct((M, D), f32), jax.ShapeDtypeStruct((8, D), f32)),
            in_specs=[tile, gspec, tile, tile], out_specs=(tile, acc), compiler_params=_cp(("arbitrary",)), name=name)(x, g, dh, dres)
    return pl.pallas_call(
        body, grid=(M // ts,), out_shape=jax.ShapeDtypeStruct((8, D), f32),
        in_specs=[tile, gspec, tile], out_specs=acc, compiler_params=_cp(("arbitrary",)), name=name)(x, g, dh)


def _loss_head(x, g, tgt, name):
    M = x.shape[0]
    ts = min(M, 512)

    def body(x_ref, g_ref, t_ref, dx_ref, dg_ref, ls_ref):
        i = pl.program_id(0)
        xv = x_ref[...]
        gv = g_ref[...]
        r = lax.rsqrt(jnp.mean(xv * xv, axis=-1, keepdims=True) + EPS)
        xh = xv * r
        e = xh * gv - t_ref[...]
        lpart = 0.5 * jnp.sum(jnp.mean(e * e, axis=-1, keepdims=True), axis=0, keepdims=True)
        dy = e * (1.0 / D)
        dgp = jnp.broadcast_to(_colsum(dy * xh), (8, D))
        lp = jnp.broadcast_to(lpart, (8, 128))

        @pl.when(i == 0)
        def _():
            dg_ref[...] = dgp
            ls_ref[...] = lp

        @pl.when(i > 0)
        def _():
            dg_ref[...] += dgp
            ls_ref[...] += lp

        dxh = dy * gv
        dx_ref[...] = r * (dxh - xh * jnp.mean(dxh * xh, axis=-1, keepdims=True))

    tile = pl.BlockSpec((ts, D), lambda i: (i, 0))
    return pl.pallas_call(
        body, grid=(M // ts,),
        out_shape=(jax.ShapeDtypeStruct((M, D), f32), jax.ShapeDtypeStruct((8, D), f32), jax.ShapeDtypeStruct((8, 128), f32)),
        in_specs=[tile, pl.BlockSpec((1, D), lambda i: (0, 0)), tile],
        out_specs=(tile, pl.BlockSpec((8, D), lambda i: (0, 0)), pl.BlockSpec((8, 128), lambda i: (0, 0))),
        compiler_params=_cp(("arbitrary",)), name=name)(x, g, tgt)


def _mm(a, b, out_dtype, tm, tn, tk, name, trans_a=False, a_cols=None, b_cols=None):
    if trans_a:
        K, M = a.shape
    else:
        M, K = a.shape
    N = b.shape[1]
    a0, b0 = 0, 0
    if a_cols is not None:
        a0, M = a_cols
    if b_cols is not None:
        b0, N = b_cols
    tm, tn, tk = min(tm, M), min(tn, N), min(tk, K)
    nk = K // tk
    a0, b0 = a0 // tm, b0 // tn

    def body(a_ref, b_ref, o_ref, acc_ref):
        k = pl.program_id(2)
        av, bv = _b(a_ref[...]), _b(b_ref[...])
        part = _dot_tn(av, bv) if trans_a else _dot(av, bv)
        if nk == 1:
            o_ref[...] = part.astype(out_dtype)
        else:
            @pl.when(k == 0)
            def _():
                acc_ref[...] = part

            @pl.when(k > 0)
            def _():
                acc_ref[...] += part

            @pl.when(k == nk - 1)
            def _():
                o_ref[...] = acc_ref[...].astype(out_dtype)

    a_spec = pl.BlockSpec((tk, tm), lambda i, j, k: (k, i + a0)) if trans_a else pl.BlockSpec((tm, tk), lambda i, j, k: (i, k))
    return pl.pallas_call(
        body, grid=(M // tm, N // tn, nk), out_shape=jax.ShapeDtypeStruct((M, N), out_dtype),
        in_specs=[a_spec, pl.BlockSpec((tk, tn), lambda i, j, k: (k, j + b0))],
        out_specs=pl.BlockSpec((tm, tn), lambda i, j, k: (i, j)),
        scratch_shapes=[pltpu.VMEM((tm, tn), f32)],
        compiler_params=_cp(("parallel", "parallel", "arbitrary"), VMEM_LIMIT), name=name)(a, b)


def _small_pars(b_fg, a_log, dt_bias):
    par = jnp.zeros((8, 128), f32)
    par = par.at[0, 0:8].set(b_fg).at[1, 8:12].set(a_log).at[2, 8:12].set(dt_bias)
    return par


def _small_prep(zs, par, name):
    S = zs.shape[0]
    ts = min(S, 512)

    def body(z_ref, par_ref, o_ref, carry_ref):
        i = pl.program_id(0)

        @pl.when(i == 0)
        def _():
            carry_ref[...] = jnp.zeros_like(carry_ref)

        z = z_ref[...]
        lane = lax.broadcasted_iota(jnp.int32, (ts, 128), 1)
        row = lax.broadcasted_iota(jnp.int32, (ts, 128), 0)
        za = z + par_ref[0:1, :]
        logf = jnp.minimum(za, 0.0) - jnp.log1p(jnp.exp(-jnp.abs(za)))
        glog = -jnp.exp(par_ref[1:2, :]) * _softplus(z + par_ref[2:3, :])
        x = jnp.where(lane < 8, logf, jnp.where(lane < 12, glog, 0.0))
        pos = jnp.where(lane < 8, row, row & (CH - 1))
        s = 1
        while s < ts:
            x = x + jnp.where(pos >= s, pltpu.roll(x, s, 0), 0.0)
            s *= 2
        tot = x + carry_ref[0:1, :]
        carry_ref[...] = jnp.broadcast_to(jnp.where(lane[0:1] < 8, tot[ts - 1:ts, :], 0.0), (8, 128))
        o_ref[...] = jnp.where(lane < 8, tot, jnp.where(lane < 12, x, jnp.where(lane < 16, _sig(z), 0.0)))

    return pl.pallas_call(
        body, grid=(S // ts,), out_shape=jax.ShapeDtypeStruct((S, 128), f32),
        in_specs=[pl.BlockSpec((ts, 128), lambda i: (i, 0)), pl.BlockSpec((8, 128), lambda i: (0, 0))],
        out_specs=pl.BlockSpec((ts, 128), lambda i: (i, 0)), scratch_shapes=[pltpu.VMEM((8, 128), f32)],
        compiler_params=_cp(("arbitrary",)), name=name)(zs, par)


def _small_bwd(zs, par, dfr, dfc, dsm, name):
    S = zs.shape[0]
    ts = min(S, 512)
    nt = S // ts

    def body(z_ref, par_ref, dfr_ref, dfc_ref, dsm_ref, dz_ref, acc_ref, carry_ref):
        i = pl.program_id(0)

        @pl.when(i == 0)
        def _():
            carry_ref[...] = jnp.zeros_like(carry_ref)

        z = z_ref[...]
        dsm_v = dsm_ref[...]
        lane = lax.broadcasted_iota(jnp.int32, (ts, 128), 1)
        row = lax.broadcasted_iota(jnp.int32, (ts, 128), 0)
        df = jnp.transpose(jnp.concatenate([dfr_ref[...], jnp.zeros((120, ts), f32)], axis=0))
        for p in range(4):
            dpair = dfc_ref[p]
            df = df - jnp.where(lane == 2 * p, dpair[:, 0:1], jnp.where(lane == 2 * p + 1, dpair[:, 64:65], 0.0))
        x = jnp.where(lane < 8, df, jnp.where(lane < 12, dsm_v, 0.0))
        pos = jnp.where(lane < 8, row, row & (CH - 1))
        seg = jnp.where(lane < 8, ts, CH)
        s = 1
        while s < ts:
            x = x + jnp.where(pos + s < seg, pltpu.roll(x, ts - s, 0), 0.0)
            s *= 2
        tot = x + carry_ref[0:1, :]
        carry_ref[...] = jnp.broadcast_to(jnp.where(lane[0:1] < 8, tot[0:1, :], 0.0), (8, 128))
        za = z + par_ref[0:1, :]
        daf = tot * _sig(-za)
        zb = z + par_ref[2:3, :]
        nea = -jnp.exp(par_ref[1:2, :])
        glog = nea * _softplus(zb)
        dba = x * nea * _sig(zb)
        beta = _sig(z)
        dbb = dsm_v * beta * (1.0 - beta)
        dz_ref[...] = _b(jnp.where(lane < 8, daf, jnp.where(lane < 12, dba, jnp.where(lane < 16, dbb, 0.0))))
        r0 = _colsum(jnp.where(lane < 8, daf, 0.0))
        r1 = _colsum(jnp.where((lane >= 8) & (lane < 12), x * glog, 0.0))
        r2 = _colsum(jnp.where((lane >= 8) & (lane < 12), dba, 0.0))
        r8 = lax.broadcasted_iota(jnp.int32, (8, 128), 0)
        part = jnp.where(r8 == 0, r0, jnp.where(r8 == 1, r1, jnp.where(r8 == 2, r2, 0.0)))

        @pl.when(i == 0)
        def _():
            acc_ref[...] = part

        @pl.when(i > 0)
        def _():
            acc_ref[...] += part

    rev = pl.BlockSpec((ts, 128), lambda i: (nt - 1 - i, 0))
    rev4 = pl.BlockSpec((4, ts, 128), lambda i: (0, nt - 1 - i, 0))
    c8 = pl.BlockSpec((8, 128), lambda i: (0, 0))
    return pl.pallas_call(
        body, grid=(nt,), out_shape=(jax.ShapeDtypeStruct((S, 128), bf16), jax.ShapeDtypeStruct((8, 128), f32)),
        in_specs=[rev, c8, pl.BlockSpec((8, ts), lambda i: (0, nt - 1 - i)), rev4, rev], out_specs=(rev, c8),
        scratch_shapes=[pltpu.VMEM((8, 128), f32)],
        compiler_params=_cp(("arbitrary",)), name=name)(zs, par, dfr, dfc, dsm)


def _split3(x):
    hi = _b(x).astype(f32)
    r = x - hi
    mid = _b(r).astype(f32)
    return hi, mid, _b(r - mid).astype(f32)


FOX_PREP_ROWS = 512
FOX_TILE = 1024
FOX_SKIP_LOG = -50.0


def _fox_prep(zb, sm, name):
    S = zb.shape[0]
    ts = min(S, FOX_PREP_ROWS)

    def body(q_ref, k_ref, f_ref, qa_ref, ka_ref, st_ref):
        lane = lax.broadcasted_iota(jnp.int32, (ts, 128), 1)
        lane8 = lax.broadcasted_iota(jnp.int32, (8, 128), 1)
        f = f_ref[...]
        st = jnp.zeros((8, 128), f32)
        for p in range(4):
            q = q_ref[:, p * 128:(p + 1) * 128].astype(f32) * FOX_SCALE
            k = k_ref[:, p * 128:(p + 1) * 128].astype(f32)
            for h in (0, 1):
                fcol = f[:, 2 * p + h:2 * p + h + 1]
                hi, mid, lo = _split3(fcol)
                own = (lane < 64) if h == 0 else (lane >= 64)
                nq = jnp.sqrt(_rowsum(jnp.where(own, q * q, 0.0)))
                nk = jnp.sqrt(_rowsum(jnp.where(own, k * k, 0.0)))
                stats = (jnp.max(nq, axis=0, keepdims=True), jnp.max(nk, axis=0, keepdims=True),
                         jnp.max(fcol, axis=0, keepdims=True), jnp.min(fcol, axis=0, keepdims=True),
                         jnp.min(-nq * nk, axis=0, keepdims=True))
                for si, val in enumerate(stats):
                    st = jnp.where(lane8 == 8 * si + 2 * p + h, val, st)
                o = 64 if h == 0 else 0
                ones_lo = (lane >= o) & (lane < o + 3)
                ones_hi = (lane >= o + 3) & (lane < o + 6)
                qaug = jnp.where(lane == o, hi, jnp.where(lane == o + 1, mid, jnp.where(lane == o + 2, lo, jnp.where(ones_hi, 1.0, 0.0))))
                kaug = jnp.where(lane == o + 3, -hi, jnp.where(lane == o + 4, -mid, jnp.where(lane == o + 5, -lo, jnp.where(ones_lo, 1.0, 0.0))))
                qa_ref[2 * p + h] = _b(jnp.where(own, q, qaug))
                ka_ref[2 * p + h] = _b(jnp.where(own, k, kaug))
        st_ref[...] = st

    out = jax.ShapeDtypeStruct((8, S, 128), bf16)
    return pl.pallas_call(
        body, grid=(S // ts,), out_shape=(out, out, jax.ShapeDtypeStruct((S // ts * 8, 128), f32)),
        in_specs=[pl.BlockSpec((ts, 512), lambda i: (i, 0)), pl.BlockSpec((ts, 512), lambda i: (i, 1)), pl.BlockSpec((ts, 128), lambda i: (i, 0))],
        out_specs=(pl.BlockSpec((8, ts, 128), lambda i: (0, i, 0)), pl.BlockSpec((8, ts, 128), lambda i: (0, i, 0)),
                   pl.BlockSpec((8, 128), lambda i: (i, 0))),
        compiler_params=_cp(("parallel",)), name=name)(zb, zb, sm)


def _fox_bound_table(st, S, T):
    ts = min(S, FOX_PREP_ROWS)
    g = T // ts
    nt = S // T
    s5 = st.reshape(S // ts, 8, 128)[:, 0, 0:40].reshape(nt, g, 5, 8)
    qn, kn, fmax = s5[:, :, 0].max(axis=1), s5[:, :, 1].max(axis=1), s5[:, :, 2].max(axis=1)
    fmin, lmin = s5[:, :, 3].min(axis=1), s5[:, :, 4].min(axis=1)
    e = qn[:, None] * kn[None, :] + fmax[:, None] - fmin[None, :] - lmin[:, None] + 1.0
    return jnp.transpose(e, (2, 0, 1)).reshape(8, nt * nt)


def _pair_rows(a, T):
    at = jnp.transpose(a)
    r8 = lax.broadcasted_iota(jnp.int32, (8, T), 0)
    return jnp.where(r8 == 0, at[0:1, :], at[64:65, :])


def _fox_fwd(qa, ka, zb, tab, name):
    S = zb.shape[0]
    T = min(S, FOX_TILE)
    nt = S // T

    def body(tab_ref, qa_ref, ka_ref, v_ref, o_ref, lset_ref, m_ref, l_ref, acc_ref):
        p, i = pl.program_id(0), pl.program_id(1)
        m_ref[...] = jnp.full_like(m_ref, NEG)
        l_ref[...] = jnp.zeros_like(l_ref)
        acc_ref[...] = jnp.zeros_like(acc_ref)
        row = lax.broadcasted_iota(jnp.int32, (T, T), 0)
        col = lax.broadcasted_iota(jnp.int32, (T, T), 1)

        def head_tile(h, j, masked):
            off = pl.multiple_of(j * T, T)
            s = _dot_nt(qa_ref[h], ka_ref[h, pl.ds(off, T), :])
            if masked:
                s = jnp.where(row >= col, s, NEG)
            m_old = m_ref[h]
            m_new = jnp.maximum(m_old, jnp.max(s, axis=1, keepdims=True))
            alpha = jnp.exp(m_old - m_new)
            pr = jnp.exp(s - jnp.tile(m_new, (1, T // 128)))
            l_ref[h] = alpha * l_ref[h] + _rowsum(pr)
            acc_ref[h] = alpha * acc_ref[h] + _dot(_b(pr), v_ref[pl.ds(off, T), :])
            m_ref[h] = m_new

        def step(j, c):
            for h in (0, 1):
                @pl.when(tab_ref[2 * p + h, i * nt + j] > FOX_SKIP_LOG)
                def _():
                    head_tile(h, j, False)
            return c

        lax.fori_loop(0, i, step, 0)
        for h in (0, 1):
            head_tile(h, i, True)
        lane2 = lax.broadcasted_iota(jnp.int32, (T, 128), 1)
        o_ref[...] = jnp.where(lane2 < 64, acc_ref[0] / l_ref[0], acc_ref[1] / l_ref[1])
        lse = jnp.where(lane2 < 64, m_ref[0] + jnp.log(l_ref[0]), m_ref[1] + jnp.log(l_ref[1]))
        lset_ref[0] = _pair_rows(lse, T)

    return pl.pallas_call(
        body, grid=(4, S // T),
        out_shape=(jax.ShapeDtypeStruct((S, 512), f32), jax.ShapeDtypeStruct((4, 8, S), f32)),
        in_specs=[pl.BlockSpec(memory_space=pltpu.SMEM), pl.BlockSpec((2, T, 128), lambda p, i: (p, i, 0)),
                  pl.BlockSpec((2, S, 128), lambda p, i: (p, 0, 0), pipeline_mode=pl.Buffered(1)),
                  pl.BlockSpec((S, 128), lambda p, i: (0, 8 + p), pipeline_mode=pl.Buffered(1))],
        out_specs=(pl.BlockSpec((T, 128), lambda p, i: (i, p)), pl.BlockSpec((1, 8, T), lambda p, i: (p, 0, i))),
        scratch_shapes=[pltpu.VMEM((2, T, 128), f32), pltpu.VMEM((2, T, 128), f32), pltpu.VMEM((2, T, 128), f32)],
        compiler_params=_cp(("arbitrary", "arbitrary"), VMEM_LIMIT), name=name)(tab, qa, ka, zb)


def _fox_bwd(qa, ka, zb, dob, lse_t, dl_t, tab, name):
    S = zb.shape[0]
    T = min(S, FOX_TILE)
    nq = S // T

    def body(tab_ref, ka_ref, v_ref, qa_ref, do_ref, lt_ref, dt_ref, dq_ref, dk_ref, dv_ref, dfc_ref, dfr_ref, dqa_ref, dka_ref, dva_ref, fs_ref):
        p, j = pl.program_id(0), pl.program_id(1)
        lane1 = lax.broadcasted_iota(jnp.int32, (1, 128), 1)
        lane2 = lax.broadcasted_iota(jnp.int32, (T, 128), 1)
        hm = (lane1 < 64, lane1 >= 64)
        v = v_ref[...]
        vsm = [jnp.where(hm[h], v, jnp.zeros_like(v)) for h in (0, 1)]
        ksm = [jnp.where(hm[h], ka_ref[h], jnp.zeros_like(v)) for h in (0, 1)]

        @pl.when(j == 0)
        def _():
            dqa_ref[...] = jnp.zeros_like(dqa_ref)
            dfr_ref[...] = jnp.zeros_like(dfr_ref)

        dka_ref[...] = jnp.zeros_like(dka_ref)
        dva_ref[...] = jnp.zeros_like(dva_ref)
        fs_ref[...] = jnp.zeros_like(fs_ref)
        row = lax.broadcasted_iota(jnp.int32, (T, T), 0)
        col = lax.broadcasted_iota(jnp.int32, (T, T), 1)

        def head_tile(h, i, masked):
            off = pl.multiple_of(i * T, T)
            dot_ = do_ref[pl.ds(off, T), :]
            hr = pl.ds(2 * p + h, 1)
            qt = qa_ref[h, pl.ds(off, T), :]
            s_t = _dot_nt(ka_ref[h], qt)
            if masked:
                s_t = jnp.where(col >= row, s_t, NEG)
            p_t = jnp.exp(s_t - lt_ref[hr, pl.ds(off, T)])
            dva_ref[h] += _dot(_b(p_t), dot_)
            dp_t = _dot_nt(vsm[h], dot_)
            ds_t = p_t * (dp_t - dt_ref[hr, pl.ds(off, T)])
            dsb = _b(ds_t)
            dka_ref[h] += _dot(dsb, qt)
            fs_ref[h] += _rowsum(ds_t)
            dfr_ref[0, pl.ds(h, 1), pl.ds(off, T)] += _colsum(ds_t)
            dqa_ref[pl.ds(off, T), :] += _dot_tn(dsb, ksm[h])

        def step(i, c):
            for h in (0, 1):
                @pl.when(tab_ref[2 * p + h, i * nq + j] > FOX_SKIP_LOG)
                def _():
                    head_tile(h, i, False)
            return c

        for h in (0, 1):
            head_tile(h, j, True)
        lax.fori_loop(j + 1, nq, step, 0)
        dk_ref[...] = _b(jnp.where(lane2 < 64, dka_ref[0], dka_ref[1]))
        dv_ref[...] = _b(jnp.where(lane2 < 64, dva_ref[0], dva_ref[1]))
        dfc_ref[0] = jnp.where(lane2 < 64, fs_ref[0], fs_ref[1])
        dq_ref[...] = _b(dqa_ref[pl.ds(pl.multiple_of(j * T, T), T), :] * FOX_SCALE)

    one = pl.Buffered(1)
    res = pl.BlockSpec((8, S), lambda p, j: (0, 0), pipeline_mode=one)
    tk = pl.BlockSpec((T, 128), lambda p, j: (j, p))
    return pl.pallas_call(
        body, grid=(4, nq),
        out_shape=(jax.ShapeDtypeStruct((S, 512), bf16), jax.ShapeDtypeStruct((S, 512), bf16), jax.ShapeDtypeStruct((S, 512), bf16),
                   jax.ShapeDtypeStruct((4, S, 128), f32), jax.ShapeDtypeStruct((4, 8, S), f32)),
        in_specs=[pl.BlockSpec(memory_space=pltpu.SMEM),
                  pl.BlockSpec((2, T, 128), lambda p, j: (p, j, 0)), pl.BlockSpec((T, 128), lambda p, j: (j, 8 + p)),
                  pl.BlockSpec((2, S, 128), lambda p, j: (p, 0, 0), pipeline_mode=one),
                  pl.BlockSpec((S, 128), lambda p, j: (0, p), pipeline_mode=one), res, res],
        out_specs=(tk, tk, tk, pl.BlockSpec((1, T, 128), lambda p, j: (p, j, 0)),
                   pl.BlockSpec((1, 8, S), lambda p, j: (p, 0, 0))),
        scratch_shapes=[pltpu.VMEM((S, 128), f32), pltpu.VMEM((2, T, 128), f32), pltpu.VMEM((2, T, 128), f32), pltpu.VMEM((2, T, 1), f32)],
        compiler_params=_cp(("arbitrary", "arbitrary"), VMEM_LIMIT), name=name)(tab, ka, zb, qa, dob, lse_t, dl_t)


def _head_rows(a):
    return a[:, 0:2, :].reshape(8, a.shape[2])


def _conv_taps(ext, x, w_ref, ts):
    y = x * w_ref[3:4, :]
    shifted = []
    for k in (1, 2, 3):
        xs = pltpu.roll(ext, k, 0)[8:]
        shifted.append(xs)
        y = y + xs * w_ref[3 - k:4 - k, :]
    return y, shifted


def _gdn_prep(zf, cw, name):
    S = zf.shape[0]
    ts = min(S, 512)

    def body(x_ref, w_ref, o_ref, tail_ref):
        i = pl.program_id(0)

        @pl.when(i == 0)
        def _():
            tail_ref[...] = jnp.zeros_like(tail_ref)

        x = x_ref[...]
        ext = jnp.concatenate([tail_ref[...], x], axis=0)
        y, _ = _conv_taps(ext, x, w_ref, ts)
        tail_ref[...] = x[ts - 8:, :]
        a = _silu(y)
        for hb in range(12):
            blk = a[:, hb * 128:(hb + 1) * 128]
            if hb < 8:
                blk = blk * lax.rsqrt(_rowsum(blk * blk) + EPS)
            if hb < 4:
                blk = blk * GDN_SCALE
            o_ref[:, hb * 128:(hb + 1) * 128] = blk

    return pl.pallas_call(
        body, grid=(S // ts,), out_shape=jax.ShapeDtypeStruct((S, 1536), f32),
        in_specs=[pl.BlockSpec((ts, 1536), lambda i: (i, 0)), pl.BlockSpec((8, 1536), lambda i: (0, 0))],
        out_specs=pl.BlockSpec((ts, 1536), lambda i: (i, 0)), scratch_shapes=[pltpu.VMEM((8, 1536), f32)],
        compiler_params=_cp(("arbitrary",), VMEM_LIMIT), name=name)(zf, cw)


def _gdn_prep_bwd(zf, cw, dg, name):
    S = zf.shape[0]
    ts = min(S, 512)
    nt = S // ts

    def body(x_ref, xp_ref, w_ref, dg_ref, dx_ref, dw_ref, head_ref):
        i = pl.program_id(0)

        @pl.when(i == 0)
        def _():
            head_ref[...] = jnp.zeros_like(head_ref)

        x = x_ref[...]
        prev = jnp.where(i == nt - 1, 0.0, xp_ref[...])
        ext = jnp.concatenate([prev, x], axis=0)
        y, shifted = _conv_taps(ext, x, w_ref, ts)
        a = _silu(y)
        das = []
        for hb in range(12):
            blk = a[:, hb * 128:(hb + 1) * 128]
            d = dg_ref[:, hb * 128:(hb + 1) * 128]
            if hb < 4:
                d = d * GDN_SCALE
            if hb < 8:
                r = lax.rsqrt(_rowsum(blk * blk) + EPS)
                n = blk * r
                d = r * (d - n * _rowsum(d * n))
            das.append(d)
        dy = jnp.concatenate(das, axis=1) * _dsilu(y)
        extd = jnp.concatenate([dy, head_ref[...]], axis=0)
        dx = dy * w_ref[3:4, :]
        for k in (1, 2, 3):
            dx = dx + pltpu.roll(extd, ts + 8 - k, 0)[:ts] * w_ref[3 - k:4 - k, :]
        head_ref[...] = dy[0:8, :]
        dx_ref[...] = _b(dx)
        r8 = lax.broadcasted_iota(jnp.int32, (8, 1536), 0)
        part = jnp.where(r8 == 3, _colsum(dy * x), 0.0)
        for k in (1, 2, 3):
            part = jnp.where(r8 == 3 - k, _colsum(dy * shifted[k - 1]), part)

        @pl.when(i == 0)
        def _():
            dw_ref[...] = part

        @pl.when(i > 0)
        def _():
            dw_ref[...] += part

    rev = pl.BlockSpec((ts, 1536), lambda i: (nt - 1 - i, 0))
    prev8 = pl.BlockSpec((8, 1536), lambda i: (jnp.maximum((nt - 1 - i) * (ts // 8) - 1, 0), 0))
    w8 = pl.BlockSpec((8, 1536), lambda i: (0, 0))
    return pl.pallas_call(
        body, grid=(nt,), out_shape=(jax.ShapeDtypeStruct((S, 1536), bf16), jax.ShapeDtypeStruct((8, 1536), f32)),
        in_specs=[rev, prev8, w8, rev], out_specs=(rev, w8), scratch_shapes=[pltpu.VMEM((8, 1536), f32)],
        compiler_params=_cp(("arbitrary",), VMEM_LIMIT), name=name)(zf, zf, cw, dg)


def _tri_inv(a, row, col):
    same = (row >> 4) == (col >> 4)
    dm = jnp.where(same, a, 0.0)
    lo = a - dm
    eye = jnp.where(row == col, 1.0, 0.0)
    d2 = _hi_b(dm, dm)
    d4 = _hi_b(d2, d2)
    d8 = _hi_b(d4, d4)
    x0 = _hi_b(_hi_b(eye - dm, eye + d2), _hi_b(eye + d4, eye + d8))
    n = _hi_b(x0, lo)
    n2 = _hi_b(n, n)
    return _hi_b(_hi_b(eye - n, eye + n2), x0)


def _bd(a, b):
    return lax.dot_general(a, b, (((2,), (1,)), ((0,), (0,))), preferred_element_type=f32)


def _bd_nt(a, b):
    return lax.dot_general(a, b, (((2,), (2,)), ((0,), (0,))), preferred_element_type=f32)


def _bd_tn(a, b):
    return lax.dot_general(a, b, (((1,), (1,)), ((0,), (0,))), preferred_element_type=f32)


def _hi_b_nt(a, b):
    return _mm3(a, b, (((2,), (2,)), ((0,), (0,))))


def _hi_b_tn(a, b):
    return _mm3(a, b, (((1,), (1,)), ((0,), (0,))))


def _gdn_local(x_ref, sm_ref, gt_ref, row, col, cps=1):
    idx = [(c, h) for c in range(cps) for h in range(4)]

    def rows(c):
        return slice(c * CH, (c + 1) * CH)

    q = jnp.stack([x_ref[rows(c), h * 128:(h + 1) * 128] for c, h in idx])
    k = jnp.stack([x_ref[rows(c), 512 + h * 128:512 + (h + 1) * 128] for c, h in idx])
    v = jnp.stack([x_ref[rows(c), 1024 + h * 128:1024 + (h + 1) * 128] for c, h in idx])
    gc = jnp.stack([sm_ref[rows(c), 8 + h:9 + h] for c, h in idx])
    beta = jnp.stack([sm_ref[rows(c), 12 + h:13 + h] for c, h in idx])
    gr = jnp.stack([gt_ref[h, c] for c, h in idx])
    eg = jnp.exp(gc)
    gl = gc[:, CH - 1:CH, :]
    dec = jnp.exp(gl - gc)
    gm = gc - gr
    gam_i = jnp.exp(jnp.where(row >= col, gm, -jnp.inf))
    gam_s = jnp.where(row > col, gam_i, 0.0)
    kb = k * beta
    return dict(q=q, k=k, v=v, beta=beta, eg=eg, egl=jnp.exp(gl), dec=dec, gam_i=gam_i, gam_s=gam_s,
                kb=kb, vb=v * beta, kbg=kb * eg, qdec=q * eg, kdec=k * dec,
                a=_bd_nt(_b(kb), _b(k)) * gam_s, aqk=_bd_nt(_b(q), _b(k)) * gam_i)


GDN_FWD_CHUNKS = 4


def _gdn_fwd(gqkv, sm, gt4, name, cps=GDN_FWD_CHUNKS):
    S = gqkv.shape[0]
    N = S // CH
    cps = min(cps, N)
    R = cps * CH

    def body(x_ref, sm_ref, gt_ref, o_ref, t_ref, st_ref, s_ref):
        n = pl.program_id(0)

        @pl.when(n == 0)
        def _():
            s_ref[...] = jnp.zeros_like(s_ref)

        row = lax.broadcasted_iota(jnp.int32, (CH, CH), 0)
        col = lax.broadcasted_iota(jnp.int32, (CH, CH), 1)
        c = _gdn_local(x_ref, sm_ref, gt_ref, row, col, cps)
        t = _tri_inv(c["a"], row, col)
        uw = _hi_b(t, jnp.concatenate([c["vb"], c["kbg"]], axis=2))
        u, w = uw[:, :, :128], uw[:, :, 128:]
        for ci in range(cps):
            sl = slice(4 * ci, 4 * ci + 4)
            rs = slice(ci * CH, (ci + 1) * CH)
            st = s_ref[...]
            st_ref[ci] = st
            sb = _b(st)
            vnew = u[sl] - _bd(_b(w[sl]), sb)
            o = _bd(_b(c["qdec"][sl]), sb) + _bd(_b(c["aqk"][sl]), _b(vnew))
            for h in range(4):
                o_ref[rs, h * 128:(h + 1) * 128] = o[h]
                t_ref[h, rs, :] = t[4 * ci + h]
            s_ref[...] = st * c["egl"][sl] + _bd_tn(_b(c["kdec"][sl]), _b(vnew))

    return pl.pallas_call(
        body, grid=(N // cps,),
        out_shape=(jax.ShapeDtypeStruct((S, 512), f32), jax.ShapeDtypeStruct((4, S, CH), f32), jax.ShapeDtypeStruct((N, 4, 128, 128), f32)),
        in_specs=[pl.BlockSpec((R, 1536), lambda n: (n, 0)), pl.BlockSpec((R, 128), lambda n: (n, 0)),
                  pl.BlockSpec((4, cps, 1, CH), lambda n: (0, n, 0, 0))],
        out_specs=(pl.BlockSpec((R, 512), lambda n: (n, 0)), pl.BlockSpec((4, R, CH), lambda n: (0, n, 0)),
                   pl.BlockSpec((cps, 4, 128, 128), lambda n: (n, 0, 0, 0))),
        scratch_shapes=[pltpu.VMEM((4, 128, 128), f32)], compiler_params=_cp(("arbitrary",)), name=name)(gqkv, sm, gt4)


GDN_BWD_CHUNKS = 4


def _gdn_bwd(gqkv, sm, gt4, tinv, states, do, name, cps=GDN_BWD_CHUNKS):
    S = gqkv.shape[0]
    N = S // CH
    cps = min(cps, N)
    R = cps * CH

    def body(x_ref, sm_ref, gt_ref, t_ref, st_ref, do_ref, dx_ref, dsm_ref, ds_ref):
        n = pl.program_id(0)

        @pl.when(n == 0)
        def _():
            ds_ref[...] = jnp.zeros_like(ds_ref)

        row = lax.broadcasted_iota(jnp.int32, (CH, CH), 0)
        col = lax.broadcasted_iota(jnp.int32, (CH, CH), 1)
        row1 = lax.broadcasted_iota(jnp.int32, (CH, 1), 0)
        lane = lax.broadcasted_iota(jnp.int32, (CH, 128), 1)
        ones = jnp.ones((4 * cps, CH, 128), f32)
        idx = [(ci, h) for ci in range(cps) for h in range(4)]
        c = _gdn_local(x_ref, sm_ref, gt_ref, row, col, cps)
        q, k, v, beta, eg = c["q"], c["k"], c["v"], c["beta"], c["eg"]
        t = jnp.stack([t_ref[h, ci * CH:(ci + 1) * CH, :] for ci, h in idx])
        uw = _hi_b(t, jnp.concatenate([c["vb"], c["kbg"]], axis=2))
        u, w = uw[:, :, :128], uw[:, :, 128:]
        st = st_ref[...].reshape(4 * cps, 128, 128)
        sb = _b(st)
        vnew = u - _bd(_b(w), sb)
        dob = _b(jnp.stack([do_ref[ci * CH:(ci + 1) * CH, h * 128:(h + 1) * 128] for ci, h in idx]))
        vnb = _b(vnew)
        dqdec = _bd_nt(dob, sb)
        daqk = jnp.where(row >= col, _bd_nt(dob, vnb), 0.0)
        qd_do = _bd_tn(_b(c["qdec"]), dob)
        aqk_do = _bd_tn(_b(c["aqk"]), dob)
        kdecb, wb = _b(c["kdec"]), _b(w)
        dvnew_l, dkdec_l, dgl_l = [None] * cps, [None] * cps, [None] * cps
        for ci in reversed(range(cps)):
            sl = slice(4 * ci, 4 * ci + 4)
            dsp = ds_ref[...]
            dspb = _b(dsp)
            dvn = _bd(kdecb[sl], dspb) + aqk_do[sl]
            dvnew_l[ci] = dvn
            dkdec_l[ci] = _bd_nt(vnb[sl], dspb)
            dgl_l[ci] = c["egl"][sl] * jnp.sum(dsp * st[sl], axis=(1, 2), keepdims=True)
            ds_ref[...] = dsp * c["egl"][sl] + qd_do[sl] - _bd_tn(wb[sl], _b(dvn))
        dvnew = jnp.concatenate(dvnew_l, axis=0)
        dkdec = jnp.concatenate(dkdec_l, axis=0)
        dgl = jnp.concatenate(dgl_l, axis=0)
        dw = -_bd_nt(_b(dvnew), sb)
        duw = _hi_b_tn(t, jnp.concatenate([dvnew, dw], axis=2))
        dvb, dkbg = duw[:, :, :128], duw[:, :, 128:]
        da = -jnp.where(row > col, _hi_b_nt(duw, uw), 0.0)
        dp = da * c["gam_s"]
        dqk = daqk * c["gam_i"]
        m = da * c["a"] + daqk * c["aqk"]
        csum = _hi_b_tn(m, ones)[:, :, 0:1]
        kk = dkdec * c["kdec"]

        def lsum(a):
            return jnp.sum(a, axis=2, keepdims=True)

        dgv = lsum(m) - csum + lsum(dqdec * c["qdec"]) - lsum(kk) + lsum(dkbg * c["kbg"])
        dgv = dgv + jnp.where(row1 == CH - 1, dgl + jnp.sum(kk, axis=(1, 2), keepdims=True), 0.0)
        dpb, dqkb = _b(dp), _b(dqk)
        dkb = _bd(dpb, _b(k)) + dkbg * eg
        dk = _bd_tn(dpb, _b(c["kb"])) + _bd_tn(dqkb, _b(q)) + dkdec * c["dec"] + dkb * beta
        dq = _bd(dqkb, _b(k)) + dqdec * eg
        dbeta = lsum(dkb * k) + lsum(dvb * v)
        dv = dvb * beta
        for ci in range(cps):
            rs = slice(ci * CH, (ci + 1) * CH)
            dsm = jnp.zeros((CH, 128), f32)
            for h in range(4):
                b = 4 * ci + h
                dx_ref[rs, h * 128:(h + 1) * 128] = dq[b]
                dx_ref[rs, 512 + h * 128:512 + (h + 1) * 128] = dk[b]
                dx_ref[rs, 1024 + h * 128:1024 + (h + 1) * 128] = dv[b]
                dsm = jnp.where(lane == 8 + h, dgv[b], jnp.where(lane == 12 + h, dbeta[b], dsm))
            dsm_ref[rs, :] = dsm

    G = N // cps
    return pl.pallas_call(
        body, grid=(G,), out_shape=(jax.ShapeDtypeStruct((S, 1536), f32), jax.ShapeDtypeStruct((S, 128), f32)),
        in_specs=[pl.BlockSpec((R, 1536), lambda n: (G - 1 - n, 0)), pl.BlockSpec((R, 128), lambda n: (G - 1 - n, 0)),
                  pl.BlockSpec((4, cps, 1, CH), lambda n: (0, G - 1 - n, 0, 0)), pl.BlockSpec((4, R, CH), lambda n: (0, G - 1 - n, 0)),
                  pl.BlockSpec((cps, 4, 128, 128), lambda n: (G - 1 - n, 0, 0, 0)), pl.BlockSpec((R, 512), lambda n: (G - 1 - n, 0))],
        out_specs=(pl.BlockSpec((R, 1536), lambda n: (G - 1 - n, 0)), pl.BlockSpec((R, 128), lambda n: (G - 1 - n, 0))),
        scratch_shapes=[pltpu.VMEM((4, 128, 128), f32)], compiler_params=_cp(("arbitrary",), VMEM_LIMIT), name=name)(gqkv, sm, gt4, tinv, states, do)


def _mem_attn(q, kv_ref, h):
    s = _dot_nt(q, kv_ref[:, h * 128:(h + 1) * 128]) * MEM_SCALE
    e = jnp.exp(s - jnp.max(s, axis=1, keepdims=True))
    return e / _rowsum(e)


def _gdn_out_norm(ob):
    r = lax.rsqrt(jnp.mean(ob * ob, axis=-1, keepdims=True) + EPS)
    return ob * r, r


def _merge_fwd(x, oa, ob, zb, zf, kv, b_merge, gdn_g, w_branch, w_out, name):
    S = x.shape[0]
    ts = min(S, 256)

    def body(x_ref, oa_ref, ob_ref, mq_ref, az_ref, bz_ref, mz_ref, gt_ref, kv_ref, bm_ref, gg_ref, wb_ref, wo_ref,
             xo_ref, y_ref, mg_ref):
        y_ref[:, 0:512] = _b(oa_ref[...] * _silu(az_ref[...]))
        for h in range(4):
            sl = slice(h * 128, (h + 1) * 128)
            nb, _ = _gdn_out_norm(ob_ref[:, sl])
            y_ref[:, 512 + h * 128:512 + (h + 1) * 128] = _b(nb * gg_ref[...] * _silu(bz_ref[:, sl]))
            pm = _mem_attn(mq_ref[:, sl], kv_ref, h)
            om = _dot(_b(pm), kv_ref[:, 512 + h * 128:512 + (h + 1) * 128])
            y_ref[:, 1024 + h * 128:1024 + (h + 1) * 128] = _b(om * _silu(mz_ref[:, sl]))
        merged = jnp.zeros((ts, D), f32)
        for n in range(3):
            gate = _sig(gt_ref[:, n * D:(n + 1) * D] + bm_ref[:, n * D:(n + 1) * D])
            merged = merged + gate * _dot(y_ref[:, n * 512:(n + 1) * 512], wb_ref[n])
        mb = _b(merged)
        mg_ref[...] = mb
        xo_ref[...] = x_ref[...] + _dot(mb, wo_ref[...])

    def col(w, c):
        return pl.BlockSpec((ts, w), lambda i: (i, c))

    def full(shape):
        return pl.BlockSpec(shape, lambda i: tuple(0 for _ in shape))

    return pl.pallas_call(
        body, grid=(S // ts,),
        out_shape=(jax.ShapeDtypeStruct((S, D), f32), jax.ShapeDtypeStruct((S, 1536), bf16), jax.ShapeDtypeStruct((S, D), bf16)),
        in_specs=[col(D, 0), col(512, 0), col(512, 0), col(512, 3), col(512, 3), col(512, 4), col(512, 5), col(3072, 1),
                  full((256, D)), full((1, 3072)), full((1, 128)), full((3, 512, D)), full((D, D))],
        out_specs=(col(D, 0), col(1536, 0), col(D, 0)),
        compiler_params=_cp(("parallel",), VMEM_LIMIT), name=name)(x, oa, ob, zb, zf, zf, zf, zf, kv, b_merge, gdn_g, w_branch, w_out)


def _merge_bwd(dout, ycat, oa, ob, zb, zf, kv, b_merge, gdn_g, w_branch, w_branch_t, w_out_t, name):
    S = dout.shape[0]
    ts = min(S, 256)

    def body(do_ref, y_ref, oa_ref, ob_ref, mq_ref, az_ref, bz_ref, mz_ref, gt_ref, kv_ref, bm_ref, gg_ref, wb_ref, wbt_ref, wot_ref,
             dpj_ref, dz_ref, dmq_ref, dlt_ref, doab_ref, dob_ref, dkv_ref, dbm_ref, dgg_ref):
        i = pl.program_id(0)
        dmerged = _dot(_b(do_ref[...]), wot_ref[...])
        dys = []
        dbm_parts = []
        for n in range(3):
            cs = slice(n * D, (n + 1) * D)
            gate = _sig(gt_ref[:, cs] + bm_ref[:, cs])
            proj = _dot(y_ref[:, n * 512:(n + 1) * 512], wb_ref[n])
            dlogit = dmerged * proj * gate * (1.0 - gate)
            dz_ref[:, 1536 + n * D:1536 + (n + 1) * D] = _b(dlogit)
            dbm_parts.append(_colsum(dlogit))
            dproj = _b(dmerged * gate)
            dpj_ref[:, cs] = dproj
            dys.append(_dot(dproj, wbt_ref[n]))
        dbm = jnp.broadcast_to(jnp.concatenate(dbm_parts, axis=1), (8, 3072))
        az = az_ref[...]
        oa = oa_ref[...]
        doa = dys[0] * _silu(az)
        doab_ref[...] = _b(doa)
        prod = doa * oa
        lane = lax.broadcasted_iota(jnp.int32, (ts, 128), 1)
        dl = jnp.zeros((ts, 128), f32)
        for p in range(4):
            blk = prod[:, p * 128:(p + 1) * 128]
            dl = jnp.where(lane == 2 * p, _rowsum(jnp.where(lane < 64, blk, 0.0)),
                           jnp.where(lane == 2 * p + 1, _rowsum(jnp.where(lane >= 64, blk, 0.0)), dl))
        dlt_ref[...] = jnp.transpose(dl)[0:8, :]
        dz_ref[:, 0:512] = _b(dys[0] * oa * _dsilu(az))
        gg = gg_ref[...]
        dgg = jnp.zeros((1, 128), f32)
        dkv_parts_k, dkv_parts_v = [], []
        for h in range(4):
            sl = slice(h * 128, (h + 1) * 128)
            bz = bz_ref[:, sl]
            dyb = dys[1][:, sl]
            nb, r = _gdn_out_norm(ob_ref[:, sl])
            dz_ref[:, 512 + h * 128:512 + (h + 1) * 128] = _b(dyb * nb * gg * _dsilu(bz))
            dng = dyb * _silu(bz)
            dgg = dgg + _colsum(dng * nb)
            dnb = dng * gg
            dob_ref[:, sl] = r * (dnb - nb * jnp.mean(dnb * nb, axis=-1, keepdims=True))
            mz = mz_ref[:, sl]
            dym = dys[2][:, sl]
            q = mq_ref[:, sl]
            kh = kv_ref[:, sl]
            vh = kv_ref[:, 512 + h * 128:512 + (h + 1) * 128]
            pm = _mem_attn(q, kv_ref, h)
            pmb = _b(pm)
            om = _dot(pmb, vh)
            dz_ref[:, 1024 + h * 128:1024 + (h + 1) * 128] = _b(dym * om * _dsilu(mz))
            dom = _b(dym * _silu(mz))
            dkv_parts_v.append(_dot_tn(pmb, dom))
            dpm = _dot_nt(dom, vh)
            dsm = _b(pm * (dpm - _rowsum(dpm * pm)) * MEM_SCALE)
            dmq_ref[:, sl] = _b(_dot(dsm, kh))
            dkv_parts_k.append(_dot_tn(dsm, q))
        dkv = jnp.concatenate(dkv_parts_k + dkv_parts_v, axis=1)
        dggb = jnp.broadcast_to(dgg, (8, 128))

        @pl.when(i == 0)
        def _():
            dkv_ref[...] = dkv
            dbm_ref[...] = dbm
            dgg_ref[...] = dggb

        @pl.when(i > 0)
        def _():
            dkv_ref[...] += dkv
            dbm_ref[...] += dbm
            dgg_ref[...] += dggb

    def col(w, c):
        return pl.BlockSpec((ts, w), lambda i: (i, c))

    def full(shape):
        return pl.BlockSpec(shape, lambda i: tuple(0 for _ in shape))

    return pl.pallas_call(
        body, grid=(S // ts,),
        out_shape=(jax.ShapeDtypeStruct((S, 3072), bf16), jax.ShapeDtypeStruct((S, 4608), bf16), jax.ShapeDtypeStruct((S, 512), bf16),
                   jax.ShapeDtypeStruct((8, S), f32), jax.ShapeDtypeStruct((S, 512), bf16), jax.ShapeDtypeStruct((S, 512), f32),
                   jax.ShapeDtypeStruct((256, D), f32), jax.ShapeDtypeStruct((8, 3072), f32), jax.ShapeDtypeStruct((8, 128), f32)),
        in_specs=[col(D, 0), col(1536, 0), col(512, 0), col(512, 0), col(512, 3), col(512, 3), col(512, 4), col(512, 5), col(3072, 1),
                  full((256, D)), full((1, 3072)), full((1, 128)), full((3, 512, D)), full((3, D, 512)), full((D, D))],
        out_specs=(col(3072, 0), col(4608, 0), col(512, 0), pl.BlockSpec((8, ts), lambda i: (0, i)), col(512, 0), col(512, 0),
                   full((256, D)), full((8, 3072)), full((8, 128))),
        compiler_params=_cp(("arbitrary",), VMEM_LIMIT), name=name)(
            dout, ycat, oa, ob, zb, zf, zf, zf, zf, kv, b_merge, gdn_g, w_branch, w_branch_t, w_out_t)


def _mesh_pos():
    return lax.axis_index("x"), lax.axis_index("y"), lax.axis_index("c")


def _all_gather(xs, name):
    n = len(xs)

    def body(*refs):
        x_refs, out_refs = refs[:n], refs[n:2 * n]
        send_sems, recv_sems, local_sems = refs[2 * n:]
        mx, my, mc = _mesh_pos()
        me, sibling = (mx, my, mc), (mx, my, 1 - mc)
        chips = [(1 - mx, my), (mx, 1 - my), (1 - mx, 1 - my)]

        def copy(a, k, block, to, src=None):
            px, py, pc = block
            slot = out_refs[a].at[4 * px + 2 * py + pc]
            return pltpu.make_async_remote_copy(
                src_ref=slot if src is None else src, dst_ref=slot,
                send_sem=send_sems.at[7 * a + k], recv_sem=recv_sems.at[7 * a + k], device_id=to, device_id_type=pl.DeviceIdType.MESH)

        mine = [pltpu.make_async_copy(x_refs[a], out_refs[a].at[4 * mx + 2 * my + mc], local_sems.at[a]) for a in range(n)]
        for cp in mine:
            cp.start()
        first = []
        for a in range(n):
            first.append(copy(a, 0, me, sibling, src=x_refs[a]))
            first += [copy(a, 1 + j, me, (*chip, mc), src=x_refs[a]) for j, chip in enumerate(chips)]
        for cp in first:
            cp.start()
        passed = []
        for j, chip in enumerate(chips):
            for a in range(n):
                copy(a, 1 + j, (*chip, mc), me).wait_recv()
                fwd = copy(a, 4 + j, (*chip, mc), sibling)
                fwd.start()
                passed.append(fwd)
        for a in range(n):
            copy(a, 0, sibling, me).wait_recv()
            for j, chip in enumerate(chips):
                copy(a, 4 + j, (*chip, 1 - mc), me).wait_recv()
        for cp in first + passed:
            cp.wait_send()
        for cp in mine:
            cp.wait()

    anyspec = pl.BlockSpec(memory_space=pl.ANY)
    return pl.pallas_call(
        body, out_shape=tuple(jax.ShapeDtypeStruct((N_DEV,) + x.shape, x.dtype) for x in xs),
        in_specs=[anyspec] * n, out_specs=tuple([anyspec] * n),
        scratch_shapes=[pltpu.SemaphoreType.DMA((7 * n,)), pltpu.SemaphoreType.DMA((7 * n,)), pltpu.SemaphoreType.DMA((n,))],
        name=name)(*xs)


def _exchange(sends, name):
    n = len(sends)

    def body(*refs):
        s_refs, r_refs = refs[:n], refs[n:2 * n]
        send_sems, recv_sems, local_sems = refs[2 * n:]
        mx, my, mc = _mesh_pos()
        me_id = 4 * mx + 2 * my + mc
        mine = [pltpu.make_async_copy(s_refs[a].at[me_id], r_refs[a].at[me_id], local_sems.at[a]) for a in range(n)]
        for cp in mine:
            cp.start()
        copies = []
        for k in range(1, N_DEV):
            px = 1 - mx if k & 4 else mx
            py = 1 - my if k & 2 else my
            pc = 1 - mc if k & 1 else mc
            for a in range(n):
                copies.append(pltpu.make_async_remote_copy(
                    src_ref=s_refs[a].at[4 * px + 2 * py + pc], dst_ref=r_refs[a].at[me_id],
                    send_sem=send_sems.at[7 * a + k - 1], recv_sem=recv_sems.at[7 * a + k - 1],
                    device_id=(px, py, pc), device_id_type=pl.DeviceIdType.MESH))
        for cp in copies:
            cp.start()
        for cp in copies:
            cp.wait()
        for cp in mine:
            cp.wait()

    anyspec = pl.BlockSpec(memory_space=pl.ANY)
    return pl.pallas_call(
        body, out_shape=tuple(jax.ShapeDtypeStruct(s.shape, s.dtype) for s in sends),
        in_specs=[anyspec] * n, out_specs=tuple([anyspec] * n),
        scratch_shapes=[pltpu.SemaphoreType.DMA((7 * n,)), pltpu.SemaphoreType.DMA((7 * n,)), pltpu.SemaphoreType.DMA((n,))],
        name=name)(*sends)


ADAMW_BLOCK_BYTES = 4 * 1024 * 1024


def _adamw(parts, w, m, v, name):
    _, R, C = parts.shape
    tr = R
    for t in (1024, 512, 256, 128, 64, 32, 16, 8):
        if R % t == 0 and N_DEV * t * C * 4 <= ADAMW_BLOCK_BYTES:
            tr = t
            break

    def body(p_ref, w_ref, m_ref, v_ref, g_ref, d_ref, nm_ref, nv_ref):
        g = p_ref[0].astype(f32)
        for j in range(1, N_DEV):
            g = g + p_ref[j].astype(f32)
        mn = ADAM_B1 * m_ref[...] + (1.0 - ADAM_B1) * g
        vn = ADAM_B2 * v_ref[...] + (1.0 - ADAM_B2) * jnp.square(g)
        m_hat = mn / (1.0 - ADAM_B1 ** ADAM_STEP)
        v_hat = vn / (1.0 - ADAM_B2 ** ADAM_STEP)
        g_ref[...] = g
        d_ref[...] = -ADAM_LR * (m_hat / (jnp.sqrt(v_hat) + ADAM_EPS) + ADAM_WD * w_ref[...])
        nm_ref[...] = mn
        nv_ref[...] = vn

    t2 = pl.BlockSpec((tr, C), lambda i: (i, 0))
    out = jax.ShapeDtypeStruct((R, C), f32)
    return pl.pallas_call(
        body, grid=(R // tr,), out_shape=(out, out, out, out),
        in_specs=[pl.BlockSpec((N_DEV, tr, C), lambda i: (0, i, 0)), t2, t2, t2], out_specs=(t2, t2, t2, t2),
        compiler_params=_cp(("parallel",), VMEM_LIMIT), name=name)(parts, w, m, v)


def _as2d(a):
    return a.reshape(-1, a.shape[-1])


def _perm_cols(w):
    parts = [w[..., _COLS[n][0]:_COLS[n][1]] for n in _ORDER]
    pad = jnp.zeros(w.shape[:-1] + (N_ALL - N_IN,), w.dtype)
    return jnp.concatenate(parts + [pad], axis=-1)


def _unperm_cols(w):
    pieces, off = {}, 0
    for n in _ORDER:
        width = _COLS[n][1] - _COLS[n][0]
        pieces[n] = w[..., off:off + width]
        off += width
    return jnp.concatenate([pieces[n] for n in sorted(_COLS, key=lambda n: _COLS[n][0])], axis=-1)


_SMALL_ROWS = 16


def _pack_small(t):
    z = jnp.zeros((D,), f32)
    misc = z.at[0:16].set(t["b_fg"].reshape(-1)).at[16:24].set(t["a_log"].reshape(-1)).at[24:32].set(t["dt_bias"].reshape(-1))
    misc = misc.at[128:384].set(t["gdn_norm_g"].reshape(-1))
    if "extra" in t:
        misc = misc.at[512].set(t["extra"])
    rows = [t["norm_g"], t["b_merge"].reshape(6, D), t["mem_norm_g"], t["final_norm_g"][None], misc[None],
            jnp.zeros((_SMALL_ROWS - 12, D), f32)]
    return jnp.concatenate(rows, axis=0)


def _unpack_small(a):
    misc = a[11]
    return dict(norm_g=a[0:2], b_merge=a[2:8].reshape(2, 3072), mem_norm_g=a[8:10], final_norm_g=a[10],
                b_fg=misc[0:16].reshape(2, 8), a_log=misc[16:24].reshape(2, 4), dt_bias=misc[24:32].reshape(2, 4),
                gdn_norm_g=misc[128:384].reshape(2, 128), extra=misc[512])


def _layer_fwd(l, x, mem, p):
    sfx = f"_l{l}"
    h = _norm_fwd(x, p["norm_g"], "norm_fwd" + sfx)
    zb = _mm(h, p["w_b"], bf16, 512, 1024, 1024, "inproj_b" + sfx)
    zf = _mm(h, p["w_f"], f32, 512, 1024, 1024, "inproj_f" + sfx)
    zs = _mm(h, p["w_s"], f32, 512, 128, 1024, "inproj_s" + sfx)
    sm = _small_prep(zs, p["par"], "small_prep" + sfx)
    S = x.shape[0]
    gt4 = jnp.transpose(sm[:, 8:12]).reshape(4, S // CH, 1, CH)
    qa, ka, st = _fox_prep(zb, sm, "fox_prep" + sfx)
    tab = _fox_bound_table(st, S, min(S, FOX_TILE))
    oa, lse_t = _fox_fwd(qa, ka, zb, tab, "fox_fwd" + sfx)
    gqkv = _gdn_prep(zf, p["conv_w"], "gdn_prep" + sfx)
    ob, tinv, states = _gdn_fwd(gqkv, sm, gt4, "gdn_fwd" + sfx)
    memn = _norm_fwd(mem, p["mem_norm_g"], "mem_norm" + sfx)
    kv = _mm(memn, p["w_mem_kv"], bf16, 256, 1024, 1024, "mem_kv" + sfx)
    xo, ycat, merged = _merge_fwd(x, oa, ob, zb, zf, kv, p["b_merge"], p["gdn_norm_g"], p["w_branch"], p["w_out"], "merge_fwd" + sfx)
    saved = dict(x=x, h=h, zb=zb, zf=zf, zs=zs, sm=sm, qa=qa, ka=ka, tab=tab, gt4=gt4, oa=oa, lse_t=lse_t, gqkv=gqkv, ob=ob, tinv=tinv,
                 states=states, memn=memn, kv=kv, ycat=ycat, merged=merged)
    return xo, saved


def _layer_bwd(l, dout, mem, p, s):
    sfx = f"_l{l}"
    dproj, dzf2, dmq, delta, doab, dob, dkv, dbm, dgg = _merge_bwd(
        dout, s["ycat"], s["oa"], s["ob"], s["zb"], s["zf"], s["kv"], p["b_merge"], p["gdn_norm_g"],
        p["w_branch"], p["w_branch_t"], p["w_out_t"], "merge_bwd" + sfx)
    g = {}
    g["w_out"] = _mm(s["merged"], dout, f32, 512, 1024, 512, "dw_out" + sfx, trans_a=True)
    g["w_branch"] = jnp.stack([
        _mm(s["ycat"], dproj, f32, 512, 1024, 512, f"dw_branch{n}" + sfx, trans_a=True, a_cols=(n * 512, 512), b_cols=(n * D, D))
        for n in range(3)])
    g["b_merge"] = dbm[0]
    g["gdn_norm_g"] = dgg[0]
    g["w_mem_kv"] = _mm(s["memn"], dkv, f32, 512, 1024, 256, "dw_mem_kv" + sfx, trans_a=True)
    dmemn = _mm(dkv, p["w_mem_kv_t"], f32, 256, 1024, 1024, "dmem_n" + sfx)
    g["mem_norm_g"] = _norm_bwd(mem, p["mem_norm_g"], dmemn, None, "mem_norm_bwd" + sfx)[0]
    dgqkv, dsm = _gdn_bwd(s["gqkv"], s["sm"], s["gt4"], s["tinv"], s["states"], dob, "gdn_bwd" + sfx)
    dbqkv, dcw = _gdn_prep_bwd(s["zf"], p["conv_w"], dgqkv, "gdn_prep_bwd" + sfx)
    g["conv_w"] = dcw[0:4]
    dq, dk, dv, dfc, dfr = _fox_bwd(s["qa"], s["ka"], s["zb"], doab, _head_rows(s["lse_t"]), delta, s["tab"], "fox_bwd" + sfx)
    dzs, sacc = _small_bwd(s["zs"], p["par"], _head_rows(dfr), dfc, dsm, "small_bwd" + sfx)
    g["b_fg"], g["a_log"], g["dt_bias"] = sacc[0, 0:8], sacc[1, 8:12], sacc[2, 8:12]
    dz = jnp.concatenate([dq, dk, dv, dmq, dbqkv, dzf2, dzs], axis=1)
    dh = _mm(dz, p["w_all_t"], f32, 512, 1024, 1664, "dh" + sfx)
    g["w_in"] = _mm(s["h"], dz, f32, 512, 1664, 512, "dw_in" + sfx, trans_a=True)
    dx, dng = _norm_bwd(s["x"], p["norm_g"], dh, dout, "norm_bwd" + sfx)
    g["norm_g"] = dng[0]
    return dx, g


def kernel(x, mem, norm_g, w_in, b_fg, b_merge, conv_w, a_log, dt_bias, gdn_norm_g, mem_norm_g, w_mem_kv, w_branch, w_out, final_norm_g, loss_target, m_norm_g, m_w_in, m_b_fg, m_b_merge, m_conv_w, m_a_log, m_dt_bias, m_gdn_norm_g, m_mem_norm_g, m_w_mem_kv, m_w_branch, m_w_out, m_final_norm_g, v_norm_g, v_w_in, v_b_fg, v_b_merge, v_conv_w, v_a_log, v_dt_bias, v_gdn_norm_g, v_mem_norm_g, v_w_mem_kv, v_w_branch, v_w_out, v_final_norm_g):
    x0, mem0, tgt = x[0], mem[0], loss_target[0]
    shard_w = dict(w_in=w_in, w_mem_kv=w_mem_kv, w_branch=w_branch, w_out=w_out, conv_w=conv_w)
    shard_m = dict(w_in=m_w_in, w_mem_kv=m_w_mem_kv, w_branch=m_w_branch, w_out=m_w_out, conv_w=m_conv_w)
    shard_v = dict(w_in=v_w_in, w_mem_kv=v_w_mem_kv, w_branch=v_w_branch, w_out=v_w_out, conv_w=v_conv_w)
    small_w = dict(norm_g=norm_g, b_fg=b_fg, b_merge=b_merge, a_log=a_log, dt_bias=dt_bias, gdn_norm_g=gdn_norm_g,
                   mem_norm_g=mem_norm_g, final_norm_g=final_norm_g)
    small_m = dict(norm_g=m_norm_g, b_fg=m_b_fg, b_merge=m_b_merge, a_log=m_a_log, dt_bias=m_dt_bias, gdn_norm_g=m_gdn_norm_g,
                   mem_norm_g=m_mem_norm_g, final_norm_g=m_final_norm_g)
    small_v = dict(norm_g=v_norm_g, b_fg=v_b_fg, b_merge=v_b_merge, a_log=v_a_log, dt_bias=v_dt_bias, gdn_norm_g=v_gdn_norm_g,
                   mem_norm_g=v_mem_norm_g, final_norm_g=v_final_norm_g)

    g_in, g_kv, g_br, g_out, conv_all = _all_gather(
        [_b(_as2d(w_in)), _b(_as2d(w_mem_kv)), _b(_as2d(w_branch)), _b(_as2d(w_out)), _as2d(conv_w)], "gather_weights")
    conv_full = jnp.transpose(conv_all.reshape(N_DEV, DEPTH, 4, 192), (1, 2, 0, 3)).reshape(DEPTH, 4, 1536)
    w_in_full = jnp.transpose(g_in.reshape(N_DEV, DEPTH, D, 1026), (1, 2, 0, 3)).reshape(DEPTH, D, N_IN)
    w_all = _perm_cols(w_in_full)
    w_kv_full = jnp.transpose(g_kv.reshape(N_DEV, DEPTH, 128, D), (1, 0, 2, 3)).reshape(DEPTH, D, D)
    w_br_full = jnp.transpose(g_br.reshape(N_DEV, DEPTH, 3, 512, 128), (1, 2, 3, 0, 4)).reshape(DEPTH, 3, 512, D)
    w_out_full = jnp.transpose(g_out.reshape(N_DEV, DEPTH, 128, D), (1, 0, 2, 3)).reshape(DEPTH, D, D)

    layers = []
    for l in range(DEPTH):
        layers.append(dict(
            norm_g=norm_g[l][None], mem_norm_g=mem_norm_g[l][None], gdn_norm_g=gdn_norm_g[l][None], b_merge=b_merge[l][None],
            par=_small_pars(b_fg[l], a_log[l], dt_bias[l]),
            conv_w=jnp.pad(conv_full[l], ((0, 4), (0, 0))),
            w_b=w_all[l][:, 0:NB], w_f=w_all[l][:, NB:NB + NF], w_s=w_all[l][:, NB + NF:], w_all_t=jnp.transpose(w_all[l]),
            w_mem_kv=w_kv_full[l], w_mem_kv_t=jnp.transpose(w_kv_full[l]),
            w_branch=w_br_full[l], w_branch_t=jnp.transpose(w_br_full[l], (0, 2, 1)),
            w_out=w_out_full[l], w_out_t=jnp.transpose(w_out_full[l])))

    acts, saved = x0, []
    for l in range(DEPTH):
        acts, s = _layer_fwd(l, acts, mem0, layers[l])
        saved.append(s)
    dx, dfg, lsum = _loss_head(acts, final_norm_g[None], tgt, "loss_head")

    grads = [None] * DEPTH
    for l in reversed(range(DEPTH)):
        dx, grads[l] = _layer_bwd(l, dx, mem0, layers[l], saved[l])
    grad_x = dx[None]

    def per_dev(name):
        return jnp.stack([grads[l][name] for l in range(DEPTH)])

    dw_in = _unperm_cols(per_dev("w_in"))
    send = dict(
        w_in=jnp.transpose(dw_in.reshape(DEPTH, D, N_DEV, 1026), (2, 0, 1, 3)).reshape(N_DEV, DEPTH * D, 1026),
        w_mem_kv=jnp.transpose(per_dev("w_mem_kv").reshape(DEPTH, N_DEV, 128, D), (1, 0, 2, 3)).reshape(N_DEV, DEPTH * 128, D),
        w_branch=jnp.transpose(per_dev("w_branch").reshape(DEPTH, 3, 512, N_DEV, 128), (3, 0, 1, 2, 4)).reshape(N_DEV, DEPTH * 3 * 512, 128),
        w_out=jnp.transpose(per_dev("w_out").reshape(DEPTH, N_DEV, 128, D), (1, 0, 2, 3)).reshape(N_DEV, DEPTH * 128, D),
        conv_w=jnp.transpose(per_dev("conv_w").reshape(DEPTH, 4, N_DEV, 192), (2, 0, 1, 3)).reshape(N_DEV, DEPTH * 4, 192))
    parts = dict(zip(_SHARDED, _exchange([_b(send[n]) for n in _SHARDED], "scatter_grads")))
    big = [{}, {}, {}, {}]
    for n in _SHARDED:
        res = _adamw(parts[n], _as2d(shard_w[n]), _as2d(shard_m[n]), _as2d(shard_v[n]), "adamw_" + n)
        for kind in range(4):
            big[kind][n] = res[kind].reshape(shard_w[n].shape)

    small_g = {k: jnp.stack([grads[l][k] for l in range(DEPTH)]) for k in ("norm_g", "b_fg", "b_merge", "a_log", "dt_bias", "gdn_norm_g", "mem_norm_g")}
    small_g["final_norm_g"] = dfg[0]
    small_g["extra"] = lsum[0, 0]
    parts_s, = _all_gather([_pack_small(small_g)], "gather_small")
    g_sm, d_sm, m_sm, v_sm = _adamw(parts_s, _pack_small(small_w), _pack_small(small_m), _pack_small(small_v), "adamw_replicated")

    sml = [_unpack_small(a) for a in (g_sm, d_sm, m_sm, v_sm)]
    loss = sml[0]["extra"]
    names = ("norm_g", "w_in", "b_fg", "b_merge", "conv_w", "a_log", "dt_bias", "gdn_norm_g", "mem_norm_g", "w_mem_kv", "w_branch", "w_out", "final_norm_g")
    outs = [loss, grad_x]
    for kind in range(4):
        for n in names:
            outs.append(big[kind][n] if n in big[kind] else sml[kind][n])
    return tuple(outs)
```

```python
import functools

import jax
import jax.numpy as jnp
from jax import lax
from jax.experimental import pallas as pl
from jax.experimental.pallas import tpu as pltpu

f32, bf16 = jnp.float32, jnp.bfloat16

D = 1024
EPS = 1e-6
CH = 64
N_DEV = 8
DEPTH = 2
FOX_SCALE = 64 ** -0.5
GDN_SCALE = 128 ** -0.5
MEM_SCALE = 128 ** -0.5
NEG = -1e30
VMEM_LIMIT = 56 * 1024 * 1024

ADAM_LR, ADAM_B1, ADAM_B2, ADAM_EPS, ADAM_WD, ADAM_STEP = 0.001, 0.9, 0.999, 1e-08, 0.01, 10

_COLS = dict(aq=(0, 512), ak=(512, 1024), av=(1024, 1536), af=(1536, 1544), az=(1544, 2056),
             bq=(2056, 2568), bk=(2568, 3080), bv=(3080, 3592), ba=(3592, 3596), bb=(3596, 3600),
             bz=(3600, 4112), mq=(4112, 4624), mz=(4624, 5136), gates=(5136, 8208))
_ORDER = ("aq", "ak", "av", "mq", "bq", "bk", "bv", "az", "bz", "mz", "gates", "af", "ba", "bb")
_ORDER_BWD = ("az", "bz", "mz", "gates", "aq", "ak", "av", "mq", "bq", "bk", "bv", "af", "ba", "bb")
DZ_MERGE_COLS = 4608
N_IN = 8208
NB, NF, NS = 2048, 6144, 128
N_ALL = NB + NF + NS

_SHARDED = ("w_in", "w_mem_kv", "w_branch", "w_out", "conv_w")


def _cp(sem=None, vmem=None):
    kw = {}
    if sem is not None:
        kw["dimension_semantics"] = sem
    if vmem is not None:
        kw["vmem_limit_bytes"] = vmem
    return pltpu.CompilerParams(**kw)


def _dot(a, b):
    return jnp.dot(a, b, preferred_element_type=f32)


def _dot_nt(a, b):
    return lax.dot_general(a, b, (((1,), (1,)), ((), ())), preferred_element_type=f32)


def _dot_tn(a, b):
    return lax.dot_general(a, b, (((0,), (0,)), ((), ())), preferred_element_type=f32)


def _split2(x):
    hi = x.astype(bf16)
    return hi, (x - hi.astype(f32)).astype(bf16)


def _mm3(a, b, dims):
    ah, al = _split2(a)
    bh, bl = _split2(b)
    dg = functools.partial(lax.dot_general, dimension_numbers=dims, preferred_element_type=f32)
    return dg(ah, bh) + (dg(ah, bl) + dg(al, bh))


def _hi(a, b):
    return _mm3(a, b, (((1,), (0,)), ((), ())))


def _hi_nt(a, b):
    return _mm3(a, b, (((1,), (1,)), ((), ())))


def _hi_tn(a, b):
    return _mm3(a, b, (((0,), (0,)), ((), ())))


def _hi_b(a, b):
    return _mm3(a, b, (((2,), (1,)), ((0,), (0,))))


def _b(x):
    return x.astype(bf16)


def _sig(x):
    return jax.nn.sigmoid(x)


def _silu(x):
    return x * _sig(x)


def _dsilu(x):
    s = _sig(x)
    return s * (1.0 + x * (1.0 - s))


def _softplus(x):
    return jnp.maximum(x, 0.0) + jnp.log1p(jnp.exp(-jnp.abs(x)))


def _rowsum(x):
    return jnp.sum(x, axis=1, keepdims=True)


def _colsum(x):
    return jnp.sum(x, axis=0, keepdims=True)


def _norm_fwd(x, g, name, with_t=False):
    M = x.shape[0]
    ts = min(M, 512)

    def body(x_ref, g_ref, h_ref, *t_ref):
        xv = x_ref[...]
        r = lax.rsqrt(jnp.mean(xv * xv, axis=-1, keepdims=True) + EPS)
        h = xv * r * g_ref[...]
        h_ref[...] = _b(h)
        if with_t:
            t_ref[0][...] = _b(jnp.transpose(h))

    tile = pl.BlockSpec((ts, D), lambda i: (i, 0))
    shapes, specs = jax.ShapeDtypeStruct((M, D), bf16), tile
    if with_t:
        shapes, specs = (shapes, jax.ShapeDtypeStruct((D, M), bf16)), (tile, pl.BlockSpec((D, ts), lambda i: (0, i)))
    return pl.pallas_call(
        body, grid=(M // ts,), out_shape=shapes,
        in_specs=[tile, pl.BlockSpec((1, D), lambda i: (0, 0))],
        out_specs=specs, compiler_params=_cp(("parallel",)), name=name)(x, g)


def _norm_bwd(x, g, dh, dres, name):
    M = x.shape[0]
    ts = min(M, 512)
    with_dx = dres is not None

    def body(*refs):
        if with_dx:
            x_ref, g_ref, dh_ref, dres_ref, dx_ref, dg_ref = refs
        else:
            x_ref, g_ref, dh_ref, dg_ref = refs
        i = pl.program_id(0)
        xv = x_ref[...]
        r = lax.rsqrt(jnp.mean(xv * xv, axis=-1, keepdims=True) + EPS)
        xh = xv * r
        dh = dh_ref[...].astype(f32)
        part = jnp.broadcast_to(_colsum(dh * xh), (8, D))

        @pl.when(i == 0)
        def _():
            dg_ref[...] = part

        @pl.when(i > 0)
        def _():
            dg_ref[...] += part

        if with_dx:
            dxh = dh * g_ref[...]
            dx_ref[...] = dres_ref[...] + r * (dxh - xh * jnp.mean(dxh * xh, axis=-1, keepdims=True))

    tile = pl.BlockSpec((ts, D), lambda i: (i, 0))
    gspec = pl.BlockSpec((1, D), lambda i: (0, 0))
    acc = pl.BlockSpec((8, D), lambda i: (0, 0))
    if with_dx:
        return pl.pallas_call(
            body, grid=(M // ts,), out_shape=(jax.ShapeDtypeStruct((M, D), f32), jax.ShapeDtypeStruct((8, D), f32)),
            in_specs=[tile, gspec, tile, tile], out_specs=(tile, acc), compiler_params=_cp(("arbitrary",)), name=name)(x, g, dh, dres)
    return pl.pallas_call(
        body, grid=(M // ts,), out_shape=jax.ShapeDtypeStruct((8, D), f32),
        in_specs=[tile, gspec, tile], out_specs=acc, compiler_params=_cp(("arbitrary",)), name=name)(x, g, dh)


def _loss_head(x, g, tgt, name):
    M = x.shape[0]
    ts = min(M, 512)

    def body(x_ref, g_ref, t_ref, dx_ref, dg_ref, ls_ref):
        i = pl.program_id(0)
        xv = x_ref[...]
        gv = g_ref[...]
        r = lax.rsqrt(jnp.mean(xv * xv, axis=-1, keepdims=True) + EPS)
        xh = xv * r
        e = xh * gv - t_ref[...]
        lpart = 0.5 * jnp.sum(jnp.mean(e * e, axis=-1, keepdims=True), axis=0, keepdims=True)
        dy = e * (1.0 / D)
        dgp = jnp.broadcast_to(_colsum(dy * xh), (8, D))
        lp = jnp.broadcast_to(lpart, (8, 128))

        @pl.when(i == 0)
        def _():
            dg_ref[...] = dgp
            ls_ref[...] = lp

        @pl.when(i > 0)
        def _():
            dg_ref[...] += dgp
            ls_ref[...] += lp

        dxh = dy * gv
        dx_ref[...] = r * (dxh - xh * jnp.mean(dxh * xh, axis=-1, keepdims=True))

    tile = pl.BlockSpec((ts, D), lambda i: (i, 0))
    return pl.pallas_call(
        body, grid=(M // ts,),
        out_shape=(jax.ShapeDtypeStruct((M, D), f32), jax.ShapeDtypeStruct((8, D), f32), jax.ShapeDtypeStruct((8, 128), f32)),
        in_specs=[tile, pl.BlockSpec((1, D), lambda i: (0, 0)), tile],
        out_specs=(tile, pl.BlockSpec((8, D), lambda i: (0, 0)), pl.BlockSpec((8, 128), lambda i: (0, 0))),
        compiler_params=_cp(("arbitrary",)), name=name)(x, g, tgt)


def _mm(a, b, out_dtype, tm, tn, tk, name, trans_a=False, a_cols=None, b_cols=None):
    if trans_a:
        K, M = a.shape
    else:
        M, K = a.shape
    N = b.shape[1]
    a0, b0 = 0, 0
    if a_cols is not None:
        a0, M = a_cols
    if b_cols is not None:
        b0, N = b_cols
    tm, tn, tk = min(tm, M), min(tn, N), min(tk, K)
    nk = K // tk
    a0, b0 = a0 // tm, b0 // tn

    def body(a_ref, b_ref, o_ref, acc_ref):
        k = pl.program_id(2)
        av, bv = _b(a_ref[...]), _b(b_ref[...])
        part = _dot_tn(av, bv) if trans_a else _dot(av, bv)
        if nk == 1:
            o_ref[...] = part.astype(out_dtype)
        else:
            @pl.when(k == 0)
            def _():
                acc_ref[...] = part

            @pl.when(k > 0)
            def _():
                acc_ref[...] += part

            @pl.when(k == nk - 1)
            def _():
                o_ref[...] = acc_ref[...].astype(out_dtype)

    a_spec = pl.BlockSpec((tk, tm), lambda i, j, k: (k, i + a0)) if trans_a else pl.BlockSpec((tm, tk), lambda i, j, k: (i, k))
    return pl.pallas_call(
        body, grid=(M // tm, N // tn, nk), out_shape=jax.ShapeDtypeStruct((M, N), out_dtype),
        in_specs=[a_spec, pl.BlockSpec((tk, tn), lambda i, j, k: (k, j + b0))],
        out_specs=pl.BlockSpec((tm, tn), lambda i, j, k: (i, j)),
        scratch_shapes=[pltpu.VMEM((tm, tn), f32)],
        compiler_params=_cp(("parallel", "parallel", "arbitrary"), VMEM_LIMIT), name=name)(a, b)


def _small_pars(b_fg, a_log, dt_bias):
    par = jnp.zeros((8, 128), f32)
    par = par.at[0, 0:8].set(b_fg).at[1, 8:12].set(a_log).at[2, 8:12].set(dt_bias)
    return par


def _small_prep(zs, par, name):
    S = zs.shape[0]
    ts = min(S, 512)

    def body(z_ref, par_ref, o_ref, carry_ref):
        i = pl.program_id(0)

        @pl.when(i == 0)
        def _():
            carry_ref[...] = jnp.zeros_like(carry_ref)

        z = z_ref[...]
        lane = lax.broadcasted_iota(jnp.int32, (ts, 128), 1)
        row = lax.broadcasted_iota(jnp.int32, (ts, 128), 0)
        za = z + par_ref[0:1, :]
        logf = jnp.minimum(za, 0.0) - jnp.log1p(jnp.exp(-jnp.abs(za)))
        glog = -jnp.exp(par_ref[1:2, :]) * _softplus(z + par_ref[2:3, :])
        x = jnp.where(lane < 8, logf, jnp.where(lane < 12, glog, 0.0))
        pos = jnp.where(lane < 8, row, row & (CH - 1))
        s = 1
        while s < ts:
            x = x + jnp.where(pos >= s, pltpu.roll(x, s, 0), 0.0)
            s *= 2
        tot = x + carry_ref[0:1, :]
        carry_ref[...] = jnp.broadcast_to(jnp.where(lane[0:1] < 8, tot[ts - 1:ts, :], 0.0), (8, 128))
        o_ref[...] = jnp.where(lane < 8, tot, jnp.where(lane < 12, x, jnp.where(lane < 16, _sig(z), 0.0)))

    return pl.pallas_call(
        body, grid=(S // ts,), out_shape=jax.ShapeDtypeStruct((S, 128), f32),
        in_specs=[pl.BlockSpec((ts, 128), lambda i: (i, 0)), pl.BlockSpec((8, 128), lambda i: (0, 0))],
        out_specs=pl.BlockSpec((ts, 128), lambda i: (i, 0)), scratch_shapes=[pltpu.VMEM((8, 128), f32)],
        compiler_params=_cp(("arbitrary",)), name=name)(zs, par)


def _small_bwd(zs, par, dfr, dfc, dsm, name):
    S = zs.shape[0]
    ts = min(S, 512)
    nt = S // ts

    def body(z_ref, par_ref, dfr_ref, dfc_ref, dsm_ref, dz_ref, acc_ref, carry_ref):
        i = pl.program_id(0)

        @pl.when(i == 0)
        def _():
            carry_ref[...] = jnp.zeros_like(carry_ref)

        z = z_ref[...]
        dsm_v = dsm_ref[...]
        lane = lax.broadcasted_iota(jnp.int32, (ts, 128), 1)
        row = lax.broadcasted_iota(jnp.int32, (ts, 128), 0)
        df = jnp.transpose(jnp.concatenate([dfr_ref[...], jnp.zeros((120, ts), f32)], axis=0))
        for p in range(4):
            dpair = dfc_ref[p]
            df = df - jnp.where(lane == 2 * p, dpair[:, 0:1], jnp.where(lane == 2 * p + 1, dpair[:, 64:65], 0.0))
        x = jnp.where(lane < 8, df, jnp.where(lane < 12, dsm_v, 0.0))
        pos = jnp.where(lane < 8, row, row & (CH - 1))
        seg = jnp.where(lane < 8, ts, CH)
        s = 1
        while s < ts:
            x = x + jnp.where(pos + s < seg, pltpu.roll(x, ts - s, 0), 0.0)
            s *= 2
        tot = x + carry_ref[0:1, :]
        carry_ref[...] = jnp.broadcast_to(jnp.where(lane[0:1] < 8, tot[0:1, :], 0.0), (8, 128))
        za = z + par_ref[0:1, :]
        daf = tot * _sig(-za)
        zb = z + par_ref[2:3, :]
        nea = -jnp.exp(par_ref[1:2, :])
        glog = nea * _softplus(zb)
        dba = x * nea * _sig(zb)
        beta = _sig(z)
        dbb = dsm_v * beta * (1.0 - beta)
        dz_ref[...] = _b(jnp.where(lane < 8, daf, jnp.where(lane < 12, dba, jnp.where(lane < 16, dbb, 0.0))))
        r0 = _colsum(jnp.where(lane < 8, daf, 0.0))
        r1 = _colsum(jnp.where((lane >= 8) & (lane < 12), x * glog, 0.0))
        r2 = _colsum(jnp.where((lane >= 8) & (lane < 12), dba, 0.0))
        r8 = lax.broadcasted_iota(jnp.int32, (8, 128), 0)
        part = jnp.where(r8 == 0, r0, jnp.where(r8 == 1, r1, jnp.where(r8 == 2, r2, 0.0)))

        @pl.when(i == 0)
        def _():
            acc_ref[...] = part

        @pl.when(i > 0)
        def _():
            acc_ref[...] += part

    rev = pl.BlockSpec((ts, 128), lambda i: (nt - 1 - i, 0))
    rev4 = pl.BlockSpec((4, ts, 128), lambda i: (0, nt - 1 - i, 0))
    c8 = pl.BlockSpec((8, 128), lambda i: (0, 0))
    return pl.pallas_call(
        body, grid=(nt,), out_shape=(jax.ShapeDtypeStruct((S, 128), bf16), jax.ShapeDtypeStruct((8, 128), f32)),
        in_specs=[rev, c8, pl.BlockSpec((8, ts), lambda i: (0, nt - 1 - i)), rev4, rev], out_specs=(rev, c8),
        scratch_shapes=[pltpu.VMEM((8, 128), f32)],
        compiler_params=_cp(("arbitrary",)), name=name)(zs, par, dfr, dfc, dsm)


def _split3(x):
    hi = _b(x).astype(f32)
    r = x - hi
    mid = _b(r).astype(f32)
    return hi, mid, _b(r - mid).astype(f32)


FOX_PREP_ROWS = 512
FOX_TILE = 1024
FOX_SKIP_LOG = -50.0


def _fox_prep(zb, sm, name):
    S = zb.shape[0]
    ts = min(S, FOX_PREP_ROWS)

    def body(q_ref, k_ref, f_ref, qa_ref, ka_ref, st_ref):
        lane = lax.broadcasted_iota(jnp.int32, (ts, 128), 1)
        lane8 = lax.broadcasted_iota(jnp.int32, (8, 128), 1)
        f = f_ref[...]
        st = jnp.zeros((8, 128), f32)
        for p in range(4):
            q = q_ref[:, p * 128:(p + 1) * 128].astype(f32) * FOX_SCALE
            k = k_ref[:, p * 128:(p + 1) * 128].astype(f32)
            for h in (0, 1):
                fcol = f[:, 2 * p + h:2 * p + h + 1]
                hi, mid, lo = _split3(fcol)
                own = (lane < 64) if h == 0 else (lane >= 64)
                nq = jnp.sqrt(_rowsum(jnp.where(own, q * q, 0.0)))
                nk = jnp.sqrt(_rowsum(jnp.where(own, k * k, 0.0)))
                stats = (jnp.max(nq, axis=0, keepdims=True), jnp.max(nk, axis=0, keepdims=True),
                         jnp.max(fcol, axis=0, keepdims=True), jnp.min(fcol, axis=0, keepdims=True),
                         jnp.min(-nq * nk, axis=0, keepdims=True))
                for si, val in enumerate(stats):
                    st = jnp.where(lane8 == 8 * si + 2 * p + h, val, st)
                o = 64 if h == 0 else 0
                ones_lo = (lane >= o) & (lane < o + 3)
                ones_hi = (lane >= o + 3) & (lane < o + 6)
                qaug = jnp.where(lane == o, hi, jnp.where(lane == o + 1, mid, jnp.where(lane == o + 2, lo, jnp.where(ones_hi, 1.0, 0.0))))
                kaug = jnp.where(lane == o + 3, -hi, jnp.where(lane == o + 4, -mid, jnp.where(lane == o + 5, -lo, jnp.where(ones_lo, 1.0, 0.0))))
                qa_ref[2 * p + h] = _b(jnp.where(own, q, qaug))
                ka_ref[2 * p + h] = _b(jnp.where(own, k, kaug))
        st_ref[...] = st

    out = jax.ShapeDtypeStruct((8, S, 128), bf16)
    return pl.pallas_call(
        body, grid=(S // ts,), out_shape=(out, out, jax.ShapeDtypeStruct((S // ts * 8, 128), f32)),
        in_specs=[pl.BlockSpec((ts, 512), lambda i: (i, 0)), pl.BlockSpec((ts, 512), lambda i: (i, 1)), pl.BlockSpec((ts, 128), lambda i: (i, 0))],
        out_specs=(pl.BlockSpec((8, ts, 128), lambda i: (0, i, 0)), pl.BlockSpec((8, ts, 128), lambda i: (0, i, 0)),
                   pl.BlockSpec((8, 128), lambda i: (i, 0))),
        compiler_params=_cp(("parallel",)), name=name)(zb, zb, sm)


def _fox_bound_table(st, S, T):
    ts = min(S, FOX_PREP_ROWS)
    g = T // ts
    nt = S // T
    s5 = st.reshape(S // ts, 8, 128)[:, 0, 0:40].reshape(nt, g, 5, 8)
    qn, kn, fmax = s5[:, :, 0].max(axis=1), s5[:, :, 1].max(axis=1), s5[:, :, 2].max(axis=1)
    fmin, lmin = s5[:, :, 3].min(axis=1), s5[:, :, 4].min(axis=1)
    e = qn[:, None] * kn[None, :] + fmax[:, None] - fmin[None, :] - lmin[:, None] + 1.0
    return jnp.transpose(e, (2, 0, 1)).reshape(8, nt * nt)


def _pair_rows(a, T):
    at = jnp.transpose(a)
    r8 = lax.broadcasted_iota(jnp.int32, (8, T), 0)
    return jnp.where(r8 == 0, at[0:1, :], at[64:65, :])


def _fox_fwd(qa, ka, zb, tab, name):
    S = zb.shape[0]
    T = min(S, FOX_TILE)
    nt = S // T

    def body(tab_ref, qa_ref, ka_ref, v_ref, o_ref, lset_ref, m_ref, l_ref, acc_ref):
        p, i = pl.program_id(0), pl.program_id(1)
        m_ref[...] = jnp.full_like(m_ref, NEG)
        l_ref[...] = jnp.zeros_like(l_ref)
        acc_ref[...] = jnp.zeros_like(acc_ref)
        row = lax.broadcasted_iota(jnp.int32, (T, T), 0)
        col = lax.broadcasted_iota(jnp.int32, (T, T), 1)

        def head_tile(h, j, masked):
            off = pl.multiple_of(j * T, T)
            s = _dot_nt(qa_ref[h], ka_ref[h, pl.ds(off, T), :])
            if masked:
                s = jnp.where(row >= col, s, NEG)
            m_old = m_ref[h]
            m_new = jnp.maximum(m_old, jnp.max(s, axis=1, keepdims=True))
            alpha = jnp.exp(m_old - m_new)
            pr = jnp.exp(s - jnp.tile(m_new, (1, T // 128)))
            l_ref[h] = alpha * l_ref[h] + _rowsum(pr)
            acc_ref[h] = alpha * acc_ref[h] + _dot(_b(pr), v_ref[pl.ds(off, T), :])
            m_ref[h] = m_new

        def step(j, c):
            for h in (0, 1):
                @pl.when(tab_ref[2 * p + h, i * nt + j] > FOX_SKIP_LOG)
                def _():
                    head_tile(h, j, False)
            return c

        lax.fori_loop(0, i, step, 0)
        for h in (0, 1):
            head_tile(h, i, True)
        lane2 = lax.broadcasted_iota(jnp.int32, (T, 128), 1)
        o_ref[...] = jnp.where(lane2 < 64, acc_ref[0] / l_ref[0], acc_ref[1] / l_ref[1])
        lse = jnp.where(lane2 < 64, m_ref[0] + jnp.log(l_ref[0]), m_ref[1] + jnp.log(l_ref[1]))
        lset_ref[0] = _pair_rows(lse, T)

    return pl.pallas_call(
        body, grid=(4, S // T),
        out_shape=(jax.ShapeDtypeStruct((S, 512), f32), jax.ShapeDtypeStruct((4, 8, S), f32)),
        in_specs=[pl.BlockSpec(memory_space=pltpu.SMEM), pl.BlockSpec((2, T, 128), lambda p, i: (p, i, 0)),
                  pl.BlockSpec((2, S, 128), lambda p, i: (p, 0, 0), pipeline_mode=pl.Buffered(1)),
                  pl.BlockSpec((S, 128), lambda p, i: (0, 8 + p), pipeline_mode=pl.Buffered(1))],
        out_specs=(pl.BlockSpec((T, 128), lambda p, i: (i, p)), pl.BlockSpec((1, 8, T), lambda p, i: (p, 0, i))),
        scratch_shapes=[pltpu.VMEM((2, T, 128), f32), pltpu.VMEM((2, T, 128), f32), pltpu.VMEM((2, T, 128), f32)],
        compiler_params=_cp(("arbitrary", "arbitrary"), VMEM_LIMIT), name=name)(tab, qa, ka, zb)


def _fox_bwd(qa, ka, zb, dob, lse_t, dl_t, tab, name):
    S = zb.shape[0]
    T = min(S, FOX_TILE)
    nq = S // T

    def body(tab_ref, ka_ref, v_ref, qa_ref, do_ref, lt_ref, dt_ref, dq_ref, dk_ref, dv_ref, dfc_ref, dfr_ref, dqa_ref, dka_ref, dva_ref, fs_ref):
        p, j = pl.program_id(0), pl.program_id(1)
        lane1 = lax.broadcasted_iota(jnp.int32, (1, 128), 1)
        lane2 = lax.broadcasted_iota(jnp.int32, (T, 128), 1)
        hm = (lane1 < 64, lane1 >= 64)
        v = v_ref[...]
        vsm = [jnp.where(hm[h], v, jnp.zeros_like(v)) for h in (0, 1)]
        ksm = [jnp.where(hm[h], ka_ref[h], jnp.zeros_like(v)) for h in (0, 1)]

        @pl.when(j == 0)
        def _():
            dqa_ref[...] = jnp.zeros_like(dqa_ref)
            dfr_ref[...] = jnp.zeros_like(dfr_ref)

        dka_ref[...] = jnp.zeros_like(dka_ref)
        dva_ref[...] = jnp.zeros_like(dva_ref)
        fs_ref[...] = jnp.zeros_like(fs_ref)
        row = lax.broadcasted_iota(jnp.int32, (T, T), 0)
        col = lax.broadcasted_iota(jnp.int32, (T, T), 1)

        def head_tile(h, i, masked):
            off = pl.multiple_of(i * T, T)
            dot_ = do_ref[pl.ds(off, T), :]
            hr = pl.ds(2 * p + h, 1)
            qt = qa_ref[h, pl.ds(off, T), :]
            s_t = _dot_nt(ka_ref[h], qt)
            if masked:
                s_t = jnp.where(col >= row, s_t, NEG)
            p_t = jnp.exp(s_t - lt_ref[hr, pl.ds(off, T)])
            dva_ref[h] += _dot(_b(p_t), dot_)
            dp_t = _dot_nt(vsm[h], dot_)
            ds_t = p_t * (dp_t - dt_ref[hr, pl.ds(off, T)])
            dsb = _b(ds_t)
            dka_ref[h] += _dot(dsb, qt)
            fs_ref[h] += _rowsum(ds_t)
            dfr_ref[0, pl.ds(h, 1), pl.ds(off, T)] += _colsum(ds_t)
            dqa_ref[pl.ds(off, T), :] += _dot_tn(dsb, ksm[h])

        def step(i, c):
            for h in (0, 1):
                @pl.when(tab_ref[2 * p + h, i * nq + j] > FOX_SKIP_LOG)
                def _():
                    head_tile(h, i, False)
            return c

        for h in (0, 1):
            head_tile(h, j, True)
        lax.fori_loop(j + 1, nq, step, 0)
        dk_ref[...] = _b(jnp.where(lane2 < 64, dka_ref[0], dka_ref[1]))
        dv_ref[...] = _b(jnp.where(lane2 < 64, dva_ref[0], dva_ref[1]))
        dfc_ref[0] = jnp.where(lane2 < 64, fs_ref[0], fs_ref[1])
        dq_ref[...] = _b(dqa_ref[pl.ds(pl.multiple_of(j * T, T), T), :] * FOX_SCALE)

    one = pl.Buffered(1)
    res = pl.BlockSpec((8, S), lambda p, j: (0, 0), pipeline_mode=one)
    tk = pl.BlockSpec((T, 128), lambda p, j: (j, p))
    return pl.pallas_call(
        body, grid=(4, nq),
        out_shape=(jax.ShapeDtypeStruct((S, 512), bf16), jax.ShapeDtypeStruct((S, 512), bf16), jax.ShapeDtypeStruct((S, 512), bf16),
                   jax.ShapeDtypeStruct((4, S, 128), f32), jax.ShapeDtypeStruct((4, 8, S), f32)),
        in_specs=[pl.BlockSpec(memory_space=pltpu.SMEM),
                  pl.BlockSpec((2, T, 128), lambda p, j: (p, j, 0)), pl.BlockSpec((T, 128), lambda p, j: (j, 8 + p)),
                  pl.BlockSpec((2, S, 128), lambda p, j: (p, 0, 0), pipeline_mode=one),
                  pl.BlockSpec((S, 128), lambda p, j: (0, p), pipeline_mode=one), res, res],
        out_specs=(tk, tk, tk, pl.BlockSpec((1, T, 128), lambda p, j: (p, j, 0)),
                   pl.BlockSpec((1, 8, S), lambda p, j: (p, 0, 0))),
        scratch_shapes=[pltpu.VMEM((S, 128), f32), pltpu.VMEM((2, T, 128), f32), pltpu.VMEM((2, T, 128), f32), pltpu.VMEM((2, T, 1), f32)],
        compiler_params=_cp(("arbitrary", "arbitrary"), VMEM_LIMIT), name=name)(tab, ka, zb, qa, dob, lse_t, dl_t)


def _head_rows(a):
    return a[:, 0:2, :].reshape(8, a.shape[2])


def _conv_taps(ext, x, w_ref, ts):
    y = x * w_ref[3:4, :]
    shifted = []
    for k in (1, 2, 3):
        xs = pltpu.roll(ext, k, 0)[8:]
        shifted.append(xs)
        y = y + xs * w_ref[3 - k:4 - k, :]
    return y, shifted


def _gdn_prep(zf, cw, name):
    S = zf.shape[0]
    ts = min(S, 512)

    def body(x_ref, w_ref, o_ref, tail_ref):
        i = pl.program_id(0)

        @pl.when(i == 0)
        def _():
            tail_ref[...] = jnp.zeros_like(tail_ref)

        x = x_ref[...]
        ext = jnp.concatenate([tail_ref[...], x], axis=0)
        y, _ = _conv_taps(ext, x, w_ref, ts)
        tail_ref[...] = x[ts - 8:, :]
        a = _silu(y)
        for hb in range(12):
            blk = a[:, hb * 128:(hb + 1) * 128]
            if hb < 8:
                blk = blk * lax.rsqrt(_rowsum(blk * blk) + EPS)
            if hb < 4:
                blk = blk * GDN_SCALE
            o_ref[:, hb * 128:(hb + 1) * 128] = blk

    return pl.pallas_call(
        body, grid=(S // ts,), out_shape=jax.ShapeDtypeStruct((S, 1536), f32),
        in_specs=[pl.BlockSpec((ts, 1536), lambda i: (i, 0)), pl.BlockSpec((8, 1536), lambda i: (0, 0))],
        out_specs=pl.BlockSpec((ts, 1536), lambda i: (i, 0)), scratch_shapes=[pltpu.VMEM((8, 1536), f32)],
        compiler_params=_cp(("arbitrary",), VMEM_LIMIT), name=name)(zf, cw)


def _gdn_prep_bwd(zf, cw, dg, name):
    S = zf.shape[0]
    ts = min(S, 512)
    nt = S // ts

    def body(x_ref, xp_ref, w_ref, dg_ref, dx_ref, dw_ref, head_ref):
        i = pl.program_id(0)

        @pl.when(i == 0)
        def _():
            head_ref[...] = jnp.zeros_like(head_ref)

        x = x_ref[...]
        prev = jnp.where(i == nt - 1, 0.0, xp_ref[...])
        ext = jnp.concatenate([prev, x], axis=0)
        y, shifted = _conv_taps(ext, x, w_ref, ts)
        a = _silu(y)
        das = []
        for hb in range(12):
            blk = a[:, hb * 128:(hb + 1) * 128]
            d = dg_ref[:, hb * 128:(hb + 1) * 128]
            if hb < 4:
                d = d * GDN_SCALE
            if hb < 8:
                r = lax.rsqrt(_rowsum(blk * blk) + EPS)
                n = blk * r
                d = r * (d - n * _rowsum(d * n))
            das.append(d)
        dy = jnp.concatenate(das, axis=1) * _dsilu(y)
        extd = jnp.concatenate([dy, head_ref[...]], axis=0)
        dx = dy * w_ref[3:4, :]
        for k in (1, 2, 3):
            dx = dx + pltpu.roll(extd, ts + 8 - k, 0)[:ts] * w_ref[3 - k:4 - k, :]
        head_ref[...] = dy[0:8, :]
        dx_ref[...] = _b(dx)
        r8 = lax.broadcasted_iota(jnp.int32, (8, 1536), 0)
        part = jnp.where(r8 == 3, _colsum(dy * x), 0.0)
        for k in (1, 2, 3):
            part = jnp.where(r8 == 3 - k, _colsum(dy * shifted[k - 1]), part)

        @pl.when(i == 0)
        def _():
            dw_ref[...] = part

        @pl.when(i > 0)
        def _():
            dw_ref[...] += part

    rev = pl.BlockSpec((ts, 1536), lambda i: (nt - 1 - i, 0))
    prev8 = pl.BlockSpec((8, 1536), lambda i: (jnp.maximum((nt - 1 - i) * (ts // 8) - 1, 0), 0))
    w8 = pl.BlockSpec((8, 1536), lambda i: (0, 0))
    return pl.pallas_call(
        body, grid=(nt,), out_shape=(jax.ShapeDtypeStruct((S, 1536), bf16), jax.ShapeDtypeStruct((8, 1536), f32)),
        in_specs=[rev, prev8, w8, rev], out_specs=(rev, w8), scratch_shapes=[pltpu.VMEM((8, 1536), f32)],
        compiler_params=_cp(("arbitrary",), VMEM_LIMIT), name=name)(zf, zf, cw, dg)


def _tri_inv(a, row, col):
    same = (row >> 4) == (col >> 4)
    dm = jnp.where(same, a, 0.0)
    lo = a - dm
    eye = jnp.where(row == col, 1.0, 0.0)
    d2 = _hi_b(dm, dm)
    d4 = _hi_b(d2, d2)
    d8 = _hi_b(d4, d4)
    x0 = _hi_b(_hi_b(eye - dm, eye + d2), _hi_b(eye + d4, eye + d8))
    n = _hi_b(x0, lo)
    n2 = _hi_b(n, n)
    return _hi_b(_hi_b(eye - n, eye + n2), x0)


def _bd(a, b):
    return lax.dot_general(a, b, (((2,), (1,)), ((0,), (0,))), preferred_element_type=f32)


def _bd_nt(a, b):
    return lax.dot_general(a, b, (((2,), (2,)), ((0,), (0,))), preferred_element_type=f32)


def _bd_tn(a, b):
    return lax.dot_general(a, b, (((1,), (1,)), ((0,), (0,))), preferred_element_type=f32)


def _hi_b_nt(a, b):
    return _mm3(a, b, (((2,), (2,)), ((0,), (0,))))


def _hi_b_tn(a, b):
    return _mm3(a, b, (((1,), (1,)), ((0,), (0,))))


def _gdn_local(x_ref, sm_ref, gt_ref, row, col, cps=1):
    idx = [(c, h) for c in range(cps) for h in range(4)]

    def rows(c):
        return slice(c * CH, (c + 1) * CH)

    q = jnp.stack([x_ref[rows(c), h * 128:(h + 1) * 128] for c, h in idx])
    k = jnp.stack([x_ref[rows(c), 512 + h * 128:512 + (h + 1) * 128] for c, h in idx])
    v = jnp.stack([x_ref[rows(c), 1024 + h * 128:1024 + (h + 1) * 128] for c, h in idx])
    gc = jnp.stack([sm_ref[rows(c), 8 + h:9 + h] for c, h in idx])
    beta = jnp.stack([sm_ref[rows(c), 12 + h:13 + h] for c, h in idx])
    gr = jnp.stack([gt_ref[h, c] for c, h in idx])
    eg = jnp.exp(gc)
    gl = gc[:, CH - 1:CH, :]
    dec = jnp.exp(gl - gc)
    gm = gc - gr
    gam_i = jnp.exp(jnp.where(row >= col, gm, -jnp.inf))
    gam_s = jnp.where(row > col, gam_i, 0.0)
    kb = k * beta
    return dict(q=q, k=k, v=v, beta=beta, eg=eg, egl=jnp.exp(gl), dec=dec, gam_i=gam_i, gam_s=gam_s,
                kb=kb, vb=v * beta, kbg=kb * eg, qdec=q * eg, kdec=k * dec,
                a=_bd_nt(_b(kb), _b(k)) * gam_s, aqk=_bd_nt(_b(q), _b(k)) * gam_i)


GDN_FWD_CHUNKS = 4


def _gdn_fwd(gqkv, sm, gt4, name, cps=GDN_FWD_CHUNKS):
    S = gqkv.shape[0]
    N = S // CH
    cps = min(cps, N)
    R = cps * CH

    def body(x_ref, sm_ref, gt_ref, o_ref, t_ref, st_ref, s_ref):
        n = pl.program_id(0)

        @pl.when(n == 0)
        def _():
            s_ref[...] = jnp.zeros_like(s_ref)

        row = lax.broadcasted_iota(jnp.int32, (CH, CH), 0)
        col = lax.broadcasted_iota(jnp.int32, (CH, CH), 1)
        c = _gdn_local(x_ref, sm_ref, gt_ref, row, col, cps)
        t = _tri_inv(c["a"], row, col)
        uw = _hi_b(t, jnp.concatenate([c["vb"], c["kbg"]], axis=2))
        u, w = uw[:, :, :128], uw[:, :, 128:]
        for ci in range(cps):
            sl = slice(4 * ci, 4 * ci + 4)
            rs = slice(ci * CH, (ci + 1) * CH)
            st = s_ref[...]
            st_ref[ci] = st
            sb = _b(st)
            vnew = u[sl] - _bd(_b(w[sl]), sb)
            o = _bd(_b(c["qdec"][sl]), sb) + _bd(_b(c["aqk"][sl]), _b(vnew))
            for h in range(4):
                o_ref[rs, h * 128:(h + 1) * 128] = o[h]
                t_ref[h, rs, :] = t[4 * ci + h]
            s_ref[...] = st * c["egl"][sl] + _bd_tn(_b(c["kdec"][sl]), _b(vnew))

    return pl.pallas_call(
        body, grid=(N // cps,),
        out_shape=(jax.ShapeDtypeStruct((S, 512), f32), jax.ShapeDtypeStruct((4, S, CH), f32), jax.ShapeDtypeStruct((N, 4, 128, 128), f32)),
        in_specs=[pl.BlockSpec((R, 1536), lambda n: (n, 0)), pl.BlockSpec((R, 128), lambda n: (n, 0)),
                  pl.BlockSpec((4, cps, 1, CH), lambda n: (0, n, 0, 0))],
        out_specs=(pl.BlockSpec((R, 512), lambda n: (n, 0)), pl.BlockSpec((4, R, CH), lambda n: (0, n, 0)),
                   pl.BlockSpec((cps, 4, 128, 128), lambda n: (n, 0, 0, 0))),
        scratch_shapes=[pltpu.VMEM((4, 128, 128), f32)], compiler_params=_cp(("arbitrary",)), name=name)(gqkv, sm, gt4)


GDN_BWD_CHUNKS = 4


def _gdn_bwd(gqkv, sm, gt4, tinv, states, do, name, cps=GDN_BWD_CHUNKS):
    S = gqkv.shape[0]
    N = S // CH
    cps = min(cps, N)
    R = cps * CH

    def body(x_ref, sm_ref, gt_ref, t_ref, st_ref, do_ref, dx_ref, dsm_ref, ds_ref):
        n = pl.program_id(0)

        @pl.when(n == 0)
        def _():
            ds_ref[...] = jnp.zeros_like(ds_ref)

        row = lax.broadcasted_iota(jnp.int32, (CH, CH), 0)
        col = lax.broadcasted_iota(jnp.int32, (CH, CH), 1)
        row1 = lax.broadcasted_iota(jnp.int32, (CH, 1), 0)
        lane = lax.broadcasted_iota(jnp.int32, (CH, 128), 1)
        ones = jnp.ones((4 * cps, CH, 128), f32)
        idx = [(ci, h) for ci in range(cps) for h in range(4)]
        c = _gdn_local(x_ref, sm_ref, gt_ref, row, col, cps)
        q, k, v, beta, eg = c["q"], c["k"], c["v"], c["beta"], c["eg"]
        t = jnp.stack([t_ref[h, ci * CH:(ci + 1) * CH, :] for ci, h in idx])
        uw = _hi_b(t, jnp.concatenate([c["vb"], c["kbg"]], axis=2))
        u, w = uw[:, :, :128], uw[:, :, 128:]
        st = st_ref[...].reshape(4 * cps, 128, 128)
        sb = _b(st)
        vnew = u - _bd(_b(w), sb)
        dob = _b(jnp.stack([do_ref[ci * CH:(ci + 1) * CH, h * 128:(h + 1) * 128] for ci, h in idx]))
        vnb = _b(vnew)
        dqdec = _bd_nt(dob, sb)
        daqk = jnp.where(row >= col, _bd_nt(dob, vnb), 0.0)
        qd_do = _bd_tn(_b(c["qdec"]), dob)
        aqk_do = _bd_tn(_b(c["aqk"]), dob)
        kdecb, wb = _b(c["kdec"]), _b(w)
        dvnew_l, dkdec_l, dgl_l = [None] * cps, [None] * cps, [None] * cps
        for ci in reversed(range(cps)):
            sl = slice(4 * ci, 4 * ci + 4)
            dsp = ds_ref[...]
            dspb = _b(dsp)
            dvn = _bd(kdecb[sl], dspb) + aqk_do[sl]
            dvnew_l[ci] = dvn
            dkdec_l[ci] = _bd_nt(vnb[sl], dspb)
            dgl_l[ci] = c["egl"][sl] * jnp.sum(dsp * st[sl], axis=(1, 2), keepdims=True)
            ds_ref[...] = dsp * c["egl"][sl] + qd_do[sl] - _bd_tn(wb[sl], _b(dvn))
        dvnew = jnp.concatenate(dvnew_l, axis=0)
        dkdec = jnp.concatenate(dkdec_l, axis=0)
        dgl = jnp.concatenate(dgl_l, axis=0)
        dw = -_bd_nt(_b(dvnew), sb)
        duw = _hi_b_tn(t, jnp.concatenate([dvnew, dw], axis=2))
        dvb, dkbg = duw[:, :, :128], duw[:, :, 128:]
        da = -jnp.where(row > col, _hi_b_nt(duw, uw), 0.0)
        dp = da * c["gam_s"]
        dqk = daqk * c["gam_i"]
        m = da * c["a"] + daqk * c["aqk"]
        csum = _hi_b_tn(m, ones)[:, :, 0:1]
        kk = dkdec * c["kdec"]

        def lsum(a):
            return jnp.sum(a, axis=2, keepdims=True)

        dgv = lsum(m) - csum + lsum(dqdec * c["qdec"]) - lsum(kk) + lsum(dkbg * c["kbg"])
        dgv = dgv + jnp.where(row1 == CH - 1, dgl + jnp.sum(kk, axis=(1, 2), keepdims=True), 0.0)
        dpb, dqkb = _b(dp), _b(dqk)
        dkb = _bd(dpb, _b(k)) + dkbg * eg
        dk = _bd_tn(dpb, _b(c["kb"])) + _bd_tn(dqkb, _b(q)) + dkdec * c["dec"] + dkb * beta
        dq = _bd(dqkb, _b(k)) + dqdec * eg
        dbeta = lsum(dkb * k) + lsum(dvb * v)
        dv = dvb * beta
        for ci in range(cps):
            rs = slice(ci * CH, (ci + 1) * CH)
            dsm = jnp.zeros((CH, 128), f32)
            for h in range(4):
                b = 4 * ci + h
                dx_ref[rs, h * 128:(h + 1) * 128] = dq[b]
                dx_ref[rs, 512 + h * 128:512 + (h + 1) * 128] = dk[b]
                dx_ref[rs, 1024 + h * 128:1024 + (h + 1) * 128] = dv[b]
                dsm = jnp.where(lane == 8 + h, dgv[b], jnp.where(lane == 12 + h, dbeta[b], dsm))
            dsm_ref[rs, :] = dsm

    G = N // cps
    return pl.pallas_call(
        body, grid=(G,), out_shape=(jax.ShapeDtypeStruct((S, 1536), f32), jax.ShapeDtypeStruct((S, 128), f32)),
        in_specs=[pl.BlockSpec((R, 1536), lambda n: (G - 1 - n, 0)), pl.BlockSpec((R, 128), lambda n: (G - 1 - n, 0)),
                  pl.BlockSpec((4, cps, 1, CH), lambda n: (0, G - 1 - n, 0, 0)), pl.BlockSpec((4, R, CH), lambda n: (0, G - 1 - n, 0)),
                  pl.BlockSpec((cps, 4, 128, 128), lambda n: (G - 1 - n, 0, 0, 0)), pl.BlockSpec((R, 512), lambda n: (G - 1 - n, 0))],
        out_specs=(pl.BlockSpec((R, 1536), lambda n: (G - 1 - n, 0)), pl.BlockSpec((R, 128), lambda n: (G - 1 - n, 0))),
        scratch_shapes=[pltpu.VMEM((4, 128, 128), f32)], compiler_params=_cp(("arbitrary",), VMEM_LIMIT), name=name)(gqkv, sm, gt4, tinv, states, do)


def _mem_attn(q, kv_ref, h):
    s = _dot_nt(q, kv_ref[:, h * 128:(h + 1) * 128]) * MEM_SCALE
    e = jnp.exp(s - jnp.max(s, axis=1, keepdims=True))
    return e / _rowsum(e)


def _gdn_out_norm(ob):
    r = lax.rsqrt(jnp.mean(ob * ob, axis=-1, keepdims=True) + EPS)
    return ob * r, r


def _merge_fwd(x, oa, ob, zb, zf, kv, b_merge, gdn_g, w_branch, w_out, name):
    S = x.shape[0]
    ts = min(S, 256)

    def body(x_ref, oa_ref, ob_ref, mq_ref, az_ref, bz_ref, mz_ref, gt_ref, kv_ref, bm_ref, gg_ref, wb_ref, wo_ref,
             xo_ref, y_ref, mg_ref):
        y_ref[:, 0:512] = _b(oa_ref[...] * _silu(az_ref[...]))
        for h in range(4):
            sl = slice(h * 128, (h + 1) * 128)
            nb, _ = _gdn_out_norm(ob_ref[:, sl])
            y_ref[:, 512 + h * 128:512 + (h + 1) * 128] = _b(nb * gg_ref[...] * _silu(bz_ref[:, sl]))
            pm = _mem_attn(mq_ref[:, sl], kv_ref, h)
            om = _dot(_b(pm), kv_ref[:, 512 + h * 128:512 + (h + 1) * 128])
            y_ref[:, 1024 + h * 128:1024 + (h + 1) * 128] = _b(om * _silu(mz_ref[:, sl]))
        merged = jnp.zeros((ts, D), f32)
        for n in range(3):
            gate = _sig(gt_ref[:, n * D:(n + 1) * D] + bm_ref[:, n * D:(n + 1) * D])
            merged = merged + gate * _dot(y_ref[:, n * 512:(n + 1) * 512], wb_ref[n])
        mb = _b(merged)
        mg_ref[...] = mb
        xo_ref[...] = x_ref[...] + _dot(mb, wo_ref[...])

    def col(w, c):
        return pl.BlockSpec((ts, w), lambda i: (i, c))

    def full(shape):
        return pl.BlockSpec(shape, lambda i: tuple(0 for _ in shape))

    return pl.pallas_call(
        body, grid=(S // ts,),
        out_shape=(jax.ShapeDtypeStruct((S, D), f32), jax.ShapeDtypeStruct((S, 1536), bf16), jax.ShapeDtypeStruct((S, D), bf16)),
        in_specs=[col(D, 0), col(512, 0), col(512, 0), col(512, 3), col(512, 3), col(512, 4), col(512, 5), col(3072, 1),
                  full((256, D)), full((1, 3072)), full((1, 128)), full((3, 512, D)), full((D, D))],
        out_specs=(col(D, 0), col(1536, 0), col(D, 0)),
        compiler_params=_cp(("parallel",), VMEM_LIMIT), name=name)(x, oa, ob, zb, zf, zf, zf, zf, kv, b_merge, gdn_g, w_branch, w_out)


def _merge_bwd(dout, ycat, oa, ob, zb, zf, kv, b_merge, gdn_g, w_branch, w_branch_t, w_out_t, name):
    S = dout.shape[0]
    ts = min(S, 256)

    def body(do_ref, y_ref, oa_ref, ob_ref, mq_ref, az_ref, bz_ref, mz_ref, gt_ref, kv_ref, bm_ref, gg_ref, wb_ref, wbt_ref, wot_ref,
             dpj_ref, dz_ref, dmq_ref, dlt_ref, doab_ref, dob_ref, dkv_ref, dbm_ref, dgg_ref):
        i = pl.program_id(0)
        dmerged = _dot(_b(do_ref[...]), wot_ref[...])
        dys = []
        dbm_parts = []
        for n in range(3):
            cs = slice(n * D, (n + 1) * D)
            gate = _sig(gt_ref[:, cs] + bm_ref[:, cs])
            proj = _dot(y_ref[:, n * 512:(n + 1) * 512], wb_ref[n])
            dlogit = dmerged * proj * gate * (1.0 - gate)
            dz_ref[:, 1536 + n * D:1536 + (n + 1) * D] = _b(dlogit)
            dbm_parts.append(_colsum(dlogit))
            dproj = _b(dmerged * gate)
            dpj_ref[:, cs] = dproj
            dys.append(_dot(dproj, wbt_ref[n]))
        dbm = jnp.broadcast_to(jnp.concatenate(dbm_parts, axis=1), (8, 3072))
        az = az_ref[...]
        oa = oa_ref[...]
        doa = dys[0] * _silu(az)
        doab_ref[...] = _b(doa)
        prod = doa * oa
        lane = lax.broadcasted_iota(jnp.int32, (ts, 128), 1)
        dl = jnp.zeros((ts, 128), f32)
        for p in range(4):
            blk = prod[:, p * 128:(p + 1) * 128]
            dl = jnp.where(lane == 2 * p, _rowsum(jnp.where(lane < 64, blk, 0.0)),
                           jnp.where(lane == 2 * p + 1, _rowsum(jnp.where(lane >= 64, blk, 0.0)), dl))
        dlt_ref[...] = jnp.transpose(dl)[0:8, :]
        dz_ref[:, 0:512] = _b(dys[0] * oa * _dsilu(az))
        gg = gg_ref[...]
        dgg = jnp.zeros((1, 128), f32)
        dkv_parts_k, dkv_parts_v = [], []
        for h in range(4):
            sl = slice(h * 128, (h + 1) * 128)
            bz = bz_ref[:, sl]
            dyb = dys[1][:, sl]
            nb, r = _gdn_out_norm(ob_ref[:, sl])
            dz_ref[:, 512 + h * 128:512 + (h + 1) * 128] = _b(dyb * nb * gg * _dsilu(bz))
            dng = dyb * _silu(bz)
            dgg = dgg + _colsum(dng * nb)
            dnb = dng * gg
            dob_ref[:, sl] = r * (dnb - nb * jnp.mean(dnb * nb, axis=-1, keepdims=True))
            mz = mz_ref[:, sl]
            dym = dys[2][:, sl]
            q = mq_ref[:, sl]
            kh = kv_ref[:, sl]
            vh = kv_ref[:, 512 + h * 128:512 + (h + 1) * 128]
            pm = _mem_attn(q, kv_ref, h)
            pmb = _b(pm)
            om = _dot(pmb, vh)
            dz_ref[:, 1024 + h * 128:1024 + (h + 1) * 128] = _b(dym * om * _dsilu(mz))
            dom = _b(dym * _silu(mz))
            dkv_parts_v.append(_dot_tn(pmb, dom))
            dpm = _dot_nt(dom, vh)
            dsm = _b(pm * (dpm - _rowsum(dpm * pm)) * MEM_SCALE)
            dmq_ref[:, sl] = _b(_dot(dsm, kh))
            dkv_parts_k.append(_dot_tn(dsm, q))
        dkv = jnp.concatenate(dkv_parts_k + dkv_parts_v, axis=1)
        dggb = jnp.broadcast_to(dgg, (8, 128))

        @pl.when(i == 0)
        def _():
            dkv_ref[...] = dkv
            dbm_ref[...] = dbm
            dgg_ref[...] = dggb

        @pl.when(i > 0)
        def _():
            dkv_ref[...] += dkv
            dbm_ref[...] += dbm
            dgg_ref[...] += dggb

    def col(w, c):
        return pl.BlockSpec((ts, w), lambda i: (i, c))

    def full(shape):
        return pl.BlockSpec(shape, lambda i: tuple(0 for _ in shape))

    return pl.pallas_call(
        body, grid=(S // ts,),
        out_shape=(jax.ShapeDtypeStruct((S, 3072), bf16), jax.ShapeDtypeStruct((S, N_ALL), bf16), jax.ShapeDtypeStruct((S, 512), bf16),
                   jax.ShapeDtypeStruct((8, S), f32), jax.ShapeDtypeStruct((S, 512), bf16), jax.ShapeDtypeStruct((S, 512), f32),
                   jax.ShapeDtypeStruct((256, D), f32), jax.ShapeDtypeStruct((8, 3072), f32), jax.ShapeDtypeStruct((8, 128), f32)),
        in_specs=[col(D, 0), col(1536, 0), col(512, 0), col(512, 0), col(512, 3), col(512, 3), col(512, 4), col(512, 5), col(3072, 1),
                  full((256, D)), full((1, 3072)), full((1, 128)), full((3, 512, D)), full((3, D, 512)), full((D, D))],
        out_specs=(col(3072, 0), col(4608, 0), col(512, 0), pl.BlockSpec((8, ts), lambda i: (0, i)), col(512, 0), col(512, 0),
                   full((256, D)), full((8, 3072)), full((8, 128))),
        compiler_params=_cp(("arbitrary",), VMEM_LIMIT), name=name)(
            dout, ycat, oa, ob, zb, zf, zf, zf, zf, kv, b_merge, gdn_g, w_branch, w_branch_t, w_out_t)


def _mesh_pos():
    return lax.axis_index("x"), lax.axis_index("y"), lax.axis_index("c")


def _all_gather(xs, name):
    n = len(xs)

    def body(*refs):
        x_refs, out_refs = refs[:n], refs[n:2 * n]
        send_sems, recv_sems, local_sems = refs[2 * n:]
        mx, my, mc = _mesh_pos()
        me, sibling = (mx, my, mc), (mx, my, 1 - mc)
        chips = [(1 - mx, my), (mx, 1 - my), (1 - mx, 1 - my)]

        def copy(a, k, block, to, src=None):
            px, py, pc = block
            slot = out_refs[a].at[4 * px + 2 * py + pc]
            return pltpu.make_async_remote_copy(
                src_ref=slot if src is None else src, dst_ref=slot,
                send_sem=send_sems.at[7 * a + k], recv_sem=recv_sems.at[7 * a + k], device_id=to, device_id_type=pl.DeviceIdType.MESH)

        mine = [pltpu.make_async_copy(x_refs[a], out_refs[a].at[4 * mx + 2 * my + mc], local_sems.at[a]) for a in range(n)]
        for cp in mine:
            cp.start()
        first = []
        for a in range(n):
            first.append(copy(a, 0, me, sibling, src=x_refs[a]))
            first += [copy(a, 1 + j, me, (*chip, mc), src=x_refs[a]) for j, chip in enumerate(chips)]
        for cp in first:
            cp.start()
        passed = []
        for j, chip in enumerate(chips):
            for a in range(n):
                copy(a, 1 + j, (*chip, mc), me).wait_recv()
                fwd = copy(a, 4 + j, (*chip, mc), sibling)
                fwd.start()
                passed.append(fwd)
        for a in range(n):
            copy(a, 0, sibling, me).wait_recv()
            for j, chip in enumerate(chips):
                copy(a, 4 + j, (*chip, 1 - mc), me).wait_recv()
        for cp in first + passed:
            cp.wait_send()
        for cp in mine:
            cp.wait()

    anyspec = pl.BlockSpec(memory_space=pl.ANY)
    return pl.pallas_call(
        body, out_shape=tuple(jax.ShapeDtypeStruct((N_DEV,) + x.shape, x.dtype) for x in xs),
        in_specs=[anyspec] * n, out_specs=tuple([anyspec] * n),
        scratch_shapes=[pltpu.SemaphoreType.DMA((7 * n,)), pltpu.SemaphoreType.DMA((7 * n,)), pltpu.SemaphoreType.DMA((n,))],
        name=name)(*xs)


def _exchange(sends, name):
    n = len(sends)

    def body(*refs):
        s_refs, r_refs = refs[:n], refs[n:2 * n]
        send_sems, recv_sems, local_sems = refs[2 * n:]
        mx, my, mc = _mesh_pos()
        me_id = 4 * mx + 2 * my + mc
        mine = [pltpu.make_async_copy(s_refs[a].at[me_id], r_refs[a].at[me_id], local_sems.at[a]) for a in range(n)]
        for cp in mine:
            cp.start()
        copies = []
        for k in range(1, N_DEV):
            px = 1 - mx if k & 4 else mx
            py = 1 - my if k & 2 else my
            pc = 1 - mc if k & 1 else mc
            for a in range(n):
                copies.append(pltpu.make_async_remote_copy(
                    src_ref=s_refs[a].at[4 * px + 2 * py + pc], dst_ref=r_refs[a].at[me_id],
                    send_sem=send_sems.at[7 * a + k - 1], recv_sem=recv_sems.at[7 * a + k - 1],
                    device_id=(px, py, pc), device_id_type=pl.DeviceIdType.MESH))
        for cp in copies:
            cp.start()
        for cp in copies:
            cp.wait()
        for cp in mine:
            cp.wait()

    anyspec = pl.BlockSpec(memory_space=pl.ANY)
    return pl.pallas_call(
        body, out_shape=tuple(jax.ShapeDtypeStruct(s.shape, s.dtype) for s in sends),
        in_specs=[anyspec] * n, out_specs=tuple([anyspec] * n),
        scratch_shapes=[pltpu.SemaphoreType.DMA((7 * n,)), pltpu.SemaphoreType.DMA((7 * n,)), pltpu.SemaphoreType.DMA((n,))],
        name=name)(*sends)


ADAMW_BLOCK_BYTES = 4 * 1024 * 1024


def _adamw(parts, w, m, v, name):
    _, R, C = parts.shape
    tr = R
    for t in (1024, 512, 256, 128, 64, 32, 16, 8):
        if R % t == 0 and N_DEV * t * C * 4 <= ADAMW_BLOCK_BYTES:
            tr = t
            break

    def body(p_ref, w_ref, m_ref, v_ref, g_ref, d_ref, nm_ref, nv_ref):
        g = p_ref[0].astype(f32)
        for j in range(1, N_DEV):
            g = g + p_ref[j].astype(f32)
        mn = ADAM_B1 * m_ref[...] + (1.0 - ADAM_B1) * g
        vn = ADAM_B2 * v_ref[...] + (1.0 - ADAM_B2) * jnp.square(g)
        m_hat = mn / (1.0 - ADAM_B1 ** ADAM_STEP)
        v_hat = vn / (1.0 - ADAM_B2 ** ADAM_STEP)
        g_ref[...] = g
        d_ref[...] = -ADAM_LR * (m_hat / (jnp.sqrt(v_hat) + ADAM_EPS) + ADAM_WD * w_ref[...])
        nm_ref[...] = mn
        nv_ref[...] = vn

    t2 = pl.BlockSpec((tr, C), lambda i: (i, 0))
    out = jax.ShapeDtypeStruct((R, C), f32)
    return pl.pallas_call(
        body, grid=(R // tr,), out_shape=(out, out, out, out),
        in_specs=[pl.BlockSpec((N_DEV, tr, C), lambda i: (0, i, 0)), t2, t2, t2], out_specs=(t2, t2, t2, t2),
        compiler_params=_cp(("parallel",), VMEM_LIMIT), name=name)(parts, w, m, v)


def _as2d(a):
    return a.reshape(-1, a.shape[-1])


def _perm_cols(w, order=_ORDER):
    parts = [w[..., _COLS[n][0]:_COLS[n][1]] for n in order]
    pad = jnp.zeros(w.shape[:-1] + (N_ALL - N_IN,), w.dtype)
    return jnp.concatenate(parts + [pad], axis=-1)


def _unperm_cols(w, order=_ORDER):
    pieces, off = {}, 0
    for n in order:
        width = _COLS[n][1] - _COLS[n][0]
        pieces[n] = w[..., off:off + width]
        off += width
    return jnp.concatenate([pieces[n] for n in sorted(_COLS, key=lambda n: _COLS[n][0])], axis=-1)


_SMALL_ROWS = 16


def _pack_small(t):
    z = jnp.zeros((D,), f32)
    misc = z.at[0:16].set(t["b_fg"].reshape(-1)).at[16:24].set(t["a_log"].reshape(-1)).at[24:32].set(t["dt_bias"].reshape(-1))
    misc = misc.at[128:384].set(t["gdn_norm_g"].reshape(-1))
    if "extra" in t:
        misc = misc.at[512].set(t["extra"])
    rows = [t["norm_g"], t["b_merge"].reshape(6, D), t["mem_norm_g"], t["final_norm_g"][None], misc[None],
            jnp.zeros((_SMALL_ROWS - 12, D), f32)]
    return jnp.concatenate(rows, axis=0)


def _unpack_small(a):
    misc = a[11]
    return dict(norm_g=a[0:2], b_merge=a[2:8].reshape(2, 3072), mem_norm_g=a[8:10], final_norm_g=a[10],
                b_fg=misc[0:16].reshape(2, 8), a_log=misc[16:24].reshape(2, 4), dt_bias=misc[24:32].reshape(2, 4),
                gdn_norm_g=misc[128:384].reshape(2, 128), extra=misc[512])


def _layer_fwd(l, x, mem, p):
    sfx = f"_l{l}"
    h, ht = _norm_fwd(x, p["norm_g"], "norm_fwd" + sfx, with_t=True)
    zb = _mm(h, p["w_b"], bf16, 1024, 1024, 1024, "inproj_b" + sfx)
    zf = _mm(h, p["w_f"], f32, 1024, 1024, 1024, "inproj_f" + sfx)
    zs = _mm(h, p["w_s"], f32, 512, 128, 1024, "inproj_s" + sfx)
    sm = _small_prep(zs, p["par"], "small_prep" + sfx)
    S = x.shape[0]
    gt4 = jnp.transpose(sm[:, 8:12]).reshape(4, S // CH, 1, CH)
    qa, ka, st = _fox_prep(zb, sm, "fox_prep" + sfx)
    tab = _fox_bound_table(st, S, min(S, FOX_TILE))
    oa, lse_t = _fox_fwd(qa, ka, zb, tab, "fox_fwd" + sfx)
    gqkv = _gdn_prep(zf, p["conv_w"], "gdn_prep" + sfx)
    ob, tinv, states = _gdn_fwd(gqkv, sm, gt4, "gdn_fwd" + sfx)
    memn = _norm_fwd(mem, p["mem_norm_g"], "mem_norm" + sfx)
    kv = _mm(memn, p["w_mem_kv"], bf16, 256, 1024, 1024, "mem_kv" + sfx)
    xo, ycat, merged = _merge_fwd(x, oa, ob, zb, zf, kv, p["b_merge"], p["gdn_norm_g"], p["w_branch"], p["w_out"], "merge_fwd" + sfx)
    saved = dict(x=x, ht=ht, zb=zb, zf=zf, zs=zs, sm=sm, qa=qa, ka=ka, tab=tab, gt4=gt4, oa=oa, lse_t=lse_t, gqkv=gqkv, ob=ob, tinv=tinv,
                 states=states, memn=memn, kv=kv, ycat=ycat, merged=merged)
    return xo, saved


def _layer_bwd(l, dout, mem, p, s):
    sfx = f"_l{l}"
    dproj, dzf2, dmq, delta, doab, dob, dkv, dbm, dgg = _merge_bwd(
        dout, s["ycat"], s["oa"], s["ob"], s["zb"], s["zf"], s["kv"], p["b_merge"], p["gdn_norm_g"],
        p["w_branch"], p["w_branch_t"], p["w_out_t"], "merge_bwd" + sfx)
    g = {}
    g["w_out"] = _mm(s["merged"], dout, f32, 512, 1024, 512, "dw_out" + sfx, trans_a=True)
    g["w_branch"] = jnp.stack([
        _mm(s["ycat"], dproj, f32, 512, 1024, 512, f"dw_branch{n}" + sfx, trans_a=True, a_cols=(n * 512, 512), b_cols=(n * D, D))
        for n in range(3)])
    g["b_merge"] = dbm[0]
    g["gdn_norm_g"] = dgg[0]
    g["w_mem_kv"] = _mm(s["memn"], dkv, f32, 512, 1024, 256, "dw_mem_kv" + sfx, trans_a=True)
    dmemn = _mm(dkv, p["w_mem_kv_t"], f32, 256, 1024, 1024, "dmem_n" + sfx)
    g["mem_norm_g"] = _norm_bwd(mem, p["mem_norm_g"], dmemn, None, "mem_norm_bwd" + sfx)[0]
    dgqkv, dsm = _gdn_bwd(s["gqkv"], s["sm"], s["gt4"], s["tinv"], s["states"], dob, "gdn_bwd" + sfx)
    dbqkv, dcw = _gdn_prep_bwd(s["zf"], p["conv_w"], dgqkv, "gdn_prep_bwd" + sfx)
    g["conv_w"] = dcw[0:4]
    dq, dk, dv, dfc, dfr = _fox_bwd(s["qa"], s["ka"], s["zb"], doab, _head_rows(s["lse_t"]), delta, s["tab"], "fox_bwd" + sfx)
    dzs, sacc = _small_bwd(s["zs"], p["par"], _head_rows(dfr), dfc, dsm, "small_bwd" + sfx)
    g["b_fg"], g["a_log"], g["dt_bias"] = sacc[0, 0:8], sacc[1, 8:12], sacc[2, 8:12]
    dz = lax.dynamic_update_slice(dzf2, jnp.concatenate([dq, dk, dv, dmq, dbqkv, dzs], axis=1), (0, DZ_MERGE_COLS))
    dh = _mm(dz, p["w_all_t"], f32, 1024, 1024, 1664, "dh" + sfx)
    g["w_in"] = _mm(s["ht"], dz, f32, 1024, 1664, 1024, "dw_in" + sfx)
    dx, dng = _norm_bwd(s["x"], p["norm_g"], dh, dout, "norm_bwd" + sfx)
    g["norm_g"] = dng[0]
    return dx, g


def kernel(x, mem, norm_g, w_in, b_fg, b_merge, conv_w, a_log, dt_bias, gdn_norm_g, mem_norm_g, w_mem_kv, w_branch, w_out, final_norm_g, loss_target, m_norm_g, m_w_in, m_b_fg, m_b_merge, m_conv_w, m_a_log, m_dt_bias, m_gdn_norm_g, m_mem_norm_g, m_w_mem_kv, m_w_branch, m_w_out, m_final_norm_g, v_norm_g, v_w_in, v_b_fg, v_b_merge, v_conv_w, v_a_log, v_dt_bias, v_gdn_norm_g, v_mem_norm_g, v_w_mem_kv, v_w_branch, v_w_out, v_final_norm_g):
    x0, mem0, tgt = x[0], mem[0], loss_target[0]
    shard_w = dict(w_in=w_in, w_mem_kv=w_mem_kv, w_branch=w_branch, w_out=w_out, conv_w=conv_w)
    shard_m = dict(w_in=m_w_in, w_mem_kv=m_w_mem_kv, w_branch=m_w_branch, w_out=m_w_out, conv_w=m_conv_w)
    shard_v = dict(w_in=v_w_in, w_mem_kv=v_w_mem_kv, w_branch=v_w_branch, w_out=v_w_out, conv_w=v_conv_w)
    small_w = dict(norm_g=norm_g, b_fg=b_fg, b_merge=b_merge, a_log=a_log, dt_bias=dt_bias, gdn_norm_g=gdn_norm_g,
                   mem_norm_g=mem_norm_g, final_norm_g=final_norm_g)
    small_m = dict(norm_g=m_norm_g, b_fg=m_b_fg, b_merge=m_b_merge, a_log=m_a_log, dt_bias=m_dt_bias, gdn_norm_g=m_gdn_norm_g,
                   mem_norm_g=m_mem_norm_g, final_norm_g=m_final_norm_g)
    small_v = dict(norm_g=v_norm_g, b_fg=v_b_fg, b_merge=v_b_merge, a_log=v_a_log, dt_bias=v_dt_bias, gdn_norm_g=v_gdn_norm_g,
                   mem_norm_g=v_mem_norm_g, final_norm_g=v_final_norm_g)

    g_in, g_kv, g_br, g_out, conv_all = _all_gather(
        [_b(_as2d(w_in)), _b(_as2d(w_mem_kv)), _b(_as2d(w_branch)), _b(_as2d(w_out)), _as2d(conv_w)], "gather_weights")
    conv_full = jnp.transpose(conv_all.reshape(N_DEV, DEPTH, 4, 192), (1, 2, 0, 3)).reshape(DEPTH, 4, 1536)
    w_in_full = jnp.transpose(g_in.reshape(N_DEV, DEPTH, D, 1026), (1, 2, 0, 3)).reshape(DEPTH, D, N_IN)
    w_all = _perm_cols(w_in_full)
    w_bwd = _perm_cols(w_in_full, _ORDER_BWD)
    w_kv_full = jnp.transpose(g_kv.reshape(N_DEV, DEPTH, 128, D), (1, 0, 2, 3)).reshape(DEPTH, D, D)
    w_br_full = jnp.transpose(g_br.reshape(N_DEV, DEPTH, 3, 512, 128), (1, 2, 3, 0, 4)).reshape(DEPTH, 3, 512, D)
    w_out_full = jnp.transpose(g_out.reshape(N_DEV, DEPTH, 128, D), (1, 0, 2, 3)).reshape(DEPTH, D, D)

    layers = []
    for l in range(DEPTH):
        layers.append(dict(
            norm_g=norm_g[l][None], mem_norm_g=mem_norm_g[l][None], gdn_norm_g=gdn_norm_g[l][None], b_merge=b_merge[l][None],
            par=_small_pars(b_fg[l], a_log[l], dt_bias[l]),
            conv_w=jnp.pad(conv_full[l], ((0, 4), (0, 0))),
            w_b=w_all[l][:, 0:NB], w_f=w_all[l][:, NB:NB + NF], w_s=w_all[l][:, NB + NF:], w_all_t=jnp.transpose(w_bwd[l]),
            w_mem_kv=w_kv_full[l], w_mem_kv_t=jnp.transpose(w_kv_full[l]),
            w_branch=w_br_full[l], w_branch_t=jnp.transpose(w_br_full[l], (0, 2, 1)),
            w_out=w_out_full[l], w_out_t=jnp.transpose(w_out_full[l])))

    acts, saved = x0, []
    for l in range(DEPTH):
        acts, s = _layer_fwd(l, acts, mem0, layers[l])
        saved.append(s)
    dx, dfg, lsum = _loss_head(acts, final_norm_g[None], tgt, "loss_head")

    grads = [None] * DEPTH
    for l in reversed(range(DEPTH)):
        dx, grads[l] = _layer_bwd(l, dx, mem0, layers[l], saved[l])
    grad_x = dx[None]

    def per_dev(name):
        return jnp.stack([grads[l][name] for l in range(DEPTH)])

    dw_in = _unperm_cols(per_dev("w_in"), _ORDER_BWD)
    send = dict(
        w_in=jnp.transpose(dw_in.reshape(DEPTH, D, N_DEV, 1026), (2, 0, 1, 3)).reshape(N_DEV, DEPTH * D, 1026),
        w_mem_kv=jnp.transpose(per_dev("w_mem_kv").reshape(DEPTH, N_DEV, 128, D), (1, 0, 2, 3)).reshape(N_DEV, DEPTH * 128, D),
        w_branch=jnp.transpose(per_dev("w_branch").reshape(DEPTH, 3, 512, N_DEV, 128), (3, 0, 1, 2, 4)).reshape(N_DEV, DEPTH * 3 * 512, 128),
        w_out=jnp.transpose(per_dev("w_out").reshape(DEPTH, N_DEV, 128, D), (1, 0, 2, 3)).reshape(N_DEV, DEPTH * 128, D),
        conv_w=jnp.transpose(per_dev("conv_w").reshape(DEPTH, 4, N_DEV, 192), (2, 0, 1, 3)).reshape(N_DEV, DEPTH * 4, 192))
    parts = dict(zip(_SHARDED, _exchange([_b(send[n]) for n in _SHARDED], "scatter_grads")))
    big = [{}, {}, {}, {}]
    for n in _SHARDED:
        res = _adamw(parts[n], _as2d(shard_w[n]), _as2d(shard_m[n]), _as2d(shard_v[n]), "adamw_" + n)
        for kind in range(4):
            big[kind][n] = res[kind].reshape(shard_w[n].shape)

    small_g = {k: jnp.stack([grads[l][k] for l in range(DEPTH)]) for k in ("norm_g", "b_fg", "b_merge", "a_log", "dt_bias", "gdn_norm_g", "mem_norm_g")}
    small_g["final_norm_g"] = dfg[0]
    small_g["extra"] = lsum[0, 0]
    parts_s, = _all_gather([_pack_small(small_g)], "gather_small")
    g_sm, d_sm, m_sm, v_sm = _adamw(parts_s, _pack_small(small_w), _pack_small(small_m), _pack_small(small_v), "adamw_replicated")

    sml = [_unpack_small(a) for a in (g_sm, d_sm, m_sm, v_sm)]
    loss = sml[0]["extra"]
    names = ("norm_g", "w_in", "b_fg", "b_merge", "conv_w", "a_log", "dt_bias", "gdn_norm_g", "mem_norm_g", "w_mem_kv", "w_branch", "w_out", "final_norm_g")
    outs = [loss, grad_x]
    for kind in range(4):
        for n in names:
            outs.append(big[kind][n] if n in big[kind] else sml[kind][n])
    return tuple(outs)
```

```python
import functools

import jax
import jax.numpy as jnp
from jax import lax
from jax.experimental import pallas as pl
from jax.experimental.pallas import tpu as pltpu

f32, bf16 = jnp.float32, jnp.bfloat16

D = 1024
EPS = 1e-6
CH = 64
N_DEV = 8
DEPTH = 2
FOX_SCALE = 64 ** -0.5
GDN_SCALE = 128 ** -0.5
MEM_SCALE = 128 ** -0.5
NEG = -1e30
VMEM_LIMIT = 56 * 1024 * 1024

ADAM_LR, ADAM_B1, ADAM_B2, ADAM_EPS, ADAM_WD, ADAM_STEP = 0.001, 0.9, 0.999, 1e-08, 0.01, 10

_COLS = dict(aq=(0, 512), ak=(512, 1024), av=(1024, 1536), af=(1536, 1544), az=(1544, 2056),
             bq=(2056, 2568), bk=(2568, 3080), bv=(3080, 3592), ba=(3592, 3596), bb=(3596, 3600),
             bz=(3600, 4112), mq=(4112, 4624), mz=(4624, 5136), gates=(5136, 8208))
_ORDER = ("aq", "ak", "av", "mq", "bq", "bk", "bv", "az", "bz", "mz", "gates", "af", "ba", "bb")
_ORDER_BWD = ("az", "bz", "mz", "gates", "aq", "ak", "av", "mq", "bq", "bk", "bv", "af", "ba", "bb")
DZ_MERGE_COLS = 4608
N_IN = 8208
NB, NF, NS = 2048, 6144, 128
N_ALL = NB + NF + NS

_SHARDED = ("w_in", "w_mem_kv", "w_branch", "w_out", "conv_w")


def _cp(sem=None, vmem=None):
    kw = {}
    if sem is not None:
        kw["dimension_semantics"] = sem
    if vmem is not None:
        kw["vmem_limit_bytes"] = vmem
    return pltpu.CompilerParams(**kw)


def _dot(a, b):
    return jnp.dot(a, b, preferred_element_type=f32)


def _dot_nt(a, b):
    return lax.dot_general(a, b, (((1,), (1,)), ((), ())), preferred_element_type=f32)


def _dot_tn(a, b):
    return lax.dot_general(a, b, (((0,), (0,)), ((), ())), preferred_element_type=f32)


def _split2(x):
    hi = x.astype(bf16)
    return hi, (x - hi.astype(f32)).astype(bf16)


def _mm3(a, b, dims):
    ah, al = _split2(a)
    bh, bl = _split2(b)
    dg = functools.partial(lax.dot_general, dimension_numbers=dims, preferred_element_type=f32)
    return dg(ah, bh) + (dg(ah, bl) + dg(al, bh))


def _hi(a, b):
    return _mm3(a, b, (((1,), (0,)), ((), ())))


def _hi_nt(a, b):
    return _mm3(a, b, (((1,), (1,)), ((), ())))


def _hi_tn(a, b):
    return _mm3(a, b, (((0,), (0,)), ((), ())))


def _hi_b(a, b):
    return _mm3(a, b, (((2,), (1,)), ((0,), (0,))))


def _b(x):
    return x.astype(bf16)


def _sig(x):
    return jax.nn.sigmoid(x)


def _silu(x):
    return x * _sig(x)


def _dsilu(x):
    s = _sig(x)
    return s * (1.0 + x * (1.0 - s))


def _softplus(x):
    return jnp.maximum(x, 0.0) + jnp.log1p(jnp.exp(-jnp.abs(x)))


def _rowsum(x):
    return jnp.sum(x, axis=1, keepdims=True)


def _colsum(x):
    return jnp.sum(x, axis=0, keepdims=True)


def _norm_fwd(x, g, name, with_t=False):
    M = x.shape[0]
    ts = min(M, 512)

    def body(x_ref, g_ref, h_ref, *t_ref):
        xv = x_ref[...]
        r = lax.rsqrt(jnp.mean(xv * xv, axis=-1, keepdims=True) + EPS)
        h = xv * r * g_ref[...]
        h_ref[...] = _b(h)
        if with_t:
            t_ref[0][...] = _b(jnp.transpose(h))

    tile = pl.BlockSpec((ts, D), lambda i: (i, 0))
    shapes, specs = jax.ShapeDtypeStruct((M, D), bf16), tile
    if with_t:
        shapes, specs = (shapes, jax.ShapeDtypeStruct((D, M), bf16)), (tile, pl.BlockSpec((D, ts), lambda i: (0, i)))
    return pl.pallas_call(
        body, grid=(M // ts,), out_shape=shapes,
        in_specs=[tile, pl.BlockSpec((1, D), lambda i: (0, 0))],
        out_specs=specs, compiler_params=_cp(("parallel",)), name=name)(x, g)


def _norm_bwd(x, g, dh, dres, name):
    M = x.shape[0]
    ts = min(M, 512)
    with_dx = dres is not None

    def body(*refs):
        if with_dx:
            x_ref, g_ref, dh_ref, dres_ref, dx_ref, dg_ref = refs
        else:
            x_ref, g_ref, dh_ref, dg_ref = refs
        i = pl.program_id(0)
        xv = x_ref[...]
        r = lax.rsqrt(jnp.mean(xv * xv, axis=-1, keepdims=True) + EPS)
        xh = xv * r
        dh = dh_ref[...].astype(f32)
        part = jnp.broadcast_to(_colsum(dh * xh), (8, D))

        @pl.when(i == 0)
        def _():
            dg_ref[...] = part

        @pl.when(i > 0)
        def _():
            dg_ref[...] += part

        if with_dx:
            dxh = dh * g_ref[...]
            dx_ref[...] = dres_ref[...] + r * (dxh - xh * jnp.mean(dxh * xh, axis=-1, keepdims=True))

    tile = pl.BlockSpec((ts, D), lambda i: (i, 0))
    gspec = pl.BlockSpec((1, D), lambda i: (0, 0))
    acc = pl.BlockSpec((8, D), lambda i: (0, 0))
    if with_dx:
        return pl.pallas_call(
            body, grid=(M // ts,), out_shape=(jax.ShapeDtypeStruct((M, D), f32), jax.ShapeDtypeStruct((8, D), f32)),
            in_specs=[tile, gspec, tile, tile], out_specs=(tile, acc), compiler_params=_cp(("arbitrary",)), name=name)(x, g, dh, dres)
    return pl.pallas_call(
        body, grid=(M // ts,), out_shape=jax.ShapeDtypeStruct((8, D), f32),
        in_specs=[tile, gspec, tile], out_specs=acc, compiler_params=_cp(("arbitrary",)), name=name)(x, g, dh)


def _loss_head(x, g, tgt, name):
    M = x.shape[0]
    ts = min(M, 512)

    def body(x_ref, g_ref, t_ref, dx_ref, dg_ref, ls_ref):
        i = pl.program_id(0)
        xv = x_ref[...]
        gv = g_ref[...]
        r = lax.rsqrt(jnp.mean(xv * xv, axis=-1, keepdims=True) + EPS)
        xh = xv * r
        e = xh * gv - t_ref[...]
        lpart = 0.5 * jnp.sum(jnp.mean(e * e, axis=-1, keepdims=True), axis=0, keepdims=True)
        dy = e * (1.0 / D)
        dgp = jnp.broadcast_to(_colsum(dy * xh), (8, D))
        lp = jnp.broadcast_to(lpart, (8, 128))

        @pl.when(i == 0)
        def _():
            dg_ref[...] = dgp
            ls_ref[...] = lp

        @pl.when(i > 0)
        def _():
            dg_ref[...] += dgp
            ls_ref[...] += lp

        dxh = dy * gv
        dx_ref[...] = r * (dxh - xh * jnp.mean(dxh * xh, axis=-1, keepdims=True))

    tile = pl.BlockSpec((ts, D), lambda i: (i, 0))
    return pl.pallas_call(
        body, grid=(M // ts,),
        out_shape=(jax.ShapeDtypeStruct((M, D), f32), jax.ShapeDtypeStruct((8, D), f32), jax.ShapeDtypeStruct((8, 128), f32)),
        in_specs=[tile, pl.BlockSpec((1, D), lambda i: (0, 0)), tile],
        out_specs=(tile, pl.BlockSpec((8, D), lambda i: (0, 0)), pl.BlockSpec((8, 128), lambda i: (0, 0))),
        compiler_params=_cp(("arbitrary",)), name=name)(x, g, tgt)


def _mm(a, b, out_dtype, tm, tn, tk, name, trans_a=False, a_cols=None, b_cols=None):
    if trans_a:
        K, M = a.shape
    else:
        M, K = a.shape
    N = b.shape[1]
    a0, b0 = 0, 0
    if a_cols is not None:
        a0, M = a_cols
    if b_cols is not None:
        b0, N = b_cols
    tm, tn, tk = min(tm, M), min(tn, N), min(tk, K)
    nk = K // tk
    a0, b0 = a0 // tm, b0 // tn

    def body(a_ref, b_ref, o_ref, acc_ref):
        k = pl.program_id(2)
        av, bv = _b(a_ref[...]), _b(b_ref[...])
        part = _dot_tn(av, bv) if trans_a else _dot(av, bv)
        if nk == 1:
            o_ref[...] = part.astype(out_dtype)
        else:
            @pl.when(k == 0)
            def _():
                acc_ref[...] = part

            @pl.when(k > 0)
            def _():
                acc_ref[...] += part

            @pl.when(k == nk - 1)
            def _():
                o_ref[...] = acc_ref[...].astype(out_dtype)

    a_spec = pl.BlockSpec((tk, tm), lambda i, j, k: (k, i + a0)) if trans_a else pl.BlockSpec((tm, tk), lambda i, j, k: (i, k))
    return pl.pallas_call(
        body, grid=(M // tm, N // tn, nk), out_shape=jax.ShapeDtypeStruct((M, N), out_dtype),
        in_specs=[a_spec, pl.BlockSpec((tk, tn), lambda i, j, k: (k, j + b0))],
        out_specs=pl.BlockSpec((tm, tn), lambda i, j, k: (i, j)),
        scratch_shapes=[pltpu.VMEM((tm, tn), f32)],
        compiler_params=_cp(("parallel", "parallel", "arbitrary"), VMEM_LIMIT), name=name)(a, b)


def _small_pars(b_fg, a_log, dt_bias):
    par = jnp.zeros((8, 128), f32)
    par = par.at[0, 0:8].set(b_fg).at[1, 8:12].set(a_log).at[2, 8:12].set(dt_bias)
    return par


def _small_prep(zs, par, name):
    S = zs.shape[0]
    ts = min(S, 512)

    def body(z_ref, par_ref, o_ref, carry_ref):
        i = pl.program_id(0)

        @pl.when(i == 0)
        def _():
            carry_ref[...] = jnp.zeros_like(carry_ref)

        z = z_ref[...]
        lane = lax.broadcasted_iota(jnp.int32, (ts, 128), 1)
        row = lax.broadcasted_iota(jnp.int32, (ts, 128), 0)
        za = z + par_ref[0:1, :]
        logf = jnp.minimum(za, 0.0) - jnp.log1p(jnp.exp(-jnp.abs(za)))
        glog = -jnp.exp(par_ref[1:2, :]) * _softplus(z + par_ref[2:3, :])
        x = jnp.where(lane < 8, logf, jnp.where(lane < 12, glog, 0.0))
        pos = jnp.where(lane < 8, row, row & (CH - 1))
        s = 1
        while s < ts:
            x = x + jnp.where(pos >= s, pltpu.roll(x, s, 0), 0.0)
            s *= 2
        tot = x + carry_ref[0:1, :]
        carry_ref[...] = jnp.broadcast_to(jnp.where(lane[0:1] < 8, tot[ts - 1:ts, :], 0.0), (8, 128))
        o_ref[...] = jnp.where(lane < 8, tot, jnp.where(lane < 12, x, jnp.where(lane < 16, _sig(z), 0.0)))

    return pl.pallas_call(
        body, grid=(S // ts,), out_shape=jax.ShapeDtypeStruct((S, 128), f32),
        in_specs=[pl.BlockSpec((ts, 128), lambda i: (i, 0)), pl.BlockSpec((8, 128), lambda i: (0, 0))],
        out_specs=pl.BlockSpec((ts, 128), lambda i: (i, 0)), scratch_shapes=[pltpu.VMEM((8, 128), f32)],
        compiler_params=_cp(("arbitrary",)), name=name)(zs, par)


def _small_bwd(zs, par, dfr, dfc, dsm, name):
    S = zs.shape[0]
    ts = min(S, 512)
    nt = S // ts

    def body(z_ref, par_ref, dfr_ref, dfc_ref, dsm_ref, dz_ref, acc_ref, carry_ref):
        i = pl.program_id(0)

        @pl.when(i == 0)
        def _():
            carry_ref[...] = jnp.zeros_like(carry_ref)

        z = z_ref[...]
        dsm_v = dsm_ref[...]
        lane = lax.broadcasted_iota(jnp.int32, (ts, 128), 1)
        row = lax.broadcasted_iota(jnp.int32, (ts, 128), 0)
        df = jnp.transpose(jnp.concatenate([dfr_ref[...], jnp.zeros((120, ts), f32)], axis=0))
        for p in range(4):
            dpair = dfc_ref[p]
            df = df - jnp.where(lane == 2 * p, dpair[:, 0:1], jnp.where(lane == 2 * p + 1, dpair[:, 64:65], 0.0))
        x = jnp.where(lane < 8, df, jnp.where(lane < 12, dsm_v, 0.0))
        pos = jnp.where(lane < 8, row, row & (CH - 1))
        seg = jnp.where(lane < 8, ts, CH)
        s = 1
        while s < ts:
            x = x + jnp.where(pos + s < seg, pltpu.roll(x, ts - s, 0), 0.0)
            s *= 2
        tot = x + carry_ref[0:1, :]
        carry_ref[...] = jnp.broadcast_to(jnp.where(lane[0:1] < 8, tot[0:1, :], 0.0), (8, 128))
        za = z + par_ref[0:1, :]
        daf = tot * _sig(-za)
        zb = z + par_ref[2:3, :]
        nea = -jnp.exp(par_ref[1:2, :])
        glog = nea * _softplus(zb)
        dba = x * nea * _sig(zb)
        beta = _sig(z)
        dbb = dsm_v * beta * (1.0 - beta)
        dz_ref[...] = _b(jnp.where(lane < 8, daf, jnp.where(lane < 12, dba, jnp.where(lane < 16, dbb, 0.0))))
        r0 = _colsum(jnp.where(lane < 8, daf, 0.0))
        r1 = _colsum(jnp.where((lane >= 8) & (lane < 12), x * glog, 0.0))
        r2 = _colsum(jnp.where((lane >= 8) & (lane < 12), dba, 0.0))
        r8 = lax.broadcasted_iota(jnp.int32, (8, 128), 0)
        part = jnp.where(r8 == 0, r0, jnp.where(r8 == 1, r1, jnp.where(r8 == 2, r2, 0.0)))

        @pl.when(i == 0)
        def _():
            acc_ref[...] = part

        @pl.when(i > 0)
        def _():
            acc_ref[...] += part

    rev = pl.BlockSpec((ts, 128), lambda i: (nt - 1 - i, 0))
    rev4 = pl.BlockSpec((4, ts, 128), lambda i: (0, nt - 1 - i, 0))
    c8 = pl.BlockSpec((8, 128), lambda i: (0, 0))
    return pl.pallas_call(
        body, grid=(nt,), out_shape=(jax.ShapeDtypeStruct((S, 128), bf16), jax.ShapeDtypeStruct((8, 128), f32)),
        in_specs=[rev, c8, pl.BlockSpec((8, ts), lambda i: (0, nt - 1 - i)), rev4, rev], out_specs=(rev, c8),
        scratch_shapes=[pltpu.VMEM((8, 128), f32)],
        compiler_params=_cp(("arbitrary",)), name=name)(zs, par, dfr, dfc, dsm)


def _split3(x):
    hi = _b(x).astype(f32)
    r = x - hi
    mid = _b(r).astype(f32)
    return hi, mid, _b(r - mid).astype(f32)


FOX_PREP_ROWS = 512
FOX_TILE = 512
FOX_SKIP_LOG = -32.0


def _fox_prep(zb, sm, name):
    S = zb.shape[0]
    ts = min(S, FOX_PREP_ROWS)

    def body(q_ref, k_ref, f_ref, qa_ref, ka_ref, st_ref):
        lane = lax.broadcasted_iota(jnp.int32, (ts, 128), 1)
        lane8 = lax.broadcasted_iota(jnp.int32, (8, 128), 1)
        f = f_ref[...]
        st = jnp.zeros((8, 128), f32)
        for p in range(4):
            q = q_ref[:, p * 128:(p + 1) * 128].astype(f32) * FOX_SCALE
            k = k_ref[:, p * 128:(p + 1) * 128].astype(f32)
            for h in (0, 1):
                fcol = f[:, 2 * p + h:2 * p + h + 1]
                hi, mid, lo = _split3(fcol)
                own = (lane < 64) if h == 0 else (lane >= 64)
                nq = jnp.sqrt(_rowsum(jnp.where(own, q * q, 0.0)))
                nk = jnp.sqrt(_rowsum(jnp.where(own, k * k, 0.0)))
                stats = (jnp.max(nq, axis=0, keepdims=True), jnp.max(nk, axis=0, keepdims=True),
                         jnp.max(fcol, axis=0, keepdims=True), jnp.min(fcol, axis=0, keepdims=True),
                         jnp.min(-nq * nk, axis=0, keepdims=True))
                for si, val in enumerate(stats):
                    st = jnp.where(lane8 == 8 * si + 2 * p + h, val, st)
                o = 64 if h == 0 else 0
                ones_lo = (lane >= o) & (lane < o + 3)
                ones_hi = (lane >= o + 3) & (lane < o + 6)
                qaug = jnp.where(lane == o, hi, jnp.where(lane == o + 1, mid, jnp.where(lane == o + 2, lo, jnp.where(ones_hi, 1.0, 0.0))))
                kaug = jnp.where(lane == o + 3, -hi, jnp.where(lane == o + 4, -mid, jnp.where(lane == o + 5, -lo, jnp.where(ones_lo, 1.0, 0.0))))
                qa_ref[2 * p + h] = _b(jnp.where(own, q, qaug))
                ka_ref[2 * p + h] = _b(jnp.where(own, k, kaug))
        st_ref[...] = st

    out = jax.ShapeDtypeStruct((8, S, 128), bf16)
    return pl.pallas_call(
        body, grid=(S // ts,), out_shape=(out, out, jax.ShapeDtypeStruct((S // ts * 8, 128), f32)),
        in_specs=[pl.BlockSpec((ts, 512), lambda i: (i, 0)), pl.BlockSpec((ts, 512), lambda i: (i, 1)), pl.BlockSpec((ts, 128), lambda i: (i, 0))],
        out_specs=(pl.BlockSpec((8, ts, 128), lambda i: (0, i, 0)), pl.BlockSpec((8, ts, 128), lambda i: (0, i, 0)),
                   pl.BlockSpec((8, 128), lambda i: (i, 0))),
        compiler_params=_cp(("parallel",)), name=name)(zb, zb, sm)


def _fox_bound_table(st, S, T):
    ts = min(S, FOX_PREP_ROWS)
    g = T // ts
    nt = S // T
    s5 = st.reshape(S // ts, 8, 128)[:, 0, 0:40].reshape(nt, g, 5, 8)
    qn, kn, fmax = s5[:, :, 0].max(axis=1), s5[:, :, 1].max(axis=1), s5[:, :, 2].max(axis=1)
    fmin, lmin = s5[:, :, 3].min(axis=1), s5[:, :, 4].min(axis=1)
    e = qn[:, None] * kn[None, :] + fmax[:, None] - fmin[None, :] - lmin[:, None] + 1.0
    return jnp.transpose(e, (2, 0, 1)).reshape(8, nt * nt)


def _pair_rows(a, T):
    at = jnp.transpose(a)
    r8 = lax.broadcasted_iota(jnp.int32, (8, T), 0)
    return jnp.where(r8 == 0, at[0:1, :], at[64:65, :])


def _fox_fwd(qa, ka, zb, tab, name):
    S = zb.shape[0]
    T = min(S, FOX_TILE)
    nt = S // T

    def body(tab_ref, qa_ref, ka_ref, v_ref, o_ref, lset_ref, m_ref, l_ref, acc_ref):
        p, i = pl.program_id(0), pl.program_id(1)
        m_ref[...] = jnp.full_like(m_ref, NEG)
        l_ref[...] = jnp.zeros_like(l_ref)
        acc_ref[...] = jnp.zeros_like(acc_ref)
        row = lax.broadcasted_iota(jnp.int32, (T, T), 0)
        col = lax.broadcasted_iota(jnp.int32, (T, T), 1)

        def head_tile(h, j, masked):
            off = pl.multiple_of(j * T, T)
            s = _dot_nt(qa_ref[h], ka_ref[h, pl.ds(off, T), :])
            if masked:
                s = jnp.where(row >= col, s, NEG)
            m_old = m_ref[h]
            m_new = jnp.maximum(m_old, jnp.max(s, axis=1, keepdims=True))
            alpha = jnp.exp(m_old - m_new)
            pr = jnp.exp(s - jnp.tile(m_new, (1, T // 128)))
            l_ref[h] = alpha * l_ref[h] + _rowsum(pr)
            acc_ref[h] = alpha * acc_ref[h] + _dot(_b(pr), v_ref[pl.ds(off, T), :])
            m_ref[h] = m_new

        def step(j, c):
            for h in (0, 1):
                @pl.when(tab_ref[2 * p + h, i * nt + j] > FOX_SKIP_LOG)
                def _():
                    head_tile(h, j, False)
            return c

        lax.fori_loop(0, i, step, 0)
        for h in (0, 1):
            head_tile(h, i, True)
        lane2 = lax.broadcasted_iota(jnp.int32, (T, 128), 1)
        o_ref[...] = jnp.where(lane2 < 64, acc_ref[0] / l_ref[0], acc_ref[1] / l_ref[1])
        lse = jnp.where(lane2 < 64, m_ref[0] + jnp.log(l_ref[0]), m_ref[1] + jnp.log(l_ref[1]))
        lset_ref[0] = _pair_rows(lse, T)

    return pl.pallas_call(
        body, grid=(4, S // T),
        out_shape=(jax.ShapeDtypeStruct((S, 512), f32), jax.ShapeDtypeStruct((4, 8, S), f32)),
        in_specs=[pl.BlockSpec(memory_space=pltpu.SMEM), pl.BlockSpec((2, T, 128), lambda p, i: (p, i, 0)),
                  pl.BlockSpec((2, S, 128), lambda p, i: (p, 0, 0), pipeline_mode=pl.Buffered(1)),
                  pl.BlockSpec((S, 128), lambda p, i: (0, 8 + p), pipeline_mode=pl.Buffered(1))],
        out_specs=(pl.BlockSpec((T, 128), lambda p, i: (i, p)), pl.BlockSpec((1, 8, T), lambda p, i: (p, 0, i))),
        scratch_shapes=[pltpu.VMEM((2, T, 128), f32), pltpu.VMEM((2, T, 128), f32), pltpu.VMEM((2, T, 128), f32)],
        compiler_params=_cp(("arbitrary", "arbitrary"), VMEM_LIMIT), name=name)(tab, qa, ka, zb)


def _fox_bwd(qa, ka, zb, dob, lse_t, dl_t, tab, name):
    S = zb.shape[0]
    T = min(S, FOX_TILE)
    nq = S // T

    def body(tab_ref, ka_ref, v_ref, qa_ref, do_ref, lt_ref, dt_ref, dq_ref, dk_ref, dv_ref, dfc_ref, dfr_ref, dqa_ref, dka_ref, dva_ref, fs_ref):
        p, j = pl.program_id(0), pl.program_id(1)
        lane1 = lax.broadcasted_iota(jnp.int32, (1, 128), 1)
        lane2 = lax.broadcasted_iota(jnp.int32, (T, 128), 1)
        hm = (lane1 < 64, lane1 >= 64)
        v = v_ref[...]
        vsm = [jnp.where(hm[h], v, jnp.zeros_like(v)) for h in (0, 1)]
        ksm = [jnp.where(hm[h], ka_ref[h], jnp.zeros_like(v)) for h in (0, 1)]

        @pl.when(j == 0)
        def _():
            dqa_ref[...] = jnp.zeros_like(dqa_ref)
            dfr_ref[...] = jnp.zeros_like(dfr_ref)

        dka_ref[...] = jnp.zeros_like(dka_ref)
        dva_ref[...] = jnp.zeros_like(dva_ref)
        fs_ref[...] = jnp.zeros_like(fs_ref)
        row = lax.broadcasted_iota(jnp.int32, (T, T), 0)
        col = lax.broadcasted_iota(jnp.int32, (T, T), 1)

        def head_tile(h, i, masked):
            off = pl.multiple_of(i * T, T)
            dot_ = do_ref[pl.ds(off, T), :]
            hr = pl.ds(2 * p + h, 1)
            qt = qa_ref[h, pl.ds(off, T), :]
            s_t = _dot_nt(ka_ref[h], qt)
            if masked:
                s_t = jnp.where(col >= row, s_t, NEG)
            p_t = jnp.exp(s_t - lt_ref[hr, pl.ds(off, T)])
            dva_ref[h] += _dot(_b(p_t), dot_)
            dp_t = _dot_nt(vsm[h], dot_)
            ds_t = p_t * (dp_t - dt_ref[hr, pl.ds(off, T)])
            dsb = _b(ds_t)
            dka_ref[h] += _dot(dsb, qt)
            fs_ref[h] += _rowsum(ds_t)
            dfr_ref[0, pl.ds(h, 1), pl.ds(off, T)] += _colsum(ds_t)
            dqa_ref[pl.ds(off, T), :] += _dot_tn(dsb, ksm[h])

        def step(i, c):
            for h in (0, 1):
                @pl.when(tab_ref[2 * p + h, i * nq + j] > FOX_SKIP_LOG)
                def _():
                    head_tile(h, i, False)
            return c

        for h in (0, 1):
            head_tile(h, j, True)
        lax.fori_loop(j + 1, nq, step, 0)
        dk_ref[...] = _b(jnp.where(lane2 < 64, dka_ref[0], dka_ref[1]))
        dv_ref[...] = _b(jnp.where(lane2 < 64, dva_ref[0], dva_ref[1]))
        dfc_ref[0] = jnp.where(lane2 < 64, fs_ref[0], fs_ref[1])
        dq_ref[...] = _b(dqa_ref[pl.ds(pl.multiple_of(j * T, T), T), :] * FOX_SCALE)

    one = pl.Buffered(1)
    res = pl.BlockSpec((8, S), lambda p, j: (0, 0), pipeline_mode=one)
    tk = pl.BlockSpec((T, 128), lambda p, j: (j, p))
    return pl.pallas_call(
        body, grid=(4, nq),
        out_shape=(jax.ShapeDtypeStruct((S, 512), bf16), jax.ShapeDtypeStruct((S, 512), bf16), jax.ShapeDtypeStruct((S, 512), bf16),
                   jax.ShapeDtypeStruct((4, S, 128), f32), jax.ShapeDtypeStruct((4, 8, S), f32)),
        in_specs=[pl.BlockSpec(memory_space=pltpu.SMEM),
                  pl.BlockSpec((2, T, 128), lambda p, j: (p, j, 0)), pl.BlockSpec((T, 128), lambda p, j: (j, 8 + p)),
                  pl.BlockSpec((2, S, 128), lambda p, j: (p, 0, 0), pipeline_mode=one),
                  pl.BlockSpec((S, 128), lambda p, j: (0, p), pipeline_mode=one), res, res],
        out_specs=(tk, tk, tk, pl.BlockSpec((1, T, 128), lambda p, j: (p, j, 0)),
                   pl.BlockSpec((1, 8, S), lambda p, j: (p, 0, 0))),
        scratch_shapes=[pltpu.VMEM((S, 128), f32), pltpu.VMEM((2, T, 128), f32), pltpu.VMEM((2, T, 128), f32), pltpu.VMEM((2, T, 1), f32)],
        compiler_params=_cp(("arbitrary", "arbitrary"), VMEM_LIMIT), name=name)(tab, ka, zb, qa, dob, lse_t, dl_t)


def _head_rows(a):
    return a[:, 0:2, :].reshape(8, a.shape[2])


def _conv_taps(ext, x, w_ref, ts):
    y = x * w_ref[3:4, :]
    shifted = []
    for k in (1, 2, 3):
        xs = pltpu.roll(ext, k, 0)[8:]
        shifted.append(xs)
        y = y + xs * w_ref[3 - k:4 - k, :]
    return y, shifted


def _gdn_prep(zf, cw, name):
    S = zf.shape[0]
    ts = min(S, 512)

    def body(x_ref, w_ref, o_ref, tail_ref):
        i = pl.program_id(0)

        @pl.when(i == 0)
        def _():
            tail_ref[...] = jnp.zeros_like(tail_ref)

        x = x_ref[...]
        ext = jnp.concatenate([tail_ref[...], x], axis=0)
        y, _ = _conv_taps(ext, x, w_ref, ts)
        tail_ref[...] = x[ts - 8:, :]
        a = _silu(y)
        for hb in range(12):
            blk = a[:, hb * 128:(hb + 1) * 128]
            if hb < 8:
                blk = blk * lax.rsqrt(_rowsum(blk * blk) + EPS)
            if hb < 4:
                blk = blk * GDN_SCALE
            o_ref[:, hb * 128:(hb + 1) * 128] = blk

    return pl.pallas_call(
        body, grid=(S // ts,), out_shape=jax.ShapeDtypeStruct((S, 1536), f32),
        in_specs=[pl.BlockSpec((ts, 1536), lambda i: (i, 0)), pl.BlockSpec((8, 1536), lambda i: (0, 0))],
        out_specs=pl.BlockSpec((ts, 1536), lambda i: (i, 0)), scratch_shapes=[pltpu.VMEM((8, 1536), f32)],
        compiler_params=_cp(("arbitrary",), VMEM_LIMIT), name=name)(zf, cw)


def _gdn_prep_bwd(zf, cw, dg, name):
    S = zf.shape[0]
    ts = min(S, 512)
    nt = S // ts

    def body(x_ref, xp_ref, w_ref, dg_ref, dx_ref, dw_ref, head_ref):
        i = pl.program_id(0)

        @pl.when(i == 0)
        def _():
            head_ref[...] = jnp.zeros_like(head_ref)

        x = x_ref[...]
        prev = jnp.where(i == nt - 1, 0.0, xp_ref[...])
        ext = jnp.concatenate([prev, x], axis=0)
        y, shifted = _conv_taps(ext, x, w_ref, ts)
        a = _silu(y)
        das = []
        for hb in range(12):
            blk = a[:, hb * 128:(hb + 1) * 128]
            d = dg_ref[:, hb * 128:(hb + 1) * 128]
            if hb < 4:
                d = d * GDN_SCALE
            if hb < 8:
                r = lax.rsqrt(_rowsum(blk * blk) + EPS)
                n = blk * r
                d = r * (d - n * _rowsum(d * n))
            das.append(d)
        dy = jnp.concatenate(das, axis=1) * _dsilu(y)
        extd = jnp.concatenate([dy, head_ref[...]], axis=0)
        dx = dy * w_ref[3:4, :]
        for k in (1, 2, 3):
            dx = dx + pltpu.roll(extd, ts + 8 - k, 0)[:ts] * w_ref[3 - k:4 - k, :]
        head_ref[...] = dy[0:8, :]
        dx_ref[...] = _b(dx)
        r8 = lax.broadcasted_iota(jnp.int32, (8, 1536), 0)
        part = jnp.where(r8 == 3, _colsum(dy * x), 0.0)
        for k in (1, 2, 3):
            part = jnp.where(r8 == 3 - k, _colsum(dy * shifted[k - 1]), part)

        @pl.when(i == 0)
        def _():
            dw_ref[...] = part

        @pl.when(i > 0)
        def _():
            dw_ref[...] += part

    rev = pl.BlockSpec((ts, 1536), lambda i: (nt - 1 - i, 0))
    prev8 = pl.BlockSpec((8, 1536), lambda i: (jnp.maximum((nt - 1 - i) * (ts // 8) - 1, 0), 0))
    w8 = pl.BlockSpec((8, 1536), lambda i: (0, 0))
    return pl.pallas_call(
        body, grid=(nt,), out_shape=(jax.ShapeDtypeStruct((S, 1536), bf16), jax.ShapeDtypeStruct((8, 1536), f32)),
        in_specs=[rev, prev8, w8, rev], out_specs=(rev, w8), scratch_shapes=[pltpu.VMEM((8, 1536), f32)],
        compiler_params=_cp(("arbitrary",), VMEM_LIMIT), name=name)(zf, zf, cw, dg)


def _tri_inv(a, row, col):
    same = (row >> 4) == (col >> 4)
    dm = jnp.where(same, a, 0.0)
    lo = a - dm
    eye = jnp.where(row == col, 1.0, 0.0)
    d2 = _hi_b(dm, dm)
    d4 = _hi_b(d2, d2)
    d8 = _hi_b(d4, d4)
    x0 = _hi_b(_hi_b(eye - dm, eye + d2), _hi_b(eye + d4, eye + d8))
    n = _hi_b(x0, lo)
    n2 = _hi_b(n, n)
    return _hi_b(_hi_b(eye - n, eye + n2), x0)


def _bd(a, b):
    return lax.dot_general(a, b, (((2,), (1,)), ((0,), (0,))), preferred_element_type=f32)


def _bd_nt(a, b):
    return lax.dot_general(a, b, (((2,), (2,)), ((0,), (0,))), preferred_element_type=f32)


def _bd_tn(a, b):
    return lax.dot_general(a, b, (((1,), (1,)), ((0,), (0,))), preferred_element_type=f32)


def _hi_b_nt(a, b):
    return _mm3(a, b, (((2,), (2,)), ((0,), (0,))))


def _hi_b_tn(a, b):
    return _mm3(a, b, (((1,), (1,)), ((0,), (0,))))


def _gdn_local(x_ref, sm_ref, gt_ref, row, col, cps=1):
    idx = [(c, h) for c in range(cps) for h in range(4)]

    def rows(c):
        return slice(c * CH, (c + 1) * CH)

    q = jnp.stack([x_ref[rows(c), h * 128:(h + 1) * 128] for c, h in idx])
    k = jnp.stack([x_ref[rows(c), 512 + h * 128:512 + (h + 1) * 128] for c, h in idx])
    v = jnp.stack([x_ref[rows(c), 1024 + h * 128:1024 + (h + 1) * 128] for c, h in idx])
    gc = jnp.stack([sm_ref[rows(c), 8 + h:9 + h] for c, h in idx])
    beta = jnp.stack([sm_ref[rows(c), 12 + h:13 + h] for c, h in idx])
    gr = jnp.stack([gt_ref[h, c] for c, h in idx])
    eg = jnp.exp(gc)
    gl = gc[:, CH - 1:CH, :]
    dec = jnp.exp(gl - gc)
    gm = gc - gr
    gam_i = jnp.exp(jnp.where(row >= col, gm, -jnp.inf))
    gam_s = jnp.where(row > col, gam_i, 0.0)
    kb = k * beta
    return dict(q=q, k=k, v=v, beta=beta, eg=eg, egl=jnp.exp(gl), dec=dec, gam_i=gam_i, gam_s=gam_s,
                kb=kb, vb=v * beta, kbg=kb * eg, qdec=q * eg, kdec=k * dec,
                a=_bd_nt(_b(kb), _b(k)) * gam_s, aqk=_bd_nt(_b(q), _b(k)) * gam_i)


GDN_FWD_CHUNKS = 8


def _gdn_fwd(gqkv, sm, gt4, name, cps=GDN_FWD_CHUNKS):
    S = gqkv.shape[0]
    N = S // CH
    cps = min(cps, N)
    R = cps * CH

    def body(x_ref, sm_ref, gt_ref, o_ref, t_ref, st_ref, s_ref):
        n = pl.program_id(0)

        @pl.when(n == 0)
        def _():
            s_ref[...] = jnp.zeros_like(s_ref)

        row = lax.broadcasted_iota(jnp.int32, (CH, CH), 0)
        col = lax.broadcasted_iota(jnp.int32, (CH, CH), 1)
        c = _gdn_local(x_ref, sm_ref, gt_ref, row, col, cps)
        t = _tri_inv(c["a"], row, col)
        uw = _hi_b(t, jnp.concatenate([c["vb"], c["kbg"]], axis=2))
        u, w = uw[:, :, :128], uw[:, :, 128:]
        for ci in range(cps):
            sl = slice(4 * ci, 4 * ci + 4)
            rs = slice(ci * CH, (ci + 1) * CH)
            st = s_ref[...]
            st_ref[ci] = st
            sb = _b(st)
            vnew = u[sl] - _bd(_b(w[sl]), sb)
            o = _bd(_b(c["qdec"][sl]), sb) + _bd(_b(c["aqk"][sl]), _b(vnew))
            for h in range(4):
                o_ref[rs, h * 128:(h + 1) * 128] = o[h]
                t_ref[h, rs, :] = t[4 * ci + h]
            s_ref[...] = st * c["egl"][sl] + _bd_tn(_b(c["kdec"][sl]), _b(vnew))

    return pl.pallas_call(
        body, grid=(N // cps,),
        out_shape=(jax.ShapeDtypeStruct((S, 512), f32), jax.ShapeDtypeStruct((4, S, CH), f32), jax.ShapeDtypeStruct((N, 4, 128, 128), f32)),
        in_specs=[pl.BlockSpec((R, 1536), lambda n: (n, 0)), pl.BlockSpec((R, 128), lambda n: (n, 0)),
                  pl.BlockSpec((4, cps, 1, CH), lambda n: (0, n, 0, 0))],
        out_specs=(pl.BlockSpec((R, 512), lambda n: (n, 0)), pl.BlockSpec((4, R, CH), lambda n: (0, n, 0)),
                   pl.BlockSpec((cps, 4, 128, 128), lambda n: (n, 0, 0, 0))),
        scratch_shapes=[pltpu.VMEM((4, 128, 128), f32)], compiler_params=_cp(("arbitrary",)), name=name)(gqkv, sm, gt4)


GDN_BWD_CHUNKS = 4


def _gdn_bwd(gqkv, sm, gt4, tinv, states, do, name, cps=GDN_BWD_CHUNKS):
    S = gqkv.shape[0]
    N = S // CH
    cps = min(cps, N)
    R = cps * CH

    def body(x_ref, sm_ref, gt_ref, t_ref, st_ref, do_ref, dx_ref, dsm_ref, ds_ref):
        n = pl.program_id(0)

        @pl.when(n == 0)
        def _():
            ds_ref[...] = jnp.zeros_like(ds_ref)

        row = lax.broadcasted_iota(jnp.int32, (CH, CH), 0)
        col = lax.broadcasted_iota(jnp.int32, (CH, CH), 1)
        row1 = lax.broadcasted_iota(jnp.int32, (CH, 1), 0)
        lane = lax.broadcasted_iota(jnp.int32, (CH, 128), 1)
        ones = jnp.ones((4 * cps, CH, 128), f32)
        idx = [(ci, h) for ci in range(cps) for h in range(4)]
        c = _gdn_local(x_ref, sm_ref, gt_ref, row, col, cps)
        q, k, v, beta, eg = c["q"], c["k"], c["v"], c["beta"], c["eg"]
        t = jnp.stack([t_ref[h, ci * CH:(ci + 1) * CH, :] for ci, h in idx])
        uw = _hi_b(t, jnp.concatenate([c["vb"], c["kbg"]], axis=2))
        u, w = uw[:, :, :128], uw[:, :, 128:]
        st = st_ref[...].reshape(4 * cps, 128, 128)
        sb = _b(st)
        vnew = u - _bd(_b(w), sb)
        dob = _b(jnp.stack([do_ref[ci * CH:(ci + 1) * CH, h * 128:(h + 1) * 128] for ci, h in idx]))
        vnb = _b(vnew)
        dqdec = _bd_nt(dob, sb)
        daqk = jnp.where(row >= col, _bd_nt(dob, vnb), 0.0)
        qd_do = _bd_tn(_b(c["qdec"]), dob)
        aqk_do = _bd_tn(_b(c["aqk"]), dob)
        kdecb, wb = _b(c["kdec"]), _b(w)
        dvnew_l, dkdec_l, dgl_l = [None] * cps, [None] * cps, [None] * cps
        for ci in reversed(range(cps)):
            sl = slice(4 * ci, 4 * ci + 4)
            dsp = ds_ref[...]
            dspb = _b(dsp)
            dvn = _bd(kdecb[sl], dspb) + aqk_do[sl]
            dvnew_l[ci] = dvn
            dkdec_l[ci] = _bd_nt(vnb[sl], dspb)
            dgl_l[ci] = c["egl"][sl] * jnp.sum(dsp * st[sl], axis=(1, 2), keepdims=True)
            ds_ref[...] = dsp * c["egl"][sl] + qd_do[sl] - _bd_tn(wb[sl], _b(dvn))
        dvnew = jnp.concatenate(dvnew_l, axis=0)
        dkdec = jnp.concatenate(dkdec_l, axis=0)
        dgl = jnp.concatenate(dgl_l, axis=0)
        dw = -_bd_nt(_b(dvnew), sb)
        duw = _hi_b_tn(t, jnp.concatenate([dvnew, dw], axis=2))
        dvb, dkbg = duw[:, :, :128], duw[:, :, 128:]
        da = -jnp.where(row > col, _hi_b_nt(duw, uw), 0.0)
        dp = da * c["gam_s"]
        dqk = daqk * c["gam_i"]
        m = da * c["a"] + daqk * c["aqk"]
        csum = _hi_b_tn(m, ones)[:, :, 0:1]
        kk = dkdec * c["kdec"]

        def lsum(a):
            return jnp.sum(a, axis=2, keepdims=True)

        dgv = lsum(m) - csum + lsum(dqdec * c["qdec"]) - lsum(kk) + lsum(dkbg * c["kbg"])
        dgv = dgv + jnp.where(row1 == CH - 1, dgl + jnp.sum(kk, axis=(1, 2), keepdims=True), 0.0)
        dpb, dqkb = _b(dp), _b(dqk)
        dkb = _bd(dpb, _b(k)) + dkbg * eg
        dk = _bd_tn(dpb, _b(c["kb"])) + _bd_tn(dqkb, _b(q)) + dkdec * c["dec"] + dkb * beta
        dq = _bd(dqkb, _b(k)) + dqdec * eg
        dbeta = lsum(dkb * k) + lsum(dvb * v)
        dv = dvb * beta
        for ci in range(cps):
            rs = slice(ci * CH, (ci + 1) * CH)
            dsm = jnp.zeros((CH, 128), f32)
            for h in range(4):
                b = 4 * ci + h
                dx_ref[rs, h * 128:(h + 1) * 128] = dq[b]
                dx_ref[rs, 512 + h * 128:512 + (h + 1) * 128] = dk[b]
                dx_ref[rs, 1024 + h * 128:1024 + (h + 1) * 128] = dv[b]
                dsm = jnp.where(lane == 8 + h, dgv[b], jnp.where(lane == 12 + h, dbeta[b], dsm))
            dsm_ref[rs, :] = dsm

    G = N // cps
    return pl.pallas_call(
        body, grid=(G,), out_shape=(jax.ShapeDtypeStruct((S, 1536), f32), jax.ShapeDtypeStruct((S, 128), f32)),
        in_specs=[pl.BlockSpec((R, 1536), lambda n: (G - 1 - n, 0)), pl.BlockSpec((R, 128), lambda n: (G - 1 - n, 0)),
                  pl.BlockSpec((4, cps, 1, CH), lambda n: (0, G - 1 - n, 0, 0)), pl.BlockSpec((4, R, CH), lambda n: (0, G - 1 - n, 0)),
                  pl.BlockSpec((cps, 4, 128, 128), lambda n: (G - 1 - n, 0, 0, 0)), pl.BlockSpec((R, 512), lambda n: (G - 1 - n, 0))],
        out_specs=(pl.BlockSpec((R, 1536), lambda n: (G - 1 - n, 0)), pl.BlockSpec((R, 128), lambda n: (G - 1 - n, 0))),
        scratch_shapes=[pltpu.VMEM((4, 128, 128), f32)], compiler_params=_cp(("arbitrary",), VMEM_LIMIT), name=name)(gqkv, sm, gt4, tinv, states, do)


MERGE_ROWS = 256
MERGE_FWD_ROWS = 512


def _mem_attn(q, kv_ref, h):
    s = _dot_nt(q, kv_ref[:, h * 128:(h + 1) * 128]) * MEM_SCALE
    e = jnp.exp(s - jnp.max(s, axis=1, keepdims=True))
    return e / _rowsum(e)


def _gdn_out_norm(ob):
    r = lax.rsqrt(jnp.mean(ob * ob, axis=-1, keepdims=True) + EPS)
    return ob * r, r


def _merge_fwd(x, oa, ob, zb, zf, kv, b_merge, gdn_g, w_branch, w_out, name):
    S = x.shape[0]
    ts = min(S, MERGE_FWD_ROWS)

    def body(x_ref, oa_ref, ob_ref, mq_ref, az_ref, bz_ref, mz_ref, gt_ref, kv_ref, bm_ref, gg_ref, wb_ref, wo_ref,
             xo_ref, y_ref, mg_ref):
        y_ref[:, 0:512] = _b(oa_ref[...] * _silu(az_ref[...]))
        for h in range(4):
            sl = slice(h * 128, (h + 1) * 128)
            nb, _ = _gdn_out_norm(ob_ref[:, sl])
            y_ref[:, 512 + h * 128:512 + (h + 1) * 128] = _b(nb * gg_ref[...] * _silu(bz_ref[:, sl]))
            pm = _mem_attn(mq_ref[:, sl], kv_ref, h)
            om = _dot(_b(pm), kv_ref[:, 512 + h * 128:512 + (h + 1) * 128])
            y_ref[:, 1024 + h * 128:1024 + (h + 1) * 128] = _b(om * _silu(mz_ref[:, sl]))
        merged = jnp.zeros((ts, D), f32)
        for n in range(3):
            gate = _sig(gt_ref[:, n * D:(n + 1) * D] + bm_ref[:, n * D:(n + 1) * D])
            merged = merged + gate * _dot(y_ref[:, n * 512:(n + 1) * 512], wb_ref[n])
        mb = _b(merged)
        mg_ref[...] = mb
        xo_ref[...] = x_ref[...] + _dot(mb, wo_ref[...])

    def col(w, c):
        return pl.BlockSpec((ts, w), lambda i: (i, c))

    def full(shape):
        return pl.BlockSpec(shape, lambda i: tuple(0 for _ in shape))

    return pl.pallas_call(
        body, grid=(S // ts,),
        out_shape=(jax.ShapeDtypeStruct((S, D), f32), jax.ShapeDtypeStruct((S, 1536), bf16), jax.ShapeDtypeStruct((S, D), bf16)),
        in_specs=[col(D, 0), col(512, 0), col(512, 0), col(512, 3), col(512, 3), col(512, 4), col(512, 5), col(3072, 1),
                  full((256, D)), full((1, 3072)), full((1, 128)), full((3, 512, D)), full((D, D))],
        out_specs=(col(D, 0), col(1536, 0), col(D, 0)),
        compiler_params=_cp(("parallel",), VMEM_LIMIT), name=name)(x, oa, ob, zb, zf, zf, zf, zf, kv, b_merge, gdn_g, w_branch, w_out)


def _merge_bwd(dout, ycat, oa, ob, zb, zf, kv, b_merge, gdn_g, w_branch, w_branch_t, w_out_t, name):
    S = dout.shape[0]
    ts = min(S, MERGE_ROWS)

    def body(do_ref, y_ref, oa_ref, ob_ref, mq_ref, az_ref, bz_ref, mz_ref, gt_ref, kv_ref, bm_ref, gg_ref, wb_ref, wbt_ref, wot_ref,
             dpj_ref, dz_ref, dmq_ref, dlt_ref, doab_ref, dob_ref, dkv_ref, dbm_ref, dgg_ref):
        i = pl.program_id(0)
        dmerged = _dot(_b(do_ref[...]), wot_ref[...])
        dys = []
        dbm_parts = []
        for n in range(3):
            cs = slice(n * D, (n + 1) * D)
            gate = _sig(gt_ref[:, cs] + bm_ref[:, cs])
            proj = _dot(y_ref[:, n * 512:(n + 1) * 512], wb_ref[n])
            dlogit = dmerged * proj * gate * (1.0 - gate)
            dz_ref[:, 1536 + n * D:1536 + (n + 1) * D] = _b(dlogit)
            dbm_parts.append(_colsum(dlogit))
            dproj = _b(dmerged * gate)
            dpj_ref[:, cs] = dproj
            dys.append(_dot(dproj, wbt_ref[n]))
        dbm = jnp.broadcast_to(jnp.concatenate(dbm_parts, axis=1), (8, 3072))
        az = az_ref[...]
        oa = oa_ref[...]
        doa = dys[0] * _silu(az)
        doab_ref[...] = _b(doa)
        prod = doa * oa
        lane = lax.broadcasted_iota(jnp.int32, (ts, 128), 1)
        dl = jnp.zeros((ts, 128), f32)
        for p in range(4):
            blk = prod[:, p * 128:(p + 1) * 128]
            dl = jnp.where(lane == 2 * p, _rowsum(jnp.where(lane < 64, blk, 0.0)),
                           jnp.where(lane == 2 * p + 1, _rowsum(jnp.where(lane >= 64, blk, 0.0)), dl))
        dlt_ref[...] = jnp.transpose(dl)[0:8, :]
        dz_ref[:, 0:512] = _b(dys[0] * oa * _dsilu(az))
        gg = gg_ref[...]
        dgg = jnp.zeros((1, 128), f32)
        dkv_parts_k, dkv_parts_v = [], []
        for h in range(4):
            sl = slice(h * 128, (h + 1) * 128)
            bz = bz_ref[:, sl]
            dyb = dys[1][:, sl]
            nb, r = _gdn_out_norm(ob_ref[:, sl])
            dz_ref[:, 512 + h * 128:512 + (h + 1) * 128] = _b(dyb * nb * gg * _dsilu(bz))
            dng = dyb * _silu(bz)
            dgg = dgg + _colsum(dng * nb)
            dnb = dng * gg
            dob_ref[:, sl] = r * (dnb - nb * jnp.mean(dnb * nb, axis=-1, keepdims=True))
            mz = mz_ref[:, sl]
            dym = dys[2][:, sl]
            q = mq_ref[:, sl]
            kh = kv_ref[:, sl]
            vh = kv_ref[:, 512 + h * 128:512 + (h + 1) * 128]
            pm = _mem_attn(q, kv_ref, h)
            pmb = _b(pm)
            om = _dot(pmb, vh)
            dz_ref[:, 1024 + h * 128:1024 + (h + 1) * 128] = _b(dym * om * _dsilu(mz))
            dom = _b(dym * _silu(mz))
            dkv_parts_v.append(_dot_tn(pmb, dom))
            dpm = _dot_nt(dom, vh)
            dsm = _b(pm * (dpm - _rowsum(dpm * pm)) * MEM_SCALE)
            dmq_ref[:, sl] = _b(_dot(dsm, kh))
            dkv_parts_k.append(_dot_tn(dsm, q))
        dkv = jnp.concatenate(dkv_parts_k + dkv_parts_v, axis=1)
        dggb = jnp.broadcast_to(dgg, (8, 128))

        @pl.when(i == 0)
        def _():
            dkv_ref[...] = dkv
            dbm_ref[...] = dbm
            dgg_ref[...] = dggb

        @pl.when(i > 0)
        def _():
            dkv_ref[...] += dkv
            dbm_ref[...] += dbm
            dgg_ref[...] += dggb

    def col(w, c):
        return pl.BlockSpec((ts, w), lambda i: (i, c))

    def full(shape):
        return pl.BlockSpec(shape, lambda i: tuple(0 for _ in shape))

    return pl.pallas_call(
        body, grid=(S // ts,),
        out_shape=(jax.ShapeDtypeStruct((S, 3072), bf16), jax.ShapeDtypeStruct((S, N_ALL), bf16), jax.ShapeDtypeStruct((S, 512), bf16),
                   jax.ShapeDtypeStruct((8, S), f32), jax.ShapeDtypeStruct((S, 512), bf16), jax.ShapeDtypeStruct((S, 512), f32),
                   jax.ShapeDtypeStruct((256, D), f32), jax.ShapeDtypeStruct((8, 3072), f32), jax.ShapeDtypeStruct((8, 128), f32)),
        in_specs=[col(D, 0), col(1536, 0), col(512, 0), col(512, 0), col(512, 3), col(512, 3), col(512, 4), col(512, 5), col(3072, 1),
                  full((256, D)), full((1, 3072)), full((1, 128)), full((3, 512, D)), full((3, D, 512)), full((D, D))],
        out_specs=(col(3072, 0), col(4608, 0), col(512, 0), pl.BlockSpec((8, ts), lambda i: (0, i)), col(512, 0), col(512, 0),
                   full((256, D)), full((8, 3072)), full((8, 128))),
        compiler_params=_cp(("arbitrary",), VMEM_LIMIT), name=name)(
            dout, ycat, oa, ob, zb, zf, zf, zf, zf, kv, b_merge, gdn_g, w_branch, w_branch_t, w_out_t)


def _mesh_pos():
    return lax.axis_index("x"), lax.axis_index("y"), lax.axis_index("c")


def _all_gather(xs, name):
    n = len(xs)

    def body(*refs):
        x_refs, out_refs = refs[:n], refs[n:2 * n]
        send_sems, recv_sems, local_sems = refs[2 * n:]
        mx, my, mc = _mesh_pos()
        me, sibling = (mx, my, mc), (mx, my, 1 - mc)
        chips = [(1 - mx, my), (mx, 1 - my), (1 - mx, 1 - my)]

        def copy(a, k, block, to, src=None):
            px, py, pc = block
            slot = out_refs[a].at[4 * px + 2 * py + pc]
            return pltpu.make_async_remote_copy(
                src_ref=slot if src is None else src, dst_ref=slot,
                send_sem=send_sems.at[7 * a + k], recv_sem=recv_sems.at[7 * a + k], device_id=to, device_id_type=pl.DeviceIdType.MESH)

        mine = [pltpu.make_async_copy(x_refs[a], out_refs[a].at[4 * mx + 2 * my + mc], local_sems.at[a]) for a in range(n)]
        for cp in mine:
            cp.start()
        first = []
        for a in range(n):
            first.append(copy(a, 0, me, sibling, src=x_refs[a]))
            first += [copy(a, 1 + j, me, (*chip, mc), src=x_refs[a]) for j, chip in enumerate(chips)]
        for cp in first:
            cp.start()
        passed = []
        for j, chip in enumerate(chips):
            for a in range(n):
                copy(a, 1 + j, (*chip, mc), me).wait_recv()
                fwd = copy(a, 4 + j, (*chip, mc), sibling)
                fwd.start()
                passed.append(fwd)
        for a in range(n):
            copy(a, 0, sibling, me).wait_recv()
            for j, chip in enumerate(chips):
                copy(a, 4 + j, (*chip, 1 - mc), me).wait_recv()
        for cp in first + passed:
            cp.wait_send()
        for cp in mine:
            cp.wait()

    anyspec = pl.BlockSpec(memory_space=pl.ANY)
    return pl.pallas_call(
        body, out_shape=tuple(jax.ShapeDtypeStruct((N_DEV,) + x.shape, x.dtype) for x in xs),
        in_specs=[anyspec] * n, out_specs=tuple([anyspec] * n),
        scratch_shapes=[pltpu.SemaphoreType.DMA((7 * n,)), pltpu.SemaphoreType.DMA((7 * n,)), pltpu.SemaphoreType.DMA((n,))],
        name=name)(*xs)


def _exchange(sends, name):
    n = len(sends)

    def body(*refs):
        s_refs, r_refs = refs[:n], refs[n:2 * n]
        send_sems, recv_sems, local_sems = refs[2 * n:]
        mx, my, mc = _mesh_pos()
        me_id = 4 * mx + 2 * my + mc
        mine = [pltpu.make_async_copy(s_refs[a].at[me_id], r_refs[a].at[me_id], local_sems.at[a]) for a in range(n)]
        for cp in mine:
            cp.start()
        copies = []
        for k in range(1, N_DEV):
            px = 1 - mx if k & 4 else mx
            py = 1 - my if k & 2 else my
            pc = 1 - mc if k & 1 else mc
            for a in range(n):
                copies.append(pltpu.make_async_remote_copy(
                    src_ref=s_refs[a].at[4 * px + 2 * py + pc], dst_ref=r_refs[a].at[me_id],
                    send_sem=send_sems.at[7 * a + k - 1], recv_sem=recv_sems.at[7 * a + k - 1],
                    device_id=(px, py, pc), device_id_type=pl.DeviceIdType.MESH))
        for cp in copies:
            cp.start()
        for cp in copies:
            cp.wait()
        for cp in mine:
            cp.wait()

    anyspec = pl.BlockSpec(memory_space=pl.ANY)
    return pl.pallas_call(
        body, out_shape=tuple(jax.ShapeDtypeStruct(s.shape, s.dtype) for s in sends),
        in_specs=[anyspec] * n, out_specs=tuple([anyspec] * n),
        scratch_shapes=[pltpu.SemaphoreType.DMA((7 * n,)), pltpu.SemaphoreType.DMA((7 * n,)), pltpu.SemaphoreType.DMA((n,))],
        name=name)(*sends)


ADAMW_BLOCK_BYTES = 4 * 1024 * 1024


def _adamw(parts, w, m, v, name):
    _, R, C = parts.shape
    tr = R
    for t in (1024, 512, 256, 128, 64, 32, 16, 8):
        if R % t == 0 and N_DEV * t * C * 4 <= ADAMW_BLOCK_BYTES:
            tr = t
            break

    def body(p_ref, w_ref, m_ref, v_ref, g_ref, d_ref, nm_ref, nv_ref):
        g = p_ref[0].astype(f32)
        for j in range(1, N_DEV):
            g = g + p_ref[j].astype(f32)
        mn = ADAM_B1 * m_ref[...] + (1.0 - ADAM_B1) * g
        vn = ADAM_B2 * v_ref[...] + (1.0 - ADAM_B2) * jnp.square(g)
        m_hat = mn / (1.0 - ADAM_B1 ** ADAM_STEP)
        v_hat = vn / (1.0 - ADAM_B2 ** ADAM_STEP)
        g_ref[...] = g
        d_ref[...] = -ADAM_LR * (m_hat / (jnp.sqrt(v_hat) + ADAM_EPS) + ADAM_WD * w_ref[...])
        nm_ref[...] = mn
        nv_ref[...] = vn

    t2 = pl.BlockSpec((tr, C), lambda i: (i, 0))
    out = jax.ShapeDtypeStruct((R, C), f32)
    return pl.pallas_call(
        body, grid=(R // tr,), out_shape=(out, out, out, out),
        in_specs=[pl.BlockSpec((N_DEV, tr, C), lambda i: (0, i, 0)), t2, t2, t2], out_specs=(t2, t2, t2, t2),
        compiler_params=_cp(("parallel",), VMEM_LIMIT), name=name)(parts, w, m, v)


def _as2d(a):
    return a.reshape(-1, a.shape[-1])


def _perm_cols(w, order=_ORDER):
    parts = [w[..., _COLS[n][0]:_COLS[n][1]] for n in order]
    pad = jnp.zeros(w.shape[:-1] + (N_ALL - N_IN,), w.dtype)
    return jnp.concatenate(parts + [pad], axis=-1)


def _unperm_cols(w, order=_ORDER):
    pieces, off = {}, 0
    for n in order:
        width = _COLS[n][1] - _COLS[n][0]
        pieces[n] = w[..., off:off + width]
        off += width
    return jnp.concatenate([pieces[n] for n in sorted(_COLS, key=lambda n: _COLS[n][0])], axis=-1)


_SMALL_ROWS = 16


def _pack_small(t):
    z = jnp.zeros((D,), f32)
    misc = z.at[0:16].set(t["b_fg"].reshape(-1)).at[16:24].set(t["a_log"].reshape(-1)).at[24:32].set(t["dt_bias"].reshape(-1))
    misc = misc.at[128:384].set(t["gdn_norm_g"].reshape(-1))
    if "extra" in t:
        misc = misc.at[512].set(t["extra"])
    rows = [t["norm_g"], t["b_merge"].reshape(6, D), t["mem_norm_g"], t["final_norm_g"][None], misc[None],
            jnp.zeros((_SMALL_ROWS - 12, D), f32)]
    return jnp.concatenate(rows, axis=0)


def _unpack_small(a):
    misc = a[11]
    return dict(norm_g=a[0:2], b_merge=a[2:8].reshape(2, 3072), mem_norm_g=a[8:10], final_norm_g=a[10],
                b_fg=misc[0:16].reshape(2, 8), a_log=misc[16:24].reshape(2, 4), dt_bias=misc[24:32].reshape(2, 4),
                gdn_norm_g=misc[128:384].reshape(2, 128), extra=misc[512])


def _layer_fwd(l, x, mem, p):
    sfx = f"_l{l}"
    h, ht = _norm_fwd(x, p["norm_g"], "norm_fwd" + sfx, with_t=True)
    zb = _mm(h, p["w_b"], bf16, 1024, 1024, 1024, "inproj_b" + sfx)
    zf = _mm(h, p["w_f"], f32, 1024, 1024, 1024, "inproj_f" + sfx)
    zs = _mm(h, p["w_s"], f32, 512, 128, 1024, "inproj_s" + sfx)
    sm = _small_prep(zs, p["par"], "small_prep" + sfx)
    S = x.shape[0]
    gt4 = jnp.transpose(sm[:, 8:12]).reshape(4, S // CH, 1, CH)
    qa, ka, st = _fox_prep(zb, sm, "fox_prep" + sfx)
    tab = _fox_bound_table(st, S, min(S, FOX_TILE))
    oa, lse_t = _fox_fwd(qa, ka, zb, tab, "fox_fwd" + sfx)
    gqkv = _gdn_prep(zf, p["conv_w"], "gdn_prep" + sfx)
    ob, tinv, states = _gdn_fwd(gqkv, sm, gt4, "gdn_fwd" + sfx)
    memn = _norm_fwd(mem, p["mem_norm_g"], "mem_norm" + sfx)
    kv = _mm(memn, p["w_mem_kv"], bf16, 256, 1024, 1024, "mem_kv" + sfx)
    xo, ycat, merged = _merge_fwd(x, oa, ob, zb, zf, kv, p["b_merge"], p["gdn_norm_g"], p["w_branch"], p["w_out"], "merge_fwd" + sfx)
    saved = dict(x=x, ht=ht, zb=zb, zf=zf, zs=zs, sm=sm, qa=qa, ka=ka, tab=tab, gt4=gt4, oa=oa, lse_t=lse_t, gqkv=gqkv, ob=ob, tinv=tinv,
                 states=states, memn=memn, kv=kv, ycat=ycat, merged=merged)
    return xo, saved


def _layer_bwd(l, dout, mem, p, s):
    sfx = f"_l{l}"
    dproj, dzf2, dmq, delta, doab, dob, dkv, dbm, dgg = _merge_bwd(
        dout, s["ycat"], s["oa"], s["ob"], s["zb"], s["zf"], s["kv"], p["b_merge"], p["gdn_norm_g"],
        p["w_branch"], p["w_branch_t"], p["w_out_t"], "merge_bwd" + sfx)
    g = {}
    g["w_out"] = _mm(s["merged"], dout, f32, 512, 1024, 512, "dw_out" + sfx, trans_a=True)
    g["w_branch"] = jnp.stack([
        _mm(s["ycat"], dproj, f32, 512, 1024, 512, f"dw_branch{n}" + sfx, trans_a=True, a_cols=(n * 512, 512), b_cols=(n * D, D))
        for n in range(3)])
    g["b_merge"] = dbm[0]
    g["gdn_norm_g"] = dgg[0]
    g["w_mem_kv"] = _mm(s["memn"], dkv, f32, 512, 1024, 256, "dw_mem_kv" + sfx, trans_a=True)
    dmemn = _mm(dkv, p["w_mem_kv_t"], f32, 256, 1024, 1024, "dmem_n" + sfx)
    g["mem_norm_g"] = _norm_bwd(mem, p["mem_norm_g"], dmemn, None, "mem_norm_bwd" + sfx)[0]
    dgqkv, dsm = _gdn_bwd(s["gqkv"], s["sm"], s["gt4"], s["tinv"], s["states"], dob, "gdn_bwd" + sfx)
    dbqkv, dcw = _gdn_prep_bwd(s["zf"], p["conv_w"], dgqkv, "gdn_prep_bwd" + sfx)
    g["conv_w"] = dcw[0:4]
    dq, dk, dv, dfc, dfr = _fox_bwd(s["qa"], s["ka"], s["zb"], doab, _head_rows(s["lse_t"]), delta, s["tab"], "fox_bwd" + sfx)
    dzs, sacc = _small_bwd(s["zs"], p["par"], _head_rows(dfr), dfc, dsm, "small_bwd" + sfx)
    g["b_fg"], g["a_log"], g["dt_bias"] = sacc[0, 0:8], sacc[1, 8:12], sacc[2, 8:12]
    dz = lax.dynamic_update_slice(dzf2, jnp.concatenate([dq, dk, dv, dmq, dbqkv, dzs], axis=1), (0, DZ_MERGE_COLS))
    dh = _mm(dz, p["w_all_t"], f32, 1024, 1024, 1664, "dh" + sfx)
    g["w_in"] = _mm(s["ht"], dz, f32, 1024, 1664, 1024, "dw_in" + sfx)
    dx, dng = _norm_bwd(s["x"], p["norm_g"], dh, dout, "norm_bwd" + sfx)
    g["norm_g"] = dng[0]
    return dx, g


def kernel(x, mem, norm_g, w_in, b_fg, b_merge, conv_w, a_log, dt_bias, gdn_norm_g, mem_norm_g, w_mem_kv, w_branch, w_out, final_norm_g, loss_target, m_norm_g, m_w_in, m_b_fg, m_b_merge, m_conv_w, m_a_log, m_dt_bias, m_gdn_norm_g, m_mem_norm_g, m_w_mem_kv, m_w_branch, m_w_out, m_final_norm_g, v_norm_g, v_w_in, v_b_fg, v_b_merge, v_conv_w, v_a_log, v_dt_bias, v_gdn_norm_g, v_mem_norm_g, v_w_mem_kv, v_w_branch, v_w_out, v_final_norm_g):
    x0, mem0, tgt = x[0], mem[0], loss_target[0]
    shard_w = dict(w_in=w_in, w_mem_kv=w_mem_kv, w_branch=w_branch, w_out=w_out, conv_w=conv_w)
    shard_m = dict(w_in=m_w_in, w_mem_kv=m_w_mem_kv, w_branch=m_w_branch, w_out=m_w_out, conv_w=m_conv_w)
    shard_v = dict(w_in=v_w_in, w_mem_kv=v_w_mem_kv, w_branch=v_w_branch, w_out=v_w_out, conv_w=v_conv_w)
    small_w = dict(norm_g=norm_g, b_fg=b_fg, b_merge=b_merge, a_log=a_log, dt_bias=dt_bias, gdn_norm_g=gdn_norm_g,
                   mem_norm_g=mem_norm_g, final_norm_g=final_norm_g)
    small_m = dict(norm_g=m_norm_g, b_fg=m_b_fg, b_merge=m_b_merge, a_log=m_a_log, dt_bias=m_dt_bias, gdn_norm_g=m_gdn_norm_g,
                   mem_norm_g=m_mem_norm_g, final_norm_g=m_final_norm_g)
    small_v = dict(norm_g=v_norm_g, b_fg=v_b_fg, b_merge=v_b_merge, a_log=v_a_log, dt_bias=v_dt_bias, gdn_norm_g=v_gdn_norm_g,
                   mem_norm_g=v_mem_norm_g, final_norm_g=v_final_norm_g)

    g_in, g_kv, g_br, g_out, conv_all = _all_gather(
        [_b(_as2d(w_in)), _b(_as2d(w_mem_kv)), _b(_as2d(w_branch)), _b(_as2d(w_out)), _as2d(conv_w)], "gather_weights")
    conv_full = jnp.transpose(conv_all.reshape(N_DEV, DEPTH, 4, 192), (1, 2, 0, 3)).reshape(DEPTH, 4, 1536)
    w_in_full = jnp.transpose(g_in.reshape(N_DEV, DEPTH, D, 1026), (1, 2, 0, 3)).reshape(DEPTH, D, N_IN)
    w_all = _perm_cols(w_in_full)
    w_bwd = _perm_cols(w_in_full, _ORDER_BWD)
    w_kv_full = jnp.transpose(g_kv.reshape(N_DEV, DEPTH, 128, D), (1, 0, 2, 3)).reshape(DEPTH, D, D)
    w_br_full = jnp.transpose(g_br.reshape(N_DEV, DEPTH, 3, 512, 128), (1, 2, 3, 0, 4)).reshape(DEPTH, 3, 512, D)
    w_out_full = jnp.transpose(g_out.reshape(N_DEV, DEPTH, 128, D), (1, 0, 2, 3)).reshape(DEPTH, D, D)

    layers = []
    for l in range(DEPTH):
        layers.append(dict(
            norm_g=norm_g[l][None], mem_norm_g=mem_norm_g[l][None], gdn_norm_g=gdn_norm_g[l][None], b_merge=b_merge[l][None],
            par=_small_pars(b_fg[l], a_log[l], dt_bias[l]),
            conv_w=jnp.pad(conv_full[l], ((0, 4), (0, 0))),
            w_b=w_all[l][:, 0:NB], w_f=w_all[l][:, NB:NB + NF], w_s=w_all[l][:, NB + NF:], w_all_t=jnp.transpose(w_bwd[l]),
            w_mem_kv=w_kv_full[l], w_mem_kv_t=jnp.transpose(w_kv_full[l]),
            w_branch=w_br_full[l], w_branch_t=jnp.transpose(w_br_full[l], (0, 2, 1)),
            w_out=w_out_full[l], w_out_t=jnp.transpose(w_out_full[l])))

    acts, saved = x0, []
    for l in range(DEPTH):
        acts, s = _layer_fwd(l, acts, mem0, layers[l])
        saved.append(s)
    dx, dfg, lsum = _loss_head(acts, final_norm_g[None], tgt, "loss_head")

    grads = [None] * DEPTH
    for l in reversed(range(DEPTH)):
        dx, grads[l] = _layer_bwd(l, dx, mem0, layers[l], saved[l])
    grad_x = dx[None]

    def per_dev(name):
        return jnp.stack([grads[l][name] for l in range(DEPTH)])

    dw_in = _unperm_cols(per_dev("w_in"), _ORDER_BWD)
    send = dict(
        w_in=jnp.transpose(dw_in.reshape(DEPTH, D, N_DEV, 1026), (2, 0, 1, 3)).reshape(N_DEV, DEPTH * D, 1026),
        w_mem_kv=jnp.transpose(per_dev("w_mem_kv").reshape(DEPTH, N_DEV, 128, D), (1, 0, 2, 3)).reshape(N_DEV, DEPTH * 128, D),
        w_branch=jnp.transpose(per_dev("w_branch").reshape(DEPTH, 3, 512, N_DEV, 128), (3, 0, 1, 2, 4)).reshape(N_DEV, DEPTH * 3 * 512, 128),
        w_out=jnp.transpose(per_dev("w_out").reshape(DEPTH, N_DEV, 128, D), (1, 0, 2, 3)).reshape(N_DEV, DEPTH * 128, D),
        conv_w=jnp.transpose(per_dev("conv_w").reshape(DEPTH, 4, N_DEV, 192), (2, 0, 1, 3)).reshape(N_DEV, DEPTH * 4, 192))
    parts = dict(zip(_SHARDED, _exchange([_b(send[n]) for n in _SHARDED], "scatter_grads")))
    big = [{}, {}, {}, {}]
    for n in _SHARDED:
        res = _adamw(parts[n], _as2d(shard_w[n]), _as2d(shard_m[n]), _as2d(shard_v[n]), "adamw_" + n)
        for kind in range(4):
            big[kind][n] = res[kind].reshape(shard_w[n].shape)

    small_g = {k: jnp.stack([grads[l][k] for l in range(DEPTH)]) for k in ("norm_g", "b_fg", "b_merge", "a_log", "dt_bias", "gdn_norm_g", "mem_norm_g")}
    small_g["final_norm_g"] = dfg[0]
    small_g["extra"] = lsum[0, 0]
    parts_s, = _all_gather([_pack_small(small_g)], "gather_small")
    g_sm, d_sm, m_sm, v_sm = _adamw(parts_s, _pack_small(small_w), _pack_small(small_m), _pack_small(small_v), "adamw_replicated")

    sml = [_unpack_small(a) for a in (g_sm, d_sm, m_sm, v_sm)]
    loss = sml[0]["extra"]
    names = ("norm_g", "w_in", "b_fg", "b_merge", "conv_w", "a_log", "dt_bias", "gdn_norm_g", "mem_norm_g", "w_mem_kv", "w_branch", "w_out", "final_norm_g")
    outs = [loss, grad_x]
    for kind in range(4):
        for n in names:
            outs.append(big[kind][n] if n in big[kind] else sml[kind][n])
    return tuple(outs)
```

```python
import functools

import jax
import jax.numpy as jnp
from jax import lax
from jax.experimental import pallas as pl
from jax.experimental.pallas import tpu as pltpu

f32, bf16 = jnp.float32, jnp.bfloat16

D = 1024
EPS = 1e-6
CH = 64
N_DEV = 8
DEPTH = 2
FOX_SCALE = 64 ** -0.5
GDN_SCALE = 128 ** -0.5
MEM_SCALE = 128 ** -0.5
NEG = -1e30
VMEM_LIMIT = 56 * 1024 * 1024

ADAM_LR, ADAM_B1, ADAM_B2, ADAM_EPS, ADAM_WD, ADAM_STEP = 0.001, 0.9, 0.999, 1e-08, 0.01, 10

_COLS = dict(aq=(0, 512), ak=(512, 1024), av=(1024, 1536), af=(1536, 1544), az=(1544, 2056),
             bq=(2056, 2568), bk=(2568, 3080), bv=(3080, 3592), ba=(3592, 3596), bb=(3596, 3600),
             bz=(3600, 4112), mq=(4112, 4624), mz=(4624, 5136), gates=(5136, 8208))
_ORDER = ("aq", "ak", "av", "mq", "bq", "bk", "bv", "az", "bz", "mz", "gates", "af", "ba", "bb")
_ORDER_BWD = ("az", "bz", "mz", "gates", "aq", "ak", "av", "mq", "bq", "bk", "bv", "af", "ba", "bb")
DZ_MERGE_COLS = 4608
N_IN = 8208
NB, NF, NS = 2048, 6144, 128
N_ALL = NB + NF + NS

_SHARDED = ("w_in", "w_mem_kv", "w_branch", "w_out", "conv_w")


def _cp(sem=None, vmem=None):
    kw = {}
    if sem is not None:
        kw["dimension_semantics"] = sem
    if vmem is not None:
        kw["vmem_limit_bytes"] = vmem
    return pltpu.CompilerParams(**kw)


def _dot(a, b):
    return jnp.dot(a, b, preferred_element_type=f32)


def _dot_nt(a, b):
    return lax.dot_general(a, b, (((1,), (1,)), ((), ())), preferred_element_type=f32)


def _dot_tn(a, b):
    return lax.dot_general(a, b, (((0,), (0,)), ((), ())), preferred_element_type=f32)


def _split2(x):
    hi = x.astype(bf16)
    return hi, (x - hi.astype(f32)).astype(bf16)


def _mm3(a, b, dims):
    ah, al = _split2(a)
    bh, bl = _split2(b)
    dg = functools.partial(lax.dot_general, dimension_numbers=dims, preferred_element_type=f32)
    return dg(ah, bh) + (dg(ah, bl) + dg(al, bh))


def _hi(a, b):
    return _mm3(a, b, (((1,), (0,)), ((), ())))


def _hi_nt(a, b):
    return _mm3(a, b, (((1,), (1,)), ((), ())))


def _hi_tn(a, b):
    return _mm3(a, b, (((0,), (0,)), ((), ())))


def _hi_b(a, b):
    return _mm3(a, b, (((2,), (1,)), ((0,), (0,))))


def _b(x):
    return x.astype(bf16)


def _sig(x):
    return jax.nn.sigmoid(x)


def _silu(x):
    return x * _sig(x)


def _dsilu(x):
    s = _sig(x)
    return s * (1.0 + x * (1.0 - s))


def _softplus(x):
    return jnp.maximum(x, 0.0) + jnp.log1p(jnp.exp(-jnp.abs(x)))


def _rowsum(x):
    return jnp.sum(x, axis=1, keepdims=True)


def _colsum(x):
    return jnp.sum(x, axis=0, keepdims=True)


def _norm_fwd(x, g, name, with_t=False):
    M = x.shape[0]
    ts = min(M, 512)

    def body(x_ref, g_ref, h_ref, *t_ref):
        xv = x_ref[...]
        r = lax.rsqrt(jnp.mean(xv * xv, axis=-1, keepdims=True) + EPS)
        h = xv * r * g_ref[...]
        h_ref[...] = _b(h)
        if with_t:
            t_ref[0][...] = _b(jnp.transpose(h))

    tile = pl.BlockSpec((ts, D), lambda i: (i, 0))
    shapes, specs = jax.ShapeDtypeStruct((M, D), bf16), tile
    if with_t:
        shapes, specs = (shapes, jax.ShapeDtypeStruct((D, M), bf16)), (tile, pl.BlockSpec((D, ts), lambda i: (0, i)))
    return pl.pallas_call(
        body, grid=(M // ts,), out_shape=shapes,
        in_specs=[tile, pl.BlockSpec((1, D), lambda i: (0, 0))],
        out_specs=specs, compiler_params=_cp(("parallel",)), name=name)(x, g)


def _norm_bwd(x, g, dh, dres, name):
    M = x.shape[0]
    ts = min(M, 512)
    with_dx = dres is not None

    def body(*refs):
        if with_dx:
            x_ref, g_ref, dh_ref, dres_ref, dx_ref, dg_ref = refs
        else:
            x_ref, g_ref, dh_ref, dg_ref = refs
        i = pl.program_id(0)
        xv = x_ref[...]
        r = lax.rsqrt(jnp.mean(xv * xv, axis=-1, keepdims=True) + EPS)
        xh = xv * r
        dh = dh_ref[...].astype(f32)
        part = jnp.broadcast_to(_colsum(dh * xh), (8, D))

        @pl.when(i == 0)
        def _():
            dg_ref[...] = part

        @pl.when(i > 0)
        def _():
            dg_ref[...] += part

        if with_dx:
            dxh = dh * g_ref[...]
            dx_ref[...] = dres_ref[...] + r * (dxh - xh * jnp.mean(dxh * xh, axis=-1, keepdims=True))

    tile = pl.BlockSpec((ts, D), lambda i: (i, 0))
    gspec = pl.BlockSpec((1, D), lambda i: (0, 0))
    acc = pl.BlockSpec((8, D), lambda i: (0, 0))
    if with_dx:
        return pl.pallas_call(
            body, grid=(M // ts,), out_shape=(jax.ShapeDtypeStruct((M, D), f32), jax.ShapeDtypeStruct((8, D), f32)),
            in_specs=[tile, gspec, tile, tile], out_specs=(tile, acc), compiler_params=_cp(("arbitrary",)), name=name)(x, g, dh, dres)
    return pl.pallas_call(
        body, grid=(M // ts,), out_shape=jax.ShapeDtypeStruct((8, D), f32),
        in_specs=[tile, gspec, tile], out_specs=acc, compiler_params=_cp(("arbitrary",)), name=name)(x, g, dh)


def _loss_head(x, g, tgt, name):
    M = x.shape[0]
    ts = min(M, 512)

    def body(x_ref, g_ref, t_ref, dx_ref, dg_ref, ls_ref):
        i = pl.program_id(0)
        xv = x_ref[...]
        gv = g_ref[...]
        r = lax.rsqrt(jnp.mean(xv * xv, axis=-1, keepdims=True) + EPS)
        xh = xv * r
        e = xh * gv - t_ref[...]
        lpart = 0.5 * jnp.sum(jnp.mean(e * e, axis=-1, keepdims=True), axis=0, keepdims=True)
        dy = e * (1.0 / D)
        dgp = jnp.broadcast_to(_colsum(dy * xh), (8, D))
        lp = jnp.broadcast_to(lpart, (8, 128))

        @pl.when(i == 0)
        def _():
            dg_ref[...] = dgp
            ls_ref[...] = lp

        @pl.when(i > 0)
        def _():
            dg_ref[...] += dgp
            ls_ref[...] += lp

        dxh = dy * gv
        dx_ref[...] = r * (dxh - xh * jnp.mean(dxh * xh, axis=-1, keepdims=True))

    tile = pl.BlockSpec((ts, D), lambda i: (i, 0))
    return pl.pallas_call(
        body, grid=(M // ts,),
        out_shape=(jax.ShapeDtypeStruct((M, D), f32), jax.ShapeDtypeStruct((8, D), f32), jax.ShapeDtypeStruct((8, 128), f32)),
        in_specs=[tile, pl.BlockSpec((1, D), lambda i: (0, 0)), tile],
        out_specs=(tile, pl.BlockSpec((8, D), lambda i: (0, 0)), pl.BlockSpec((8, 128), lambda i: (0, 0))),
        compiler_params=_cp(("arbitrary",)), name=name)(x, g, tgt)


def _mm(a, b, out_dtype, tm, tn, tk, name, trans_a=False, a_cols=None, b_cols=None):
    if trans_a:
        K, M = a.shape
    else:
        M, K = a.shape
    N = b.shape[1]
    a0, b0 = 0, 0
    if a_cols is not None:
        a0, M = a_cols
    if b_cols is not None:
        b0, N = b_cols
    tm, tn, tk = min(tm, M), min(tn, N), min(tk, K)
    nk = K // tk
    a0, b0 = a0 // tm, b0 // tn

    def body(a_ref, b_ref, o_ref, acc_ref):
        k = pl.program_id(2)
        av, bv = _b(a_ref[...]), _b(b_ref[...])
        part = _dot_tn(av, bv) if trans_a else _dot(av, bv)
        if nk == 1:
            o_ref[...] = part.astype(out_dtype)
        else:
            @pl.when(k == 0)
            def _():
                acc_ref[...] = part

            @pl.when(k > 0)
            def _():
                acc_ref[...] += part

            @pl.when(k == nk - 1)
            def _():
                o_ref[...] = acc_ref[...].astype(out_dtype)

    a_spec = pl.BlockSpec((tk, tm), lambda i, j, k: (k, i + a0)) if trans_a else pl.BlockSpec((tm, tk), lambda i, j, k: (i, k))
    return pl.pallas_call(
        body, grid=(M // tm, N // tn, nk), out_shape=jax.ShapeDtypeStruct((M, N), out_dtype),
        in_specs=[a_spec, pl.BlockSpec((tk, tn), lambda i, j, k: (k, j + b0))],
        out_specs=pl.BlockSpec((tm, tn), lambda i, j, k: (i, j)),
        scratch_shapes=[pltpu.VMEM((tm, tn), f32)],
        compiler_params=_cp(("parallel", "parallel", "arbitrary"), VMEM_LIMIT), name=name)(a, b)


def _small_pars(b_fg, a_log, dt_bias):
    par = jnp.zeros((8, 128), f32)
    par = par.at[0, 0:8].set(b_fg).at[1, 8:12].set(a_log).at[2, 8:12].set(dt_bias)
    return par


def _small_prep(zs, par, name):
    S = zs.shape[0]
    ts = min(S, 512)

    def body(z_ref, par_ref, o_ref, carry_ref):
        i = pl.program_id(0)

        @pl.when(i == 0)
        def _():
            carry_ref[...] = jnp.zeros_like(carry_ref)

        z = z_ref[...]
        lane = lax.broadcasted_iota(jnp.int32, (ts, 128), 1)
        row = lax.broadcasted_iota(jnp.int32, (ts, 128), 0)
        za = z + par_ref[0:1, :]
        logf = jnp.minimum(za, 0.0) - jnp.log1p(jnp.exp(-jnp.abs(za)))
        glog = -jnp.exp(par_ref[1:2, :]) * _softplus(z + par_ref[2:3, :])
        x = jnp.where(lane < 8, logf, jnp.where(lane < 12, glog, 0.0))
        pos = jnp.where(lane < 8, row, row & (CH - 1))
        s = 1
        while s < ts:
            x = x + jnp.where(pos >= s, pltpu.roll(x, s, 0), 0.0)
            s *= 2
        tot = x + carry_ref[0:1, :]
        carry_ref[...] = jnp.broadcast_to(jnp.where(lane[0:1] < 8, tot[ts - 1:ts, :], 0.0), (8, 128))
        o_ref[...] = jnp.where(lane < 8, tot, jnp.where(lane < 12, x, jnp.where(lane < 16, _sig(z), 0.0)))

    return pl.pallas_call(
        body, grid=(S // ts,), out_shape=jax.ShapeDtypeStruct((S, 128), f32),
        in_specs=[pl.BlockSpec((ts, 128), lambda i: (i, 0)), pl.BlockSpec((8, 128), lambda i: (0, 0))],
        out_specs=pl.BlockSpec((ts, 128), lambda i: (i, 0)), scratch_shapes=[pltpu.VMEM((8, 128), f32)],
        compiler_params=_cp(("arbitrary",)), name=name)(zs, par)


def _small_bwd(zs, par, dfr, dfc, dsm, name):
    S = zs.shape[0]
    ts = min(S, 512)
    nt = S // ts

    def body(z_ref, par_ref, dfr_ref, dfc_ref, dsm_ref, dz_ref, acc_ref, carry_ref):
        i = pl.program_id(0)

        @pl.when(i == 0)
        def _():
            carry_ref[...] = jnp.zeros_like(carry_ref)

        z = z_ref[...]
        dsm_v = dsm_ref[...]
        lane = lax.broadcasted_iota(jnp.int32, (ts, 128), 1)
        row = lax.broadcasted_iota(jnp.int32, (ts, 128), 0)
        df = jnp.transpose(jnp.concatenate([dfr_ref[...], jnp.zeros((120, ts), f32)], axis=0))
        for p in range(4):
            dpair = dfc_ref[p]
            df = df - jnp.where(lane == 2 * p, dpair[:, 0:1], jnp.where(lane == 2 * p + 1, dpair[:, 64:65], 0.0))
        x = jnp.where(lane < 8, df, jnp.where(lane < 12, dsm_v, 0.0))
        pos = jnp.where(lane < 8, row, row & (CH - 1))
        seg = jnp.where(lane < 8, ts, CH)
        s = 1
        while s < ts:
            x = x + jnp.where(pos + s < seg, pltpu.roll(x, ts - s, 0), 0.0)
            s *= 2
        tot = x + carry_ref[0:1, :]
        carry_ref[...] = jnp.broadcast_to(jnp.where(lane[0:1] < 8, tot[0:1, :], 0.0), (8, 128))
        za = z + par_ref[0:1, :]
        daf = tot * _sig(-za)
        zb = z + par_ref[2:3, :]
        nea = -jnp.exp(par_ref[1:2, :])
        glog = nea * _softplus(zb)
        dba = x * nea * _sig(zb)
        beta = _sig(z)
        dbb = dsm_v * beta * (1.0 - beta)
        dz_ref[...] = _b(jnp.where(lane < 8, daf, jnp.where(lane < 12, dba, jnp.where(lane < 16, dbb, 0.0))))
        r0 = _colsum(jnp.where(lane < 8, daf, 0.0))
        r1 = _colsum(jnp.where((lane >= 8) & (lane < 12), x * glog, 0.0))
        r2 = _colsum(jnp.where((lane >= 8) & (lane < 12), dba, 0.0))
        r8 = lax.broadcasted_iota(jnp.int32, (8, 128), 0)
        part = jnp.where(r8 == 0, r0, jnp.where(r8 == 1, r1, jnp.where(r8 == 2, r2, 0.0)))

        @pl.when(i == 0)
        def _():
            acc_ref[...] = part

        @pl.when(i > 0)
        def _():
            acc_ref[...] += part

    rev = pl.BlockSpec((ts, 128), lambda i: (nt - 1 - i, 0))
    rev4 = pl.BlockSpec((4, ts, 128), lambda i: (0, nt - 1 - i, 0))
    c8 = pl.BlockSpec((8, 128), lambda i: (0, 0))
    return pl.pallas_call(
        body, grid=(nt,), out_shape=(jax.ShapeDtypeStruct((S, 128), bf16), jax.ShapeDtypeStruct((8, 128), f32)),
        in_specs=[rev, c8, pl.BlockSpec((8, ts), lambda i: (0, nt - 1 - i)), rev4, rev], out_specs=(rev, c8),
        scratch_shapes=[pltpu.VMEM((8, 128), f32)],
        compiler_params=_cp(("arbitrary",)), name=name)(zs, par, dfr, dfc, dsm)


def _split3(x):
    hi = _b(x).astype(f32)
    r = x - hi
    mid = _b(r).astype(f32)
    return hi, mid, _b(r - mid).astype(f32)


FOX_PREP_ROWS = 512
FOX_TILE = 512
FOX_SKIP_LOG = -32.0


def _fox_prep(zb, sm, name):
    S = zb.shape[0]
    ts = min(S, FOX_PREP_ROWS)

    def body(q_ref, k_ref, f_ref, qa_ref, ka_ref, st_ref):
        lane = lax.broadcasted_iota(jnp.int32, (ts, 128), 1)
        lane8 = lax.broadcasted_iota(jnp.int32, (8, 128), 1)
        f = f_ref[...]
        st = jnp.zeros((8, 128), f32)
        for p in range(4):
            q = q_ref[:, p * 128:(p + 1) * 128].astype(f32) * FOX_SCALE
            k = k_ref[:, p * 128:(p + 1) * 128].astype(f32)
            for h in (0, 1):
                fcol = f[:, 2 * p + h:2 * p + h + 1]
                hi, mid, lo = _split3(fcol)
                own = (lane < 64) if h == 0 else (lane >= 64)
                nq = jnp.sqrt(_rowsum(jnp.where(own, q * q, 0.0)))
                nk = jnp.sqrt(_rowsum(jnp.where(own, k * k, 0.0)))
                stats = (jnp.max(nq, axis=0, keepdims=True), jnp.max(nk, axis=0, keepdims=True),
                         jnp.max(fcol, axis=0, keepdims=True), jnp.min(fcol, axis=0, keepdims=True),
                         jnp.min(-nq * nk, axis=0, keepdims=True))
                for si, val in enumerate(stats):
                    st = jnp.where(lane8 == 8 * si + 2 * p + h, val, st)
                o = 64 if h == 0 else 0
                ones_lo = (lane >= o) & (lane < o + 3)
                ones_hi = (lane >= o + 3) & (lane < o + 6)
                qaug = jnp.where(lane == o, hi, jnp.where(lane == o + 1, mid, jnp.where(lane == o + 2, lo, jnp.where(ones_hi, 1.0, 0.0))))
                kaug = jnp.where(lane == o + 3, -hi, jnp.where(lane == o + 4, -mid, jnp.where(lane == o + 5, -lo, jnp.where(ones_lo, 1.0, 0.0))))
                qa_ref[2 * p + h] = _b(jnp.where(own, q, qaug))
                ka_ref[2 * p + h] = _b(jnp.where(own, k, kaug))
        st_ref[...] = st

    out = jax.ShapeDtypeStruct((8, S, 128), bf16)
    return pl.pallas_call(
        body, grid=(S // ts,), out_shape=(out, out, jax.ShapeDtypeStruct((S // ts * 8, 128), f32)),
        in_specs=[pl.BlockSpec((ts, 512), lambda i: (i, 0)), pl.BlockSpec((ts, 512), lambda i: (i, 1)), pl.BlockSpec((ts, 128), lambda i: (i, 0))],
        out_specs=(pl.BlockSpec((8, ts, 128), lambda i: (0, i, 0)), pl.BlockSpec((8, ts, 128), lambda i: (0, i, 0)),
                   pl.BlockSpec((8, 128), lambda i: (i, 0))),
        compiler_params=_cp(("parallel",)), name=name)(zb, zb, sm)


def _fox_bound_table(st, S, T):
    ts = min(S, FOX_PREP_ROWS)
    g = T // ts
    nt = S // T
    s5 = st.reshape(S // ts, 8, 128)[:, 0, 0:40].reshape(nt, g, 5, 8)
    qn, kn, fmax = s5[:, :, 0].max(axis=1), s5[:, :, 1].max(axis=1), s5[:, :, 2].max(axis=1)
    fmin, lmin = s5[:, :, 3].min(axis=1), s5[:, :, 4].min(axis=1)
    e = qn[:, None] * kn[None, :] + fmax[:, None] - fmin[None, :] - lmin[:, None] + 1.0
    return jnp.transpose(e, (2, 0, 1)).reshape(8, nt * nt)


def _pair_rows(a, T):
    at = jnp.transpose(a)
    r8 = lax.broadcasted_iota(jnp.int32, (8, T), 0)
    return jnp.where(r8 == 0, at[0:1, :], at[64:65, :])


def _fox_fwd(qa, ka, zb, tab, name):
    S = zb.shape[0]
    T = min(S, FOX_TILE)
    nt = S // T

    def body(tab_ref, qa_ref, ka_ref, v_ref, o_ref, lset_ref, m_ref, l_ref, acc_ref):
        p, i = pl.program_id(0), pl.program_id(1)
        m_ref[...] = jnp.full_like(m_ref, NEG)
        l_ref[...] = jnp.zeros_like(l_ref)
        acc_ref[...] = jnp.zeros_like(acc_ref)
        row = lax.broadcasted_iota(jnp.int32, (T, T), 0)
        col = lax.broadcasted_iota(jnp.int32, (T, T), 1)

        def head_tile(h, j, masked):
            off = pl.multiple_of(j * T, T)
            s = _dot_nt(qa_ref[h], ka_ref[h, pl.ds(off, T), :])
            if masked:
                s = jnp.where(row >= col, s, NEG)
            m_old = m_ref[h]
            m_new = jnp.maximum(m_old, jnp.max(s, axis=1, keepdims=True))
            alpha = jnp.exp(m_old - m_new)
            pr = jnp.exp(s - jnp.tile(m_new, (1, T // 128)))
            l_ref[h] = alpha * l_ref[h] + _rowsum(pr)
            acc_ref[h] = alpha * acc_ref[h] + _dot(_b(pr), v_ref[pl.ds(off, T), :])
            m_ref[h] = m_new

        def step(j, c):
            for h in (0, 1):
                @pl.when(tab_ref[2 * p + h, i * nt + j] > FOX_SKIP_LOG)
                def _():
                    head_tile(h, j, False)
            return c

        lax.fori_loop(0, i, step, 0)
        for h in (0, 1):
            head_tile(h, i, True)
        lane2 = lax.broadcasted_iota(jnp.int32, (T, 128), 1)
        o_ref[...] = jnp.where(lane2 < 64, acc_ref[0] / l_ref[0], acc_ref[1] / l_ref[1])
        lse = jnp.where(lane2 < 64, m_ref[0] + jnp.log(l_ref[0]), m_ref[1] + jnp.log(l_ref[1]))
        lset_ref[0] = _pair_rows(lse, T)

    return pl.pallas_call(
        body, grid=(4, S // T),
        out_shape=(jax.ShapeDtypeStruct((S, 512), f32), jax.ShapeDtypeStruct((4, 8, S), f32)),
        in_specs=[pl.BlockSpec(memory_space=pltpu.SMEM), pl.BlockSpec((2, T, 128), lambda p, i: (p, i, 0)),
                  pl.BlockSpec((2, S, 128), lambda p, i: (p, 0, 0), pipeline_mode=pl.Buffered(1)),
                  pl.BlockSpec((S, 128), lambda p, i: (0, 8 + p), pipeline_mode=pl.Buffered(1))],
        out_specs=(pl.BlockSpec((T, 128), lambda p, i: (i, p)), pl.BlockSpec((1, 8, T), lambda p, i: (p, 0, i))),
        scratch_shapes=[pltpu.VMEM((2, T, 128), f32), pltpu.VMEM((2, T, 128), f32), pltpu.VMEM((2, T, 128), f32)],
        compiler_params=_cp(("arbitrary", "arbitrary"), VMEM_LIMIT), name=name)(tab, qa, ka, zb)


def _fox_bwd(qa, ka, zb, dob, lse_t, dl_t, tab, name, comm_sends=()):
    S = zb.shape[0]
    T = min(S, FOX_TILE)
    nq = S // T
    nc = len(comm_sends)

    def body(*refs):
        tab_ref, ka_ref, v_ref, qa_ref, do_ref, lt_ref, dt_ref = refs[:7]
        cs_refs = refs[7:7 + nc]
        dq_ref, dk_ref, dv_ref, dfc_ref, dfr_ref = refs[7 + nc:12 + nc]
        cr_refs = refs[12 + nc:12 + 2 * nc]
        dqa_ref, dka_ref, dva_ref, fs_ref = refs[12 + 2 * nc:16 + 2 * nc]
        p, j = pl.program_id(0), pl.program_id(1)
        if nc:
            @pl.when((p == 0) & (j == 0))
            def _():
                for cp in _exchange_copies(cs_refs, cr_refs, *refs[16 + 2 * nc:]):
                    cp.start()
        lane1 = lax.broadcasted_iota(jnp.int32, (1, 128), 1)
        lane2 = lax.broadcasted_iota(jnp.int32, (T, 128), 1)
        hm = (lane1 < 64, lane1 >= 64)
        v = v_ref[...]
        vsm = [jnp.where(hm[h], v, jnp.zeros_like(v)) for h in (0, 1)]
        ksm = [jnp.where(hm[h], ka_ref[h], jnp.zeros_like(v)) for h in (0, 1)]

        @pl.when(j == 0)
        def _():
            dqa_ref[...] = jnp.zeros_like(dqa_ref)
            dfr_ref[...] = jnp.zeros_like(dfr_ref)

        dka_ref[...] = jnp.zeros_like(dka_ref)
        dva_ref[...] = jnp.zeros_like(dva_ref)
        fs_ref[...] = jnp.zeros_like(fs_ref)
        row = lax.broadcasted_iota(jnp.int32, (T, T), 0)
        col = lax.broadcasted_iota(jnp.int32, (T, T), 1)

        def head_tile(h, i, masked):
            off = pl.multiple_of(i * T, T)
            dot_ = do_ref[pl.ds(off, T), :]
            hr = pl.ds(2 * p + h, 1)
            qt = qa_ref[h, pl.ds(off, T), :]
            s_t = _dot_nt(ka_ref[h], qt)
            if masked:
                s_t = jnp.where(col >= row, s_t, NEG)
            p_t = jnp.exp(s_t - lt_ref[hr, pl.ds(off, T)])
            dva_ref[h] += _dot(_b(p_t), dot_)
            dp_t = _dot_nt(vsm[h], dot_)
            ds_t = p_t * (dp_t - dt_ref[hr, pl.ds(off, T)])
            dsb = _b(ds_t)
            dka_ref[h] += _dot(dsb, qt)
            fs_ref[h] += _rowsum(ds_t)
            dfr_ref[0, pl.ds(h, 1), pl.ds(off, T)] += _colsum(ds_t)
            dqa_ref[pl.ds(off, T), :] += _dot_tn(dsb, ksm[h])

        def step(i, c):
            for h in (0, 1):
                @pl.when(tab_ref[2 * p + h, i * nq + j] > FOX_SKIP_LOG)
                def _():
                    head_tile(h, i, False)
            return c

        for h in (0, 1):
            head_tile(h, j, True)
        lax.fori_loop(j + 1, nq, step, 0)
        dk_ref[...] = _b(jnp.where(lane2 < 64, dka_ref[0], dka_ref[1]))
        dv_ref[...] = _b(jnp.where(lane2 < 64, dva_ref[0], dva_ref[1]))
        dfc_ref[0] = jnp.where(lane2 < 64, fs_ref[0], fs_ref[1])
        dq_ref[...] = _b(dqa_ref[pl.ds(pl.multiple_of(j * T, T), T), :] * FOX_SCALE)
        if nc:
            @pl.when((p == 3) & (j == nq - 1))
            def _():
                for cp in _exchange_copies(cs_refs, cr_refs, *refs[16 + 2 * nc:]):
                    cp.wait()

    one = pl.Buffered(1)
    res = pl.BlockSpec((8, S), lambda p, j: (0, 0), pipeline_mode=one)
    tk = pl.BlockSpec((T, 128), lambda p, j: (j, p))
    anyspec = pl.BlockSpec(memory_space=pl.ANY)
    outs = pl.pallas_call(
        body, grid=(4, nq),
        out_shape=(jax.ShapeDtypeStruct((S, 512), bf16), jax.ShapeDtypeStruct((S, 512), bf16), jax.ShapeDtypeStruct((S, 512), bf16),
                   jax.ShapeDtypeStruct((4, S, 128), f32), jax.ShapeDtypeStruct((4, 8, S), f32))
        + tuple(jax.ShapeDtypeStruct(c.shape, c.dtype) for c in comm_sends),
        in_specs=[pl.BlockSpec(memory_space=pltpu.SMEM),
                  pl.BlockSpec((2, T, 128), lambda p, j: (p, j, 0)), pl.BlockSpec((T, 128), lambda p, j: (j, 8 + p)),
                  pl.BlockSpec((2, S, 128), lambda p, j: (p, 0, 0), pipeline_mode=one),
                  pl.BlockSpec((S, 128), lambda p, j: (0, p), pipeline_mode=one), res, res] + [anyspec] * nc,
        out_specs=(tk, tk, tk, pl.BlockSpec((1, T, 128), lambda p, j: (p, j, 0)),
                   pl.BlockSpec((1, 8, S), lambda p, j: (p, 0, 0))) + tuple([anyspec] * nc),
        scratch_shapes=[pltpu.VMEM((S, 128), f32), pltpu.VMEM((2, T, 128), f32), pltpu.VMEM((2, T, 128), f32), pltpu.VMEM((2, T, 1), f32)]
        + (_exchange_scratch(nc) if nc else []),
        compiler_params=_cp(("arbitrary", "arbitrary"), VMEM_LIMIT), name=name)(tab, ka, zb, qa, dob, lse_t, dl_t, *comm_sends)
    return outs[:5], list(outs[5:])


def _head_rows(a):
    return a[:, 0:2, :].reshape(8, a.shape[2])


def _conv_taps(ext, x, w_ref, ts):
    y = x * w_ref[3:4, :]
    shifted = []
    for k in (1, 2, 3):
        xs = pltpu.roll(ext, k, 0)[8:]
        shifted.append(xs)
        y = y + xs * w_ref[3 - k:4 - k, :]
    return y, shifted


def _gdn_prep(zf, cw, name):
    S = zf.shape[0]
    ts = min(S, 512)

    def body(x_ref, w_ref, o_ref, tail_ref):
        i = pl.program_id(0)

        @pl.when(i == 0)
        def _():
            tail_ref[...] = jnp.zeros_like(tail_ref)

        x = x_ref[...]
        ext = jnp.concatenate([tail_ref[...], x], axis=0)
        y, _ = _conv_taps(ext, x, w_ref, ts)
        tail_ref[...] = x[ts - 8:, :]
        a = _silu(y)
        for hb in range(12):
            blk = a[:, hb * 128:(hb + 1) * 128]
            if hb < 8:
                blk = blk * lax.rsqrt(_rowsum(blk * blk) + EPS)
            if hb < 4:
                blk = blk * GDN_SCALE
            o_ref[:, hb * 128:(hb + 1) * 128] = blk

    return pl.pallas_call(
        body, grid=(S // ts,), out_shape=jax.ShapeDtypeStruct((S, 1536), f32),
        in_specs=[pl.BlockSpec((ts, 1536), lambda i: (i, 0)), pl.BlockSpec((8, 1536), lambda i: (0, 0))],
        out_specs=pl.BlockSpec((ts, 1536), lambda i: (i, 0)), scratch_shapes=[pltpu.VMEM((8, 1536), f32)],
        compiler_params=_cp(("arbitrary",), VMEM_LIMIT), name=name)(zf, cw)


def _gdn_prep_bwd(zf, cw, dg, name):
    S = zf.shape[0]
    ts = min(S, 512)
    nt = S // ts

    def body(x_ref, xp_ref, w_ref, dg_ref, dx_ref, dw_ref, head_ref):
        i = pl.program_id(0)

        @pl.when(i == 0)
        def _():
            head_ref[...] = jnp.zeros_like(head_ref)

        x = x_ref[...]
        prev = jnp.where(i == nt - 1, 0.0, xp_ref[...])
        ext = jnp.concatenate([prev, x], axis=0)
        y, shifted = _conv_taps(ext, x, w_ref, ts)
        a = _silu(y)
        das = []
        for hb in range(12):
            blk = a[:, hb * 128:(hb + 1) * 128]
            d = dg_ref[:, hb * 128:(hb + 1) * 128]
            if hb < 4:
                d = d * GDN_SCALE
            if hb < 8:
                r = lax.rsqrt(_rowsum(blk * blk) + EPS)
                n = blk * r
                d = r * (d - n * _rowsum(d * n))
            das.append(d)
        dy = jnp.concatenate(das, axis=1) * _dsilu(y)
        extd = jnp.concatenate([dy, head_ref[...]], axis=0)
        dx = dy * w_ref[3:4, :]
        for k in (1, 2, 3):
            dx = dx + pltpu.roll(extd, ts + 8 - k, 0)[:ts] * w_ref[3 - k:4 - k, :]
        head_ref[...] = dy[0:8, :]
        dx_ref[...] = _b(dx)
        r8 = lax.broadcasted_iota(jnp.int32, (8, 1536), 0)
        part = jnp.where(r8 == 3, _colsum(dy * x), 0.0)
        for k in (1, 2, 3):
            part = jnp.where(r8 == 3 - k, _colsum(dy * shifted[k - 1]), part)

        @pl.when(i == 0)
        def _():
            dw_ref[...] = part

        @pl.when(i > 0)
        def _():
            dw_ref[...] += part

    rev = pl.BlockSpec((ts, 1536), lambda i: (nt - 1 - i, 0))
    prev8 = pl.BlockSpec((8, 1536), lambda i: (jnp.maximum((nt - 1 - i) * (ts // 8) - 1, 0), 0))
    w8 = pl.BlockSpec((8, 1536), lambda i: (0, 0))
    return pl.pallas_call(
        body, grid=(nt,), out_shape=(jax.ShapeDtypeStruct((S, 1536), bf16), jax.ShapeDtypeStruct((8, 1536), f32)),
        in_specs=[rev, prev8, w8, rev], out_specs=(rev, w8), scratch_shapes=[pltpu.VMEM((8, 1536), f32)],
        compiler_params=_cp(("arbitrary",), VMEM_LIMIT), name=name)(zf, zf, cw, dg)


def _tri_inv(a, row, col):
    same = (row >> 4) == (col >> 4)
    dm = jnp.where(same, a, 0.0)
    lo = a - dm
    eye = jnp.where(row == col, 1.0, 0.0)
    d2 = _hi_b(dm, dm)
    d4 = _hi_b(d2, d2)
    d8 = _hi_b(d4, d4)
    x0 = _hi_b(_hi_b(eye - dm, eye + d2), _hi_b(eye + d4, eye + d8))
    n = _hi_b(x0, lo)
    n2 = _hi_b(n, n)
    return _hi_b(_hi_b(eye - n, eye + n2), x0)


def _bd(a, b):
    return lax.dot_general(a, b, (((2,), (1,)), ((0,), (0,))), preferred_element_type=f32)


def _bd_nt(a, b):
    return lax.dot_general(a, b, (((2,), (2,)), ((0,), (0,))), preferred_element_type=f32)


def _bd_tn(a, b):
    return lax.dot_general(a, b, (((1,), (1,)), ((0,), (0,))), preferred_element_type=f32)


def _hi_b_nt(a, b):
    return _mm3(a, b, (((2,), (2,)), ((0,), (0,))))


def _hi_b_tn(a, b):
    return _mm3(a, b, (((1,), (1,)), ((0,), (0,))))


def _gdn_local(x_ref, sm_ref, gt_ref, row, col, cps=1):
    idx = [(c, h) for c in range(cps) for h in range(4)]

    def rows(c):
        return slice(c * CH, (c + 1) * CH)

    q = jnp.stack([x_ref[rows(c), h * 128:(h + 1) * 128] for c, h in idx])
    k = jnp.stack([x_ref[rows(c), 512 + h * 128:512 + (h + 1) * 128] for c, h in idx])
    v = jnp.stack([x_ref[rows(c), 1024 + h * 128:1024 + (h + 1) * 128] for c, h in idx])
    gc = jnp.stack([sm_ref[rows(c), 8 + h:9 + h] for c, h in idx])
    beta = jnp.stack([sm_ref[rows(c), 12 + h:13 + h] for c, h in idx])
    gr = jnp.stack([gt_ref[h, c] for c, h in idx])
    eg = jnp.exp(gc)
    gl = gc[:, CH - 1:CH, :]
    dec = jnp.exp(gl - gc)
    gm = gc - gr
    gam_i = jnp.exp(jnp.where(row >= col, gm, -jnp.inf))
    gam_s = jnp.where(row > col, gam_i, 0.0)
    kb = k * beta
    return dict(q=q, k=k, v=v, beta=beta, eg=eg, egl=jnp.exp(gl), dec=dec, gam_i=gam_i, gam_s=gam_s,
                kb=kb, vb=v * beta, kbg=kb * eg, qdec=q * eg, kdec=k * dec,
                a=_bd_nt(_b(kb), _b(k)) * gam_s, aqk=_bd_nt(_b(q), _b(k)) * gam_i)


GDN_FWD_CHUNKS = 8


def _gdn_fwd(gqkv, sm, gt4, name, cps=GDN_FWD_CHUNKS):
    S = gqkv.shape[0]
    N = S // CH
    cps = min(cps, N)
    R = cps * CH

    def body(x_ref, sm_ref, gt_ref, o_ref, t_ref, st_ref, s_ref):
        n = pl.program_id(0)

        @pl.when(n == 0)
        def _():
            s_ref[...] = jnp.zeros_like(s_ref)

        row = lax.broadcasted_iota(jnp.int32, (CH, CH), 0)
        col = lax.broadcasted_iota(jnp.int32, (CH, CH), 1)
        c = _gdn_local(x_ref, sm_ref, gt_ref, row, col, cps)
        t = _tri_inv(c["a"], row, col)
        uw = _hi_b(t, jnp.concatenate([c["vb"], c["kbg"]], axis=2))
        u, w = uw[:, :, :128], uw[:, :, 128:]
        for ci in range(cps):
            sl = slice(4 * ci, 4 * ci + 4)
            rs = slice(ci * CH, (ci + 1) * CH)
            st = s_ref[...]
            st_ref[ci] = st
            sb = _b(st)
            vnew = u[sl] - _bd(_b(w[sl]), sb)
            o = _bd(_b(c["qdec"][sl]), sb) + _bd(_b(c["aqk"][sl]), _b(vnew))
            for h in range(4):
                o_ref[rs, h * 128:(h + 1) * 128] = o[h]
                t_ref[h, rs, :] = t[4 * ci + h]
            s_ref[...] = st * c["egl"][sl] + _bd_tn(_b(c["kdec"][sl]), _b(vnew))

    return pl.pallas_call(
        body, grid=(N // cps,),
        out_shape=(jax.ShapeDtypeStruct((S, 512), f32), jax.ShapeDtypeStruct((4, S, CH), f32), jax.ShapeDtypeStruct((N, 4, 128, 128), f32)),
        in_specs=[pl.BlockSpec((R, 1536), lambda n: (n, 0)), pl.BlockSpec((R, 128), lambda n: (n, 0)),
                  pl.BlockSpec((4, cps, 1, CH), lambda n: (0, n, 0, 0))],
        out_specs=(pl.BlockSpec((R, 512), lambda n: (n, 0)), pl.BlockSpec((4, R, CH), lambda n: (0, n, 0)),
                   pl.BlockSpec((cps, 4, 128, 128), lambda n: (n, 0, 0, 0))),
        scratch_shapes=[pltpu.VMEM((4, 128, 128), f32)], compiler_params=_cp(("arbitrary",)), name=name)(gqkv, sm, gt4)


GDN_BWD_CHUNKS = 4


def _gdn_bwd(gqkv, sm, gt4, tinv, states, do, name, cps=GDN_BWD_CHUNKS):
    S = gqkv.shape[0]
    N = S // CH
    cps = min(cps, N)
    R = cps * CH

    def body(x_ref, sm_ref, gt_ref, t_ref, st_ref, do_ref, dx_ref, dsm_ref, ds_ref):
        n = pl.program_id(0)

        @pl.when(n == 0)
        def _():
            ds_ref[...] = jnp.zeros_like(ds_ref)

        row = lax.broadcasted_iota(jnp.int32, (CH, CH), 0)
        col = lax.broadcasted_iota(jnp.int32, (CH, CH), 1)
        row1 = lax.broadcasted_iota(jnp.int32, (CH, 1), 0)
        lane = lax.broadcasted_iota(jnp.int32, (CH, 128), 1)
        ones = jnp.ones((4 * cps, CH, 128), f32)
        idx = [(ci, h) for ci in range(cps) for h in range(4)]
        c = _gdn_local(x_ref, sm_ref, gt_ref, row, col, cps)
        q, k, v, beta, eg = c["q"], c["k"], c["v"], c["beta"], c["eg"]
        t = jnp.stack([t_ref[h, ci * CH:(ci + 1) * CH, :] for ci, h in idx])
        uw = _hi_b(t, jnp.concatenate([c["vb"], c["kbg"]], axis=2))
        u, w = uw[:, :, :128], uw[:, :, 128:]
        st = st_ref[...].reshape(4 * cps, 128, 128)
        sb = _b(st)
        vnew = u - _bd(_b(w), sb)
        dob = _b(jnp.stack([do_ref[ci * CH:(ci + 1) * CH, h * 128:(h + 1) * 128] for ci, h in idx]))
        vnb = _b(vnew)
        dqdec = _bd_nt(dob, sb)
        daqk = jnp.where(row >= col, _bd_nt(dob, vnb), 0.0)
        qd_do = _bd_tn(_b(c["qdec"]), dob)
        aqk_do = _bd_tn(_b(c["aqk"]), dob)
        kdecb, wb = _b(c["kdec"]), _b(w)
        dvnew_l, dkdec_l, dgl_l = [None] * cps, [None] * cps, [None] * cps
        for ci in reversed(range(cps)):
            sl = slice(4 * ci, 4 * ci + 4)
            dsp = ds_ref[...]
            dspb = _b(dsp)
            dvn = _bd(kdecb[sl], dspb) + aqk_do[sl]
            dvnew_l[ci] = dvn
            dkdec_l[ci] = _bd_nt(vnb[sl], dspb)
            dgl_l[ci] = c["egl"][sl] * jnp.sum(dsp * st[sl], axis=(1, 2), keepdims=True)
            ds_ref[...] = dsp * c["egl"][sl] + qd_do[sl] - _bd_tn(wb[sl], _b(dvn))
        dvnew = jnp.concatenate(dvnew_l, axis=0)
        dkdec = jnp.concatenate(dkdec_l, axis=0)
        dgl = jnp.concatenate(dgl_l, axis=0)
        dw = -_bd_nt(_b(dvnew), sb)
        duw = _hi_b_tn(t, jnp.concatenate([dvnew, dw], axis=2))
        dvb, dkbg = duw[:, :, :128], duw[:, :, 128:]
        da = -jnp.where(row > col, _hi_b_nt(duw, uw), 0.0)
        dp = da * c["gam_s"]
        dqk = daqk * c["gam_i"]
        m = da * c["a"] + daqk * c["aqk"]
        csum = _hi_b_tn(m, ones)[:, :, 0:1]
        kk = dkdec * c["kdec"]

        def lsum(a):
            return jnp.sum(a, axis=2, keepdims=True)

        dgv = lsum(m) - csum + lsum(dqdec * c["qdec"]) - lsum(kk) + lsum(dkbg * c["kbg"])
        dgv = dgv + jnp.where(row1 == CH - 1, dgl + jnp.sum(kk, axis=(1, 2), keepdims=True), 0.0)
        dpb, dqkb = _b(dp), _b(dqk)
        dkb = _bd(dpb, _b(k)) + dkbg * eg
        dk = _bd_tn(dpb, _b(c["kb"])) + _bd_tn(dqkb, _b(q)) + dkdec * c["dec"] + dkb * beta
        dq = _bd(dqkb, _b(k)) + dqdec * eg
        dbeta = lsum(dkb * k) + lsum(dvb * v)
        dv = dvb * beta
        for ci in range(cps):
            rs = slice(ci * CH, (ci + 1) * CH)
            dsm = jnp.zeros((CH, 128), f32)
            for h in range(4):
                b = 4 * ci + h
                dx_ref[rs, h * 128:(h + 1) * 128] = dq[b]
                dx_ref[rs, 512 + h * 128:512 + (h + 1) * 128] = dk[b]
                dx_ref[rs, 1024 + h * 128:1024 + (h + 1) * 128] = dv[b]
                dsm = jnp.where(lane == 8 + h, dgv[b], jnp.where(lane == 12 + h, dbeta[b], dsm))
            dsm_ref[rs, :] = dsm

    G = N // cps
    return pl.pallas_call(
        body, grid=(G,), out_shape=(jax.ShapeDtypeStruct((S, 1536), f32), jax.ShapeDtypeStruct((S, 128), f32)),
        in_specs=[pl.BlockSpec((R, 1536), lambda n: (G - 1 - n, 0)), pl.BlockSpec((R, 128), lambda n: (G - 1 - n, 0)),
                  pl.BlockSpec((4, cps, 1, CH), lambda n: (0, G - 1 - n, 0, 0)), pl.BlockSpec((4, R, CH), lambda n: (0, G - 1 - n, 0)),
                  pl.BlockSpec((cps, 4, 128, 128), lambda n: (G - 1 - n, 0, 0, 0)), pl.BlockSpec((R, 512), lambda n: (G - 1 - n, 0))],
        out_specs=(pl.BlockSpec((R, 1536), lambda n: (G - 1 - n, 0)), pl.BlockSpec((R, 128), lambda n: (G - 1 - n, 0))),
        scratch_shapes=[pltpu.VMEM((4, 128, 128), f32)], compiler_params=_cp(("arbitrary",), VMEM_LIMIT), name=name)(gqkv, sm, gt4, tinv, states, do)


MERGE_ROWS = 256
MERGE_FWD_ROWS = 512


def _mem_attn(q, kv_ref, h):
    s = _dot_nt(q, kv_ref[:, h * 128:(h + 1) * 128]) * MEM_SCALE
    e = jnp.exp(s - jnp.max(s, axis=1, keepdims=True))
    return e / _rowsum(e)


def _gdn_out_norm(ob):
    r = lax.rsqrt(jnp.mean(ob * ob, axis=-1, keepdims=True) + EPS)
    return ob * r, r


def _merge_fwd(x, oa, ob, zb, zf, kv, b_merge, gdn_g, w_branch, w_out, name):
    S = x.shape[0]
    ts = min(S, MERGE_FWD_ROWS)

    def body(x_ref, oa_ref, ob_ref, mq_ref, az_ref, bz_ref, mz_ref, gt_ref, kv_ref, bm_ref, gg_ref, wb_ref, wo_ref,
             xo_ref, y_ref, mg_ref):
        y_ref[:, 0:512] = _b(oa_ref[...] * _silu(az_ref[...]))
        for h in range(4):
            sl = slice(h * 128, (h + 1) * 128)
            nb, _ = _gdn_out_norm(ob_ref[:, sl])
            y_ref[:, 512 + h * 128:512 + (h + 1) * 128] = _b(nb * gg_ref[...] * _silu(bz_ref[:, sl]))
            pm = _mem_attn(mq_ref[:, sl], kv_ref, h)
            om = _dot(_b(pm), kv_ref[:, 512 + h * 128:512 + (h + 1) * 128])
            y_ref[:, 1024 + h * 128:1024 + (h + 1) * 128] = _b(om * _silu(mz_ref[:, sl]))
        merged = jnp.zeros((ts, D), f32)
        for n in range(3):
            gate = _sig(gt_ref[:, n * D:(n + 1) * D] + bm_ref[:, n * D:(n + 1) * D])
            merged = merged + gate * _dot(y_ref[:, n * 512:(n + 1) * 512], wb_ref[n])
        mb = _b(merged)
        mg_ref[...] = mb
        xo_ref[...] = x_ref[...] + _dot(mb, wo_ref[...])

    def col(w, c):
        return pl.BlockSpec((ts, w), lambda i: (i, c))

    def full(shape):
        return pl.BlockSpec(shape, lambda i: tuple(0 for _ in shape))

    return pl.pallas_call(
        body, grid=(S // ts,),
        out_shape=(jax.ShapeDtypeStruct((S, D), f32), jax.ShapeDtypeStruct((S, 1536), bf16), jax.ShapeDtypeStruct((S, D), bf16)),
        in_specs=[col(D, 0), col(512, 0), col(512, 0), col(512, 3), col(512, 3), col(512, 4), col(512, 5), col(3072, 1),
                  full((256, D)), full((1, 3072)), full((1, 128)), full((3, 512, D)), full((D, D))],
        out_specs=(col(D, 0), col(1536, 0), col(D, 0)),
        compiler_params=_cp(("parallel",), VMEM_LIMIT), name=name)(x, oa, ob, zb, zf, zf, zf, zf, kv, b_merge, gdn_g, w_branch, w_out)


def _merge_bwd(dout, ycat, oa, ob, zb, zf, kv, b_merge, gdn_g, w_branch, w_branch_t, w_out_t, name):
    S = dout.shape[0]
    ts = min(S, MERGE_ROWS)

    def body(do_ref, y_ref, oa_ref, ob_ref, mq_ref, az_ref, bz_ref, mz_ref, gt_ref, kv_ref, bm_ref, gg_ref, wb_ref, wbt_ref, wot_ref,
             dpj_ref, dz_ref, dmq_ref, dlt_ref, doab_ref, dob_ref, dkv_ref, dbm_ref, dgg_ref):
        i = pl.program_id(0)
        dmerged = _dot(_b(do_ref[...]), wot_ref[...])
        dys = []
        dbm_parts = []
        for n in range(3):
            cs = slice(n * D, (n + 1) * D)
            gate = _sig(gt_ref[:, cs] + bm_ref[:, cs])
            proj = _dot(y_ref[:, n * 512:(n + 1) * 512], wb_ref[n])
            dlogit = dmerged * proj * gate * (1.0 - gate)
            dz_ref[:, 1536 + n * D:1536 + (n + 1) * D] = _b(dlogit)
            dbm_parts.append(_colsum(dlogit))
            dproj = _b(dmerged * gate)
            dpj_ref[:, cs] = dproj
            dys.append(_dot(dproj, wbt_ref[n]))
        dbm = jnp.broadcast_to(jnp.concatenate(dbm_parts, axis=1), (8, 3072))
        az = az_ref[...]
        oa = oa_ref[...]
        doa = dys[0] * _silu(az)
        doab_ref[...] = _b(doa)
        prod = doa * oa
        lane = lax.broadcasted_iota(jnp.int32, (ts, 128), 1)
        dl = jnp.zeros((ts, 128), f32)
        for p in range(4):
            blk = prod[:, p * 128:(p + 1) * 128]
            dl = jnp.where(lane == 2 * p, _rowsum(jnp.where(lane < 64, blk, 0.0)),
                           jnp.where(lane == 2 * p + 1, _rowsum(jnp.where(lane >= 64, blk, 0.0)), dl))
        dlt_ref[...] = jnp.transpose(dl)[0:8, :]
        dz_ref[:, 0:512] = _b(dys[0] * oa * _dsilu(az))
        gg = gg_ref[...]
        dgg = jnp.zeros((1, 128), f32)
        dkv_parts_k, dkv_parts_v = [], []
        for h in range(4):
            sl = slice(h * 128, (h + 1) * 128)
            bz = bz_ref[:, sl]
            dyb = dys[1][:, sl]
            nb, r = _gdn_out_norm(ob_ref[:, sl])
            dz_ref[:, 512 + h * 128:512 + (h + 1) * 128] = _b(dyb * nb * gg * _dsilu(bz))
            dng = dyb * _silu(bz)
            dgg = dgg + _colsum(dng * nb)
            dnb = dng * gg
            dob_ref[:, sl] = r * (dnb - nb * jnp.mean(dnb * nb, axis=-1, keepdims=True))
            mz = mz_ref[:, sl]
            dym = dys[2][:, sl]
            q = mq_ref[:, sl]
            kh = kv_ref[:, sl]
            vh = kv_ref[:, 512 + h * 128:512 + (h + 1) * 128]
            pm = _mem_attn(q, kv_ref, h)
            pmb = _b(pm)
            om = _dot(pmb, vh)
            dz_ref[:, 1024 + h * 128:1024 + (h + 1) * 128] = _b(dym * om * _dsilu(mz))
            dom = _b(dym * _silu(mz))
            dkv_parts_v.append(_dot_tn(pmb, dom))
            dpm = _dot_nt(dom, vh)
            dsm = _b(pm * (dpm - _rowsum(dpm * pm)) * MEM_SCALE)
            dmq_ref[:, sl] = _b(_dot(dsm, kh))
            dkv_parts_k.append(_dot_tn(dsm, q))
        dkv = jnp.concatenate(dkv_parts_k + dkv_parts_v, axis=1)
        dggb = jnp.broadcast_to(dgg, (8, 128))

        @pl.when(i == 0)
        def _():
            dkv_ref[...] = dkv
            dbm_ref[...] = dbm
            dgg_ref[...] = dggb

        @pl.when(i > 0)
        def _():
            dkv_ref[...] += dkv
            dbm_ref[...] += dbm
            dgg_ref[...] += dggb

    def col(w, c):
        return pl.BlockSpec((ts, w), lambda i: (i, c))

    def full(shape):
        return pl.BlockSpec(shape, lambda i: tuple(0 for _ in shape))

    return pl.pallas_call(
        body, grid=(S // ts,),
        out_shape=(jax.ShapeDtypeStruct((S, 3072), bf16), jax.ShapeDtypeStruct((S, N_ALL), bf16), jax.ShapeDtypeStruct((S, 512), bf16),
                   jax.ShapeDtypeStruct((8, S), f32), jax.ShapeDtypeStruct((S, 512), bf16), jax.ShapeDtypeStruct((S, 512), f32),
                   jax.ShapeDtypeStruct((256, D), f32), jax.ShapeDtypeStruct((8, 3072), f32), jax.ShapeDtypeStruct((8, 128), f32)),
        in_specs=[col(D, 0), col(1536, 0), col(512, 0), col(512, 0), col(512, 3), col(512, 3), col(512, 4), col(512, 5), col(3072, 1),
                  full((256, D)), full((1, 3072)), full((1, 128)), full((3, 512, D)), full((3, D, 512)), full((D, D))],
        out_specs=(col(3072, 0), col(4608, 0), col(512, 0), pl.BlockSpec((8, ts), lambda i: (0, i)), col(512, 0), col(512, 0),
                   full((256, D)), full((8, 3072)), full((8, 128))),
        compiler_params=_cp(("arbitrary",), VMEM_LIMIT), name=name)(
            dout, ycat, oa, ob, zb, zf, zf, zf, zf, kv, b_merge, gdn_g, w_branch, w_branch_t, w_out_t)


def _mesh_pos():
    return lax.axis_index("x"), lax.axis_index("y"), lax.axis_index("c")


def _all_gather(xs, name):
    n = len(xs)

    def body(*refs):
        x_refs, out_refs = refs[:n], refs[n:2 * n]
        send_sems, recv_sems, local_sems = refs[2 * n:]
        mx, my, mc = _mesh_pos()
        me, sibling = (mx, my, mc), (mx, my, 1 - mc)
        chips = [(1 - mx, my), (mx, 1 - my), (1 - mx, 1 - my)]

        def copy(a, k, block, to, src=None):
            px, py, pc = block
            slot = out_refs[a].at[4 * px + 2 * py + pc]
            return pltpu.make_async_remote_copy(
                src_ref=slot if src is None else src, dst_ref=slot,
                send_sem=send_sems.at[7 * a + k], recv_sem=recv_sems.at[7 * a + k], device_id=to, device_id_type=pl.DeviceIdType.MESH)

        mine = [pltpu.make_async_copy(x_refs[a], out_refs[a].at[4 * mx + 2 * my + mc], local_sems.at[a]) for a in range(n)]
        for cp in mine:
            cp.start()
        first = []
        for a in range(n):
            first.append(copy(a, 0, me, sibling, src=x_refs[a]))
            first += [copy(a, 1 + j, me, (*chip, mc), src=x_refs[a]) for j, chip in enumerate(chips)]
        for cp in first:
            cp.start()
        passed = []
        for j, chip in enumerate(chips):
            for a in range(n):
                copy(a, 1 + j, (*chip, mc), me).wait_recv()
                fwd = copy(a, 4 + j, (*chip, mc), sibling)
                fwd.start()
                passed.append(fwd)
        for a in range(n):
            copy(a, 0, sibling, me).wait_recv()
            for j, chip in enumerate(chips):
                copy(a, 4 + j, (*chip, 1 - mc), me).wait_recv()
        for cp in first + passed:
            cp.wait_send()
        for cp in mine:
            cp.wait()

    anyspec = pl.BlockSpec(memory_space=pl.ANY)
    return pl.pallas_call(
        body, out_shape=tuple(jax.ShapeDtypeStruct((N_DEV,) + x.shape, x.dtype) for x in xs),
        in_specs=[anyspec] * n, out_specs=tuple([anyspec] * n),
        scratch_shapes=[pltpu.SemaphoreType.DMA((7 * n,)), pltpu.SemaphoreType.DMA((7 * n,)), pltpu.SemaphoreType.DMA((n,))],
        name=name)(*xs)


def _exchange_copies(s_refs, r_refs, send_sems, recv_sems, local_sems):
    n = len(s_refs)
    mx, my, mc = _mesh_pos()
    me_id = 4 * mx + 2 * my + mc
    copies = [pltpu.make_async_copy(s_refs[a].at[me_id], r_refs[a].at[me_id], local_sems.at[a]) for a in range(n)]
    for k in range(1, N_DEV):
        px = 1 - mx if k & 4 else mx
        py = 1 - my if k & 2 else my
        pc = 1 - mc if k & 1 else mc
        for a in range(n):
            copies.append(pltpu.make_async_remote_copy(
                src_ref=s_refs[a].at[4 * px + 2 * py + pc], dst_ref=r_refs[a].at[me_id],
                send_sem=send_sems.at[7 * a + k - 1], recv_sem=recv_sems.at[7 * a + k - 1],
                device_id=(px, py, pc), device_id_type=pl.DeviceIdType.MESH))
    return copies


def _exchange_scratch(n):
    return [pltpu.SemaphoreType.DMA((7 * n,)), pltpu.SemaphoreType.DMA((7 * n,)), pltpu.SemaphoreType.DMA((n,))]


def _exchange(sends, name):
    n = len(sends)

    def body(*refs):
        copies = _exchange_copies(refs[:n], refs[n:2 * n], *refs[2 * n:])
        for cp in copies:
            cp.start()
        for cp in copies:
            cp.wait()

    anyspec = pl.BlockSpec(memory_space=pl.ANY)
    return pl.pallas_call(
        body, out_shape=tuple(jax.ShapeDtypeStruct(s.shape, s.dtype) for s in sends),
        in_specs=[anyspec] * n, out_specs=tuple([anyspec] * n), scratch_shapes=_exchange_scratch(n), name=name)(*sends)


ADAMW_BLOCK_BYTES = 4 * 1024 * 1024


def _adamw(parts, w, m, v, name):
    _, R, C = parts.shape
    tr = R
    for t in (1024, 512, 256, 128, 64, 32, 16, 8):
        if R % t == 0 and N_DEV * t * C * 4 <= ADAMW_BLOCK_BYTES:
            tr = t
            break

    def body(p_ref, w_ref, m_ref, v_ref, g_ref, d_ref, nm_ref, nv_ref):
        g = p_ref[0].astype(f32)
        for j in range(1, N_DEV):
            g = g + p_ref[j].astype(f32)
        mn = ADAM_B1 * m_ref[...] + (1.0 - ADAM_B1) * g
        vn = ADAM_B2 * v_ref[...] + (1.0 - ADAM_B2) * jnp.square(g)
        m_hat = mn / (1.0 - ADAM_B1 ** ADAM_STEP)
        v_hat = vn / (1.0 - ADAM_B2 ** ADAM_STEP)
        g_ref[...] = g
        d_ref[...] = -ADAM_LR * (m_hat / (jnp.sqrt(v_hat) + ADAM_EPS) + ADAM_WD * w_ref[...])
        nm_ref[...] = mn
        nv_ref[...] = vn

    t2 = pl.BlockSpec((tr, C), lambda i: (i, 0))
    out = jax.ShapeDtypeStruct((R, C), f32)
    return pl.pallas_call(
        body, grid=(R // tr,), out_shape=(out, out, out, out),
        in_specs=[pl.BlockSpec((N_DEV, tr, C), lambda i: (0, i, 0)), t2, t2, t2], out_specs=(t2, t2, t2, t2),
        compiler_params=_cp(("parallel",), VMEM_LIMIT), name=name)(parts, w, m, v)


def _as2d(a):
    return a.reshape(-1, a.shape[-1])


def _perm_cols(w, order=_ORDER):
    parts = [w[..., _COLS[n][0]:_COLS[n][1]] for n in order]
    pad = jnp.zeros(w.shape[:-1] + (N_ALL - N_IN,), w.dtype)
    return jnp.concatenate(parts + [pad], axis=-1)


def _unperm_cols(w, order=_ORDER):
    pieces, off = {}, 0
    for n in order:
        width = _COLS[n][1] - _COLS[n][0]
        pieces[n] = w[..., off:off + width]
        off += width
    return jnp.concatenate([pieces[n] for n in sorted(_COLS, key=lambda n: _COLS[n][0])], axis=-1)


_SMALL_ROWS = 16


def _pack_small(t):
    z = jnp.zeros((D,), f32)
    misc = z.at[0:16].set(t["b_fg"].reshape(-1)).at[16:24].set(t["a_log"].reshape(-1)).at[24:32].set(t["dt_bias"].reshape(-1))
    misc = misc.at[128:384].set(t["gdn_norm_g"].reshape(-1))
    if "extra" in t:
        misc = misc.at[512].set(t["extra"])
    rows = [t["norm_g"], t["b_merge"].reshape(6, D), t["mem_norm_g"], t["final_norm_g"][None], misc[None],
            jnp.zeros((_SMALL_ROWS - 12, D), f32)]
    return jnp.concatenate(rows, axis=0)


def _unpack_small(a):
    misc = a[11]
    return dict(norm_g=a[0:2], b_merge=a[2:8].reshape(2, 3072), mem_norm_g=a[8:10], final_norm_g=a[10],
                b_fg=misc[0:16].reshape(2, 8), a_log=misc[16:24].reshape(2, 4), dt_bias=misc[24:32].reshape(2, 4),
                gdn_norm_g=misc[128:384].reshape(2, 128), extra=misc[512])


def _layer_fwd(l, x, mem, p):
    sfx = f"_l{l}"
    h, ht = _norm_fwd(x, p["norm_g"], "norm_fwd" + sfx, with_t=True)
    zb = _mm(h, p["w_b"], bf16, 1024, 1024, 1024, "inproj_b" + sfx)
    zf = _mm(h, p["w_f"], f32, 1024, 1024, 1024, "inproj_f" + sfx)
    zs = _mm(h, p["w_s"], f32, 512, 128, 1024, "inproj_s" + sfx)
    sm = _small_prep(zs, p["par"], "small_prep" + sfx)
    S = x.shape[0]
    gt4 = jnp.transpose(sm[:, 8:12]).reshape(4, S // CH, 1, CH)
    qa, ka, st = _fox_prep(zb, sm, "fox_prep" + sfx)
    tab = _fox_bound_table(st, S, min(S, FOX_TILE))
    oa, lse_t = _fox_fwd(qa, ka, zb, tab, "fox_fwd" + sfx)
    gqkv = _gdn_prep(zf, p["conv_w"], "gdn_prep" + sfx)
    ob, tinv, states = _gdn_fwd(gqkv, sm, gt4, "gdn_fwd" + sfx)
    memn = _norm_fwd(mem, p["mem_norm_g"], "mem_norm" + sfx)
    kv = _mm(memn, p["w_mem_kv"], bf16, 256, 1024, 1024, "mem_kv" + sfx)
    xo, ycat, merged = _merge_fwd(x, oa, ob, zb, zf, kv, p["b_merge"], p["gdn_norm_g"], p["w_branch"], p["w_out"], "merge_fwd" + sfx)
    saved = dict(x=x, ht=ht, zb=zb, zf=zf, zs=zs, sm=sm, qa=qa, ka=ka, tab=tab, gt4=gt4, oa=oa, lse_t=lse_t, gqkv=gqkv, ob=ob, tinv=tinv,
                 states=states, memn=memn, kv=kv, ycat=ycat, merged=merged)
    return xo, saved


def _layer_bwd(l, dout, mem, p, s, comm_sends=()):
    sfx = f"_l{l}"
    dproj, dzf2, dmq, delta, doab, dob, dkv, dbm, dgg = _merge_bwd(
        dout, s["ycat"], s["oa"], s["ob"], s["zb"], s["zf"], s["kv"], p["b_merge"], p["gdn_norm_g"],
        p["w_branch"], p["w_branch_t"], p["w_out_t"], "merge_bwd" + sfx)
    g = {}
    g["w_out"] = _mm(s["merged"], dout, f32, 512, 1024, 512, "dw_out" + sfx, trans_a=True)
    g["w_branch"] = jnp.stack([
        _mm(s["ycat"], dproj, f32, 512, 1024, 512, f"dw_branch{n}" + sfx, trans_a=True, a_cols=(n * 512, 512), b_cols=(n * D, D))
        for n in range(3)])
    g["b_merge"] = dbm[0]
    g["gdn_norm_g"] = dgg[0]
    g["w_mem_kv"] = _mm(s["memn"], dkv, f32, 512, 1024, 256, "dw_mem_kv" + sfx, trans_a=True)
    dmemn = _mm(dkv, p["w_mem_kv_t"], f32, 256, 1024, 1024, "dmem_n" + sfx)
    g["mem_norm_g"] = _norm_bwd(mem, p["mem_norm_g"], dmemn, None, "mem_norm_bwd" + sfx)[0]
    dgqkv, dsm = _gdn_bwd(s["gqkv"], s["sm"], s["gt4"], s["tinv"], s["states"], dob, "gdn_bwd" + sfx)
    dbqkv, dcw = _gdn_prep_bwd(s["zf"], p["conv_w"], dgqkv, "gdn_prep_bwd" + sfx)
    g["conv_w"] = dcw[0:4]
    (dq, dk, dv, dfc, dfr), received = _fox_bwd(s["qa"], s["ka"], s["zb"], doab, _head_rows(s["lse_t"]), delta, s["tab"],
                                                "fox_bwd" + sfx, comm_sends=comm_sends)
    dzs, sacc = _small_bwd(s["zs"], p["par"], _head_rows(dfr), dfc, dsm, "small_bwd" + sfx)
    g["b_fg"], g["a_log"], g["dt_bias"] = sacc[0, 0:8], sacc[1, 8:12], sacc[2, 8:12]
    dz = lax.dynamic_update_slice(dzf2, jnp.concatenate([dq, dk, dv, dmq, dbqkv, dzs], axis=1), (0, DZ_MERGE_COLS))
    dh = _mm(dz, p["w_all_t"], f32, 1024, 1024, 1664, "dh" + sfx)
    g["w_in"] = _mm(s["ht"], dz, f32, 1024, 1664, 1024, "dw_in" + sfx)
    dx, dng = _norm_bwd(s["x"], p["norm_g"], dh, dout, "norm_bwd" + sfx)
    g["norm_g"] = dng[0]
    return dx, g, received


def kernel(x, mem, norm_g, w_in, b_fg, b_merge, conv_w, a_log, dt_bias, gdn_norm_g, mem_norm_g, w_mem_kv, w_branch, w_out, final_norm_g, loss_target, m_norm_g, m_w_in, m_b_fg, m_b_merge, m_conv_w, m_a_log, m_dt_bias, m_gdn_norm_g, m_mem_norm_g, m_w_mem_kv, m_w_branch, m_w_out, m_final_norm_g, v_norm_g, v_w_in, v_b_fg, v_b_merge, v_conv_w, v_a_log, v_dt_bias, v_gdn_norm_g, v_mem_norm_g, v_w_mem_kv, v_w_branch, v_w_out, v_final_norm_g):
    x0, mem0, tgt = x[0], mem[0], loss_target[0]
    shard_w = dict(w_in=w_in, w_mem_kv=w_mem_kv, w_branch=w_branch, w_out=w_out, conv_w=conv_w)
    shard_m = dict(w_in=m_w_in, w_mem_kv=m_w_mem_kv, w_branch=m_w_branch, w_out=m_w_out, conv_w=m_conv_w)
    shard_v = dict(w_in=v_w_in, w_mem_kv=v_w_mem_kv, w_branch=v_w_branch, w_out=v_w_out, conv_w=v_conv_w)
    small_w = dict(norm_g=norm_g, b_fg=b_fg, b_merge=b_merge, a_log=a_log, dt_bias=dt_bias, gdn_norm_g=gdn_norm_g,
                   mem_norm_g=mem_norm_g, final_norm_g=final_norm_g)
    small_m = dict(norm_g=m_norm_g, b_fg=m_b_fg, b_merge=m_b_merge, a_log=m_a_log, dt_bias=m_dt_bias, gdn_norm_g=m_gdn_norm_g,
                   mem_norm_g=m_mem_norm_g, final_norm_g=m_final_norm_g)
    small_v = dict(norm_g=v_norm_g, b_fg=v_b_fg, b_merge=v_b_merge, a_log=v_a_log, dt_bias=v_dt_bias, gdn_norm_g=v_gdn_norm_g,
                   mem_norm_g=v_mem_norm_g, final_norm_g=v_final_norm_g)

    g_in, g_kv, g_br, g_out, conv_all = _all_gather(
        [_b(_as2d(w_in)), _b(_as2d(w_mem_kv)), _b(_as2d(w_branch)), _b(_as2d(w_out)), _as2d(conv_w)], "gather_weights")
    conv_full = jnp.transpose(conv_all.reshape(N_DEV, DEPTH, 4, 192), (1, 2, 0, 3)).reshape(DEPTH, 4, 1536)
    w_in_full = jnp.transpose(g_in.reshape(N_DEV, DEPTH, D, 1026), (1, 2, 0, 3)).reshape(DEPTH, D, N_IN)
    w_all = _perm_cols(w_in_full)
    w_bwd = _perm_cols(w_in_full, _ORDER_BWD)
    w_kv_full = jnp.transpose(g_kv.reshape(N_DEV, DEPTH, 128, D), (1, 0, 2, 3)).reshape(DEPTH, D, D)
    w_br_full = jnp.transpose(g_br.reshape(N_DEV, DEPTH, 3, 512, 128), (1, 2, 3, 0, 4)).reshape(DEPTH, 3, 512, D)
    w_out_full = jnp.transpose(g_out.reshape(N_DEV, DEPTH, 128, D), (1, 0, 2, 3)).reshape(DEPTH, D, D)

    layers = []
    for l in range(DEPTH):
        layers.append(dict(
            norm_g=norm_g[l][None], mem_norm_g=mem_norm_g[l][None], gdn_norm_g=gdn_norm_g[l][None], b_merge=b_merge[l][None],
            par=_small_pars(b_fg[l], a_log[l], dt_bias[l]),
            conv_w=jnp.pad(conv_full[l], ((0, 4), (0, 0))),
            w_b=w_all[l][:, 0:NB], w_f=w_all[l][:, NB:NB + NF], w_s=w_all[l][:, NB + NF:], w_all_t=jnp.transpose(w_bwd[l]),
            w_mem_kv=w_kv_full[l], w_mem_kv_t=jnp.transpose(w_kv_full[l]),
            w_branch=w_br_full[l], w_branch_t=jnp.transpose(w_br_full[l], (0, 2, 1)),
            w_out=w_out_full[l], w_out_t=jnp.transpose(w_out_full[l])))

    acts, saved = x0, []
    for l in range(DEPTH):
        acts, s = _layer_fwd(l, acts, mem0, layers[l])
        saved.append(s)
    dx, dfg, lsum = _loss_head(acts, final_norm_g[None], tgt, "loss_head")

    def send_buffers(g):
        dw_in = _unperm_cols(g["w_in"], _ORDER_BWD)
        send = dict(
            w_in=jnp.transpose(dw_in.reshape(D, N_DEV, 1026), (1, 0, 2)),
            w_mem_kv=g["w_mem_kv"].reshape(N_DEV, 128, D),
            w_branch=jnp.transpose(g["w_branch"].reshape(3, 512, N_DEV, 128), (2, 0, 1, 3)).reshape(N_DEV, 3 * 512, 128),
            w_out=g["w_out"].reshape(N_DEV, 128, D),
            conv_w=jnp.transpose(g["conv_w"].reshape(4, N_DEV, 192), (1, 0, 2)))
        return [_b(send[n]) for n in _SHARDED]

    grads, parts = [None] * DEPTH, [None] * DEPTH
    dx, grads[1], _ = _layer_bwd(1, dx, mem0, layers[1], saved[1])
    dx, grads[0], parts[1] = _layer_bwd(0, dx, mem0, layers[0], saved[0], comm_sends=send_buffers(grads[1]))
    parts[0] = _exchange(send_buffers(grads[0]), "scatter_grads")
    grad_x = dx[None]

    big = [{}, {}, {}, {}]
    for a, n in enumerate(_SHARDED):
        res = [_adamw(parts[l][a], _as2d(shard_w[n][l]), _as2d(shard_m[n][l]), _as2d(shard_v[n][l]), f"adamw_{n}_l{l}")
               for l in range(DEPTH)]
        for kind in range(4):
            big[kind][n] = jnp.stack([res[l][kind] for l in range(DEPTH)]).reshape(shard_w[n].shape)

    small_g = {k: jnp.stack([grads[l][k] for l in range(DEPTH)]) for k in ("norm_g", "b_fg", "b_merge", "a_log", "dt_bias", "gdn_norm_g", "mem_norm_g")}
    small_g["final_norm_g"] = dfg[0]
    small_g["extra"] = lsum[0, 0]
    parts_s, = _all_gather([_pack_small(small_g)], "gather_small")
    g_sm, d_sm, m_sm, v_sm = _adamw(parts_s, _pack_small(small_w), _pack_small(small_m), _pack_small(small_v), "adamw_replicated")

    sml = [_unpack_small(a) for a in (g_sm, d_sm, m_sm, v_sm)]
    loss = sml[0]["extra"]
    names = ("norm_g", "w_in", "b_fg", "b_merge", "conv_w", "a_log", "dt_bias", "gdn_norm_g", "mem_norm_g", "w_mem_kv", "w_branch", "w_out", "final_norm_g")
    outs = [loss, grad_x]
    for kind in range(4):
        for n in names:
            outs.append(big[kind][n] if n in big[kind] else sml[kind][n])
    return tuple(outs)
```

```python
import functools

import jax
import jax.numpy as jnp
from jax import lax
from jax.experimental import pallas as pl
from jax.experimental.pallas import tpu as pltpu

f32, bf16 = jnp.float32, jnp.bfloat16

D = 1024
EPS = 1e-6
CH = 64
N_DEV = 8
DEPTH = 2
FOX_SCALE = 64 ** -0.5
GDN_SCALE = 128 ** -0.5
MEM_SCALE = 128 ** -0.5
NEG = -1e30
VMEM_LIMIT = 56 * 1024 * 1024

ADAM_LR, ADAM_B1, ADAM_B2, ADAM_EPS, ADAM_WD, ADAM_STEP = 0.001, 0.9, 0.999, 1e-08, 0.01, 10

_COLS = dict(aq=(0, 512), ak=(512, 1024), av=(1024, 1536), af=(1536, 1544), az=(1544, 2056),
             bq=(2056, 2568), bk=(2568, 3080), bv=(3080, 3592), ba=(3592, 3596), bb=(3596, 3600),
             bz=(3600, 4112), mq=(4112, 4624), mz=(4624, 5136), gates=(5136, 8208))
_ORDER = ("aq", "ak", "av", "mq", "bq", "bk", "bv", "az", "bz", "mz", "gates", "af", "ba", "bb")
_ORDER_BWD = ("az", "bz", "mz", "gates", "aq", "ak", "av", "mq", "bq", "bk", "bv", "af", "ba", "bb")
DZ_MERGE_COLS = 4608
N_IN = 8208
NB, NF, NS = 2048, 6144, 128
N_ALL = NB + NF + NS

_SHARDED = ("w_in", "w_mem_kv", "w_branch", "w_out", "conv_w")


def _cp(sem=None, vmem=None):
    kw = {}
    if sem is not None:
        kw["dimension_semantics"] = sem
    if vmem is not None:
        kw["vmem_limit_bytes"] = vmem
    return pltpu.CompilerParams(**kw)


def _dot(a, b):
    return jnp.dot(a, b, preferred_element_type=f32)


def _dot_nt(a, b):
    return lax.dot_general(a, b, (((1,), (1,)), ((), ())), preferred_element_type=f32)


def _dot_tn(a, b):
    return lax.dot_general(a, b, (((0,), (0,)), ((), ())), preferred_element_type=f32)


def _split2(x):
    hi = x.astype(bf16)
    return hi, (x - hi.astype(f32)).astype(bf16)


def _mm3(a, b, dims):
    ah, al = _split2(a)
    bh, bl = _split2(b)
    dg = functools.partial(lax.dot_general, dimension_numbers=dims, preferred_element_type=f32)
    return dg(ah, bh) + (dg(ah, bl) + dg(al, bh))


def _hi(a, b):
    return _mm3(a, b, (((1,), (0,)), ((), ())))


def _hi_nt(a, b):
    return _mm3(a, b, (((1,), (1,)), ((), ())))


def _hi_tn(a, b):
    return _mm3(a, b, (((0,), (0,)), ((), ())))


def _hi_b(a, b):
    return _mm3(a, b, (((2,), (1,)), ((0,), (0,))))


def _b(x):
    return x.astype(bf16)


def _sig(x):
    return jax.nn.sigmoid(x)


def _silu(x):
    return x * _sig(x)


def _dsilu(x):
    s = _sig(x)
    return s * (1.0 + x * (1.0 - s))


def _softplus(x):
    return jnp.maximum(x, 0.0) + jnp.log1p(jnp.exp(-jnp.abs(x)))


def _rowsum(x):
    return jnp.sum(x, axis=1, keepdims=True)


def _colsum(x):
    return jnp.sum(x, axis=0, keepdims=True)


def _norm_fwd(x, g, name, with_t=False):
    M = x.shape[0]
    ts = min(M, 512)

    def body(x_ref, g_ref, h_ref, *t_ref):
        xv = x_ref[...]
        r = lax.rsqrt(jnp.mean(xv * xv, axis=-1, keepdims=True) + EPS)
        h = xv * r * g_ref[...]
        h_ref[...] = _b(h)
        if with_t:
            t_ref[0][...] = _b(jnp.transpose(h))

    tile = pl.BlockSpec((ts, D), lambda i: (i, 0))
    shapes, specs = jax.ShapeDtypeStruct((M, D), bf16), tile
    if with_t:
        shapes, specs = (shapes, jax.ShapeDtypeStruct((D, M), bf16)), (tile, pl.BlockSpec((D, ts), lambda i: (0, i)))
    return pl.pallas_call(
        body, grid=(M // ts,), out_shape=shapes,
        in_specs=[tile, pl.BlockSpec((1, D), lambda i: (0, 0))],
        out_specs=specs, compiler_params=_cp(("parallel",)), name=name)(x, g)


def _norm_bwd(x, g, dh, dres, name):
    M = x.shape[0]
    ts = min(M, 512)
    with_dx = dres is not None

    def body(*refs):
        if with_dx:
            x_ref, g_ref, dh_ref, dres_ref, dx_ref, dg_ref = refs
        else:
            x_ref, g_ref, dh_ref, dg_ref = refs
        i = pl.program_id(0)
        xv = x_ref[...]
        r = lax.rsqrt(jnp.mean(xv * xv, axis=-1, keepdims=True) + EPS)
        xh = xv * r
        dh = dh_ref[...].astype(f32)
        part = jnp.broadcast_to(_colsum(dh * xh), (8, D))

        @pl.when(i == 0)
        def _():
            dg_ref[...] = part

        @pl.when(i > 0)
        def _():
            dg_ref[...] += part

        if with_dx:
            dxh = dh * g_ref[...]
            dx_ref[...] = dres_ref[...] + r * (dxh - xh * jnp.mean(dxh * xh, axis=-1, keepdims=True))

    tile = pl.BlockSpec((ts, D), lambda i: (i, 0))
    gspec = pl.BlockSpec((1, D), lambda i: (0, 0))
    acc = pl.BlockSpec((8, D), lambda i: (0, 0))
    if with_dx:
        return pl.pallas_call(
            body, grid=(M // ts,), out_shape=(jax.ShapeDtypeStruct((M, D), f32), jax.ShapeDtypeStruct((8, D), f32)),
            in_specs=[tile, gspec, tile, tile], out_specs=(tile, acc), compiler_params=_cp(("arbitrary",)), name=name)(x, g, dh, dres)
    return pl.pallas_call(
        body, grid=(M // ts,), out_shape=jax.ShapeDtypeStruct((8, D), f32),
        in_specs=[tile, gspec, tile], out_specs=acc, compiler_params=_cp(("arbitrary",)), name=name)(x, g, dh)


def _loss_head(x, g, tgt, name):
    M = x.shape[0]
    ts = min(M, 512)

    def body(x_ref, g_ref, t_ref, dx_ref, dg_ref, ls_ref):
        i = pl.program_id(0)
        xv = x_ref[...]
        gv = g_ref[...]
        r = lax.rsqrt(jnp.mean(xv * xv, axis=-1, keepdims=True) + EPS)
        xh = xv * r
        e = xh * gv - t_ref[...]
        lpart = 0.5 * jnp.sum(jnp.mean(e * e, axis=-1, keepdims=True), axis=0, keepdims=True)
        dy = e * (1.0 / D)
        dgp = jnp.broadcast_to(_colsum(dy * xh), (8, D))
        lp = jnp.broadcast_to(lpart, (8, 128))

        @pl.when(i == 0)
        def _():
            dg_ref[...] = dgp
            ls_ref[...] = lp

        @pl.when(i > 0)
        def _():
            dg_ref[...] += dgp
            ls_ref[...] += lp

        dxh = dy * gv
        dx_ref[...] = r * (dxh - xh * jnp.mean(dxh * xh, axis=-1, keepdims=True))

    tile = pl.BlockSpec((ts, D), lambda i: (i, 0))
    return pl.pallas_call(
        body, grid=(M // ts,),
        out_shape=(jax.ShapeDtypeStruct((M, D), f32), jax.ShapeDtypeStruct((8, D), f32), jax.ShapeDtypeStruct((8, 128), f32)),
        in_specs=[tile, pl.BlockSpec((1, D), lambda i: (0, 0)), tile],
        out_specs=(tile, pl.BlockSpec((8, D), lambda i: (0, 0)), pl.BlockSpec((8, 128), lambda i: (0, 0))),
        compiler_params=_cp(("arbitrary",)), name=name)(x, g, tgt)


def _mm(a, b, out_dtype, tm, tn, tk, name, trans_a=False, a_cols=None, b_cols=None, comm=None):
    if trans_a:
        K, M = a.shape
    else:
        M, K = a.shape
    N = b.shape[1]
    a0, b0 = 0, 0
    if a_cols is not None:
        a0, M = a_cols
    if b_cols is not None:
        b0, N = b_cols
    tm, tn, tk = min(tm, M), min(tn, N), min(tk, K)
    nk = K // tk
    a0, b0 = a0 // tm, b0 // tn
    grid = (M // tm, N // tn, nk)
    kind, carr = comm if comm is not None else (None, ())
    nc = len(carr)

    def body(*refs):
        a_ref, b_ref = refs[:2]
        o_ref = refs[2 + nc]
        acc_ref = refs[3 + 2 * nc]
        k = pl.program_id(2)
        if nc:
            cs_refs, cr_refs, sems = refs[2:2 + nc], refs[3 + nc:3 + 2 * nc], refs[4 + 2 * nc:]
            step = (pl.program_id(0) * grid[1] + pl.program_id(1)) * nk + k

            @pl.when(step == 0)
            def _():
                if kind == "gather":
                    _Gather(cs_refs, cr_refs, *sems).start()
                else:
                    for cp in _exchange_copies(cs_refs, cr_refs, *sems):
                        cp.start()
        av, bv = _b(a_ref[...]), _b(b_ref[...])
        part = _dot_tn(av, bv) if trans_a else _dot(av, bv)
        if nk == 1:
            o_ref[...] = part.astype(out_dtype)
        else:
            @pl.when(k == 0)
            def _():
                acc_ref[...] = part

            @pl.when(k > 0)
            def _():
                acc_ref[...] += part

            @pl.when(k == nk - 1)
            def _():
                o_ref[...] = acc_ref[...].astype(out_dtype)
        if nc:
            @pl.when(step == grid[0] * grid[1] * nk - 1)
            def _():
                if kind == "gather":
                    _Gather(cs_refs, cr_refs, *sems).finish()
                else:
                    for cp in _exchange_copies(cs_refs, cr_refs, *sems):
                        cp.wait()

    a_spec = pl.BlockSpec((tk, tm), lambda i, j, k: (k, i + a0)) if trans_a else pl.BlockSpec((tm, tk), lambda i, j, k: (i, k))
    anyspec = pl.BlockSpec(memory_space=pl.ANY)
    recv_shapes = tuple(jax.ShapeDtypeStruct(((N_DEV,) + c.shape) if kind == "gather" else c.shape, c.dtype) for c in carr)
    out = pl.pallas_call(
        body, grid=grid, out_shape=(jax.ShapeDtypeStruct((M, N), out_dtype),) + recv_shapes,
        in_specs=[a_spec, pl.BlockSpec((tk, tn), lambda i, j, k: (k, j + b0))] + [anyspec] * nc,
        out_specs=(pl.BlockSpec((tm, tn), lambda i, j, k: (i, j)),) + tuple([anyspec] * nc),
        scratch_shapes=[pltpu.VMEM((tm, tn), f32)] + (_exchange_scratch(nc) if nc else []),
        compiler_params=_cp(("arbitrary",) * 3 if nc else ("parallel", "parallel", "arbitrary"), VMEM_LIMIT), name=name)(a, b, *carr)
    return (out[0], list(out[1:])) if nc else out[0]


def _small_pars(b_fg, a_log, dt_bias):
    par = jnp.zeros((8, 128), f32)
    par = par.at[0, 0:8].set(b_fg).at[1, 8:12].set(a_log).at[2, 8:12].set(dt_bias)
    return par


def _small_prep(zs, par, name):
    S = zs.shape[0]
    ts = min(S, 512)

    def body(z_ref, par_ref, o_ref, carry_ref):
        i = pl.program_id(0)

        @pl.when(i == 0)
        def _():
            carry_ref[...] = jnp.zeros_like(carry_ref)

        z = z_ref[...]
        lane = lax.broadcasted_iota(jnp.int32, (ts, 128), 1)
        row = lax.broadcasted_iota(jnp.int32, (ts, 128), 0)
        za = z + par_ref[0:1, :]
        logf = jnp.minimum(za, 0.0) - jnp.log1p(jnp.exp(-jnp.abs(za)))
        glog = -jnp.exp(par_ref[1:2, :]) * _softplus(z + par_ref[2:3, :])
        x = jnp.where(lane < 8, logf, jnp.where(lane < 12, glog, 0.0))
        pos = jnp.where(lane < 8, row, row & (CH - 1))
        s = 1
        while s < ts:
            x = x + jnp.where(pos >= s, pltpu.roll(x, s, 0), 0.0)
            s *= 2
        tot = x + carry_ref[0:1, :]
        carry_ref[...] = jnp.broadcast_to(jnp.where(lane[0:1] < 8, tot[ts - 1:ts, :], 0.0), (8, 128))
        o_ref[...] = jnp.where(lane < 8, tot, jnp.where(lane < 12, x, jnp.where(lane < 16, _sig(z), 0.0)))

    return pl.pallas_call(
        body, grid=(S // ts,), out_shape=jax.ShapeDtypeStruct((S, 128), f32),
        in_specs=[pl.BlockSpec((ts, 128), lambda i: (i, 0)), pl.BlockSpec((8, 128), lambda i: (0, 0))],
        out_specs=pl.BlockSpec((ts, 128), lambda i: (i, 0)), scratch_shapes=[pltpu.VMEM((8, 128), f32)],
        compiler_params=_cp(("arbitrary",)), name=name)(zs, par)


def _small_bwd(zs, par, dfr, dfc, dsm, name):
    S = zs.shape[0]
    ts = min(S, 512)
    nt = S // ts

    def body(z_ref, par_ref, dfr_ref, dfc_ref, dsm_ref, dz_ref, acc_ref, carry_ref):
        i = pl.program_id(0)

        @pl.when(i == 0)
        def _():
            carry_ref[...] = jnp.zeros_like(carry_ref)

        z = z_ref[...]
        dsm_v = dsm_ref[...]
        lane = lax.broadcasted_iota(jnp.int32, (ts, 128), 1)
        row = lax.broadcasted_iota(jnp.int32, (ts, 128), 0)
        df = jnp.transpose(jnp.concatenate([dfr_ref[...], jnp.zeros((120, ts), f32)], axis=0))
        for p in range(4):
            dpair = dfc_ref[p]
            df = df - jnp.where(lane == 2 * p, dpair[:, 0:1], jnp.where(lane == 2 * p + 1, dpair[:, 64:65], 0.0))
        x = jnp.where(lane < 8, df, jnp.where(lane < 12, dsm_v, 0.0))
        pos = jnp.where(lane < 8, row, row & (CH - 1))
        seg = jnp.where(lane < 8, ts, CH)
        s = 1
        while s < ts:
            x = x + jnp.where(pos + s < seg, pltpu.roll(x, ts - s, 0), 0.0)
            s *= 2
        tot = x + carry_ref[0:1, :]
        carry_ref[...] = jnp.broadcast_to(jnp.where(lane[0:1] < 8, tot[0:1, :], 0.0), (8, 128))
        za = z + par_ref[0:1, :]
        daf = tot * _sig(-za)
        zb = z + par_ref[2:3, :]
        nea = -jnp.exp(par_ref[1:2, :])
        glog = nea * _softplus(zb)
        dba = x * nea * _sig(zb)
        beta = _sig(z)
        dbb = dsm_v * beta * (1.0 - beta)
        dz_ref[...] = _b(jnp.where(lane < 8, daf, jnp.where(lane < 12, dba, jnp.where(lane < 16, dbb, 0.0))))
        r0 = _colsum(jnp.where(lane < 8, daf, 0.0))
        r1 = _colsum(jnp.where((lane >= 8) & (lane < 12), x * glog, 0.0))
        r2 = _colsum(jnp.where((lane >= 8) & (lane < 12), dba, 0.0))
        r8 = lax.broadcasted_iota(jnp.int32, (8, 128), 0)
        part = jnp.where(r8 == 0, r0, jnp.where(r8 == 1, r1, jnp.where(r8 == 2, r2, 0.0)))

        @pl.when(i == 0)
        def _():
            acc_ref[...] = part

        @pl.when(i > 0)
        def _():
            acc_ref[...] += part

    rev = pl.BlockSpec((ts, 128), lambda i: (nt - 1 - i, 0))
    rev4 = pl.BlockSpec((4, ts, 128), lambda i: (0, nt - 1 - i, 0))
    c8 = pl.BlockSpec((8, 128), lambda i: (0, 0))
    return pl.pallas_call(
        body, grid=(nt,), out_shape=(jax.ShapeDtypeStruct((S, 128), bf16), jax.ShapeDtypeStruct((8, 128), f32)),
        in_specs=[rev, c8, pl.BlockSpec((8, ts), lambda i: (0, nt - 1 - i)), rev4, rev], out_specs=(rev, c8),
        scratch_shapes=[pltpu.VMEM((8, 128), f32)],
        compiler_params=_cp(("arbitrary",)), name=name)(zs, par, dfr, dfc, dsm)


def _split3(x):
    hi = _b(x).astype(f32)
    r = x - hi
    mid = _b(r).astype(f32)
    return hi, mid, _b(r - mid).astype(f32)


FOX_PREP_ROWS = 512
FOX_TILE = 512
FOX_SKIP_LOG = -32.0


def _fox_prep(zb, sm, name):
    S = zb.shape[0]
    ts = min(S, FOX_PREP_ROWS)

    def body(q_ref, k_ref, f_ref, qa_ref, ka_ref, st_ref):
        lane = lax.broadcasted_iota(jnp.int32, (ts, 128), 1)
        lane8 = lax.broadcasted_iota(jnp.int32, (8, 128), 1)
        f = f_ref[...]
        st = jnp.zeros((8, 128), f32)
        for p in range(4):
            q = q_ref[:, p * 128:(p + 1) * 128].astype(f32) * FOX_SCALE
            k = k_ref[:, p * 128:(p + 1) * 128].astype(f32)
            for h in (0, 1):
                fcol = f[:, 2 * p + h:2 * p + h + 1]
                hi, mid, lo = _split3(fcol)
                own = (lane < 64) if h == 0 else (lane >= 64)
                nq = jnp.sqrt(_rowsum(jnp.where(own, q * q, 0.0)))
                nk = jnp.sqrt(_rowsum(jnp.where(own, k * k, 0.0)))
                stats = (jnp.max(nq, axis=0, keepdims=True), jnp.max(nk, axis=0, keepdims=True),
                         jnp.max(fcol, axis=0, keepdims=True), jnp.min(fcol, axis=0, keepdims=True),
                         jnp.min(-nq * nk, axis=0, keepdims=True))
                for si, val in enumerate(stats):
                    st = jnp.where(lane8 == 8 * si + 2 * p + h, val, st)
                o = 64 if h == 0 else 0
                ones_lo = (lane >= o) & (lane < o + 3)
                ones_hi = (lane >= o + 3) & (lane < o + 6)
                qaug = jnp.where(lane == o, hi, jnp.where(lane == o + 1, mid, jnp.where(lane == o + 2, lo, jnp.where(ones_hi, 1.0, 0.0))))
                kaug = jnp.where(lane == o + 3, -hi, jnp.where(lane == o + 4, -mid, jnp.where(lane == o + 5, -lo, jnp.where(ones_lo, 1.0, 0.0))))
                qa_ref[2 * p + h] = _b(jnp.where(own, q, qaug))
                ka_ref[2 * p + h] = _b(jnp.where(own, k, kaug))
        st_ref[...] = st

    out = jax.ShapeDtypeStruct((8, S, 128), bf16)
    return pl.pallas_call(
        body, grid=(S // ts,), out_shape=(out, out, jax.ShapeDtypeStruct((S // ts * 8, 128), f32)),
        in_specs=[pl.BlockSpec((ts, 512), lambda i: (i, 0)), pl.BlockSpec((ts, 512), lambda i: (i, 1)), pl.BlockSpec((ts, 128), lambda i: (i, 0))],
        out_specs=(pl.BlockSpec((8, ts, 128), lambda i: (0, i, 0)), pl.BlockSpec((8, ts, 128), lambda i: (0, i, 0)),
                   pl.BlockSpec((8, 128), lambda i: (i, 0))),
        compiler_params=_cp(("parallel",)), name=name)(zb, zb, sm)


def _fox_bound_table(st, S, T):
    ts = min(S, FOX_PREP_ROWS)
    g = T // ts
    nt = S // T
    s5 = st.reshape(S // ts, 8, 128)[:, 0, 0:40].reshape(nt, g, 5, 8)
    qn, kn, fmax = s5[:, :, 0].max(axis=1), s5[:, :, 1].max(axis=1), s5[:, :, 2].max(axis=1)
    fmin, lmin = s5[:, :, 3].min(axis=1), s5[:, :, 4].min(axis=1)
    e = qn[:, None] * kn[None, :] + fmax[:, None] - fmin[None, :] - lmin[:, None] + 1.0
    return jnp.transpose(e, (2, 0, 1)).reshape(8, nt * nt)


def _pair_rows(a, T):
    at = jnp.transpose(a)
    r8 = lax.broadcasted_iota(jnp.int32, (8, T), 0)
    return jnp.where(r8 == 0, at[0:1, :], at[64:65, :])


def _fox_fwd(qa, ka, zb, tab, name):
    S = zb.shape[0]
    T = min(S, FOX_TILE)
    nt = S // T

    def body(tab_ref, qa_ref, ka_ref, v_ref, o_ref, lset_ref, m_ref, l_ref, acc_ref):
        p, i = pl.program_id(0), pl.program_id(1)
        m_ref[...] = jnp.full_like(m_ref, NEG)
        l_ref[...] = jnp.zeros_like(l_ref)
        acc_ref[...] = jnp.zeros_like(acc_ref)
        row = lax.broadcasted_iota(jnp.int32, (T, T), 0)
        col = lax.broadcasted_iota(jnp.int32, (T, T), 1)

        def head_tile(h, j, masked):
            off = pl.multiple_of(j * T, T)
            s = _dot_nt(qa_ref[h], ka_ref[h, pl.ds(off, T), :])
            if masked:
                s = jnp.where(row >= col, s, NEG)
            m_old = m_ref[h]
            m_new = jnp.maximum(m_old, jnp.max(s, axis=1, keepdims=True))
            alpha = jnp.exp(m_old - m_new)
            pr = jnp.exp(s - jnp.tile(m_new, (1, T // 128)))
            l_ref[h] = alpha * l_ref[h] + _rowsum(pr)
            acc_ref[h] = alpha * acc_ref[h] + _dot(_b(pr), v_ref[pl.ds(off, T), :])
            m_ref[h] = m_new

        def step(j, c):
            for h in (0, 1):
                @pl.when(tab_ref[2 * p + h, i * nt + j] > FOX_SKIP_LOG)
                def _():
                    head_tile(h, j, False)
            return c

        lax.fori_loop(0, i, step, 0)
        for h in (0, 1):
            head_tile(h, i, True)
        lane2 = lax.broadcasted_iota(jnp.int32, (T, 128), 1)
        o_ref[...] = jnp.where(lane2 < 64, acc_ref[0] / l_ref[0], acc_ref[1] / l_ref[1])
        lse = jnp.where(lane2 < 64, m_ref[0] + jnp.log(l_ref[0]), m_ref[1] + jnp.log(l_ref[1]))
        lset_ref[0] = _pair_rows(lse, T)

    return pl.pallas_call(
        body, grid=(4, S // T),
        out_shape=(jax.ShapeDtypeStruct((S, 512), f32), jax.ShapeDtypeStruct((4, 8, S), f32)),
        in_specs=[pl.BlockSpec(memory_space=pltpu.SMEM), pl.BlockSpec((2, T, 128), lambda p, i: (p, i, 0)),
                  pl.BlockSpec((2, S, 128), lambda p, i: (p, 0, 0), pipeline_mode=pl.Buffered(1)),
                  pl.BlockSpec((S, 128), lambda p, i: (0, 8 + p), pipeline_mode=pl.Buffered(1))],
        out_specs=(pl.BlockSpec((T, 128), lambda p, i: (i, p)), pl.BlockSpec((1, 8, T), lambda p, i: (p, 0, i))),
        scratch_shapes=[pltpu.VMEM((2, T, 128), f32), pltpu.VMEM((2, T, 128), f32), pltpu.VMEM((2, T, 128), f32)],
        compiler_params=_cp(("arbitrary", "arbitrary"), VMEM_LIMIT), name=name)(tab, qa, ka, zb)


def _fox_bwd(qa, ka, zb, dob, lse_t, dl_t, tab, name, comm_sends=()):
    S = zb.shape[0]
    T = min(S, FOX_TILE)
    nq = S // T
    nc = len(comm_sends)

    def body(*refs):
        tab_ref, ka_ref, v_ref, qa_ref, do_ref, lt_ref, dt_ref = refs[:7]
        cs_refs = refs[7:7 + nc]
        dq_ref, dk_ref, dv_ref, dfc_ref, dfr_ref = refs[7 + nc:12 + nc]
        cr_refs = refs[12 + nc:12 + 2 * nc]
        dqa_ref, dka_ref, dva_ref, fs_ref = refs[12 + 2 * nc:16 + 2 * nc]
        p, j = pl.program_id(0), pl.program_id(1)
        if nc:
            @pl.when((p == 0) & (j == 0))
            def _():
                for cp in _exchange_copies(cs_refs, cr_refs, *refs[16 + 2 * nc:]):
                    cp.start()
        lane1 = lax.broadcasted_iota(jnp.int32, (1, 128), 1)
        lane2 = lax.broadcasted_iota(jnp.int32, (T, 128), 1)
        hm = (lane1 < 64, lane1 >= 64)
        v = v_ref[...]
        vsm = [jnp.where(hm[h], v, jnp.zeros_like(v)) for h in (0, 1)]
        ksm = [jnp.where(hm[h], ka_ref[h], jnp.zeros_like(v)) for h in (0, 1)]

        @pl.when(j == 0)
        def _():
            dqa_ref[...] = jnp.zeros_like(dqa_ref)
            dfr_ref[...] = jnp.zeros_like(dfr_ref)

        dka_ref[...] = jnp.zeros_like(dka_ref)
        dva_ref[...] = jnp.zeros_like(dva_ref)
        fs_ref[...] = jnp.zeros_like(fs_ref)
        row = lax.broadcasted_iota(jnp.int32, (T, T), 0)
        col = lax.broadcasted_iota(jnp.int32, (T, T), 1)

        def head_tile(h, i, masked):
            off = pl.multiple_of(i * T, T)
            dot_ = do_ref[pl.ds(off, T), :]
            hr = pl.ds(2 * p + h, 1)
            qt = qa_ref[h, pl.ds(off, T), :]
            s_t = _dot_nt(ka_ref[h], qt)
            if masked:
                s_t = jnp.where(col >= row, s_t, NEG)
            p_t = jnp.exp(s_t - lt_ref[hr, pl.ds(off, T)])
            dva_ref[h] += _dot(_b(p_t), dot_)
            dp_t = _dot_nt(vsm[h], dot_)
            ds_t = p_t * (dp_t - dt_ref[hr, pl.ds(off, T)])
            dsb = _b(ds_t)
            dka_ref[h] += _dot(dsb, qt)
            fs_ref[h] += _rowsum(ds_t)
            dfr_ref[0, pl.ds(h, 1), pl.ds(off, T)] += _colsum(ds_t)
            dqa_ref[pl.ds(off, T), :] += _dot_tn(dsb, ksm[h])

        def step(i, c):
            for h in (0, 1):
                @pl.when(tab_ref[2 * p + h, i * nq + j] > FOX_SKIP_LOG)
                def _():
                    head_tile(h, i, False)
            return c

        for h in (0, 1):
            head_tile(h, j, True)
        lax.fori_loop(j + 1, nq, step, 0)
        dk_ref[...] = _b(jnp.where(lane2 < 64, dka_ref[0], dka_ref[1]))
        dv_ref[...] = _b(jnp.where(lane2 < 64, dva_ref[0], dva_ref[1]))
        dfc_ref[0] = jnp.where(lane2 < 64, fs_ref[0], fs_ref[1])
        dq_ref[...] = _b(dqa_ref[pl.ds(pl.multiple_of(j * T, T), T), :] * FOX_SCALE)
        if nc:
            @pl.when((p == 3) & (j == nq - 1))
            def _():
                for cp in _exchange_copies(cs_refs, cr_refs, *refs[16 + 2 * nc:]):
                    cp.wait()

    one = pl.Buffered(1)
    res = pl.BlockSpec((8, S), lambda p, j: (0, 0), pipeline_mode=one)
    tk = pl.BlockSpec((T, 128), lambda p, j: (j, p))
    anyspec = pl.BlockSpec(memory_space=pl.ANY)
    outs = pl.pallas_call(
        body, grid=(4, nq),
        out_shape=(jax.ShapeDtypeStruct((S, 512), bf16), jax.ShapeDtypeStruct((S, 512), bf16), jax.ShapeDtypeStruct((S, 512), bf16),
                   jax.ShapeDtypeStruct((4, S, 128), f32), jax.ShapeDtypeStruct((4, 8, S), f32))
        + tuple(jax.ShapeDtypeStruct(c.shape, c.dtype) for c in comm_sends),
        in_specs=[pl.BlockSpec(memory_space=pltpu.SMEM),
                  pl.BlockSpec((2, T, 128), lambda p, j: (p, j, 0)), pl.BlockSpec((T, 128), lambda p, j: (j, 8 + p)),
                  pl.BlockSpec((2, S, 128), lambda p, j: (p, 0, 0), pipeline_mode=one),
                  pl.BlockSpec((S, 128), lambda p, j: (0, p), pipeline_mode=one), res, res] + [anyspec] * nc,
        out_specs=(tk, tk, tk, pl.BlockSpec((1, T, 128), lambda p, j: (p, j, 0)),
                   pl.BlockSpec((1, 8, S), lambda p, j: (p, 0, 0))) + tuple([anyspec] * nc),
        scratch_shapes=[pltpu.VMEM((S, 128), f32), pltpu.VMEM((2, T, 128), f32), pltpu.VMEM((2, T, 128), f32), pltpu.VMEM((2, T, 1), f32)]
        + (_exchange_scratch(nc) if nc else []),
        compiler_params=_cp(("arbitrary", "arbitrary"), VMEM_LIMIT), name=name)(tab, ka, zb, qa, dob, lse_t, dl_t, *comm_sends)
    return outs[:5], list(outs[5:])


def _head_rows(a):
    return a[:, 0:2, :].reshape(8, a.shape[2])


def _conv_taps(ext, x, w_ref, ts):
    y = x * w_ref[3:4, :]
    shifted = []
    for k in (1, 2, 3):
        xs = pltpu.roll(ext, k, 0)[8:]
        shifted.append(xs)
        y = y + xs * w_ref[3 - k:4 - k, :]
    return y, shifted


def _gdn_prep(zf, cw, name):
    S = zf.shape[0]
    ts = min(S, 512)

    def body(x_ref, w_ref, o_ref, tail_ref):
        i = pl.program_id(0)

        @pl.when(i == 0)
        def _():
            tail_ref[...] = jnp.zeros_like(tail_ref)

        x = x_ref[...]
        ext = jnp.concatenate([tail_ref[...], x], axis=0)
        y, _ = _conv_taps(ext, x, w_ref, ts)
        tail_ref[...] = x[ts - 8:, :]
        a = _silu(y)
        for hb in range(12):
            blk = a[:, hb * 128:(hb + 1) * 128]
            if hb < 8:
                blk = blk * lax.rsqrt(_rowsum(blk * blk) + EPS)
            if hb < 4:
                blk = blk * GDN_SCALE
            o_ref[:, hb * 128:(hb + 1) * 128] = blk

    return pl.pallas_call(
        body, grid=(S // ts,), out_shape=jax.ShapeDtypeStruct((S, 1536), f32),
        in_specs=[pl.BlockSpec((ts, 1536), lambda i: (i, 0)), pl.BlockSpec((8, 1536), lambda i: (0, 0))],
        out_specs=pl.BlockSpec((ts, 1536), lambda i: (i, 0)), scratch_shapes=[pltpu.VMEM((8, 1536), f32)],
        compiler_params=_cp(("arbitrary",), VMEM_LIMIT), name=name)(zf, cw)


def _gdn_prep_bwd(zf, cw, dg, name):
    S = zf.shape[0]
    ts = min(S, 512)
    nt = S // ts

    def body(x_ref, xp_ref, w_ref, dg_ref, dx_ref, dw_ref, head_ref):
        i = pl.program_id(0)

        @pl.when(i == 0)
        def _():
            head_ref[...] = jnp.zeros_like(head_ref)

        x = x_ref[...]
        prev = jnp.where(i == nt - 1, 0.0, xp_ref[...])
        ext = jnp.concatenate([prev, x], axis=0)
        y, shifted = _conv_taps(ext, x, w_ref, ts)
        a = _silu(y)
        das = []
        for hb in range(12):
            blk = a[:, hb * 128:(hb + 1) * 128]
            d = dg_ref[:, hb * 128:(hb + 1) * 128]
            if hb < 4:
                d = d * GDN_SCALE
            if hb < 8:
                r = lax.rsqrt(_rowsum(blk * blk) + EPS)
                n = blk * r
                d = r * (d - n * _rowsum(d * n))
            das.append(d)
        dy = jnp.concatenate(das, axis=1) * _dsilu(y)
        extd = jnp.concatenate([dy, head_ref[...]], axis=0)
        dx = dy * w_ref[3:4, :]
        for k in (1, 2, 3):
            dx = dx + pltpu.roll(extd, ts + 8 - k, 0)[:ts] * w_ref[3 - k:4 - k, :]
        head_ref[...] = dy[0:8, :]
        dx_ref[...] = _b(dx)
        r8 = lax.broadcasted_iota(jnp.int32, (8, 1536), 0)
        part = jnp.where(r8 == 3, _colsum(dy * x), 0.0)
        for k in (1, 2, 3):
            part = jnp.where(r8 == 3 - k, _colsum(dy * shifted[k - 1]), part)

        @pl.when(i == 0)
        def _():
            dw_ref[...] = part

        @pl.when(i > 0)
        def _():
            dw_ref[...] += part

    rev = pl.BlockSpec((ts, 1536), lambda i: (nt - 1 - i, 0))
    prev8 = pl.BlockSpec((8, 1536), lambda i: (jnp.maximum((nt - 1 - i) * (ts // 8) - 1, 0), 0))
    w8 = pl.BlockSpec((8, 1536), lambda i: (0, 0))
    return pl.pallas_call(
        body, grid=(nt,), out_shape=(jax.ShapeDtypeStruct((S, 1536), bf16), jax.ShapeDtypeStruct((8, 1536), f32)),
        in_specs=[rev, prev8, w8, rev], out_specs=(rev, w8), scratch_shapes=[pltpu.VMEM((8, 1536), f32)],
        compiler_params=_cp(("arbitrary",), VMEM_LIMIT), name=name)(zf, zf, cw, dg)


def _tri_inv(a, row, col):
    same = (row >> 4) == (col >> 4)
    dm = jnp.where(same, a, 0.0)
    lo = a - dm
    eye = jnp.where(row == col, 1.0, 0.0)
    d2 = _hi_b(dm, dm)
    d4 = _hi_b(d2, d2)
    d8 = _hi_b(d4, d4)
    x0 = _hi_b(_hi_b(eye - dm, eye + d2), _hi_b(eye + d4, eye + d8))
    n = _hi_b(x0, lo)
    n2 = _hi_b(n, n)
    return _hi_b(_hi_b(eye - n, eye + n2), x0)


def _bd(a, b):
    return lax.dot_general(a, b, (((2,), (1,)), ((0,), (0,))), preferred_element_type=f32)


def _bd_nt(a, b):
    return lax.dot_general(a, b, (((2,), (2,)), ((0,), (0,))), preferred_element_type=f32)


def _bd_tn(a, b):
    return lax.dot_general(a, b, (((1,), (1,)), ((0,), (0,))), preferred_element_type=f32)


def _hi_b_nt(a, b):
    return _mm3(a, b, (((2,), (2,)), ((0,), (0,))))


def _hi_b_tn(a, b):
    return _mm3(a, b, (((1,), (1,)), ((0,), (0,))))


def _gdn_local(x_ref, sm_ref, gt_ref, row, col, cps=1):
    idx = [(c, h) for c in range(cps) for h in range(4)]

    def rows(c):
        return slice(c * CH, (c + 1) * CH)

    q = jnp.stack([x_ref[rows(c), h * 128:(h + 1) * 128] for c, h in idx])
    k = jnp.stack([x_ref[rows(c), 512 + h * 128:512 + (h + 1) * 128] for c, h in idx])
    v = jnp.stack([x_ref[rows(c), 1024 + h * 128:1024 + (h + 1) * 128] for c, h in idx])
    gc = jnp.stack([sm_ref[rows(c), 8 + h:9 + h] for c, h in idx])
    beta = jnp.stack([sm_ref[rows(c), 12 + h:13 + h] for c, h in idx])
    gr = jnp.stack([gt_ref[h, c] for c, h in idx])
    eg = jnp.exp(gc)
    gl = gc[:, CH - 1:CH, :]
    dec = jnp.exp(gl - gc)
    gm = gc - gr
    gam_i = jnp.exp(jnp.where(row >= col, gm, -jnp.inf))
    gam_s = jnp.where(row > col, gam_i, 0.0)
    kb = k * beta
    return dict(q=q, k=k, v=v, beta=beta, eg=eg, egl=jnp.exp(gl), dec=dec, gam_i=gam_i, gam_s=gam_s,
                kb=kb, vb=v * beta, kbg=kb * eg, qdec=q * eg, kdec=k * dec,
                a=_bd_nt(_b(kb), _b(k)) * gam_s, aqk=_bd_nt(_b(q), _b(k)) * gam_i)


GDN_FWD_CHUNKS = 8


def _gdn_fwd(gqkv, sm, gt4, name, cps=GDN_FWD_CHUNKS):
    S = gqkv.shape[0]
    N = S // CH
    cps = min(cps, N)
    R = cps * CH

    def body(x_ref, sm_ref, gt_ref, o_ref, t_ref, st_ref, s_ref):
        n = pl.program_id(0)

        @pl.when(n == 0)
        def _():
            s_ref[...] = jnp.zeros_like(s_ref)

        row = lax.broadcasted_iota(jnp.int32, (CH, CH), 0)
        col = lax.broadcasted_iota(jnp.int32, (CH, CH), 1)
        c = _gdn_local(x_ref, sm_ref, gt_ref, row, col, cps)
        t = _tri_inv(c["a"], row, col)
        uw = _hi_b(t, jnp.concatenate([c["vb"], c["kbg"]], axis=2))
        u, w = uw[:, :, :128], uw[:, :, 128:]
        for ci in range(cps):
            sl = slice(4 * ci, 4 * ci + 4)
            rs = slice(ci * CH, (ci + 1) * CH)
            st = s_ref[...]
            st_ref[ci] = st
            sb = _b(st)
            vnew = u[sl] - _bd(_b(w[sl]), sb)
            o = _bd(_b(c["qdec"][sl]), sb) + _bd(_b(c["aqk"][sl]), _b(vnew))
            for h in range(4):
                o_ref[rs, h * 128:(h + 1) * 128] = o[h]
                t_ref[h, rs, :] = t[4 * ci + h]
            s_ref[...] = st * c["egl"][sl] + _bd_tn(_b(c["kdec"][sl]), _b(vnew))

    return pl.pallas_call(
        body, grid=(N // cps,),
        out_shape=(jax.ShapeDtypeStruct((S, 512), f32), jax.ShapeDtypeStruct((4, S, CH), f32), jax.ShapeDtypeStruct((N, 4, 128, 128), f32)),
        in_specs=[pl.BlockSpec((R, 1536), lambda n: (n, 0)), pl.BlockSpec((R, 128), lambda n: (n, 0)),
                  pl.BlockSpec((4, cps, 1, CH), lambda n: (0, n, 0, 0))],
        out_specs=(pl.BlockSpec((R, 512), lambda n: (n, 0)), pl.BlockSpec((4, R, CH), lambda n: (0, n, 0)),
                   pl.BlockSpec((cps, 4, 128, 128), lambda n: (n, 0, 0, 0))),
        scratch_shapes=[pltpu.VMEM((4, 128, 128), f32)], compiler_params=_cp(("arbitrary",)), name=name)(gqkv, sm, gt4)


GDN_BWD_CHUNKS = 4


def _gdn_bwd(gqkv, sm, gt4, tinv, states, do, name, cps=GDN_BWD_CHUNKS):
    S = gqkv.shape[0]
    N = S // CH
    cps = min(cps, N)
    R = cps * CH

    def body(x_ref, sm_ref, gt_ref, t_ref, st_ref, do_ref, dx_ref, dsm_ref, ds_ref):
        n = pl.program_id(0)

        @pl.when(n == 0)
        def _():
            ds_ref[...] = jnp.zeros_like(ds_ref)

        row = lax.broadcasted_iota(jnp.int32, (CH, CH), 0)
        col = lax.broadcasted_iota(jnp.int32, (CH, CH), 1)
        row1 = lax.broadcasted_iota(jnp.int32, (CH, 1), 0)
        lane = lax.broadcasted_iota(jnp.int32, (CH, 128), 1)
        ones = jnp.ones((4 * cps, CH, 128), f32)
        idx = [(ci, h) for ci in range(cps) for h in range(4)]
        c = _gdn_local(x_ref, sm_ref, gt_ref, row, col, cps)
        q, k, v, beta, eg = c["q"], c["k"], c["v"], c["beta"], c["eg"]
        t = jnp.stack([t_ref[h, ci * CH:(ci + 1) * CH, :] for ci, h in idx])
        uw = _hi_b(t, jnp.concatenate([c["vb"], c["kbg"]], axis=2))
        u, w = uw[:, :, :128], uw[:, :, 128:]
        st = st_ref[...].reshape(4 * cps, 128, 128)
        sb = _b(st)
        vnew = u - _bd(_b(w), sb)
        dob = _b(jnp.stack([do_ref[ci * CH:(ci + 1) * CH, h * 128:(h + 1) * 128] for ci, h in idx]))
        vnb = _b(vnew)
        dqdec = _bd_nt(dob, sb)
        daqk = jnp.where(row >= col, _bd_nt(dob, vnb), 0.0)
        qd_do = _bd_tn(_b(c["qdec"]), dob)
        aqk_do = _bd_tn(_b(c["aqk"]), dob)
        kdecb, wb = _b(c["kdec"]), _b(w)
        dvnew_l, dkdec_l, dgl_l = [None] * cps, [None] * cps, [None] * cps
        for ci in reversed(range(cps)):
            sl = slice(4 * ci, 4 * ci + 4)
            dsp = ds_ref[...]
            dspb = _b(dsp)
            dvn = _bd(kdecb[sl], dspb) + aqk_do[sl]
            dvnew_l[ci] = dvn
            dkdec_l[ci] = _bd_nt(vnb[sl], dspb)
            dgl_l[ci] = c["egl"][sl] * jnp.sum(dsp * st[sl], axis=(1, 2), keepdims=True)
            ds_ref[...] = dsp * c["egl"][sl] + qd_do[sl] - _bd_tn(wb[sl], _b(dvn))
        dvnew = jnp.concatenate(dvnew_l, axis=0)
        dkdec = jnp.concatenate(dkdec_l, axis=0)
        dgl = jnp.concatenate(dgl_l, axis=0)
        dw = -_bd_nt(_b(dvnew), sb)
        duw = _hi_b_tn(t, jnp.concatenate([dvnew, dw], axis=2))
        dvb, dkbg = duw[:, :, :128], duw[:, :, 128:]
        da = -jnp.where(row > col, _hi_b_nt(duw, uw), 0.0)
        dp = da * c["gam_s"]
        dqk = daqk * c["gam_i"]
        m = da * c["a"] + daqk * c["aqk"]
        csum = _hi_b_tn(m, ones)[:, :, 0:1]
        kk = dkdec * c["kdec"]

        def lsum(a):
            return jnp.sum(a, axis=2, keepdims=True)

        dgv = lsum(m) - csum + lsum(dqdec * c["qdec"]) - lsum(kk) + lsum(dkbg * c["kbg"])
        dgv = dgv + jnp.where(row1 == CH - 1, dgl + jnp.sum(kk, axis=(1, 2), keepdims=True), 0.0)
        dpb, dqkb = _b(dp), _b(dqk)
        dkb = _bd(dpb, _b(k)) + dkbg * eg
        dk = _bd_tn(dpb, _b(c["kb"])) + _bd_tn(dqkb, _b(q)) + dkdec * c["dec"] + dkb * beta
        dq = _bd(dqkb, _b(k)) + dqdec * eg
        dbeta = lsum(dkb * k) + lsum(dvb * v)
        dv = dvb * beta
        for ci in range(cps):
            rs = slice(ci * CH, (ci + 1) * CH)
            dsm = jnp.zeros((CH, 128), f32)
            for h in range(4):
                b = 4 * ci + h
                dx_ref[rs, h * 128:(h + 1) * 128] = dq[b]
                dx_ref[rs, 512 + h * 128:512 + (h + 1) * 128] = dk[b]
                dx_ref[rs, 1024 + h * 128:1024 + (h + 1) * 128] = dv[b]
                dsm = jnp.where(lane == 8 + h, dgv[b], jnp.where(lane == 12 + h, dbeta[b], dsm))
            dsm_ref[rs, :] = dsm

    G = N // cps
    return pl.pallas_call(
        body, grid=(G,), out_shape=(jax.ShapeDtypeStruct((S, 1536), f32), jax.ShapeDtypeStruct((S, 128), f32)),
        in_specs=[pl.BlockSpec((R, 1536), lambda n: (G - 1 - n, 0)), pl.BlockSpec((R, 128), lambda n: (G - 1 - n, 0)),
                  pl.BlockSpec((4, cps, 1, CH), lambda n: (0, G - 1 - n, 0, 0)), pl.BlockSpec((4, R, CH), lambda n: (0, G - 1 - n, 0)),
                  pl.BlockSpec((cps, 4, 128, 128), lambda n: (G - 1 - n, 0, 0, 0)), pl.BlockSpec((R, 512), lambda n: (G - 1 - n, 0))],
        out_specs=(pl.BlockSpec((R, 1536), lambda n: (G - 1 - n, 0)), pl.BlockSpec((R, 128), lambda n: (G - 1 - n, 0))),
        scratch_shapes=[pltpu.VMEM((4, 128, 128), f32)], compiler_params=_cp(("arbitrary",), VMEM_LIMIT), name=name)(gqkv, sm, gt4, tinv, states, do)


MERGE_ROWS = 256
MERGE_FWD_ROWS = 512


def _mem_attn(q, kv_ref, h):
    s = _dot_nt(q, kv_ref[:, h * 128:(h + 1) * 128]) * MEM_SCALE
    e = jnp.exp(s - jnp.max(s, axis=1, keepdims=True))
    return e / _rowsum(e)


def _gdn_out_norm(ob):
    r = lax.rsqrt(jnp.mean(ob * ob, axis=-1, keepdims=True) + EPS)
    return ob * r, r


def _merge_fwd(x, oa, ob, zb, zf, kv, b_merge, gdn_g, w_branch, w_out, name):
    S = x.shape[0]
    ts = min(S, MERGE_FWD_ROWS)

    def body(x_ref, oa_ref, ob_ref, mq_ref, az_ref, bz_ref, mz_ref, gt_ref, kv_ref, bm_ref, gg_ref, wb_ref, wo_ref,
             xo_ref, y_ref, mg_ref):
        y_ref[:, 0:512] = _b(oa_ref[...] * _silu(az_ref[...]))
        for h in range(4):
            sl = slice(h * 128, (h + 1) * 128)
            nb, _ = _gdn_out_norm(ob_ref[:, sl])
            y_ref[:, 512 + h * 128:512 + (h + 1) * 128] = _b(nb * gg_ref[...] * _silu(bz_ref[:, sl]))
            pm = _mem_attn(mq_ref[:, sl], kv_ref, h)
            om = _dot(_b(pm), kv_ref[:, 512 + h * 128:512 + (h + 1) * 128])
            y_ref[:, 1024 + h * 128:1024 + (h + 1) * 128] = _b(om * _silu(mz_ref[:, sl]))
        merged = jnp.zeros((ts, D), f32)
        for n in range(3):
            gate = _sig(gt_ref[:, n * D:(n + 1) * D] + bm_ref[:, n * D:(n + 1) * D])
            merged = merged + gate * _dot(y_ref[:, n * 512:(n + 1) * 512], wb_ref[n])
        mb = _b(merged)
        mg_ref[...] = mb
        xo_ref[...] = x_ref[...] + _dot(mb, wo_ref[...])

    def col(w, c):
        return pl.BlockSpec((ts, w), lambda i: (i, c))

    def full(shape):
        return pl.BlockSpec(shape, lambda i: tuple(0 for _ in shape))

    return pl.pallas_call(
        body, grid=(S // ts,),
        out_shape=(jax.ShapeDtypeStruct((S, D), f32), jax.ShapeDtypeStruct((S, 1536), bf16), jax.ShapeDtypeStruct((S, D), bf16)),
        in_specs=[col(D, 0), col(512, 0), col(512, 0), col(512, 3), col(512, 3), col(512, 4), col(512, 5), col(3072, 1),
                  full((256, D)), full((1, 3072)), full((1, 128)), full((3, 512, D)), full((D, D))],
        out_specs=(col(D, 0), col(1536, 0), col(D, 0)),
        compiler_params=_cp(("parallel",), VMEM_LIMIT), name=name)(x, oa, ob, zb, zf, zf, zf, zf, kv, b_merge, gdn_g, w_branch, w_out)


def _merge_bwd(dout, ycat, oa, ob, zb, zf, kv, b_merge, gdn_g, w_branch, w_branch_t, w_out_t, name):
    S = dout.shape[0]
    ts = min(S, MERGE_ROWS)

    def body(do_ref, y_ref, oa_ref, ob_ref, mq_ref, az_ref, bz_ref, mz_ref, gt_ref, kv_ref, bm_ref, gg_ref, wb_ref, wbt_ref, wot_ref,
             dpj_ref, dz_ref, dmq_ref, dlt_ref, doab_ref, dob_ref, dkv_ref, dbm_ref, dgg_ref):
        i = pl.program_id(0)
        dmerged = _dot(_b(do_ref[...]), wot_ref[...])
        dys = []
        dbm_parts = []
        for n in range(3):
            cs = slice(n * D, (n + 1) * D)
            gate = _sig(gt_ref[:, cs] + bm_ref[:, cs])
            proj = _dot(y_ref[:, n * 512:(n + 1) * 512], wb_ref[n])
            dlogit = dmerged * proj * gate * (1.0 - gate)
            dz_ref[:, 1536 + n * D:1536 + (n + 1) * D] = _b(dlogit)
            dbm_parts.append(_colsum(dlogit))
            dproj = _b(dmerged * gate)
            dpj_ref[:, cs] = dproj
            dys.append(_dot(dproj, wbt_ref[n]))
        dbm = jnp.broadcast_to(jnp.concatenate(dbm_parts, axis=1), (8, 3072))
        az = az_ref[...]
        oa = oa_ref[...]
        doa = dys[0] * _silu(az)
        doab_ref[...] = _b(doa)
        prod = doa * oa
        lane = lax.broadcasted_iota(jnp.int32, (ts, 128), 1)
        dl = jnp.zeros((ts, 128), f32)
        for p in range(4):
            blk = prod[:, p * 128:(p + 1) * 128]
            dl = jnp.where(lane == 2 * p, _rowsum(jnp.where(lane < 64, blk, 0.0)),
                           jnp.where(lane == 2 * p + 1, _rowsum(jnp.where(lane >= 64, blk, 0.0)), dl))
        dlt_ref[...] = jnp.transpose(dl)[0:8, :]
        dz_ref[:, 0:512] = _b(dys[0] * oa * _dsilu(az))
        gg = gg_ref[...]
        dgg = jnp.zeros((1, 128), f32)
        dkv_parts_k, dkv_parts_v = [], []
        for h in range(4):
            sl = slice(h * 128, (h + 1) * 128)
            bz = bz_ref[:, sl]
            dyb = dys[1][:, sl]
            nb, r = _gdn_out_norm(ob_ref[:, sl])
            dz_ref[:, 512 + h * 128:512 + (h + 1) * 128] = _b(dyb * nb * gg * _dsilu(bz))
            dng = dyb * _silu(bz)
            dgg = dgg + _colsum(dng * nb)
            dnb = dng * gg
            dob_ref[:, sl] = r * (dnb - nb * jnp.mean(dnb * nb, axis=-1, keepdims=True))
            mz = mz_ref[:, sl]
            dym = dys[2][:, sl]
            q = mq_ref[:, sl]
            kh = kv_ref[:, sl]
            vh = kv_ref[:, 512 + h * 128:512 + (h + 1) * 128]
            pm = _mem_attn(q, kv_ref, h)
            pmb = _b(pm)
            om = _dot(pmb, vh)
            dz_ref[:, 1024 + h * 128:1024 + (h + 1) * 128] = _b(dym * om * _dsilu(mz))
            dom = _b(dym * _silu(mz))
            dkv_parts_v.append(_dot_tn(pmb, dom))
            dpm = _dot_nt(dom, vh)
            dsm = _b(pm * (dpm - _rowsum(dpm * pm)) * MEM_SCALE)
            dmq_ref[:, sl] = _b(_dot(dsm, kh))
            dkv_parts_k.append(_dot_tn(dsm, q))
        dkv = jnp.concatenate(dkv_parts_k + dkv_parts_v, axis=1)
        dggb = jnp.broadcast_to(dgg, (8, 128))

        @pl.when(i == 0)
        def _():
            dkv_ref[...] = dkv
            dbm_ref[...] = dbm
            dgg_ref[...] = dggb

        @pl.when(i > 0)
        def _():
            dkv_ref[...] += dkv
            dbm_ref[...] += dbm
            dgg_ref[...] += dggb

    def col(w, c):
        return pl.BlockSpec((ts, w), lambda i: (i, c))

    def full(shape):
        return pl.BlockSpec(shape, lambda i: tuple(0 for _ in shape))

    return pl.pallas_call(
        body, grid=(S // ts,),
        out_shape=(jax.ShapeDtypeStruct((S, 3072), bf16), jax.ShapeDtypeStruct((S, N_ALL), bf16), jax.ShapeDtypeStruct((S, 512), bf16),
                   jax.ShapeDtypeStruct((8, S), f32), jax.ShapeDtypeStruct((S, 512), bf16), jax.ShapeDtypeStruct((S, 512), f32),
                   jax.ShapeDtypeStruct((256, D), f32), jax.ShapeDtypeStruct((8, 3072), f32), jax.ShapeDtypeStruct((8, 128), f32)),
        in_specs=[col(D, 0), col(1536, 0), col(512, 0), col(512, 0), col(512, 3), col(512, 3), col(512, 4), col(512, 5), col(3072, 1),
                  full((256, D)), full((1, 3072)), full((1, 128)), full((3, 512, D)), full((3, D, 512)), full((D, D))],
        out_specs=(col(3072, 0), col(4608, 0), col(512, 0), pl.BlockSpec((8, ts), lambda i: (0, i)), col(512, 0), col(512, 0),
                   full((256, D)), full((8, 3072)), full((8, 128))),
        compiler_params=_cp(("arbitrary",), VMEM_LIMIT), name=name)(
            dout, ycat, oa, ob, zb, zf, zf, zf, zf, kv, b_merge, gdn_g, w_branch, w_branch_t, w_out_t)


def _mesh_pos():
    return lax.axis_index("x"), lax.axis_index("y"), lax.axis_index("c")


class _Gather:
    def __init__(self, x_refs, out_refs, send_sems, recv_sems, local_sems):
        self.n = len(x_refs)
        self.x_refs, self.out_refs = x_refs, out_refs
        self.send_sems, self.recv_sems, self.local_sems = send_sems, recv_sems, local_sems
        mx, my, mc = _mesh_pos()
        self.mc = mc
        self.me, self.sibling = (mx, my, mc), (mx, my, 1 - mc)
        self.chips = [(1 - mx, my), (mx, 1 - my), (1 - mx, 1 - my)]

    def copy(self, a, k, block, to, src=None):
        px, py, pc = block
        slot = self.out_refs[a].at[4 * px + 2 * py + pc]
        return pltpu.make_async_remote_copy(
            src_ref=slot if src is None else src, dst_ref=slot, send_sem=self.send_sems.at[7 * a + k],
            recv_sem=self.recv_sems.at[7 * a + k], device_id=to, device_id_type=pl.DeviceIdType.MESH)

    def own(self):
        mx, my, mc = self.me
        mine = [pltpu.make_async_copy(self.x_refs[a], self.out_refs[a].at[4 * mx + 2 * my + mc], self.local_sems.at[a])
                for a in range(self.n)]
        first = []
        for a in range(self.n):
            first.append(self.copy(a, 0, self.me, self.sibling, src=self.x_refs[a]))
            first += [self.copy(a, 1 + j, self.me, (*chip, self.mc), src=self.x_refs[a]) for j, chip in enumerate(self.chips)]
        return mine, first

    def start(self):
        mine, first = self.own()
        for cp in mine + first:
            cp.start()

    def finish(self):
        mine, first = self.own()
        passed = []
        for j, chip in enumerate(self.chips):
            for a in range(self.n):
                self.copy(a, 1 + j, (*chip, self.mc), self.me).wait_recv()
                fwd = self.copy(a, 4 + j, (*chip, self.mc), self.sibling)
                fwd.start()
                passed.append(fwd)
        for a in range(self.n):
            self.copy(a, 0, self.sibling, self.me).wait_recv()
            for j, chip in enumerate(self.chips):
                self.copy(a, 4 + j, (*chip, 1 - self.mc), self.me).wait_recv()
        for cp in first + passed:
            cp.wait_send()
        for cp in mine:
            cp.wait()


def _all_gather(xs, name):
    n = len(xs)

    def body(*refs):
        g = _Gather(refs[:n], refs[n:2 * n], *refs[2 * n:])
        g.start()
        g.finish()

    anyspec = pl.BlockSpec(memory_space=pl.ANY)
    return pl.pallas_call(
        body, out_shape=tuple(jax.ShapeDtypeStruct((N_DEV,) + x.shape, x.dtype) for x in xs),
        in_specs=[anyspec] * n, out_specs=tuple([anyspec] * n), scratch_shapes=_exchange_scratch(n), name=name)(*xs)


def _exchange_copies(s_refs, r_refs, send_sems, recv_sems, local_sems):
    n = len(s_refs)
    mx, my, mc = _mesh_pos()
    me_id = 4 * mx + 2 * my + mc
    copies = [pltpu.make_async_copy(s_refs[a].at[me_id], r_refs[a].at[me_id], local_sems.at[a]) for a in range(n)]
    for k in range(1, N_DEV):
        px = 1 - mx if k & 4 else mx
        py = 1 - my if k & 2 else my
        pc = 1 - mc if k & 1 else mc
        for a in range(n):
            copies.append(pltpu.make_async_remote_copy(
                src_ref=s_refs[a].at[4 * px + 2 * py + pc], dst_ref=r_refs[a].at[me_id],
                send_sem=send_sems.at[7 * a + k - 1], recv_sem=recv_sems.at[7 * a + k - 1],
                device_id=(px, py, pc), device_id_type=pl.DeviceIdType.MESH))
    return copies


def _exchange_scratch(n):
    return [pltpu.SemaphoreType.DMA((7 * n,)), pltpu.SemaphoreType.DMA((7 * n,)), pltpu.SemaphoreType.DMA((n,))]


def _exchange(sends, name):
    n = len(sends)

    def body(*refs):
        copies = _exchange_copies(refs[:n], refs[n:2 * n], *refs[2 * n:])
        for cp in copies:
            cp.start()
        for cp in copies:
            cp.wait()

    anyspec = pl.BlockSpec(memory_space=pl.ANY)
    return pl.pallas_call(
        body, out_shape=tuple(jax.ShapeDtypeStruct(s.shape, s.dtype) for s in sends),
        in_specs=[anyspec] * n, out_specs=tuple([anyspec] * n), scratch_shapes=_exchange_scratch(n), name=name)(*sends)


ADAMW_BLOCK_BYTES = 4 * 1024 * 1024


def _adamw(parts, w, m, v, name):
    _, R, C = parts.shape
    tr = R
    for t in (1024, 512, 256, 128, 64, 32, 16, 8):
        if R % t == 0 and N_DEV * t * C * 4 <= ADAMW_BLOCK_BYTES:
            tr = t
            break

    def body(p_ref, w_ref, m_ref, v_ref, g_ref, d_ref, nm_ref, nv_ref):
        g = p_ref[0].astype(f32)
        for j in range(1, N_DEV):
            g = g + p_ref[j].astype(f32)
        mn = ADAM_B1 * m_ref[...] + (1.0 - ADAM_B1) * g
        vn = ADAM_B2 * v_ref[...] + (1.0 - ADAM_B2) * jnp.square(g)
        m_hat = mn / (1.0 - ADAM_B1 ** ADAM_STEP)
        v_hat = vn / (1.0 - ADAM_B2 ** ADAM_STEP)
        g_ref[...] = g
        d_ref[...] = -ADAM_LR * (m_hat / (jnp.sqrt(v_hat) + ADAM_EPS) + ADAM_WD * w_ref[...])
        nm_ref[...] = mn
        nv_ref[...] = vn

    t2 = pl.BlockSpec((tr, C), lambda i: (i, 0))
    out = jax.ShapeDtypeStruct((R, C), f32)
    return pl.pallas_call(
        body, grid=(R // tr,), out_shape=(out, out, out, out),
        in_specs=[pl.BlockSpec((N_DEV, tr, C), lambda i: (0, i, 0)), t2, t2, t2], out_specs=(t2, t2, t2, t2),
        compiler_params=_cp(("parallel",), VMEM_LIMIT), name=name)(parts, w, m, v)


def _as2d(a):
    return a.reshape(-1, a.shape[-1])


def _perm_cols(w, order=_ORDER):
    parts = [w[..., _COLS[n][0]:_COLS[n][1]] for n in order]
    pad = jnp.zeros(w.shape[:-1] + (N_ALL - N_IN,), w.dtype)
    return jnp.concatenate(parts + [pad], axis=-1)


def _unperm_cols(w, order=_ORDER):
    pieces, off = {}, 0
    for n in order:
        width = _COLS[n][1] - _COLS[n][0]
        pieces[n] = w[..., off:off + width]
        off += width
    return jnp.concatenate([pieces[n] for n in sorted(_COLS, key=lambda n: _COLS[n][0])], axis=-1)


_SMALL_ROWS = 16


def _pack_small(t):
    z = jnp.zeros((D,), f32)
    misc = z.at[0:16].set(t["b_fg"].reshape(-1)).at[16:24].set(t["a_log"].reshape(-1)).at[24:32].set(t["dt_bias"].reshape(-1))
    misc = misc.at[128:384].set(t["gdn_norm_g"].reshape(-1))
    if "extra" in t:
        misc = misc.at[512].set(t["extra"])
    rows = [t["norm_g"], t["b_merge"].reshape(6, D), t["mem_norm_g"], t["final_norm_g"][None], misc[None],
            jnp.zeros((_SMALL_ROWS - 12, D), f32)]
    return jnp.concatenate(rows, axis=0)


def _unpack_small(a):
    misc = a[11]
    return dict(norm_g=a[0:2], b_merge=a[2:8].reshape(2, 3072), mem_norm_g=a[8:10], final_norm_g=a[10],
                b_fg=misc[0:16].reshape(2, 8), a_log=misc[16:24].reshape(2, 4), dt_bias=misc[24:32].reshape(2, 4),
                gdn_norm_g=misc[128:384].reshape(2, 128), extra=misc[512])


def _layer_fwd(l, x, mem, p, gather_next=None):
    sfx = f"_l{l}"
    h, ht = _norm_fwd(x, p["norm_g"], "norm_fwd" + sfx, with_t=True)
    zb = _mm(h, p["w_b"], bf16, 1024, 1024, 1024, "inproj_b" + sfx)
    if gather_next is None:
        zf, gathered = _mm(h, p["w_f"], f32, 1024, 1024, 1024, "inproj_f" + sfx), None
    else:
        zf, gathered = _mm(h, p["w_f"], f32, 1024, 1024, 1024, "inproj_f" + sfx, comm=("gather", gather_next))
    zs = _mm(h, p["w_s"], f32, 512, 128, 1024, "inproj_s" + sfx)
    sm = _small_prep(zs, p["par"], "small_prep" + sfx)
    S = x.shape[0]
    gt4 = jnp.transpose(sm[:, 8:12]).reshape(4, S // CH, 1, CH)
    qa, ka, st = _fox_prep(zb, sm, "fox_prep" + sfx)
    tab = _fox_bound_table(st, S, min(S, FOX_TILE))
    oa, lse_t = _fox_fwd(qa, ka, zb, tab, "fox_fwd" + sfx)
    gqkv = _gdn_prep(zf, p["conv_w"], "gdn_prep" + sfx)
    ob, tinv, states = _gdn_fwd(gqkv, sm, gt4, "gdn_fwd" + sfx)
    memn = _norm_fwd(mem, p["mem_norm_g"], "mem_norm" + sfx)
    kv = _mm(memn, p["w_mem_kv"], bf16, 256, 1024, 1024, "mem_kv" + sfx)
    xo, ycat, merged = _merge_fwd(x, oa, ob, zb, zf, kv, p["b_merge"], p["gdn_norm_g"], p["w_branch"], p["w_out"], "merge_fwd" + sfx)
    saved = dict(x=x, ht=ht, zb=zb, zf=zf, zs=zs, sm=sm, qa=qa, ka=ka, tab=tab, gt4=gt4, oa=oa, lse_t=lse_t, gqkv=gqkv, ob=ob, tinv=tinv,
                 states=states, memn=memn, kv=kv, ycat=ycat, merged=merged)
    return xo, saved, gathered


def _layer_bwd(l, dout, mem, p, s, comm_sends=(), send_fn=None):
    sfx = f"_l{l}"
    dproj, dzf2, dmq, delta, doab, dob, dkv, dbm, dgg = _merge_bwd(
        dout, s["ycat"], s["oa"], s["ob"], s["zb"], s["zf"], s["kv"], p["b_merge"], p["gdn_norm_g"],
        p["w_branch"], p["w_branch_t"], p["w_out_t"], "merge_bwd" + sfx)
    g = {}
    g["w_out"] = _mm(s["merged"], dout, f32, 512, 1024, 512, "dw_out" + sfx, trans_a=True)
    g["w_branch"] = jnp.stack([
        _mm(s["ycat"], dproj, f32, 512, 1024, 512, f"dw_branch{n}" + sfx, trans_a=True, a_cols=(n * 512, 512), b_cols=(n * D, D))
        for n in range(3)])
    g["b_merge"] = dbm[0]
    g["gdn_norm_g"] = dgg[0]
    g["w_mem_kv"] = _mm(s["memn"], dkv, f32, 512, 1024, 256, "dw_mem_kv" + sfx, trans_a=True)
    dmemn = _mm(dkv, p["w_mem_kv_t"], f32, 256, 1024, 1024, "dmem_n" + sfx)
    g["mem_norm_g"] = _norm_bwd(mem, p["mem_norm_g"], dmemn, None, "mem_norm_bwd" + sfx)[0]
    dgqkv, dsm = _gdn_bwd(s["gqkv"], s["sm"], s["gt4"], s["tinv"], s["states"], dob, "gdn_bwd" + sfx)
    dbqkv, dcw = _gdn_prep_bwd(s["zf"], p["conv_w"], dgqkv, "gdn_prep_bwd" + sfx)
    g["conv_w"] = dcw[0:4]
    (dq, dk, dv, dfc, dfr), received = _fox_bwd(s["qa"], s["ka"], s["zb"], doab, _head_rows(s["lse_t"]), delta, s["tab"],
                                                "fox_bwd" + sfx, comm_sends=comm_sends)
    dzs, sacc = _small_bwd(s["zs"], p["par"], _head_rows(dfr), dfc, dsm, "small_bwd" + sfx)
    g["b_fg"], g["a_log"], g["dt_bias"] = sacc[0, 0:8], sacc[1, 8:12], sacc[2, 8:12]
    dz = lax.dynamic_update_slice(dzf2, jnp.concatenate([dq, dk, dv, dmq, dbqkv, dzs], axis=1), (0, DZ_MERGE_COLS))
    g["w_in"] = _mm(s["ht"], dz, f32, 1024, 1664, 1024, "dw_in" + sfx)
    if send_fn is None:
        dh, received_late = _mm(dz, p["w_all_t"], f32, 1024, 1024, 1664, "dh" + sfx), None
    else:
        dh, received_late = _mm(dz, p["w_all_t"], f32, 1024, 1024, 1664, "dh" + sfx, comm=("exchange", send_fn(g)))
    dx, dng = _norm_bwd(s["x"], p["norm_g"], dh, dout, "norm_bwd" + sfx)
    g["norm_g"] = dng[0]
    return dx, g, received, received_late


def kernel(x, mem, norm_g, w_in, b_fg, b_merge, conv_w, a_log, dt_bias, gdn_norm_g, mem_norm_g, w_mem_kv, w_branch, w_out, final_norm_g, loss_target, m_norm_g, m_w_in, m_b_fg, m_b_merge, m_conv_w, m_a_log, m_dt_bias, m_gdn_norm_g, m_mem_norm_g, m_w_mem_kv, m_w_branch, m_w_out, m_final_norm_g, v_norm_g, v_w_in, v_b_fg, v_b_merge, v_conv_w, v_a_log, v_dt_bias, v_gdn_norm_g, v_mem_norm_g, v_w_mem_kv, v_w_branch, v_w_out, v_final_norm_g):
    x0, mem0, tgt = x[0], mem[0], loss_target[0]
    shard_w = dict(w_in=w_in, w_mem_kv=w_mem_kv, w_branch=w_branch, w_out=w_out, conv_w=conv_w)
    shard_m = dict(w_in=m_w_in, w_mem_kv=m_w_mem_kv, w_branch=m_w_branch, w_out=m_w_out, conv_w=m_conv_w)
    shard_v = dict(w_in=v_w_in, w_mem_kv=v_w_mem_kv, w_branch=v_w_branch, w_out=v_w_out, conv_w=v_conv_w)
    small_w = dict(norm_g=norm_g, b_fg=b_fg, b_merge=b_merge, a_log=a_log, dt_bias=dt_bias, gdn_norm_g=gdn_norm_g,
                   mem_norm_g=mem_norm_g, final_norm_g=final_norm_g)
    small_m = dict(norm_g=m_norm_g, b_fg=m_b_fg, b_merge=m_b_merge, a_log=m_a_log, dt_bias=m_dt_bias, gdn_norm_g=m_gdn_norm_g,
                   mem_norm_g=m_mem_norm_g, final_norm_g=m_final_norm_g)
    small_v = dict(norm_g=v_norm_g, b_fg=v_b_fg, b_merge=v_b_merge, a_log=v_a_log, dt_bias=v_dt_bias, gdn_norm_g=v_gdn_norm_g,
                   mem_norm_g=v_mem_norm_g, final_norm_g=v_final_norm_g)

    def shards(l):
        return [_b(w_in[l]), _b(w_mem_kv[l]), _b(_as2d(w_branch[l])), _b(w_out[l])]

    def layer_params(l, g_in, g_kv, g_br, g_out, conv_full):
        w_full = jnp.transpose(g_in, (1, 0, 2)).reshape(D, N_IN)
        w_all = _perm_cols(w_full)
        w_kv = g_kv.reshape(D, D)
        w_br = jnp.transpose(g_br.reshape(N_DEV, 3, 512, 128), (1, 2, 0, 3)).reshape(3, 512, D)
        w_o = g_out.reshape(D, D)
        return dict(
            norm_g=norm_g[l][None], mem_norm_g=mem_norm_g[l][None], gdn_norm_g=gdn_norm_g[l][None], b_merge=b_merge[l][None],
            par=_small_pars(b_fg[l], a_log[l], dt_bias[l]),
            conv_w=jnp.pad(conv_full[l], ((0, 4), (0, 0))),
            w_b=w_all[:, 0:NB], w_f=w_all[:, NB:NB + NF], w_s=w_all[:, NB + NF:],
            w_all_t=jnp.transpose(_perm_cols(w_full, _ORDER_BWD)),
            w_mem_kv=w_kv, w_mem_kv_t=jnp.transpose(w_kv),
            w_branch=w_br, w_branch_t=jnp.transpose(w_br, (0, 2, 1)),
            w_out=w_o, w_out_t=jnp.transpose(w_o))

    *gathered0, conv_all = _all_gather(shards(0) + [_as2d(conv_w)], "gather_weights")
    conv_full = jnp.transpose(conv_all.reshape(N_DEV, DEPTH, 4, 192), (1, 2, 0, 3)).reshape(DEPTH, 4, 1536)
    layers = [layer_params(0, *gathered0, conv_full), None]

    saved = [None] * DEPTH
    acts, saved[0], gathered1 = _layer_fwd(0, x0, mem0, layers[0], gather_next=shards(1))
    layers[1] = layer_params(1, *gathered1, conv_full)
    acts, saved[1], _ = _layer_fwd(1, acts, mem0, layers[1])
    dx, dfg, lsum = _loss_head(acts, final_norm_g[None], tgt, "loss_head")

    def send_buffers(g):
        dw_in = _unperm_cols(g["w_in"], _ORDER_BWD)
        send = dict(
            w_in=jnp.transpose(dw_in.reshape(D, N_DEV, 1026), (1, 0, 2)),
            w_mem_kv=g["w_mem_kv"].reshape(N_DEV, 128, D),
            w_branch=jnp.transpose(g["w_branch"].reshape(3, 512, N_DEV, 128), (2, 0, 1, 3)).reshape(N_DEV, 3 * 512, 128),
            w_out=g["w_out"].reshape(N_DEV, 128, D),
            conv_w=jnp.transpose(g["conv_w"].reshape(4, N_DEV, 192), (1, 0, 2)))
        return [_b(send[n]) for n in _SHARDED]

    grads, parts = [None] * DEPTH, [None] * DEPTH
    dx, grads[1], _, _ = _layer_bwd(1, dx, mem0, layers[1], saved[1])
    dx, grads[0], parts[1], parts[0] = _layer_bwd(0, dx, mem0, layers[0], saved[0], comm_sends=send_buffers(grads[1]),
                                                  send_fn=send_buffers)
    grad_x = dx[None]

    big = [{}, {}, {}, {}]
    for a, n in enumerate(_SHARDED):
        res = [_adamw(parts[l][a], _as2d(shard_w[n][l]), _as2d(shard_m[n][l]), _as2d(shard_v[n][l]), f"adamw_{n}_l{l}")
               for l in range(DEPTH)]
        for kind in range(4):
            big[kind][n] = jnp.stack([res[l][kind] for l in range(DEPTH)]).reshape(shard_w[n].shape)

    small_g = {k: jnp.stack([grads[l][k] for l in range(DEPTH)]) for k in ("norm_g", "b_fg", "b_merge", "a_log", "dt_bias", "gdn_norm_g", "mem_norm_g")}
    small_g["final_norm_g"] = dfg[0]
    small_g["extra"] = lsum[0, 0]
    parts_s, = _all_gather([_pack_small(small_g)], "gather_small")
    g_sm, d_sm, m_sm, v_sm = _adamw(parts_s, _pack_small(small_w), _pack_small(small_m), _pack_small(small_v), "adamw_replicated")

    sml = [_unpack_small(a) for a in (g_sm, d_sm, m_sm, v_sm)]
    loss = sml[0]["extra"]
    names = ("norm_g", "w_in", "b_fg", "b_merge", "conv_w", "a_log", "dt_bias", "gdn_norm_g", "mem_norm_g", "w_mem_kv", "w_branch", "w_out", "final_norm_g")
    outs = [loss, grad_x]
    for kind in range(4):
        for n in names:
            outs.append(big[kind][n] if n in big[kind] else sml[kind][n])
    return tuple(outs)
```

```python
import functools

import jax
import jax.numpy as jnp
from jax import lax
from jax.experimental import pallas as pl
from jax.experimental.pallas import tpu as pltpu

f32, bf16 = jnp.float32, jnp.bfloat16

D = 1024
EPS = 1e-6
CH = 64
N_DEV = 8
DEPTH = 2
FOX_SCALE = 64 ** -0.5
GDN_SCALE = 128 ** -0.5
MEM_SCALE = 128 ** -0.5
NEG = -1e30
VMEM_LIMIT = 56 * 1024 * 1024

ADAM_LR, ADAM_B1, ADAM_B2, ADAM_EPS, ADAM_WD, ADAM_STEP = 0.001, 0.9, 0.999, 1e-08, 0.01, 10

_COLS = dict(aq=(0, 512), ak=(512, 1024), av=(1024, 1536), af=(1536, 1544), az=(1544, 2056),
             bq=(2056, 2568), bk=(2568, 3080), bv=(3080, 3592), ba=(3592, 3596), bb=(3596, 3600),
             bz=(3600, 4112), mq=(4112, 4624), mz=(4624, 5136), gates=(5136, 8208))
_ORDER = ("aq", "ak", "av", "mq", "bq", "bk", "bv", "az", "bz", "mz", "gates", "af", "ba", "bb")
_ORDER_BWD = ("az", "bz", "mz", "gates", "aq", "ak", "av", "mq", "bq", "bk", "bv", "af", "ba", "bb")
DZ_MERGE_COLS = 4608
N_IN = 8208
NB, NF, NS = 2048, 6144, 128
PIECE_COLS = 512
N_ALL = NB + NF + PIECE_COLS

_SHARDED = ("w_in", "w_mem_kv", "w_branch", "w_out", "conv_w")


def _cp(sem=None, vmem=None):
    kw = {}
    if sem is not None:
        kw["dimension_semantics"] = sem
    if vmem is not None:
        kw["vmem_limit_bytes"] = vmem
    return pltpu.CompilerParams(**kw)


def _dot(a, b):
    return jnp.dot(a, b, preferred_element_type=f32)


def _dot_nt(a, b):
    return lax.dot_general(a, b, (((1,), (1,)), ((), ())), preferred_element_type=f32)


def _dot_tn(a, b):
    return lax.dot_general(a, b, (((0,), (0,)), ((), ())), preferred_element_type=f32)


def _split2(x):
    hi = x.astype(bf16)
    return hi, (x - hi.astype(f32)).astype(bf16)


def _mm3(a, b, dims):
    ah, al = _split2(a)
    bh, bl = _split2(b)
    dg = functools.partial(lax.dot_general, dimension_numbers=dims, preferred_element_type=f32)
    return dg(ah, bh) + (dg(ah, bl) + dg(al, bh))


def _hi(a, b):
    return _mm3(a, b, (((1,), (0,)), ((), ())))


def _hi_nt(a, b):
    return _mm3(a, b, (((1,), (1,)), ((), ())))


def _hi_tn(a, b):
    return _mm3(a, b, (((0,), (0,)), ((), ())))


def _hi_b(a, b):
    return _mm3(a, b, (((2,), (1,)), ((0,), (0,))))


def _b(x):
    return x.astype(bf16)


def _sig(x):
    return jax.nn.sigmoid(x)


def _silu(x):
    return x * _sig(x)


def _dsilu(x):
    s = _sig(x)
    return s * (1.0 + x * (1.0 - s))


def _softplus(x):
    return jnp.maximum(x, 0.0) + jnp.log1p(jnp.exp(-jnp.abs(x)))


def _rowsum(x):
    return jnp.sum(x, axis=1, keepdims=True)


def _colsum(x):
    return jnp.sum(x, axis=0, keepdims=True)


def _norm_fwd(x, g, name, with_t=False):
    M = x.shape[0]
    ts = min(M, 512)

    def body(x_ref, g_ref, h_ref, *t_ref):
        xv = x_ref[...]
        r = lax.rsqrt(jnp.mean(xv * xv, axis=-1, keepdims=True) + EPS)
        h = xv * r * g_ref[...]
        h_ref[...] = _b(h)
        if with_t:
            t_ref[0][...] = _b(jnp.transpose(h))

    tile = pl.BlockSpec((ts, D), lambda i: (i, 0))
    shapes, specs = jax.ShapeDtypeStruct((M, D), bf16), tile
    if with_t:
        shapes, specs = (shapes, jax.ShapeDtypeStruct((D, M), bf16)), (tile, pl.BlockSpec((D, ts), lambda i: (0, i)))
    return pl.pallas_call(
        body, grid=(M // ts,), out_shape=shapes,
        in_specs=[tile, pl.BlockSpec((1, D), lambda i: (0, 0))],
        out_specs=specs, compiler_params=_cp(("parallel",)), name=name)(x, g)


def _norm_bwd(x, g, dh, dres, name):
    M = x.shape[0]
    ts = min(M, 512)
    with_dx = dres is not None

    def body(*refs):
        if with_dx:
            x_ref, g_ref, dh_ref, dres_ref, dx_ref, dg_ref = refs
        else:
            x_ref, g_ref, dh_ref, dg_ref = refs
        i = pl.program_id(0)
        xv = x_ref[...]
        r = lax.rsqrt(jnp.mean(xv * xv, axis=-1, keepdims=True) + EPS)
        xh = xv * r
        dh = dh_ref[...].astype(f32)
        part = jnp.broadcast_to(_colsum(dh * xh), (8, D))

        @pl.when(i == 0)
        def _():
            dg_ref[...] = part

        @pl.when(i > 0)
        def _():
            dg_ref[...] += part

        if with_dx:
            dxh = dh * g_ref[...]
            dx_ref[...] = dres_ref[...] + r * (dxh - xh * jnp.mean(dxh * xh, axis=-1, keepdims=True))

    tile = pl.BlockSpec((ts, D), lambda i: (i, 0))
    gspec = pl.BlockSpec((1, D), lambda i: (0, 0))
    acc = pl.BlockSpec((8, D), lambda i: (0, 0))
    if with_dx:
        return pl.pallas_call(
            body, grid=(M // ts,), out_shape=(jax.ShapeDtypeStruct((M, D), f32), jax.ShapeDtypeStruct((8, D), f32)),
            in_specs=[tile, gspec, tile, tile], out_specs=(tile, acc), compiler_params=_cp(("arbitrary",)), name=name)(x, g, dh, dres)
    return pl.pallas_call(
        body, grid=(M // ts,), out_shape=jax.ShapeDtypeStruct((8, D), f32),
        in_specs=[tile, gspec, tile], out_specs=acc, compiler_params=_cp(("arbitrary",)), name=name)(x, g, dh)


def _loss_head(x, g, tgt, name):
    M = x.shape[0]
    ts = min(M, 512)

    def body(x_ref, g_ref, t_ref, dx_ref, dg_ref, ls_ref):
        i = pl.program_id(0)
        xv = x_ref[...]
        gv = g_ref[...]
        r = lax.rsqrt(jnp.mean(xv * xv, axis=-1, keepdims=True) + EPS)
        xh = xv * r
        e = xh * gv - t_ref[...]
        lpart = 0.5 * jnp.sum(jnp.mean(e * e, axis=-1, keepdims=True), axis=0, keepdims=True)
        dy = e * (1.0 / D)
        dgp = jnp.broadcast_to(_colsum(dy * xh), (8, D))
        lp = jnp.broadcast_to(lpart, (8, 128))

        @pl.when(i == 0)
        def _():
            dg_ref[...] = dgp
            ls_ref[...] = lp

        @pl.when(i > 0)
        def _():
            dg_ref[...] += dgp
            ls_ref[...] += lp

        dxh = dy * gv
        dx_ref[...] = r * (dxh - xh * jnp.mean(dxh * xh, axis=-1, keepdims=True))

    tile = pl.BlockSpec((ts, D), lambda i: (i, 0))
    return pl.pallas_call(
        body, grid=(M // ts,),
        out_shape=(jax.ShapeDtypeStruct((M, D), f32), jax.ShapeDtypeStruct((8, D), f32), jax.ShapeDtypeStruct((8, 128), f32)),
        in_specs=[tile, pl.BlockSpec((1, D), lambda i: (0, 0)), tile],
        out_specs=(tile, pl.BlockSpec((8, D), lambda i: (0, 0)), pl.BlockSpec((8, 128), lambda i: (0, 0))),
        compiler_params=_cp(("arbitrary",)), name=name)(x, g, tgt)


def _mm(a, b, out_dtype, tm, tn, tk, name, trans_a=False, a_cols=None, b_cols=None, comm=None):
    if trans_a:
        K, M = a.shape
    else:
        M, K = a.shape
    N = b.shape[1]
    a0, b0 = 0, 0
    if a_cols is not None:
        a0, M = a_cols
    if b_cols is not None:
        b0, N = b_cols
    tm, tn, tk = min(tm, M), min(tn, N), min(tk, K)
    nk = K // tk
    a0, b0 = a0 // tm, b0 // tn
    grid = (M // tm, N // tn, nk)
    kind, carr = comm if comm is not None else (None, ())
    nc = len(carr)

    def body(*refs):
        a_ref, b_ref = refs[:2]
        o_ref = refs[2 + nc]
        acc_ref = refs[3 + 2 * nc]
        k = pl.program_id(2)
        if nc:
            cs_refs, cr_refs, sems = refs[2:2 + nc], refs[3 + nc:3 + 2 * nc], refs[4 + 2 * nc:]
            step = (pl.program_id(0) * grid[1] + pl.program_id(1)) * nk + k

            @pl.when(step == 0)
            def _():
                if kind == "gather":
                    _Gather(cs_refs, cr_refs, *sems).start()
                else:
                    for cp in _exchange_copies(cs_refs, cr_refs, *sems):
                        cp.start()
        av, bv = _b(a_ref[...]), _b(b_ref[...])
        part = _dot_tn(av, bv) if trans_a else _dot(av, bv)
        if nk == 1:
            o_ref[...] = part.astype(out_dtype)
        else:
            @pl.when(k == 0)
            def _():
                acc_ref[...] = part

            @pl.when(k > 0)
            def _():
                acc_ref[...] += part

            @pl.when(k == nk - 1)
            def _():
                o_ref[...] = acc_ref[...].astype(out_dtype)
        if nc:
            @pl.when(step == grid[0] * grid[1] * nk - 1)
            def _():
                if kind == "gather":
                    _Gather(cs_refs, cr_refs, *sems).finish()
                else:
                    for cp in _exchange_copies(cs_refs, cr_refs, *sems):
                        cp.wait()

    a_spec = pl.BlockSpec((tk, tm), lambda i, j, k: (k, i + a0)) if trans_a else pl.BlockSpec((tm, tk), lambda i, j, k: (i, k))
    anyspec = pl.BlockSpec(memory_space=pl.ANY)
    recv_shapes = tuple(jax.ShapeDtypeStruct(((N_DEV,) + c.shape) if kind == "gather" else c.shape, c.dtype) for c in carr)
    out = pl.pallas_call(
        body, grid=grid, out_shape=(jax.ShapeDtypeStruct((M, N), out_dtype),) + recv_shapes,
        in_specs=[a_spec, pl.BlockSpec((tk, tn), lambda i, j, k: (k, j + b0))] + [anyspec] * nc,
        out_specs=(pl.BlockSpec((tm, tn), lambda i, j, k: (i, j)),) + tuple([anyspec] * nc),
        scratch_shapes=[pltpu.VMEM((tm, tn), f32)] + (_exchange_scratch(nc) if nc else []),
        compiler_params=_cp(("arbitrary",) * 3 if nc else ("parallel", "parallel", "arbitrary"), VMEM_LIMIT), name=name)(a, b, *carr)
    return (out[0], list(out[1:])) if nc else out[0]


def _piece_spans(widths):
    spans, start = [], 0
    for w in widths:
        assert w % PIECE_COLS == 0
        spans.append((start, w // PIECE_COLS))
        start += w // PIECE_COLS
    return spans, start


def _mm_k_pieces(pieces, w, out_dtype, tm, name, comm=None):
    M, N = pieces[0][0].shape[0], w.shape[1]
    tm = min(tm, M)
    spans, nk = _piece_spans([wd for _, wd in pieces])
    npc = len(pieces)
    kind, carr = comm if comm is not None else (None, ())
    nc = len(carr)

    def body(*refs):
        p_refs, w_ref = refs[:npc], refs[npc]
        o_ref = refs[npc + 1 + nc]
        acc_ref = refs[npc + 2 + 2 * nc]
        k = pl.program_id(1)
        if nc:
            cs_refs, cr_refs, sems = refs[npc + 1:npc + 1 + nc], refs[npc + 2 + nc:npc + 2 + 2 * nc], refs[npc + 3 + 2 * nc:]
            step = pl.program_id(0) * nk + k

            @pl.when(step == 0)
            def _():
                for cp in _exchange_copies(cs_refs, cr_refs, *sems):
                    cp.start()

        @pl.when(k == 0)
        def _():
            acc_ref[...] = jnp.zeros_like(acc_ref)

        for p, (start, n) in enumerate(spans):
            @pl.when((k >= start) & (k < start + n))
            def _():
                acc_ref[...] += _dot(_b(p_refs[p][...]), w_ref[...])

        @pl.when(k == nk - 1)
        def _():
            o_ref[...] = acc_ref[...].astype(out_dtype)

        if nc:
            @pl.when(step == (M // tm) * nk - 1)
            def _():
                for cp in _exchange_copies(cs_refs, cr_refs, *sems):
                    cp.wait()

    assert kind in (None, "exchange")
    anyspec = pl.BlockSpec(memory_space=pl.ANY)
    p_specs = [pl.BlockSpec((tm, PIECE_COLS), functools.partial(lambda i, k, s, n: (i, jnp.clip(k - s, 0, n - 1)), s=s, n=n))
               for s, n in spans]
    out = pl.pallas_call(
        body, grid=(M // tm, nk),
        out_shape=(jax.ShapeDtypeStruct((M, N), out_dtype),) + tuple(jax.ShapeDtypeStruct(c.shape, c.dtype) for c in carr),
        in_specs=p_specs + [pl.BlockSpec((PIECE_COLS, N), lambda i, k: (k, 0))] + [anyspec] * nc,
        out_specs=(pl.BlockSpec((tm, N), lambda i, k: (i, 0)),) + tuple([anyspec] * nc),
        scratch_shapes=[pltpu.VMEM((tm, N), f32)] + (_exchange_scratch(nc) if nc else []),
        compiler_params=_cp(("arbitrary", "arbitrary"), VMEM_LIMIT), name=name)(*[a for a, _ in pieces], w, *carr)
    return (out[0], list(out[1:])) if nc else out[0]


def _mm_n_pieces(a, pieces, out_dtype, tm, tk, name):
    M, K = a.shape
    tm, tk = min(tm, M), min(tk, K)
    spans, nn = _piece_spans([wd for _, wd in pieces])
    npc, nk = len(pieces), K // tk

    def body(*refs):
        a_ref, p_refs, o_ref, acc_ref = refs[0], refs[1:1 + npc], refs[1 + npc], refs[2 + npc]
        j, k = pl.program_id(1), pl.program_id(2)

        @pl.when(k == 0)
        def _():
            acc_ref[...] = jnp.zeros_like(acc_ref)

        for p, (start, n) in enumerate(spans):
            @pl.when((j >= start) & (j < start + n))
            def _():
                acc_ref[...] += _dot(_b(a_ref[...]), _b(p_refs[p][...]))

        @pl.when(k == nk - 1)
        def _():
            o_ref[...] = acc_ref[...].astype(out_dtype)

    def p_map(i, j, k, s, n):
        inside = (j >= s) & (j < s + n)
        return jnp.where(inside, k, 0), jnp.clip(j - s, 0, n - 1)

    p_specs = [pl.BlockSpec((tk, PIECE_COLS), functools.partial(p_map, s=s, n=n)) for s, n in spans]
    return pl.pallas_call(
        body, grid=(M // tm, nn, nk), out_shape=jax.ShapeDtypeStruct((M, nn * PIECE_COLS), out_dtype),
        in_specs=[pl.BlockSpec((tm, tk), lambda i, j, k: (i, k))] + p_specs,
        out_specs=pl.BlockSpec((tm, PIECE_COLS), lambda i, j, k: (i, j)),
        scratch_shapes=[pltpu.VMEM((tm, PIECE_COLS), f32)],
        compiler_params=_cp(("parallel", "parallel", "arbitrary"), VMEM_LIMIT), name=name)(a, *[b for b, _ in pieces])


def _small_pars(b_fg, a_log, dt_bias):
    par = jnp.zeros((8, 128), f32)
    par = par.at[0, 0:8].set(b_fg).at[1, 8:12].set(a_log).at[2, 8:12].set(dt_bias)
    return par


def _small_prep(zs, par, name):
    S = zs.shape[0]
    ts = min(S, 512)

    def body(z_ref, par_ref, o_ref, carry_ref):
        i = pl.program_id(0)

        @pl.when(i == 0)
        def _():
            carry_ref[...] = jnp.zeros_like(carry_ref)

        z = z_ref[...]
        lane = lax.broadcasted_iota(jnp.int32, (ts, 128), 1)
        row = lax.broadcasted_iota(jnp.int32, (ts, 128), 0)
        za = z + par_ref[0:1, :]
        logf = jnp.minimum(za, 0.0) - jnp.log1p(jnp.exp(-jnp.abs(za)))
        glog = -jnp.exp(par_ref[1:2, :]) * _softplus(z + par_ref[2:3, :])
        x = jnp.where(lane < 8, logf, jnp.where(lane < 12, glog, 0.0))
        pos = jnp.where(lane < 8, row, row & (CH - 1))
        s = 1
        while s < ts:
            x = x + jnp.where(pos >= s, pltpu.roll(x, s, 0), 0.0)
            s *= 2
        tot = x + carry_ref[0:1, :]
        carry_ref[...] = jnp.broadcast_to(jnp.where(lane[0:1] < 8, tot[ts - 1:ts, :], 0.0), (8, 128))
        o_ref[...] = jnp.where(lane < 8, tot, jnp.where(lane < 12, x, jnp.where(lane < 16, _sig(z), 0.0)))

    return pl.pallas_call(
        body, grid=(S // ts,), out_shape=jax.ShapeDtypeStruct((S, 128), f32),
        in_specs=[pl.BlockSpec((ts, 128), lambda i: (i, 0)), pl.BlockSpec((8, 128), lambda i: (0, 0))],
        out_specs=pl.BlockSpec((ts, 128), lambda i: (i, 0)), scratch_shapes=[pltpu.VMEM((8, 128), f32)],
        compiler_params=_cp(("arbitrary",)), name=name)(zs, par)


def _small_bwd(zs, par, dfr, dfc, dsm, name):
    S = zs.shape[0]
    ts = min(S, 512)
    nt = S // ts

    def body(z_ref, par_ref, dfr_ref, dfc_ref, dsm_ref, dz_ref, acc_ref, carry_ref):
        i = pl.program_id(0)

        @pl.when(i == 0)
        def _():
            carry_ref[...] = jnp.zeros_like(carry_ref)

        z = z_ref[...]
        dsm_v = dsm_ref[...]
        lane = lax.broadcasted_iota(jnp.int32, (ts, 128), 1)
        row = lax.broadcasted_iota(jnp.int32, (ts, 128), 0)
        df = jnp.transpose(jnp.concatenate([dfr_ref[...], jnp.zeros((120, ts), f32)], axis=0))
        for p in range(4):
            dpair = dfc_ref[p]
            df = df - jnp.where(lane == 2 * p, dpair[:, 0:1], jnp.where(lane == 2 * p + 1, dpair[:, 64:65], 0.0))
        x = jnp.where(lane < 8, df, jnp.where(lane < 12, dsm_v, 0.0))
        pos = jnp.where(lane < 8, row, row & (CH - 1))
        seg = jnp.where(lane < 8, ts, CH)
        s = 1
        while s < ts:
            x = x + jnp.where(pos + s < seg, pltpu.roll(x, ts - s, 0), 0.0)
            s *= 2
        tot = x + carry_ref[0:1, :]
        carry_ref[...] = jnp.broadcast_to(jnp.where(lane[0:1] < 8, tot[0:1, :], 0.0), (8, 128))
        za = z + par_ref[0:1, :]
        daf = tot * _sig(-za)
        zb = z + par_ref[2:3, :]
        nea = -jnp.exp(par_ref[1:2, :])
        glog = nea * _softplus(zb)
        dba = x * nea * _sig(zb)
        beta = _sig(z)
        dbb = dsm_v * beta * (1.0 - beta)
        dz_ref[:, 0:128] = _b(jnp.where(lane < 8, daf, jnp.where(lane < 12, dba, jnp.where(lane < 16, dbb, 0.0))))
        dz_ref[:, 128:PIECE_COLS] = jnp.zeros((ts, PIECE_COLS - 128), bf16)
        r0 = _colsum(jnp.where(lane < 8, daf, 0.0))
        r1 = _colsum(jnp.where((lane >= 8) & (lane < 12), x * glog, 0.0))
        r2 = _colsum(jnp.where((lane >= 8) & (lane < 12), dba, 0.0))
        r8 = lax.broadcasted_iota(jnp.int32, (8, 128), 0)
        part = jnp.where(r8 == 0, r0, jnp.where(r8 == 1, r1, jnp.where(r8 == 2, r2, 0.0)))

        @pl.when(i == 0)
        def _():
            acc_ref[...] = part

        @pl.when(i > 0)
        def _():
            acc_ref[...] += part

    rev = pl.BlockSpec((ts, 128), lambda i: (nt - 1 - i, 0))
    rev4 = pl.BlockSpec((4, ts, 128), lambda i: (0, nt - 1 - i, 0))
    c8 = pl.BlockSpec((8, 128), lambda i: (0, 0))
    return pl.pallas_call(
        body, grid=(nt,), out_shape=(jax.ShapeDtypeStruct((S, PIECE_COLS), bf16), jax.ShapeDtypeStruct((8, 128), f32)),
        in_specs=[rev, c8, pl.BlockSpec((8, ts), lambda i: (0, nt - 1 - i)), rev4, rev],
        out_specs=(pl.BlockSpec((ts, PIECE_COLS), lambda i: (nt - 1 - i, 0)), c8),
        scratch_shapes=[pltpu.VMEM((8, 128), f32)],
        compiler_params=_cp(("arbitrary",)), name=name)(zs, par, dfr, dfc, dsm)


def _split3(x):
    hi = _b(x).astype(f32)
    r = x - hi
    mid = _b(r).astype(f32)
    return hi, mid, _b(r - mid).astype(f32)


FOX_PREP_ROWS = 512
FOX_TILE = 512
FOX_SKIP_LOG = -32.0


def _fox_prep(zb, sm, name):
    S = zb.shape[0]
    ts = min(S, FOX_PREP_ROWS)

    def body(q_ref, k_ref, f_ref, qa_ref, ka_ref, st_ref):
        lane = lax.broadcasted_iota(jnp.int32, (ts, 128), 1)
        lane8 = lax.broadcasted_iota(jnp.int32, (8, 128), 1)
        f = f_ref[...]
        st = jnp.zeros((8, 128), f32)
        for p in range(4):
            q = q_ref[:, p * 128:(p + 1) * 128].astype(f32) * FOX_SCALE
            k = k_ref[:, p * 128:(p + 1) * 128].astype(f32)
            for h in (0, 1):
                fcol = f[:, 2 * p + h:2 * p + h + 1]
                hi, mid, lo = _split3(fcol)
                own = (lane < 64) if h == 0 else (lane >= 64)
                nq = jnp.sqrt(_rowsum(jnp.where(own, q * q, 0.0)))
                nk = jnp.sqrt(_rowsum(jnp.where(own, k * k, 0.0)))
                stats = (jnp.max(nq, axis=0, keepdims=True), jnp.max(nk, axis=0, keepdims=True),
                         jnp.max(fcol, axis=0, keepdims=True), jnp.min(fcol, axis=0, keepdims=True),
                         jnp.min(-nq * nk, axis=0, keepdims=True))
                for si, val in enumerate(stats):
                    st = jnp.where(lane8 == 8 * si + 2 * p + h, val, st)
                o = 64 if h == 0 else 0
                ones_lo = (lane >= o) & (lane < o + 3)
                ones_hi = (lane >= o + 3) & (lane < o + 6)
                qaug = jnp.where(lane == o, hi, jnp.where(lane == o + 1, mid, jnp.where(lane == o + 2, lo, jnp.where(ones_hi, 1.0, 0.0))))
                kaug = jnp.where(lane == o + 3, -hi, jnp.where(lane == o + 4, -mid, jnp.where(lane == o + 5, -lo, jnp.where(ones_lo, 1.0, 0.0))))
                qa_ref[2 * p + h] = _b(jnp.where(own, q, qaug))
                ka_ref[2 * p + h] = _b(jnp.where(own, k, kaug))
        st_ref[...] = st

    out = jax.ShapeDtypeStruct((8, S, 128), bf16)
    return pl.pallas_call(
        body, grid=(S // ts,), out_shape=(out, out, jax.ShapeDtypeStruct((S // ts * 8, 128), f32)),
        in_specs=[pl.BlockSpec((ts, 512), lambda i: (i, 0)), pl.BlockSpec((ts, 512), lambda i: (i, 1)), pl.BlockSpec((ts, 128), lambda i: (i, 0))],
        out_specs=(pl.BlockSpec((8, ts, 128), lambda i: (0, i, 0)), pl.BlockSpec((8, ts, 128), lambda i: (0, i, 0)),
                   pl.BlockSpec((8, 128), lambda i: (i, 0))),
        compiler_params=_cp(("parallel",)), name=name)(zb, zb, sm)


def _fox_bound_table(st, S, T):
    ts = min(S, FOX_PREP_ROWS)
    g = T // ts
    nt = S // T
    s5 = st.reshape(S // ts, 8, 128)[:, 0, 0:40].reshape(nt, g, 5, 8)
    qn, kn, fmax = s5[:, :, 0].max(axis=1), s5[:, :, 1].max(axis=1), s5[:, :, 2].max(axis=1)
    fmin, lmin = s5[:, :, 3].min(axis=1), s5[:, :, 4].min(axis=1)
    e = qn[:, None] * kn[None, :] + fmax[:, None] - fmin[None, :] - lmin[:, None] + 1.0
    return jnp.transpose(e, (2, 0, 1)).reshape(8, nt * nt)


def _pair_rows(a, T):
    at = jnp.transpose(a)
    r8 = lax.broadcasted_iota(jnp.int32, (8, T), 0)
    return jnp.where(r8 == 0, at[0:1, :], at[64:65, :])


def _fox_fwd(qa, ka, zb, tab, name):
    S = zb.shape[0]
    T = min(S, FOX_TILE)
    nt = S // T

    def body(tab_ref, qa_ref, ka_ref, v_ref, o_ref, lset_ref, m_ref, l_ref, acc_ref):
        p, i = pl.program_id(0), pl.program_id(1)
        m_ref[...] = jnp.full_like(m_ref, NEG)
        l_ref[...] = jnp.zeros_like(l_ref)
        acc_ref[...] = jnp.zeros_like(acc_ref)
        row = lax.broadcasted_iota(jnp.int32, (T, T), 0)
        col = lax.broadcasted_iota(jnp.int32, (T, T), 1)

        def head_tile(h, j, masked):
            off = pl.multiple_of(j * T, T)
            s = _dot_nt(qa_ref[h], ka_ref[h, pl.ds(off, T), :])
            if masked:
                s = jnp.where(row >= col, s, NEG)
            m_old = m_ref[h]
            m_new = jnp.maximum(m_old, jnp.max(s, axis=1, keepdims=True))
            alpha = jnp.exp(m_old - m_new)
            pr = jnp.exp(s - jnp.tile(m_new, (1, T // 128)))
            l_ref[h] = alpha * l_ref[h] + _rowsum(pr)
            acc_ref[h] = alpha * acc_ref[h] + _dot(_b(pr), v_ref[pl.ds(off, T), :])
            m_ref[h] = m_new

        def step(j, c):
            for h in (0, 1):
                @pl.when(tab_ref[2 * p + h, i * nt + j] > FOX_SKIP_LOG)
                def _():
                    head_tile(h, j, False)
            return c

        lax.fori_loop(0, i, step, 0)
        for h in (0, 1):
            head_tile(h, i, True)
        lane2 = lax.broadcasted_iota(jnp.int32, (T, 128), 1)
        o_ref[...] = jnp.where(lane2 < 64, acc_ref[0] / l_ref[0], acc_ref[1] / l_ref[1])
        lse = jnp.where(lane2 < 64, m_ref[0] + jnp.log(l_ref[0]), m_ref[1] + jnp.log(l_ref[1]))
        lset_ref[0] = _pair_rows(lse, T)

    return pl.pallas_call(
        body, grid=(4, S // T),
        out_shape=(jax.ShapeDtypeStruct((S, 512), f32), jax.ShapeDtypeStruct((4, 8, S), f32)),
        in_specs=[pl.BlockSpec(memory_space=pltpu.SMEM), pl.BlockSpec((2, T, 128), lambda p, i: (p, i, 0)),
                  pl.BlockSpec((2, S, 128), lambda p, i: (p, 0, 0), pipeline_mode=pl.Buffered(1)),
                  pl.BlockSpec((S, 128), lambda p, i: (0, 8 + p), pipeline_mode=pl.Buffered(1))],
        out_specs=(pl.BlockSpec((T, 128), lambda p, i: (i, p)), pl.BlockSpec((1, 8, T), lambda p, i: (p, 0, i))),
        scratch_shapes=[pltpu.VMEM((2, T, 128), f32), pltpu.VMEM((2, T, 128), f32), pltpu.VMEM((2, T, 128), f32)],
        compiler_params=_cp(("arbitrary", "arbitrary"), VMEM_LIMIT), name=name)(tab, qa, ka, zb)


def _fox_bwd(qa, ka, zb, dob, lse_t, dl_t, tab, name, comm_sends=()):
    S = zb.shape[0]
    T = min(S, FOX_TILE)
    nq = S // T
    nc = len(comm_sends)

    def body(*refs):
        tab_ref, ka_ref, v_ref, qa_ref, do_ref, lt_ref, dt_ref = refs[:7]
        cs_refs = refs[7:7 + nc]
        dq_ref, dk_ref, dv_ref, dfc_ref, dfr_ref = refs[7 + nc:12 + nc]
        cr_refs = refs[12 + nc:12 + 2 * nc]
        dqa_ref, dka_ref, dva_ref, fs_ref = refs[12 + 2 * nc:16 + 2 * nc]
        p, j = pl.program_id(0), pl.program_id(1)
        if nc:
            @pl.when((p == 0) & (j == 0))
            def _():
                for cp in _exchange_copies(cs_refs, cr_refs, *refs[16 + 2 * nc:]):
                    cp.start()
        lane1 = lax.broadcasted_iota(jnp.int32, (1, 128), 1)
        lane2 = lax.broadcasted_iota(jnp.int32, (T, 128), 1)
        hm = (lane1 < 64, lane1 >= 64)
        v = v_ref[...]
        vsm = [jnp.where(hm[h], v, jnp.zeros_like(v)) for h in (0, 1)]
        ksm = [jnp.where(hm[h], ka_ref[h], jnp.zeros_like(v)) for h in (0, 1)]

        @pl.when(j == 0)
        def _():
            dqa_ref[...] = jnp.zeros_like(dqa_ref)
            dfr_ref[...] = jnp.zeros_like(dfr_ref)

        dka_ref[...] = jnp.zeros_like(dka_ref)
        dva_ref[...] = jnp.zeros_like(dva_ref)
        fs_ref[...] = jnp.zeros_like(fs_ref)
        row = lax.broadcasted_iota(jnp.int32, (T, T), 0)
        col = lax.broadcasted_iota(jnp.int32, (T, T), 1)

        def head_tile(h, i, masked):
            off = pl.multiple_of(i * T, T)
            dot_ = do_ref[pl.ds(off, T), :]
            hr = pl.ds(2 * p + h, 1)
            qt = qa_ref[h, pl.ds(off, T), :]
            s_t = _dot_nt(ka_ref[h], qt)
            if masked:
                s_t = jnp.where(col >= row, s_t, NEG)
            p_t = jnp.exp(s_t - lt_ref[hr, pl.ds(off, T)])
            dva_ref[h] += _dot(_b(p_t), dot_)
            dp_t = _dot_nt(vsm[h], dot_)
            ds_t = p_t * (dp_t - dt_ref[hr, pl.ds(off, T)])
            dsb = _b(ds_t)
            dka_ref[h] += _dot(dsb, qt)
            fs_ref[h] += _rowsum(ds_t)
            dfr_ref[0, pl.ds(h, 1), pl.ds(off, T)] += _colsum(ds_t)
            dqa_ref[pl.ds(off, T), :] += _dot_tn(dsb, ksm[h])

        def step(i, c):
            for h in (0, 1):
                @pl.when(tab_ref[2 * p + h, i * nq + j] > FOX_SKIP_LOG)
                def _():
                    head_tile(h, i, False)
            return c

        for h in (0, 1):
            head_tile(h, j, True)
        lax.fori_loop(j + 1, nq, step, 0)
        dk_ref[...] = _b(jnp.where(lane2 < 64, dka_ref[0], dka_ref[1]))
        dv_ref[...] = _b(jnp.where(lane2 < 64, dva_ref[0], dva_ref[1]))
        dfc_ref[0] = jnp.where(lane2 < 64, fs_ref[0], fs_ref[1])
        dq_ref[...] = _b(dqa_ref[pl.ds(pl.multiple_of(j * T, T), T), :] * FOX_SCALE)
        if nc:
            @pl.when((p == 3) & (j == nq - 1))
            def _():
                for cp in _exchange_copies(cs_refs, cr_refs, *refs[16 + 2 * nc:]):
                    cp.wait()

    one = pl.Buffered(1)
    res = pl.BlockSpec((8, S), lambda p, j: (0, 0), pipeline_mode=one)
    tk = pl.BlockSpec((T, 128), lambda p, j: (j, p))
    anyspec = pl.BlockSpec(memory_space=pl.ANY)
    outs = pl.pallas_call(
        body, grid=(4, nq),
        out_shape=(jax.ShapeDtypeStruct((S, 512), bf16), jax.ShapeDtypeStruct((S, 512), bf16), jax.ShapeDtypeStruct((S, 512), bf16),
                   jax.ShapeDtypeStruct((4, S, 128), f32), jax.ShapeDtypeStruct((4, 8, S), f32))
        + tuple(jax.ShapeDtypeStruct(c.shape, c.dtype) for c in comm_sends),
        in_specs=[pl.BlockSpec(memory_space=pltpu.SMEM),
                  pl.BlockSpec((2, T, 128), lambda p, j: (p, j, 0)), pl.BlockSpec((T, 128), lambda p, j: (j, 8 + p)),
                  pl.BlockSpec((2, S, 128), lambda p, j: (p, 0, 0), pipeline_mode=one),
                  pl.BlockSpec((S, 128), lambda p, j: (0, p), pipeline_mode=one), res, res] + [anyspec] * nc,
        out_specs=(tk, tk, tk, pl.BlockSpec((1, T, 128), lambda p, j: (p, j, 0)),
                   pl.BlockSpec((1, 8, S), lambda p, j: (p, 0, 0))) + tuple([anyspec] * nc),
        scratch_shapes=[pltpu.VMEM((S, 128), f32), pltpu.VMEM((2, T, 128), f32), pltpu.VMEM((2, T, 128), f32), pltpu.VMEM((2, T, 1), f32)]
        + (_exchange_scratch(nc) if nc else []),
        compiler_params=_cp(("arbitrary", "arbitrary"), VMEM_LIMIT), name=name)(tab, ka, zb, qa, dob, lse_t, dl_t, *comm_sends)
    return outs[:5], list(outs[5:])


def _head_rows(a):
    return a[:, 0:2, :].reshape(8, a.shape[2])


def _conv_taps(ext, x, w_ref, ts):
    y = x * w_ref[3:4, :]
    shifted = []
    for k in (1, 2, 3):
        xs = pltpu.roll(ext, k, 0)[8:]
        shifted.append(xs)
        y = y + xs * w_ref[3 - k:4 - k, :]
    return y, shifted


def _gdn_prep(zf, cw, name):
    S = zf.shape[0]
    ts = min(S, 512)

    def body(x_ref, w_ref, o_ref, tail_ref):
        i = pl.program_id(0)

        @pl.when(i == 0)
        def _():
            tail_ref[...] = jnp.zeros_like(tail_ref)

        x = x_ref[...]
        ext = jnp.concatenate([tail_ref[...], x], axis=0)
        y, _ = _conv_taps(ext, x, w_ref, ts)
        tail_ref[...] = x[ts - 8:, :]
        a = _silu(y)
        for hb in range(12):
            blk = a[:, hb * 128:(hb + 1) * 128]
            if hb < 8:
                blk = blk * lax.rsqrt(_rowsum(blk * blk) + EPS)
            if hb < 4:
                blk = blk * GDN_SCALE
            o_ref[:, hb * 128:(hb + 1) * 128] = blk

    return pl.pallas_call(
        body, grid=(S // ts,), out_shape=jax.ShapeDtypeStruct((S, 1536), f32),
        in_specs=[pl.BlockSpec((ts, 1536), lambda i: (i, 0)), pl.BlockSpec((8, 1536), lambda i: (0, 0))],
        out_specs=pl.BlockSpec((ts, 1536), lambda i: (i, 0)), scratch_shapes=[pltpu.VMEM((8, 1536), f32)],
        compiler_params=_cp(("arbitrary",), VMEM_LIMIT), name=name)(zf, cw)


def _gdn_prep_bwd(zf, cw, dg, name):
    S = zf.shape[0]
    ts = min(S, 512)
    nt = S // ts

    def body(x_ref, xp_ref, w_ref, dg_ref, dx_ref, dw_ref, head_ref):
        i = pl.program_id(0)

        @pl.when(i == 0)
        def _():
            head_ref[...] = jnp.zeros_like(head_ref)

        x = x_ref[...]
        prev = jnp.where(i == nt - 1, 0.0, xp_ref[...])
        ext = jnp.concatenate([prev, x], axis=0)
        y, shifted = _conv_taps(ext, x, w_ref, ts)
        a = _silu(y)
        das = []
        for hb in range(12):
            blk = a[:, hb * 128:(hb + 1) * 128]
            d = dg_ref[:, hb * 128:(hb + 1) * 128]
            if hb < 4:
                d = d * GDN_SCALE
            if hb < 8:
                r = lax.rsqrt(_rowsum(blk * blk) + EPS)
                n = blk * r
                d = r * (d - n * _rowsum(d * n))
            das.append(d)
        dy = jnp.concatenate(das, axis=1) * _dsilu(y)
        extd = jnp.concatenate([dy, head_ref[...]], axis=0)
        dx = dy * w_ref[3:4, :]
        for k in (1, 2, 3):
            dx = dx + pltpu.roll(extd, ts + 8 - k, 0)[:ts] * w_ref[3 - k:4 - k, :]
        head_ref[...] = dy[0:8, :]
        dx_ref[...] = _b(dx)
        r8 = lax.broadcasted_iota(jnp.int32, (8, 1536), 0)
        part = jnp.where(r8 == 3, _colsum(dy * x), 0.0)
        for k in (1, 2, 3):
            part = jnp.where(r8 == 3 - k, _colsum(dy * shifted[k - 1]), part)

        @pl.when(i == 0)
        def _():
            dw_ref[...] = part

        @pl.when(i > 0)
        def _():
            dw_ref[...] += part

    rev = pl.BlockSpec((ts, 1536), lambda i: (nt - 1 - i, 0))
    prev8 = pl.BlockSpec((8, 1536), lambda i: (jnp.maximum((nt - 1 - i) * (ts // 8) - 1, 0), 0))
    w8 = pl.BlockSpec((8, 1536), lambda i: (0, 0))
    return pl.pallas_call(
        body, grid=(nt,), out_shape=(jax.ShapeDtypeStruct((S, 1536), bf16), jax.ShapeDtypeStruct((8, 1536), f32)),
        in_specs=[rev, prev8, w8, rev], out_specs=(rev, w8), scratch_shapes=[pltpu.VMEM((8, 1536), f32)],
        compiler_params=_cp(("arbitrary",), VMEM_LIMIT), name=name)(zf, zf, cw, dg)


def _tri_inv(a, row, col):
    same = (row >> 4) == (col >> 4)
    dm = jnp.where(same, a, 0.0)
    lo = a - dm
    eye = jnp.where(row == col, 1.0, 0.0)
    d2 = _hi_b(dm, dm)
    d4 = _hi_b(d2, d2)
    d8 = _hi_b(d4, d4)
    x0 = _hi_b(_hi_b(eye - dm, eye + d2), _hi_b(eye + d4, eye + d8))
    n = _hi_b(x0, lo)
    n2 = _hi_b(n, n)
    return _hi_b(_hi_b(eye - n, eye + n2), x0)


def _bd(a, b):
    return lax.dot_general(a, b, (((2,), (1,)), ((0,), (0,))), preferred_element_type=f32)


def _bd_nt(a, b):
    return lax.dot_general(a, b, (((2,), (2,)), ((0,), (0,))), preferred_element_type=f32)


def _bd_tn(a, b):
    return lax.dot_general(a, b, (((1,), (1,)), ((0,), (0,))), preferred_element_type=f32)


def _hi_b_nt(a, b):
    return _mm3(a, b, (((2,), (2,)), ((0,), (0,))))


def _hi_b_tn(a, b):
    return _mm3(a, b, (((1,), (1,)), ((0,), (0,))))


def _gdn_local(x_ref, sm_ref, gt_ref, row, col, cps=1):
    idx = [(c, h) for c in range(cps) for h in range(4)]

    def rows(c):
        return slice(c * CH, (c + 1) * CH)

    q = jnp.stack([x_ref[rows(c), h * 128:(h + 1) * 128] for c, h in idx])
    k = jnp.stack([x_ref[rows(c), 512 + h * 128:512 + (h + 1) * 128] for c, h in idx])
    v = jnp.stack([x_ref[rows(c), 1024 + h * 128:1024 + (h + 1) * 128] for c, h in idx])
    gc = jnp.stack([sm_ref[rows(c), 8 + h:9 + h] for c, h in idx])
    beta = jnp.stack([sm_ref[rows(c), 12 + h:13 + h] for c, h in idx])
    gr = jnp.stack([gt_ref[h, c] for c, h in idx])
    eg = jnp.exp(gc)
    gl = gc[:, CH - 1:CH, :]
    dec = jnp.exp(gl - gc)
    gm = gc - gr
    gam_i = jnp.exp(jnp.where(row >= col, gm, -jnp.inf))
    gam_s = jnp.where(row > col, gam_i, 0.0)
    kb = k * beta
    return dict(q=q, k=k, v=v, beta=beta, eg=eg, egl=jnp.exp(gl), dec=dec, gam_i=gam_i, gam_s=gam_s,
                kb=kb, vb=v * beta, kbg=kb * eg, qdec=q * eg, kdec=k * dec,
                a=_bd_nt(_b(kb), _b(k)) * gam_s, aqk=_bd_nt(_b(q), _b(k)) * gam_i)


GDN_FWD_CHUNKS = 8


def _gdn_fwd(gqkv, sm, gt4, name, cps=GDN_FWD_CHUNKS):
    S = gqkv.shape[0]
    N = S // CH
    cps = min(cps, N)
    R = cps * CH

    def body(x_ref, sm_ref, gt_ref, o_ref, t_ref, st_ref, s_ref):
        n = pl.program_id(0)

        @pl.when(n == 0)
        def _():
            s_ref[...] = jnp.zeros_like(s_ref)

        row = lax.broadcasted_iota(jnp.int32, (CH, CH), 0)
        col = lax.broadcasted_iota(jnp.int32, (CH, CH), 1)
        c = _gdn_local(x_ref, sm_ref, gt_ref, row, col, cps)
        t = _tri_inv(c["a"], row, col)
        uw = _hi_b(t, jnp.concatenate([c["vb"], c["kbg"]], axis=2))
        u, w = uw[:, :, :128], uw[:, :, 128:]
        for ci in range(cps):
            sl = slice(4 * ci, 4 * ci + 4)
            rs = slice(ci * CH, (ci + 1) * CH)
            st = s_ref[...]
            st_ref[ci] = st
            sb = _b(st)
            vnew = u[sl] - _bd(_b(w[sl]), sb)
            o = _bd(_b(c["qdec"][sl]), sb) + _bd(_b(c["aqk"][sl]), _b(vnew))
            for h in range(4):
                o_ref[rs, h * 128:(h + 1) * 128] = o[h]
                t_ref[h, rs, :] = t[4 * ci + h]
            s_ref[...] = st * c["egl"][sl] + _bd_tn(_b(c["kdec"][sl]), _b(vnew))

    return pl.pallas_call(
        body, grid=(N // cps,),
        out_shape=(jax.ShapeDtypeStruct((S, 512), f32), jax.ShapeDtypeStruct((4, S, CH), f32), jax.ShapeDtypeStruct((N, 4, 128, 128), f32)),
        in_specs=[pl.BlockSpec((R, 1536), lambda n: (n, 0)), pl.BlockSpec((R, 128), lambda n: (n, 0)),
                  pl.BlockSpec((4, cps, 1, CH), lambda n: (0, n, 0, 0))],
        out_specs=(pl.BlockSpec((R, 512), lambda n: (n, 0)), pl.BlockSpec((4, R, CH), lambda n: (0, n, 0)),
                   pl.BlockSpec((cps, 4, 128, 128), lambda n: (n, 0, 0, 0))),
        scratch_shapes=[pltpu.VMEM((4, 128, 128), f32)], compiler_params=_cp(("arbitrary",)), name=name)(gqkv, sm, gt4)


GDN_BWD_CHUNKS = 4


def _gdn_bwd(gqkv, sm, gt4, tinv, states, do, name, cps=GDN_BWD_CHUNKS):
    S = gqkv.shape[0]
    N = S // CH
    cps = min(cps, N)
    R = cps * CH

    def body(x_ref, sm_ref, gt_ref, t_ref, st_ref, do_ref, dx_ref, dsm_ref, ds_ref):
        n = pl.program_id(0)

        @pl.when(n == 0)
        def _():
            ds_ref[...] = jnp.zeros_like(ds_ref)

        row = lax.broadcasted_iota(jnp.int32, (CH, CH), 0)
        col = lax.broadcasted_iota(jnp.int32, (CH, CH), 1)
        row1 = lax.broadcasted_iota(jnp.int32, (CH, 1), 0)
        lane = lax.broadcasted_iota(jnp.int32, (CH, 128), 1)
        ones = jnp.ones((4 * cps, CH, 128), f32)
        idx = [(ci, h) for ci in range(cps) for h in range(4)]
        c = _gdn_local(x_ref, sm_ref, gt_ref, row, col, cps)
        q, k, v, beta, eg = c["q"], c["k"], c["v"], c["beta"], c["eg"]
        t = jnp.stack([t_ref[h, ci * CH:(ci + 1) * CH, :] for ci, h in idx])
        uw = _hi_b(t, jnp.concatenate([c["vb"], c["kbg"]], axis=2))
        u, w = uw[:, :, :128], uw[:, :, 128:]
        st = st_ref[...].reshape(4 * cps, 128, 128)
        sb = _b(st)
        vnew = u - _bd(_b(w), sb)
        dob = _b(jnp.stack([do_ref[ci * CH:(ci + 1) * CH, h * 128:(h + 1) * 128] for ci, h in idx]))
        vnb = _b(vnew)
        dqdec = _bd_nt(dob, sb)
        daqk = jnp.where(row >= col, _bd_nt(dob, vnb), 0.0)
        qd_do = _bd_tn(_b(c["qdec"]), dob)
        aqk_do = _bd_tn(_b(c["aqk"]), dob)
        kdecb, wb = _b(c["kdec"]), _b(w)
        dvnew_l, dkdec_l, dgl_l = [None] * cps, [None] * cps, [None] * cps
        for ci in reversed(range(cps)):
            sl = slice(4 * ci, 4 * ci + 4)
            dsp = ds_ref[...]
            dspb = _b(dsp)
            dvn = _bd(kdecb[sl], dspb) + aqk_do[sl]
            dvnew_l[ci] = dvn
            dkdec_l[ci] = _bd_nt(vnb[sl], dspb)
            dgl_l[ci] = c["egl"][sl] * jnp.sum(dsp * st[sl], axis=(1, 2), keepdims=True)
            ds_ref[...] = dsp * c["egl"][sl] + qd_do[sl] - _bd_tn(wb[sl], _b(dvn))
        dvnew = jnp.concatenate(dvnew_l, axis=0)
        dkdec = jnp.concatenate(dkdec_l, axis=0)
        dgl = jnp.concatenate(dgl_l, axis=0)
        dw = -_bd_nt(_b(dvnew), sb)
        duw = _hi_b_tn(t, jnp.concatenate([dvnew, dw], axis=2))
        dvb, dkbg = duw[:, :, :128], duw[:, :, 128:]
        da = -jnp.where(row > col, _hi_b_nt(duw, uw), 0.0)
        dp = da * c["gam_s"]
        dqk = daqk * c["gam_i"]
        m = da * c["a"] + daqk * c["aqk"]
        csum = _hi_b_tn(m, ones)[:, :, 0:1]
        kk = dkdec * c["kdec"]

        def lsum(a):
            return jnp.sum(a, axis=2, keepdims=True)

        dgv = lsum(m) - csum + lsum(dqdec * c["qdec"]) - lsum(kk) + lsum(dkbg * c["kbg"])
        dgv = dgv + jnp.where(row1 == CH - 1, dgl + jnp.sum(kk, axis=(1, 2), keepdims=True), 0.0)
        dpb, dqkb = _b(dp), _b(dqk)
        dkb = _bd(dpb, _b(k)) + dkbg * eg
        dk = _bd_tn(dpb, _b(c["kb"])) + _bd_tn(dqkb, _b(q)) + dkdec * c["dec"] + dkb * beta
        dq = _bd(dqkb, _b(k)) + dqdec * eg
        dbeta = lsum(dkb * k) + lsum(dvb * v)
        dv = dvb * beta
        for ci in range(cps):
            rs = slice(ci * CH, (ci + 1) * CH)
            dsm = jnp.zeros((CH, 128), f32)
            for h in range(4):
                b = 4 * ci + h
                dx_ref[rs, h * 128:(h + 1) * 128] = dq[b]
                dx_ref[rs, 512 + h * 128:512 + (h + 1) * 128] = dk[b]
                dx_ref[rs, 1024 + h * 128:1024 + (h + 1) * 128] = dv[b]
                dsm = jnp.where(lane == 8 + h, dgv[b], jnp.where(lane == 12 + h, dbeta[b], dsm))
            dsm_ref[rs, :] = dsm

    G = N // cps
    return pl.pallas_call(
        body, grid=(G,), out_shape=(jax.ShapeDtypeStruct((S, 1536), f32), jax.ShapeDtypeStruct((S, 128), f32)),
        in_specs=[pl.BlockSpec((R, 1536), lambda n: (G - 1 - n, 0)), pl.BlockSpec((R, 128), lambda n: (G - 1 - n, 0)),
                  pl.BlockSpec((4, cps, 1, CH), lambda n: (0, G - 1 - n, 0, 0)), pl.BlockSpec((4, R, CH), lambda n: (0, G - 1 - n, 0)),
                  pl.BlockSpec((cps, 4, 128, 128), lambda n: (G - 1 - n, 0, 0, 0)), pl.BlockSpec((R, 512), lambda n: (G - 1 - n, 0))],
        out_specs=(pl.BlockSpec((R, 1536), lambda n: (G - 1 - n, 0)), pl.BlockSpec((R, 128), lambda n: (G - 1 - n, 0))),
        scratch_shapes=[pltpu.VMEM((4, 128, 128), f32)], compiler_params=_cp(("arbitrary",), VMEM_LIMIT), name=name)(gqkv, sm, gt4, tinv, states, do)


MERGE_ROWS = 256
MERGE_FWD_ROWS = 512


def _mem_attn(q, kv_ref, h):
    s = _dot_nt(q, kv_ref[:, h * 128:(h + 1) * 128]) * MEM_SCALE
    e = jnp.exp(s - jnp.max(s, axis=1, keepdims=True))
    return e / _rowsum(e)


def _gdn_out_norm(ob):
    r = lax.rsqrt(jnp.mean(ob * ob, axis=-1, keepdims=True) + EPS)
    return ob * r, r


def _merge_fwd(x, oa, ob, zb, zf, kv, b_merge, gdn_g, w_branch, w_out, name):
    S = x.shape[0]
    ts = min(S, MERGE_FWD_ROWS)

    def body(x_ref, oa_ref, ob_ref, mq_ref, az_ref, bz_ref, mz_ref, gt_ref, kv_ref, bm_ref, gg_ref, wb_ref, wo_ref,
             xo_ref, y_ref, mg_ref):
        y_ref[:, 0:512] = _b(oa_ref[...] * _silu(az_ref[...]))
        for h in range(4):
            sl = slice(h * 128, (h + 1) * 128)
            nb, _ = _gdn_out_norm(ob_ref[:, sl])
            y_ref[:, 512 + h * 128:512 + (h + 1) * 128] = _b(nb * gg_ref[...] * _silu(bz_ref[:, sl]))
            pm = _mem_attn(mq_ref[:, sl], kv_ref, h)
            om = _dot(_b(pm), kv_ref[:, 512 + h * 128:512 + (h + 1) * 128])
            y_ref[:, 1024 + h * 128:1024 + (h + 1) * 128] = _b(om * _silu(mz_ref[:, sl]))
        merged = jnp.zeros((ts, D), f32)
        for n in range(3):
            gate = _sig(gt_ref[:, n * D:(n + 1) * D] + bm_ref[:, n * D:(n + 1) * D])
            merged = merged + gate * _dot(y_ref[:, n * 512:(n + 1) * 512], wb_ref[n])
        mb = _b(merged)
        mg_ref[...] = mb
        xo_ref[...] = x_ref[...] + _dot(mb, wo_ref[...])

    def col(w, c):
        return pl.BlockSpec((ts, w), lambda i: (i, c))

    def full(shape):
        return pl.BlockSpec(shape, lambda i: tuple(0 for _ in shape))

    return pl.pallas_call(
        body, grid=(S // ts,),
        out_shape=(jax.ShapeDtypeStruct((S, D), f32), jax.ShapeDtypeStruct((S, 1536), bf16), jax.ShapeDtypeStruct((S, D), bf16)),
        in_specs=[col(D, 0), col(512, 0), col(512, 0), col(512, 3), col(512, 3), col(512, 4), col(512, 5), col(3072, 1),
                  full((256, D)), full((1, 3072)), full((1, 128)), full((3, 512, D)), full((D, D))],
        out_specs=(col(D, 0), col(1536, 0), col(D, 0)),
        compiler_params=_cp(("parallel",), VMEM_LIMIT), name=name)(x, oa, ob, zb, zf, zf, zf, zf, kv, b_merge, gdn_g, w_branch, w_out)


def _merge_bwd(dout, ycat, oa, ob, zb, zf, kv, b_merge, gdn_g, w_branch, w_branch_t, w_out_t, name):
    S = dout.shape[0]
    ts = min(S, MERGE_ROWS)

    def body(do_ref, y_ref, oa_ref, ob_ref, mq_ref, az_ref, bz_ref, mz_ref, gt_ref, kv_ref, bm_ref, gg_ref, wb_ref, wbt_ref, wot_ref,
             dpj_ref, dz_ref, dmq_ref, dlt_ref, doab_ref, dob_ref, dkv_ref, dbm_ref, dgg_ref):
        i = pl.program_id(0)
        dmerged = _dot(_b(do_ref[...]), wot_ref[...])
        dys = []
        dbm_parts = []
        for n in range(3):
            cs = slice(n * D, (n + 1) * D)
            gate = _sig(gt_ref[:, cs] + bm_ref[:, cs])
            proj = _dot(y_ref[:, n * 512:(n + 1) * 512], wb_ref[n])
            dlogit = dmerged * proj * gate * (1.0 - gate)
            dz_ref[:, 1536 + n * D:1536 + (n + 1) * D] = _b(dlogit)
            dbm_parts.append(_colsum(dlogit))
            dproj = _b(dmerged * gate)
            dpj_ref[:, cs] = dproj
            dys.append(_dot(dproj, wbt_ref[n]))
        dbm = jnp.broadcast_to(jnp.concatenate(dbm_parts, axis=1), (8, 3072))
        az = az_ref[...]
        oa = oa_ref[...]
        doa = dys[0] * _silu(az)
        doab_ref[...] = _b(doa)
        prod = doa * oa
        lane = lax.broadcasted_iota(jnp.int32, (ts, 128), 1)
        dl = jnp.zeros((ts, 128), f32)
        for p in range(4):
            blk = prod[:, p * 128:(p + 1) * 128]
            dl = jnp.where(lane == 2 * p, _rowsum(jnp.where(lane < 64, blk, 0.0)),
                           jnp.where(lane == 2 * p + 1, _rowsum(jnp.where(lane >= 64, blk, 0.0)), dl))
        dlt_ref[...] = jnp.transpose(dl)[0:8, :]
        dz_ref[:, 0:512] = _b(dys[0] * oa * _dsilu(az))
        gg = gg_ref[...]
        dgg = jnp.zeros((1, 128), f32)
        dkv_parts_k, dkv_parts_v = [], []
        for h in range(4):
            sl = slice(h * 128, (h + 1) * 128)
            bz = bz_ref[:, sl]
            dyb = dys[1][:, sl]
            nb, r = _gdn_out_norm(ob_ref[:, sl])
            dz_ref[:, 512 + h * 128:512 + (h + 1) * 128] = _b(dyb * nb * gg * _dsilu(bz))
            dng = dyb * _silu(bz)
            dgg = dgg + _colsum(dng * nb)
            dnb = dng * gg
            dob_ref[:, sl] = r * (dnb - nb * jnp.mean(dnb * nb, axis=-1, keepdims=True))
            mz = mz_ref[:, sl]
            dym = dys[2][:, sl]
            q = mq_ref[:, sl]
            kh = kv_ref[:, sl]
            vh = kv_ref[:, 512 + h * 128:512 + (h + 1) * 128]
            pm = _mem_attn(q, kv_ref, h)
            pmb = _b(pm)
            om = _dot(pmb, vh)
            dz_ref[:, 1024 + h * 128:1024 + (h + 1) * 128] = _b(dym * om * _dsilu(mz))
            dom = _b(dym * _silu(mz))
            dkv_parts_v.append(_dot_tn(pmb, dom))
            dpm = _dot_nt(dom, vh)
            dsm = _b(pm * (dpm - _rowsum(dpm * pm)) * MEM_SCALE)
            dmq_ref[:, sl] = _b(_dot(dsm, kh))
            dkv_parts_k.append(_dot_tn(dsm, q))
        dkv = jnp.concatenate(dkv_parts_k + dkv_parts_v, axis=1)
        dggb = jnp.broadcast_to(dgg, (8, 128))

        @pl.when(i == 0)
        def _():
            dkv_ref[...] = dkv
            dbm_ref[...] = dbm
            dgg_ref[...] = dggb

        @pl.when(i > 0)
        def _():
            dkv_ref[...] += dkv
            dbm_ref[...] += dbm
            dgg_ref[...] += dggb

    def col(w, c):
        return pl.BlockSpec((ts, w), lambda i: (i, c))

    def full(shape):
        return pl.BlockSpec(shape, lambda i: tuple(0 for _ in shape))

    return pl.pallas_call(
        body, grid=(S // ts,),
        out_shape=(jax.ShapeDtypeStruct((S, 3072), bf16), jax.ShapeDtypeStruct((S, DZ_MERGE_COLS), bf16), jax.ShapeDtypeStruct((S, 512), bf16),
                   jax.ShapeDtypeStruct((8, S), f32), jax.ShapeDtypeStruct((S, 512), bf16), jax.ShapeDtypeStruct((S, 512), f32),
                   jax.ShapeDtypeStruct((256, D), f32), jax.ShapeDtypeStruct((8, 3072), f32), jax.ShapeDtypeStruct((8, 128), f32)),
        in_specs=[col(D, 0), col(1536, 0), col(512, 0), col(512, 0), col(512, 3), col(512, 3), col(512, 4), col(512, 5), col(3072, 1),
                  full((256, D)), full((1, 3072)), full((1, 128)), full((3, 512, D)), full((3, D, 512)), full((D, D))],
        out_specs=(col(3072, 0), col(4608, 0), col(512, 0), pl.BlockSpec((8, ts), lambda i: (0, i)), col(512, 0), col(512, 0),
                   full((256, D)), full((8, 3072)), full((8, 128))),
        compiler_params=_cp(("arbitrary",), VMEM_LIMIT), name=name)(
            dout, ycat, oa, ob, zb, zf, zf, zf, zf, kv, b_merge, gdn_g, w_branch, w_branch_t, w_out_t)


def _mesh_pos():
    return lax.axis_index("x"), lax.axis_index("y"), lax.axis_index("c")


class _Gather:
    def __init__(self, x_refs, out_refs, send_sems, recv_sems, local_sems):
        self.n = len(x_refs)
        self.x_refs, self.out_refs = x_refs, out_refs
        self.send_sems, self.recv_sems, self.local_sems = send_sems, recv_sems, local_sems
        mx, my, mc = _mesh_pos()
        self.mc = mc
        self.me, self.sibling = (mx, my, mc), (mx, my, 1 - mc)
        self.chips = [(1 - mx, my), (mx, 1 - my), (1 - mx, 1 - my)]

    def copy(self, a, k, block, to, src=None):
        px, py, pc = block
        slot = self.out_refs[a].at[4 * px + 2 * py + pc]
        return pltpu.make_async_remote_copy(
            src_ref=slot if src is None else src, dst_ref=slot, send_sem=self.send_sems.at[7 * a + k],
            recv_sem=self.recv_sems.at[7 * a + k], device_id=to, device_id_type=pl.DeviceIdType.MESH)

    def own(self):
        mx, my, mc = self.me
        mine = [pltpu.make_async_copy(self.x_refs[a], self.out_refs[a].at[4 * mx + 2 * my + mc], self.local_sems.at[a])
                for a in range(self.n)]
        first = []
        for a in range(self.n):
            first.append(self.copy(a, 0, self.me, self.sibling, src=self.x_refs[a]))
            first += [self.copy(a, 1 + j, self.me, (*chip, self.mc), src=self.x_refs[a]) for j, chip in enumerate(self.chips)]
        return mine, first

    def start(self):
        mine, first = self.own()
        for cp in mine + first:
            cp.start()

    def finish(self):
        mine, first = self.own()
        passed = []
        for j, chip in enumerate(self.chips):
            for a in range(self.n):
                self.copy(a, 1 + j, (*chip, self.mc), self.me).wait_recv()
                fwd = self.copy(a, 4 + j, (*chip, self.mc), self.sibling)
                fwd.start()
                passed.append(fwd)
        for a in range(self.n):
            self.copy(a, 0, self.sibling, self.me).wait_recv()
            for j, chip in enumerate(self.chips):
                self.copy(a, 4 + j, (*chip, 1 - self.mc), self.me).wait_recv()
        for cp in first + passed:
            cp.wait_send()
        for cp in mine:
            cp.wait()


def _all_gather(xs, name):
    n = len(xs)

    def body(*refs):
        g = _Gather(refs[:n], refs[n:2 * n], *refs[2 * n:])
        g.start()
        g.finish()

    anyspec = pl.BlockSpec(memory_space=pl.ANY)
    return pl.pallas_call(
        body, out_shape=tuple(jax.ShapeDtypeStruct((N_DEV,) + x.shape, x.dtype) for x in xs),
        in_specs=[anyspec] * n, out_specs=tuple([anyspec] * n), scratch_shapes=_exchange_scratch(n), name=name)(*xs)


def _exchange_copies(s_refs, r_refs, send_sems, recv_sems, local_sems):
    n = len(s_refs)
    mx, my, mc = _mesh_pos()
    me_id = 4 * mx + 2 * my + mc
    copies = [pltpu.make_async_copy(s_refs[a].at[me_id], r_refs[a].at[me_id], local_sems.at[a]) for a in range(n)]
    for k in range(1, N_DEV):
        px = 1 - mx if k & 4 else mx
        py = 1 - my if k & 2 else my
        pc = 1 - mc if k & 1 else mc
        for a in range(n):
            copies.append(pltpu.make_async_remote_copy(
                src_ref=s_refs[a].at[4 * px + 2 * py + pc], dst_ref=r_refs[a].at[me_id],
                send_sem=send_sems.at[7 * a + k - 1], recv_sem=recv_sems.at[7 * a + k - 1],
                device_id=(px, py, pc), device_id_type=pl.DeviceIdType.MESH))
    return copies


def _exchange_scratch(n):
    return [pltpu.SemaphoreType.DMA((7 * n,)), pltpu.SemaphoreType.DMA((7 * n,)), pltpu.SemaphoreType.DMA((n,))]


def _exchange(sends, name):
    n = len(sends)

    def body(*refs):
        copies = _exchange_copies(refs[:n], refs[n:2 * n], *refs[2 * n:])
        for cp in copies:
            cp.start()
        for cp in copies:
            cp.wait()

    anyspec = pl.BlockSpec(memory_space=pl.ANY)
    return pl.pallas_call(
        body, out_shape=tuple(jax.ShapeDtypeStruct(s.shape, s.dtype) for s in sends),
        in_specs=[anyspec] * n, out_specs=tuple([anyspec] * n), scratch_shapes=_exchange_scratch(n), name=name)(*sends)


ADAMW_BLOCK_BYTES = 4 * 1024 * 1024


def _adamw(parts, w, m, v, name):
    _, R, C = parts.shape
    tr = R
    for t in (1024, 512, 256, 128, 64, 32, 16, 8):
        if R % t == 0 and N_DEV * t * C * 4 <= ADAMW_BLOCK_BYTES:
            tr = t
            break

    def body(p_ref, w_ref, m_ref, v_ref, g_ref, d_ref, nm_ref, nv_ref):
        g = p_ref[0].astype(f32)
        for j in range(1, N_DEV):
            g = g + p_ref[j].astype(f32)
        mn = ADAM_B1 * m_ref[...] + (1.0 - ADAM_B1) * g
        vn = ADAM_B2 * v_ref[...] + (1.0 - ADAM_B2) * jnp.square(g)
        m_hat = mn / (1.0 - ADAM_B1 ** ADAM_STEP)
        v_hat = vn / (1.0 - ADAM_B2 ** ADAM_STEP)
        g_ref[...] = g
        d_ref[...] = -ADAM_LR * (m_hat / (jnp.sqrt(v_hat) + ADAM_EPS) + ADAM_WD * w_ref[...])
        nm_ref[...] = mn
        nv_ref[...] = vn

    t2 = pl.BlockSpec((tr, C), lambda i: (i, 0))
    out = jax.ShapeDtypeStruct((R, C), f32)
    return pl.pallas_call(
        body, grid=(R // tr,), out_shape=(out, out, out, out),
        in_specs=[pl.BlockSpec((N_DEV, tr, C), lambda i: (0, i, 0)), t2, t2, t2], out_specs=(t2, t2, t2, t2),
        compiler_params=_cp(("parallel",), VMEM_LIMIT), name=name)(parts, w, m, v)


def _as2d(a):
    return a.reshape(-1, a.shape[-1])


def _perm_cols(w, order=_ORDER):
    parts = [w[..., _COLS[n][0]:_COLS[n][1]] for n in order]
    pad = jnp.zeros(w.shape[:-1] + (N_ALL - N_IN,), w.dtype)
    return jnp.concatenate(parts + [pad], axis=-1)


def _unperm_cols(w, order=_ORDER):
    pieces, off = {}, 0
    for n in order:
        width = _COLS[n][1] - _COLS[n][0]
        pieces[n] = w[..., off:off + width]
        off += width
    return jnp.concatenate([pieces[n] for n in sorted(_COLS, key=lambda n: _COLS[n][0])], axis=-1)


_SMALL_ROWS = 16


def _pack_small(t):
    z = jnp.zeros((D,), f32)
    misc = z.at[0:16].set(t["b_fg"].reshape(-1)).at[16:24].set(t["a_log"].reshape(-1)).at[24:32].set(t["dt_bias"].reshape(-1))
    misc = misc.at[128:384].set(t["gdn_norm_g"].reshape(-1))
    if "extra" in t:
        misc = misc.at[512].set(t["extra"])
    rows = [t["norm_g"], t["b_merge"].reshape(6, D), t["mem_norm_g"], t["final_norm_g"][None], misc[None],
            jnp.zeros((_SMALL_ROWS - 12, D), f32)]
    return jnp.concatenate(rows, axis=0)


def _unpack_small(a):
    misc = a[11]
    return dict(norm_g=a[0:2], b_merge=a[2:8].reshape(2, 3072), mem_norm_g=a[8:10], final_norm_g=a[10],
                b_fg=misc[0:16].reshape(2, 8), a_log=misc[16:24].reshape(2, 4), dt_bias=misc[24:32].reshape(2, 4),
                gdn_norm_g=misc[128:384].reshape(2, 128), extra=misc[512])


def _layer_fwd(l, x, mem, p, gather_next=None):
    sfx = f"_l{l}"
    h, ht = _norm_fwd(x, p["norm_g"], "norm_fwd" + sfx, with_t=True)
    zb = _mm(h, p["w_b"], bf16, 1024, 1024, 1024, "inproj_b" + sfx)
    if gather_next is None:
        zf, gathered = _mm(h, p["w_f"], f32, 1024, 1024, 1024, "inproj_f" + sfx), None
    else:
        zf, gathered = _mm(h, p["w_f"], f32, 1024, 1024, 1024, "inproj_f" + sfx, comm=("gather", gather_next))
    zs = _mm(h, p["w_s"], f32, 512, 128, 1024, "inproj_s" + sfx)
    sm = _small_prep(zs, p["par"], "small_prep" + sfx)
    S = x.shape[0]
    gt4 = jnp.transpose(sm[:, 8:12]).reshape(4, S // CH, 1, CH)
    qa, ka, st = _fox_prep(zb, sm, "fox_prep" + sfx)
    tab = _fox_bound_table(st, S, min(S, FOX_TILE))
    oa, lse_t = _fox_fwd(qa, ka, zb, tab, "fox_fwd" + sfx)
    gqkv = _gdn_prep(zf, p["conv_w"], "gdn_prep" + sfx)
    ob, tinv, states = _gdn_fwd(gqkv, sm, gt4, "gdn_fwd" + sfx)
    memn = _norm_fwd(mem, p["mem_norm_g"], "mem_norm" + sfx)
    kv = _mm(memn, p["w_mem_kv"], bf16, 256, 1024, 1024, "mem_kv" + sfx)
    xo, ycat, merged = _merge_fwd(x, oa, ob, zb, zf, kv, p["b_merge"], p["gdn_norm_g"], p["w_branch"], p["w_out"], "merge_fwd" + sfx)
    saved = dict(x=x, ht=ht, zb=zb, zf=zf, zs=zs, sm=sm, qa=qa, ka=ka, tab=tab, gt4=gt4, oa=oa, lse_t=lse_t, gqkv=gqkv, ob=ob, tinv=tinv,
                 states=states, memn=memn, kv=kv, ycat=ycat, merged=merged)
    return xo, saved, gathered


def _layer_bwd(l, dout, mem, p, s, comm_sends=(), send_fn=None):
    sfx = f"_l{l}"
    dproj, dzf2, dmq, delta, doab, dob, dkv, dbm, dgg = _merge_bwd(
        dout, s["ycat"], s["oa"], s["ob"], s["zb"], s["zf"], s["kv"], p["b_merge"], p["gdn_norm_g"],
        p["w_branch"], p["w_branch_t"], p["w_out_t"], "merge_bwd" + sfx)
    g = {}
    g["w_out"] = _mm(s["merged"], dout, f32, 512, 1024, 512, "dw_out" + sfx, trans_a=True)
    g["w_branch"] = jnp.stack([
        _mm(s["ycat"], dproj, f32, 512, 1024, 512, f"dw_branch{n}" + sfx, trans_a=True, a_cols=(n * 512, 512), b_cols=(n * D, D))
        for n in range(3)])
    g["b_merge"] = dbm[0]
    g["gdn_norm_g"] = dgg[0]
    g["w_mem_kv"] = _mm(s["memn"], dkv, f32, 512, 1024, 256, "dw_mem_kv" + sfx, trans_a=True)
    dmemn = _mm(dkv, p["w_mem_kv_t"], f32, 256, 1024, 1024, "dmem_n" + sfx)
    g["mem_norm_g"] = _norm_bwd(mem, p["mem_norm_g"], dmemn, None, "mem_norm_bwd" + sfx)[0]
    dgqkv, dsm = _gdn_bwd(s["gqkv"], s["sm"], s["gt4"], s["tinv"], s["states"], dob, "gdn_bwd" + sfx)
    dbqkv, dcw = _gdn_prep_bwd(s["zf"], p["conv_w"], dgqkv, "gdn_prep_bwd" + sfx)
    g["conv_w"] = dcw[0:4]
    (dq, dk, dv, dfc, dfr), received = _fox_bwd(s["qa"], s["ka"], s["zb"], doab, _head_rows(s["lse_t"]), delta, s["tab"],
                                                "fox_bwd" + sfx, comm_sends=comm_sends)
    dzs, sacc = _small_bwd(s["zs"], p["par"], _head_rows(dfr), dfc, dsm, "small_bwd" + sfx)
    g["b_fg"], g["a_log"], g["dt_bias"] = sacc[0, 0:8], sacc[1, 8:12], sacc[2, 8:12]
    dz = [(dzf2, DZ_MERGE_COLS), (dq, 512), (dk, 512), (dv, 512), (dmq, 512), (dbqkv, 1536), (dzs, PIECE_COLS)]
    g["w_in"] = _mm_n_pieces(s["ht"], dz, f32, 1024, 1024, "dw_in" + sfx)
    if send_fn is None:
        dh, received_late = _mm_k_pieces(dz, p["w_all_t"], f32, 1024, "dh" + sfx), None
    else:
        dh, received_late = _mm_k_pieces(dz, p["w_all_t"], f32, 1024, "dh" + sfx, comm=("exchange", send_fn(g)))
    dx, dng = _norm_bwd(s["x"], p["norm_g"], dh, dout, "norm_bwd" + sfx)
    g["norm_g"] = dng[0]
    return dx, g, received, received_late


def kernel(x, mem, norm_g, w_in, b_fg, b_merge, conv_w, a_log, dt_bias, gdn_norm_g, mem_norm_g, w_mem_kv, w_branch, w_out, final_norm_g, loss_target, m_norm_g, m_w_in, m_b_fg, m_b_merge, m_conv_w, m_a_log, m_dt_bias, m_gdn_norm_g, m_mem_norm_g, m_w_mem_kv, m_w_branch, m_w_out, m_final_norm_g, v_norm_g, v_w_in, v_b_fg, v_b_merge, v_conv_w, v_a_log, v_dt_bias, v_gdn_norm_g, v_mem_norm_g, v_w_mem_kv, v_w_branch, v_w_out, v_final_norm_g):
    x0, mem0, tgt = x[0], mem[0], loss_target[0]
    shard_w = dict(w_in=w_in, w_mem_kv=w_mem_kv, w_branch=w_branch, w_out=w_out, conv_w=conv_w)
    shard_m = dict(w_in=m_w_in, w_mem_kv=m_w_mem_kv, w_branch=m_w_branch, w_out=m_w_out, conv_w=m_conv_w)
    shard_v = dict(w_in=v_w_in, w_mem_kv=v_w_mem_kv, w_branch=v_w_branch, w_out=v_w_out, conv_w=v_conv_w)
    small_w = dict(norm_g=norm_g, b_fg=b_fg, b_merge=b_merge, a_log=a_log, dt_bias=dt_bias, gdn_norm_g=gdn_norm_g,
                   mem_norm_g=mem_norm_g, final_norm_g=final_norm_g)
    small_m = dict(norm_g=m_norm_g, b_fg=m_b_fg, b_merge=m_b_merge, a_log=m_a_log, dt_bias=m_dt_bias, gdn_norm_g=m_gdn_norm_g,
                   mem_norm_g=m_mem_norm_g, final_norm_g=m_final_norm_g)
    small_v = dict(norm_g=v_norm_g, b_fg=v_b_fg, b_merge=v_b_merge, a_log=v_a_log, dt_bias=v_dt_bias, gdn_norm_g=v_gdn_norm_g,
                   mem_norm_g=v_mem_norm_g, final_norm_g=v_final_norm_g)

    def shards(l):
        return [_b(w_in[l]), _b(w_mem_kv[l]), _b(_as2d(w_branch[l])), _b(w_out[l])]

    def layer_params(l, g_in, g_kv, g_br, g_out, conv_full):
        w_full = jnp.transpose(g_in, (1, 0, 2)).reshape(D, N_IN)
        w_all = _perm_cols(w_full)
        w_kv = g_kv.reshape(D, D)
        w_br = jnp.transpose(g_br.reshape(N_DEV, 3, 512, 128), (1, 2, 0, 3)).reshape(3, 512, D)
        w_o = g_out.reshape(D, D)
        return dict(
            norm_g=norm_g[l][None], mem_norm_g=mem_norm_g[l][None], gdn_norm_g=gdn_norm_g[l][None], b_merge=b_merge[l][None],
            par=_small_pars(b_fg[l], a_log[l], dt_bias[l]),
            conv_w=jnp.pad(conv_full[l], ((0, 4), (0, 0))),
            w_b=w_all[:, 0:NB], w_f=w_all[:, NB:NB + NF], w_s=w_all[:, NB + NF:NB + NF + NS],
            w_all_t=jnp.transpose(_perm_cols(w_full, _ORDER_BWD)),
            w_mem_kv=w_kv, w_mem_kv_t=jnp.transpose(w_kv),
            w_branch=w_br, w_branch_t=jnp.transpose(w_br, (0, 2, 1)),
            w_out=w_o, w_out_t=jnp.transpose(w_o))

    *gathered0, conv_all = _all_gather(shards(0) + [_as2d(conv_w)], "gather_weights")
    conv_full = jnp.transpose(conv_all.reshape(N_DEV, DEPTH, 4, 192), (1, 2, 0, 3)).reshape(DEPTH, 4, 1536)
    layers = [layer_params(0, *gathered0, conv_full), None]

    saved = [None] * DEPTH
    acts, saved[0], gathered1 = _layer_fwd(0, x0, mem0, layers[0], gather_next=shards(1))
    layers[1] = layer_params(1, *gathered1, conv_full)
    acts, saved[1], _ = _layer_fwd(1, acts, mem0, layers[1])
    dx, dfg, lsum = _loss_head(acts, final_norm_g[None], tgt, "loss_head")

    def send_buffers(g):
        dw_in = _unperm_cols(g["w_in"], _ORDER_BWD)
        send = dict(
            w_in=jnp.transpose(dw_in.reshape(D, N_DEV, 1026), (1, 0, 2)),
            w_mem_kv=g["w_mem_kv"].reshape(N_DEV, 128, D),
            w_branch=jnp.transpose(g["w_branch"].reshape(3, 512, N_DEV, 128), (2, 0, 1, 3)).reshape(N_DEV, 3 * 512, 128),
            w_out=g["w_out"].reshape(N_DEV, 128, D),
            conv_w=jnp.transpose(g["conv_w"].reshape(4, N_DEV, 192), (1, 0, 2)))
        return [_b(send[n]) for n in _SHARDED]

    grads, parts = [None] * DEPTH, [None] * DEPTH
    dx, grads[1], _, _ = _layer_bwd(1, dx, mem0, layers[1], saved[1])
    dx, grads[0], parts[1], parts[0] = _layer_bwd(0, dx, mem0, layers[0], saved[0], comm_sends=send_buffers(grads[1]),
                                                  send_fn=send_buffers)
    grad_x = dx[None]

    big = [{}, {}, {}, {}]
    for a, n in enumerate(_SHARDED):
        res = [_adamw(parts[l][a], _as2d(shard_w[n][l]), _as2d(shard_m[n][l]), _as2d(shard_v[n][l]), f"adamw_{n}_l{l}")
               for l in range(DEPTH)]
        for kind in range(4):
            big[kind][n] = jnp.stack([res[l][kind] for l in range(DEPTH)]).reshape(shard_w[n].shape)

    small_g = {k: jnp.stack([grads[l][k] for l in range(DEPTH)]) for k in ("norm_g", "b_fg", "b_merge", "a_log", "dt_bias", "gdn_norm_g", "mem_norm_g")}
    small_g["final_norm_g"] = dfg[0]
    small_g["extra"] = lsum[0, 0]
    parts_s, = _all_gather([_pack_small(small_g)], "gather_small")
    g_sm, d_sm, m_sm, v_sm = _adamw(parts_s, _pack_small(small_w), _pack_small(small_m), _pack_small(small_v), "adamw_replicated")

    sml = [_unpack_small(a) for a in (g_sm, d_sm, m_sm, v_sm)]
    loss = sml[0]["extra"]
    names = ("norm_g", "w_in", "b_fg", "b_merge", "conv_w", "a_log", "dt_bias", "gdn_norm_g", "mem_norm_g", "w_mem_kv", "w_branch", "w_out", "final_norm_g")
    outs = [loss, grad_x]
    for kind in range(4):
        for n in names:
            outs.append(big[kind][n] if n in big[kind] else sml[kind][n])
    return tuple(outs)
```

```python
import functools

import jax
import jax.numpy as jnp
from jax import lax
from jax.experimental import pallas as pl
from jax.experimental.pallas import tpu as pltpu

f32, bf16 = jnp.float32, jnp.bfloat16

D = 1024
EPS = 1e-6
CH = 64
N_DEV = 8
DEPTH = 2
FOX_SCALE = 64 ** -0.5
GDN_SCALE = 128 ** -0.5
MEM_SCALE = 128 ** -0.5
NEG = -1e30
VMEM_LIMIT = 56 * 1024 * 1024

ADAM_LR, ADAM_B1, ADAM_B2, ADAM_EPS, ADAM_WD, ADAM_STEP = 0.001, 0.9, 0.999, 1e-08, 0.01, 10

_COLS = dict(aq=(0, 512), ak=(512, 1024), av=(1024, 1536), af=(1536, 1544), az=(1544, 2056),
             bq=(2056, 2568), bk=(2568, 3080), bv=(3080, 3592), ba=(3592, 3596), bb=(3596, 3600),
             bz=(3600, 4112), mq=(4112, 4624), mz=(4624, 5136), gates=(5136, 8208))
_ORDER = ("aq", "ak", "av", "mq", "bq", "bk", "bv", "az", "bz", "mz", "gates", "af", "ba", "bb")
_ORDER_BWD = ("az", "bz", "mz", "gates", "bq", "bk", "bv", "aq", "ak", "av", "mq", "af", "ba", "bb")
DZ_MERGE_COLS = 4608
N_IN = 8208
NB, NF, NS = 2048, 6144, 128
PIECE_COLS = 512
WIDE_COLS = 1536
N_ALL = NB + NF + PIECE_COLS

_SHARDED = ("w_in", "w_mem_kv", "w_branch", "w_out", "conv_w")


def _cp(sem=None, vmem=None):
    kw = {}
    if sem is not None:
        kw["dimension_semantics"] = sem
    if vmem is not None:
        kw["vmem_limit_bytes"] = vmem
    return pltpu.CompilerParams(**kw)


def _dot(a, b):
    return jnp.dot(a, b, preferred_element_type=f32)


def _dot_nt(a, b):
    return lax.dot_general(a, b, (((1,), (1,)), ((), ())), preferred_element_type=f32)


def _dot_tn(a, b):
    return lax.dot_general(a, b, (((0,), (0,)), ((), ())), preferred_element_type=f32)


def _split2(x):
    hi = x.astype(bf16)
    return hi, (x - hi.astype(f32)).astype(bf16)


def _mm3(a, b, dims):
    ah, al = _split2(a)
    bh, bl = _split2(b)
    dg = functools.partial(lax.dot_general, dimension_numbers=dims, preferred_element_type=f32)
    return dg(ah, bh) + (dg(ah, bl) + dg(al, bh))


def _hi(a, b):
    return _mm3(a, b, (((1,), (0,)), ((), ())))


def _hi_nt(a, b):
    return _mm3(a, b, (((1,), (1,)), ((), ())))


def _hi_tn(a, b):
    return _mm3(a, b, (((0,), (0,)), ((), ())))


def _hi_b(a, b):
    return _mm3(a, b, (((2,), (1,)), ((0,), (0,))))


def _b(x):
    return x.astype(bf16)


def _sig(x):
    return jax.nn.sigmoid(x)


def _silu(x):
    return x * _sig(x)


def _dsilu(x):
    s = _sig(x)
    return s * (1.0 + x * (1.0 - s))


def _softplus(x):
    return jnp.maximum(x, 0.0) + jnp.log1p(jnp.exp(-jnp.abs(x)))


def _rowsum(x):
    return jnp.sum(x, axis=1, keepdims=True)


def _colsum(x):
    return jnp.sum(x, axis=0, keepdims=True)


def _norm_fwd(x, g, name, with_t=False):
    M = x.shape[0]
    ts = min(M, 512)

    def body(x_ref, g_ref, h_ref, *t_ref):
        xv = x_ref[...]
        r = lax.rsqrt(jnp.mean(xv * xv, axis=-1, keepdims=True) + EPS)
        h = xv * r * g_ref[...]
        h_ref[...] = _b(h)
        if with_t:
            t_ref[0][...] = _b(jnp.transpose(h))

    tile = pl.BlockSpec((ts, D), lambda i: (i, 0))
    shapes, specs = jax.ShapeDtypeStruct((M, D), bf16), tile
    if with_t:
        shapes, specs = (shapes, jax.ShapeDtypeStruct((D, M), bf16)), (tile, pl.BlockSpec((D, ts), lambda i: (0, i)))
    return pl.pallas_call(
        body, grid=(M // ts,), out_shape=shapes,
        in_specs=[tile, pl.BlockSpec((1, D), lambda i: (0, 0))],
        out_specs=specs, compiler_params=_cp(("parallel",)), name=name)(x, g)


def _norm_bwd(x, g, dh, dres, name):
    M = x.shape[0]
    ts = min(M, 512)
    with_dx = dres is not None

    def body(*refs):
        if with_dx:
            x_ref, g_ref, dh_ref, dres_ref, dx_ref, dg_ref = refs
        else:
            x_ref, g_ref, dh_ref, dg_ref = refs
        i = pl.program_id(0)
        xv = x_ref[...]
        r = lax.rsqrt(jnp.mean(xv * xv, axis=-1, keepdims=True) + EPS)
        xh = xv * r
        dh = dh_ref[...].astype(f32)
        part = jnp.broadcast_to(_colsum(dh * xh), (8, D))

        @pl.when(i == 0)
        def _():
            dg_ref[...] = part

        @pl.when(i > 0)
        def _():
            dg_ref[...] += part

        if with_dx:
            dxh = dh * g_ref[...]
            dx_ref[...] = dres_ref[...] + r * (dxh - xh * jnp.mean(dxh * xh, axis=-1, keepdims=True))

    tile = pl.BlockSpec((ts, D), lambda i: (i, 0))
    gspec = pl.BlockSpec((1, D), lambda i: (0, 0))
    acc = pl.BlockSpec((8, D), lambda i: (0, 0))
    if with_dx:
        return pl.pallas_call(
            body, grid=(M // ts,), out_shape=(jax.ShapeDtypeStruct((M, D), f32), jax.ShapeDtypeStruct((8, D), f32)),
            in_specs=[tile, gspec, tile, tile], out_specs=(tile, acc), compiler_params=_cp(("arbitrary",)), name=name)(x, g, dh, dres)
    return pl.pallas_call(
        body, grid=(M // ts,), out_shape=jax.ShapeDtypeStruct((8, D), f32),
        in_specs=[tile, gspec, tile], out_specs=acc, compiler_params=_cp(("arbitrary",)), name=name)(x, g, dh)


def _loss_head(x, g, tgt, name):
    M = x.shape[0]
    ts = min(M, 512)

    def body(x_ref, g_ref, t_ref, dx_ref, dg_ref, ls_ref):
        i = pl.program_id(0)
        xv = x_ref[...]
        gv = g_ref[...]
        r = lax.rsqrt(jnp.mean(xv * xv, axis=-1, keepdims=True) + EPS)
        xh = xv * r
        e = xh * gv - t_ref[...]
        lpart = 0.5 * jnp.sum(jnp.mean(e * e, axis=-1, keepdims=True), axis=0, keepdims=True)
        dy = e * (1.0 / D)
        dgp = jnp.broadcast_to(_colsum(dy * xh), (8, D))
        lp = jnp.broadcast_to(lpart, (8, 128))

        @pl.when(i == 0)
        def _():
            dg_ref[...] = dgp
            ls_ref[...] = lp

        @pl.when(i > 0)
        def _():
            dg_ref[...] += dgp
            ls_ref[...] += lp

        dxh = dy * gv
        dx_ref[...] = r * (dxh - xh * jnp.mean(dxh * xh, axis=-1, keepdims=True))

    tile = pl.BlockSpec((ts, D), lambda i: (i, 0))
    return pl.pallas_call(
        body, grid=(M // ts,),
        out_shape=(jax.ShapeDtypeStruct((M, D), f32), jax.ShapeDtypeStruct((8, D), f32), jax.ShapeDtypeStruct((8, 128), f32)),
        in_specs=[tile, pl.BlockSpec((1, D), lambda i: (0, 0)), tile],
        out_specs=(tile, pl.BlockSpec((8, D), lambda i: (0, 0)), pl.BlockSpec((8, 128), lambda i: (0, 0))),
        compiler_params=_cp(("arbitrary",)), name=name)(x, g, tgt)


def _mm(a, b, out_dtype, tm, tn, tk, name, trans_a=False, a_cols=None, b_cols=None, comm=None):
    if trans_a:
        K, M = a.shape
    else:
        M, K = a.shape
    N = b.shape[1]
    a0, b0 = 0, 0
    if a_cols is not None:
        a0, M = a_cols
    if b_cols is not None:
        b0, N = b_cols
    tm, tn, tk = min(tm, M), min(tn, N), min(tk, K)
    nk = K // tk
    a0, b0 = a0 // tm, b0 // tn
    grid = (M // tm, N // tn, nk)
    kind, carr = comm if comm is not None else (None, ())
    nc = len(carr)

    def body(*refs):
        a_ref, b_ref = refs[:2]
        o_ref = refs[2 + nc]
        acc_ref = refs[3 + 2 * nc]
        k = pl.program_id(2)
        if nc:
            cs_refs, cr_refs, sems = refs[2:2 + nc], refs[3 + nc:3 + 2 * nc], refs[4 + 2 * nc:]
            step = (pl.program_id(0) * grid[1] + pl.program_id(1)) * nk + k

            @pl.when(step == 0)
            def _():
                if kind == "gather":
                    _Gather(cs_refs, cr_refs, *sems).start()
                else:
                    for cp in _exchange_copies(cs_refs, cr_refs, *sems):
                        cp.start()
        av, bv = _b(a_ref[...]), _b(b_ref[...])
        part = _dot_tn(av, bv) if trans_a else _dot(av, bv)
        if nk == 1:
            o_ref[...] = part.astype(out_dtype)
        else:
            @pl.when(k == 0)
            def _():
                acc_ref[...] = part

            @pl.when(k > 0)
            def _():
                acc_ref[...] += part

            @pl.when(k == nk - 1)
            def _():
                o_ref[...] = acc_ref[...].astype(out_dtype)
        if nc:
            @pl.when(step == grid[0] * grid[1] * nk - 1)
            def _():
                if kind == "gather":
                    _Gather(cs_refs, cr_refs, *sems).finish()
                else:
                    for cp in _exchange_copies(cs_refs, cr_refs, *sems):
                        cp.wait()

    a_spec = pl.BlockSpec((tk, tm), lambda i, j, k: (k, i + a0)) if trans_a else pl.BlockSpec((tm, tk), lambda i, j, k: (i, k))
    anyspec = pl.BlockSpec(memory_space=pl.ANY)
    recv_shapes = tuple(jax.ShapeDtypeStruct(((N_DEV,) + c.shape) if kind == "gather" else c.shape, c.dtype) for c in carr)
    out = pl.pallas_call(
        body, grid=grid, out_shape=(jax.ShapeDtypeStruct((M, N), out_dtype),) + recv_shapes,
        in_specs=[a_spec, pl.BlockSpec((tk, tn), lambda i, j, k: (k, j + b0))] + [anyspec] * nc,
        out_specs=(pl.BlockSpec((tm, tn), lambda i, j, k: (i, j)),) + tuple([anyspec] * nc),
        scratch_shapes=[pltpu.VMEM((tm, tn), f32)] + (_exchange_scratch(nc) if nc else []),
        compiler_params=_cp(("arbitrary",) * 3 if nc else ("parallel", "parallel", "arbitrary"), VMEM_LIMIT), name=name)(a, b, *carr)
    return (out[0], list(out[1:])) if nc else out[0]


def _piece_spans(pieces):
    spans, start, col = [], 0, 0
    for _, w, bw in pieces:
        assert w % bw == 0 and col % bw == 0
        spans.append((start, w // bw, bw, col))
        start += w // bw
        col += w
    return spans, start


def _mm_k_pieces(pieces, w, out_dtype, tm, name, comm=None):
    M, N = pieces[0][0].shape[0], w.shape[1]
    tm = min(tm, M)
    spans, nk = _piece_spans(pieces)
    npc = len(pieces)
    bws = sorted({bw for _, _, bw, _ in spans}, reverse=True)
    nw = len(bws)
    kind, carr = comm if comm is not None else (None, ())
    nc = len(carr)

    def body(*refs):
        p_refs, w_refs = refs[:npc], refs[npc:npc + nw]
        refs = refs[nw - 1:]
        o_ref = refs[npc + 1 + nc]
        acc_ref = refs[npc + 2 + 2 * nc]
        k = pl.program_id(1)
        if nc:
            cs_refs, cr_refs, sems = refs[npc + 1:npc + 1 + nc], refs[npc + 2 + nc:npc + 2 + 2 * nc], refs[npc + 3 + 2 * nc:]
            step = pl.program_id(0) * nk + k

            @pl.when(step == 0)
            def _():
                for cp in _exchange_copies(cs_refs, cr_refs, *sems):
                    cp.start()

        @pl.when(k == 0)
        def _():
            acc_ref[...] = jnp.zeros_like(acc_ref)

        for p, (start, n, bw, _) in enumerate(spans):
            @pl.when((k >= start) & (k < start + n))
            def _():
                acc_ref[...] += _dot(_b(p_refs[p][...]), w_refs[bws.index(bw)][...])

        @pl.when(k == nk - 1)
        def _():
            o_ref[...] = acc_ref[...].astype(out_dtype)

        if nc:
            @pl.when(step == (M // tm) * nk - 1)
            def _():
                for cp in _exchange_copies(cs_refs, cr_refs, *sems):
                    cp.wait()

    assert kind in (None, "exchange")
    anyspec = pl.BlockSpec(memory_space=pl.ANY)
    p_specs = [pl.BlockSpec((tm, bw), functools.partial(lambda i, k, s, n: (i, jnp.clip(k - s, 0, n - 1)), s=s, n=n))
               for s, n, bw, _ in spans]
    w_specs = []
    for bw in bws:
        mine = [sp for sp in spans if sp[2] == bw]
        first, steps, row0 = mine[0][0], sum(sp[1] for sp in mine), mine[0][3] // bw
        assert mine[-1][0] + mine[-1][1] - first == steps
        w_specs.append(pl.BlockSpec((bw, N), functools.partial(lambda i, k, f, s, r: (r + jnp.clip(k - f, 0, s - 1), 0), f=first, s=steps, r=row0)))
    out = pl.pallas_call(
        body, grid=(M // tm, nk),
        out_shape=(jax.ShapeDtypeStruct((M, N), out_dtype),) + tuple(jax.ShapeDtypeStruct(c.shape, c.dtype) for c in carr),
        in_specs=p_specs + w_specs + [anyspec] * nc,
        out_specs=(pl.BlockSpec((tm, N), lambda i, k: (i, 0)),) + tuple([anyspec] * nc),
        scratch_shapes=[pltpu.VMEM((tm, N), f32)] + (_exchange_scratch(nc) if nc else []),
        compiler_params=_cp(("arbitrary", "arbitrary"), VMEM_LIMIT), name=name)(*[a for a, _, _ in pieces], *([w] * nw), *carr)
    return (out[0], list(out[1:])) if nc else out[0]


def _mm_n_pieces(a, pieces, out_dtype, tm, tk, name):
    M, K = a.shape
    tm, tk = min(tm, M), min(tk, K)
    spans, nn = _piece_spans(pieces)
    cols = pieces[0][2]
    assert all(bw == cols for _, _, bw in pieces)
    npc, nk = len(pieces), K // tk

    def body(*refs):
        a_ref, p_refs, o_ref, acc_ref = refs[0], refs[1:1 + npc], refs[1 + npc], refs[2 + npc]
        j, k = pl.program_id(1), pl.program_id(2)

        @pl.when(k == 0)
        def _():
            acc_ref[...] = jnp.zeros_like(acc_ref)

        for p, (start, n, _, _) in enumerate(spans):
            @pl.when((j >= start) & (j < start + n))
            def _():
                acc_ref[...] += _dot(_b(a_ref[...]), _b(p_refs[p][...]))

        @pl.when(k == nk - 1)
        def _():
            o_ref[...] = acc_ref[...].astype(out_dtype)

    def p_map(i, j, k, s, n):
        inside = (j >= s) & (j < s + n)
        return jnp.where(inside, k, 0), jnp.clip(j - s, 0, n - 1)

    p_specs = [pl.BlockSpec((tk, cols), functools.partial(p_map, s=s, n=n)) for s, n, _, _ in spans]
    return pl.pallas_call(
        body, grid=(M // tm, nn, nk), out_shape=jax.ShapeDtypeStruct((M, nn * cols), out_dtype),
        in_specs=[pl.BlockSpec((tm, tk), lambda i, j, k: (i, k))] + p_specs,
        out_specs=pl.BlockSpec((tm, cols), lambda i, j, k: (i, j)),
        scratch_shapes=[pltpu.VMEM((tm, cols), f32)],
        compiler_params=_cp(("parallel", "parallel", "arbitrary"), VMEM_LIMIT), name=name)(a, *[b for b, _, _ in pieces])


def _small_pars(b_fg, a_log, dt_bias):
    par = jnp.zeros((8, 128), f32)
    par = par.at[0, 0:8].set(b_fg).at[1, 8:12].set(a_log).at[2, 8:12].set(dt_bias)
    return par


def _small_prep(zs, par, name):
    S = zs.shape[0]
    ts = min(S, 512)

    def body(z_ref, par_ref, o_ref, carry_ref):
        i = pl.program_id(0)

        @pl.when(i == 0)
        def _():
            carry_ref[...] = jnp.zeros_like(carry_ref)

        z = z_ref[...]
        lane = lax.broadcasted_iota(jnp.int32, (ts, 128), 1)
        row = lax.broadcasted_iota(jnp.int32, (ts, 128), 0)
        za = z + par_ref[0:1, :]
        logf = jnp.minimum(za, 0.0) - jnp.log1p(jnp.exp(-jnp.abs(za)))
        glog = -jnp.exp(par_ref[1:2, :]) * _softplus(z + par_ref[2:3, :])
        x = jnp.where(lane < 8, logf, jnp.where(lane < 12, glog, 0.0))
        pos = jnp.where(lane < 8, row, row & (CH - 1))
        s = 1
        while s < ts:
            x = x + jnp.where(pos >= s, pltpu.roll(x, s, 0), 0.0)
            s *= 2
        tot = x + carry_ref[0:1, :]
        carry_ref[...] = jnp.broadcast_to(jnp.where(lane[0:1] < 8, tot[ts - 1:ts, :], 0.0), (8, 128))
        o_ref[...] = jnp.where(lane < 8, tot, jnp.where(lane < 12, x, jnp.where(lane < 16, _sig(z), 0.0)))

    return pl.pallas_call(
        body, grid=(S // ts,), out_shape=jax.ShapeDtypeStruct((S, 128), f32),
        in_specs=[pl.BlockSpec((ts, 128), lambda i: (i, 0)), pl.BlockSpec((8, 128), lambda i: (0, 0))],
        out_specs=pl.BlockSpec((ts, 128), lambda i: (i, 0)), scratch_shapes=[pltpu.VMEM((8, 128), f32)],
        compiler_params=_cp(("arbitrary",)), name=name)(zs, par)


def _small_bwd(zs, par, dfr, dfc, dsm, name):
    S = zs.shape[0]
    ts = min(S, 512)
    nt = S // ts

    def body(z_ref, par_ref, dfr_ref, dfc_ref, dsm_ref, dz_ref, acc_ref, carry_ref):
        i = pl.program_id(0)

        @pl.when(i == 0)
        def _():
            carry_ref[...] = jnp.zeros_like(carry_ref)

        z = z_ref[...]
        dsm_v = dsm_ref[...]
        lane = lax.broadcasted_iota(jnp.int32, (ts, 128), 1)
        row = lax.broadcasted_iota(jnp.int32, (ts, 128), 0)
        df = jnp.transpose(jnp.concatenate([dfr_ref[...], jnp.zeros((120, ts), f32)], axis=0))
        for p in range(4):
            dpair = dfc_ref[p]
            df = df - jnp.where(lane == 2 * p, dpair[:, 0:1], jnp.where(lane == 2 * p + 1, dpair[:, 64:65], 0.0))
        x = jnp.where(lane < 8, df, jnp.where(lane < 12, dsm_v, 0.0))
        pos = jnp.where(lane < 8, row, row & (CH - 1))
        seg = jnp.where(lane < 8, ts, CH)
        s = 1
        while s < ts:
            x = x + jnp.where(pos + s < seg, pltpu.roll(x, ts - s, 0), 0.0)
            s *= 2
        tot = x + carry_ref[0:1, :]
        carry_ref[...] = jnp.broadcast_to(jnp.where(lane[0:1] < 8, tot[0:1, :], 0.0), (8, 128))
        za = z + par_ref[0:1, :]
        daf = tot * _sig(-za)
        zb = z + par_ref[2:3, :]
        nea = -jnp.exp(par_ref[1:2, :])
        glog = nea * _softplus(zb)
        dba = x * nea * _sig(zb)
        beta = _sig(z)
        dbb = dsm_v * beta * (1.0 - beta)
        dz_ref[:, 0:128] = _b(jnp.where(lane < 8, daf, jnp.where(lane < 12, dba, jnp.where(lane < 16, dbb, 0.0))))
        dz_ref[:, 128:PIECE_COLS] = jnp.zeros((ts, PIECE_COLS - 128), bf16)
        r0 = _colsum(jnp.where(lane < 8, daf, 0.0))
        r1 = _colsum(jnp.where((lane >= 8) & (lane < 12), x * glog, 0.0))
        r2 = _colsum(jnp.where((lane >= 8) & (lane < 12), dba, 0.0))
        r8 = lax.broadcasted_iota(jnp.int32, (8, 128), 0)
        part = jnp.where(r8 == 0, r0, jnp.where(r8 == 1, r1, jnp.where(r8 == 2, r2, 0.0)))

        @pl.when(i == 0)
        def _():
            acc_ref[...] = part

        @pl.when(i > 0)
        def _():
            acc_ref[...] += part

    rev = pl.BlockSpec((ts, 128), lambda i: (nt - 1 - i, 0))
    rev4 = pl.BlockSpec((4, ts, 128), lambda i: (0, nt - 1 - i, 0))
    c8 = pl.BlockSpec((8, 128), lambda i: (0, 0))
    return pl.pallas_call(
        body, grid=(nt,), out_shape=(jax.ShapeDtypeStruct((S, PIECE_COLS), bf16), jax.ShapeDtypeStruct((8, 128), f32)),
        in_specs=[rev, c8, pl.BlockSpec((8, ts), lambda i: (0, nt - 1 - i)), rev4, rev],
        out_specs=(pl.BlockSpec((ts, PIECE_COLS), lambda i: (nt - 1 - i, 0)), c8),
        scratch_shapes=[pltpu.VMEM((8, 128), f32)],
        compiler_params=_cp(("arbitrary",)), name=name)(zs, par, dfr, dfc, dsm)


def _split3(x):
    hi = _b(x).astype(f32)
    r = x - hi
    mid = _b(r).astype(f32)
    return hi, mid, _b(r - mid).astype(f32)


FOX_PREP_ROWS = 512
FOX_TILE = 512
FOX_SKIP_LOG = -32.0


def _fox_prep(zb, sm, name):
    S = zb.shape[0]
    ts = min(S, FOX_PREP_ROWS)

    def body(q_ref, k_ref, f_ref, qa_ref, ka_ref, st_ref):
        lane = lax.broadcasted_iota(jnp.int32, (ts, 128), 1)
        lane8 = lax.broadcasted_iota(jnp.int32, (8, 128), 1)
        f = f_ref[...]
        st = jnp.zeros((8, 128), f32)
        for p in range(4):
            q = q_ref[:, p * 128:(p + 1) * 128].astype(f32) * FOX_SCALE
            k = k_ref[:, p * 128:(p + 1) * 128].astype(f32)
            for h in (0, 1):
                fcol = f[:, 2 * p + h:2 * p + h + 1]
                hi, mid, lo = _split3(fcol)
                own = (lane < 64) if h == 0 else (lane >= 64)
                nq = jnp.sqrt(_rowsum(jnp.where(own, q * q, 0.0)))
                nk = jnp.sqrt(_rowsum(jnp.where(own, k * k, 0.0)))
                stats = (jnp.max(nq, axis=0, keepdims=True), jnp.max(nk, axis=0, keepdims=True),
                         jnp.max(fcol, axis=0, keepdims=True), jnp.min(fcol, axis=0, keepdims=True),
                         jnp.min(-nq * nk, axis=0, keepdims=True))
                for si, val in enumerate(stats):
                    st = jnp.where(lane8 == 8 * si + 2 * p + h, val, st)
                o = 64 if h == 0 else 0
                ones_lo = (lane >= o) & (lane < o + 3)
                ones_hi = (lane >= o + 3) & (lane < o + 6)
                qaug = jnp.where(lane == o, hi, jnp.where(lane == o + 1, mid, jnp.where(lane == o + 2, lo, jnp.where(ones_hi, 1.0, 0.0))))
                kaug = jnp.where(lane == o + 3, -hi, jnp.where(lane == o + 4, -mid, jnp.where(lane == o + 5, -lo, jnp.where(ones_lo, 1.0, 0.0))))
                qa_ref[2 * p + h] = _b(jnp.where(own, q, qaug))
                ka_ref[2 * p + h] = _b(jnp.where(own, k, kaug))
        st_ref[...] = st

    out = jax.ShapeDtypeStruct((8, S, 128), bf16)
    return pl.pallas_call(
        body, grid=(S // ts,), out_shape=(out, out, jax.ShapeDtypeStruct((S // ts * 8, 128), f32)),
        in_specs=[pl.BlockSpec((ts, 512), lambda i: (i, 0)), pl.BlockSpec((ts, 512), lambda i: (i, 1)), pl.BlockSpec((ts, 128), lambda i: (i, 0))],
        out_specs=(pl.BlockSpec((8, ts, 128), lambda i: (0, i, 0)), pl.BlockSpec((8, ts, 128), lambda i: (0, i, 0)),
                   pl.BlockSpec((8, 128), lambda i: (i, 0))),
        compiler_params=_cp(("parallel",)), name=name)(zb, zb, sm)


def _fox_bound_table(st, S, T):
    ts = min(S, FOX_PREP_ROWS)
    g = T // ts
    nt = S // T
    s5 = st.reshape(S // ts, 8, 128)[:, 0, 0:40].reshape(nt, g, 5, 8)
    qn, kn, fmax = s5[:, :, 0].max(axis=1), s5[:, :, 1].max(axis=1), s5[:, :, 2].max(axis=1)
    fmin, lmin = s5[:, :, 3].min(axis=1), s5[:, :, 4].min(axis=1)
    e = qn[:, None] * kn[None, :] + fmax[:, None] - fmin[None, :] - lmin[:, None] + 1.0
    return jnp.transpose(e, (2, 0, 1)).reshape(8, nt * nt)


def _pair_rows(a, T):
    at = jnp.transpose(a)
    r8 = lax.broadcasted_iota(jnp.int32, (8, T), 0)
    return jnp.where(r8 == 0, at[0:1, :], at[64:65, :])


def _fox_fwd(qa, ka, zb, tab, name):
    S = zb.shape[0]
    T = min(S, FOX_TILE)
    nt = S // T

    def body(tab_ref, qa_ref, ka_ref, v_ref, o_ref, lset_ref, m_ref, l_ref, acc_ref):
        p, i = pl.program_id(0), pl.program_id(1)
        m_ref[...] = jnp.full_like(m_ref, NEG)
        l_ref[...] = jnp.zeros_like(l_ref)
        acc_ref[...] = jnp.zeros_like(acc_ref)
        row = lax.broadcasted_iota(jnp.int32, (T, T), 0)
        col = lax.broadcasted_iota(jnp.int32, (T, T), 1)

        def head_tile(h, j, masked):
            off = pl.multiple_of(j * T, T)
            s = _dot_nt(qa_ref[h], ka_ref[h, pl.ds(off, T), :])
            if masked:
                s = jnp.where(row >= col, s, NEG)
            m_old = m_ref[h]
            m_new = jnp.maximum(m_old, jnp.max(s, axis=1, keepdims=True))
            alpha = jnp.exp(m_old - m_new)
            pr = jnp.exp(s - jnp.tile(m_new, (1, T // 128)))
            l_ref[h] = alpha * l_ref[h] + _rowsum(pr)
            acc_ref[h] = alpha * acc_ref[h] + _dot(_b(pr), v_ref[pl.ds(off, T), :])
            m_ref[h] = m_new

        def step(j, c):
            for h in (0, 1):
                @pl.when(tab_ref[2 * p + h, i * nt + j] > FOX_SKIP_LOG)
                def _():
                    head_tile(h, j, False)
            return c

        lax.fori_loop(0, i, step, 0)
        for h in (0, 1):
            head_tile(h, i, True)
        lane2 = lax.broadcasted_iota(jnp.int32, (T, 128), 1)
        o_ref[...] = jnp.where(lane2 < 64, acc_ref[0] / l_ref[0], acc_ref[1] / l_ref[1])
        lse = jnp.where(lane2 < 64, m_ref[0] + jnp.log(l_ref[0]), m_ref[1] + jnp.log(l_ref[1]))
        lset_ref[0] = _pair_rows(lse, T)

    return pl.pallas_call(
        body, grid=(4, S // T),
        out_shape=(jax.ShapeDtypeStruct((S, 512), f32), jax.ShapeDtypeStruct((4, 8, S), f32)),
        in_specs=[pl.BlockSpec(memory_space=pltpu.SMEM), pl.BlockSpec((2, T, 128), lambda p, i: (p, i, 0)),
                  pl.BlockSpec((2, S, 128), lambda p, i: (p, 0, 0), pipeline_mode=pl.Buffered(1)),
                  pl.BlockSpec((S, 128), lambda p, i: (0, 8 + p), pipeline_mode=pl.Buffered(1))],
        out_specs=(pl.BlockSpec((T, 128), lambda p, i: (i, p)), pl.BlockSpec((1, 8, T), lambda p, i: (p, 0, i))),
        scratch_shapes=[pltpu.VMEM((2, T, 128), f32), pltpu.VMEM((2, T, 128), f32), pltpu.VMEM((2, T, 128), f32)],
        compiler_params=_cp(("arbitrary", "arbitrary"), VMEM_LIMIT), name=name)(tab, qa, ka, zb)


def _fox_bwd(qa, ka, zb, dob, lse_t, dl_t, tab, name, comm_sends=()):
    S = zb.shape[0]
    T = min(S, FOX_TILE)
    nq = S // T
    nc = len(comm_sends)

    def body(*refs):
        tab_ref, ka_ref, v_ref, qa_ref, do_ref, lt_ref, dt_ref = refs[:7]
        cs_refs = refs[7:7 + nc]
        dq_ref, dk_ref, dv_ref, dfc_ref, dfr_ref = refs[7 + nc:12 + nc]
        cr_refs = refs[12 + nc:12 + 2 * nc]
        dqa_ref, dka_ref, dva_ref, fs_ref = refs[12 + 2 * nc:16 + 2 * nc]
        p, j = pl.program_id(0), pl.program_id(1)
        if nc:
            @pl.when((p == 0) & (j == 0))
            def _():
                for cp in _exchange_copies(cs_refs, cr_refs, *refs[16 + 2 * nc:]):
                    cp.start()
        lane1 = lax.broadcasted_iota(jnp.int32, (1, 128), 1)
        lane2 = lax.broadcasted_iota(jnp.int32, (T, 128), 1)
        hm = (lane1 < 64, lane1 >= 64)
        v = v_ref[...]
        vsm = [jnp.where(hm[h], v, jnp.zeros_like(v)) for h in (0, 1)]
        ksm = [jnp.where(hm[h], ka_ref[h], jnp.zeros_like(v)) for h in (0, 1)]

        @pl.when(j == 0)
        def _():
            dqa_ref[...] = jnp.zeros_like(dqa_ref)
            dfr_ref[...] = jnp.zeros_like(dfr_ref)

        dka_ref[...] = jnp.zeros_like(dka_ref)
        dva_ref[...] = jnp.zeros_like(dva_ref)
        fs_ref[...] = jnp.zeros_like(fs_ref)
        row = lax.broadcasted_iota(jnp.int32, (T, T), 0)
        col = lax.broadcasted_iota(jnp.int32, (T, T), 1)

        def head_tile(h, i, masked):
            off = pl.multiple_of(i * T, T)
            dot_ = do_ref[pl.ds(off, T), :]
            hr = pl.ds(2 * p + h, 1)
            qt = qa_ref[h, pl.ds(off, T), :]
            s_t = _dot_nt(ka_ref[h], qt)
            if masked:
                s_t = jnp.where(col >= row, s_t, NEG)
            p_t = jnp.exp(s_t - lt_ref[hr, pl.ds(off, T)])
            dva_ref[h] += _dot(_b(p_t), dot_)
            dp_t = _dot_nt(vsm[h], dot_)
            ds_t = p_t * (dp_t - dt_ref[hr, pl.ds(off, T)])
            dsb = _b(ds_t)
            dka_ref[h] += _dot(dsb, qt)
            fs_ref[h] += _rowsum(ds_t)
            dfr_ref[0, pl.ds(h, 1), pl.ds(off, T)] += _colsum(ds_t)
            dqa_ref[pl.ds(off, T), :] += _dot_tn(dsb, ksm[h])

        def step(i, c):
            for h in (0, 1):
                @pl.when(tab_ref[2 * p + h, i * nq + j] > FOX_SKIP_LOG)
                def _():
                    head_tile(h, i, False)
            return c

        for h in (0, 1):
            head_tile(h, j, True)
        lax.fori_loop(j + 1, nq, step, 0)
        dk_ref[...] = _b(jnp.where(lane2 < 64, dka_ref[0], dka_ref[1]))
        dv_ref[...] = _b(jnp.where(lane2 < 64, dva_ref[0], dva_ref[1]))
        dfc_ref[0] = jnp.where(lane2 < 64, fs_ref[0], fs_ref[1])
        dq_ref[...] = _b(dqa_ref[pl.ds(pl.multiple_of(j * T, T), T), :] * FOX_SCALE)
        if nc:
            @pl.when((p == 3) & (j == nq - 1))
            def _():
                for cp in _exchange_copies(cs_refs, cr_refs, *refs[16 + 2 * nc:]):
                    cp.wait()

    one = pl.Buffered(1)
    res = pl.BlockSpec((8, S), lambda p, j: (0, 0), pipeline_mode=one)
    tk = pl.BlockSpec((T, 128), lambda p, j: (j, p))
    anyspec = pl.BlockSpec(memory_space=pl.ANY)
    outs = pl.pallas_call(
        body, grid=(4, nq),
        out_shape=(jax.ShapeDtypeStruct((S, 512), bf16), jax.ShapeDtypeStruct((S, 512), bf16), jax.ShapeDtypeStruct((S, 512), bf16),
                   jax.ShapeDtypeStruct((4, S, 128), f32), jax.ShapeDtypeStruct((4, 8, S), f32))
        + tuple(jax.ShapeDtypeStruct(c.shape, c.dtype) for c in comm_sends),
        in_specs=[pl.BlockSpec(memory_space=pltpu.SMEM),
                  pl.BlockSpec((2, T, 128), lambda p, j: (p, j, 0)), pl.BlockSpec((T, 128), lambda p, j: (j, 8 + p)),
                  pl.BlockSpec((2, S, 128), lambda p, j: (p, 0, 0), pipeline_mode=one),
                  pl.BlockSpec((S, 128), lambda p, j: (0, p), pipeline_mode=one), res, res] + [anyspec] * nc,
        out_specs=(tk, tk, tk, pl.BlockSpec((1, T, 128), lambda p, j: (p, j, 0)),
                   pl.BlockSpec((1, 8, S), lambda p, j: (p, 0, 0))) + tuple([anyspec] * nc),
        scratch_shapes=[pltpu.VMEM((S, 128), f32), pltpu.VMEM((2, T, 128), f32), pltpu.VMEM((2, T, 128), f32), pltpu.VMEM((2, T, 1), f32)]
        + (_exchange_scratch(nc) if nc else []),
        compiler_params=_cp(("arbitrary", "arbitrary"), VMEM_LIMIT), name=name)(tab, ka, zb, qa, dob, lse_t, dl_t, *comm_sends)
    return outs[:5], list(outs[5:])


def _head_rows(a):
    return a[:, 0:2, :].reshape(8, a.shape[2])


def _conv_taps(ext, x, w_ref, ts):
    y = x * w_ref[3:4, :]
    shifted = []
    for k in (1, 2, 3):
        xs = pltpu.roll(ext, k, 0)[8:]
        shifted.append(xs)
        y = y + xs * w_ref[3 - k:4 - k, :]
    return y, shifted


def _gdn_prep(zf, cw, name):
    S = zf.shape[0]
    ts = min(S, 512)

    def body(x_ref, w_ref, o_ref, tail_ref):
        i = pl.program_id(0)

        @pl.when(i == 0)
        def _():
            tail_ref[...] = jnp.zeros_like(tail_ref)

        x = x_ref[...]
        ext = jnp.concatenate([tail_ref[...], x], axis=0)
        y, _ = _conv_taps(ext, x, w_ref, ts)
        tail_ref[...] = x[ts - 8:, :]
        a = _silu(y)
        for hb in range(12):
            blk = a[:, hb * 128:(hb + 1) * 128]
            if hb < 8:
                blk = blk * lax.rsqrt(_rowsum(blk * blk) + EPS)
            if hb < 4:
                blk = blk * GDN_SCALE
            o_ref[:, hb * 128:(hb + 1) * 128] = blk

    return pl.pallas_call(
        body, grid=(S // ts,), out_shape=jax.ShapeDtypeStruct((S, 1536), f32),
        in_specs=[pl.BlockSpec((ts, 1536), lambda i: (i, 0)), pl.BlockSpec((8, 1536), lambda i: (0, 0))],
        out_specs=pl.BlockSpec((ts, 1536), lambda i: (i, 0)), scratch_shapes=[pltpu.VMEM((8, 1536), f32)],
        compiler_params=_cp(("arbitrary",), VMEM_LIMIT), name=name)(zf, cw)


def _gdn_prep_bwd(zf, cw, dg, name):
    S = zf.shape[0]
    ts = min(S, 512)
    nt = S // ts

    def body(x_ref, xp_ref, w_ref, dg_ref, dx_ref, dw_ref, head_ref):
        i = pl.program_id(0)

        @pl.when(i == 0)
        def _():
            head_ref[...] = jnp.zeros_like(head_ref)

        x = x_ref[...]
        prev = jnp.where(i == nt - 1, 0.0, xp_ref[...])
        ext = jnp.concatenate([prev, x], axis=0)
        y, shifted = _conv_taps(ext, x, w_ref, ts)
        a = _silu(y)
        das = []
        for hb in range(12):
            blk = a[:, hb * 128:(hb + 1) * 128]
            d = dg_ref[:, hb * 128:(hb + 1) * 128]
            if hb < 4:
                d = d * GDN_SCALE
            if hb < 8:
                r = lax.rsqrt(_rowsum(blk * blk) + EPS)
                n = blk * r
                d = r * (d - n * _rowsum(d * n))
            das.append(d)
        dy = jnp.concatenate(das, axis=1) * _dsilu(y)
        extd = jnp.concatenate([dy, head_ref[...]], axis=0)
        dx = dy * w_ref[3:4, :]
        for k in (1, 2, 3):
            dx = dx + pltpu.roll(extd, ts + 8 - k, 0)[:ts] * w_ref[3 - k:4 - k, :]
        head_ref[...] = dy[0:8, :]
        dx_ref[...] = _b(dx)
        r8 = lax.broadcasted_iota(jnp.int32, (8, 1536), 0)
        part = jnp.where(r8 == 3, _colsum(dy * x), 0.0)
        for k in (1, 2, 3):
            part = jnp.where(r8 == 3 - k, _colsum(dy * shifted[k - 1]), part)

        @pl.when(i == 0)
        def _():
            dw_ref[...] = part

        @pl.when(i > 0)
        def _():
            dw_ref[...] += part

    rev = pl.BlockSpec((ts, 1536), lambda i: (nt - 1 - i, 0))
    prev8 = pl.BlockSpec((8, 1536), lambda i: (jnp.maximum((nt - 1 - i) * (ts // 8) - 1, 0), 0))
    w8 = pl.BlockSpec((8, 1536), lambda i: (0, 0))
    return pl.pallas_call(
        body, grid=(nt,), out_shape=(jax.ShapeDtypeStruct((S, 1536), bf16), jax.ShapeDtypeStruct((8, 1536), f32)),
        in_specs=[rev, prev8, w8, rev], out_specs=(rev, w8), scratch_shapes=[pltpu.VMEM((8, 1536), f32)],
        compiler_params=_cp(("arbitrary",), VMEM_LIMIT), name=name)(zf, zf, cw, dg)


def _tri_inv(a, row, col):
    same = (row >> 4) == (col >> 4)
    dm = jnp.where(same, a, 0.0)
    lo = a - dm
    eye = jnp.where(row == col, 1.0, 0.0)
    d2 = _hi_b(dm, dm)
    d4 = _hi_b(d2, d2)
    d8 = _hi_b(d4, d4)
    x0 = _hi_b(_hi_b(eye - dm, eye + d2), _hi_b(eye + d4, eye + d8))
    n = _hi_b(x0, lo)
    n2 = _hi_b(n, n)
    return _hi_b(_hi_b(eye - n, eye + n2), x0)


def _bd(a, b):
    return lax.dot_general(a, b, (((2,), (1,)), ((0,), (0,))), preferred_element_type=f32)


def _bd_nt(a, b):
    return lax.dot_general(a, b, (((2,), (2,)), ((0,), (0,))), preferred_element_type=f32)


def _bd_tn(a, b):
    return lax.dot_general(a, b, (((1,), (1,)), ((0,), (0,))), preferred_element_type=f32)


def _hi_b_nt(a, b):
    return _mm3(a, b, (((2,), (2,)), ((0,), (0,))))


def _hi_b_tn(a, b):
    return _mm3(a, b, (((1,), (1,)), ((0,), (0,))))


def _gdn_local(x_ref, sm_ref, gt_ref, row, col, cps=1):
    idx = [(c, h) for c in range(cps) for h in range(4)]

    def rows(c):
        return slice(c * CH, (c + 1) * CH)

    q = jnp.stack([x_ref[rows(c), h * 128:(h + 1) * 128] for c, h in idx])
    k = jnp.stack([x_ref[rows(c), 512 + h * 128:512 + (h + 1) * 128] for c, h in idx])
    v = jnp.stack([x_ref[rows(c), 1024 + h * 128:1024 + (h + 1) * 128] for c, h in idx])
    gc = jnp.stack([sm_ref[rows(c), 8 + h:9 + h] for c, h in idx])
    beta = jnp.stack([sm_ref[rows(c), 12 + h:13 + h] for c, h in idx])
    gr = jnp.stack([gt_ref[h, c] for c, h in idx])
    eg = jnp.exp(gc)
    gl = gc[:, CH - 1:CH, :]
    dec = jnp.exp(gl - gc)
    gm = gc - gr
    gam_i = jnp.exp(jnp.where(row >= col, gm, -jnp.inf))
    gam_s = jnp.where(row > col, gam_i, 0.0)
    kb = k * beta
    return dict(q=q, k=k, v=v, beta=beta, eg=eg, egl=jnp.exp(gl), dec=dec, gam_i=gam_i, gam_s=gam_s,
                kb=kb, vb=v * beta, kbg=kb * eg, qdec=q * eg, kdec=k * dec,
                a=_bd_nt(_b(kb), _b(k)) * gam_s, aqk=_bd_nt(_b(q), _b(k)) * gam_i)


GDN_FWD_CHUNKS = 8


def _gdn_fwd(gqkv, sm, gt4, name, cps=GDN_FWD_CHUNKS):
    S = gqkv.shape[0]
    N = S // CH
    cps = min(cps, N)
    R = cps * CH

    def body(x_ref, sm_ref, gt_ref, o_ref, t_ref, st_ref, s_ref):
        n = pl.program_id(0)

        @pl.when(n == 0)
        def _():
            s_ref[...] = jnp.zeros_like(s_ref)

        row = lax.broadcasted_iota(jnp.int32, (CH, CH), 0)
        col = lax.broadcasted_iota(jnp.int32, (CH, CH), 1)
        c = _gdn_local(x_ref, sm_ref, gt_ref, row, col, cps)
        t = _tri_inv(c["a"], row, col)
        uw = _hi_b(t, jnp.concatenate([c["vb"], c["kbg"]], axis=2))
        u, w = uw[:, :, :128], uw[:, :, 128:]
        for ci in range(cps):
            sl = slice(4 * ci, 4 * ci + 4)
            rs = slice(ci * CH, (ci + 1) * CH)
            st = s_ref[...]
            st_ref[ci] = st
            sb = _b(st)
            vnew = u[sl] - _bd(_b(w[sl]), sb)
            o = _bd(_b(c["qdec"][sl]), sb) + _bd(_b(c["aqk"][sl]), _b(vnew))
            for h in range(4):
                o_ref[rs, h * 128:(h + 1) * 128] = o[h]
                t_ref[h, rs, :] = t[4 * ci + h]
            s_ref[...] = st * c["egl"][sl] + _bd_tn(_b(c["kdec"][sl]), _b(vnew))

    return pl.pallas_call(
        body, grid=(N // cps,),
        out_shape=(jax.ShapeDtypeStruct((S, 512), f32), jax.ShapeDtypeStruct((4, S, CH), f32), jax.ShapeDtypeStruct((N, 4, 128, 128), f32)),
        in_specs=[pl.BlockSpec((R, 1536), lambda n: (n, 0)), pl.BlockSpec((R, 128), lambda n: (n, 0)),
                  pl.BlockSpec((4, cps, 1, CH), lambda n: (0, n, 0, 0))],
        out_specs=(pl.BlockSpec((R, 512), lambda n: (n, 0)), pl.BlockSpec((4, R, CH), lambda n: (0, n, 0)),
                   pl.BlockSpec((cps, 4, 128, 128), lambda n: (n, 0, 0, 0))),
        scratch_shapes=[pltpu.VMEM((4, 128, 128), f32)], compiler_params=_cp(("arbitrary",)), name=name)(gqkv, sm, gt4)


GDN_BWD_CHUNKS = 4


def _gdn_bwd(gqkv, sm, gt4, tinv, states, do, name, cps=GDN_BWD_CHUNKS):
    S = gqkv.shape[0]
    N = S // CH
    cps = min(cps, N)
    R = cps * CH

    def body(x_ref, sm_ref, gt_ref, t_ref, st_ref, do_ref, dx_ref, dsm_ref, ds_ref):
        n = pl.program_id(0)

        @pl.when(n == 0)
        def _():
            ds_ref[...] = jnp.zeros_like(ds_ref)

        row = lax.broadcasted_iota(jnp.int32, (CH, CH), 0)
        col = lax.broadcasted_iota(jnp.int32, (CH, CH), 1)
        row1 = lax.broadcasted_iota(jnp.int32, (CH, 1), 0)
        lane = lax.broadcasted_iota(jnp.int32, (CH, 128), 1)
        ones = jnp.ones((4 * cps, CH, 128), f32)
        idx = [(ci, h) for ci in range(cps) for h in range(4)]
        c = _gdn_local(x_ref, sm_ref, gt_ref, row, col, cps)
        q, k, v, beta, eg = c["q"], c["k"], c["v"], c["beta"], c["eg"]
        t = jnp.stack([t_ref[h, ci * CH:(ci + 1) * CH, :] for ci, h in idx])
        uw = _hi_b(t, jnp.concatenate([c["vb"], c["kbg"]], axis=2))
        u, w = uw[:, :, :128], uw[:, :, 128:]
        st = st_ref[...].reshape(4 * cps, 128, 128)
        sb = _b(st)
        vnew = u - _bd(_b(w), sb)
        dob = _b(jnp.stack([do_ref[ci * CH:(ci + 1) * CH, h * 128:(h + 1) * 128] for ci, h in idx]))
        vnb = _b(vnew)
        dqdec = _bd_nt(dob, sb)
        daqk = jnp.where(row >= col, _bd_nt(dob, vnb), 0.0)
        qd_do = _bd_tn(_b(c["qdec"]), dob)
        aqk_do = _bd_tn(_b(c["aqk"]), dob)
        kdecb, wb = _b(c["kdec"]), _b(w)
        dvnew_l, dkdec_l, dgl_l = [None] * cps, [None] * cps, [None] * cps
        for ci in reversed(range(cps)):
            sl = slice(4 * ci, 4 * ci + 4)
            dsp = ds_ref[...]
            dspb = _b(dsp)
            dvn = _bd(kdecb[sl], dspb) + aqk_do[sl]
            dvnew_l[ci] = dvn
            dkdec_l[ci] = _bd_nt(vnb[sl], dspb)
            dgl_l[ci] = c["egl"][sl] * jnp.sum(dsp * st[sl], axis=(1, 2), keepdims=True)
            ds_ref[...] = dsp * c["egl"][sl] + qd_do[sl] - _bd_tn(wb[sl], _b(dvn))
        dvnew = jnp.concatenate(dvnew_l, axis=0)
        dkdec = jnp.concatenate(dkdec_l, axis=0)
        dgl = jnp.concatenate(dgl_l, axis=0)
        dw = -_bd_nt(_b(dvnew), sb)
        duw = _hi_b_tn(t, jnp.concatenate([dvnew, dw], axis=2))
        dvb, dkbg = duw[:, :, :128], duw[:, :, 128:]
        da = -jnp.where(row > col, _hi_b_nt(duw, uw), 0.0)
        dp = da * c["gam_s"]
        dqk = daqk * c["gam_i"]
        m = da * c["a"] + daqk * c["aqk"]
        csum = _hi_b_tn(m, ones)[:, :, 0:1]
        kk = dkdec * c["kdec"]

        def lsum(a):
            return jnp.sum(a, axis=2, keepdims=True)

        dgv = lsum(m) - csum + lsum(dqdec * c["qdec"]) - lsum(kk) + lsum(dkbg * c["kbg"])
        dgv = dgv + jnp.where(row1 == CH - 1, dgl + jnp.sum(kk, axis=(1, 2), keepdims=True), 0.0)
        dpb, dqkb = _b(dp), _b(dqk)
        dkb = _bd(dpb, _b(k)) + dkbg * eg
        dk = _bd_tn(dpb, _b(c["kb"])) + _bd_tn(dqkb, _b(q)) + dkdec * c["dec"] + dkb * beta
        dq = _bd(dqkb, _b(k)) + dqdec * eg
        dbeta = lsum(dkb * k) + lsum(dvb * v)
        dv = dvb * beta
        for ci in range(cps):
            rs = slice(ci * CH, (ci + 1) * CH)
            dsm = jnp.zeros((CH, 128), f32)
            for h in range(4):
                b = 4 * ci + h
                dx_ref[rs, h * 128:(h + 1) * 128] = dq[b]
                dx_ref[rs, 512 + h * 128:512 + (h + 1) * 128] = dk[b]
                dx_ref[rs, 1024 + h * 128:1024 + (h + 1) * 128] = dv[b]
                dsm = jnp.where(lane == 8 + h, dgv[b], jnp.where(lane == 12 + h, dbeta[b], dsm))
            dsm_ref[rs, :] = dsm

    G = N // cps
    return pl.pallas_call(
        body, grid=(G,), out_shape=(jax.ShapeDtypeStruct((S, 1536), f32), jax.ShapeDtypeStruct((S, 128), f32)),
        in_specs=[pl.BlockSpec((R, 1536), lambda n: (G - 1 - n, 0)), pl.BlockSpec((R, 128), lambda n: (G - 1 - n, 0)),
                  pl.BlockSpec((4, cps, 1, CH), lambda n: (0, G - 1 - n, 0, 0)), pl.BlockSpec((4, R, CH), lambda n: (0, G - 1 - n, 0)),
                  pl.BlockSpec((cps, 4, 128, 128), lambda n: (G - 1 - n, 0, 0, 0)), pl.BlockSpec((R, 512), lambda n: (G - 1 - n, 0))],
        out_specs=(pl.BlockSpec((R, 1536), lambda n: (G - 1 - n, 0)), pl.BlockSpec((R, 128), lambda n: (G - 1 - n, 0))),
        scratch_shapes=[pltpu.VMEM((4, 128, 128), f32)], compiler_params=_cp(("arbitrary",), VMEM_LIMIT), name=name)(gqkv, sm, gt4, tinv, states, do)


MERGE_ROWS = 256
MERGE_FWD_ROWS = 512


def _mem_attn(q, kv_ref, h):
    s = _dot_nt(q, kv_ref[:, h * 128:(h + 1) * 128]) * MEM_SCALE
    e = jnp.exp(s - jnp.max(s, axis=1, keepdims=True))
    return e / _rowsum(e)


def _gdn_out_norm(ob):
    r = lax.rsqrt(jnp.mean(ob * ob, axis=-1, keepdims=True) + EPS)
    return ob * r, r


def _merge_fwd(x, oa, ob, zb, zf, kv, b_merge, gdn_g, w_branch, w_out, name):
    S = x.shape[0]
    ts = min(S, MERGE_FWD_ROWS)

    def body(x_ref, oa_ref, ob_ref, mq_ref, az_ref, bz_ref, mz_ref, gt_ref, kv_ref, bm_ref, gg_ref, wb_ref, wo_ref,
             xo_ref, y_ref, mg_ref):
        y_ref[:, 0:512] = _b(oa_ref[...] * _silu(az_ref[...]))
        for h in range(4):
            sl = slice(h * 128, (h + 1) * 128)
            nb, _ = _gdn_out_norm(ob_ref[:, sl])
            y_ref[:, 512 + h * 128:512 + (h + 1) * 128] = _b(nb * gg_ref[...] * _silu(bz_ref[:, sl]))
            pm = _mem_attn(mq_ref[:, sl], kv_ref, h)
            om = _dot(_b(pm), kv_ref[:, 512 + h * 128:512 + (h + 1) * 128])
            y_ref[:, 1024 + h * 128:1024 + (h + 1) * 128] = _b(om * _silu(mz_ref[:, sl]))
        merged = jnp.zeros((ts, D), f32)
        for n in range(3):
            gate = _sig(gt_ref[:, n * D:(n + 1) * D] + bm_ref[:, n * D:(n + 1) * D])
            merged = merged + gate * _dot(y_ref[:, n * 512:(n + 1) * 512], wb_ref[n])
        mb = _b(merged)
        mg_ref[...] = mb
        xo_ref[...] = x_ref[...] + _dot(mb, wo_ref[...])

    def col(w, c):
        return pl.BlockSpec((ts, w), lambda i: (i, c))

    def full(shape):
        return pl.BlockSpec(shape, lambda i: tuple(0 for _ in shape))

    return pl.pallas_call(
        body, grid=(S // ts,),
        out_shape=(jax.ShapeDtypeStruct((S, D), f32), jax.ShapeDtypeStruct((S, 1536), bf16), jax.ShapeDtypeStruct((S, D), bf16)),
        in_specs=[col(D, 0), col(512, 0), col(512, 0), col(512, 3), col(512, 3), col(512, 4), col(512, 5), col(3072, 1),
                  full((256, D)), full((1, 3072)), full((1, 128)), full((3, 512, D)), full((D, D))],
        out_specs=(col(D, 0), col(1536, 0), col(D, 0)),
        compiler_params=_cp(("parallel",), VMEM_LIMIT), name=name)(x, oa, ob, zb, zf, zf, zf, zf, kv, b_merge, gdn_g, w_branch, w_out)


def _merge_bwd(dout, ycat, oa, ob, zb, zf, kv, b_merge, gdn_g, w_branch, w_branch_t, w_out_t, name):
    S = dout.shape[0]
    ts = min(S, MERGE_ROWS)

    def body(do_ref, y_ref, oa_ref, ob_ref, mq_ref, az_ref, bz_ref, mz_ref, gt_ref, kv_ref, bm_ref, gg_ref, wb_ref, wbt_ref, wot_ref,
             dpj_ref, dz_ref, dmq_ref, dlt_ref, doab_ref, dob_ref, dkv_ref, dbm_ref, dgg_ref):
        i = pl.program_id(0)
        dmerged = _dot(_b(do_ref[...]), wot_ref[...])
        dys = []
        dbm_parts = []
        for n in range(3):
            cs = slice(n * D, (n + 1) * D)
            gate = _sig(gt_ref[:, cs] + bm_ref[:, cs])
            proj = _dot(y_ref[:, n * 512:(n + 1) * 512], wb_ref[n])
            dlogit = dmerged * proj * gate * (1.0 - gate)
            dz_ref[:, 1536 + n * D:1536 + (n + 1) * D] = _b(dlogit)
            dbm_parts.append(_colsum(dlogit))
            dproj = _b(dmerged * gate)
            dpj_ref[:, cs] = dproj
            dys.append(_dot(dproj, wbt_ref[n]))
        dbm = jnp.broadcast_to(jnp.concatenate(dbm_parts, axis=1), (8, 3072))
        az = az_ref[...]
        oa = oa_ref[...]
        doa = dys[0] * _silu(az)
        doab_ref[...] = _b(doa)
        prod = doa * oa
        lane = lax.broadcasted_iota(jnp.int32, (ts, 128), 1)
        dl = jnp.zeros((ts, 128), f32)
        for p in range(4):
            blk = prod[:, p * 128:(p + 1) * 128]
            dl = jnp.where(lane == 2 * p, _rowsum(jnp.where(lane < 64, blk, 0.0)),
                           jnp.where(lane == 2 * p + 1, _rowsum(jnp.where(lane >= 64, blk, 0.0)), dl))
        dlt_ref[...] = jnp.transpose(dl)[0:8, :]
        dz_ref[:, 0:512] = _b(dys[0] * oa * _dsilu(az))
        gg = gg_ref[...]
        dgg = jnp.zeros((1, 128), f32)
        dkv_parts_k, dkv_parts_v = [], []
        for h in range(4):
            sl = slice(h * 128, (h + 1) * 128)
            bz = bz_ref[:, sl]
            dyb = dys[1][:, sl]
            nb, r = _gdn_out_norm(ob_ref[:, sl])
            dz_ref[:, 512 + h * 128:512 + (h + 1) * 128] = _b(dyb * nb * gg * _dsilu(bz))
            dng = dyb * _silu(bz)
            dgg = dgg + _colsum(dng * nb)
            dnb = dng * gg
            dob_ref[:, sl] = r * (dnb - nb * jnp.mean(dnb * nb, axis=-1, keepdims=True))
            mz = mz_ref[:, sl]
            dym = dys[2][:, sl]
            q = mq_ref[:, sl]
            kh = kv_ref[:, sl]
            vh = kv_ref[:, 512 + h * 128:512 + (h + 1) * 128]
            pm = _mem_attn(q, kv_ref, h)
            pmb = _b(pm)
            om = _dot(pmb, vh)
            dz_ref[:, 1024 + h * 128:1024 + (h + 1) * 128] = _b(dym * om * _dsilu(mz))
            dom = _b(dym * _silu(mz))
            dkv_parts_v.append(_dot_tn(pmb, dom))
            dpm = _dot_nt(dom, vh)
            dsm = _b(pm * (dpm - _rowsum(dpm * pm)) * MEM_SCALE)
            dmq_ref[:, sl] = _b(_dot(dsm, kh))
            dkv_parts_k.append(_dot_tn(dsm, q))
        dkv = jnp.concatenate(dkv_parts_k + dkv_parts_v, axis=1)
        dggb = jnp.broadcast_to(dgg, (8, 128))

        @pl.when(i == 0)
        def _():
            dkv_ref[...] = dkv
            dbm_ref[...] = dbm
            dgg_ref[...] = dggb

        @pl.when(i > 0)
        def _():
            dkv_ref[...] += dkv
            dbm_ref[...] += dbm
            dgg_ref[...] += dggb

    def col(w, c):
        return pl.BlockSpec((ts, w), lambda i: (i, c))

    def full(shape):
        return pl.BlockSpec(shape, lambda i: tuple(0 for _ in shape))

    return pl.pallas_call(
        body, grid=(S // ts,),
        out_shape=(jax.ShapeDtypeStruct((S, 3072), bf16), jax.ShapeDtypeStruct((S, DZ_MERGE_COLS), bf16), jax.ShapeDtypeStruct((S, 512), bf16),
                   jax.ShapeDtypeStruct((8, S), f32), jax.ShapeDtypeStruct((S, 512), bf16), jax.ShapeDtypeStruct((S, 512), f32),
                   jax.ShapeDtypeStruct((256, D), f32), jax.ShapeDtypeStruct((8, 3072), f32), jax.ShapeDtypeStruct((8, 128), f32)),
        in_specs=[col(D, 0), col(1536, 0), col(512, 0), col(512, 0), col(512, 3), col(512, 3), col(512, 4), col(512, 5), col(3072, 1),
                  full((256, D)), full((1, 3072)), full((1, 128)), full((3, 512, D)), full((3, D, 512)), full((D, D))],
        out_specs=(col(3072, 0), col(4608, 0), col(512, 0), pl.BlockSpec((8, ts), lambda i: (0, i)), col(512, 0), col(512, 0),
                   full((256, D)), full((8, 3072)), full((8, 128))),
        compiler_params=_cp(("arbitrary",), VMEM_LIMIT), name=name)(
            dout, ycat, oa, ob, zb, zf, zf, zf, zf, kv, b_merge, gdn_g, w_branch, w_branch_t, w_out_t)


def _mesh_pos():
    return lax.axis_index("x"), lax.axis_index("y"), lax.axis_index("c")


class _Gather:
    def __init__(self, x_refs, out_refs, send_sems, recv_sems, local_sems):
        self.n = len(x_refs)
        self.x_refs, self.out_refs = x_refs, out_refs
        self.send_sems, self.recv_sems, self.local_sems = send_sems, recv_sems, local_sems
        mx, my, mc = _mesh_pos()
        self.mc = mc
        self.me, self.sibling = (mx, my, mc), (mx, my, 1 - mc)
        self.chips = [(1 - mx, my), (mx, 1 - my), (1 - mx, 1 - my)]

    def copy(self, a, k, block, to, src=None):
        px, py, pc = block
        slot = self.out_refs[a].at[4 * px + 2 * py + pc]
        return pltpu.make_async_remote_copy(
            src_ref=slot if src is None else src, dst_ref=slot, send_sem=self.send_sems.at[7 * a + k],
            recv_sem=self.recv_sems.at[7 * a + k], device_id=to, device_id_type=pl.DeviceIdType.MESH)

    def own(self):
        mx, my, mc = self.me
        mine = [pltpu.make_async_copy(self.x_refs[a], self.out_refs[a].at[4 * mx + 2 * my + mc], self.local_sems.at[a])
                for a in range(self.n)]
        first = []
        for a in range(self.n):
            first.append(self.copy(a, 0, self.me, self.sibling, src=self.x_refs[a]))
            first += [self.copy(a, 1 + j, self.me, (*chip, self.mc), src=self.x_refs[a]) for j, chip in enumerate(self.chips)]
        return mine, first

    def start(self):
        mine, first = self.own()
        for cp in mine + first:
            cp.start()

    def finish(self):
        mine, first = self.own()
        passed = []
        for j, chip in enumerate(self.chips):
            for a in range(self.n):
                self.copy(a, 1 + j, (*chip, self.mc), self.me).wait_recv()
                fwd = self.copy(a, 4 + j, (*chip, self.mc), self.sibling)
                fwd.start()
                passed.append(fwd)
        for a in range(self.n):
            self.copy(a, 0, self.sibling, self.me).wait_recv()
            for j, chip in enumerate(self.chips):
                self.copy(a, 4 + j, (*chip, 1 - self.mc), self.me).wait_recv()
        for cp in first + passed:
            cp.wait_send()
        for cp in mine:
            cp.wait()


def _all_gather(xs, name):
    n = len(xs)

    def body(*refs):
        g = _Gather(refs[:n], refs[n:2 * n], *refs[2 * n:])
        g.start()
        g.finish()

    anyspec = pl.BlockSpec(memory_space=pl.ANY)
    return pl.pallas_call(
        body, out_shape=tuple(jax.ShapeDtypeStruct((N_DEV,) + x.shape, x.dtype) for x in xs),
        in_specs=[anyspec] * n, out_specs=tuple([anyspec] * n), scratch_shapes=_exchange_scratch(n), name=name)(*xs)


def _exchange_copies(s_refs, r_refs, send_sems, recv_sems, local_sems):
    n = len(s_refs)
    mx, my, mc = _mesh_pos()
    me_id = 4 * mx + 2 * my + mc
    copies = [pltpu.make_async_copy(s_refs[a].at[me_id], r_refs[a].at[me_id], local_sems.at[a]) for a in range(n)]
    for k in range(1, N_DEV):
        px = 1 - mx if k & 4 else mx
        py = 1 - my if k & 2 else my
        pc = 1 - mc if k & 1 else mc
        for a in range(n):
            copies.append(pltpu.make_async_remote_copy(
                src_ref=s_refs[a].at[4 * px + 2 * py + pc], dst_ref=r_refs[a].at[me_id],
                send_sem=send_sems.at[7 * a + k - 1], recv_sem=recv_sems.at[7 * a + k - 1],
                device_id=(px, py, pc), device_id_type=pl.DeviceIdType.MESH))
    return copies


def _exchange_scratch(n):
    return [pltpu.SemaphoreType.DMA((7 * n,)), pltpu.SemaphoreType.DMA((7 * n,)), pltpu.SemaphoreType.DMA((n,))]


def _exchange(sends, name):
    n = len(sends)

    def body(*refs):
        copies = _exchange_copies(refs[:n], refs[n:2 * n], *refs[2 * n:])
        for cp in copies:
            cp.start()
        for cp in copies:
            cp.wait()

    anyspec = pl.BlockSpec(memory_space=pl.ANY)
    return pl.pallas_call(
        body, out_shape=tuple(jax.ShapeDtypeStruct(s.shape, s.dtype) for s in sends),
        in_specs=[anyspec] * n, out_specs=tuple([anyspec] * n), scratch_shapes=_exchange_scratch(n), name=name)(*sends)


ADAMW_BLOCK_BYTES = 4 * 1024 * 1024


def _adamw(parts, w, m, v, name):
    _, R, C = parts.shape
    tr = R
    for t in (1024, 512, 256, 128, 64, 32, 16, 8):
        if R % t == 0 and N_DEV * t * C * 4 <= ADAMW_BLOCK_BYTES:
            tr = t
            break

    def body(p_ref, w_ref, m_ref, v_ref, g_ref, d_ref, nm_ref, nv_ref):
        g = p_ref[0].astype(f32)
        for j in range(1, N_DEV):
            g = g + p_ref[j].astype(f32)
        mn = ADAM_B1 * m_ref[...] + (1.0 - ADAM_B1) * g
        vn = ADAM_B2 * v_ref[...] + (1.0 - ADAM_B2) * jnp.square(g)
        m_hat = mn / (1.0 - ADAM_B1 ** ADAM_STEP)
        v_hat = vn / (1.0 - ADAM_B2 ** ADAM_STEP)
        g_ref[...] = g
        d_ref[...] = -ADAM_LR * (m_hat / (jnp.sqrt(v_hat) + ADAM_EPS) + ADAM_WD * w_ref[...])
        nm_ref[...] = mn
        nv_ref[...] = vn

    t2 = pl.BlockSpec((tr, C), lambda i: (i, 0))
    out = jax.ShapeDtypeStruct((R, C), f32)
    return pl.pallas_call(
        body, grid=(R // tr,), out_shape=(out, out, out, out),
        in_specs=[pl.BlockSpec((N_DEV, tr, C), lambda i: (0, i, 0)), t2, t2, t2], out_specs=(t2, t2, t2, t2),
        compiler_params=_cp(("parallel",), VMEM_LIMIT), name=name)(parts, w, m, v)


def _as2d(a):
    return a.reshape(-1, a.shape[-1])


def _perm_cols(w, order=_ORDER):
    parts = [w[..., _COLS[n][0]:_COLS[n][1]] for n in order]
    pad = jnp.zeros(w.shape[:-1] + (N_ALL - N_IN,), w.dtype)
    return jnp.concatenate(parts + [pad], axis=-1)


def _unperm_cols(w, order=_ORDER):
    pieces, off = {}, 0
    for n in order:
        width = _COLS[n][1] - _COLS[n][0]
        pieces[n] = w[..., off:off + width]
        off += width
    return jnp.concatenate([pieces[n] for n in sorted(_COLS, key=lambda n: _COLS[n][0])], axis=-1)


_SMALL_ROWS = 16


def _pack_small(t):
    z = jnp.zeros((D,), f32)
    misc = z.at[0:16].set(t["b_fg"].reshape(-1)).at[16:24].set(t["a_log"].reshape(-1)).at[24:32].set(t["dt_bias"].reshape(-1))
    misc = misc.at[128:384].set(t["gdn_norm_g"].reshape(-1))
    if "extra" in t:
        misc = misc.at[512].set(t["extra"])
    rows = [t["norm_g"], t["b_merge"].reshape(6, D), t["mem_norm_g"], t["final_norm_g"][None], misc[None],
            jnp.zeros((_SMALL_ROWS - 12, D), f32)]
    return jnp.concatenate(rows, axis=0)


def _unpack_small(a):
    misc = a[11]
    return dict(norm_g=a[0:2], b_merge=a[2:8].reshape(2, 3072), mem_norm_g=a[8:10], final_norm_g=a[10],
                b_fg=misc[0:16].reshape(2, 8), a_log=misc[16:24].reshape(2, 4), dt_bias=misc[24:32].reshape(2, 4),
                gdn_norm_g=misc[128:384].reshape(2, 128), extra=misc[512])


def _layer_fwd(l, x, mem, p, gather_next=None):
    sfx = f"_l{l}"
    h, ht = _norm_fwd(x, p["norm_g"], "norm_fwd" + sfx, with_t=True)
    zb = _mm(h, p["w_b"], bf16, 1024, 1024, 1024, "inproj_b" + sfx)
    if gather_next is None:
        zf, gathered = _mm(h, p["w_f"], f32, 1024, 1024, 1024, "inproj_f" + sfx), None
    else:
        zf, gathered = _mm(h, p["w_f"], f32, 1024, 1024, 1024, "inproj_f" + sfx, comm=("gather", gather_next))
    zs = _mm(h, p["w_s"], f32, 512, 128, 1024, "inproj_s" + sfx)
    sm = _small_prep(zs, p["par"], "small_prep" + sfx)
    S = x.shape[0]
    gt4 = jnp.transpose(sm[:, 8:12]).reshape(4, S // CH, 1, CH)
    qa, ka, st = _fox_prep(zb, sm, "fox_prep" + sfx)
    tab = _fox_bound_table(st, S, min(S, FOX_TILE))
    oa, lse_t = _fox_fwd(qa, ka, zb, tab, "fox_fwd" + sfx)
    gqkv = _gdn_prep(zf, p["conv_w"], "gdn_prep" + sfx)
    ob, tinv, states = _gdn_fwd(gqkv, sm, gt4, "gdn_fwd" + sfx)
    memn = _norm_fwd(mem, p["mem_norm_g"], "mem_norm" + sfx)
    kv = _mm(memn, p["w_mem_kv"], bf16, 256, 1024, 1024, "mem_kv" + sfx)
    xo, ycat, merged = _merge_fwd(x, oa, ob, zb, zf, kv, p["b_merge"], p["gdn_norm_g"], p["w_branch"], p["w_out"], "merge_fwd" + sfx)
    saved = dict(x=x, ht=ht, zb=zb, zf=zf, zs=zs, sm=sm, qa=qa, ka=ka, tab=tab, gt4=gt4, oa=oa, lse_t=lse_t, gqkv=gqkv, ob=ob, tinv=tinv,
                 states=states, memn=memn, kv=kv, ycat=ycat, merged=merged)
    return xo, saved, gathered


def _layer_bwd(l, dout, mem, p, s, comm_sends=(), send_fn=None):
    sfx = f"_l{l}"
    dproj, dzf2, dmq, delta, doab, dob, dkv, dbm, dgg = _merge_bwd(
        dout, s["ycat"], s["oa"], s["ob"], s["zb"], s["zf"], s["kv"], p["b_merge"], p["gdn_norm_g"],
        p["w_branch"], p["w_branch_t"], p["w_out_t"], "merge_bwd" + sfx)
    g = {}
    g["w_out"] = _mm(s["merged"], dout, f32, 512, 1024, 512, "dw_out" + sfx, trans_a=True)
    g["w_branch"] = jnp.stack([
        _mm(s["ycat"], dproj, f32, 512, 1024, 512, f"dw_branch{n}" + sfx, trans_a=True, a_cols=(n * 512, 512), b_cols=(n * D, D))
        for n in range(3)])
    g["b_merge"] = dbm[0]
    g["gdn_norm_g"] = dgg[0]
    g["w_mem_kv"] = _mm(s["memn"], dkv, f32, 512, 1024, 256, "dw_mem_kv" + sfx, trans_a=True)
    dmemn = _mm(dkv, p["w_mem_kv_t"], f32, 256, 1024, 1024, "dmem_n" + sfx)
    g["mem_norm_g"] = _norm_bwd(mem, p["mem_norm_g"], dmemn, None, "mem_norm_bwd" + sfx)[0]
    dgqkv, dsm = _gdn_bwd(s["gqkv"], s["sm"], s["gt4"], s["tinv"], s["states"], dob, "gdn_bwd" + sfx)
    dbqkv, dcw = _gdn_prep_bwd(s["zf"], p["conv_w"], dgqkv, "gdn_prep_bwd" + sfx)
    g["conv_w"] = dcw[0:4]
    (dq, dk, dv, dfc, dfr), received = _fox_bwd(s["qa"], s["ka"], s["zb"], doab, _head_rows(s["lse_t"]), delta, s["tab"],
                                                "fox_bwd" + sfx, comm_sends=comm_sends)
    dzs, sacc = _small_bwd(s["zs"], p["par"], _head_rows(dfr), dfc, dsm, "small_bwd" + sfx)
    g["b_fg"], g["a_log"], g["dt_bias"] = sacc[0, 0:8], sacc[1, 8:12], sacc[2, 8:12]
    wide = [(dzf2, DZ_MERGE_COLS, WIDE_COLS), (dbqkv, 1536, WIDE_COLS)]
    narrow = [(dq, 512, PIECE_COLS), (dk, 512, PIECE_COLS), (dv, 512, PIECE_COLS), (dmq, 512, PIECE_COLS), (dzs, PIECE_COLS, PIECE_COLS)]
    dz = wide + narrow
    g["w_in"] = jnp.concatenate([_mm_n_pieces(s["ht"], wide, f32, 1024, 1024, "dw_in_wide" + sfx),
                                 _mm_n_pieces(s["ht"], narrow, f32, 1024, 1024, "dw_in_narrow" + sfx)], axis=1)
    if send_fn is None:
        dh, received_late = _mm_k_pieces(dz, p["w_all_t"], f32, 1024, "dh" + sfx), None
    else:
        dh, received_late = _mm_k_pieces(dz, p["w_all_t"], f32, 1024, "dh" + sfx, comm=("exchange", send_fn(g)))
    dx, dng = _norm_bwd(s["x"], p["norm_g"], dh, dout, "norm_bwd" + sfx)
    g["norm_g"] = dng[0]
    return dx, g, received, received_late


def kernel(x, mem, norm_g, w_in, b_fg, b_merge, conv_w, a_log, dt_bias, gdn_norm_g, mem_norm_g, w_mem_kv, w_branch, w_out, final_norm_g, loss_target, m_norm_g, m_w_in, m_b_fg, m_b_merge, m_conv_w, m_a_log, m_dt_bias, m_gdn_norm_g, m_mem_norm_g, m_w_mem_kv, m_w_branch, m_w_out, m_final_norm_g, v_norm_g, v_w_in, v_b_fg, v_b_merge, v_conv_w, v_a_log, v_dt_bias, v_gdn_norm_g, v_mem_norm_g, v_w_mem_kv, v_w_branch, v_w_out, v_final_norm_g):
    x0, mem0, tgt = x[0], mem[0], loss_target[0]
    shard_w = dict(w_in=w_in, w_mem_kv=w_mem_kv, w_branch=w_branch, w_out=w_out, conv_w=conv_w)
    shard_m = dict(w_in=m_w_in, w_mem_kv=m_w_mem_kv, w_branch=m_w_branch, w_out=m_w_out, conv_w=m_conv_w)
    shard_v = dict(w_in=v_w_in, w_mem_kv=v_w_mem_kv, w_branch=v_w_branch, w_out=v_w_out, conv_w=v_conv_w)
    small_w = dict(norm_g=norm_g, b_fg=b_fg, b_merge=b_merge, a_log=a_log, dt_bias=dt_bias, gdn_norm_g=gdn_norm_g,
                   mem_norm_g=mem_norm_g, final_norm_g=final_norm_g)
    small_m = dict(norm_g=m_norm_g, b_fg=m_b_fg, b_merge=m_b_merge, a_log=m_a_log, dt_bias=m_dt_bias, gdn_norm_g=m_gdn_norm_g,
                   mem_norm_g=m_mem_norm_g, final_norm_g=m_final_norm_g)
    small_v = dict(norm_g=v_norm_g, b_fg=v_b_fg, b_merge=v_b_merge, a_log=v_a_log, dt_bias=v_dt_bias, gdn_norm_g=v_gdn_norm_g,
                   mem_norm_g=v_mem_norm_g, final_norm_g=v_final_norm_g)

    def shards(l):
        return [_b(w_in[l]), _b(w_mem_kv[l]), _b(_as2d(w_branch[l])), _b(w_out[l])]

    def layer_params(l, g_in, g_kv, g_br, g_out, conv_full):
        w_full = jnp.transpose(g_in, (1, 0, 2)).reshape(D, N_IN)
        w_all = _perm_cols(w_full)
        w_kv = g_kv.reshape(D, D)
        w_br = jnp.transpose(g_br.reshape(N_DEV, 3, 512, 128), (1, 2, 0, 3)).reshape(3, 512, D)
        w_o = g_out.reshape(D, D)
        return dict(
            norm_g=norm_g[l][None], mem_norm_g=mem_norm_g[l][None], gdn_norm_g=gdn_norm_g[l][None], b_merge=b_merge[l][None],
            par=_small_pars(b_fg[l], a_log[l], dt_bias[l]),
            conv_w=jnp.pad(conv_full[l], ((0, 4), (0, 0))),
            w_b=w_all[:, 0:NB], w_f=w_all[:, NB:NB + NF], w_s=w_all[:, NB + NF:NB + NF + NS],
            w_all_t=jnp.transpose(_perm_cols(w_full, _ORDER_BWD)),
            w_mem_kv=w_kv, w_mem_kv_t=jnp.transpose(w_kv),
            w_branch=w_br, w_branch_t=jnp.transpose(w_br, (0, 2, 1)),
            w_out=w_o, w_out_t=jnp.transpose(w_o))

    *gathered0, conv_all = _all_gather(shards(0) + [_as2d(conv_w)], "gather_weights")
    conv_full = jnp.transpose(conv_all.reshape(N_DEV, DEPTH, 4, 192), (1, 2, 0, 3)).reshape(DEPTH, 4, 1536)
    layers = [layer_params(0, *gathered0, conv_full), None]

    saved = [None] * DEPTH
    acts, saved[0], gathered1 = _layer_fwd(0, x0, mem0, layers[0], gather_next=shards(1))
    layers[1] = layer_params(1, *gathered1, conv_full)
    acts, saved[1], _ = _layer_fwd(1, acts, mem0, layers[1])
    dx, dfg, lsum = _loss_head(acts, final_norm_g[None], tgt, "loss_head")

    def send_buffers(g):
        dw_in = _unperm_cols(g["w_in"], _ORDER_BWD)
        send = dict(
            w_in=jnp.transpose(dw_in.reshape(D, N_DEV, 1026), (1, 0, 2)),
            w_mem_kv=g["w_mem_kv"].reshape(N_DEV, 128, D),
            w_branch=jnp.transpose(g["w_branch"].reshape(3, 512, N_DEV, 128), (2, 0, 1, 3)).reshape(N_DEV, 3 * 512, 128),
            w_out=g["w_out"].reshape(N_DEV, 128, D),
            conv_w=jnp.transpose(g["conv_w"].reshape(4, N_DEV, 192), (1, 0, 2)))
        return [_b(send[n]) for n in _SHARDED]

    grads, parts = [None] * DEPTH, [None] * DEPTH
    dx, grads[1], _, _ = _layer_bwd(1, dx, mem0, layers[1], saved[1])
    dx, grads[0], parts[1], parts[0] = _layer_bwd(0, dx, mem0, layers[0], saved[0], comm_sends=send_buffers(grads[1]),
                                                  send_fn=send_buffers)
    grad_x = dx[None]

    big = [{}, {}, {}, {}]
    for a, n in enumerate(_SHARDED):
        res = [_adamw(parts[l][a], _as2d(shard_w[n][l]), _as2d(shard_m[n][l]), _as2d(shard_v[n][l]), f"adamw_{n}_l{l}")
               for l in range(DEPTH)]
        for kind in range(4):
            big[kind][n] = jnp.stack([res[l][kind] for l in range(DEPTH)]).reshape(shard_w[n].shape)

    small_g = {k: jnp.stack([grads[l][k] for l in range(DEPTH)]) for k in ("norm_g", "b_fg", "b_merge", "a_log", "dt_bias", "gdn_norm_g", "mem_norm_g")}
    small_g["final_norm_g"] = dfg[0]
    small_g["extra"] = lsum[0, 0]
    parts_s, = _all_gather([_pack_small(small_g)], "gather_small")
    g_sm, d_sm, m_sm, v_sm = _adamw(parts_s, _pack_small(small_w), _pack_small(small_m), _pack_small(small_v), "adamw_replicated")

    sml = [_unpack_small(a) for a in (g_sm, d_sm, m_sm, v_sm)]
    loss = sml[0]["extra"]
    names = ("norm_g", "w_in", "b_fg", "b_merge", "conv_w", "a_log", "dt_bias", "gdn_norm_g", "mem_norm_g", "w_mem_kv", "w_branch", "w_out", "final_norm_g")
    outs = [loss, grad_x]
    for kind in range(4):
        for n in names:
            outs.append(big[kind][n] if n in big[kind] else sml[kind][n])
    return tuple(outs)
```

```python
import functools

import jax
import jax.numpy as jnp
from jax import lax
from jax.experimental import pallas as pl
from jax.experimental.pallas import tpu as pltpu

f32, bf16 = jnp.float32, jnp.bfloat16

D = 1024
EPS = 1e-6
CH = 64
N_DEV = 8
DEPTH = 2
FOX_SCALE = 64 ** -0.5
GDN_SCALE = 128 ** -0.5
MEM_SCALE = 128 ** -0.5
NEG = -1e30
VMEM_LIMIT = 56 * 1024 * 1024

ADAM_LR, ADAM_B1, ADAM_B2, ADAM_EPS, ADAM_WD, ADAM_STEP = 0.001, 0.9, 0.999, 1e-08, 0.01, 10

_COLS = dict(aq=(0, 512), ak=(512, 1024), av=(1024, 1536), af=(1536, 1544), az=(1544, 2056),
             bq=(2056, 2568), bk=(2568, 3080), bv=(3080, 3592), ba=(3592, 3596), bb=(3596, 3600),
             bz=(3600, 4112), mq=(4112, 4624), mz=(4624, 5136), gates=(5136, 8208))
_ORDER = ("aq", "ak", "av", "mq", "bq", "bk", "bv", "az", "bz", "mz", "gates", "af", "ba", "bb")
_ORDER_BWD = ("az", "bz", "mz", "gates", "bq", "bk", "bv", "aq", "ak", "av", "mq", "af", "ba", "bb")
DZ_MERGE_COLS = 4608
N_IN = 8208
NB, NF, NS = 2048, 6144, 128
PIECE_COLS = 1536
SMALL_COLS = 1024
N_ALL = NB + NF + SMALL_COLS

_SHARDED = ("w_in", "w_mem_kv", "w_branch", "w_out", "conv_w")


def _cp(sem=None, vmem=None):
    kw = {}
    if sem is not None:
        kw["dimension_semantics"] = sem
    if vmem is not None:
        kw["vmem_limit_bytes"] = vmem
    return pltpu.CompilerParams(**kw)


def _dot(a, b):
    return jnp.dot(a, b, preferred_element_type=f32)


def _dot_nt(a, b):
    return lax.dot_general(a, b, (((1,), (1,)), ((), ())), preferred_element_type=f32)


def _dot_tn(a, b):
    return lax.dot_general(a, b, (((0,), (0,)), ((), ())), preferred_element_type=f32)


def _split2(x):
    hi = x.astype(bf16)
    return hi, (x - hi.astype(f32)).astype(bf16)


def _mm3(a, b, dims):
    ah, al = _split2(a)
    bh, bl = _split2(b)
    dg = functools.partial(lax.dot_general, dimension_numbers=dims, preferred_element_type=f32)
    return dg(ah, bh) + (dg(ah, bl) + dg(al, bh))


def _hi(a, b):
    return _mm3(a, b, (((1,), (0,)), ((), ())))


def _hi_nt(a, b):
    return _mm3(a, b, (((1,), (1,)), ((), ())))


def _hi_tn(a, b):
    return _mm3(a, b, (((0,), (0,)), ((), ())))


def _hi_b(a, b):
    return _mm3(a, b, (((2,), (1,)), ((0,), (0,))))


def _b(x):
    return x.astype(bf16)


def _sig(x):
    return jax.nn.sigmoid(x)


def _silu(x):
    return x * _sig(x)


def _dsilu(x):
    s = _sig(x)
    return s * (1.0 + x * (1.0 - s))


def _softplus(x):
    return jnp.maximum(x, 0.0) + jnp.log1p(jnp.exp(-jnp.abs(x)))


def _rowsum(x):
    return jnp.sum(x, axis=1, keepdims=True)


def _colsum(x):
    return jnp.sum(x, axis=0, keepdims=True)


def _norm_fwd(x, g, name, with_t=False):
    M = x.shape[0]
    ts = min(M, 512)

    def body(x_ref, g_ref, h_ref, *t_ref):
        xv = x_ref[...]
        r = lax.rsqrt(jnp.mean(xv * xv, axis=-1, keepdims=True) + EPS)
        h = xv * r * g_ref[...]
        h_ref[...] = _b(h)
        if with_t:
            t_ref[0][...] = _b(jnp.transpose(h))

    tile = pl.BlockSpec((ts, D), lambda i: (i, 0))
    shapes, specs = jax.ShapeDtypeStruct((M, D), bf16), tile
    if with_t:
        shapes, specs = (shapes, jax.ShapeDtypeStruct((D, M), bf16)), (tile, pl.BlockSpec((D, ts), lambda i: (0, i)))
    return pl.pallas_call(
        body, grid=(M // ts,), out_shape=shapes,
        in_specs=[tile, pl.BlockSpec((1, D), lambda i: (0, 0))],
        out_specs=specs, compiler_params=_cp(("parallel",)), name=name)(x, g)


def _norm_bwd(x, g, dh, dres, name):
    M = x.shape[0]
    ts = min(M, 512)
    with_dx = dres is not None

    def body(*refs):
        if with_dx:
            x_ref, g_ref, dh_ref, dres_ref, dx_ref, dg_ref = refs
        else:
            x_ref, g_ref, dh_ref, dg_ref = refs
        i = pl.program_id(0)
        xv = x_ref[...]
        r = lax.rsqrt(jnp.mean(xv * xv, axis=-1, keepdims=True) + EPS)
        xh = xv * r
        dh = dh_ref[...].astype(f32)
        part = jnp.broadcast_to(_colsum(dh * xh), (8, D))

        @pl.when(i == 0)
        def _():
            dg_ref[...] = part

        @pl.when(i > 0)
        def _():
            dg_ref[...] += part

        if with_dx:
            dxh = dh * g_ref[...]
            dx_ref[...] = dres_ref[...] + r * (dxh - xh * jnp.mean(dxh * xh, axis=-1, keepdims=True))

    tile = pl.BlockSpec((ts, D), lambda i: (i, 0))
    gspec = pl.BlockSpec((1, D), lambda i: (0, 0))
    acc = pl.BlockSpec((8, D), lambda i: (0, 0))
    if with_dx:
        return pl.pallas_call(
            body, grid=(M // ts,), out_shape=(jax.ShapeDtypeStruct((M, D), f32), jax.ShapeDtypeStruct((8, D), f32)),
            in_specs=[tile, gspec, tile, tile], out_specs=(tile, acc), compiler_params=_cp(("arbitrary",)), name=name)(x, g, dh, dres)
    return pl.pallas_call(
        body, grid=(M // ts,), out_shape=jax.ShapeDtypeStruct((8, D), f32),
        in_specs=[tile, gspec, tile], out_specs=acc, compiler_params=_cp(("arbitrary",)), name=name)(x, g, dh)


def _loss_head(x, g, tgt, name):
    M = x.shape[0]
    ts = min(M, 512)

    def body(x_ref, g_ref, t_ref, dx_ref, dg_ref, ls_ref):
        i = pl.program_id(0)
        xv = x_ref[...]
        gv = g_ref[...]
        r = lax.rsqrt(jnp.mean(xv * xv, axis=-1, keepdims=True) + EPS)
        xh = xv * r
        e = xh * gv - t_ref[...]
        lpart = 0.5 * jnp.sum(jnp.mean(e * e, axis=-1, keepdims=True), axis=0, keepdims=True)
        dy = e * (1.0 / D)
        dgp = jnp.broadcast_to(_colsum(dy * xh), (8, D))
        lp = jnp.broadcast_to(lpart, (8, 128))

        @pl.when(i == 0)
        def _():
            dg_ref[...] = dgp
            ls_ref[...] = lp

        @pl.when(i > 0)
        def _():
            dg_ref[...] += dgp
            ls_ref[...] += lp

        dxh = dy * gv
        dx_ref[...] = r * (dxh - xh * jnp.mean(dxh * xh, axis=-1, keepdims=True))

    tile = pl.BlockSpec((ts, D), lambda i: (i, 0))
    return pl.pallas_call(
        body, grid=(M // ts,),
        out_shape=(jax.ShapeDtypeStruct((M, D), f32), jax.ShapeDtypeStruct((8, D), f32), jax.ShapeDtypeStruct((8, 128), f32)),
        in_specs=[tile, pl.BlockSpec((1, D), lambda i: (0, 0)), tile],
        out_specs=(tile, pl.BlockSpec((8, D), lambda i: (0, 0)), pl.BlockSpec((8, 128), lambda i: (0, 0))),
        compiler_params=_cp(("arbitrary",)), name=name)(x, g, tgt)


def _mm(a, b, out_dtype, tm, tn, tk, name, trans_a=False, a_cols=None, b_cols=None, comm=None):
    if trans_a:
        K, M = a.shape
    else:
        M, K = a.shape
    N = b.shape[1]
    a0, b0 = 0, 0
    if a_cols is not None:
        a0, M = a_cols
    if b_cols is not None:
        b0, N = b_cols
    tm, tn, tk = min(tm, M), min(tn, N), min(tk, K)
    nk = K // tk
    a0, b0 = a0 // tm, b0 // tn
    grid = (M // tm, N // tn, nk)
    kind, carr = comm if comm is not None else (None, ())
    nc = len(carr)

    def body(*refs):
        a_ref, b_ref = refs[:2]
        o_ref = refs[2 + nc]
        acc_ref = refs[3 + 2 * nc]
        k = pl.program_id(2)
        if nc:
            cs_refs, cr_refs, sems = refs[2:2 + nc], refs[3 + nc:3 + 2 * nc], refs[4 + 2 * nc:]
            step = (pl.program_id(0) * grid[1] + pl.program_id(1)) * nk + k

            @pl.when(step == 0)
            def _():
                if kind == "gather":
                    _Gather(cs_refs, cr_refs, *sems).start()
                else:
                    for cp in _exchange_copies(cs_refs, cr_refs, *sems):
                        cp.start()
        av, bv = _b(a_ref[...]), _b(b_ref[...])
        part = _dot_tn(av, bv) if trans_a else _dot(av, bv)
        if nk == 1:
            o_ref[...] = part.astype(out_dtype)
        else:
            @pl.when(k == 0)
            def _():
                acc_ref[...] = part

            @pl.when(k > 0)
            def _():
                acc_ref[...] += part

            @pl.when(k == nk - 1)
            def _():
                o_ref[...] = acc_ref[...].astype(out_dtype)
        if nc:
            @pl.when(step == grid[0] * grid[1] * nk - 1)
            def _():
                if kind == "gather":
                    _Gather(cs_refs, cr_refs, *sems).finish()
                else:
                    for cp in _exchange_copies(cs_refs, cr_refs, *sems):
                        cp.wait()

    a_spec = pl.BlockSpec((tk, tm), lambda i, j, k: (k, i + a0)) if trans_a else pl.BlockSpec((tm, tk), lambda i, j, k: (i, k))
    anyspec = pl.BlockSpec(memory_space=pl.ANY)
    recv_shapes = tuple(jax.ShapeDtypeStruct(((N_DEV,) + c.shape) if kind == "gather" else c.shape, c.dtype) for c in carr)
    out = pl.pallas_call(
        body, grid=grid, out_shape=(jax.ShapeDtypeStruct((M, N), out_dtype),) + recv_shapes,
        in_specs=[a_spec, pl.BlockSpec((tk, tn), lambda i, j, k: (k, j + b0))] + [anyspec] * nc,
        out_specs=(pl.BlockSpec((tm, tn), lambda i, j, k: (i, j)),) + tuple([anyspec] * nc),
        scratch_shapes=[pltpu.VMEM((tm, tn), f32)] + (_exchange_scratch(nc) if nc else []),
        compiler_params=_cp(("arbitrary",) * 3 if nc else ("parallel", "parallel", "arbitrary"), VMEM_LIMIT), name=name)(a, b, *carr)
    return (out[0], list(out[1:])) if nc else out[0]


def _piece_spans(pieces):
    spans, start, col = [], 0, 0
    for a, w, bw in pieces:
        assert w % bw == 0 and col % bw == 0
        assert not isinstance(a, (list, tuple)) or (w == bw and sum(x.shape[1] for x in a) == w)
        spans.append((start, w // bw, bw, col))
        start += w // bw
        col += w
    return spans, start


def _piece_arrays(pieces):
    arrs = [list(a) if isinstance(a, (list, tuple)) else [a] for a, _, _ in pieces]
    offs = [sum(len(x) for x in arrs[:p]) for p in range(len(arrs))]
    return arrs, offs


def _piece_value(p_refs, arrs, offs, p):
    vals = [_b(p_refs[offs[p] + j][...]) for j in range(len(arrs[p]))]
    return vals[0] if len(vals) == 1 else jnp.concatenate(vals, axis=1)


def _mm_k_pieces(pieces, w, out_dtype, tm, name, comm=None):
    spans, nk = _piece_spans(pieces)
    arrs, offs = _piece_arrays(pieces)
    M, N = arrs[0][0].shape[0], w.shape[1]
    tm = min(tm, M)
    npc = sum(len(x) for x in arrs)
    bws = sorted({bw for _, _, bw, _ in spans}, reverse=True)
    nw = len(bws)
    kind, carr = comm if comm is not None else (None, ())
    nc = len(carr)

    def body(*refs):
        p_refs, w_refs = refs[:npc], refs[npc:npc + nw]
        refs = refs[nw - 1:]
        o_ref = refs[npc + 1 + nc]
        acc_ref = refs[npc + 2 + 2 * nc]
        k = pl.program_id(1)
        if nc:
            cs_refs, cr_refs, sems = refs[npc + 1:npc + 1 + nc], refs[npc + 2 + nc:npc + 2 + 2 * nc], refs[npc + 3 + 2 * nc:]
            step = pl.program_id(0) * nk + k

            @pl.when(step == 0)
            def _():
                for cp in _exchange_copies(cs_refs, cr_refs, *sems):
                    cp.start()

        @pl.when(k == 0)
        def _():
            acc_ref[...] = jnp.zeros_like(acc_ref)

        for p, (start, n, bw, _) in enumerate(spans):
            @pl.when((k >= start) & (k < start + n))
            def _():
                acc_ref[...] += _dot(_piece_value(p_refs, arrs, offs, p), w_refs[bws.index(bw)][...])

        @pl.when(k == nk - 1)
        def _():
            o_ref[...] = acc_ref[...].astype(out_dtype)

        if nc:
            @pl.when(step == (M // tm) * nk - 1)
            def _():
                for cp in _exchange_copies(cs_refs, cr_refs, *sems):
                    cp.wait()

    assert kind in (None, "exchange")
    anyspec = pl.BlockSpec(memory_space=pl.ANY)
    p_specs = []
    for p, (s, n, bw, _) in enumerate(spans):
        if len(arrs[p]) == 1:
            p_specs.append(pl.BlockSpec((tm, bw), functools.partial(lambda i, k, s, n: (i, jnp.clip(k - s, 0, n - 1)), s=s, n=n)))
        else:
            p_specs += [pl.BlockSpec((tm, x.shape[1]), lambda i, k: (i, 0)) for x in arrs[p]]
    w_specs = []
    for bw in bws:
        mine = [sp for sp in spans if sp[2] == bw]
        first, steps, row0 = mine[0][0], sum(sp[1] for sp in mine), mine[0][3] // bw
        assert mine[-1][0] + mine[-1][1] - first == steps
        w_specs.append(pl.BlockSpec((bw, N), functools.partial(lambda i, k, f, s, r: (r + jnp.clip(k - f, 0, s - 1), 0), f=first, s=steps, r=row0)))
    out = pl.pallas_call(
        body, grid=(M // tm, nk),
        out_shape=(jax.ShapeDtypeStruct((M, N), out_dtype),) + tuple(jax.ShapeDtypeStruct(c.shape, c.dtype) for c in carr),
        in_specs=p_specs + w_specs + [anyspec] * nc,
        out_specs=(pl.BlockSpec((tm, N), lambda i, k: (i, 0)),) + tuple([anyspec] * nc),
        scratch_shapes=[pltpu.VMEM((tm, N), f32)] + (_exchange_scratch(nc) if nc else []),
        compiler_params=_cp(("arbitrary", "arbitrary"), VMEM_LIMIT), name=name)(*[x for a in arrs for x in a], *([w] * nw), *carr)
    return (out[0], list(out[1:])) if nc else out[0]


def _mm_n_pieces(a, pieces, out_dtype, tm, tk, name):
    M, K = a.shape
    tm, tk = min(tm, M), min(tk, K)
    spans, nn = _piece_spans(pieces)
    arrs, offs = _piece_arrays(pieces)
    cols = pieces[0][2]
    assert all(bw == cols for _, _, bw in pieces)
    npc, nk = sum(len(x) for x in arrs), K // tk

    def body(*refs):
        a_ref, p_refs, o_ref, acc_ref = refs[0], refs[1:1 + npc], refs[1 + npc], refs[2 + npc]
        j, k = pl.program_id(1), pl.program_id(2)

        @pl.when(k == 0)
        def _():
            acc_ref[...] = jnp.zeros_like(acc_ref)

        for p, (start, n, _, _) in enumerate(spans):
            @pl.when((j >= start) & (j < start + n))
            def _():
                acc_ref[...] += _dot(_b(a_ref[...]), _piece_value(p_refs, arrs, offs, p))

        @pl.when(k == nk - 1)
        def _():
            o_ref[...] = acc_ref[...].astype(out_dtype)

    def p_map(i, j, k, s, n):
        inside = (j >= s) & (j < s + n)
        return jnp.where(inside, k, 0), jnp.clip(j - s, 0, n - 1)

    p_specs = []
    for p, (s, n, _, _) in enumerate(spans):
        widths = [cols] if len(arrs[p]) == 1 else [x.shape[1] for x in arrs[p]]
        p_specs += [pl.BlockSpec((tk, wd), functools.partial(p_map, s=s, n=n)) for wd in widths]
    return pl.pallas_call(
        body, grid=(M // tm, nn, nk), out_shape=jax.ShapeDtypeStruct((M, nn * cols), out_dtype),
        in_specs=[pl.BlockSpec((tm, tk), lambda i, j, k: (i, k))] + p_specs,
        out_specs=pl.BlockSpec((tm, cols), lambda i, j, k: (i, j)),
        scratch_shapes=[pltpu.VMEM((tm, cols), f32)],
        compiler_params=_cp(("parallel", "parallel", "arbitrary"), VMEM_LIMIT), name=name)(a, *[x for b in arrs for x in b])


def _small_pars(b_fg, a_log, dt_bias):
    par = jnp.zeros((8, 128), f32)
    par = par.at[0, 0:8].set(b_fg).at[1, 8:12].set(a_log).at[2, 8:12].set(dt_bias)
    return par


def _small_prep(zs, par, name):
    S = zs.shape[0]
    ts = min(S, 512)

    def body(z_ref, par_ref, o_ref, carry_ref):
        i = pl.program_id(0)

        @pl.when(i == 0)
        def _():
            carry_ref[...] = jnp.zeros_like(carry_ref)

        z = z_ref[...]
        lane = lax.broadcasted_iota(jnp.int32, (ts, 128), 1)
        row = lax.broadcasted_iota(jnp.int32, (ts, 128), 0)
        za = z + par_ref[0:1, :]
        logf = jnp.minimum(za, 0.0) - jnp.log1p(jnp.exp(-jnp.abs(za)))
        glog = -jnp.exp(par_ref[1:2, :]) * _softplus(z + par_ref[2:3, :])
        x = jnp.where(lane < 8, logf, jnp.where(lane < 12, glog, 0.0))
        pos = jnp.where(lane < 8, row, row & (CH - 1))
        s = 1
        while s < ts:
            x = x + jnp.where(pos >= s, pltpu.roll(x, s, 0), 0.0)
            s *= 2
        tot = x + carry_ref[0:1, :]
        carry_ref[...] = jnp.broadcast_to(jnp.where(lane[0:1] < 8, tot[ts - 1:ts, :], 0.0), (8, 128))
        o_ref[...] = jnp.where(lane < 8, tot, jnp.where(lane < 12, x, jnp.where(lane < 16, _sig(z), 0.0)))

    return pl.pallas_call(
        body, grid=(S // ts,), out_shape=jax.ShapeDtypeStruct((S, 128), f32),
        in_specs=[pl.BlockSpec((ts, 128), lambda i: (i, 0)), pl.BlockSpec((8, 128), lambda i: (0, 0))],
        out_specs=pl.BlockSpec((ts, 128), lambda i: (i, 0)), scratch_shapes=[pltpu.VMEM((8, 128), f32)],
        compiler_params=_cp(("arbitrary",)), name=name)(zs, par)


def _small_bwd(zs, par, dfr, dfc, dsm, name):
    S = zs.shape[0]
    ts = min(S, 512)
    nt = S // ts

    def body(z_ref, par_ref, dfr_ref, dfc_ref, dsm_ref, dz_ref, acc_ref, carry_ref):
        i = pl.program_id(0)

        @pl.when(i == 0)
        def _():
            carry_ref[...] = jnp.zeros_like(carry_ref)

        z = z_ref[...]
        dsm_v = dsm_ref[...]
        lane = lax.broadcasted_iota(jnp.int32, (ts, 128), 1)
        row = lax.broadcasted_iota(jnp.int32, (ts, 128), 0)
        df = jnp.transpose(jnp.concatenate([dfr_ref[...], jnp.zeros((120, ts), f32)], axis=0))
        for p in range(4):
            dpair = dfc_ref[p]
            df = df - jnp.where(lane == 2 * p, dpair[:, 0:1], jnp.where(lane == 2 * p + 1, dpair[:, 64:65], 0.0))
        x = jnp.where(lane < 8, df, jnp.where(lane < 12, dsm_v, 0.0))
        pos = jnp.where(lane < 8, row, row & (CH - 1))
        seg = jnp.where(lane < 8, ts, CH)
        s = 1
        while s < ts:
            x = x + jnp.where(pos + s < seg, pltpu.roll(x, ts - s, 0), 0.0)
            s *= 2
        tot = x + carry_ref[0:1, :]
        carry_ref[...] = jnp.broadcast_to(jnp.where(lane[0:1] < 8, tot[0:1, :], 0.0), (8, 128))
        za = z + par_ref[0:1, :]
        daf = tot * _sig(-za)
        zb = z + par_ref[2:3, :]
        nea = -jnp.exp(par_ref[1:2, :])
        glog = nea * _softplus(zb)
        dba = x * nea * _sig(zb)
        beta = _sig(z)
        dbb = dsm_v * beta * (1.0 - beta)
        dz_ref[:, 0:128] = _b(jnp.where(lane < 8, daf, jnp.where(lane < 12, dba, jnp.where(lane < 16, dbb, 0.0))))
        dz_ref[:, 128:SMALL_COLS] = jnp.zeros((ts, SMALL_COLS - 128), bf16)
        r0 = _colsum(jnp.where(lane < 8, daf, 0.0))
        r1 = _colsum(jnp.where((lane >= 8) & (lane < 12), x * glog, 0.0))
        r2 = _colsum(jnp.where((lane >= 8) & (lane < 12), dba, 0.0))
        r8 = lax.broadcasted_iota(jnp.int32, (8, 128), 0)
        part = jnp.where(r8 == 0, r0, jnp.where(r8 == 1, r1, jnp.where(r8 == 2, r2, 0.0)))

        @pl.when(i == 0)
        def _():
            acc_ref[...] = part

        @pl.when(i > 0)
        def _():
            acc_ref[...] += part

    rev = pl.BlockSpec((ts, 128), lambda i: (nt - 1 - i, 0))
    rev4 = pl.BlockSpec((4, ts, 128), lambda i: (0, nt - 1 - i, 0))
    c8 = pl.BlockSpec((8, 128), lambda i: (0, 0))
    return pl.pallas_call(
        body, grid=(nt,), out_shape=(jax.ShapeDtypeStruct((S, SMALL_COLS), bf16), jax.ShapeDtypeStruct((8, 128), f32)),
        in_specs=[rev, c8, pl.BlockSpec((8, ts), lambda i: (0, nt - 1 - i)), rev4, rev],
        out_specs=(pl.BlockSpec((ts, SMALL_COLS), lambda i: (nt - 1 - i, 0)), c8),
        scratch_shapes=[pltpu.VMEM((8, 128), f32)],
        compiler_params=_cp(("arbitrary",)), name=name)(zs, par, dfr, dfc, dsm)


def _split3(x):
    hi = _b(x).astype(f32)
    r = x - hi
    mid = _b(r).astype(f32)
    return hi, mid, _b(r - mid).astype(f32)


FOX_PREP_ROWS = 512
FOX_TILE = 512
FOX_SKIP_LOG = -32.0


def _fox_prep(zb, sm, name):
    S = zb.shape[0]
    ts = min(S, FOX_PREP_ROWS)

    def body(q_ref, k_ref, f_ref, qa_ref, ka_ref, st_ref):
        lane = lax.broadcasted_iota(jnp.int32, (ts, 128), 1)
        lane8 = lax.broadcasted_iota(jnp.int32, (8, 128), 1)
        f = f_ref[...]
        st = jnp.zeros((8, 128), f32)
        for p in range(4):
            q = q_ref[:, p * 128:(p + 1) * 128].astype(f32) * FOX_SCALE
            k = k_ref[:, p * 128:(p + 1) * 128].astype(f32)
            for h in (0, 1):
                fcol = f[:, 2 * p + h:2 * p + h + 1]
                hi, mid, lo = _split3(fcol)
                own = (lane < 64) if h == 0 else (lane >= 64)
                nq = jnp.sqrt(_rowsum(jnp.where(own, q * q, 0.0)))
                nk = jnp.sqrt(_rowsum(jnp.where(own, k * k, 0.0)))
                stats = (jnp.max(nq, axis=0, keepdims=True), jnp.max(nk, axis=0, keepdims=True),
                         jnp.max(fcol, axis=0, keepdims=True), jnp.min(fcol, axis=0, keepdims=True),
                         jnp.min(-nq * nk, axis=0, keepdims=True))
                for si, val in enumerate(stats):
                    st = jnp.where(lane8 == 8 * si + 2 * p + h, val, st)
                o = 64 if h == 0 else 0
                ones_lo = (lane >= o) & (lane < o + 3)
                ones_hi = (lane >= o + 3) & (lane < o + 6)
                qaug = jnp.where(lane == o, hi, jnp.where(lane == o + 1, mid, jnp.where(lane == o + 2, lo, jnp.where(ones_hi, 1.0, 0.0))))
                kaug = jnp.where(lane == o + 3, -hi, jnp.where(lane == o + 4, -mid, jnp.where(lane == o + 5, -lo, jnp.where(ones_lo, 1.0, 0.0))))
                qa_ref[2 * p + h] = _b(jnp.where(own, q, qaug))
                ka_ref[2 * p + h] = _b(jnp.where(own, k, kaug))
        st_ref[...] = st

    out = jax.ShapeDtypeStruct((8, S, 128), bf16)
    return pl.pallas_call(
        body, grid=(S // ts,), out_shape=(out, out, jax.ShapeDtypeStruct((S // ts * 8, 128), f32)),
        in_specs=[pl.BlockSpec((ts, 512), lambda i: (i, 0)), pl.BlockSpec((ts, 512), lambda i: (i, 1)), pl.BlockSpec((ts, 128), lambda i: (i, 0))],
        out_specs=(pl.BlockSpec((8, ts, 128), lambda i: (0, i, 0)), pl.BlockSpec((8, ts, 128), lambda i: (0, i, 0)),
                   pl.BlockSpec((8, 128), lambda i: (i, 0))),
        compiler_params=_cp(("parallel",)), name=name)(zb, zb, sm)


def _fox_bound_table(st, S, T):
    ts = min(S, FOX_PREP_ROWS)
    g = T // ts
    nt = S // T
    s5 = st.reshape(S // ts, 8, 128)[:, 0, 0:40].reshape(nt, g, 5, 8)
    qn, kn, fmax = s5[:, :, 0].max(axis=1), s5[:, :, 1].max(axis=1), s5[:, :, 2].max(axis=1)
    fmin, lmin = s5[:, :, 3].min(axis=1), s5[:, :, 4].min(axis=1)
    e = qn[:, None] * kn[None, :] + fmax[:, None] - fmin[None, :] - lmin[:, None] + 1.0
    return jnp.transpose(e, (2, 0, 1)).reshape(8, nt * nt)


def _pair_rows(a, T):
    at = jnp.transpose(a)
    r8 = lax.broadcasted_iota(jnp.int32, (8, T), 0)
    return jnp.where(r8 == 0, at[0:1, :], at[64:65, :])


def _fox_fwd(qa, ka, zb, tab, name):
    S = zb.shape[0]
    T = min(S, FOX_TILE)
    nt = S // T

    def body(tab_ref, qa_ref, ka_ref, v_ref, o_ref, lset_ref, m_ref, l_ref, acc_ref):
        p, i = pl.program_id(0), pl.program_id(1)
        m_ref[...] = jnp.full_like(m_ref, NEG)
        l_ref[...] = jnp.zeros_like(l_ref)
        acc_ref[...] = jnp.zeros_like(acc_ref)
        row = lax.broadcasted_iota(jnp.int32, (T, T), 0)
        col = lax.broadcasted_iota(jnp.int32, (T, T), 1)

        def head_tile(h, j, masked):
            off = pl.multiple_of(j * T, T)
            s = _dot_nt(qa_ref[h], ka_ref[h, pl.ds(off, T), :])
            if masked:
                s = jnp.where(row >= col, s, NEG)
            m_old = m_ref[h]
            m_new = jnp.maximum(m_old, jnp.max(s, axis=1, keepdims=True))
            alpha = jnp.exp(m_old - m_new)
            pr = jnp.exp(s - jnp.tile(m_new, (1, T // 128)))
            l_ref[h] = alpha * l_ref[h] + _rowsum(pr)
            acc_ref[h] = alpha * acc_ref[h] + _dot(_b(pr), v_ref[pl.ds(off, T), :])
            m_ref[h] = m_new

        def step(j, c):
            for h in (0, 1):
                @pl.when(tab_ref[2 * p + h, i * nt + j] > FOX_SKIP_LOG)
                def _():
                    head_tile(h, j, False)
            return c

        lax.fori_loop(0, i, step, 0)
        for h in (0, 1):
            head_tile(h, i, True)
        lane2 = lax.broadcasted_iota(jnp.int32, (T, 128), 1)
        o_ref[...] = jnp.where(lane2 < 64, acc_ref[0] / l_ref[0], acc_ref[1] / l_ref[1])
        lse = jnp.where(lane2 < 64, m_ref[0] + jnp.log(l_ref[0]), m_ref[1] + jnp.log(l_ref[1]))
        lset_ref[0] = _pair_rows(lse, T)

    return pl.pallas_call(
        body, grid=(4, S // T),
        out_shape=(jax.ShapeDtypeStruct((S, 512), f32), jax.ShapeDtypeStruct((4, 8, S), f32)),
        in_specs=[pl.BlockSpec(memory_space=pltpu.SMEM), pl.BlockSpec((2, T, 128), lambda p, i: (p, i, 0)),
                  pl.BlockSpec((2, S, 128), lambda p, i: (p, 0, 0), pipeline_mode=pl.Buffered(1)),
                  pl.BlockSpec((S, 128), lambda p, i: (0, 8 + p), pipeline_mode=pl.Buffered(1))],
        out_specs=(pl.BlockSpec((T, 128), lambda p, i: (i, p)), pl.BlockSpec((1, 8, T), lambda p, i: (p, 0, i))),
        scratch_shapes=[pltpu.VMEM((2, T, 128), f32), pltpu.VMEM((2, T, 128), f32), pltpu.VMEM((2, T, 128), f32)],
        compiler_params=_cp(("arbitrary", "arbitrary"), VMEM_LIMIT), name=name)(tab, qa, ka, zb)


def _fox_bwd(qa, ka, zb, dob, lse_t, dl_t, tab, name, comm_sends=()):
    S = zb.shape[0]
    T = min(S, FOX_TILE)
    nq = S // T
    nc = len(comm_sends)

    def body(*refs):
        tab_ref, ka_ref, v_ref, qa_ref, do_ref, lt_ref, dt_ref = refs[:7]
        cs_refs = refs[7:7 + nc]
        dq_ref, dk_ref, dv_ref, dfc_ref, dfr_ref = refs[7 + nc:12 + nc]
        cr_refs = refs[12 + nc:12 + 2 * nc]
        dqa_ref, dka_ref, dva_ref, fs_ref = refs[12 + 2 * nc:16 + 2 * nc]
        p, j = pl.program_id(0), pl.program_id(1)
        if nc:
            @pl.when((p == 0) & (j == 0))
            def _():
                for cp in _exchange_copies(cs_refs, cr_refs, *refs[16 + 2 * nc:]):
                    cp.start()
        lane1 = lax.broadcasted_iota(jnp.int32, (1, 128), 1)
        lane2 = lax.broadcasted_iota(jnp.int32, (T, 128), 1)
        hm = (lane1 < 64, lane1 >= 64)
        v = v_ref[...]
        vsm = [jnp.where(hm[h], v, jnp.zeros_like(v)) for h in (0, 1)]
        ksm = [jnp.where(hm[h], ka_ref[h], jnp.zeros_like(v)) for h in (0, 1)]

        @pl.when(j == 0)
        def _():
            dqa_ref[...] = jnp.zeros_like(dqa_ref)
            dfr_ref[...] = jnp.zeros_like(dfr_ref)

        dka_ref[...] = jnp.zeros_like(dka_ref)
        dva_ref[...] = jnp.zeros_like(dva_ref)
        fs_ref[...] = jnp.zeros_like(fs_ref)
        row = lax.broadcasted_iota(jnp.int32, (T, T), 0)
        col = lax.broadcasted_iota(jnp.int32, (T, T), 1)

        def head_tile(h, i, masked):
            off = pl.multiple_of(i * T, T)
            dot_ = do_ref[pl.ds(off, T), :]
            hr = pl.ds(2 * p + h, 1)
            qt = qa_ref[h, pl.ds(off, T), :]
            s_t = _dot_nt(ka_ref[h], qt)
            if masked:
                s_t = jnp.where(col >= row, s_t, NEG)
            p_t = jnp.exp(s_t - lt_ref[hr, pl.ds(off, T)])
            dva_ref[h] += _dot(_b(p_t), dot_)
            dp_t = _dot_nt(vsm[h], dot_)
            ds_t = p_t * (dp_t - dt_ref[hr, pl.ds(off, T)])
            dsb = _b(ds_t)
            dka_ref[h] += _dot(dsb, qt)
            fs_ref[h] += _rowsum(ds_t)
            dfr_ref[0, pl.ds(h, 1), pl.ds(off, T)] += _colsum(ds_t)
            dqa_ref[pl.ds(off, T), :] += _dot_tn(dsb, ksm[h])

        def step(i, c):
            for h in (0, 1):
                @pl.when(tab_ref[2 * p + h, i * nq + j] > FOX_SKIP_LOG)
                def _():
                    head_tile(h, i, False)
            return c

        for h in (0, 1):
            head_tile(h, j, True)
        lax.fori_loop(j + 1, nq, step, 0)
        dk_ref[...] = _b(jnp.where(lane2 < 64, dka_ref[0], dka_ref[1]))
        dv_ref[...] = _b(jnp.where(lane2 < 64, dva_ref[0], dva_ref[1]))
        dfc_ref[0] = jnp.where(lane2 < 64, fs_ref[0], fs_ref[1])
        dq_ref[...] = _b(dqa_ref[pl.ds(pl.multiple_of(j * T, T), T), :] * FOX_SCALE)
        if nc:
            @pl.when((p == 3) & (j == nq - 1))
            def _():
                for cp in _exchange_copies(cs_refs, cr_refs, *refs[16 + 2 * nc:]):
                    cp.wait()

    one = pl.Buffered(1)
    res = pl.BlockSpec((8, S), lambda p, j: (0, 0), pipeline_mode=one)
    tk = pl.BlockSpec((T, 128), lambda p, j: (j, p))
    anyspec = pl.BlockSpec(memory_space=pl.ANY)
    outs = pl.pallas_call(
        body, grid=(4, nq),
        out_shape=(jax.ShapeDtypeStruct((S, 512), bf16), jax.ShapeDtypeStruct((S, 512), bf16), jax.ShapeDtypeStruct((S, 512), bf16),
                   jax.ShapeDtypeStruct((4, S, 128), f32), jax.ShapeDtypeStruct((4, 8, S), f32))
        + tuple(jax.ShapeDtypeStruct(c.shape, c.dtype) for c in comm_sends),
        in_specs=[pl.BlockSpec(memory_space=pltpu.SMEM),
                  pl.BlockSpec((2, T, 128), lambda p, j: (p, j, 0)), pl.BlockSpec((T, 128), lambda p, j: (j, 8 + p)),
                  pl.BlockSpec((2, S, 128), lambda p, j: (p, 0, 0), pipeline_mode=one),
                  pl.BlockSpec((S, 128), lambda p, j: (0, p), pipeline_mode=one), res, res] + [anyspec] * nc,
        out_specs=(tk, tk, tk, pl.BlockSpec((1, T, 128), lambda p, j: (p, j, 0)),
                   pl.BlockSpec((1, 8, S), lambda p, j: (p, 0, 0))) + tuple([anyspec] * nc),
        scratch_shapes=[pltpu.VMEM((S, 128), f32), pltpu.VMEM((2, T, 128), f32), pltpu.VMEM((2, T, 128), f32), pltpu.VMEM((2, T, 1), f32)]
        + (_exchange_scratch(nc) if nc else []),
        compiler_params=_cp(("arbitrary", "arbitrary"), VMEM_LIMIT), name=name)(tab, ka, zb, qa, dob, lse_t, dl_t, *comm_sends)
    return outs[:5], list(outs[5:])


def _head_rows(a):
    return a[:, 0:2, :].reshape(8, a.shape[2])


def _conv_taps(ext, x, w_ref, ts):
    y = x * w_ref[3:4, :]
    shifted = []
    for k in (1, 2, 3):
        xs = pltpu.roll(ext, k, 0)[8:]
        shifted.append(xs)
        y = y + xs * w_ref[3 - k:4 - k, :]
    return y, shifted


def _gdn_prep(zf, cw, name):
    S = zf.shape[0]
    ts = min(S, 512)

    def body(x_ref, w_ref, o_ref, tail_ref):
        i = pl.program_id(0)

        @pl.when(i == 0)
        def _():
            tail_ref[...] = jnp.zeros_like(tail_ref)

        x = x_ref[...]
        ext = jnp.concatenate([tail_ref[...], x], axis=0)
        y, _ = _conv_taps(ext, x, w_ref, ts)
        tail_ref[...] = x[ts - 8:, :]
        a = _silu(y)
        for hb in range(12):
            blk = a[:, hb * 128:(hb + 1) * 128]
            if hb < 8:
                blk = blk * lax.rsqrt(_rowsum(blk * blk) + EPS)
            if hb < 4:
                blk = blk * GDN_SCALE
            o_ref[:, hb * 128:(hb + 1) * 128] = blk

    return pl.pallas_call(
        body, grid=(S // ts,), out_shape=jax.ShapeDtypeStruct((S, 1536), f32),
        in_specs=[pl.BlockSpec((ts, 1536), lambda i: (i, 0)), pl.BlockSpec((8, 1536), lambda i: (0, 0))],
        out_specs=pl.BlockSpec((ts, 1536), lambda i: (i, 0)), scratch_shapes=[pltpu.VMEM((8, 1536), f32)],
        compiler_params=_cp(("arbitrary",), VMEM_LIMIT), name=name)(zf, cw)


def _gdn_prep_bwd(zf, cw, dg, name):
    S = zf.shape[0]
    ts = min(S, 512)
    nt = S // ts

    def body(x_ref, xp_ref, w_ref, dg_ref, dx_ref, dw_ref, head_ref):
        i = pl.program_id(0)

        @pl.when(i == 0)
        def _():
            head_ref[...] = jnp.zeros_like(head_ref)

        x = x_ref[...]
        prev = jnp.where(i == nt - 1, 0.0, xp_ref[...])
        ext = jnp.concatenate([prev, x], axis=0)
        y, shifted = _conv_taps(ext, x, w_ref, ts)
        a = _silu(y)
        das = []
        for hb in range(12):
            blk = a[:, hb * 128:(hb + 1) * 128]
            d = dg_ref[:, hb * 128:(hb + 1) * 128]
            if hb < 4:
                d = d * GDN_SCALE
            if hb < 8:
                r = lax.rsqrt(_rowsum(blk * blk) + EPS)
                n = blk * r
                d = r * (d - n * _rowsum(d * n))
            das.append(d)
        dy = jnp.concatenate(das, axis=1) * _dsilu(y)
        extd = jnp.concatenate([dy, head_ref[...]], axis=0)
        dx = dy * w_ref[3:4, :]
        for k in (1, 2, 3):
            dx = dx + pltpu.roll(extd, ts + 8 - k, 0)[:ts] * w_ref[3 - k:4 - k, :]
        head_ref[...] = dy[0:8, :]
        dx_ref[...] = _b(dx)
        r8 = lax.broadcasted_iota(jnp.int32, (8, 1536), 0)
        part = jnp.where(r8 == 3, _colsum(dy * x), 0.0)
        for k in (1, 2, 3):
            part = jnp.where(r8 == 3 - k, _colsum(dy * shifted[k - 1]), part)

        @pl.when(i == 0)
        def _():
            dw_ref[...] = part

        @pl.when(i > 0)
        def _():
            dw_ref[...] += part

    rev = pl.BlockSpec((ts, 1536), lambda i: (nt - 1 - i, 0))
    prev8 = pl.BlockSpec((8, 1536), lambda i: (jnp.maximum((nt - 1 - i) * (ts // 8) - 1, 0), 0))
    w8 = pl.BlockSpec((8, 1536), lambda i: (0, 0))
    return pl.pallas_call(
        body, grid=(nt,), out_shape=(jax.ShapeDtypeStruct((S, 1536), bf16), jax.ShapeDtypeStruct((8, 1536), f32)),
        in_specs=[rev, prev8, w8, rev], out_specs=(rev, w8), scratch_shapes=[pltpu.VMEM((8, 1536), f32)],
        compiler_params=_cp(("arbitrary",), VMEM_LIMIT), name=name)(zf, zf, cw, dg)


def _tri_inv(a, row, col):
    same = (row >> 4) == (col >> 4)
    dm = jnp.where(same, a, 0.0)
    lo = a - dm
    eye = jnp.where(row == col, 1.0, 0.0)
    d2 = _hi_b(dm, dm)
    d4 = _hi_b(d2, d2)
    d8 = _hi_b(d4, d4)
    x0 = _hi_b(_hi_b(eye - dm, eye + d2), _hi_b(eye + d4, eye + d8))
    n = _hi_b(x0, lo)
    n2 = _hi_b(n, n)
    return _hi_b(_hi_b(eye - n, eye + n2), x0)


def _bd(a, b):
    return lax.dot_general(a, b, (((2,), (1,)), ((0,), (0,))), preferred_element_type=f32)


def _bd_nt(a, b):
    return lax.dot_general(a, b, (((2,), (2,)), ((0,), (0,))), preferred_element_type=f32)


def _bd_tn(a, b):
    return lax.dot_general(a, b, (((1,), (1,)), ((0,), (0,))), preferred_element_type=f32)


def _hi_b_nt(a, b):
    return _mm3(a, b, (((2,), (2,)), ((0,), (0,))))


def _hi_b_tn(a, b):
    return _mm3(a, b, (((1,), (1,)), ((0,), (0,))))


def _gdn_local(x_ref, sm_ref, gt_ref, row, col, cps=1):
    idx = [(c, h) for c in range(cps) for h in range(4)]

    def rows(c):
        return slice(c * CH, (c + 1) * CH)

    q = jnp.stack([x_ref[rows(c), h * 128:(h + 1) * 128] for c, h in idx])
    k = jnp.stack([x_ref[rows(c), 512 + h * 128:512 + (h + 1) * 128] for c, h in idx])
    v = jnp.stack([x_ref[rows(c), 1024 + h * 128:1024 + (h + 1) * 128] for c, h in idx])
    gc = jnp.stack([sm_ref[rows(c), 8 + h:9 + h] for c, h in idx])
    beta = jnp.stack([sm_ref[rows(c), 12 + h:13 + h] for c, h in idx])
    gr = jnp.stack([gt_ref[h, c] for c, h in idx])
    eg = jnp.exp(gc)
    gl = gc[:, CH - 1:CH, :]
    dec = jnp.exp(gl - gc)
    gm = gc - gr
    gam_i = jnp.exp(jnp.where(row >= col, gm, -jnp.inf))
    gam_s = jnp.where(row > col, gam_i, 0.0)
    kb = k * beta
    return dict(q=q, k=k, v=v, beta=beta, eg=eg, egl=jnp.exp(gl), dec=dec, gam_i=gam_i, gam_s=gam_s,
                kb=kb, vb=v * beta, kbg=kb * eg, qdec=q * eg, kdec=k * dec,
                a=_bd_nt(_b(kb), _b(k)) * gam_s, aqk=_bd_nt(_b(q), _b(k)) * gam_i)


GDN_FWD_CHUNKS = 8


def _gdn_fwd(gqkv, sm, gt4, name, cps=GDN_FWD_CHUNKS):
    S = gqkv.shape[0]
    N = S // CH
    cps = min(cps, N)
    R = cps * CH

    def body(x_ref, sm_ref, gt_ref, o_ref, t_ref, st_ref, s_ref):
        n = pl.program_id(0)

        @pl.when(n == 0)
        def _():
            s_ref[...] = jnp.zeros_like(s_ref)

        row = lax.broadcasted_iota(jnp.int32, (CH, CH), 0)
        col = lax.broadcasted_iota(jnp.int32, (CH, CH), 1)
        c = _gdn_local(x_ref, sm_ref, gt_ref, row, col, cps)
        t = _tri_inv(c["a"], row, col)
        uw = _hi_b(t, jnp.concatenate([c["vb"], c["kbg"]], axis=2))
        u, w = uw[:, :, :128], uw[:, :, 128:]
        for ci in range(cps):
            sl = slice(4 * ci, 4 * ci + 4)
            rs = slice(ci * CH, (ci + 1) * CH)
            st = s_ref[...]
            st_ref[ci] = st
            sb = _b(st)
            vnew = u[sl] - _bd(_b(w[sl]), sb)
            o = _bd(_b(c["qdec"][sl]), sb) + _bd(_b(c["aqk"][sl]), _b(vnew))
            for h in range(4):
                o_ref[rs, h * 128:(h + 1) * 128] = o[h]
                t_ref[h, rs, :] = t[4 * ci + h]
            s_ref[...] = st * c["egl"][sl] + _bd_tn(_b(c["kdec"][sl]), _b(vnew))

    return pl.pallas_call(
        body, grid=(N // cps,),
        out_shape=(jax.ShapeDtypeStruct((S, 512), f32), jax.ShapeDtypeStruct((4, S, CH), f32), jax.ShapeDtypeStruct((N, 4, 128, 128), f32)),
        in_specs=[pl.BlockSpec((R, 1536), lambda n: (n, 0)), pl.BlockSpec((R, 128), lambda n: (n, 0)),
                  pl.BlockSpec((4, cps, 1, CH), lambda n: (0, n, 0, 0))],
        out_specs=(pl.BlockSpec((R, 512), lambda n: (n, 0)), pl.BlockSpec((4, R, CH), lambda n: (0, n, 0)),
                   pl.BlockSpec((cps, 4, 128, 128), lambda n: (n, 0, 0, 0))),
        scratch_shapes=[pltpu.VMEM((4, 128, 128), f32)], compiler_params=_cp(("arbitrary",)), name=name)(gqkv, sm, gt4)


GDN_BWD_CHUNKS = 4


def _gdn_bwd(gqkv, sm, gt4, tinv, states, do, name, cps=GDN_BWD_CHUNKS):
    S = gqkv.shape[0]
    N = S // CH
    cps = min(cps, N)
    R = cps * CH

    def body(x_ref, sm_ref, gt_ref, t_ref, st_ref, do_ref, dx_ref, dsm_ref, ds_ref):
        n = pl.program_id(0)

        @pl.when(n == 0)
        def _():
            ds_ref[...] = jnp.zeros_like(ds_ref)

        row = lax.broadcasted_iota(jnp.int32, (CH, CH), 0)
        col = lax.broadcasted_iota(jnp.int32, (CH, CH), 1)
        row1 = lax.broadcasted_iota(jnp.int32, (CH, 1), 0)
        lane = lax.broadcasted_iota(jnp.int32, (CH, 128), 1)
        ones = jnp.ones((4 * cps, CH, 128), f32)
        idx = [(ci, h) for ci in range(cps) for h in range(4)]
        c = _gdn_local(x_ref, sm_ref, gt_ref, row, col, cps)
        q, k, v, beta, eg = c["q"], c["k"], c["v"], c["beta"], c["eg"]
        t = jnp.stack([t_ref[h, ci * CH:(ci + 1) * CH, :] for ci, h in idx])
        uw = _hi_b(t, jnp.concatenate([c["vb"], c["kbg"]], axis=2))
        u, w = uw[:, :, :128], uw[:, :, 128:]
        st = st_ref[...].reshape(4 * cps, 128, 128)
        sb = _b(st)
        vnew = u - _bd(_b(w), sb)
        dob = _b(jnp.stack([do_ref[ci * CH:(ci + 1) * CH, h * 128:(h + 1) * 128] for ci, h in idx]))
        vnb = _b(vnew)
        dqdec = _bd_nt(dob, sb)
        daqk = jnp.where(row >= col, _bd_nt(dob, vnb), 0.0)
        qd_do = _bd_tn(_b(c["qdec"]), dob)
        aqk_do = _bd_tn(_b(c["aqk"]), dob)
        kdecb, wb = _b(c["kdec"]), _b(w)
        dvnew_l, dkdec_l, dgl_l = [None] * cps, [None] * cps, [None] * cps
        for ci in reversed(range(cps)):
            sl = slice(4 * ci, 4 * ci + 4)
            dsp = ds_ref[...]
            dspb = _b(dsp)
            dvn = _bd(kdecb[sl], dspb) + aqk_do[sl]
            dvnew_l[ci] = dvn
            dkdec_l[ci] = _bd_nt(vnb[sl], dspb)
            dgl_l[ci] = c["egl"][sl] * jnp.sum(dsp * st[sl], axis=(1, 2), keepdims=True)
            ds_ref[...] = dsp * c["egl"][sl] + qd_do[sl] - _bd_tn(wb[sl], _b(dvn))
        dvnew = jnp.concatenate(dvnew_l, axis=0)
        dkdec = jnp.concatenate(dkdec_l, axis=0)
        dgl = jnp.concatenate(dgl_l, axis=0)
        dw = -_bd_nt(_b(dvnew), sb)
        duw = _hi_b_tn(t, jnp.concatenate([dvnew, dw], axis=2))
        dvb, dkbg = duw[:, :, :128], duw[:, :, 128:]
        da = -jnp.where(row > col, _hi_b_nt(duw, uw), 0.0)
        dp = da * c["gam_s"]
        dqk = daqk * c["gam_i"]
        m = da * c["a"] + daqk * c["aqk"]
        csum = _hi_b_tn(m, ones)[:, :, 0:1]
        kk = dkdec * c["kdec"]

        def lsum(a):
            return jnp.sum(a, axis=2, keepdims=True)

        dgv = lsum(m) - csum + lsum(dqdec * c["qdec"]) - lsum(kk) + lsum(dkbg * c["kbg"])
        dgv = dgv + jnp.where(row1 == CH - 1, dgl + jnp.sum(kk, axis=(1, 2), keepdims=True), 0.0)
        dpb, dqkb = _b(dp), _b(dqk)
        dkb = _bd(dpb, _b(k)) + dkbg * eg
        dk = _bd_tn(dpb, _b(c["kb"])) + _bd_tn(dqkb, _b(q)) + dkdec * c["dec"] + dkb * beta
        dq = _bd(dqkb, _b(k)) + dqdec * eg
        dbeta = lsum(dkb * k) + lsum(dvb * v)
        dv = dvb * beta
        for ci in range(cps):
            rs = slice(ci * CH, (ci + 1) * CH)
            dsm = jnp.zeros((CH, 128), f32)
            for h in range(4):
                b = 4 * ci + h
                dx_ref[rs, h * 128:(h + 1) * 128] = dq[b]
                dx_ref[rs, 512 + h * 128:512 + (h + 1) * 128] = dk[b]
                dx_ref[rs, 1024 + h * 128:1024 + (h + 1) * 128] = dv[b]
                dsm = jnp.where(lane == 8 + h, dgv[b], jnp.where(lane == 12 + h, dbeta[b], dsm))
            dsm_ref[rs, :] = dsm

    G = N // cps
    return pl.pallas_call(
        body, grid=(G,), out_shape=(jax.ShapeDtypeStruct((S, 1536), f32), jax.ShapeDtypeStruct((S, 128), f32)),
        in_specs=[pl.BlockSpec((R, 1536), lambda n: (G - 1 - n, 0)), pl.BlockSpec((R, 128), lambda n: (G - 1 - n, 0)),
                  pl.BlockSpec((4, cps, 1, CH), lambda n: (0, G - 1 - n, 0, 0)), pl.BlockSpec((4, R, CH), lambda n: (0, G - 1 - n, 0)),
                  pl.BlockSpec((cps, 4, 128, 128), lambda n: (G - 1 - n, 0, 0, 0)), pl.BlockSpec((R, 512), lambda n: (G - 1 - n, 0))],
        out_specs=(pl.BlockSpec((R, 1536), lambda n: (G - 1 - n, 0)), pl.BlockSpec((R, 128), lambda n: (G - 1 - n, 0))),
        scratch_shapes=[pltpu.VMEM((4, 128, 128), f32)], compiler_params=_cp(("arbitrary",), VMEM_LIMIT), name=name)(gqkv, sm, gt4, tinv, states, do)


MERGE_ROWS = 256
MERGE_FWD_ROWS = 512


def _mem_attn(q, kv_ref, h):
    s = _dot_nt(q, kv_ref[:, h * 128:(h + 1) * 128]) * MEM_SCALE
    e = jnp.exp(s - jnp.max(s, axis=1, keepdims=True))
    return e / _rowsum(e)


def _gdn_out_norm(ob):
    r = lax.rsqrt(jnp.mean(ob * ob, axis=-1, keepdims=True) + EPS)
    return ob * r, r


def _merge_fwd(x, oa, ob, zb, zf, kv, b_merge, gdn_g, w_branch, w_out, name):
    S = x.shape[0]
    ts = min(S, MERGE_FWD_ROWS)

    def body(x_ref, oa_ref, ob_ref, mq_ref, az_ref, bz_ref, mz_ref, gt_ref, kv_ref, bm_ref, gg_ref, wb_ref, wo_ref,
             xo_ref, y_ref, mg_ref):
        y_ref[:, 0:512] = _b(oa_ref[...] * _silu(az_ref[...]))
        for h in range(4):
            sl = slice(h * 128, (h + 1) * 128)
            nb, _ = _gdn_out_norm(ob_ref[:, sl])
            y_ref[:, 512 + h * 128:512 + (h + 1) * 128] = _b(nb * gg_ref[...] * _silu(bz_ref[:, sl]))
            pm = _mem_attn(mq_ref[:, sl], kv_ref, h)
            om = _dot(_b(pm), kv_ref[:, 512 + h * 128:512 + (h + 1) * 128])
            y_ref[:, 1024 + h * 128:1024 + (h + 1) * 128] = _b(om * _silu(mz_ref[:, sl]))
        merged = jnp.zeros((ts, D), f32)
        for n in range(3):
            gate = _sig(gt_ref[:, n * D:(n + 1) * D] + bm_ref[:, n * D:(n + 1) * D])
            merged = merged + gate * _dot(y_ref[:, n * 512:(n + 1) * 512], wb_ref[n])
        mb = _b(merged)
        mg_ref[...] = mb
        xo_ref[...] = x_ref[...] + _dot(mb, wo_ref[...])

    def col(w, c):
        return pl.BlockSpec((ts, w), lambda i: (i, c))

    def full(shape):
        return pl.BlockSpec(shape, lambda i: tuple(0 for _ in shape))

    return pl.pallas_call(
        body, grid=(S // ts,),
        out_shape=(jax.ShapeDtypeStruct((S, D), f32), jax.ShapeDtypeStruct((S, 1536), bf16), jax.ShapeDtypeStruct((S, D), bf16)),
        in_specs=[col(D, 0), col(512, 0), col(512, 0), col(512, 3), col(512, 3), col(512, 4), col(512, 5), col(3072, 1),
                  full((256, D)), full((1, 3072)), full((1, 128)), full((3, 512, D)), full((D, D))],
        out_specs=(col(D, 0), col(1536, 0), col(D, 0)),
        compiler_params=_cp(("parallel",), VMEM_LIMIT), name=name)(x, oa, ob, zb, zf, zf, zf, zf, kv, b_merge, gdn_g, w_branch, w_out)


def _merge_bwd(dout, ycat, oa, ob, zb, zf, kv, b_merge, gdn_g, w_branch, w_branch_t, w_out_t, name):
    S = dout.shape[0]
    ts = min(S, MERGE_ROWS)

    def body(do_ref, y_ref, oa_ref, ob_ref, mq_ref, az_ref, bz_ref, mz_ref, gt_ref, kv_ref, bm_ref, gg_ref, wb_ref, wbt_ref, wot_ref,
             dpj_ref, dz_ref, dmq_ref, dlt_ref, doab_ref, dob_ref, dkv_ref, dbm_ref, dgg_ref):
        i = pl.program_id(0)
        dmerged = _dot(_b(do_ref[...]), wot_ref[...])
        dys = []
        dbm_parts = []
        for n in range(3):
            cs = slice(n * D, (n + 1) * D)
            gate = _sig(gt_ref[:, cs] + bm_ref[:, cs])
            proj = _dot(y_ref[:, n * 512:(n + 1) * 512], wb_ref[n])
            dlogit = dmerged * proj * gate * (1.0 - gate)
            dz_ref[:, 1536 + n * D:1536 + (n + 1) * D] = _b(dlogit)
            dbm_parts.append(_colsum(dlogit))
            dproj = _b(dmerged * gate)
            dpj_ref[:, cs] = dproj
            dys.append(_dot(dproj, wbt_ref[n]))
        dbm = jnp.broadcast_to(jnp.concatenate(dbm_parts, axis=1), (8, 3072))
        az = az_ref[...]
        oa = oa_ref[...]
        doa = dys[0] * _silu(az)
        doab_ref[...] = _b(doa)
        prod = doa * oa
        lane = lax.broadcasted_iota(jnp.int32, (ts, 128), 1)
        dl = jnp.zeros((ts, 128), f32)
        for p in range(4):
            blk = prod[:, p * 128:(p + 1) * 128]
            dl = jnp.where(lane == 2 * p, _rowsum(jnp.where(lane < 64, blk, 0.0)),
                           jnp.where(lane == 2 * p + 1, _rowsum(jnp.where(lane >= 64, blk, 0.0)), dl))
        dlt_ref[...] = jnp.transpose(dl)[0:8, :]
        dz_ref[:, 0:512] = _b(dys[0] * oa * _dsilu(az))
        gg = gg_ref[...]
        dgg = jnp.zeros((1, 128), f32)
        dkv_parts_k, dkv_parts_v = [], []
        for h in range(4):
            sl = slice(h * 128, (h + 1) * 128)
            bz = bz_ref[:, sl]
            dyb = dys[1][:, sl]
            nb, r = _gdn_out_norm(ob_ref[:, sl])
            dz_ref[:, 512 + h * 128:512 + (h + 1) * 128] = _b(dyb * nb * gg * _dsilu(bz))
            dng = dyb * _silu(bz)
            dgg = dgg + _colsum(dng * nb)
            dnb = dng * gg
            dob_ref[:, sl] = r * (dnb - nb * jnp.mean(dnb * nb, axis=-1, keepdims=True))
            mz = mz_ref[:, sl]
            dym = dys[2][:, sl]
            q = mq_ref[:, sl]
            kh = kv_ref[:, sl]
            vh = kv_ref[:, 512 + h * 128:512 + (h + 1) * 128]
            pm = _mem_attn(q, kv_ref, h)
            pmb = _b(pm)
            om = _dot(pmb, vh)
            dz_ref[:, 1024 + h * 128:1024 + (h + 1) * 128] = _b(dym * om * _dsilu(mz))
            dom = _b(dym * _silu(mz))
            dkv_parts_v.append(_dot_tn(pmb, dom))
            dpm = _dot_nt(dom, vh)
            dsm = _b(pm * (dpm - _rowsum(dpm * pm)) * MEM_SCALE)
            dmq_ref[:, sl] = _b(_dot(dsm, kh))
            dkv_parts_k.append(_dot_tn(dsm, q))
        dkv = jnp.concatenate(dkv_parts_k + dkv_parts_v, axis=1)
        dggb = jnp.broadcast_to(dgg, (8, 128))

        @pl.when(i == 0)
        def _():
            dkv_ref[...] = dkv
            dbm_ref[...] = dbm
            dgg_ref[...] = dggb

        @pl.when(i > 0)
        def _():
            dkv_ref[...] += dkv
            dbm_ref[...] += dbm
            dgg_ref[...] += dggb

    def col(w, c):
        return pl.BlockSpec((ts, w), lambda i: (i, c))

    def full(shape):
        return pl.BlockSpec(shape, lambda i: tuple(0 for _ in shape))

    return pl.pallas_call(
        body, grid=(S // ts,),
        out_shape=(jax.ShapeDtypeStruct((S, 3072), bf16), jax.ShapeDtypeStruct((S, DZ_MERGE_COLS), bf16), jax.ShapeDtypeStruct((S, 512), bf16),
                   jax.ShapeDtypeStruct((8, S), f32), jax.ShapeDtypeStruct((S, 512), bf16), jax.ShapeDtypeStruct((S, 512), f32),
                   jax.ShapeDtypeStruct((256, D), f32), jax.ShapeDtypeStruct((8, 3072), f32), jax.ShapeDtypeStruct((8, 128), f32)),
        in_specs=[col(D, 0), col(1536, 0), col(512, 0), col(512, 0), col(512, 3), col(512, 3), col(512, 4), col(512, 5), col(3072, 1),
                  full((256, D)), full((1, 3072)), full((1, 128)), full((3, 512, D)), full((3, D, 512)), full((D, D))],
        out_specs=(col(3072, 0), col(4608, 0), col(512, 0), pl.BlockSpec((8, ts), lambda i: (0, i)), col(512, 0), col(512, 0),
                   full((256, D)), full((8, 3072)), full((8, 128))),
        compiler_params=_cp(("arbitrary",), VMEM_LIMIT), name=name)(
            dout, ycat, oa, ob, zb, zf, zf, zf, zf, kv, b_merge, gdn_g, w_branch, w_branch_t, w_out_t)


def _mesh_pos():
    return lax.axis_index("x"), lax.axis_index("y"), lax.axis_index("c")


class _Gather:
    def __init__(self, x_refs, out_refs, send_sems, recv_sems, local_sems):
        self.n = len(x_refs)
        self.x_refs, self.out_refs = x_refs, out_refs
        self.send_sems, self.recv_sems, self.local_sems = send_sems, recv_sems, local_sems
        mx, my, mc = _mesh_pos()
        self.mc = mc
        self.me, self.sibling = (mx, my, mc), (mx, my, 1 - mc)
        self.chips = [(1 - mx, my), (mx, 1 - my), (1 - mx, 1 - my)]

    def copy(self, a, k, block, to, src=None):
        px, py, pc = block
        slot = self.out_refs[a].at[4 * px + 2 * py + pc]
        return pltpu.make_async_remote_copy(
            src_ref=slot if src is None else src, dst_ref=slot, send_sem=self.send_sems.at[7 * a + k],
            recv_sem=self.recv_sems.at[7 * a + k], device_id=to, device_id_type=pl.DeviceIdType.MESH)

    def own(self):
        mx, my, mc = self.me
        mine = [pltpu.make_async_copy(self.x_refs[a], self.out_refs[a].at[4 * mx + 2 * my + mc], self.local_sems.at[a])
                for a in range(self.n)]
        first = []
        for a in range(self.n):
            first.append(self.copy(a, 0, self.me, self.sibling, src=self.x_refs[a]))
            first += [self.copy(a, 1 + j, self.me, (*chip, self.mc), src=self.x_refs[a]) for j, chip in enumerate(self.chips)]
        return mine, first

    def start(self):
        mine, first = self.own()
        for cp in mine + first:
            cp.start()

    def finish(self):
        mine, first = self.own()
        passed = []
        for j, chip in enumerate(self.chips):
            for a in range(self.n):
                self.copy(a, 1 + j, (*chip, self.mc), self.me).wait_recv()
                fwd = self.copy(a, 4 + j, (*chip, self.mc), self.sibling)
                fwd.start()
                passed.append(fwd)
        for a in range(self.n):
            self.copy(a, 0, self.sibling, self.me).wait_recv()
            for j, chip in enumerate(self.chips):
                self.copy(a, 4 + j, (*chip, 1 - self.mc), self.me).wait_recv()
        for cp in first + passed:
            cp.wait_send()
        for cp in mine:
            cp.wait()


def _all_gather(xs, name):
    n = len(xs)

    def body(*refs):
        g = _Gather(refs[:n], refs[n:2 * n], *refs[2 * n:])
        g.start()
        g.finish()

    anyspec = pl.BlockSpec(memory_space=pl.ANY)
    return pl.pallas_call(
        body, out_shape=tuple(jax.ShapeDtypeStruct((N_DEV,) + x.shape, x.dtype) for x in xs),
        in_specs=[anyspec] * n, out_specs=tuple([anyspec] * n), scratch_shapes=_exchange_scratch(n), name=name)(*xs)


def _exchange_copies(s_refs, r_refs, send_sems, recv_sems, local_sems):
    n = len(s_refs)
    mx, my, mc = _mesh_pos()
    me_id = 4 * mx + 2 * my + mc
    copies = [pltpu.make_async_copy(s_refs[a].at[me_id], r_refs[a].at[me_id], local_sems.at[a]) for a in range(n)]
    for k in range(1, N_DEV):
        px = 1 - mx if k & 4 else mx
        py = 1 - my if k & 2 else my
        pc = 1 - mc if k & 1 else mc
        for a in range(n):
            copies.append(pltpu.make_async_remote_copy(
                src_ref=s_refs[a].at[4 * px + 2 * py + pc], dst_ref=r_refs[a].at[me_id],
                send_sem=send_sems.at[7 * a + k - 1], recv_sem=recv_sems.at[7 * a + k - 1],
                device_id=(px, py, pc), device_id_type=pl.DeviceIdType.MESH))
    return copies


def _exchange_scratch(n):
    return [pltpu.SemaphoreType.DMA((7 * n,)), pltpu.SemaphoreType.DMA((7 * n,)), pltpu.SemaphoreType.DMA((n,))]


def _exchange(sends, name):
    n = len(sends)

    def body(*refs):
        copies = _exchange_copies(refs[:n], refs[n:2 * n], *refs[2 * n:])
        for cp in copies:
            cp.start()
        for cp in copies:
            cp.wait()

    anyspec = pl.BlockSpec(memory_space=pl.ANY)
    return pl.pallas_call(
        body, out_shape=tuple(jax.ShapeDtypeStruct(s.shape, s.dtype) for s in sends),
        in_specs=[anyspec] * n, out_specs=tuple([anyspec] * n), scratch_shapes=_exchange_scratch(n), name=name)(*sends)


ADAMW_BLOCK_BYTES = 4 * 1024 * 1024


def _adamw(parts, w, m, v, name):
    _, R, C = parts.shape
    tr = R
    for t in (1024, 512, 256, 128, 64, 32, 16, 8):
        if R % t == 0 and N_DEV * t * C * 4 <= ADAMW_BLOCK_BYTES:
            tr = t
            break

    def body(p_ref, w_ref, m_ref, v_ref, g_ref, d_ref, nm_ref, nv_ref):
        g = p_ref[0].astype(f32)
        for j in range(1, N_DEV):
            g = g + p_ref[j].astype(f32)
        mn = ADAM_B1 * m_ref[...] + (1.0 - ADAM_B1) * g
        vn = ADAM_B2 * v_ref[...] + (1.0 - ADAM_B2) * jnp.square(g)
        m_hat = mn / (1.0 - ADAM_B1 ** ADAM_STEP)
        v_hat = vn / (1.0 - ADAM_B2 ** ADAM_STEP)
        g_ref[...] = g
        d_ref[...] = -ADAM_LR * (m_hat / (jnp.sqrt(v_hat) + ADAM_EPS) + ADAM_WD * w_ref[...])
        nm_ref[...] = mn
        nv_ref[...] = vn

    t2 = pl.BlockSpec((tr, C), lambda i: (i, 0))
    out = jax.ShapeDtypeStruct((R, C), f32)
    return pl.pallas_call(
        body, grid=(R // tr,), out_shape=(out, out, out, out),
        in_specs=[pl.BlockSpec((N_DEV, tr, C), lambda i: (0, i, 0)), t2, t2, t2], out_specs=(t2, t2, t2, t2),
        compiler_params=_cp(("parallel",), VMEM_LIMIT), name=name)(parts, w, m, v)


def _as2d(a):
    return a.reshape(-1, a.shape[-1])


def _perm_cols(w, order=_ORDER):
    parts = [w[..., _COLS[n][0]:_COLS[n][1]] for n in order]
    pad = jnp.zeros(w.shape[:-1] + (N_ALL - N_IN,), w.dtype)
    return jnp.concatenate(parts + [pad], axis=-1)


def _unperm_cols(w, order=_ORDER):
    pieces, off = {}, 0
    for n in order:
        width = _COLS[n][1] - _COLS[n][0]
        pieces[n] = w[..., off:off + width]
        off += width
    return jnp.concatenate([pieces[n] for n in sorted(_COLS, key=lambda n: _COLS[n][0])], axis=-1)


_SMALL_ROWS = 16


def _pack_small(t):
    z = jnp.zeros((D,), f32)
    misc = z.at[0:16].set(t["b_fg"].reshape(-1)).at[16:24].set(t["a_log"].reshape(-1)).at[24:32].set(t["dt_bias"].reshape(-1))
    misc = misc.at[128:384].set(t["gdn_norm_g"].reshape(-1))
    if "extra" in t:
        misc = misc.at[512].set(t["extra"])
    rows = [t["norm_g"], t["b_merge"].reshape(6, D), t["mem_norm_g"], t["final_norm_g"][None], misc[None],
            jnp.zeros((_SMALL_ROWS - 12, D), f32)]
    return jnp.concatenate(rows, axis=0)


def _unpack_small(a):
    misc = a[11]
    return dict(norm_g=a[0:2], b_merge=a[2:8].reshape(2, 3072), mem_norm_g=a[8:10], final_norm_g=a[10],
                b_fg=misc[0:16].reshape(2, 8), a_log=misc[16:24].reshape(2, 4), dt_bias=misc[24:32].reshape(2, 4),
                gdn_norm_g=misc[128:384].reshape(2, 128), extra=misc[512])


def _layer_fwd(l, x, mem, p, gather_next=None):
    sfx = f"_l{l}"
    h, ht = _norm_fwd(x, p["norm_g"], "norm_fwd" + sfx, with_t=True)
    zb = _mm(h, p["w_b"], bf16, 1024, 1024, 1024, "inproj_b" + sfx)
    if gather_next is None:
        zf, gathered = _mm(h, p["w_f"], f32, 1024, 1024, 1024, "inproj_f" + sfx), None
    else:
        zf, gathered = _mm(h, p["w_f"], f32, 1024, 1024, 1024, "inproj_f" + sfx, comm=("gather", gather_next))
    zs = _mm(h, p["w_s"], f32, 512, 128, 1024, "inproj_s" + sfx)
    sm = _small_prep(zs, p["par"], "small_prep" + sfx)
    S = x.shape[0]
    gt4 = jnp.transpose(sm[:, 8:12]).reshape(4, S // CH, 1, CH)
    qa, ka, st = _fox_prep(zb, sm, "fox_prep" + sfx)
    tab = _fox_bound_table(st, S, min(S, FOX_TILE))
    oa, lse_t = _fox_fwd(qa, ka, zb, tab, "fox_fwd" + sfx)
    gqkv = _gdn_prep(zf, p["conv_w"], "gdn_prep" + sfx)
    ob, tinv, states = _gdn_fwd(gqkv, sm, gt4, "gdn_fwd" + sfx)
    memn = _norm_fwd(mem, p["mem_norm_g"], "mem_norm" + sfx)
    kv = _mm(memn, p["w_mem_kv"], bf16, 256, 1024, 1024, "mem_kv" + sfx)
    xo, ycat, merged = _merge_fwd(x, oa, ob, zb, zf, kv, p["b_merge"], p["gdn_norm_g"], p["w_branch"], p["w_out"], "merge_fwd" + sfx)
    saved = dict(x=x, ht=ht, zb=zb, zf=zf, zs=zs, sm=sm, qa=qa, ka=ka, tab=tab, gt4=gt4, oa=oa, lse_t=lse_t, gqkv=gqkv, ob=ob, tinv=tinv,
                 states=states, memn=memn, kv=kv, ycat=ycat, merged=merged)
    return xo, saved, gathered


def _layer_bwd(l, dout, mem, p, s, comm_sends=(), send_fn=None):
    sfx = f"_l{l}"
    dproj, dzf2, dmq, delta, doab, dob, dkv, dbm, dgg = _merge_bwd(
        dout, s["ycat"], s["oa"], s["ob"], s["zb"], s["zf"], s["kv"], p["b_merge"], p["gdn_norm_g"],
        p["w_branch"], p["w_branch_t"], p["w_out_t"], "merge_bwd" + sfx)
    g = {}
    g["w_out"] = _mm(s["merged"], dout, f32, 512, 1024, 512, "dw_out" + sfx, trans_a=True)
    g["w_branch"] = jnp.stack([
        _mm(s["ycat"], dproj, f32, 512, 1024, 512, f"dw_branch{n}" + sfx, trans_a=True, a_cols=(n * 512, 512), b_cols=(n * D, D))
        for n in range(3)])
    g["b_merge"] = dbm[0]
    g["gdn_norm_g"] = dgg[0]
    g["w_mem_kv"] = _mm(s["memn"], dkv, f32, 512, 1024, 256, "dw_mem_kv" + sfx, trans_a=True)
    dmemn = _mm(dkv, p["w_mem_kv_t"], f32, 256, 1024, 1024, "dmem_n" + sfx)
    g["mem_norm_g"] = _norm_bwd(mem, p["mem_norm_g"], dmemn, None, "mem_norm_bwd" + sfx)[0]
    dgqkv, dsm = _gdn_bwd(s["gqkv"], s["sm"], s["gt4"], s["tinv"], s["states"], dob, "gdn_bwd" + sfx)
    dbqkv, dcw = _gdn_prep_bwd(s["zf"], p["conv_w"], dgqkv, "gdn_prep_bwd" + sfx)
    g["conv_w"] = dcw[0:4]
    (dq, dk, dv, dfc, dfr), received = _fox_bwd(s["qa"], s["ka"], s["zb"], doab, _head_rows(s["lse_t"]), delta, s["tab"],
                                                "fox_bwd" + sfx, comm_sends=comm_sends)
    dzs, sacc = _small_bwd(s["zs"], p["par"], _head_rows(dfr), dfc, dsm, "small_bwd" + sfx)
    g["b_fg"], g["a_log"], g["dt_bias"] = sacc[0, 0:8], sacc[1, 8:12], sacc[2, 8:12]
    dz = [(dzf2, DZ_MERGE_COLS, PIECE_COLS), (dbqkv, PIECE_COLS, PIECE_COLS), ([dq, dk, dv], PIECE_COLS, PIECE_COLS),
          ([dmq, dzs], PIECE_COLS, PIECE_COLS)]
    g["w_in"] = _mm_n_pieces(s["ht"], dz, f32, 1024, 1024, "dw_in" + sfx)
    if send_fn is None:
        dh, received_late = _mm_k_pieces(dz, p["w_all_t"], f32, 1024, "dh" + sfx), None
    else:
        dh, received_late = _mm_k_pieces(dz, p["w_all_t"], f32, 1024, "dh" + sfx, comm=("exchange", send_fn(g)))
    dx, dng = _norm_bwd(s["x"], p["norm_g"], dh, dout, "norm_bwd" + sfx)
    g["norm_g"] = dng[0]
    return dx, g, received, received_late


def kernel(x, mem, norm_g, w_in, b_fg, b_merge, conv_w, a_log, dt_bias, gdn_norm_g, mem_norm_g, w_mem_kv, w_branch, w_out, final_norm_g, loss_target, m_norm_g, m_w_in, m_b_fg, m_b_merge, m_conv_w, m_a_log, m_dt_bias, m_gdn_norm_g, m_mem_norm_g, m_w_mem_kv, m_w_branch, m_w_out, m_final_norm_g, v_norm_g, v_w_in, v_b_fg, v_b_merge, v_conv_w, v_a_log, v_dt_bias, v_gdn_norm_g, v_mem_norm_g, v_w_mem_kv, v_w_branch, v_w_out, v_final_norm_g):
    x0, mem0, tgt = x[0], mem[0], loss_target[0]
    shard_w = dict(w_in=w_in, w_mem_kv=w_mem_kv, w_branch=w_branch, w_out=w_out, conv_w=conv_w)
    shard_m = dict(w_in=m_w_in, w_mem_kv=m_w_mem_kv, w_branch=m_w_branch, w_out=m_w_out, conv_w=m_conv_w)
    shard_v = dict(w_in=v_w_in, w_mem_kv=v_w_mem_kv, w_branch=v_w_branch, w_out=v_w_out, conv_w=v_conv_w)
    small_w = dict(norm_g=norm_g, b_fg=b_fg, b_merge=b_merge, a_log=a_log, dt_bias=dt_bias, gdn_norm_g=gdn_norm_g,
                   mem_norm_g=mem_norm_g, final_norm_g=final_norm_g)
    small_m = dict(norm_g=m_norm_g, b_fg=m_b_fg, b_merge=m_b_merge, a_log=m_a_log, dt_bias=m_dt_bias, gdn_norm_g=m_gdn_norm_g,
                   mem_norm_g=m_mem_norm_g, final_norm_g=m_final_norm_g)
    small_v = dict(norm_g=v_norm_g, b_fg=v_b_fg, b_merge=v_b_merge, a_log=v_a_log, dt_bias=v_dt_bias, gdn_norm_g=v_gdn_norm_g,
                   mem_norm_g=v_mem_norm_g, final_norm_g=v_final_norm_g)

    def shards(l):
        return [_b(w_in[l]), _b(w_mem_kv[l]), _b(_as2d(w_branch[l])), _b(w_out[l])]

    def layer_params(l, g_in, g_kv, g_br, g_out, conv_full):
        w_full = jnp.transpose(g_in, (1, 0, 2)).reshape(D, N_IN)
        w_all = _perm_cols(w_full)
        w_kv = g_kv.reshape(D, D)
        w_br = jnp.transpose(g_br.reshape(N_DEV, 3, 512, 128), (1, 2, 0, 3)).reshape(3, 512, D)
        w_o = g_out.reshape(D, D)
        return dict(
            norm_g=norm_g[l][None], mem_norm_g=mem_norm_g[l][None], gdn_norm_g=gdn_norm_g[l][None], b_merge=b_merge[l][None],
            par=_small_pars(b_fg[l], a_log[l], dt_bias[l]),
            conv_w=jnp.pad(conv_full[l], ((0, 4), (0, 0))),
            w_b=w_all[:, 0:NB], w_f=w_all[:, NB:NB + NF], w_s=w_all[:, NB + NF:NB + NF + NS],
            w_all_t=jnp.transpose(_perm_cols(w_full, _ORDER_BWD)),
            w_mem_kv=w_kv, w_mem_kv_t=jnp.transpose(w_kv),
            w_branch=w_br, w_branch_t=jnp.transpose(w_br, (0, 2, 1)),
            w_out=w_o, w_out_t=jnp.transpose(w_o))

    *gathered0, conv_all = _all_gather(shards(0) + [_as2d(conv_w)], "gather_weights")
    conv_full = jnp.transpose(conv_all.reshape(N_DEV, DEPTH, 4, 192), (1, 2, 0, 3)).reshape(DEPTH, 4, 1536)
    layers = [layer_params(0, *gathered0, conv_full), None]

    saved = [None] * DEPTH
    acts, saved[0], gathered1 = _layer_fwd(0, x0, mem0, layers[0], gather_next=shards(1))
    layers[1] = layer_params(1, *gathered1, conv_full)
    acts, saved[1], _ = _layer_fwd(1, acts, mem0, layers[1])
    dx, dfg, lsum = _loss_head(acts, final_norm_g[None], tgt, "loss_head")

    def send_buffers(g):
        dw_in = _unperm_cols(g["w_in"], _ORDER_BWD)
        send = dict(
            w_in=jnp.transpose(dw_in.reshape(D, N_DEV, 1026), (1, 0, 2)),
            w_mem_kv=g["w_mem_kv"].reshape(N_DEV, 128, D),
            w_branch=jnp.transpose(g["w_branch"].reshape(3, 512, N_DEV, 128), (2, 0, 1, 3)).reshape(N_DEV, 3 * 512, 128),
            w_out=g["w_out"].reshape(N_DEV, 128, D),
            conv_w=jnp.transpose(g["conv_w"].reshape(4, N_DEV, 192), (1, 0, 2)))
        return [_b(send[n]) for n in _SHARDED]

    grads, parts = [None] * DEPTH, [None] * DEPTH
    dx, grads[1], _, _ = _layer_bwd(1, dx, mem0, layers[1], saved[1])
    dx, grads[0], parts[1], parts[0] = _layer_bwd(0, dx, mem0, layers[0], saved[0], comm_sends=send_buffers(grads[1]),
                                                  send_fn=send_buffers)
    grad_x = dx[None]

    big = [{}, {}, {}, {}]
    for a, n in enumerate(_SHARDED):
        res = [_adamw(parts[l][a], _as2d(shard_w[n][l]), _as2d(shard_m[n][l]), _as2d(shard_v[n][l]), f"adamw_{n}_l{l}")
               for l in range(DEPTH)]
        for kind in range(4):
            big[kind][n] = jnp.stack([res[l][kind] for l in range(DEPTH)]).reshape(shard_w[n].shape)

    small_g = {k: jnp.stack([grads[l][k] for l in range(DEPTH)]) for k in ("norm_g", "b_fg", "b_merge", "a_log", "dt_bias", "gdn_norm_g", "mem_norm_g")}
    small_g["final_norm_g"] = dfg[0]
    small_g["extra"] = lsum[0, 0]
    parts_s, = _all_gather([_pack_small(small_g)], "gather_small")
    g_sm, d_sm, m_sm, v_sm = _adamw(parts_s, _pack_small(small_w), _pack_small(small_m), _pack_small(small_v), "adamw_replicated")

    sml = [_unpack_small(a) for a in (g_sm, d_sm, m_sm, v_sm)]
    loss = sml[0]["extra"]
    names = ("norm_g", "w_in", "b_fg", "b_merge", "conv_w", "a_log", "dt_bias", "gdn_norm_g", "mem_norm_g", "w_mem_kv", "w_branch", "w_out", "final_norm_g")
    outs = [loss, grad_x]
    for kind in range(4):
        for n in names:
            outs.append(big[kind][n] if n in big[kind] else sml[kind][n])
    return tuple(outs)
```

```python
import functools

import jax
import jax.numpy as jnp
from jax import lax
from jax.experimental import pallas as pl
from jax.experimental.pallas import tpu as pltpu

f32, bf16 = jnp.float32, jnp.bfloat16

D = 1024
EPS = 1e-6
CH = 64
N_DEV = 8
DEPTH = 2
FOX_SCALE = 64 ** -0.5
GDN_SCALE = 128 ** -0.5
MEM_SCALE = 128 ** -0.5
NEG = -1e30
VMEM_LIMIT = 56 * 1024 * 1024

ADAM_LR, ADAM_B1, ADAM_B2, ADAM_EPS, ADAM_WD, ADAM_STEP = 0.001, 0.9, 0.999, 1e-08, 0.01, 10

_COLS = dict(aq=(0, 512), ak=(512, 1024), av=(1024, 1536), af=(1536, 1544), az=(1544, 2056),
             bq=(2056, 2568), bk=(2568, 3080), bv=(3080, 3592), ba=(3592, 3596), bb=(3596, 3600),
             bz=(3600, 4112), mq=(4112, 4624), mz=(4624, 5136), gates=(5136, 8208))
_ORDER = ("aq", "ak", "av", "mq", "bq", "bk", "bv", "az", "bz", "mz", "gates", "af", "ba", "bb")
_ORDER_BWD = ("az", "bz", "mz", "gates", "bq", "bk", "bv", "aq", "ak", "av", "mq", "af", "ba", "bb")
DZ_MERGE_COLS = 4608
N_IN = 8208
NB, NF, NS = 2048, 6144, 128
PIECE_COLS = 1536
SMALL_COLS = 1024
N_ALL = NB + NF + SMALL_COLS

_SHARDED = ("w_in", "w_mem_kv", "w_branch", "w_out", "conv_w")


def _cp(sem=None, vmem=None):
    kw = {}
    if sem is not None:
        kw["dimension_semantics"] = sem
    if vmem is not None:
        kw["vmem_limit_bytes"] = vmem
    return pltpu.CompilerParams(**kw)


def _dot(a, b):
    return jnp.dot(a, b, preferred_element_type=f32)


def _dot_nt(a, b):
    return lax.dot_general(a, b, (((1,), (1,)), ((), ())), preferred_element_type=f32)


def _dot_tn(a, b):
    return lax.dot_general(a, b, (((0,), (0,)), ((), ())), preferred_element_type=f32)


def _split2(x):
    hi = x.astype(bf16)
    return hi, (x - hi.astype(f32)).astype(bf16)


def _mm3(a, b, dims):
    ah, al = _split2(a)
    bh, bl = _split2(b)
    dg = functools.partial(lax.dot_general, dimension_numbers=dims, preferred_element_type=f32)
    return dg(ah, bh) + (dg(ah, bl) + dg(al, bh))


def _hi(a, b):
    return _mm3(a, b, (((1,), (0,)), ((), ())))


def _hi_nt(a, b):
    return _mm3(a, b, (((1,), (1,)), ((), ())))


def _hi_tn(a, b):
    return _mm3(a, b, (((0,), (0,)), ((), ())))


def _hi_b(a, b):
    return _mm3(a, b, (((2,), (1,)), ((0,), (0,))))


def _b(x):
    return x.astype(bf16)


def _sig(x):
    return jax.nn.sigmoid(x)


def _silu(x):
    return x * _sig(x)


def _dsilu(x):
    s = _sig(x)
    return s * (1.0 + x * (1.0 - s))


def _softplus(x):
    return jnp.maximum(x, 0.0) + jnp.log1p(jnp.exp(-jnp.abs(x)))


def _rowsum(x):
    return jnp.sum(x, axis=1, keepdims=True)


def _colsum(x):
    return jnp.sum(x, axis=0, keepdims=True)


def _norm_fwd(x, g, name, with_t=False):
    M = x.shape[0]
    ts = min(M, 512)

    def body(x_ref, g_ref, h_ref, *t_ref):
        xv = x_ref[...]
        r = lax.rsqrt(jnp.mean(xv * xv, axis=-1, keepdims=True) + EPS)
        h = xv * r * g_ref[...]
        h_ref[...] = _b(h)
        if with_t:
            t_ref[0][...] = _b(jnp.transpose(h))

    tile = pl.BlockSpec((ts, D), lambda i: (i, 0))
    shapes, specs = jax.ShapeDtypeStruct((M, D), bf16), tile
    if with_t:
        shapes, specs = (shapes, jax.ShapeDtypeStruct((D, M), bf16)), (tile, pl.BlockSpec((D, ts), lambda i: (0, i)))
    return pl.pallas_call(
        body, grid=(M // ts,), out_shape=shapes,
        in_specs=[tile, pl.BlockSpec((1, D), lambda i: (0, 0))],
        out_specs=specs, compiler_params=_cp(("parallel",)), name=name)(x, g)


def _norm_bwd(x, g, dh, dres, name):
    M = x.shape[0]
    ts = min(M, 512)
    with_dx = dres is not None

    def body(*refs):
        if with_dx:
            x_ref, g_ref, dh_ref, dres_ref, dx_ref, dg_ref = refs
        else:
            x_ref, g_ref, dh_ref, dg_ref = refs
        i = pl.program_id(0)
        xv = x_ref[...]
        r = lax.rsqrt(jnp.mean(xv * xv, axis=-1, keepdims=True) + EPS)
        xh = xv * r
        dh = dh_ref[...].astype(f32)
        part = jnp.broadcast_to(_colsum(dh * xh), (8, D))

        @pl.when(i == 0)
        def _():
            dg_ref[...] = part

        @pl.when(i > 0)
        def _():
            dg_ref[...] += part

        if with_dx:
            dxh = dh * g_ref[...]
            dx_ref[...] = dres_ref[...] + r * (dxh - xh * jnp.mean(dxh * xh, axis=-1, keepdims=True))

    tile = pl.BlockSpec((ts, D), lambda i: (i, 0))
    gspec = pl.BlockSpec((1, D), lambda i: (0, 0))
    acc = pl.BlockSpec((8, D), lambda i: (0, 0))
    if with_dx:
        return pl.pallas_call(
            body, grid=(M // ts,), out_shape=(jax.ShapeDtypeStruct((M, D), f32), jax.ShapeDtypeStruct((8, D), f32)),
            in_specs=[tile, gspec, tile, tile], out_specs=(tile, acc), compiler_params=_cp(("arbitrary",)), name=name)(x, g, dh, dres)
    return pl.pallas_call(
        body, grid=(M // ts,), out_shape=jax.ShapeDtypeStruct((8, D), f32),
        in_specs=[tile, gspec, tile], out_specs=acc, compiler_params=_cp(("arbitrary",)), name=name)(x, g, dh)


def _loss_head(x, g, tgt, name):
    M = x.shape[0]
    ts = min(M, 512)

    def body(x_ref, g_ref, t_ref, dx_ref, dg_ref, ls_ref):
        i = pl.program_id(0)
        xv = x_ref[...]
        gv = g_ref[...]
        r = lax.rsqrt(jnp.mean(xv * xv, axis=-1, keepdims=True) + EPS)
        xh = xv * r
        e = xh * gv - t_ref[...]
        lpart = 0.5 * jnp.sum(jnp.mean(e * e, axis=-1, keepdims=True), axis=0, keepdims=True)
        dy = e * (1.0 / D)
        dgp = jnp.broadcast_to(_colsum(dy * xh), (8, D))
        lp = jnp.broadcast_to(lpart, (8, 128))

        @pl.when(i == 0)
        def _():
            dg_ref[...] = dgp
            ls_ref[...] = lp

        @pl.when(i > 0)
        def _():
            dg_ref[...] += dgp
            ls_ref[...] += lp

        dxh = dy * gv
        dx_ref[...] = r * (dxh - xh * jnp.mean(dxh * xh, axis=-1, keepdims=True))

    tile = pl.BlockSpec((ts, D), lambda i: (i, 0))
    return pl.pallas_call(
        body, grid=(M // ts,),
        out_shape=(jax.ShapeDtypeStruct((M, D), f32), jax.ShapeDtypeStruct((8, D), f32), jax.ShapeDtypeStruct((8, 128), f32)),
        in_specs=[tile, pl.BlockSpec((1, D), lambda i: (0, 0)), tile],
        out_specs=(tile, pl.BlockSpec((8, D), lambda i: (0, 0)), pl.BlockSpec((8, 128), lambda i: (0, 0))),
        compiler_params=_cp(("arbitrary",)), name=name)(x, g, tgt)


def _mm(a, b, out_dtype, tm, tn, tk, name, trans_a=False, a_cols=None, b_cols=None, comm=None):
    if trans_a:
        K, M = a.shape
    else:
        M, K = a.shape
    N = b.shape[1]
    a0, b0 = 0, 0
    if a_cols is not None:
        a0, M = a_cols
    if b_cols is not None:
        b0, N = b_cols
    tm, tn, tk = min(tm, M), min(tn, N), min(tk, K)
    nk = K // tk
    a0, b0 = a0 // tm, b0 // tn
    grid = (M // tm, N // tn, nk)
    kind, carr = comm if comm is not None else (None, ())
    nc = len(carr)

    def body(*refs):
        a_ref, b_ref = refs[:2]
        o_ref = refs[2 + nc]
        acc_ref = refs[3 + 2 * nc]
        k = pl.program_id(2)
        if nc:
            cs_refs, cr_refs, sems = refs[2:2 + nc], refs[3 + nc:3 + 2 * nc], refs[4 + 2 * nc:]
            step = (pl.program_id(0) * grid[1] + pl.program_id(1)) * nk + k

            @pl.when(step == 0)
            def _():
                if kind == "gather":
                    _Gather(cs_refs, cr_refs, *sems).start()
                else:
                    for cp in _exchange_copies(cs_refs, cr_refs, *sems):
                        cp.start()
        av, bv = _b(a_ref[...]), _b(b_ref[...])
        part = _dot_tn(av, bv) if trans_a else _dot(av, bv)
        if nk == 1:
            o_ref[...] = part.astype(out_dtype)
        else:
            @pl.when(k == 0)
            def _():
                acc_ref[...] = part

            @pl.when(k > 0)
            def _():
                acc_ref[...] += part

            @pl.when(k == nk - 1)
            def _():
                o_ref[...] = acc_ref[...].astype(out_dtype)
        if nc:
            @pl.when(step == grid[0] * grid[1] * nk - 1)
            def _():
                if kind == "gather":
                    _Gather(cs_refs, cr_refs, *sems).finish()
                else:
                    for cp in _exchange_copies(cs_refs, cr_refs, *sems):
                        cp.wait()

    a_spec = pl.BlockSpec((tk, tm), lambda i, j, k: (k, i + a0)) if trans_a else pl.BlockSpec((tm, tk), lambda i, j, k: (i, k))
    anyspec = pl.BlockSpec(memory_space=pl.ANY)
    recv_shapes = tuple(jax.ShapeDtypeStruct(((N_DEV,) + c.shape) if kind == "gather" else c.shape, c.dtype) for c in carr)
    out = pl.pallas_call(
        body, grid=grid, out_shape=(jax.ShapeDtypeStruct((M, N), out_dtype),) + recv_shapes,
        in_specs=[a_spec, pl.BlockSpec((tk, tn), lambda i, j, k: (k, j + b0))] + [anyspec] * nc,
        out_specs=(pl.BlockSpec((tm, tn), lambda i, j, k: (i, j)),) + tuple([anyspec] * nc),
        scratch_shapes=[pltpu.VMEM((tm, tn), f32)] + (_exchange_scratch(nc) if nc else []),
        compiler_params=_cp(("arbitrary",) * 3 if nc else ("parallel", "parallel", "arbitrary"), VMEM_LIMIT), name=name)(a, b, *carr)
    return (out[0], list(out[1:])) if nc else out[0]


def _piece_spans(pieces):
    spans, start, col = [], 0, 0
    for a, w, bw in pieces:
        assert w % bw == 0 and col % bw == 0
        assert not isinstance(a, (list, tuple)) or (w == bw and sum(x.shape[1] for x in a) == w)
        spans.append((start, w // bw, bw, col))
        start += w // bw
        col += w
    return spans, start


def _piece_arrays(pieces):
    arrs = [list(a) if isinstance(a, (list, tuple)) else [a] for a, _, _ in pieces]
    offs = [sum(len(x) for x in arrs[:p]) for p in range(len(arrs))]
    return arrs, offs


def _piece_value(p_refs, arrs, offs, p):
    vals = [_b(p_refs[offs[p] + j][...]) for j in range(len(arrs[p]))]
    return vals[0] if len(vals) == 1 else jnp.concatenate(vals, axis=1)


def _mm_k_pieces(pieces, w, out_dtype, tm, name, comm=None):
    spans, nk = _piece_spans(pieces)
    arrs, offs = _piece_arrays(pieces)
    M, N = arrs[0][0].shape[0], w.shape[1]
    tm = min(tm, M)
    npc = sum(len(x) for x in arrs)
    bws = sorted({bw for _, _, bw, _ in spans}, reverse=True)
    nw = len(bws)
    kind, carr = comm if comm is not None else (None, ())
    nc = len(carr)

    def body(*refs):
        p_refs, w_refs = refs[:npc], refs[npc:npc + nw]
        refs = refs[nw - 1:]
        o_ref = refs[npc + 1 + nc]
        acc_ref = refs[npc + 2 + 2 * nc]
        k = pl.program_id(1)
        if nc:
            cs_refs, cr_refs, sems = refs[npc + 1:npc + 1 + nc], refs[npc + 2 + nc:npc + 2 + 2 * nc], refs[npc + 3 + 2 * nc:]
            step = pl.program_id(0) * nk + k

            @pl.when(step == 0)
            def _():
                for cp in _exchange_copies(cs_refs, cr_refs, *sems):
                    cp.start()

        @pl.when(k == 0)
        def _():
            acc_ref[...] = jnp.zeros_like(acc_ref)

        for p, (start, n, bw, _) in enumerate(spans):
            @pl.when((k >= start) & (k < start + n))
            def _():
                acc_ref[...] += _dot(_piece_value(p_refs, arrs, offs, p), w_refs[bws.index(bw)][...])

        @pl.when(k == nk - 1)
        def _():
            o_ref[...] = acc_ref[...].astype(out_dtype)

        if nc:
            @pl.when(step == (M // tm) * nk - 1)
            def _():
                for cp in _exchange_copies(cs_refs, cr_refs, *sems):
                    cp.wait()

    assert kind in (None, "exchange")
    anyspec = pl.BlockSpec(memory_space=pl.ANY)
    p_specs = []
    for p, (s, n, bw, _) in enumerate(spans):
        if len(arrs[p]) == 1:
            p_specs.append(pl.BlockSpec((tm, bw), functools.partial(lambda i, k, s, n: (i, jnp.clip(k - s, 0, n - 1)), s=s, n=n)))
        else:
            p_specs += [pl.BlockSpec((tm, x.shape[1]), lambda i, k: (i, 0)) for x in arrs[p]]
    w_specs = []
    for bw in bws:
        mine = [sp for sp in spans if sp[2] == bw]
        first, steps, row0 = mine[0][0], sum(sp[1] for sp in mine), mine[0][3] // bw
        assert mine[-1][0] + mine[-1][1] - first == steps
        w_specs.append(pl.BlockSpec((bw, N), functools.partial(lambda i, k, f, s, r: (r + jnp.clip(k - f, 0, s - 1), 0), f=first, s=steps, r=row0)))
    out = pl.pallas_call(
        body, grid=(M // tm, nk),
        out_shape=(jax.ShapeDtypeStruct((M, N), out_dtype),) + tuple(jax.ShapeDtypeStruct(c.shape, c.dtype) for c in carr),
        in_specs=p_specs + w_specs + [anyspec] * nc,
        out_specs=(pl.BlockSpec((tm, N), lambda i, k: (i, 0)),) + tuple([anyspec] * nc),
        scratch_shapes=[pltpu.VMEM((tm, N), f32)] + (_exchange_scratch(nc) if nc else []),
        compiler_params=_cp(("arbitrary", "arbitrary"), VMEM_LIMIT), name=name)(*[x for a in arrs for x in a], *([w] * nw), *carr)
    return (out[0], list(out[1:])) if nc else out[0]


def _mm_n_pieces(a, pieces, out_dtype, tm, tk, name):
    M, K = a.shape
    tm, tk = min(tm, M), min(tk, K)
    spans, nn = _piece_spans(pieces)
    arrs, offs = _piece_arrays(pieces)
    cols = pieces[0][2]
    assert all(bw == cols for _, _, bw in pieces)
    npc, nk = sum(len(x) for x in arrs), K // tk

    def body(*refs):
        a_ref, p_refs, o_ref, acc_ref = refs[0], refs[1:1 + npc], refs[1 + npc], refs[2 + npc]
        j, k = pl.program_id(1), pl.program_id(2)

        @pl.when(k == 0)
        def _():
            acc_ref[...] = jnp.zeros_like(acc_ref)

        for p, (start, n, _, _) in enumerate(spans):
            @pl.when((j >= start) & (j < start + n))
            def _():
                acc_ref[...] += _dot(_b(a_ref[...]), _piece_value(p_refs, arrs, offs, p))

        @pl.when(k == nk - 1)
        def _():
            o_ref[...] = acc_ref[...].astype(out_dtype)

    def p_map(i, j, k, s, n):
        inside = (j >= s) & (j < s + n)
        return jnp.where(inside, k, 0), jnp.clip(j - s, 0, n - 1)

    p_specs = []
    for p, (s, n, _, _) in enumerate(spans):
        widths = [cols] if len(arrs[p]) == 1 else [x.shape[1] for x in arrs[p]]
        p_specs += [pl.BlockSpec((tk, wd), functools.partial(p_map, s=s, n=n)) for wd in widths]
    return pl.pallas_call(
        body, grid=(M // tm, nn, nk), out_shape=jax.ShapeDtypeStruct((M, nn * cols), out_dtype),
        in_specs=[pl.BlockSpec((tm, tk), lambda i, j, k: (i, k))] + p_specs,
        out_specs=pl.BlockSpec((tm, cols), lambda i, j, k: (i, j)),
        scratch_shapes=[pltpu.VMEM((tm, cols), f32)],
        compiler_params=_cp(("parallel", "parallel", "arbitrary"), VMEM_LIMIT), name=name)(a, *[x for b in arrs for x in b])


def _small_pars(b_fg, a_log, dt_bias):
    par = jnp.zeros((8, 128), f32)
    par = par.at[0, 0:8].set(b_fg).at[1, 8:12].set(a_log).at[2, 8:12].set(dt_bias)
    return par


def _small_prep(zs, par, name):
    S = zs.shape[0]
    ts = min(S, 512)

    def body(z_ref, par_ref, o_ref, carry_ref):
        i = pl.program_id(0)

        @pl.when(i == 0)
        def _():
            carry_ref[...] = jnp.zeros_like(carry_ref)

        z = z_ref[...]
        lane = lax.broadcasted_iota(jnp.int32, (ts, 128), 1)
        row = lax.broadcasted_iota(jnp.int32, (ts, 128), 0)
        za = z + par_ref[0:1, :]
        logf = jnp.minimum(za, 0.0) - jnp.log1p(jnp.exp(-jnp.abs(za)))
        glog = -jnp.exp(par_ref[1:2, :]) * _softplus(z + par_ref[2:3, :])
        x = jnp.where(lane < 8, logf, jnp.where(lane < 12, glog, 0.0))
        pos = jnp.where(lane < 8, row, row & (CH - 1))
        s = 1
        while s < ts:
            x = x + jnp.where(pos >= s, pltpu.roll(x, s, 0), 0.0)
            s *= 2
        tot = x + carry_ref[0:1, :]
        carry_ref[...] = jnp.broadcast_to(jnp.where(lane[0:1] < 8, tot[ts - 1:ts, :], 0.0), (8, 128))
        o_ref[...] = jnp.where(lane < 8, tot, jnp.where(lane < 12, x, jnp.where(lane < 16, _sig(z), 0.0)))

    return pl.pallas_call(
        body, grid=(S // ts,), out_shape=jax.ShapeDtypeStruct((S, 128), f32),
        in_specs=[pl.BlockSpec((ts, 128), lambda i: (i, 0)), pl.BlockSpec((8, 128), lambda i: (0, 0))],
        out_specs=pl.BlockSpec((ts, 128), lambda i: (i, 0)), scratch_shapes=[pltpu.VMEM((8, 128), f32)],
        compiler_params=_cp(("arbitrary",)), name=name)(zs, par)


def _small_bwd(zs, par, dfr, dfc, dsm, name):
    S = zs.shape[0]
    ts = min(S, 512)
    nt = S // ts

    def body(z_ref, par_ref, dfr_ref, dfc_ref, dsm_ref, dz_ref, acc_ref, carry_ref):
        i = pl.program_id(0)

        @pl.when(i == 0)
        def _():
            carry_ref[...] = jnp.zeros_like(carry_ref)

        z = z_ref[...]
        dsm_v = dsm_ref[...]
        lane = lax.broadcasted_iota(jnp.int32, (ts, 128), 1)
        row = lax.broadcasted_iota(jnp.int32, (ts, 128), 0)
        df = jnp.transpose(jnp.concatenate([dfr_ref[...], jnp.zeros((120, ts), f32)], axis=0))
        for p in range(4):
            dpair = dfc_ref[p]
            df = df - jnp.where(lane == 2 * p, dpair[:, 0:1], jnp.where(lane == 2 * p + 1, dpair[:, 64:65], 0.0))
        x = jnp.where(lane < 8, df, jnp.where(lane < 12, dsm_v, 0.0))
        pos = jnp.where(lane < 8, row, row & (CH - 1))
        seg = jnp.where(lane < 8, ts, CH)
        s = 1
        while s < ts:
            x = x + jnp.where(pos + s < seg, pltpu.roll(x, ts - s, 0), 0.0)
            s *= 2
        tot = x + carry_ref[0:1, :]
        carry_ref[...] = jnp.broadcast_to(jnp.where(lane[0:1] < 8, tot[0:1, :], 0.0), (8, 128))
        za = z + par_ref[0:1, :]
        daf = tot * _sig(-za)
        zb = z + par_ref[2:3, :]
        nea = -jnp.exp(par_ref[1:2, :])
        glog = nea * _softplus(zb)
        dba = x * nea * _sig(zb)
        beta = _sig(z)
        dbb = dsm_v * beta * (1.0 - beta)
        dz_ref[:, 0:128] = _b(jnp.where(lane < 8, daf, jnp.where(lane < 12, dba, jnp.where(lane < 16, dbb, 0.0))))
        dz_ref[:, 128:SMALL_COLS] = jnp.zeros((ts, SMALL_COLS - 128), bf16)
        r0 = _colsum(jnp.where(lane < 8, daf, 0.0))
        r1 = _colsum(jnp.where((lane >= 8) & (lane < 12), x * glog, 0.0))
        r2 = _colsum(jnp.where((lane >= 8) & (lane < 12), dba, 0.0))
        r8 = lax.broadcasted_iota(jnp.int32, (8, 128), 0)
        part = jnp.where(r8 == 0, r0, jnp.where(r8 == 1, r1, jnp.where(r8 == 2, r2, 0.0)))

        @pl.when(i == 0)
        def _():
            acc_ref[...] = part

        @pl.when(i > 0)
        def _():
            acc_ref[...] += part

    rev = pl.BlockSpec((ts, 128), lambda i: (nt - 1 - i, 0))
    rev4 = pl.BlockSpec((4, ts, 128), lambda i: (0, nt - 1 - i, 0))
    c8 = pl.BlockSpec((8, 128), lambda i: (0, 0))
    return pl.pallas_call(
        body, grid=(nt,), out_shape=(jax.ShapeDtypeStruct((S, SMALL_COLS), bf16), jax.ShapeDtypeStruct((8, 128), f32)),
        in_specs=[rev, c8, pl.BlockSpec((8, ts), lambda i: (0, nt - 1 - i)), rev4, rev],
        out_specs=(pl.BlockSpec((ts, SMALL_COLS), lambda i: (nt - 1 - i, 0)), c8),
        scratch_shapes=[pltpu.VMEM((8, 128), f32)],
        compiler_params=_cp(("arbitrary",)), name=name)(zs, par, dfr, dfc, dsm)


def _split3(x):
    hi = _b(x).astype(f32)
    r = x - hi
    mid = _b(r).astype(f32)
    return hi, mid, _b(r - mid).astype(f32)


FOX_PREP_ROWS = 512
FOX_TILE = 512
FOX_SKIP_LOG = -32.0


def _fox_prep(zb, sm, name):
    S = zb.shape[0]
    ts = min(S, FOX_PREP_ROWS)

    def body(q_ref, k_ref, f_ref, qa_ref, ka_ref, st_ref):
        lane = lax.broadcasted_iota(jnp.int32, (ts, 128), 1)
        lane8 = lax.broadcasted_iota(jnp.int32, (8, 128), 1)
        f = f_ref[...]
        st = jnp.zeros((8, 128), f32)
        for p in range(4):
            q = q_ref[:, p * 128:(p + 1) * 128].astype(f32) * FOX_SCALE
            k = k_ref[:, p * 128:(p + 1) * 128].astype(f32)
            for h in (0, 1):
                fcol = f[:, 2 * p + h:2 * p + h + 1]
                hi, mid, lo = _split3(fcol)
                own = (lane < 64) if h == 0 else (lane >= 64)
                nq = jnp.sqrt(_rowsum(jnp.where(own, q * q, 0.0)))
                nk = jnp.sqrt(_rowsum(jnp.where(own, k * k, 0.0)))
                stats = (jnp.max(nq, axis=0, keepdims=True), jnp.max(nk, axis=0, keepdims=True),
                         jnp.max(fcol, axis=0, keepdims=True), jnp.min(fcol, axis=0, keepdims=True),
                         jnp.min(-nq * nk, axis=0, keepdims=True))
                for si, val in enumerate(stats):
                    st = jnp.where(lane8 == 8 * si + 2 * p + h, val, st)
                o = 64 if h == 0 else 0
                ones_lo = (lane >= o) & (lane < o + 3)
                ones_hi = (lane >= o + 3) & (lane < o + 6)
                qaug = jnp.where(lane == o, hi, jnp.where(lane == o + 1, mid, jnp.where(lane == o + 2, lo, jnp.where(ones_hi, 1.0, 0.0))))
                kaug = jnp.where(lane == o + 3, -hi, jnp.where(lane == o + 4, -mid, jnp.where(lane == o + 5, -lo, jnp.where(ones_lo, 1.0, 0.0))))
                qa_ref[2 * p + h] = _b(jnp.where(own, q, qaug))
                ka_ref[2 * p + h] = _b(jnp.where(own, k, kaug))
        st_ref[...] = st

    out = jax.ShapeDtypeStruct((8, S, 128), bf16)
    return pl.pallas_call(
        body, grid=(S // ts,), out_shape=(out, out, jax.ShapeDtypeStruct((S // ts * 8, 128), f32)),
        in_specs=[pl.BlockSpec((ts, 512), lambda i: (i, 0)), pl.BlockSpec((ts, 512), lambda i: (i, 1)), pl.BlockSpec((ts, 128), lambda i: (i, 0))],
        out_specs=(pl.BlockSpec((8, ts, 128), lambda i: (0, i, 0)), pl.BlockSpec((8, ts, 128), lambda i: (0, i, 0)),
                   pl.BlockSpec((8, 128), lambda i: (i, 0))),
        compiler_params=_cp(("parallel",)), name=name)(zb, zb, sm)


def _fox_bound_table(st, S, T, lse_rows=None):
    ts = min(S, FOX_PREP_ROWS)
    g = T // ts
    nt = S // T
    s5 = st.reshape(S // ts, 8, 128)[:, 0, 0:40].reshape(nt, g, 5, 8)
    qn, kn, fmax = s5[:, :, 0].max(axis=1), s5[:, :, 1].max(axis=1), s5[:, :, 2].max(axis=1)
    fmin, lmin = s5[:, :, 3].min(axis=1), s5[:, :, 4].min(axis=1)
    if lse_rows is not None:
        lmin = jnp.transpose(lse_rows.reshape(8, nt, T).min(axis=2))
    e = qn[:, None] * kn[None, :] + fmax[:, None] - fmin[None, :] - lmin[:, None] + 1.0
    return jnp.transpose(e, (2, 0, 1)).reshape(8, nt * nt)


def _pair_rows(a, T):
    at = jnp.transpose(a)
    r8 = lax.broadcasted_iota(jnp.int32, (8, T), 0)
    return jnp.where(r8 == 0, at[0:1, :], at[64:65, :])


def _fox_fwd(qa, ka, zb, tab, name):
    S = zb.shape[0]
    T = min(S, FOX_TILE)
    nt = S // T

    def body(tab_ref, qa_ref, ka_ref, v_ref, o_ref, lset_ref, m_ref, l_ref, acc_ref):
        p, i = pl.program_id(0), pl.program_id(1)
        m_ref[...] = jnp.full_like(m_ref, NEG)
        l_ref[...] = jnp.zeros_like(l_ref)
        acc_ref[...] = jnp.zeros_like(acc_ref)
        row = lax.broadcasted_iota(jnp.int32, (T, T), 0)
        col = lax.broadcasted_iota(jnp.int32, (T, T), 1)

        def head_tile(h, j, masked):
            off = pl.multiple_of(j * T, T)
            s = _dot_nt(qa_ref[h], ka_ref[h, pl.ds(off, T), :])
            if masked:
                s = jnp.where(row >= col, s, NEG)
            m_old = m_ref[h]
            m_new = jnp.maximum(m_old, jnp.max(s, axis=1, keepdims=True))
            alpha = jnp.exp(m_old - m_new)
            pr = jnp.exp(s - jnp.tile(m_new, (1, T // 128)))
            l_ref[h] = alpha * l_ref[h] + _rowsum(pr)
            acc_ref[h] = alpha * acc_ref[h] + _dot(_b(pr), v_ref[pl.ds(off, T), :])
            m_ref[h] = m_new

        def step(j, c):
            for h in (0, 1):
                @pl.when(tab_ref[2 * p + h, i * nt + j] > FOX_SKIP_LOG)
                def _():
                    head_tile(h, j, False)
            return c

        lax.fori_loop(0, i, step, 0)
        for h in (0, 1):
            head_tile(h, i, True)
        lane2 = lax.broadcasted_iota(jnp.int32, (T, 128), 1)
        o_ref[...] = jnp.where(lane2 < 64, acc_ref[0] / l_ref[0], acc_ref[1] / l_ref[1])
        lse = jnp.where(lane2 < 64, m_ref[0] + jnp.log(l_ref[0]), m_ref[1] + jnp.log(l_ref[1]))
        lset_ref[0] = _pair_rows(lse, T)

    return pl.pallas_call(
        body, grid=(4, S // T),
        out_shape=(jax.ShapeDtypeStruct((S, 512), f32), jax.ShapeDtypeStruct((4, 8, S), f32)),
        in_specs=[pl.BlockSpec(memory_space=pltpu.SMEM), pl.BlockSpec((2, T, 128), lambda p, i: (p, i, 0)),
                  pl.BlockSpec((2, S, 128), lambda p, i: (p, 0, 0), pipeline_mode=pl.Buffered(1)),
                  pl.BlockSpec((S, 128), lambda p, i: (0, 8 + p), pipeline_mode=pl.Buffered(1))],
        out_specs=(pl.BlockSpec((T, 128), lambda p, i: (i, p)), pl.BlockSpec((1, 8, T), lambda p, i: (p, 0, i))),
        scratch_shapes=[pltpu.VMEM((2, T, 128), f32), pltpu.VMEM((2, T, 128), f32), pltpu.VMEM((2, T, 128), f32)],
        compiler_params=_cp(("arbitrary", "arbitrary"), VMEM_LIMIT), name=name)(tab, qa, ka, zb)


def _fox_bwd(qa, ka, zb, dob, lse_t, dl_t, tab, name, comm_sends=()):
    S = zb.shape[0]
    T = min(S, FOX_TILE)
    nq = S // T
    nc = len(comm_sends)

    def body(*refs):
        tab_ref, ka_ref, v_ref, qa_ref, do_ref, lt_ref, dt_ref = refs[:7]
        cs_refs = refs[7:7 + nc]
        dq_ref, dk_ref, dv_ref, dfc_ref, dfr_ref = refs[7 + nc:12 + nc]
        cr_refs = refs[12 + nc:12 + 2 * nc]
        dqa_ref, dka_ref, dva_ref, fs_ref = refs[12 + 2 * nc:16 + 2 * nc]
        p, j = pl.program_id(0), pl.program_id(1)
        if nc:
            @pl.when((p == 0) & (j == 0))
            def _():
                for cp in _exchange_copies(cs_refs, cr_refs, *refs[16 + 2 * nc:]):
                    cp.start()
        lane1 = lax.broadcasted_iota(jnp.int32, (1, 128), 1)
        lane2 = lax.broadcasted_iota(jnp.int32, (T, 128), 1)
        hm = (lane1 < 64, lane1 >= 64)
        v = v_ref[...]
        vsm = [jnp.where(hm[h], v, jnp.zeros_like(v)) for h in (0, 1)]
        ksm = [jnp.where(hm[h], ka_ref[h], jnp.zeros_like(v)) for h in (0, 1)]

        @pl.when(j == 0)
        def _():
            dqa_ref[...] = jnp.zeros_like(dqa_ref)
            dfr_ref[...] = jnp.zeros_like(dfr_ref)

        dka_ref[...] = jnp.zeros_like(dka_ref)
        dva_ref[...] = jnp.zeros_like(dva_ref)
        fs_ref[...] = jnp.zeros_like(fs_ref)
        row = lax.broadcasted_iota(jnp.int32, (T, T), 0)
        col = lax.broadcasted_iota(jnp.int32, (T, T), 1)

        def head_tile(h, i, masked):
            off = pl.multiple_of(i * T, T)
            dot_ = do_ref[pl.ds(off, T), :]
            hr = pl.ds(2 * p + h, 1)
            qt = qa_ref[h, pl.ds(off, T), :]
            s_t = _dot_nt(ka_ref[h], qt)
            if masked:
                s_t = jnp.where(col >= row, s_t, NEG)
            p_t = jnp.exp(s_t - lt_ref[hr, pl.ds(off, T)])
            dva_ref[h] += _dot(_b(p_t), dot_)
            dp_t = _dot_nt(vsm[h], dot_)
            ds_t = p_t * (dp_t - dt_ref[hr, pl.ds(off, T)])
            dsb = _b(ds_t)
            dka_ref[h] += _dot(dsb, qt)
            fs_ref[h] += _rowsum(ds_t)
            dfr_ref[0, pl.ds(h, 1), pl.ds(off, T)] += _colsum(ds_t)
            dqa_ref[pl.ds(off, T), :] += _dot_tn(dsb, ksm[h])

        def step(i, c):
            for h in (0, 1):
                @pl.when(tab_ref[2 * p + h, i * nq + j] > FOX_SKIP_LOG)
                def _():
                    head_tile(h, i, False)
            return c

        for h in (0, 1):
            head_tile(h, j, True)
        lax.fori_loop(j + 1, nq, step, 0)
        dk_ref[...] = _b(jnp.where(lane2 < 64, dka_ref[0], dka_ref[1]))
        dv_ref[...] = _b(jnp.where(lane2 < 64, dva_ref[0], dva_ref[1]))
        dfc_ref[0] = jnp.where(lane2 < 64, fs_ref[0], fs_ref[1])
        dq_ref[...] = _b(dqa_ref[pl.ds(pl.multiple_of(j * T, T), T), :] * FOX_SCALE)
        if nc:
            @pl.when((p == 3) & (j == nq - 1))
            def _():
                for cp in _exchange_copies(cs_refs, cr_refs, *refs[16 + 2 * nc:]):
                    cp.wait()

    one = pl.Buffered(1)
    res = pl.BlockSpec((8, S), lambda p, j: (0, 0), pipeline_mode=one)
    tk = pl.BlockSpec((T, 128), lambda p, j: (j, p))
    anyspec = pl.BlockSpec(memory_space=pl.ANY)
    outs = pl.pallas_call(
        body, grid=(4, nq),
        out_shape=(jax.ShapeDtypeStruct((S, 512), bf16), jax.ShapeDtypeStruct((S, 512), bf16), jax.ShapeDtypeStruct((S, 512), bf16),
                   jax.ShapeDtypeStruct((4, S, 128), f32), jax.ShapeDtypeStruct((4, 8, S), f32))
        + tuple(jax.ShapeDtypeStruct(c.shape, c.dtype) for c in comm_sends),
        in_specs=[pl.BlockSpec(memory_space=pltpu.SMEM),
                  pl.BlockSpec((2, T, 128), lambda p, j: (p, j, 0)), pl.BlockSpec((T, 128), lambda p, j: (j, 8 + p)),
                  pl.BlockSpec((2, S, 128), lambda p, j: (p, 0, 0), pipeline_mode=one),
                  pl.BlockSpec((S, 128), lambda p, j: (0, p), pipeline_mode=one), res, res] + [anyspec] * nc,
        out_specs=(tk, tk, tk, pl.BlockSpec((1, T, 128), lambda p, j: (p, j, 0)),
                   pl.BlockSpec((1, 8, S), lambda p, j: (p, 0, 0))) + tuple([anyspec] * nc),
        scratch_shapes=[pltpu.VMEM((S, 128), f32), pltpu.VMEM((2, T, 128), f32), pltpu.VMEM((2, T, 128), f32), pltpu.VMEM((2, T, 1), f32)]
        + (_exchange_scratch(nc) if nc else []),
        compiler_params=_cp(("arbitrary", "arbitrary"), VMEM_LIMIT), name=name)(tab, ka, zb, qa, dob, lse_t, dl_t, *comm_sends)
    return outs[:5], list(outs[5:])


def _head_rows(a):
    return a[:, 0:2, :].reshape(8, a.shape[2])


def _conv_taps(ext, x, w_ref, ts):
    y = x * w_ref[3:4, :]
    shifted = []
    for k in (1, 2, 3):
        xs = pltpu.roll(ext, k, 0)[8:]
        shifted.append(xs)
        y = y + xs * w_ref[3 - k:4 - k, :]
    return y, shifted


def _gdn_prep(zf, cw, name):
    S = zf.shape[0]
    ts = min(S, 512)

    def body(x_ref, w_ref, o_ref, tail_ref):
        i = pl.program_id(0)

        @pl.when(i == 0)
        def _():
            tail_ref[...] = jnp.zeros_like(tail_ref)

        x = x_ref[...]
        ext = jnp.concatenate([tail_ref[...], x], axis=0)
        y, _ = _conv_taps(ext, x, w_ref, ts)
        tail_ref[...] = x[ts - 8:, :]
        a = _silu(y)
        for hb in range(12):
            blk = a[:, hb * 128:(hb + 1) * 128]
            if hb < 8:
                blk = blk * lax.rsqrt(_rowsum(blk * blk) + EPS)
            if hb < 4:
                blk = blk * GDN_SCALE
            o_ref[:, hb * 128:(hb + 1) * 128] = blk

    return pl.pallas_call(
        body, grid=(S // ts,), out_shape=jax.ShapeDtypeStruct((S, 1536), f32),
        in_specs=[pl.BlockSpec((ts, 1536), lambda i: (i, 0)), pl.BlockSpec((8, 1536), lambda i: (0, 0))],
        out_specs=pl.BlockSpec((ts, 1536), lambda i: (i, 0)), scratch_shapes=[pltpu.VMEM((8, 1536), f32)],
        compiler_params=_cp(("arbitrary",), VMEM_LIMIT), name=name)(zf, cw)


def _gdn_prep_bwd(zf, cw, dg, name):
    S = zf.shape[0]
    ts = min(S, 512)
    nt = S // ts

    def body(x_ref, xp_ref, w_ref, dg_ref, dx_ref, dw_ref, head_ref):
        i = pl.program_id(0)

        @pl.when(i == 0)
        def _():
            head_ref[...] = jnp.zeros_like(head_ref)

        x = x_ref[...]
        prev = jnp.where(i == nt - 1, 0.0, xp_ref[...])
        ext = jnp.concatenate([prev, x], axis=0)
        y, shifted = _conv_taps(ext, x, w_ref, ts)
        a = _silu(y)
        das = []
        for hb in range(12):
            blk = a[:, hb * 128:(hb + 1) * 128]
            d = dg_ref[:, hb * 128:(hb + 1) * 128]
            if hb < 4:
                d = d * GDN_SCALE
            if hb < 8:
                r = lax.rsqrt(_rowsum(blk * blk) + EPS)
                n = blk * r
                d = r * (d - n * _rowsum(d * n))
            das.append(d)
        dy = jnp.concatenate(das, axis=1) * _dsilu(y)
        extd = jnp.concatenate([dy, head_ref[...]], axis=0)
        dx = dy * w_ref[3:4, :]
        for k in (1, 2, 3):
            dx = dx + pltpu.roll(extd, ts + 8 - k, 0)[:ts] * w_ref[3 - k:4 - k, :]
        head_ref[...] = dy[0:8, :]
        dx_ref[...] = _b(dx)
        r8 = lax.broadcasted_iota(jnp.int32, (8, 1536), 0)
        part = jnp.where(r8 == 3, _colsum(dy * x), 0.0)
        for k in (1, 2, 3):
            part = jnp.where(r8 == 3 - k, _colsum(dy * shifted[k - 1]), part)

        @pl.when(i == 0)
        def _():
            dw_ref[...] = part

        @pl.when(i > 0)
        def _():
            dw_ref[...] += part

    rev = pl.BlockSpec((ts, 1536), lambda i: (nt - 1 - i, 0))
    prev8 = pl.BlockSpec((8, 1536), lambda i: (jnp.maximum((nt - 1 - i) * (ts // 8) - 1, 0), 0))
    w8 = pl.BlockSpec((8, 1536), lambda i: (0, 0))
    return pl.pallas_call(
        body, grid=(nt,), out_shape=(jax.ShapeDtypeStruct((S, 1536), bf16), jax.ShapeDtypeStruct((8, 1536), f32)),
        in_specs=[rev, prev8, w8, rev], out_specs=(rev, w8), scratch_shapes=[pltpu.VMEM((8, 1536), f32)],
        compiler_params=_cp(("arbitrary",), VMEM_LIMIT), name=name)(zf, zf, cw, dg)


def _tri_inv(a, row, col):
    same = (row >> 4) == (col >> 4)
    dm = jnp.where(same, a, 0.0)
    lo = a - dm
    eye = jnp.where(row == col, 1.0, 0.0)
    d2 = _hi_b(dm, dm)
    d4 = _hi_b(d2, d2)
    d8 = _hi_b(d4, d4)
    x0 = _hi_b(_hi_b(eye - dm, eye + d2), _hi_b(eye + d4, eye + d8))
    n = _hi_b(x0, lo)
    n2 = _hi_b(n, n)
    return _hi_b(_hi_b(eye - n, eye + n2), x0)


def _bd(a, b):
    return lax.dot_general(a, b, (((2,), (1,)), ((0,), (0,))), preferred_element_type=f32)


def _bd_nt(a, b):
    return lax.dot_general(a, b, (((2,), (2,)), ((0,), (0,))), preferred_element_type=f32)


def _bd_tn(a, b):
    return lax.dot_general(a, b, (((1,), (1,)), ((0,), (0,))), preferred_element_type=f32)


def _hi_b_nt(a, b):
    return _mm3(a, b, (((2,), (2,)), ((0,), (0,))))


def _hi_b_tn(a, b):
    return _mm3(a, b, (((1,), (1,)), ((0,), (0,))))


def _gdn_local(x_ref, sm_ref, gt_ref, row, col, cps=1):
    idx = [(c, h) for c in range(cps) for h in range(4)]

    def rows(c):
        return slice(c * CH, (c + 1) * CH)

    q = jnp.stack([x_ref[rows(c), h * 128:(h + 1) * 128] for c, h in idx])
    k = jnp.stack([x_ref[rows(c), 512 + h * 128:512 + (h + 1) * 128] for c, h in idx])
    v = jnp.stack([x_ref[rows(c), 1024 + h * 128:1024 + (h + 1) * 128] for c, h in idx])
    gc = jnp.stack([sm_ref[rows(c), 8 + h:9 + h] for c, h in idx])
    beta = jnp.stack([sm_ref[rows(c), 12 + h:13 + h] for c, h in idx])
    gr = jnp.stack([gt_ref[h, c] for c, h in idx])
    eg = jnp.exp(gc)
    gl = gc[:, CH - 1:CH, :]
    dec = jnp.exp(gl - gc)
    gm = gc - gr
    gam_i = jnp.exp(jnp.where(row >= col, gm, -jnp.inf))
    gam_s = jnp.where(row > col, gam_i, 0.0)
    kb = k * beta
    return dict(q=q, k=k, v=v, beta=beta, eg=eg, egl=jnp.exp(gl), dec=dec, gam_i=gam_i, gam_s=gam_s,
                kb=kb, vb=v * beta, kbg=kb * eg, qdec=q * eg, kdec=k * dec,
                a=_bd_nt(_b(kb), _b(k)) * gam_s, aqk=_bd_nt(_b(q), _b(k)) * gam_i)


GDN_FWD_CHUNKS = 8


def _gdn_fwd(gqkv, sm, gt4, name, cps=GDN_FWD_CHUNKS):
    S = gqkv.shape[0]
    N = S // CH
    cps = min(cps, N)
    R = cps * CH

    def body(x_ref, sm_ref, gt_ref, o_ref, t_ref, st_ref, s_ref):
        n = pl.program_id(0)

        @pl.when(n == 0)
        def _():
            s_ref[...] = jnp.zeros_like(s_ref)

        row = lax.broadcasted_iota(jnp.int32, (CH, CH), 0)
        col = lax.broadcasted_iota(jnp.int32, (CH, CH), 1)
        c = _gdn_local(x_ref, sm_ref, gt_ref, row, col, cps)
        t = _tri_inv(c["a"], row, col)
        uw = _hi_b(t, jnp.concatenate([c["vb"], c["kbg"]], axis=2))
        u, w = uw[:, :, :128], uw[:, :, 128:]
        for ci in range(cps):
            sl = slice(4 * ci, 4 * ci + 4)
            rs = slice(ci * CH, (ci + 1) * CH)
            st = s_ref[...]
            st_ref[ci] = st
            sb = _b(st)
            vnew = u[sl] - _bd(_b(w[sl]), sb)
            o = _bd(_b(c["qdec"][sl]), sb) + _bd(_b(c["aqk"][sl]), _b(vnew))
            for h in range(4):
                o_ref[rs, h * 128:(h + 1) * 128] = o[h]
                t_ref[h, rs, :] = t[4 * ci + h]
            s_ref[...] = st * c["egl"][sl] + _bd_tn(_b(c["kdec"][sl]), _b(vnew))

    return pl.pallas_call(
        body, grid=(N // cps,),
        out_shape=(jax.ShapeDtypeStruct((S, 512), f32), jax.ShapeDtypeStruct((4, S, CH), f32), jax.ShapeDtypeStruct((N, 4, 128, 128), f32)),
        in_specs=[pl.BlockSpec((R, 1536), lambda n: (n, 0)), pl.BlockSpec((R, 128), lambda n: (n, 0)),
                  pl.BlockSpec((4, cps, 1, CH), lambda n: (0, n, 0, 0))],
        out_specs=(pl.BlockSpec((R, 512), lambda n: (n, 0)), pl.BlockSpec((4, R, CH), lambda n: (0, n, 0)),
                   pl.BlockSpec((cps, 4, 128, 128), lambda n: (n, 0, 0, 0))),
        scratch_shapes=[pltpu.VMEM((4, 128, 128), f32)], compiler_params=_cp(("arbitrary",)), name=name)(gqkv, sm, gt4)


GDN_BWD_CHUNKS = 4


def _gdn_bwd(gqkv, sm, gt4, tinv, states, do, name, cps=GDN_BWD_CHUNKS):
    S = gqkv.shape[0]
    N = S // CH
    cps = min(cps, N)
    R = cps * CH

    def body(x_ref, sm_ref, gt_ref, t_ref, st_ref, do_ref, dx_ref, dsm_ref, ds_ref):
        n = pl.program_id(0)

        @pl.when(n == 0)
        def _():
            ds_ref[...] = jnp.zeros_like(ds_ref)

        row = lax.broadcasted_iota(jnp.int32, (CH, CH), 0)
        col = lax.broadcasted_iota(jnp.int32, (CH, CH), 1)
        row1 = lax.broadcasted_iota(jnp.int32, (CH, 1), 0)
        lane = lax.broadcasted_iota(jnp.int32, (CH, 128), 1)
        ones = jnp.ones((4 * cps, CH, 128), f32)
        idx = [(ci, h) for ci in range(cps) for h in range(4)]
        c = _gdn_local(x_ref, sm_ref, gt_ref, row, col, cps)
        q, k, v, beta, eg = c["q"], c["k"], c["v"], c["beta"], c["eg"]
        t = jnp.stack([t_ref[h, ci * CH:(ci + 1) * CH, :] for ci, h in idx])
        uw = _hi_b(t, jnp.concatenate([c["vb"], c["kbg"]], axis=2))
        u, w = uw[:, :, :128], uw[:, :, 128:]
        st = st_ref[...].reshape(4 * cps, 128, 128)
        sb = _b(st)
        vnew = u - _bd(_b(w), sb)
        dob = _b(jnp.stack([do_ref[ci * CH:(ci + 1) * CH, h * 128:(h + 1) * 128] for ci, h in idx]))
        vnb = _b(vnew)
        dqdec = _bd_nt(dob, sb)
        daqk = jnp.where(row >= col, _bd_nt(dob, vnb), 0.0)
        qd_do = _bd_tn(_b(c["qdec"]), dob)
        aqk_do = _bd_tn(_b(c["aqk"]), dob)
        kdecb, wb = _b(c["kdec"]), _b(w)
        dvnew_l, dkdec_l, dgl_l = [None] * cps, [None] * cps, [None] * cps
        for ci in reversed(range(cps)):
            sl = slice(4 * ci, 4 * ci + 4)
            dsp = ds_ref[...]
            dspb = _b(dsp)
            dvn = _bd(kdecb[sl], dspb) + aqk_do[sl]
            dvnew_l[ci] = dvn
            dkdec_l[ci] = _bd_nt(vnb[sl], dspb)
            dgl_l[ci] = c["egl"][sl] * jnp.sum(dsp * st[sl], axis=(1, 2), keepdims=True)
            ds_ref[...] = dsp * c["egl"][sl] + qd_do[sl] - _bd_tn(wb[sl], _b(dvn))
        dvnew = jnp.concatenate(dvnew_l, axis=0)
        dkdec = jnp.concatenate(dkdec_l, axis=0)
        dgl = jnp.concatenate(dgl_l, axis=0)
        dw = -_bd_nt(_b(dvnew), sb)
        duw = _hi_b_tn(t, jnp.concatenate([dvnew, dw], axis=2))
        dvb, dkbg = duw[:, :, :128], duw[:, :, 128:]
        da = -jnp.where(row > col, _hi_b_nt(duw, uw), 0.0)
        dp = da * c["gam_s"]
        dqk = daqk * c["gam_i"]
        m = da * c["a"] + daqk * c["aqk"]
        csum = _hi_b_tn(m, ones)[:, :, 0:1]
        kk = dkdec * c["kdec"]

        def lsum(a):
            return jnp.sum(a, axis=2, keepdims=True)

        dgv = lsum(m) - csum + lsum(dqdec * c["qdec"]) - lsum(kk) + lsum(dkbg * c["kbg"])
        dgv = dgv + jnp.where(row1 == CH - 1, dgl + jnp.sum(kk, axis=(1, 2), keepdims=True), 0.0)
        dpb, dqkb = _b(dp), _b(dqk)
        dkb = _bd(dpb, _b(k)) + dkbg * eg
        dk = _bd_tn(dpb, _b(c["kb"])) + _bd_tn(dqkb, _b(q)) + dkdec * c["dec"] + dkb * beta
        dq = _bd(dqkb, _b(k)) + dqdec * eg
        dbeta = lsum(dkb * k) + lsum(dvb * v)
        dv = dvb * beta
        for ci in range(cps):
            rs = slice(ci * CH, (ci + 1) * CH)
            dsm = jnp.zeros((CH, 128), f32)
            for h in range(4):
                b = 4 * ci + h
                dx_ref[rs, h * 128:(h + 1) * 128] = dq[b]
                dx_ref[rs, 512 + h * 128:512 + (h + 1) * 128] = dk[b]
                dx_ref[rs, 1024 + h * 128:1024 + (h + 1) * 128] = dv[b]
                dsm = jnp.where(lane == 8 + h, dgv[b], jnp.where(lane == 12 + h, dbeta[b], dsm))
            dsm_ref[rs, :] = dsm

    G = N // cps
    return pl.pallas_call(
        body, grid=(G,), out_shape=(jax.ShapeDtypeStruct((S, 1536), f32), jax.ShapeDtypeStruct((S, 128), f32)),
        in_specs=[pl.BlockSpec((R, 1536), lambda n: (G - 1 - n, 0)), pl.BlockSpec((R, 128), lambda n: (G - 1 - n, 0)),
                  pl.BlockSpec((4, cps, 1, CH), lambda n: (0, G - 1 - n, 0, 0)), pl.BlockSpec((4, R, CH), lambda n: (0, G - 1 - n, 0)),
                  pl.BlockSpec((cps, 4, 128, 128), lambda n: (G - 1 - n, 0, 0, 0)), pl.BlockSpec((R, 512), lambda n: (G - 1 - n, 0))],
        out_specs=(pl.BlockSpec((R, 1536), lambda n: (G - 1 - n, 0)), pl.BlockSpec((R, 128), lambda n: (G - 1 - n, 0))),
        scratch_shapes=[pltpu.VMEM((4, 128, 128), f32)], compiler_params=_cp(("arbitrary",), VMEM_LIMIT), name=name)(gqkv, sm, gt4, tinv, states, do)


MERGE_ROWS = 256
MERGE_FWD_ROWS = 512


def _mem_attn(q, kv_ref, h):
    s = _dot_nt(q, kv_ref[:, h * 128:(h + 1) * 128]) * MEM_SCALE
    e = jnp.exp(s - jnp.max(s, axis=1, keepdims=True))
    return e / _rowsum(e)


def _gdn_out_norm(ob):
    r = lax.rsqrt(jnp.mean(ob * ob, axis=-1, keepdims=True) + EPS)
    return ob * r, r


def _merge_fwd(x, oa, ob, zb, zf, kv, b_merge, gdn_g, w_branch, w_out, name):
    S = x.shape[0]
    ts = min(S, MERGE_FWD_ROWS)

    def body(x_ref, oa_ref, ob_ref, mq_ref, az_ref, bz_ref, mz_ref, gt_ref, kv_ref, bm_ref, gg_ref, wb_ref, wo_ref,
             xo_ref, y_ref, mg_ref):
        y_ref[:, 0:512] = _b(oa_ref[...] * _silu(az_ref[...]))
        for h in range(4):
            sl = slice(h * 128, (h + 1) * 128)
            nb, _ = _gdn_out_norm(ob_ref[:, sl])
            y_ref[:, 512 + h * 128:512 + (h + 1) * 128] = _b(nb * gg_ref[...] * _silu(bz_ref[:, sl]))
            pm = _mem_attn(mq_ref[:, sl], kv_ref, h)
            om = _dot(_b(pm), kv_ref[:, 512 + h * 128:512 + (h + 1) * 128])
            y_ref[:, 1024 + h * 128:1024 + (h + 1) * 128] = _b(om * _silu(mz_ref[:, sl]))
        merged = jnp.zeros((ts, D), f32)
        for n in range(3):
            gate = _sig(gt_ref[:, n * D:(n + 1) * D] + bm_ref[:, n * D:(n + 1) * D])
            merged = merged + gate * _dot(y_ref[:, n * 512:(n + 1) * 512], wb_ref[n])
        mb = _b(merged)
        mg_ref[...] = mb
        xo_ref[...] = x_ref[...] + _dot(mb, wo_ref[...])

    def col(w, c):
        return pl.BlockSpec((ts, w), lambda i: (i, c))

    def full(shape):
        return pl.BlockSpec(shape, lambda i: tuple(0 for _ in shape))

    return pl.pallas_call(
        body, grid=(S // ts,),
        out_shape=(jax.ShapeDtypeStruct((S, D), f32), jax.ShapeDtypeStruct((S, 1536), bf16), jax.ShapeDtypeStruct((S, D), bf16)),
        in_specs=[col(D, 0), col(512, 0), col(512, 0), col(512, 3), col(512, 3), col(512, 4), col(512, 5), col(3072, 1),
                  full((256, D)), full((1, 3072)), full((1, 128)), full((3, 512, D)), full((D, D))],
        out_specs=(col(D, 0), col(1536, 0), col(D, 0)),
        compiler_params=_cp(("parallel",), VMEM_LIMIT), name=name)(x, oa, ob, zb, zf, zf, zf, zf, kv, b_merge, gdn_g, w_branch, w_out)


def _merge_bwd(dout, ycat, oa, ob, zb, zf, kv, b_merge, gdn_g, w_branch, w_branch_t, w_out_t, name):
    S = dout.shape[0]
    ts = min(S, MERGE_ROWS)

    def body(do_ref, y_ref, oa_ref, ob_ref, mq_ref, az_ref, bz_ref, mz_ref, gt_ref, kv_ref, bm_ref, gg_ref, wb_ref, wbt_ref, wot_ref,
             dpj_ref, dz_ref, dmq_ref, dlt_ref, doab_ref, dob_ref, dkv_ref, dbm_ref, dgg_ref):
        i = pl.program_id(0)
        dmerged = _dot(_b(do_ref[...]), wot_ref[...])
        dys = []
        dbm_parts = []
        for n in range(3):
            cs = slice(n * D, (n + 1) * D)
            gate = _sig(gt_ref[:, cs] + bm_ref[:, cs])
            proj = _dot(y_ref[:, n * 512:(n + 1) * 512], wb_ref[n])
            dlogit = dmerged * proj * gate * (1.0 - gate)
            dz_ref[:, 1536 + n * D:1536 + (n + 1) * D] = _b(dlogit)
            dbm_parts.append(_colsum(dlogit))
            dproj = _b(dmerged * gate)
            dpj_ref[:, cs] = dproj
            dys.append(_dot(dproj, wbt_ref[n]))
        dbm = jnp.broadcast_to(jnp.concatenate(dbm_parts, axis=1), (8, 3072))
        az = az_ref[...]
        oa = oa_ref[...]
        doa = dys[0] * _silu(az)
        doab_ref[...] = _b(doa)
        prod = doa * oa
        lane = lax.broadcasted_iota(jnp.int32, (ts, 128), 1)
        dl = jnp.zeros((ts, 128), f32)
        for p in range(4):
            blk = prod[:, p * 128:(p + 1) * 128]
            dl = jnp.where(lane == 2 * p, _rowsum(jnp.where(lane < 64, blk, 0.0)),
                           jnp.where(lane == 2 * p + 1, _rowsum(jnp.where(lane >= 64, blk, 0.0)), dl))
        dlt_ref[...] = jnp.transpose(dl)[0:8, :]
        dz_ref[:, 0:512] = _b(dys[0] * oa * _dsilu(az))
        gg = gg_ref[...]
        dgg = jnp.zeros((1, 128), f32)
        dkv_parts_k, dkv_parts_v = [], []
        for h in range(4):
            sl = slice(h * 128, (h + 1) * 128)
            bz = bz_ref[:, sl]
            dyb = dys[1][:, sl]
            nb, r = _gdn_out_norm(ob_ref[:, sl])
            dz_ref[:, 512 + h * 128:512 + (h + 1) * 128] = _b(dyb * nb * gg * _dsilu(bz))
            dng = dyb * _silu(bz)
            dgg = dgg + _colsum(dng * nb)
            dnb = dng * gg
            dob_ref[:, sl] = r * (dnb - nb * jnp.mean(dnb * nb, axis=-1, keepdims=True))
            mz = mz_ref[:, sl]
            dym = dys[2][:, sl]
            q = mq_ref[:, sl]
            kh = kv_ref[:, sl]
            vh = kv_ref[:, 512 + h * 128:512 + (h + 1) * 128]
            pm = _mem_attn(q, kv_ref, h)
            pmb = _b(pm)
            om = _dot(pmb, vh)
            dz_ref[:, 1024 + h * 128:1024 + (h + 1) * 128] = _b(dym * om * _dsilu(mz))
            dom = _b(dym * _silu(mz))
            dkv_parts_v.append(_dot_tn(pmb, dom))
            dpm = _dot_nt(dom, vh)
            dsm = _b(pm * (dpm - _rowsum(dpm * pm)) * MEM_SCALE)
            dmq_ref[:, sl] = _b(_dot(dsm, kh))
            dkv_parts_k.append(_dot_tn(dsm, q))
        dkv = jnp.concatenate(dkv_parts_k + dkv_parts_v, axis=1)
        dggb = jnp.broadcast_to(dgg, (8, 128))

        @pl.when(i == 0)
        def _():
            dkv_ref[...] = dkv
            dbm_ref[...] = dbm
            dgg_ref[...] = dggb

        @pl.when(i > 0)
        def _():
            dkv_ref[...] += dkv
            dbm_ref[...] += dbm
            dgg_ref[...] += dggb

    def col(w, c):
        return pl.BlockSpec((ts, w), lambda i: (i, c))

    def full(shape):
        return pl.BlockSpec(shape, lambda i: tuple(0 for _ in shape))

    return pl.pallas_call(
        body, grid=(S // ts,),
        out_shape=(jax.ShapeDtypeStruct((S, 3072), bf16), jax.ShapeDtypeStruct((S, DZ_MERGE_COLS), bf16), jax.ShapeDtypeStruct((S, 512), bf16),
                   jax.ShapeDtypeStruct((8, S), f32), jax.ShapeDtypeStruct((S, 512), bf16), jax.ShapeDtypeStruct((S, 512), f32),
                   jax.ShapeDtypeStruct((256, D), f32), jax.ShapeDtypeStruct((8, 3072), f32), jax.ShapeDtypeStruct((8, 128), f32)),
        in_specs=[col(D, 0), col(1536, 0), col(512, 0), col(512, 0), col(512, 3), col(512, 3), col(512, 4), col(512, 5), col(3072, 1),
                  full((256, D)), full((1, 3072)), full((1, 128)), full((3, 512, D)), full((3, D, 512)), full((D, D))],
        out_specs=(col(3072, 0), col(4608, 0), col(512, 0), pl.BlockSpec((8, ts), lambda i: (0, i)), col(512, 0), col(512, 0),
                   full((256, D)), full((8, 3072)), full((8, 128))),
        compiler_params=_cp(("arbitrary",), VMEM_LIMIT), name=name)(
            dout, ycat, oa, ob, zb, zf, zf, zf, zf, kv, b_merge, gdn_g, w_branch, w_branch_t, w_out_t)


def _mesh_pos():
    return lax.axis_index("x"), lax.axis_index("y"), lax.axis_index("c")


class _Gather:
    def __init__(self, x_refs, out_refs, send_sems, recv_sems, local_sems):
        self.n = len(x_refs)
        self.x_refs, self.out_refs = x_refs, out_refs
        self.send_sems, self.recv_sems, self.local_sems = send_sems, recv_sems, local_sems
        mx, my, mc = _mesh_pos()
        self.mc = mc
        self.me, self.sibling = (mx, my, mc), (mx, my, 1 - mc)
        self.chips = [(1 - mx, my), (mx, 1 - my), (1 - mx, 1 - my)]

    def copy(self, a, k, block, to, src=None):
        px, py, pc = block
        slot = self.out_refs[a].at[4 * px + 2 * py + pc]
        return pltpu.make_async_remote_copy(
            src_ref=slot if src is None else src, dst_ref=slot, send_sem=self.send_sems.at[7 * a + k],
            recv_sem=self.recv_sems.at[7 * a + k], device_id=to, device_id_type=pl.DeviceIdType.MESH)

    def own(self):
        mx, my, mc = self.me
        mine = [pltpu.make_async_copy(self.x_refs[a], self.out_refs[a].at[4 * mx + 2 * my + mc], self.local_sems.at[a])
                for a in range(self.n)]
        first = []
        for a in range(self.n):
            first.append(self.copy(a, 0, self.me, self.sibling, src=self.x_refs[a]))
            first += [self.copy(a, 1 + j, self.me, (*chip, self.mc), src=self.x_refs[a]) for j, chip in enumerate(self.chips)]
        return mine, first

    def start(self):
        mine, first = self.own()
        for cp in mine + first:
            cp.start()

    def finish(self):
        mine, first = self.own()
        passed = []
        for j, chip in enumerate(self.chips):
            for a in range(self.n):
                self.copy(a, 1 + j, (*chip, self.mc), self.me).wait_recv()
                fwd = self.copy(a, 4 + j, (*chip, self.mc), self.sibling)
                fwd.start()
                passed.append(fwd)
        for a in range(self.n):
            self.copy(a, 0, self.sibling, self.me).wait_recv()
            for j, chip in enumerate(self.chips):
                self.copy(a, 4 + j, (*chip, 1 - self.mc), self.me).wait_recv()
        for cp in first + passed:
            cp.wait_send()
        for cp in mine:
            cp.wait()


def _all_gather(xs, name):
    n = len(xs)

    def body(*refs):
        g = _Gather(refs[:n], refs[n:2 * n], *refs[2 * n:])
        g.start()
        g.finish()

    anyspec = pl.BlockSpec(memory_space=pl.ANY)
    return pl.pallas_call(
        body, out_shape=tuple(jax.ShapeDtypeStruct((N_DEV,) + x.shape, x.dtype) for x in xs),
        in_specs=[anyspec] * n, out_specs=tuple([anyspec] * n), scratch_shapes=_exchange_scratch(n), name=name)(*xs)


def _exchange_copies(s_refs, r_refs, send_sems, recv_sems, local_sems):
    n = len(s_refs)
    mx, my, mc = _mesh_pos()
    me_id = 4 * mx + 2 * my + mc
    copies = [pltpu.make_async_copy(s_refs[a].at[me_id], r_refs[a].at[me_id], local_sems.at[a]) for a in range(n)]
    for k in range(1, N_DEV):
        px = 1 - mx if k & 4 else mx
        py = 1 - my if k & 2 else my
        pc = 1 - mc if k & 1 else mc
        for a in range(n):
            copies.append(pltpu.make_async_remote_copy(
                src_ref=s_refs[a].at[4 * px + 2 * py + pc], dst_ref=r_refs[a].at[me_id],
                send_sem=send_sems.at[7 * a + k - 1], recv_sem=recv_sems.at[7 * a + k - 1],
                device_id=(px, py, pc), device_id_type=pl.DeviceIdType.MESH))
    return copies


def _exchange_scratch(n):
    return [pltpu.SemaphoreType.DMA((7 * n,)), pltpu.SemaphoreType.DMA((7 * n,)), pltpu.SemaphoreType.DMA((n,))]


def _exchange(sends, name):
    n = len(sends)

    def body(*refs):
        copies = _exchange_copies(refs[:n], refs[n:2 * n], *refs[2 * n:])
        for cp in copies:
            cp.start()
        for cp in copies:
            cp.wait()

    anyspec = pl.BlockSpec(memory_space=pl.ANY)
    return pl.pallas_call(
        body, out_shape=tuple(jax.ShapeDtypeStruct(s.shape, s.dtype) for s in sends),
        in_specs=[anyspec] * n, out_specs=tuple([anyspec] * n), scratch_shapes=_exchange_scratch(n), name=name)(*sends)


ADAMW_BLOCK_BYTES = 4 * 1024 * 1024


def _adamw(parts, w, m, v, name):
    _, R, C = parts.shape
    tr = R
    for t in (1024, 512, 256, 128, 64, 32, 16, 8):
        if R % t == 0 and N_DEV * t * C * 4 <= ADAMW_BLOCK_BYTES:
            tr = t
            break

    def body(p_ref, w_ref, m_ref, v_ref, g_ref, d_ref, nm_ref, nv_ref):
        g = p_ref[0].astype(f32)
        for j in range(1, N_DEV):
            g = g + p_ref[j].astype(f32)
        mn = ADAM_B1 * m_ref[...] + (1.0 - ADAM_B1) * g
        vn = ADAM_B2 * v_ref[...] + (1.0 - ADAM_B2) * jnp.square(g)
        m_hat = mn / (1.0 - ADAM_B1 ** ADAM_STEP)
        v_hat = vn / (1.0 - ADAM_B2 ** ADAM_STEP)
        g_ref[...] = g
        d_ref[...] = -ADAM_LR * (m_hat / (jnp.sqrt(v_hat) + ADAM_EPS) + ADAM_WD * w_ref[...])
        nm_ref[...] = mn
        nv_ref[...] = vn

    t2 = pl.BlockSpec((tr, C), lambda i: (i, 0))
    out = jax.ShapeDtypeStruct((R, C), f32)
    return pl.pallas_call(
        body, grid=(R // tr,), out_shape=(out, out, out, out),
        in_specs=[pl.BlockSpec((N_DEV, tr, C), lambda i: (0, i, 0)), t2, t2, t2], out_specs=(t2, t2, t2, t2),
        compiler_params=_cp(("parallel",), VMEM_LIMIT), name=name)(parts, w, m, v)


def _as2d(a):
    return a.reshape(-1, a.shape[-1])


def _perm_cols(w, order=_ORDER):
    parts = [w[..., _COLS[n][0]:_COLS[n][1]] for n in order]
    pad = jnp.zeros(w.shape[:-1] + (N_ALL - N_IN,), w.dtype)
    return jnp.concatenate(parts + [pad], axis=-1)


def _unperm_cols(w, order=_ORDER):
    pieces, off = {}, 0
    for n in order:
        width = _COLS[n][1] - _COLS[n][0]
        pieces[n] = w[..., off:off + width]
        off += width
    return jnp.concatenate([pieces[n] for n in sorted(_COLS, key=lambda n: _COLS[n][0])], axis=-1)


_SMALL_ROWS = 16


def _pack_small(t):
    z = jnp.zeros((D,), f32)
    misc = z.at[0:16].set(t["b_fg"].reshape(-1)).at[16:24].set(t["a_log"].reshape(-1)).at[24:32].set(t["dt_bias"].reshape(-1))
    misc = misc.at[128:384].set(t["gdn_norm_g"].reshape(-1))
    if "extra" in t:
        misc = misc.at[512].set(t["extra"])
    rows = [t["norm_g"], t["b_merge"].reshape(6, D), t["mem_norm_g"], t["final_norm_g"][None], misc[None],
            jnp.zeros((_SMALL_ROWS - 12, D), f32)]
    return jnp.concatenate(rows, axis=0)


def _unpack_small(a):
    misc = a[11]
    return dict(norm_g=a[0:2], b_merge=a[2:8].reshape(2, 3072), mem_norm_g=a[8:10], final_norm_g=a[10],
                b_fg=misc[0:16].reshape(2, 8), a_log=misc[16:24].reshape(2, 4), dt_bias=misc[24:32].reshape(2, 4),
                gdn_norm_g=misc[128:384].reshape(2, 128), extra=misc[512])


def _layer_fwd(l, x, mem, p, gather_next=None):
    sfx = f"_l{l}"
    h, ht = _norm_fwd(x, p["norm_g"], "norm_fwd" + sfx, with_t=True)
    zb = _mm(h, p["w_b"], bf16, 1024, 1024, 1024, "inproj_b" + sfx)
    if gather_next is None:
        zf, gathered = _mm(h, p["w_f"], f32, 1024, 1024, 1024, "inproj_f" + sfx), None
    else:
        zf, gathered = _mm(h, p["w_f"], f32, 1024, 1024, 1024, "inproj_f" + sfx, comm=("gather", gather_next))
    zs = _mm(h, p["w_s"], f32, 512, 128, 1024, "inproj_s" + sfx)
    sm = _small_prep(zs, p["par"], "small_prep" + sfx)
    S = x.shape[0]
    gt4 = jnp.transpose(sm[:, 8:12]).reshape(4, S // CH, 1, CH)
    qa, ka, st = _fox_prep(zb, sm, "fox_prep" + sfx)
    tab = _fox_bound_table(st, S, min(S, FOX_TILE))
    oa, lse_t = _fox_fwd(qa, ka, zb, tab, "fox_fwd" + sfx)
    gqkv = _gdn_prep(zf, p["conv_w"], "gdn_prep" + sfx)
    ob, tinv, states = _gdn_fwd(gqkv, sm, gt4, "gdn_fwd" + sfx)
    memn = _norm_fwd(mem, p["mem_norm_g"], "mem_norm" + sfx)
    kv = _mm(memn, p["w_mem_kv"], bf16, 256, 1024, 1024, "mem_kv" + sfx)
    xo, ycat, merged = _merge_fwd(x, oa, ob, zb, zf, kv, p["b_merge"], p["gdn_norm_g"], p["w_branch"], p["w_out"], "merge_fwd" + sfx)
    saved = dict(x=x, ht=ht, zb=zb, zf=zf, zs=zs, sm=sm, qa=qa, ka=ka, st=st, gt4=gt4, oa=oa, lse_t=lse_t, gqkv=gqkv, ob=ob, tinv=tinv,
                 states=states, memn=memn, kv=kv, ycat=ycat, merged=merged)
    return xo, saved, gathered


def _layer_bwd(l, dout, mem, p, s, comm_sends=(), send_fn=None):
    sfx = f"_l{l}"
    dproj, dzf2, dmq, delta, doab, dob, dkv, dbm, dgg = _merge_bwd(
        dout, s["ycat"], s["oa"], s["ob"], s["zb"], s["zf"], s["kv"], p["b_merge"], p["gdn_norm_g"],
        p["w_branch"], p["w_branch_t"], p["w_out_t"], "merge_bwd" + sfx)
    g = {}
    g["w_out"] = _mm(s["merged"], dout, f32, 512, 1024, 512, "dw_out" + sfx, trans_a=True)
    g["w_branch"] = jnp.stack([
        _mm(s["ycat"], dproj, f32, 512, 1024, 512, f"dw_branch{n}" + sfx, trans_a=True, a_cols=(n * 512, 512), b_cols=(n * D, D))
        for n in range(3)])
    g["b_merge"] = dbm[0]
    g["gdn_norm_g"] = dgg[0]
    g["w_mem_kv"] = _mm(s["memn"], dkv, f32, 512, 1024, 256, "dw_mem_kv" + sfx, trans_a=True)
    dmemn = _mm(dkv, p["w_mem_kv_t"], f32, 256, 1024, 1024, "dmem_n" + sfx)
    g["mem_norm_g"] = _norm_bwd(mem, p["mem_norm_g"], dmemn, None, "mem_norm_bwd" + sfx)[0]
    dgqkv, dsm = _gdn_bwd(s["gqkv"], s["sm"], s["gt4"], s["tinv"], s["states"], dob, "gdn_bwd" + sfx)
    dbqkv, dcw = _gdn_prep_bwd(s["zf"], p["conv_w"], dgqkv, "gdn_prep_bwd" + sfx)
    g["conv_w"] = dcw[0:4]
    S = dout.shape[0]
    lse_rows = _head_rows(s["lse_t"])
    tab = _fox_bound_table(s["st"], S, min(S, FOX_TILE), lse_rows=lse_rows)
    (dq, dk, dv, dfc, dfr), received = _fox_bwd(s["qa"], s["ka"], s["zb"], doab, lse_rows, delta, tab,
                                                "fox_bwd" + sfx, comm_sends=comm_sends)
    dzs, sacc = _small_bwd(s["zs"], p["par"], _head_rows(dfr), dfc, dsm, "small_bwd" + sfx)
    g["b_fg"], g["a_log"], g["dt_bias"] = sacc[0, 0:8], sacc[1, 8:12], sacc[2, 8:12]
    dz = [(dzf2, DZ_MERGE_COLS, PIECE_COLS), (dbqkv, PIECE_COLS, PIECE_COLS), ([dq, dk, dv], PIECE_COLS, PIECE_COLS),
          ([dmq, dzs], PIECE_COLS, PIECE_COLS)]
    g["w_in"] = _mm_n_pieces(s["ht"], dz, f32, 1024, 1024, "dw_in" + sfx)
    if send_fn is None:
        dh, received_late = _mm_k_pieces(dz, p["w_all_t"], f32, 1024, "dh" + sfx), None
    else:
        dh, received_late = _mm_k_pieces(dz, p["w_all_t"], f32, 1024, "dh" + sfx, comm=("exchange", send_fn(g)))
    dx, dng = _norm_bwd(s["x"], p["norm_g"], dh, dout, "norm_bwd" + sfx)
    g["norm_g"] = dng[0]
    return dx, g, received, received_late


def kernel(x, mem, norm_g, w_in, b_fg, b_merge, conv_w, a_log, dt_bias, gdn_norm_g, mem_norm_g, w_mem_kv, w_branch, w_out, final_norm_g, loss_target, m_norm_g, m_w_in, m_b_fg, m_b_merge, m_conv_w, m_a_log, m_dt_bias, m_gdn_norm_g, m_mem_norm_g, m_w_mem_kv, m_w_branch, m_w_out, m_final_norm_g, v_norm_g, v_w_in, v_b_fg, v_b_merge, v_conv_w, v_a_log, v_dt_bias, v_gdn_norm_g, v_mem_norm_g, v_w_mem_kv, v_w_branch, v_w_out, v_final_norm_g):
    x0, mem0, tgt = x[0], mem[0], loss_target[0]
    shard_w = dict(w_in=w_in, w_mem_kv=w_mem_kv, w_branch=w_branch, w_out=w_out, conv_w=conv_w)
    shard_m = dict(w_in=m_w_in, w_mem_kv=m_w_mem_kv, w_branch=m_w_branch, w_out=m_w_out, conv_w=m_conv_w)
    shard_v = dict(w_in=v_w_in, w_mem_kv=v_w_mem_kv, w_branch=v_w_branch, w_out=v_w_out, conv_w=v_conv_w)
    small_w = dict(norm_g=norm_g, b_fg=b_fg, b_merge=b_merge, a_log=a_log, dt_bias=dt_bias, gdn_norm_g=gdn_norm_g,
                   mem_norm_g=mem_norm_g, final_norm_g=final_norm_g)
    small_m = dict(norm_g=m_norm_g, b_fg=m_b_fg, b_merge=m_b_merge, a_log=m_a_log, dt_bias=m_dt_bias, gdn_norm_g=m_gdn_norm_g,
                   mem_norm_g=m_mem_norm_g, final_norm_g=m_final_norm_g)
    small_v = dict(norm_g=v_norm_g, b_fg=v_b_fg, b_merge=v_b_merge, a_log=v_a_log, dt_bias=v_dt_bias, gdn_norm_g=v_gdn_norm_g,
                   mem_norm_g=v_mem_norm_g, final_norm_g=v_final_norm_g)

    def shards(l):
        return [_b(w_in[l]), _b(w_mem_kv[l]), _b(_as2d(w_branch[l])), _b(w_out[l])]

    def layer_params(l, g_in, g_kv, g_br, g_out, conv_full):
        w_full = jnp.transpose(g_in, (1, 0, 2)).reshape(D, N_IN)
        w_all = _perm_cols(w_full)
        w_kv = g_kv.reshape(D, D)
        w_br = jnp.transpose(g_br.reshape(N_DEV, 3, 512, 128), (1, 2, 0, 3)).reshape(3, 512, D)
        w_o = g_out.reshape(D, D)
        return dict(
            norm_g=norm_g[l][None], mem_norm_g=mem_norm_g[l][None], gdn_norm_g=gdn_norm_g[l][None], b_merge=b_merge[l][None],
            par=_small_pars(b_fg[l], a_log[l], dt_bias[l]),
            conv_w=jnp.pad(conv_full[l], ((0, 4), (0, 0))),
            w_b=w_all[:, 0:NB], w_f=w_all[:, NB:NB + NF], w_s=w_all[:, NB + NF:NB + NF + NS],
            w_all_t=jnp.transpose(_perm_cols(w_full, _ORDER_BWD)),
            w_mem_kv=w_kv, w_mem_kv_t=jnp.transpose(w_kv),
            w_branch=w_br, w_branch_t=jnp.transpose(w_br, (0, 2, 1)),
            w_out=w_o, w_out_t=jnp.transpose(w_o))

    *gathered0, conv_all = _all_gather(shards(0) + [_as2d(conv_w)], "gather_weights")
    conv_full = jnp.transpose(conv_all.reshape(N_DEV, DEPTH, 4, 192), (1, 2, 0, 3)).reshape(DEPTH, 4, 1536)
    layers = [layer_params(0, *gathered0, conv_full), None]

    saved = [None] * DEPTH
    acts, saved[0], gathered1 = _layer_fwd(0, x0, mem0, layers[0], gather_next=shards(1))
    layers[1] = layer_params(1, *gathered1, conv_full)
    acts, saved[1], _ = _layer_fwd(1, acts, mem0, layers[1])
    dx, dfg, lsum = _loss_head(acts, final_norm_g[None], tgt, "loss_head")

    def send_buffers(g):
        dw_in = _unperm_cols(g["w_in"], _ORDER_BWD)
        send = dict(
            w_in=jnp.transpose(dw_in.reshape(D, N_DEV, 1026), (1, 0, 2)),
            w_mem_kv=g["w_mem_kv"].reshape(N_DEV, 128, D),
            w_branch=jnp.transpose(g["w_branch"].reshape(3, 512, N_DEV, 128), (2, 0, 1, 3)).reshape(N_DEV, 3 * 512, 128),
            w_out=g["w_out"].reshape(N_DEV, 128, D),
            conv_w=jnp.transpose(g["conv_w"].reshape(4, N_DEV, 192), (1, 0, 2)))
        return [_b(send[n]) for n in _SHARDED]

    grads, parts = [None] * DEPTH, [None] * DEPTH
    dx, grads[1], _, _ = _layer_bwd(1, dx, mem0, layers[1], saved[1])
    dx, grads[0], parts[1], parts[0] = _layer_bwd(0, dx, mem0, layers[0], saved[0], comm_sends=send_buffers(grads[1]),
                                                  send_fn=send_buffers)
    grad_x = dx[None]

    big = [{}, {}, {}, {}]
    for a, n in enumerate(_SHARDED):
        res = [_adamw(parts[l][a], _as2d(shard_w[n][l]), _as2d(shard_m[n][l]), _as2d(shard_v[n][l]), f"adamw_{n}_l{l}")
               for l in range(DEPTH)]
        for kind in range(4):
            big[kind][n] = jnp.stack([res[l][kind] for l in range(DEPTH)]).reshape(shard_w[n].shape)

    small_g = {k: jnp.stack([grads[l][k] for l in range(DEPTH)]) for k in ("norm_g", "b_fg", "b_merge", "a_log", "dt_bias", "gdn_norm_g", "mem_norm_g")}
    small_g["final_norm_g"] = dfg[0]
    small_g["extra"] = lsum[0, 0]
    parts_s, = _all_gather([_pack_small(small_g)], "gather_small")
    g_sm, d_sm, m_sm, v_sm = _adamw(parts_s, _pack_small(small_w), _pack_small(small_m), _pack_small(small_v), "adamw_replicated")

    sml = [_unpack_small(a) for a in (g_sm, d_sm, m_sm, v_sm)]
    loss = sml[0]["extra"]
    names = ("norm_g", "w_in", "b_fg", "b_merge", "conv_w", "a_log", "dt_bias", "gdn_norm_g", "mem_norm_g", "w_mem_kv", "w_branch", "w_out", "final_norm_g")
    outs = [loss, grad_x]
    for kind in range(4):
        for n in names:
            outs.append(big[kind][n] if n in big[kind] else sml[kind][n])
    return tuple(outs)
```

```python
import functools

import jax
import jax.numpy as jnp
from jax import lax
from jax.experimental import pallas as pl
from jax.experimental.pallas import tpu as pltpu

f32, bf16 = jnp.float32, jnp.bfloat16

D = 1024
EPS = 1e-6
CH = 64
N_DEV = 8
DEPTH = 2
FOX_SCALE = 64 ** -0.5
GDN_SCALE = 128 ** -0.5
MEM_SCALE = 128 ** -0.5
NEG = -1e30
VMEM_LIMIT = 56 * 1024 * 1024

ADAM_LR, ADAM_B1, ADAM_B2, ADAM_EPS, ADAM_WD, ADAM_STEP = 0.001, 0.9, 0.999, 1e-08, 0.01, 10

_COLS = dict(aq=(0, 512), ak=(512, 1024), av=(1024, 1536), af=(1536, 1544), az=(1544, 2056),
             bq=(2056, 2568), bk=(2568, 3080), bv=(3080, 3592), ba=(3592, 3596), bb=(3596, 3600),
             bz=(3600, 4112), mq=(4112, 4624), mz=(4624, 5136), gates=(5136, 8208))
_ORDER = ("aq", "ak", "av", "mq", "bq", "bk", "bv", "az", "bz", "mz", "gates", "af", "ba", "bb")
_ORDER_BWD = ("az", "bz", "mz", "gates", "bq", "bk", "bv", "aq", "ak", "av", "mq", "af", "ba", "bb")
DZ_MERGE_COLS = 4608
N_IN = 8208
NB, NF, NS = 2048, 6144, 128
PIECE_COLS = 1536
SMALL_COLS = 1024
N_ALL = NB + NF + SMALL_COLS

_SHARDED = ("w_in", "w_mem_kv", "w_branch", "w_out", "conv_w")


def _cp(sem=None, vmem=None):
    kw = {}
    if sem is not None:
        kw["dimension_semantics"] = sem
    if vmem is not None:
        kw["vmem_limit_bytes"] = vmem
    return pltpu.CompilerParams(**kw)


def _dot(a, b):
    return jnp.dot(a, b, preferred_element_type=f32)


def _dot_nt(a, b):
    return lax.dot_general(a, b, (((1,), (1,)), ((), ())), preferred_element_type=f32)


def _dot_tn(a, b):
    return lax.dot_general(a, b, (((0,), (0,)), ((), ())), preferred_element_type=f32)


def _split2(x):
    hi = x.astype(bf16)
    return hi, (x - hi.astype(f32)).astype(bf16)


def _mm3(a, b, dims):
    ah, al = _split2(a)
    bh, bl = _split2(b)
    dg = functools.partial(lax.dot_general, dimension_numbers=dims, preferred_element_type=f32)
    return dg(ah, bh) + (dg(ah, bl) + dg(al, bh))


def _hi(a, b):
    return _mm3(a, b, (((1,), (0,)), ((), ())))


def _hi_nt(a, b):
    return _mm3(a, b, (((1,), (1,)), ((), ())))


def _hi_tn(a, b):
    return _mm3(a, b, (((0,), (0,)), ((), ())))


def _hi_b(a, b):
    return _mm3(a, b, (((2,), (1,)), ((0,), (0,))))


def _b(x):
    return x.astype(bf16)


def _sig(x):
    return jax.nn.sigmoid(x)


def _silu(x):
    return x * _sig(x)


def _dsilu(x):
    s = _sig(x)
    return s * (1.0 + x * (1.0 - s))


def _softplus(x):
    return jnp.maximum(x, 0.0) + jnp.log1p(jnp.exp(-jnp.abs(x)))


def _rowsum(x):
    return jnp.sum(x, axis=1, keepdims=True)


def _colsum(x):
    return jnp.sum(x, axis=0, keepdims=True)


def _norm_fwd(x, g, name, with_t=False):
    M = x.shape[0]
    ts = min(M, 512)

    def body(x_ref, g_ref, h_ref, *t_ref):
        xv = x_ref[...]
        r = lax.rsqrt(jnp.mean(xv * xv, axis=-1, keepdims=True) + EPS)
        h = xv * r * g_ref[...]
        h_ref[...] = _b(h)
        if with_t:
            t_ref[0][...] = _b(jnp.transpose(h))

    tile = pl.BlockSpec((ts, D), lambda i: (i, 0))
    shapes, specs = jax.ShapeDtypeStruct((M, D), bf16), tile
    if with_t:
        shapes, specs = (shapes, jax.ShapeDtypeStruct((D, M), bf16)), (tile, pl.BlockSpec((D, ts), lambda i: (0, i)))
    return pl.pallas_call(
        body, grid=(M // ts,), out_shape=shapes,
        in_specs=[tile, pl.BlockSpec((1, D), lambda i: (0, 0))],
        out_specs=specs, compiler_params=_cp(("parallel",)), name=name)(x, g)


def _norm_bwd(x, g, dh, dres, name):
    M = x.shape[0]
    ts = min(M, 512)
    with_dx = dres is not None

    def body(*refs):
        if with_dx:
            x_ref, g_ref, dh_ref, dres_ref, dx_ref, dg_ref = refs
        else:
            x_ref, g_ref, dh_ref, dg_ref = refs
        i = pl.program_id(0)
        xv = x_ref[...]
        r = lax.rsqrt(jnp.mean(xv * xv, axis=-1, keepdims=True) + EPS)
        xh = xv * r
        dh = dh_ref[...].astype(f32)
        part = jnp.broadcast_to(_colsum(dh * xh), (8, D))

        @pl.when(i == 0)
        def _():
            dg_ref[...] = part

        @pl.when(i > 0)
        def _():
            dg_ref[...] += part

        if with_dx:
            dxh = dh * g_ref[...]
            dx_ref[...] = dres_ref[...] + r * (dxh - xh * jnp.mean(dxh * xh, axis=-1, keepdims=True))

    tile = pl.BlockSpec((ts, D), lambda i: (i, 0))
    gspec = pl.BlockSpec((1, D), lambda i: (0, 0))
    acc = pl.BlockSpec((8, D), lambda i: (0, 0))
    if with_dx:
        return pl.pallas_call(
            body, grid=(M // ts,), out_shape=(jax.ShapeDtypeStruct((M, D), f32), jax.ShapeDtypeStruct((8, D), f32)),
            in_specs=[tile, gspec, tile, tile], out_specs=(tile, acc), compiler_params=_cp(("arbitrary",)), name=name)(x, g, dh, dres)
    return pl.pallas_call(
        body, grid=(M // ts,), out_shape=jax.ShapeDtypeStruct((8, D), f32),
        in_specs=[tile, gspec, tile], out_specs=acc, compiler_params=_cp(("arbitrary",)), name=name)(x, g, dh)


def _loss_head(x, g, tgt, name):
    M = x.shape[0]
    ts = min(M, 512)

    def body(x_ref, g_ref, t_ref, dx_ref, dg_ref, ls_ref):
        i = pl.program_id(0)
        xv = x_ref[...]
        gv = g_ref[...]
        r = lax.rsqrt(jnp.mean(xv * xv, axis=-1, keepdims=True) + EPS)
        xh = xv * r
        e = xh * gv - t_ref[...]
        lpart = 0.5 * jnp.sum(jnp.mean(e * e, axis=-1, keepdims=True), axis=0, keepdims=True)
        dy = e * (1.0 / D)
        dgp = jnp.broadcast_to(_colsum(dy * xh), (8, D))
        lp = jnp.broadcast_to(lpart, (8, 128))

        @pl.when(i == 0)
        def _():
            dg_ref[...] = dgp
            ls_ref[...] = lp

        @pl.when(i > 0)
        def _():
            dg_ref[...] += dgp
            ls_ref[...] += lp

        dxh = dy * gv
        dx_ref[...] = r * (dxh - xh * jnp.mean(dxh * xh, axis=-1, keepdims=True))

    tile = pl.BlockSpec((ts, D), lambda i: (i, 0))
    return pl.pallas_call(
        body, grid=(M // ts,),
        out_shape=(jax.ShapeDtypeStruct((M, D), f32), jax.ShapeDtypeStruct((8, D), f32), jax.ShapeDtypeStruct((8, 128), f32)),
        in_specs=[tile, pl.BlockSpec((1, D), lambda i: (0, 0)), tile],
        out_specs=(tile, pl.BlockSpec((8, D), lambda i: (0, 0)), pl.BlockSpec((8, 128), lambda i: (0, 0))),
        compiler_params=_cp(("arbitrary",)), name=name)(x, g, tgt)


def _mm(a, b, out_dtype, tm, tn, tk, name, trans_a=False, a_cols=None, b_cols=None, comm=None):
    if trans_a:
        K, M = a.shape
    else:
        M, K = a.shape
    N = b.shape[1]
    a0, b0 = 0, 0
    if a_cols is not None:
        a0, M = a_cols
    if b_cols is not None:
        b0, N = b_cols
    tm, tn, tk = min(tm, M), min(tn, N), min(tk, K)
    nk = K // tk
    a0, b0 = a0 // tm, b0 // tn
    grid = (M // tm, N // tn, nk)
    kind, carr = comm if comm is not None else (None, ())
    nc = len(carr)

    def body(*refs):
        a_ref, b_ref = refs[:2]
        o_ref = refs[2 + nc]
        acc_ref = refs[3 + 2 * nc]
        k = pl.program_id(2)
        if nc:
            cs_refs, cr_refs, sems = refs[2:2 + nc], refs[3 + nc:3 + 2 * nc], refs[4 + 2 * nc:]
            step = (pl.program_id(0) * grid[1] + pl.program_id(1)) * nk + k

            @pl.when(step == 0)
            def _():
                if kind == "gather":
                    _Gather(cs_refs, cr_refs, *sems).start()
                else:
                    for cp in _exchange_copies(cs_refs, cr_refs, *sems):
                        cp.start()
        av, bv = _b(a_ref[...]), _b(b_ref[...])
        part = _dot_tn(av, bv) if trans_a else _dot(av, bv)
        if nk == 1:
            o_ref[...] = part.astype(out_dtype)
        else:
            @pl.when(k == 0)
            def _():
                acc_ref[...] = part

            @pl.when(k > 0)
            def _():
                acc_ref[...] += part

            @pl.when(k == nk - 1)
            def _():
                o_ref[...] = acc_ref[...].astype(out_dtype)
        if nc:
            @pl.when(step == grid[0] * grid[1] * nk - 1)
            def _():
                if kind == "gather":
                    _Gather(cs_refs, cr_refs, *sems).finish()
                else:
                    for cp in _exchange_copies(cs_refs, cr_refs, *sems):
                        cp.wait()

    a_spec = pl.BlockSpec((tk, tm), lambda i, j, k: (k, i + a0)) if trans_a else pl.BlockSpec((tm, tk), lambda i, j, k: (i, k))
    anyspec = pl.BlockSpec(memory_space=pl.ANY)
    recv_shapes = tuple(jax.ShapeDtypeStruct(((N_DEV,) + c.shape) if kind == "gather" else c.shape, c.dtype) for c in carr)
    out = pl.pallas_call(
        body, grid=grid, out_shape=(jax.ShapeDtypeStruct((M, N), out_dtype),) + recv_shapes,
        in_specs=[a_spec, pl.BlockSpec((tk, tn), lambda i, j, k: (k, j + b0))] + [anyspec] * nc,
        out_specs=(pl.BlockSpec((tm, tn), lambda i, j, k: (i, j)),) + tuple([anyspec] * nc),
        scratch_shapes=[pltpu.VMEM((tm, tn), f32)] + (_exchange_scratch(nc) if nc else []),
        compiler_params=_cp(("arbitrary",) * 3 if nc else ("parallel", "parallel", "arbitrary"), VMEM_LIMIT), name=name)(a, b, *carr)
    return (out[0], list(out[1:])) if nc else out[0]


def _piece_spans(pieces):
    spans, start, col = [], 0, 0
    for a, w, bw in pieces:
        assert w % bw == 0 and col % bw == 0
        assert not isinstance(a, (list, tuple)) or (w == bw and sum(x.shape[1] for x in a) == w)
        spans.append((start, w // bw, bw, col))
        start += w // bw
        col += w
    return spans, start


def _piece_arrays(pieces):
    arrs = [list(a) if isinstance(a, (list, tuple)) else [a] for a, _, _ in pieces]
    offs = [sum(len(x) for x in arrs[:p]) for p in range(len(arrs))]
    return arrs, offs


def _piece_value(p_refs, arrs, offs, p):
    vals = [_b(p_refs[offs[p] + j][...]) for j in range(len(arrs[p]))]
    return vals[0] if len(vals) == 1 else jnp.concatenate(vals, axis=1)


def _mm_k_pieces(pieces, w, out_dtype, tm, name, comm=None):
    spans, nk = _piece_spans(pieces)
    arrs, offs = _piece_arrays(pieces)
    M, N = arrs[0][0].shape[0], w.shape[1]
    tm = min(tm, M)
    npc = sum(len(x) for x in arrs)
    bws = sorted({bw for _, _, bw, _ in spans}, reverse=True)
    nw = len(bws)
    kind, carr = comm if comm is not None else (None, ())
    nc = len(carr)

    def body(*refs):
        p_refs, w_refs = refs[:npc], refs[npc:npc + nw]
        refs = refs[nw - 1:]
        o_ref = refs[npc + 1 + nc]
        acc_ref = refs[npc + 2 + 2 * nc]
        k = pl.program_id(1)
        if nc:
            cs_refs, cr_refs, sems = refs[npc + 1:npc + 1 + nc], refs[npc + 2 + nc:npc + 2 + 2 * nc], refs[npc + 3 + 2 * nc:]
            step = pl.program_id(0) * nk + k

            @pl.when(step == 0)
            def _():
                for cp in _exchange_copies(cs_refs, cr_refs, *sems):
                    cp.start()

        @pl.when(k == 0)
        def _():
            acc_ref[...] = jnp.zeros_like(acc_ref)

        for p, (start, n, bw, _) in enumerate(spans):
            @pl.when((k >= start) & (k < start + n))
            def _():
                acc_ref[...] += _dot(_piece_value(p_refs, arrs, offs, p), w_refs[bws.index(bw)][...])

        @pl.when(k == nk - 1)
        def _():
            o_ref[...] = acc_ref[...].astype(out_dtype)

        if nc:
            @pl.when(step == (M // tm) * nk - 1)
            def _():
                for cp in _exchange_copies(cs_refs, cr_refs, *sems):
                    cp.wait()

    assert kind in (None, "exchange")
    anyspec = pl.BlockSpec(memory_space=pl.ANY)
    p_specs = []
    for p, (s, n, bw, _) in enumerate(spans):
        if len(arrs[p]) == 1:
            p_specs.append(pl.BlockSpec((tm, bw), functools.partial(lambda i, k, s, n: (i, jnp.clip(k - s, 0, n - 1)), s=s, n=n)))
        else:
            p_specs += [pl.BlockSpec((tm, x.shape[1]), lambda i, k: (i, 0)) for x in arrs[p]]
    w_specs = []
    for bw in bws:
        mine = [sp for sp in spans if sp[2] == bw]
        first, steps, row0 = mine[0][0], sum(sp[1] for sp in mine), mine[0][3] // bw
        assert mine[-1][0] + mine[-1][1] - first == steps
        w_specs.append(pl.BlockSpec((bw, N), functools.partial(lambda i, k, f, s, r: (r + jnp.clip(k - f, 0, s - 1), 0), f=first, s=steps, r=row0)))
    out = pl.pallas_call(
        body, grid=(M // tm, nk),
        out_shape=(jax.ShapeDtypeStruct((M, N), out_dtype),) + tuple(jax.ShapeDtypeStruct(c.shape, c.dtype) for c in carr),
        in_specs=p_specs + w_specs + [anyspec] * nc,
        out_specs=(pl.BlockSpec((tm, N), lambda i, k: (i, 0)),) + tuple([anyspec] * nc),
        scratch_shapes=[pltpu.VMEM((tm, N), f32)] + (_exchange_scratch(nc) if nc else []),
        compiler_params=_cp(("arbitrary", "arbitrary"), VMEM_LIMIT), name=name)(*[x for a in arrs for x in a], *([w] * nw), *carr)
    return (out[0], list(out[1:])) if nc else out[0]


def _mm_n_pieces(a, pieces, out_dtype, tm, tk, name):
    M, K = a.shape
    tm, tk = min(tm, M), min(tk, K)
    spans, nn = _piece_spans(pieces)
    arrs, offs = _piece_arrays(pieces)
    cols = pieces[0][2]
    assert all(bw == cols for _, _, bw in pieces)
    npc, nk = sum(len(x) for x in arrs), K // tk

    def body(*refs):
        a_ref, p_refs, o_ref, acc_ref = refs[0], refs[1:1 + npc], refs[1 + npc], refs[2 + npc]
        j, k = pl.program_id(1), pl.program_id(2)

        @pl.when(k == 0)
        def _():
            acc_ref[...] = jnp.zeros_like(acc_ref)

        for p, (start, n, _, _) in enumerate(spans):
            @pl.when((j >= start) & (j < start + n))
            def _():
                acc_ref[...] += _dot(_b(a_ref[...]), _piece_value(p_refs, arrs, offs, p))

        @pl.when(k == nk - 1)
        def _():
            o_ref[...] = acc_ref[...].astype(out_dtype)

    def p_map(i, j, k, s, n):
        inside = (j >= s) & (j < s + n)
        return jnp.where(inside, k, 0), jnp.clip(j - s, 0, n - 1)

    p_specs = []
    for p, (s, n, _, _) in enumerate(spans):
        widths = [cols] if len(arrs[p]) == 1 else [x.shape[1] for x in arrs[p]]
        p_specs += [pl.BlockSpec((tk, wd), functools.partial(p_map, s=s, n=n)) for wd in widths]
    return pl.pallas_call(
        body, grid=(M // tm, nn, nk), out_shape=jax.ShapeDtypeStruct((M, nn * cols), out_dtype),
        in_specs=[pl.BlockSpec((tm, tk), lambda i, j, k: (i, k))] + p_specs,
        out_specs=pl.BlockSpec((tm, cols), lambda i, j, k: (i, j)),
        scratch_shapes=[pltpu.VMEM((tm, cols), f32)],
        compiler_params=_cp(("parallel", "parallel", "arbitrary"), VMEM_LIMIT), name=name)(a, *[x for b in arrs for x in b])


def _small_pars(b_fg, a_log, dt_bias):
    par = jnp.zeros((8, 128), f32)
    par = par.at[0, 0:8].set(b_fg).at[1, 8:12].set(a_log).at[2, 8:12].set(dt_bias)
    return par


def _small_prep(zs, par, name):
    S = zs.shape[0]
    ts = min(S, 512)

    def body(z_ref, par_ref, o_ref, carry_ref):
        i = pl.program_id(0)

        @pl.when(i == 0)
        def _():
            carry_ref[...] = jnp.zeros_like(carry_ref)

        z = z_ref[...]
        lane = lax.broadcasted_iota(jnp.int32, (ts, 128), 1)
        row = lax.broadcasted_iota(jnp.int32, (ts, 128), 0)
        za = z + par_ref[0:1, :]
        logf = jnp.minimum(za, 0.0) - jnp.log1p(jnp.exp(-jnp.abs(za)))
        glog = -jnp.exp(par_ref[1:2, :]) * _softplus(z + par_ref[2:3, :])
        x = jnp.where(lane < 8, logf, jnp.where(lane < 12, glog, 0.0))
        pos = jnp.where(lane < 8, row, row & (CH - 1))
        s = 1
        while s < ts:
            x = x + jnp.where(pos >= s, pltpu.roll(x, s, 0), 0.0)
            s *= 2
        tot = x + carry_ref[0:1, :]
        carry_ref[...] = jnp.broadcast_to(jnp.where(lane[0:1] < 8, tot[ts - 1:ts, :], 0.0), (8, 128))
        o_ref[...] = jnp.where(lane < 8, tot, jnp.where(lane < 12, x, jnp.where(lane < 16, _sig(z), 0.0)))

    return pl.pallas_call(
        body, grid=(S // ts,), out_shape=jax.ShapeDtypeStruct((S, 128), f32),
        in_specs=[pl.BlockSpec((ts, 128), lambda i: (i, 0)), pl.BlockSpec((8, 128), lambda i: (0, 0))],
        out_specs=pl.BlockSpec((ts, 128), lambda i: (i, 0)), scratch_shapes=[pltpu.VMEM((8, 128), f32)],
        compiler_params=_cp(("arbitrary",)), name=name)(zs, par)


def _small_bwd(zs, par, dfr, dfc, dsm, name):
    S = zs.shape[0]
    ts = min(S, 512)
    nt = S // ts

    def body(z_ref, par_ref, dfr_ref, dfc_ref, dsm_ref, dz_ref, acc_ref, carry_ref):
        i = pl.program_id(0)

        @pl.when(i == 0)
        def _():
            carry_ref[...] = jnp.zeros_like(carry_ref)

        z = z_ref[...]
        dsm_v = dsm_ref[...]
        lane = lax.broadcasted_iota(jnp.int32, (ts, 128), 1)
        row = lax.broadcasted_iota(jnp.int32, (ts, 128), 0)
        df = jnp.transpose(jnp.concatenate([dfr_ref[...], jnp.zeros((120, ts), f32)], axis=0))
        for p in range(4):
            dpair = dfc_ref[p]
            df = df - jnp.where(lane == 2 * p, dpair[:, 0:1], jnp.where(lane == 2 * p + 1, dpair[:, 64:65], 0.0))
        x = jnp.where(lane < 8, df, jnp.where(lane < 12, dsm_v, 0.0))
        pos = jnp.where(lane < 8, row, row & (CH - 1))
        seg = jnp.where(lane < 8, ts, CH)
        s = 1
        while s < ts:
            x = x + jnp.where(pos + s < seg, pltpu.roll(x, ts - s, 0), 0.0)
            s *= 2
        tot = x + carry_ref[0:1, :]
        carry_ref[...] = jnp.broadcast_to(jnp.where(lane[0:1] < 8, tot[0:1, :], 0.0), (8, 128))
        za = z + par_ref[0:1, :]
        daf = tot * _sig(-za)
        zb = z + par_ref[2:3, :]
        nea = -jnp.exp(par_ref[1:2, :])
        glog = nea * _softplus(zb)
        dba = x * nea * _sig(zb)
        beta = _sig(z)
        dbb = dsm_v * beta * (1.0 - beta)
        dz_ref[:, 0:128] = _b(jnp.where(lane < 8, daf, jnp.where(lane < 12, dba, jnp.where(lane < 16, dbb, 0.0))))
        dz_ref[:, 128:SMALL_COLS] = jnp.zeros((ts, SMALL_COLS - 128), bf16)
        r0 = _colsum(jnp.where(lane < 8, daf, 0.0))
        r1 = _colsum(jnp.where((lane >= 8) & (lane < 12), x * glog, 0.0))
        r2 = _colsum(jnp.where((lane >= 8) & (lane < 12), dba, 0.0))
        r8 = lax.broadcasted_iota(jnp.int32, (8, 128), 0)
        part = jnp.where(r8 == 0, r0, jnp.where(r8 == 1, r1, jnp.where(r8 == 2, r2, 0.0)))

        @pl.when(i == 0)
        def _():
            acc_ref[...] = part

        @pl.when(i > 0)
        def _():
            acc_ref[...] += part

    rev = pl.BlockSpec((ts, 128), lambda i: (nt - 1 - i, 0))
    rev4 = pl.BlockSpec((4, ts, 128), lambda i: (0, nt - 1 - i, 0))
    c8 = pl.BlockSpec((8, 128), lambda i: (0, 0))
    return pl.pallas_call(
        body, grid=(nt,), out_shape=(jax.ShapeDtypeStruct((S, SMALL_COLS), bf16), jax.ShapeDtypeStruct((8, 128), f32)),
        in_specs=[rev, c8, pl.BlockSpec((8, ts), lambda i: (0, nt - 1 - i)), rev4, rev],
        out_specs=(pl.BlockSpec((ts, SMALL_COLS), lambda i: (nt - 1 - i, 0)), c8),
        scratch_shapes=[pltpu.VMEM((8, 128), f32)],
        compiler_params=_cp(("arbitrary",)), name=name)(zs, par, dfr, dfc, dsm)


def _split3(x):
    hi = _b(x).astype(f32)
    r = x - hi
    mid = _b(r).astype(f32)
    return hi, mid, _b(r - mid).astype(f32)


FOX_PREP_ROWS = 512
FOX_TILE = 512
FOX_SKIP_LOG = -32.0
FOX_NORM_INFLATE = 1.004


def _fox_prep(zb, sm, name):
    S = zb.shape[0]
    ts = min(S, FOX_PREP_ROWS)

    def body(q_ref, k_ref, f_ref, qa_ref, ka_ref, st_ref):
        lane = lax.broadcasted_iota(jnp.int32, (ts, 128), 1)
        lane8 = lax.broadcasted_iota(jnp.int32, (8, 128), 1)
        f = f_ref[...]
        st = jnp.zeros((8, 128), f32)
        r128 = lax.broadcasted_iota(jnp.int32, (128, 128), 0)
        c128 = lax.broadcasted_iota(jnp.int32, (128, 128), 1)
        same_head = _b(jnp.where((r128 < 64) == (c128 < 64), 1.0, 0.0))
        for p in range(4):
            q = q_ref[:, p * 128:(p + 1) * 128].astype(f32) * FOX_SCALE
            k = k_ref[:, p * 128:(p + 1) * 128].astype(f32)
            nq2 = jnp.sqrt(_dot(_b(q * q), same_head)) * FOX_NORM_INFLATE
            nk2 = jnp.sqrt(_dot(_b(k * k), same_head)) * FOX_NORM_INFLATE
            for h in (0, 1):
                fcol = f[:, 2 * p + h:2 * p + h + 1]
                hi, mid, lo = _split3(fcol)
                own = (lane < 64) if h == 0 else (lane >= 64)
                nq = nq2[:, 64 * h:64 * h + 1]
                nk = nk2[:, 64 * h:64 * h + 1]
                stats = (jnp.max(nq, axis=0, keepdims=True), jnp.max(nk, axis=0, keepdims=True),
                         jnp.max(fcol, axis=0, keepdims=True), jnp.min(fcol, axis=0, keepdims=True),
                         jnp.min(-nq * nk, axis=0, keepdims=True))
                for si, val in enumerate(stats):
                    st = jnp.where(lane8 == 8 * si + 2 * p + h, val, st)
                o = 64 if h == 0 else 0
                ones_lo = (lane >= o) & (lane < o + 3)
                ones_hi = (lane >= o + 3) & (lane < o + 6)
                qaug = jnp.where(lane == o, hi, jnp.where(lane == o + 1, mid, jnp.where(lane == o + 2, lo, jnp.where(ones_hi, 1.0, 0.0))))
                kaug = jnp.where(lane == o + 3, -hi, jnp.where(lane == o + 4, -mid, jnp.where(lane == o + 5, -lo, jnp.where(ones_lo, 1.0, 0.0))))
                qa_ref[2 * p + h] = _b(jnp.where(own, q, qaug))
                ka_ref[2 * p + h] = _b(jnp.where(own, k, kaug))
        st_ref[...] = st

    out = jax.ShapeDtypeStruct((8, S, 128), bf16)
    return pl.pallas_call(
        body, grid=(S // ts,), out_shape=(out, out, jax.ShapeDtypeStruct((S // ts * 8, 128), f32)),
        in_specs=[pl.BlockSpec((ts, 512), lambda i: (i, 0)), pl.BlockSpec((ts, 512), lambda i: (i, 1)), pl.BlockSpec((ts, 128), lambda i: (i, 0))],
        out_specs=(pl.BlockSpec((8, ts, 128), lambda i: (0, i, 0)), pl.BlockSpec((8, ts, 128), lambda i: (0, i, 0)),
                   pl.BlockSpec((8, 128), lambda i: (i, 0))),
        compiler_params=_cp(("parallel",)), name=name)(zb, zb, sm)


def _fox_bound_table(st, S, T, lse_rows=None):
    ts = min(S, FOX_PREP_ROWS)
    g = T // ts
    nt = S // T
    s5 = st.reshape(S // ts, 8, 128)[:, 0, 0:40].reshape(nt, g, 5, 8)
    qn, kn, fmax = s5[:, :, 0].max(axis=1), s5[:, :, 1].max(axis=1), s5[:, :, 2].max(axis=1)
    fmin, lmin = s5[:, :, 3].min(axis=1), s5[:, :, 4].min(axis=1)
    if lse_rows is not None:
        lmin = jnp.transpose(lse_rows.reshape(8, nt, T).min(axis=2))
    e = qn[:, None] * kn[None, :] + fmax[:, None] - fmin[None, :] - lmin[:, None] + 1.0
    return jnp.transpose(e, (2, 0, 1)).reshape(8, nt * nt)


def _pair_rows(a, T):
    at = jnp.transpose(a)
    r8 = lax.broadcasted_iota(jnp.int32, (8, T), 0)
    return jnp.where(r8 == 0, at[0:1, :], at[64:65, :])


def _fox_fwd(qa, ka, zb, tab, name):
    S = zb.shape[0]
    T = min(S, FOX_TILE)
    nt = S // T

    def body(tab_ref, qa_ref, ka_ref, v_ref, o_ref, lset_ref, m_ref, l_ref, acc_ref):
        p, i = pl.program_id(0), pl.program_id(1)
        m_ref[...] = jnp.full_like(m_ref, NEG)
        l_ref[...] = jnp.zeros_like(l_ref)
        acc_ref[...] = jnp.zeros_like(acc_ref)
        row = lax.broadcasted_iota(jnp.int32, (T, T), 0)
        col = lax.broadcasted_iota(jnp.int32, (T, T), 1)

        def head_tile(h, j, masked):
            off = pl.multiple_of(j * T, T)
            s = _dot_nt(qa_ref[h], ka_ref[h, pl.ds(off, T), :])
            if masked:
                s = jnp.where(row >= col, s, NEG)
            m_old = m_ref[h]
            m_new = jnp.maximum(m_old, jnp.max(s, axis=1, keepdims=True))
            alpha = jnp.exp(m_old - m_new)
            pr = jnp.exp(s - jnp.tile(m_new, (1, T // 128)))
            l_ref[h] = alpha * l_ref[h] + _rowsum(pr)
            acc_ref[h] = alpha * acc_ref[h] + _dot(_b(pr), v_ref[pl.ds(off, T), :])
            m_ref[h] = m_new

        def step(j, c):
            for h in (0, 1):
                @pl.when(tab_ref[2 * p + h, i * nt + j] > FOX_SKIP_LOG)
                def _():
                    head_tile(h, j, False)
            return c

        lax.fori_loop(0, i, step, 0)
        for h in (0, 1):
            head_tile(h, i, True)
        lane2 = lax.broadcasted_iota(jnp.int32, (T, 128), 1)
        o_ref[...] = jnp.where(lane2 < 64, acc_ref[0] / l_ref[0], acc_ref[1] / l_ref[1])
        lse = jnp.where(lane2 < 64, m_ref[0] + jnp.log(l_ref[0]), m_ref[1] + jnp.log(l_ref[1]))
        lset_ref[0] = _pair_rows(lse, T)

    return pl.pallas_call(
        body, grid=(4, S // T),
        out_shape=(jax.ShapeDtypeStruct((S, 512), f32), jax.ShapeDtypeStruct((4, 8, S), f32)),
        in_specs=[pl.BlockSpec(memory_space=pltpu.SMEM), pl.BlockSpec((2, T, 128), lambda p, i: (p, i, 0)),
                  pl.BlockSpec((2, S, 128), lambda p, i: (p, 0, 0), pipeline_mode=pl.Buffered(1)),
                  pl.BlockSpec((S, 128), lambda p, i: (0, 8 + p), pipeline_mode=pl.Buffered(1))],
        out_specs=(pl.BlockSpec((T, 128), lambda p, i: (i, p)), pl.BlockSpec((1, 8, T), lambda p, i: (p, 0, i))),
        scratch_shapes=[pltpu.VMEM((2, T, 128), f32), pltpu.VMEM((2, T, 128), f32), pltpu.VMEM((2, T, 128), f32)],
        compiler_params=_cp(("arbitrary", "arbitrary"), VMEM_LIMIT), name=name)(tab, qa, ka, zb)


def _fox_bwd(qa, ka, zb, dob, lse_t, dl_t, tab, name, comm_sends=()):
    S = zb.shape[0]
    T = min(S, FOX_TILE)
    nq = S // T
    nc = len(comm_sends)

    def body(*refs):
        tab_ref, ka_ref, v_ref, qa_ref, do_ref, lt_ref, dt_ref = refs[:7]
        cs_refs = refs[7:7 + nc]
        dq_ref, dk_ref, dv_ref, dfc_ref, dfr_ref = refs[7 + nc:12 + nc]
        cr_refs = refs[12 + nc:12 + 2 * nc]
        dqa_ref, dka_ref, dva_ref, fs_ref = refs[12 + 2 * nc:16 + 2 * nc]
        p, j = pl.program_id(0), pl.program_id(1)
        if nc:
            @pl.when((p == 0) & (j == 0))
            def _():
                for cp in _exchange_copies(cs_refs, cr_refs, *refs[16 + 2 * nc:]):
                    cp.start()
        lane1 = lax.broadcasted_iota(jnp.int32, (1, 128), 1)
        lane2 = lax.broadcasted_iota(jnp.int32, (T, 128), 1)
        hm = (lane1 < 64, lane1 >= 64)
        v = v_ref[...]
        vsm = [jnp.where(hm[h], v, jnp.zeros_like(v)) for h in (0, 1)]
        ksm = [jnp.where(hm[h], ka_ref[h], jnp.zeros_like(v)) for h in (0, 1)]

        @pl.when(j == 0)
        def _():
            dqa_ref[...] = jnp.zeros_like(dqa_ref)
            dfr_ref[...] = jnp.zeros_like(dfr_ref)

        dka_ref[...] = jnp.zeros_like(dka_ref)
        dva_ref[...] = jnp.zeros_like(dva_ref)
        fs_ref[...] = jnp.zeros_like(fs_ref)
        row = lax.broadcasted_iota(jnp.int32, (T, T), 0)
        col = lax.broadcasted_iota(jnp.int32, (T, T), 1)

        def head_tile(h, i, masked):
            off = pl.multiple_of(i * T, T)
            dot_ = do_ref[pl.ds(off, T), :]
            hr = pl.ds(2 * p + h, 1)
            qt = qa_ref[h, pl.ds(off, T), :]
            s_t = _dot_nt(ka_ref[h], qt)
            if masked:
                s_t = jnp.where(col >= row, s_t, NEG)
            p_t = jnp.exp(s_t - lt_ref[hr, pl.ds(off, T)])
            dva_ref[h] += _dot(_b(p_t), dot_)
            dp_t = _dot_nt(vsm[h], dot_)
            ds_t = p_t * (dp_t - dt_ref[hr, pl.ds(off, T)])
            dsb = _b(ds_t)
            dka_ref[h] += _dot(dsb, qt)
            fs_ref[h] += _rowsum(ds_t)
            dfr_ref[0, pl.ds(h, 1), pl.ds(off, T)] += _colsum(ds_t)
            dqa_ref[pl.ds(off, T), :] += _dot_tn(dsb, ksm[h])

        def step(i, c):
            for h in (0, 1):
                @pl.when(tab_ref[2 * p + h, i * nq + j] > FOX_SKIP_LOG)
                def _():
                    head_tile(h, i, False)
            return c

        for h in (0, 1):
            head_tile(h, j, True)
        lax.fori_loop(j + 1, nq, step, 0)
        dk_ref[...] = _b(jnp.where(lane2 < 64, dka_ref[0], dka_ref[1]))
        dv_ref[...] = _b(jnp.where(lane2 < 64, dva_ref[0], dva_ref[1]))
        dfc_ref[0] = jnp.where(lane2 < 64, fs_ref[0], fs_ref[1])
        dq_ref[...] = _b(dqa_ref[pl.ds(pl.multiple_of(j * T, T), T), :] * FOX_SCALE)
        if nc:
            @pl.when((p == 3) & (j == nq - 1))
            def _():
                for cp in _exchange_copies(cs_refs, cr_refs, *refs[16 + 2 * nc:]):
                    cp.wait()

    one = pl.Buffered(1)
    res = pl.BlockSpec((8, S), lambda p, j: (0, 0), pipeline_mode=one)
    tk = pl.BlockSpec((T, 128), lambda p, j: (j, p))
    anyspec = pl.BlockSpec(memory_space=pl.ANY)
    outs = pl.pallas_call(
        body, grid=(4, nq),
        out_shape=(jax.ShapeDtypeStruct((S, 512), bf16), jax.ShapeDtypeStruct((S, 512), bf16), jax.ShapeDtypeStruct((S, 512), bf16),
                   jax.ShapeDtypeStruct((4, S, 128), f32), jax.ShapeDtypeStruct((4, 8, S), f32))
        + tuple(jax.ShapeDtypeStruct(c.shape, c.dtype) for c in comm_sends),
        in_specs=[pl.BlockSpec(memory_space=pltpu.SMEM),
                  pl.BlockSpec((2, T, 128), lambda p, j: (p, j, 0)), pl.BlockSpec((T, 128), lambda p, j: (j, 8 + p)),
                  pl.BlockSpec((2, S, 128), lambda p, j: (p, 0, 0), pipeline_mode=one),
                  pl.BlockSpec((S, 128), lambda p, j: (0, p), pipeline_mode=one), res, res] + [anyspec] * nc,
        out_specs=(tk, tk, tk, pl.BlockSpec((1, T, 128), lambda p, j: (p, j, 0)),
                   pl.BlockSpec((1, 8, S), lambda p, j: (p, 0, 0))) + tuple([anyspec] * nc),
        scratch_shapes=[pltpu.VMEM((S, 128), f32), pltpu.VMEM((2, T, 128), f32), pltpu.VMEM((2, T, 128), f32), pltpu.VMEM((2, T, 1), f32)]
        + (_exchange_scratch(nc) if nc else []),
        compiler_params=_cp(("arbitrary", "arbitrary"), VMEM_LIMIT), name=name)(tab, ka, zb, qa, dob, lse_t, dl_t, *comm_sends)
    return outs[:5], list(outs[5:])


def _head_rows(a):
    return a[:, 0:2, :].reshape(8, a.shape[2])


def _conv_taps(ext, x, w_ref, ts):
    y = x * w_ref[3:4, :]
    shifted = []
    for k in (1, 2, 3):
        xs = pltpu.roll(ext, k, 0)[8:]
        shifted.append(xs)
        y = y + xs * w_ref[3 - k:4 - k, :]
    return y, shifted


def _gdn_prep(zf, cw, name):
    S = zf.shape[0]
    ts = min(S, 512)

    def body(x_ref, w_ref, o_ref, tail_ref):
        i = pl.program_id(0)

        @pl.when(i == 0)
        def _():
            tail_ref[...] = jnp.zeros_like(tail_ref)

        x = x_ref[...]
        ext = jnp.concatenate([tail_ref[...], x], axis=0)
        y, _ = _conv_taps(ext, x, w_ref, ts)
        tail_ref[...] = x[ts - 8:, :]
        a = _silu(y)
        for hb in range(12):
            blk = a[:, hb * 128:(hb + 1) * 128]
            if hb < 8:
                blk = blk * lax.rsqrt(_rowsum(blk * blk) + EPS)
            if hb < 4:
                blk = blk * GDN_SCALE
            o_ref[:, hb * 128:(hb + 1) * 128] = blk

    return pl.pallas_call(
        body, grid=(S // ts,), out_shape=jax.ShapeDtypeStruct((S, 1536), f32),
        in_specs=[pl.BlockSpec((ts, 1536), lambda i: (i, 0)), pl.BlockSpec((8, 1536), lambda i: (0, 0))],
        out_specs=pl.BlockSpec((ts, 1536), lambda i: (i, 0)), scratch_shapes=[pltpu.VMEM((8, 1536), f32)],
        compiler_params=_cp(("arbitrary",), VMEM_LIMIT), name=name)(zf, cw)


def _gdn_prep_bwd(zf, cw, dg, name):
    S = zf.shape[0]
    ts = min(S, 512)
    nt = S // ts

    def body(x_ref, xp_ref, w_ref, dg_ref, dx_ref, dw_ref, head_ref):
        i = pl.program_id(0)

        @pl.when(i == 0)
        def _():
            head_ref[...] = jnp.zeros_like(head_ref)

        x = x_ref[...]
        prev = jnp.where(i == nt - 1, 0.0, xp_ref[...])
        ext = jnp.concatenate([prev, x], axis=0)
        y, shifted = _conv_taps(ext, x, w_ref, ts)
        a = _silu(y)
        das = []
        for hb in range(12):
            blk = a[:, hb * 128:(hb + 1) * 128]
            d = dg_ref[:, hb * 128:(hb + 1) * 128]
            if hb < 4:
                d = d * GDN_SCALE
            if hb < 8:
                r = lax.rsqrt(_rowsum(blk * blk) + EPS)
                n = blk * r
                d = r * (d - n * _rowsum(d * n))
            das.append(d)
        dy = jnp.concatenate(das, axis=1) * _dsilu(y)
        extd = jnp.concatenate([dy, head_ref[...]], axis=0)
        dx = dy * w_ref[3:4, :]
        for k in (1, 2, 3):
            dx = dx + pltpu.roll(extd, ts + 8 - k, 0)[:ts] * w_ref[3 - k:4 - k, :]
        head_ref[...] = dy[0:8, :]
        dx_ref[...] = _b(dx)
        r8 = lax.broadcasted_iota(jnp.int32, (8, 1536), 0)
        part = jnp.where(r8 == 3, _colsum(dy * x), 0.0)
        for k in (1, 2, 3):
            part = jnp.where(r8 == 3 - k, _colsum(dy * shifted[k - 1]), part)

        @pl.when(i == 0)
        def _():
            dw_ref[...] = part

        @pl.when(i > 0)
        def _():
            dw_ref[...] += part

    rev = pl.BlockSpec((ts, 1536), lambda i: (nt - 1 - i, 0))
    prev8 = pl.BlockSpec((8, 1536), lambda i: (jnp.maximum((nt - 1 - i) * (ts // 8) - 1, 0), 0))
    w8 = pl.BlockSpec((8, 1536), lambda i: (0, 0))
    return pl.pallas_call(
        body, grid=(nt,), out_shape=(jax.ShapeDtypeStruct((S, 1536), bf16), jax.ShapeDtypeStruct((8, 1536), f32)),
        in_specs=[rev, prev8, w8, rev], out_specs=(rev, w8), scratch_shapes=[pltpu.VMEM((8, 1536), f32)],
        compiler_params=_cp(("arbitrary",), VMEM_LIMIT), name=name)(zf, zf, cw, dg)


def _tri_inv(a, row, col):
    same = (row >> 4) == (col >> 4)
    dm = jnp.where(same, a, 0.0)
    lo = a - dm
    eye = jnp.where(row == col, 1.0, 0.0)
    d2 = _hi_b(dm, dm)
    d4 = _hi_b(d2, d2)
    d8 = _hi_b(d4, d4)
    x0 = _hi_b(_hi_b(eye - dm, eye + d2), _hi_b(eye + d4, eye + d8))
    n = _hi_b(x0, lo)
    n2 = _hi_b(n, n)
    return _hi_b(_hi_b(eye - n, eye + n2), x0)


def _bd(a, b):
    return lax.dot_general(a, b, (((2,), (1,)), ((0,), (0,))), preferred_element_type=f32)


def _bd_nt(a, b):
    return lax.dot_general(a, b, (((2,), (2,)), ((0,), (0,))), preferred_element_type=f32)


def _bd_tn(a, b):
    return lax.dot_general(a, b, (((1,), (1,)), ((0,), (0,))), preferred_element_type=f32)


def _hi_b_nt(a, b):
    return _mm3(a, b, (((2,), (2,)), ((0,), (0,))))


def _hi_b_tn(a, b):
    return _mm3(a, b, (((1,), (1,)), ((0,), (0,))))


def _gdn_local(x_ref, sm_ref, gt_ref, row, col, cps=1):
    idx = [(c, h) for c in range(cps) for h in range(4)]

    def rows(c):
        return slice(c * CH, (c + 1) * CH)

    q = jnp.stack([x_ref[rows(c), h * 128:(h + 1) * 128] for c, h in idx])
    k = jnp.stack([x_ref[rows(c), 512 + h * 128:512 + (h + 1) * 128] for c, h in idx])
    v = jnp.stack([x_ref[rows(c), 1024 + h * 128:1024 + (h + 1) * 128] for c, h in idx])
    gc = jnp.stack([sm_ref[rows(c), 8 + h:9 + h] for c, h in idx])
    beta = jnp.stack([sm_ref[rows(c), 12 + h:13 + h] for c, h in idx])
    gr = jnp.stack([gt_ref[h, c] for c, h in idx])
    eg = jnp.exp(gc)
    gl = gc[:, CH - 1:CH, :]
    dec = jnp.exp(gl - gc)
    gm = gc - gr
    gam_i = jnp.exp(jnp.where(row >= col, gm, -jnp.inf))
    gam_s = jnp.where(row > col, gam_i, 0.0)
    kb = k * beta
    return dict(q=q, k=k, v=v, beta=beta, eg=eg, egl=jnp.exp(gl), dec=dec, gam_i=gam_i, gam_s=gam_s,
                kb=kb, vb=v * beta, kbg=kb * eg, qdec=q * eg, kdec=k * dec,
                a=_bd_nt(_b(kb), _b(k)) * gam_s, aqk=_bd_nt(_b(q), _b(k)) * gam_i)


GDN_FWD_CHUNKS = 8


def _gdn_fwd(gqkv, sm, gt4, name, cps=GDN_FWD_CHUNKS):
    S = gqkv.shape[0]
    N = S // CH
    cps = min(cps, N)
    R = cps * CH

    def body(x_ref, sm_ref, gt_ref, o_ref, t_ref, st_ref, s_ref):
        n = pl.program_id(0)

        @pl.when(n == 0)
        def _():
            s_ref[...] = jnp.zeros_like(s_ref)

        row = lax.broadcasted_iota(jnp.int32, (CH, CH), 0)
        col = lax.broadcasted_iota(jnp.int32, (CH, CH), 1)
        c = _gdn_local(x_ref, sm_ref, gt_ref, row, col, cps)
        t = _tri_inv(c["a"], row, col)
        uw = _hi_b(t, jnp.concatenate([c["vb"], c["kbg"]], axis=2))
        u, w = uw[:, :, :128], uw[:, :, 128:]
        for ci in range(cps):
            sl = slice(4 * ci, 4 * ci + 4)
            rs = slice(ci * CH, (ci + 1) * CH)
            st = s_ref[...]
            st_ref[ci] = st
            sb = _b(st)
            vnew = u[sl] - _bd(_b(w[sl]), sb)
            o = _bd(_b(c["qdec"][sl]), sb) + _bd(_b(c["aqk"][sl]), _b(vnew))
            for h in range(4):
                o_ref[rs, h * 128:(h + 1) * 128] = o[h]
                t_ref[h, rs, :] = t[4 * ci + h]
            s_ref[...] = st * c["egl"][sl] + _bd_tn(_b(c["kdec"][sl]), _b(vnew))

    return pl.pallas_call(
        body, grid=(N // cps,),
        out_shape=(jax.ShapeDtypeStruct((S, 512), f32), jax.ShapeDtypeStruct((4, S, CH), f32), jax.ShapeDtypeStruct((N, 4, 128, 128), f32)),
        in_specs=[pl.BlockSpec((R, 1536), lambda n: (n, 0)), pl.BlockSpec((R, 128), lambda n: (n, 0)),
                  pl.BlockSpec((4, cps, 1, CH), lambda n: (0, n, 0, 0))],
        out_specs=(pl.BlockSpec((R, 512), lambda n: (n, 0)), pl.BlockSpec((4, R, CH), lambda n: (0, n, 0)),
                   pl.BlockSpec((cps, 4, 128, 128), lambda n: (n, 0, 0, 0))),
        scratch_shapes=[pltpu.VMEM((4, 128, 128), f32)], compiler_params=_cp(("arbitrary",)), name=name)(gqkv, sm, gt4)


GDN_BWD_CHUNKS = 4


def _gdn_bwd(gqkv, sm, gt4, tinv, states, do, name, cps=GDN_BWD_CHUNKS):
    S = gqkv.shape[0]
    N = S // CH
    cps = min(cps, N)
    R = cps * CH

    def body(x_ref, sm_ref, gt_ref, t_ref, st_ref, do_ref, dx_ref, dsm_ref, ds_ref):
        n = pl.program_id(0)

        @pl.when(n == 0)
        def _():
            ds_ref[...] = jnp.zeros_like(ds_ref)

        row = lax.broadcasted_iota(jnp.int32, (CH, CH), 0)
        col = lax.broadcasted_iota(jnp.int32, (CH, CH), 1)
        row1 = lax.broadcasted_iota(jnp.int32, (CH, 1), 0)
        lane = lax.broadcasted_iota(jnp.int32, (CH, 128), 1)
        ones = jnp.ones((4 * cps, CH, 128), f32)
        idx = [(ci, h) for ci in range(cps) for h in range(4)]
        c = _gdn_local(x_ref, sm_ref, gt_ref, row, col, cps)
        q, k, v, beta, eg = c["q"], c["k"], c["v"], c["beta"], c["eg"]
        t = jnp.stack([t_ref[h, ci * CH:(ci + 1) * CH, :] for ci, h in idx])
        uw = _hi_b(t, jnp.concatenate([c["vb"], c["kbg"]], axis=2))
        u, w = uw[:, :, :128], uw[:, :, 128:]
        st = st_ref[...].reshape(4 * cps, 128, 128)
        sb = _b(st)
        vnew = u - _bd(_b(w), sb)
        dob = _b(jnp.stack([do_ref[ci * CH:(ci + 1) * CH, h * 128:(h + 1) * 128] for ci, h in idx]))
        vnb = _b(vnew)
        dqdec = _bd_nt(dob, sb)
        daqk = jnp.where(row >= col, _bd_nt(dob, vnb), 0.0)
        qd_do = _bd_tn(_b(c["qdec"]), dob)
        aqk_do = _bd_tn(_b(c["aqk"]), dob)
        kdecb, wb = _b(c["kdec"]), _b(w)
        dvnew_l, dkdec_l, dgl_l = [None] * cps, [None] * cps, [None] * cps
        for ci in reversed(range(cps)):
            sl = slice(4 * ci, 4 * ci + 4)
            dsp = ds_ref[...]
            dspb = _b(dsp)
            dvn = _bd(kdecb[sl], dspb) + aqk_do[sl]
            dvnew_l[ci] = dvn
            dkdec_l[ci] = _bd_nt(vnb[sl], dspb)
            dgl_l[ci] = c["egl"][sl] * jnp.sum(dsp * st[sl], axis=(1, 2), keepdims=True)
            ds_ref[...] = dsp * c["egl"][sl] + qd_do[sl] - _bd_tn(wb[sl], _b(dvn))
        dvnew = jnp.concatenate(dvnew_l, axis=0)
        dkdec = jnp.concatenate(dkdec_l, axis=0)
        dgl = jnp.concatenate(dgl_l, axis=0)
        dw = -_bd_nt(_b(dvnew), sb)
        duw = _hi_b_tn(t, jnp.concatenate([dvnew, dw], axis=2))
        dvb, dkbg = duw[:, :, :128], duw[:, :, 128:]
        da = -jnp.where(row > col, _hi_b_nt(duw, uw), 0.0)
        dp = da * c["gam_s"]
        dqk = daqk * c["gam_i"]
        m = da * c["a"] + daqk * c["aqk"]
        csum = _hi_b_tn(m, ones)[:, :, 0:1]
        kk = dkdec * c["kdec"]

        def lsum(a):
            return jnp.sum(a, axis=2, keepdims=True)

        dgv = lsum(m) - csum + lsum(dqdec * c["qdec"]) - lsum(kk) + lsum(dkbg * c["kbg"])
        dgv = dgv + jnp.where(row1 == CH - 1, dgl + jnp.sum(kk, axis=(1, 2), keepdims=True), 0.0)
        dpb, dqkb = _b(dp), _b(dqk)
        dkb = _bd(dpb, _b(k)) + dkbg * eg
        dk = _bd_tn(dpb, _b(c["kb"])) + _bd_tn(dqkb, _b(q)) + dkdec * c["dec"] + dkb * beta
        dq = _bd(dqkb, _b(k)) + dqdec * eg
        dbeta = lsum(dkb * k) + lsum(dvb * v)
        dv = dvb * beta
        for ci in range(cps):
            rs = slice(ci * CH, (ci + 1) * CH)
            dsm = jnp.zeros((CH, 128), f32)
            for h in range(4):
                b = 4 * ci + h
                dx_ref[rs, h * 128:(h + 1) * 128] = dq[b]
                dx_ref[rs, 512 + h * 128:512 + (h + 1) * 128] = dk[b]
                dx_ref[rs, 1024 + h * 128:1024 + (h + 1) * 128] = dv[b]
                dsm = jnp.where(lane == 8 + h, dgv[b], jnp.where(lane == 12 + h, dbeta[b], dsm))
            dsm_ref[rs, :] = dsm

    G = N // cps
    return pl.pallas_call(
        body, grid=(G,), out_shape=(jax.ShapeDtypeStruct((S, 1536), f32), jax.ShapeDtypeStruct((S, 128), f32)),
        in_specs=[pl.BlockSpec((R, 1536), lambda n: (G - 1 - n, 0)), pl.BlockSpec((R, 128), lambda n: (G - 1 - n, 0)),
                  pl.BlockSpec((4, cps, 1, CH), lambda n: (0, G - 1 - n, 0, 0)), pl.BlockSpec((4, R, CH), lambda n: (0, G - 1 - n, 0)),
                  pl.BlockSpec((cps, 4, 128, 128), lambda n: (G - 1 - n, 0, 0, 0)), pl.BlockSpec((R, 512), lambda n: (G - 1 - n, 0))],
        out_specs=(pl.BlockSpec((R, 1536), lambda n: (G - 1 - n, 0)), pl.BlockSpec((R, 128), lambda n: (G - 1 - n, 0))),
        scratch_shapes=[pltpu.VMEM((4, 128, 128), f32)], compiler_params=_cp(("arbitrary",), VMEM_LIMIT), name=name)(gqkv, sm, gt4, tinv, states, do)


MERGE_ROWS = 256
MERGE_FWD_ROWS = 512


def _mem_attn(q, kv_ref, h):
    s = _dot_nt(q, kv_ref[:, h * 128:(h + 1) * 128]) * MEM_SCALE
    e = jnp.exp(s - jnp.max(s, axis=1, keepdims=True))
    return e / _rowsum(e)


def _gdn_out_norm(ob):
    r = lax.rsqrt(jnp.mean(ob * ob, axis=-1, keepdims=True) + EPS)
    return ob * r, r


def _merge_fwd(x, oa, ob, zb, zf, kv, b_merge, gdn_g, w_branch, w_out, name):
    S = x.shape[0]
    ts = min(S, MERGE_FWD_ROWS)

    def body(x_ref, oa_ref, ob_ref, mq_ref, az_ref, bz_ref, mz_ref, gt_ref, kv_ref, bm_ref, gg_ref, wb_ref, wo_ref,
             xo_ref, y_ref, mg_ref):
        y_ref[:, 0:512] = _b(oa_ref[...] * _silu(az_ref[...]))
        for h in range(4):
            sl = slice(h * 128, (h + 1) * 128)
            nb, _ = _gdn_out_norm(ob_ref[:, sl])
            y_ref[:, 512 + h * 128:512 + (h + 1) * 128] = _b(nb * gg_ref[...] * _silu(bz_ref[:, sl]))
            pm = _mem_attn(mq_ref[:, sl], kv_ref, h)
            om = _dot(_b(pm), kv_ref[:, 512 + h * 128:512 + (h + 1) * 128])
            y_ref[:, 1024 + h * 128:1024 + (h + 1) * 128] = _b(om * _silu(mz_ref[:, sl]))
        merged = jnp.zeros((ts, D), f32)
        for n in range(3):
            gate = _sig(gt_ref[:, n * D:(n + 1) * D] + bm_ref[:, n * D:(n + 1) * D])
            merged = merged + gate * _dot(y_ref[:, n * 512:(n + 1) * 512], wb_ref[n])
        mb = _b(merged)
        mg_ref[...] = mb
        xo_ref[...] = x_ref[...] + _dot(mb, wo_ref[...])

    def col(w, c):
        return pl.BlockSpec((ts, w), lambda i: (i, c))

    def full(shape):
        return pl.BlockSpec(shape, lambda i: tuple(0 for _ in shape))

    return pl.pallas_call(
        body, grid=(S // ts,),
        out_shape=(jax.ShapeDtypeStruct((S, D), f32), jax.ShapeDtypeStruct((S, 1536), bf16), jax.ShapeDtypeStruct((S, D), bf16)),
        in_specs=[col(D, 0), col(512, 0), col(512, 0), col(512, 3), col(512, 3), col(512, 4), col(512, 5), col(3072, 1),
                  full((256, D)), full((1, 3072)), full((1, 128)), full((3, 512, D)), full((D, D))],
        out_specs=(col(D, 0), col(1536, 0), col(D, 0)),
        compiler_params=_cp(("parallel",), VMEM_LIMIT), name=name)(x, oa, ob, zb, zf, zf, zf, zf, kv, b_merge, gdn_g, w_branch, w_out)


def _merge_bwd(dout, ycat, oa, ob, zb, zf, kv, b_merge, gdn_g, w_branch, w_branch_t, w_out_t, name):
    S = dout.shape[0]
    ts = min(S, MERGE_ROWS)

    def body(do_ref, y_ref, oa_ref, ob_ref, mq_ref, az_ref, bz_ref, mz_ref, gt_ref, kv_ref, bm_ref, gg_ref, wb_ref, wbt_ref, wot_ref,
             dpj_ref, dz_ref, dmq_ref, dlt_ref, doab_ref, dob_ref, dkv_ref, dbm_ref, dgg_ref):
        i = pl.program_id(0)
        dmerged = _dot(_b(do_ref[...]), wot_ref[...])
        dys = []
        dbm_parts = []
        for n in range(3):
            cs = slice(n * D, (n + 1) * D)
            gate = _sig(gt_ref[:, cs] + bm_ref[:, cs])
            proj = _dot(y_ref[:, n * 512:(n + 1) * 512], wb_ref[n])
            dlogit = dmerged * proj * gate * (1.0 - gate)
            dz_ref[:, 1536 + n * D:1536 + (n + 1) * D] = _b(dlogit)
            dbm_parts.append(_colsum(dlogit))
            dproj = _b(dmerged * gate)
            dpj_ref[:, cs] = dproj
            dys.append(_dot(dproj, wbt_ref[n]))
        dbm = jnp.broadcast_to(jnp.concatenate(dbm_parts, axis=1), (8, 3072))
        az = az_ref[...]
        oa = oa_ref[...]
        doa = dys[0] * _silu(az)
        doab_ref[...] = _b(doa)
        prod = doa * oa
        lane = lax.broadcasted_iota(jnp.int32, (ts, 128), 1)
        dl = jnp.zeros((ts, 128), f32)
        for p in range(4):
            blk = prod[:, p * 128:(p + 1) * 128]
            dl = jnp.where(lane == 2 * p, _rowsum(jnp.where(lane < 64, blk, 0.0)),
                           jnp.where(lane == 2 * p + 1, _rowsum(jnp.where(lane >= 64, blk, 0.0)), dl))
        dlt_ref[...] = jnp.transpose(dl)[0:8, :]
        dz_ref[:, 0:512] = _b(dys[0] * oa * _dsilu(az))
        gg = gg_ref[...]
        dgg = jnp.zeros((1, 128), f32)
        dkv_parts_k, dkv_parts_v = [], []
        for h in range(4):
            sl = slice(h * 128, (h + 1) * 128)
            bz = bz_ref[:, sl]
            dyb = dys[1][:, sl]
            nb, r = _gdn_out_norm(ob_ref[:, sl])
            dz_ref[:, 512 + h * 128:512 + (h + 1) * 128] = _b(dyb * nb * gg * _dsilu(bz))
            dng = dyb * _silu(bz)
            dgg = dgg + _colsum(dng * nb)
            dnb = dng * gg
            dob_ref[:, sl] = r * (dnb - nb * jnp.mean(dnb * nb, axis=-1, keepdims=True))
            mz = mz_ref[:, sl]
            dym = dys[2][:, sl]
            q = mq_ref[:, sl]
            kh = kv_ref[:, sl]
            vh = kv_ref[:, 512 + h * 128:512 + (h + 1) * 128]
            pm = _mem_attn(q, kv_ref, h)
            pmb = _b(pm)
            om = _dot(pmb, vh)
            dz_ref[:, 1024 + h * 128:1024 + (h + 1) * 128] = _b(dym * om * _dsilu(mz))
            dom = _b(dym * _silu(mz))
            dkv_parts_v.append(_dot_tn(pmb, dom))
            dpm = _dot_nt(dom, vh)
            dsm = _b(pm * (dpm - _rowsum(dpm * pm)) * MEM_SCALE)
            dmq_ref[:, sl] = _b(_dot(dsm, kh))
            dkv_parts_k.append(_dot_tn(dsm, q))
        dkv = jnp.concatenate(dkv_parts_k + dkv_parts_v, axis=1)
        dggb = jnp.broadcast_to(dgg, (8, 128))

        @pl.when(i == 0)
        def _():
            dkv_ref[...] = dkv
            dbm_ref[...] = dbm
            dgg_ref[...] = dggb

        @pl.when(i > 0)
        def _():
            dkv_ref[...] += dkv
            dbm_ref[...] += dbm
            dgg_ref[...] += dggb

    def col(w, c):
        return pl.BlockSpec((ts, w), lambda i: (i, c))

    def full(shape):
        return pl.BlockSpec(shape, lambda i: tuple(0 for _ in shape))

    return pl.pallas_call(
        body, grid=(S // ts,),
        out_shape=(jax.ShapeDtypeStruct((S, 3072), bf16), jax.ShapeDtypeStruct((S, DZ_MERGE_COLS), bf16), jax.ShapeDtypeStruct((S, 512), bf16),
                   jax.ShapeDtypeStruct((8, S), f32), jax.ShapeDtypeStruct((S, 512), bf16), jax.ShapeDtypeStruct((S, 512), f32),
                   jax.ShapeDtypeStruct((256, D), f32), jax.ShapeDtypeStruct((8, 3072), f32), jax.ShapeDtypeStruct((8, 128), f32)),
        in_specs=[col(D, 0), col(1536, 0), col(512, 0), col(512, 0), col(512, 3), col(512, 3), col(512, 4), col(512, 5), col(3072, 1),
                  full((256, D)), full((1, 3072)), full((1, 128)), full((3, 512, D)), full((3, D, 512)), full((D, D))],
        out_specs=(col(3072, 0), col(4608, 0), col(512, 0), pl.BlockSpec((8, ts), lambda i: (0, i)), col(512, 0), col(512, 0),
                   full((256, D)), full((8, 3072)), full((8, 128))),
        compiler_params=_cp(("arbitrary",), VMEM_LIMIT), name=name)(
            dout, ycat, oa, ob, zb, zf, zf, zf, zf, kv, b_merge, gdn_g, w_branch, w_branch_t, w_out_t)


def _mesh_pos():
    return lax.axis_index("x"), lax.axis_index("y"), lax.axis_index("c")


class _Gather:
    def __init__(self, x_refs, out_refs, send_sems, recv_sems, local_sems):
        self.n = len(x_refs)
        self.x_refs, self.out_refs = x_refs, out_refs
        self.send_sems, self.recv_sems, self.local_sems = send_sems, recv_sems, local_sems
        mx, my, mc = _mesh_pos()
        self.mc = mc
        self.me, self.sibling = (mx, my, mc), (mx, my, 1 - mc)
        self.chips = [(1 - mx, my), (mx, 1 - my), (1 - mx, 1 - my)]

    def copy(self, a, k, block, to, src=None):
        px, py, pc = block
        slot = self.out_refs[a].at[4 * px + 2 * py + pc]
        return pltpu.make_async_remote_copy(
            src_ref=slot if src is None else src, dst_ref=slot, send_sem=self.send_sems.at[7 * a + k],
            recv_sem=self.recv_sems.at[7 * a + k], device_id=to, device_id_type=pl.DeviceIdType.MESH)

    def own(self):
        mx, my, mc = self.me
        mine = [pltpu.make_async_copy(self.x_refs[a], self.out_refs[a].at[4 * mx + 2 * my + mc], self.local_sems.at[a])
                for a in range(self.n)]
        first = []
        for a in range(self.n):
            first.append(self.copy(a, 0, self.me, self.sibling, src=self.x_refs[a]))
            first += [self.copy(a, 1 + j, self.me, (*chip, self.mc), src=self.x_refs[a]) for j, chip in enumerate(self.chips)]
        return mine, first

    def start(self):
        mine, first = self.own()
        for cp in mine + first:
            cp.start()

    def finish(self):
        mine, first = self.own()
        passed = []
        for j, chip in enumerate(self.chips):
            for a in range(self.n):
                self.copy(a, 1 + j, (*chip, self.mc), self.me).wait_recv()
                fwd = self.copy(a, 4 + j, (*chip, self.mc), self.sibling)
                fwd.start()
                passed.append(fwd)
        for a in range(self.n):
            self.copy(a, 0, self.sibling, self.me).wait_recv()
            for j, chip in enumerate(self.chips):
                self.copy(a, 4 + j, (*chip, 1 - self.mc), self.me).wait_recv()
        for cp in first + passed:
            cp.wait_send()
        for cp in mine:
            cp.wait()


def _all_gather(xs, name):
    n = len(xs)

    def body(*refs):
        g = _Gather(refs[:n], refs[n:2 * n], *refs[2 * n:])
        g.start()
        g.finish()

    anyspec = pl.BlockSpec(memory_space=pl.ANY)
    return pl.pallas_call(
        body, out_shape=tuple(jax.ShapeDtypeStruct((N_DEV,) + x.shape, x.dtype) for x in xs),
        in_specs=[anyspec] * n, out_specs=tuple([anyspec] * n), scratch_shapes=_exchange_scratch(n), name=name)(*xs)


def _exchange_copies(s_refs, r_refs, send_sems, recv_sems, local_sems):
    n = len(s_refs)
    mx, my, mc = _mesh_pos()
    me_id = 4 * mx + 2 * my + mc
    copies = [pltpu.make_async_copy(s_refs[a].at[me_id], r_refs[a].at[me_id], local_sems.at[a]) for a in range(n)]
    for k in range(1, N_DEV):
        px = 1 - mx if k & 4 else mx
        py = 1 - my if k & 2 else my
        pc = 1 - mc if k & 1 else mc
        for a in range(n):
            copies.append(pltpu.make_async_remote_copy(
                src_ref=s_refs[a].at[4 * px + 2 * py + pc], dst_ref=r_refs[a].at[me_id],
                send_sem=send_sems.at[7 * a + k - 1], recv_sem=recv_sems.at[7 * a + k - 1],
                device_id=(px, py, pc), device_id_type=pl.DeviceIdType.MESH))
    return copies


def _exchange_scratch(n):
    return [pltpu.SemaphoreType.DMA((7 * n,)), pltpu.SemaphoreType.DMA((7 * n,)), pltpu.SemaphoreType.DMA((n,))]


def _exchange(sends, name):
    n = len(sends)

    def body(*refs):
        copies = _exchange_copies(refs[:n], refs[n:2 * n], *refs[2 * n:])
        for cp in copies:
            cp.start()
        for cp in copies:
            cp.wait()

    anyspec = pl.BlockSpec(memory_space=pl.ANY)
    return pl.pallas_call(
        body, out_shape=tuple(jax.ShapeDtypeStruct(s.shape, s.dtype) for s in sends),
        in_specs=[anyspec] * n, out_specs=tuple([anyspec] * n), scratch_shapes=_exchange_scratch(n), name=name)(*sends)


ADAMW_BLOCK_BYTES = 4 * 1024 * 1024


def _adamw(parts, w, m, v, name):
    _, R, C = parts.shape
    tr = R
    for t in (1024, 512, 256, 128, 64, 32, 16, 8):
        if R % t == 0 and N_DEV * t * C * 4 <= ADAMW_BLOCK_BYTES:
            tr = t
            break

    def body(p_ref, w_ref, m_ref, v_ref, g_ref, d_ref, nm_ref, nv_ref):
        g = p_ref[0].astype(f32)
        for j in range(1, N_DEV):
            g = g + p_ref[j].astype(f32)
        mn = ADAM_B1 * m_ref[...] + (1.0 - ADAM_B1) * g
        vn = ADAM_B2 * v_ref[...] + (1.0 - ADAM_B2) * jnp.square(g)
        m_hat = mn / (1.0 - ADAM_B1 ** ADAM_STEP)
        v_hat = vn / (1.0 - ADAM_B2 ** ADAM_STEP)
        g_ref[...] = g
        d_ref[...] = -ADAM_LR * (m_hat / (jnp.sqrt(v_hat) + ADAM_EPS) + ADAM_WD * w_ref[...])
        nm_ref[...] = mn
        nv_ref[...] = vn

    t2 = pl.BlockSpec((tr, C), lambda i: (i, 0))
    out = jax.ShapeDtypeStruct((R, C), f32)
    return pl.pallas_call(
        body, grid=(R // tr,), out_shape=(out, out, out, out),
        in_specs=[pl.BlockSpec((N_DEV, tr, C), lambda i: (0, i, 0)), t2, t2, t2], out_specs=(t2, t2, t2, t2),
        compiler_params=_cp(("parallel",), VMEM_LIMIT), name=name)(parts, w, m, v)


def _as2d(a):
    return a.reshape(-1, a.shape[-1])


def _perm_cols(w, order=_ORDER):
    parts = [w[..., _COLS[n][0]:_COLS[n][1]] for n in order]
    pad = jnp.zeros(w.shape[:-1] + (N_ALL - N_IN,), w.dtype)
    return jnp.concatenate(parts + [pad], axis=-1)


def _unperm_cols(w, order=_ORDER):
    pieces, off = {}, 0
    for n in order:
        width = _COLS[n][1] - _COLS[n][0]
        pieces[n] = w[..., off:off + width]
        off += width
    return jnp.concatenate([pieces[n] for n in sorted(_COLS, key=lambda n: _COLS[n][0])], axis=-1)


_SMALL_ROWS = 16


def _pack_small(t):
    z = jnp.zeros((D,), f32)
    misc = z.at[0:16].set(t["b_fg"].reshape(-1)).at[16:24].set(t["a_log"].reshape(-1)).at[24:32].set(t["dt_bias"].reshape(-1))
    misc = misc.at[128:384].set(t["gdn_norm_g"].reshape(-1))
    if "extra" in t:
        misc = misc.at[512].set(t["extra"])
    rows = [t["norm_g"], t["b_merge"].reshape(6, D), t["mem_norm_g"], t["final_norm_g"][None], misc[None],
            jnp.zeros((_SMALL_ROWS - 12, D), f32)]
    return jnp.concatenate(rows, axis=0)


def _unpack_small(a):
    misc = a[11]
    return dict(norm_g=a[0:2], b_merge=a[2:8].reshape(2, 3072), mem_norm_g=a[8:10], final_norm_g=a[10],
                b_fg=misc[0:16].reshape(2, 8), a_log=misc[16:24].reshape(2, 4), dt_bias=misc[24:32].reshape(2, 4),
                gdn_norm_g=misc[128:384].reshape(2, 128), extra=misc[512])


def _layer_fwd(l, x, mem, p, gather_next=None):
    sfx = f"_l{l}"
    h, ht = _norm_fwd(x, p["norm_g"], "norm_fwd" + sfx, with_t=True)
    zb = _mm(h, p["w_b"], bf16, 1024, 1024, 1024, "inproj_b" + sfx)
    if gather_next is None:
        zf, gathered = _mm(h, p["w_f"], f32, 1024, 1024, 1024, "inproj_f" + sfx), None
    else:
        zf, gathered = _mm(h, p["w_f"], f32, 1024, 1024, 1024, "inproj_f" + sfx, comm=("gather", gather_next))
    zs = _mm(h, p["w_s"], f32, 512, 128, 1024, "inproj_s" + sfx)
    sm = _small_prep(zs, p["par"], "small_prep" + sfx)
    S = x.shape[0]
    gt4 = jnp.transpose(sm[:, 8:12]).reshape(4, S // CH, 1, CH)
    qa, ka, st = _fox_prep(zb, sm, "fox_prep" + sfx)
    tab = _fox_bound_table(st, S, min(S, FOX_TILE))
    oa, lse_t = _fox_fwd(qa, ka, zb, tab, "fox_fwd" + sfx)
    gqkv = _gdn_prep(zf, p["conv_w"], "gdn_prep" + sfx)
    ob, tinv, states = _gdn_fwd(gqkv, sm, gt4, "gdn_fwd" + sfx)
    memn = _norm_fwd(mem, p["mem_norm_g"], "mem_norm" + sfx)
    kv = _mm(memn, p["w_mem_kv"], bf16, 256, 1024, 1024, "mem_kv" + sfx)
    xo, ycat, merged = _merge_fwd(x, oa, ob, zb, zf, kv, p["b_merge"], p["gdn_norm_g"], p["w_branch"], p["w_out"], "merge_fwd" + sfx)
    saved = dict(x=x, ht=ht, zb=zb, zf=zf, zs=zs, sm=sm, qa=qa, ka=ka, st=st, gt4=gt4, oa=oa, lse_t=lse_t, gqkv=gqkv, ob=ob, tinv=tinv,
                 states=states, memn=memn, kv=kv, ycat=ycat, merged=merged)
    return xo, saved, gathered


def _layer_bwd(l, dout, mem, p, s, comm_sends=(), send_fn=None):
    sfx = f"_l{l}"
    dproj, dzf2, dmq, delta, doab, dob, dkv, dbm, dgg = _merge_bwd(
        dout, s["ycat"], s["oa"], s["ob"], s["zb"], s["zf"], s["kv"], p["b_merge"], p["gdn_norm_g"],
        p["w_branch"], p["w_branch_t"], p["w_out_t"], "merge_bwd" + sfx)
    g = {}
    g["w_out"] = _mm(s["merged"], dout, f32, 512, 1024, 512, "dw_out" + sfx, trans_a=True)
    g["w_branch"] = jnp.stack([
        _mm(s["ycat"], dproj, f32, 512, 1024, 512, f"dw_branch{n}" + sfx, trans_a=True, a_cols=(n * 512, 512), b_cols=(n * D, D))
        for n in range(3)])
    g["b_merge"] = dbm[0]
    g["gdn_norm_g"] = dgg[0]
    g["w_mem_kv"] = _mm(s["memn"], dkv, f32, 512, 1024, 256, "dw_mem_kv" + sfx, trans_a=True)
    dmemn = _mm(dkv, p["w_mem_kv_t"], f32, 256, 1024, 1024, "dmem_n" + sfx)
    g["mem_norm_g"] = _norm_bwd(mem, p["mem_norm_g"], dmemn, None, "mem_norm_bwd" + sfx)[0]
    dgqkv, dsm = _gdn_bwd(s["gqkv"], s["sm"], s["gt4"], s["tinv"], s["states"], dob, "gdn_bwd" + sfx)
    dbqkv, dcw = _gdn_prep_bwd(s["zf"], p["conv_w"], dgqkv, "gdn_prep_bwd" + sfx)
    g["conv_w"] = dcw[0:4]
    S = dout.shape[0]
    lse_rows = _head_rows(s["lse_t"])
    tab = _fox_bound_table(s["st"], S, min(S, FOX_TILE), lse_rows=lse_rows)
    (dq, dk, dv, dfc, dfr), received = _fox_bwd(s["qa"], s["ka"], s["zb"], doab, lse_rows, delta, tab,
                                                "fox_bwd" + sfx, comm_sends=comm_sends)
    dzs, sacc = _small_bwd(s["zs"], p["par"], _head_rows(dfr), dfc, dsm, "small_bwd" + sfx)
    g["b_fg"], g["a_log"], g["dt_bias"] = sacc[0, 0:8], sacc[1, 8:12], sacc[2, 8:12]
    dz = [(dzf2, DZ_MERGE_COLS, PIECE_COLS), (dbqkv, PIECE_COLS, PIECE_COLS), ([dq, dk, dv], PIECE_COLS, PIECE_COLS),
          ([dmq, dzs], PIECE_COLS, PIECE_COLS)]
    g["w_in"] = _mm_n_pieces(s["ht"], dz, f32, 1024, 1024, "dw_in" + sfx)
    if send_fn is None:
        dh, received_late = _mm_k_pieces(dz, p["w_all_t"], f32, 1024, "dh" + sfx), None
    else:
        dh, received_late = _mm_k_pieces(dz, p["w_all_t"], f32, 1024, "dh" + sfx, comm=("exchange", send_fn(g)))
    dx, dng = _norm_bwd(s["x"], p["norm_g"], dh, dout, "norm_bwd" + sfx)
    g["norm_g"] = dng[0]
    return dx, g, received, received_late


def kernel(x, mem, norm_g, w_in, b_fg, b_merge, conv_w, a_log, dt_bias, gdn_norm_g, mem_norm_g, w_mem_kv, w_branch, w_out, final_norm_g, loss_target, m_norm_g, m_w_in, m_b_fg, m_b_merge, m_conv_w, m_a_log, m_dt_bias, m_gdn_norm_g, m_mem_norm_g, m_w_mem_kv, m_w_branch, m_w_out, m_final_norm_g, v_norm_g, v_w_in, v_b_fg, v_b_merge, v_conv_w, v_a_log, v_dt_bias, v_gdn_norm_g, v_mem_norm_g, v_w_mem_kv, v_w_branch, v_w_out, v_final_norm_g):
    x0, mem0, tgt = x[0], mem[0], loss_target[0]
    shard_w = dict(w_in=w_in, w_mem_kv=w_mem_kv, w_branch=w_branch, w_out=w_out, conv_w=conv_w)
    shard_m = dict(w_in=m_w_in, w_mem_kv=m_w_mem_kv, w_branch=m_w_branch, w_out=m_w_out, conv_w=m_conv_w)
    shard_v = dict(w_in=v_w_in, w_mem_kv=v_w_mem_kv, w_branch=v_w_branch, w_out=v_w_out, conv_w=v_conv_w)
    small_w = dict(norm_g=norm_g, b_fg=b_fg, b_merge=b_merge, a_log=a_log, dt_bias=dt_bias, gdn_norm_g=gdn_norm_g,
                   mem_norm_g=mem_norm_g, final_norm_g=final_norm_g)
    small_m = dict(norm_g=m_norm_g, b_fg=m_b_fg, b_merge=m_b_merge, a_log=m_a_log, dt_bias=m_dt_bias, gdn_norm_g=m_gdn_norm_g,
                   mem_norm_g=m_mem_norm_g, final_norm_g=m_final_norm_g)
    small_v = dict(norm_g=v_norm_g, b_fg=v_b_fg, b_merge=v_b_merge, a_log=v_a_log, dt_bias=v_dt_bias, gdn_norm_g=v_gdn_norm_g,
                   mem_norm_g=v_mem_norm_g, final_norm_g=v_final_norm_g)

    def shards(l):
        return [_b(w_in[l]), _b(w_mem_kv[l]), _b(_as2d(w_branch[l])), _b(w_out[l])]

    def layer_params(l, g_in, g_kv, g_br, g_out, conv_full):
        w_full = jnp.transpose(g_in, (1, 0, 2)).reshape(D, N_IN)
        w_all = _perm_cols(w_full)
        w_kv = g_kv.reshape(D, D)
        w_br = jnp.transpose(g_br.reshape(N_DEV, 3, 512, 128), (1, 2, 0, 3)).reshape(3, 512, D)
        w_o = g_out.reshape(D, D)
        return dict(
            norm_g=norm_g[l][None], mem_norm_g=mem_norm_g[l][None], gdn_norm_g=gdn_norm_g[l][None], b_merge=b_merge[l][None],
            par=_small_pars(b_fg[l], a_log[l], dt_bias[l]),
            conv_w=jnp.pad(conv_full[l], ((0, 4), (0, 0))),
            w_b=w_all[:, 0:NB], w_f=w_all[:, NB:NB + NF], w_s=w_all[:, NB + NF:NB + NF + NS],
            w_all_t=jnp.transpose(_perm_cols(w_full, _ORDER_BWD)),
            w_mem_kv=w_kv, w_mem_kv_t=jnp.transpose(w_kv),
            w_branch=w_br, w_branch_t=jnp.transpose(w_br, (0, 2, 1)),
            w_out=w_o, w_out_t=jnp.transpose(w_o))

    *gathered0, conv_all = _all_gather(shards(0) + [_as2d(conv_w)], "gather_weights")
    conv_full = jnp.transpose(conv_all.reshape(N_DEV, DEPTH, 4, 192), (1, 2, 0, 3)).reshape(DEPTH, 4, 1536)
    layers = [layer_params(0, *gathered0, conv_full), None]

    saved = [None] * DEPTH
    acts, saved[0], gathered1 = _layer_fwd(0, x0, mem0, layers[0], gather_next=shards(1))
    layers[1] = layer_params(1, *gathered1, conv_full)
    acts, saved[1], _ = _layer_fwd(1, acts, mem0, layers[1])
    dx, dfg, lsum = _loss_head(acts, final_norm_g[None], tgt, "loss_head")

    def send_buffers(g):
        dw_in = _unperm_cols(g["w_in"], _ORDER_BWD)
        send = dict(
            w_in=jnp.transpose(dw_in.reshape(D, N_DEV, 1026), (1, 0, 2)),
            w_mem_kv=g["w_mem_kv"].reshape(N_DEV, 128, D),
            w_branch=jnp.transpose(g["w_branch"].reshape(3, 512, N_DEV, 128), (2, 0, 1, 3)).reshape(N_DEV, 3 * 512, 128),
            w_out=g["w_out"].reshape(N_DEV, 128, D),
            conv_w=jnp.transpose(g["conv_w"].reshape(4, N_DEV, 192), (1, 0, 2)))
        return [_b(send[n]) for n in _SHARDED]

    grads, parts = [None] * DEPTH, [None] * DEPTH
    dx, grads[1], _, _ = _layer_bwd(1, dx, mem0, layers[1], saved[1])
    dx, grads[0], parts[1], parts[0] = _layer_bwd(0, dx, mem0, layers[0], saved[0], comm_sends=send_buffers(grads[1]),
                                                  send_fn=send_buffers)
    grad_x = dx[None]

    big = [{}, {}, {}, {}]
    for a, n in enumerate(_SHARDED):
        res = [_adamw(parts[l][a], _as2d(shard_w[n][l]), _as2d(shard_m[n][l]), _as2d(shard_v[n][l]), f"adamw_{n}_l{l}")
               for l in range(DEPTH)]
        for kind in range(4):
            big[kind][n] = jnp.stack([res[l][kind] for l in range(DEPTH)]).reshape(shard_w[n].shape)

    small_g = {k: jnp.stack([grads[l][k] for l in range(DEPTH)]) for k in ("norm_g", "b_fg", "b_merge", "a_log", "dt_bias", "gdn_norm_g", "mem_norm_g")}
    small_g["final_norm_g"] = dfg[0]
    small_g["extra"] = lsum[0, 0]
    parts_s, = _all_gather([_pack_small(small_g)], "gather_small")
    g_sm, d_sm, m_sm, v_sm = _adamw(parts_s, _pack_small(small_w), _pack_small(small_m), _pack_small(small_v), "adamw_replicated")

    sml = [_unpack_small(a) for a in (g_sm, d_sm, m_sm, v_sm)]
    loss = sml[0]["extra"]
    names = ("norm_g", "w_in", "b_fg", "b_merge", "conv_w", "a_log", "dt_bias", "gdn_norm_g", "mem_norm_g", "w_mem_kv", "w_branch", "w_out", "final_norm_g")
    outs = [loss, grad_x]
    for kind in range(4):
        for n in names:
            outs.append(big[kind][n] if n in big[kind] else sml[kind][n])
    return tuple(outs)
```
